```python
import math
import jax, jax.numpy as jnp
from jax import lax
import numpy as np

D_MODEL = 1024
BATCH = 8
SEQ = 8192
DEPTH = 2

N_MIXERS = 2
N_ATTN_LAYERS = (DEPTH + 1) // 2
N_DELTA_LAYERS = DEPTH // 2
PLE_DIM = 256
EPS = 1e-6

SWA_GROUPS = ((128, 1), (512, 4), (2048, 16))
N_GROUPS = len(SWA_GROUPS)
A_HEADS = 8
A_HEAD_DIM = 64
A_OUT_WIDTH = A_HEADS * A_HEAD_DIM
A_QKV_WIDTH = N_GROUPS * 3 * A_OUT_WIDTH
ROPE_DIM = A_HEAD_DIM // 4
ROPE_THETA = 500000.0
BAND_BLOCK = 128

DN_HEADS = 8
DN_HEAD_DIM = 128
DN_WIDTH = DN_HEADS * DN_HEAD_DIM
DN_IN_WIDTH = 3 * DN_WIDTH + 2 * DN_HEADS + DN_WIDTH
CONV_WIDTH = 4
CHUNK = 64

D_FF = 4 * D_MODEL

kernel_name = 'hybrid_dilated_swa_gated_deltanet'


def rmsnorm(x, gain):
    xf = x.astype(jnp.float32)
    y = xf * lax.rsqrt(jnp.mean(xf * xf, axis=-1, keepdims=True) + EPS)
    return (y * gain.astype(jnp.float32)).astype(x.dtype)


def l2norm(x):
    xf = x.astype(jnp.float32)
    return xf * lax.rsqrt(jnp.sum(xf * xf, axis=-1, keepdims=True) + EPS)


def rope_tables(positions):
    inv_freq = ROPE_THETA ** (-jnp.arange(0, ROPE_DIM, 2, dtype=jnp.float32) / ROPE_DIM)
    ang = positions.astype(jnp.float32)[..., None] * inv_freq
    return jnp.cos(ang)[:, :, None, :], jnp.sin(ang)[:, :, None, :]


def apply_partial_rope(x, cos, sin):
    half = ROPE_DIM // 2
    xf = x.astype(jnp.float32)
    x1, x2 = xf[..., :half], xf[..., half:ROPE_DIM]
    out = jnp.concatenate([x1 * cos - x2 * sin, x2 * cos + x1 * sin, xf[..., ROPE_DIM:]], axis=-1)
    return out.astype(x.dtype)


def dilated_band_attention(q, k, v, window, dilation):
    B, S, H, E = q.shape
    L = S // dilation
    span = window // dilation
    n_prev = -(-span // BAND_BLOCK)
    nblk = -(-L // BAND_BLOCK)
    Lp = nblk * BAND_BLOCK

    def by_residue(t):
        t = t.reshape(B, L, dilation, H, E).transpose(0, 2, 1, 3, 4)
        t = jnp.pad(t, ((0, 0), (0, 0), (0, Lp - L), (0, 0), (0, 0)))
        return t.reshape(B, dilation, nblk, BAND_BLOCK, H, E)

    def band(t):
        tp = jnp.pad(t, ((0, 0), (0, 0), (n_prev, 0), (0, 0), (0, 0), (0, 0)))
        return jnp.concatenate([tp[:, :, s:s + nblk] for s in range(n_prev + 1)], axis=3)

    qb = by_residue(q)
    kw = band(by_residue(k))
    vw = band(by_residue(v))
    qi = jnp.arange(BAND_BLOCK)[:, None]
    kj = jnp.arange((n_prev + 1) * BAND_BLOCK)[None, :]
    dist = qi + n_prev * BAND_BLOCK - kj
    k_abs = (jnp.arange(nblk)[:, None, None] - n_prev) * BAND_BLOCK + kj[None]
    valid = (dist >= 0)[None] & (dist <= span)[None] & (k_abs >= 0)

    s = jnp.einsum('bdnqhe,bdnkhe->bdnhqk', qb, kw).astype(jnp.float32) * (E ** -0.5)
    s = jnp.where(valid[None, None, :, None], s, -jnp.inf)
    m = jnp.max(s, axis=-1, keepdims=True)
    e = jnp.exp(s - m)
    l = jnp.sum(e, axis=-1, keepdims=True)
    o = jnp.einsum('bdnhqk,bdnkhe->bdnqhe', (e / l).astype(v.dtype), vw)
    lse = (m + jnp.log(l))[..., 0]
    o = o.reshape(B, dilation, Lp, H, E)[:, :, :L].transpose(0, 2, 1, 3, 4).reshape(B, S, H, E)
    lse = lse.transpose(0, 1, 2, 4, 3).reshape(B, dilation, Lp, H)[:, :, :L]
    lse = lse.transpose(0, 2, 1, 3).reshape(B, S, H)
    return o, lse


def dilated_attention_mixer(h, cos, sin, w_qkv, q_gain, k_gain, w_o):
    B, S, _ = h.shape
    qkv = (h @ w_qkv).reshape(B, S, N_GROUPS, 3, A_HEADS, A_HEAD_DIM)
    outs, lses = [], []
    for g, (window, dilation) in enumerate(SWA_GROUPS):
        q = apply_partial_rope(rmsnorm(qkv[:, :, g, 0], q_gain[g]), cos, sin)
        k = apply_partial_rope(rmsnorm(qkv[:, :, g, 1], k_gain[g]), cos, sin)
        o, lse = dilated_band_attention(q, k, qkv[:, :, g, 2], window, dilation)
        outs.append(o)
        lses.append(lse)
    wts = jax.nn.softmax(jnp.stack(lses, axis=0), axis=0)
    o = jnp.einsum('gbsh,gbshe->bshe', wts.astype(h.dtype), jnp.stack(outs, axis=0))
    return o.reshape(B, S, A_OUT_WIDTH) @ w_o


def causal_depthwise_conv(x, w):
    C = x.shape[-1]
    return lax.conv_general_dilated(
        x, w[:, None, :].astype(x.dtype), window_strides=(1,),
        padding=[(CONV_WIDTH - 1, 0)], dimension_numbers=('NWC', 'WIO', 'NWC'),
        feature_group_count=C)


def to_chunks(t):
    B, S = t.shape[:2]
    t = t.reshape(B, S // CHUNK, CHUNK, *t.shape[2:])
    return jnp.moveaxis(t, 2, 3)


def chunk_gated_delta_rule(q, k, v, g, beta):
    B, S, H, K = q.shape
    V = v.shape[-1]
    qc, kc, vc = to_chunks(q), to_chunks(k), to_chunks(v)
    gc = jnp.cumsum(to_chunks(g), axis=-1)
    bc = to_chunks(beta)[..., None]
    lower = jnp.tril(jnp.ones((CHUNK, CHUNK), dtype=bool))
    strict = jnp.tril(jnp.ones((CHUNK, CHUNK), dtype=bool), -1)
    decay = jnp.exp(jnp.where(lower, gc[..., :, None] - gc[..., None, :], -jnp.inf))
    kb = kc * bc
    a_mat = jnp.where(strict, jnp.einsum('bnhik,bnhjk->bnhij', kb, kc) * decay, 0.0)
    rhs = jnp.concatenate([vc * bc, kb * jnp.exp(gc)[..., None]], axis=-1)
    sol = lax.linalg.triangular_solve(a_mat + jnp.eye(CHUNK, dtype=a_mat.dtype), rhs,
                                      left_side=True, lower=True, unit_diagonal=True)
    u, w = sol[..., :V], sol[..., V:]
    attn = jnp.einsum('bnhik,bnhjk->bnhij', qc, kc) * decay
    q_dec = qc * jnp.exp(gc)[..., None]
    k_dec = kc * jnp.exp(gc[..., -1:] - gc)[..., None]
    c_dec = jnp.exp(gc[..., -1])

    def step(state, inp):
        u_i, w_i, qd_i, kd_i, at_i, cd_i = inp
        v_new = u_i - jnp.einsum('bhck,bhkv->bhcv', w_i, state)
        o_i = jnp.einsum('bhck,bhkv->bhcv', qd_i, state) + jnp.einsum('bhij,bhjv->bhiv', at_i, v_new)
        state = state * cd_i[..., None, None] + jnp.einsum('bhck,bhcv->bhkv', kd_i, v_new)
        return state, o_i

    xs = tuple(jnp.moveaxis(t, 1, 0) for t in (u, w, q_dec, k_dec, attn, c_dec))
    state0 = jnp.zeros((B, H, K, V), dtype=jnp.float32)
    _, o = lax.scan(step, state0, xs)
    return jnp.moveaxis(o, 0, 1).swapaxes(2, 3).reshape(B, S, H, V)


def gated_deltanet_mixer(h, w_in, conv_w, a_log, dt_bias, o_gain, w_o):
    B, S, _ = h.shape
    proj = h @ w_in
    c = 3 * DN_WIDTH
    qkv = proj[..., :c]
    a = proj[..., c:c + DN_HEADS]
    b = proj[..., c + DN_HEADS:c + 2 * DN_HEADS]
    z = proj[..., c + 2 * DN_HEADS:]
    qkv = jax.nn.silu(causal_depthwise_conv(qkv, conv_w))
    q, k, v = jnp.split(qkv, 3, axis=-1)
    q = l2norm(q.reshape(B, S, DN_HEADS, DN_HEAD_DIM)) * (DN_HEAD_DIM ** -0.5)
    k = l2norm(k.reshape(B, S, DN_HEADS, DN_HEAD_DIM))
    v = v.reshape(B, S, DN_HEADS, DN_HEAD_DIM).astype(jnp.float32)
    beta = jax.nn.sigmoid(b.astype(jnp.float32))
    g = -jnp.exp(a_log.astype(jnp.float32)) * jax.nn.softplus(a.astype(jnp.float32) + dt_bias.astype(jnp.float32))
    o = chunk_gated_delta_rule(q, k, v, g, beta)
    o = rmsnorm(o, o_gain) * jax.nn.silu(z.reshape(B, S, DN_HEADS, DN_HEAD_DIM).astype(jnp.float32))
    return o.reshape(B, S, DN_WIDTH).astype(h.dtype) @ w_o


def _fwd_setup_inputs(seed: int = 0) -> dict:
    key = jax.random.key(seed)
    ks = jax.random.split(key, 20)
    NA, NB = N_ATTN_LAYERS, N_DELTA_LAYERS

    def nrm(k, shape, fan_in):
        return jax.random.normal(k, shape, jnp.float32) * (fan_in ** -0.5)

    def gain(k, shape):
        return 1.0 + 0.02 * jax.random.normal(k, shape, jnp.float32)

    dt = jnp.exp(jax.random.uniform(ks[9], (NB, DN_HEADS), jnp.float32,
                                    minval=math.log(1e-3), maxval=math.log(1e-1)))
    return {
        'x': jax.random.normal(ks[0], (BATCH, SEQ, D_MODEL), jnp.float32),
        'p': jax.random.normal(ks[1], (DEPTH, BATCH, SEQ, PLE_DIM), jnp.float32),
        'positions': jnp.broadcast_to(jnp.arange(SEQ, dtype=jnp.int32), (BATCH, SEQ)),
        'mix_norm': gain(ks[2], (DEPTH, D_MODEL)),
        'attn_w_qkv': nrm(ks[3], (NA, D_MODEL, A_QKV_WIDTH), D_MODEL),
        'attn_q_gain': gain(ks[4], (NA, N_GROUPS, A_HEAD_DIM)),
        'attn_k_gain': gain(ks[5], (NA, N_GROUPS, A_HEAD_DIM)),
        'attn_w_o': nrm(ks[6], (NA, A_OUT_WIDTH, D_MODEL), A_OUT_WIDTH),
        'dn_w_in': nrm(ks[7], (NB, D_MODEL, DN_IN_WIDTH), D_MODEL),
        'dn_conv': nrm(ks[8], (NB, CONV_WIDTH, 3 * DN_WIDTH), CONV_WIDTH),
        'dn_a_log': jnp.log(jax.random.uniform(ks[10], (NB, DN_HEADS), jnp.float32, minval=1.0, maxval=16.0)),
        'dn_dt_bias': dt + jnp.log(-jnp.expm1(-dt)),
        'dn_o_gain': gain(ks[11], (NB, DN_HEAD_DIM)),
        'dn_w_o': nrm(ks[12], (NB, DN_WIDTH, D_MODEL), DN_WIDTH),
        'mlp_norm': gain(ks[13], (DEPTH, D_MODEL)),
        'w_up': nrm(ks[14], (DEPTH, D_MODEL, D_FF), D_MODEL),
        'w_down': nrm(ks[15], (DEPTH, D_FF, D_MODEL), D_FF),
        'ple_norm': gain(ks[16], (DEPTH, D_MODEL)),
        'w_ple': nrm(ks[17], (DEPTH, PLE_DIM, D_MODEL), PLE_DIM),
        'w_ple_gate': nrm(ks[18], (DEPTH, D_MODEL, D_MODEL), D_MODEL),
    }


def _fwd_reference(x, p, positions, mix_norm, attn_w_qkv, attn_q_gain, attn_k_gain, attn_w_o,
              dn_w_in, dn_conv, dn_a_log, dn_dt_bias, dn_o_gain, dn_w_o,
              mlp_norm, w_up, w_down, ple_norm, w_ple, w_ple_gate):
    cos, sin = rope_tables(positions)
    for i in range(DEPTH):
        j = i // N_MIXERS
        hn = rmsnorm(x, mix_norm[i])
        if i % N_MIXERS == 0:
            mix = dilated_attention_mixer(hn, cos, sin, attn_w_qkv[j], attn_q_gain[j],
                                          attn_k_gain[j], attn_w_o[j])
        else:
            mix = gated_deltanet_mixer(hn, dn_w_in[j], dn_conv[j], dn_a_log[j], dn_dt_bias[j],
                                       dn_o_gain[j], dn_w_o[j])
        x = x + mix
        hn = rmsnorm(x, mlp_norm[i])
        x = x + jnp.square(jax.nn.relu(hn @ w_up[i])) @ w_down[i]
        gate = jax.nn.sigmoid((rmsnorm(x, ple_norm[i]) @ w_ple_gate[i]).astype(jnp.float32)).astype(x.dtype)
        x = x + (p[i] @ w_ple[i]) * gate
    return x


import jax as _jax
import jax.numpy as _jnp

TWIN_FORMAT = 'train_step'
FWD_PARAMS = ['x', 'p', 'positions', 'mix_norm', 'attn_w_qkv', 'attn_q_gain', 'attn_k_gain', 'attn_w_o', 'dn_w_in', 'dn_conv', 'dn_a_log', 'dn_dt_bias', 'dn_o_gain', 'dn_w_o', 'mlp_norm', 'w_up', 'w_down', 'ple_norm', 'w_ple', 'w_ple_gate']
TWIN_WEIGHTS = ['mix_norm', 'attn_w_qkv', 'attn_q_gain', 'attn_k_gain', 'attn_w_o', 'dn_w_in', 'dn_conv', 'dn_a_log', 'dn_dt_bias', 'dn_o_gain', 'dn_w_o', 'mlp_norm', 'w_up', 'w_down', 'ple_norm', 'w_ple', 'w_ple_gate']
TWIN_DIFF_INPUT = 'x'
TWIN_INPUTS = ['x', 'p', 'positions', 'mix_norm', 'attn_w_qkv', 'attn_q_gain', 'attn_k_gain', 'attn_w_o', 'dn_w_in', 'dn_conv', 'dn_a_log', 'dn_dt_bias', 'dn_o_gain', 'dn_w_o', 'mlp_norm', 'w_up', 'w_down', 'ple_norm', 'w_ple', 'w_ple_gate', 'loss_target', 'm_mix_norm', 'm_attn_w_qkv', 'm_attn_q_gain', 'm_attn_k_gain', 'm_attn_w_o', 'm_dn_w_in', 'm_dn_conv', 'm_dn_a_log', 'm_dn_dt_bias', 'm_dn_o_gain', 'm_dn_w_o', 'm_mlp_norm', 'm_w_up', 'm_w_down', 'm_ple_norm', 'm_w_ple', 'm_w_ple_gate', 'v_mix_norm', 'v_attn_w_qkv', 'v_attn_q_gain', 'v_attn_k_gain', 'v_attn_w_o', 'v_dn_w_in', 'v_dn_conv', 'v_dn_a_log', 'v_dn_dt_bias', 'v_dn_o_gain', 'v_dn_w_o', 'v_mlp_norm', 'v_w_up', 'v_w_down', 'v_ple_norm', 'v_w_ple', 'v_w_ple_gate']
TWIN_OUTPUTS = ['loss', 'grad_x', 'grad_mix_norm', 'grad_attn_w_qkv', 'grad_attn_q_gain', 'grad_attn_k_gain', 'grad_attn_w_o', 'grad_dn_w_in', 'grad_dn_conv', 'grad_dn_a_log', 'grad_dn_dt_bias', 'grad_dn_o_gain', 'grad_dn_w_o', 'grad_mlp_norm', 'grad_w_up', 'grad_w_down', 'grad_ple_norm', 'grad_w_ple', 'grad_w_ple_gate', 'delta_mix_norm', 'delta_attn_w_qkv', 'delta_attn_q_gain', 'delta_attn_k_gain', 'delta_attn_w_o', 'delta_dn_w_in', 'delta_dn_conv', 'delta_dn_a_log', 'delta_dn_dt_bias', 'delta_dn_o_gain', 'delta_dn_w_o', 'delta_mlp_norm', 'delta_w_up', 'delta_w_down', 'delta_ple_norm', 'delta_w_ple', 'delta_w_ple_gate', 'new_m_mix_norm', 'new_m_attn_w_qkv', 'new_m_attn_q_gain', 'new_m_attn_k_gain', 'new_m_attn_w_o', 'new_m_dn_w_in', 'new_m_dn_conv', 'new_m_dn_a_log', 'new_m_dn_dt_bias', 'new_m_dn_o_gain', 'new_m_dn_w_o', 'new_m_mlp_norm', 'new_m_w_up', 'new_m_w_down', 'new_m_ple_norm', 'new_m_w_ple', 'new_m_w_ple_gate', 'new_v_mix_norm', 'new_v_attn_w_qkv', 'new_v_attn_q_gain', 'new_v_attn_k_gain', 'new_v_attn_w_o', 'new_v_dn_w_in', 'new_v_dn_conv', 'new_v_dn_a_log', 'new_v_dn_dt_bias', 'new_v_dn_o_gain', 'new_v_dn_w_o', 'new_v_mlp_norm', 'new_v_w_up', 'new_v_w_down', 'new_v_ple_norm', 'new_v_w_ple', 'new_v_w_ple_gate']
TWIN_LEAF_KINDS = {'loss': 'loss', 'grad_x': 'grad_x', 'grad_mix_norm': 'grad_w', 'grad_attn_w_qkv': 'grad_w', 'grad_attn_q_gain': 'grad_w', 'grad_attn_k_gain': 'grad_w', 'grad_attn_w_o': 'grad_w', 'grad_dn_w_in': 'grad_w', 'grad_dn_conv': 'grad_w', 'grad_dn_a_log': 'grad_w', 'grad_dn_dt_bias': 'grad_w', 'grad_dn_o_gain': 'grad_w', 'grad_dn_w_o': 'grad_w', 'grad_mlp_norm': 'grad_w', 'grad_w_up': 'grad_w', 'grad_w_down': 'grad_w', 'grad_ple_norm': 'grad_w', 'grad_w_ple': 'grad_w', 'grad_w_ple_gate': 'grad_w', 'delta_mix_norm': 'delta_w', 'delta_attn_w_qkv': 'delta_w', 'delta_attn_q_gain': 'delta_w', 'delta_attn_k_gain': 'delta_w', 'delta_attn_w_o': 'delta_w', 'delta_dn_w_in': 'delta_w', 'delta_dn_conv': 'delta_w', 'delta_dn_a_log': 'delta_w', 'delta_dn_dt_bias': 'delta_w', 'delta_dn_o_gain': 'delta_w', 'delta_dn_w_o': 'delta_w', 'delta_mlp_norm': 'delta_w', 'delta_w_up': 'delta_w', 'delta_w_down': 'delta_w', 'delta_ple_norm': 'delta_w', 'delta_w_ple': 'delta_w', 'delta_w_ple_gate': 'delta_w', 'new_m_mix_norm': 'new_m', 'new_m_attn_w_qkv': 'new_m', 'new_m_attn_q_gain': 'new_m', 'new_m_attn_k_gain': 'new_m', 'new_m_attn_w_o': 'new_m', 'new_m_dn_w_in': 'new_m', 'new_m_dn_conv': 'new_m', 'new_m_dn_a_log': 'new_m', 'new_m_dn_dt_bias': 'new_m', 'new_m_dn_o_gain': 'new_m', 'new_m_dn_w_o': 'new_m', 'new_m_mlp_norm': 'new_m', 'new_m_w_up': 'new_m', 'new_m_w_down': 'new_m', 'new_m_ple_norm': 'new_m', 'new_m_w_ple': 'new_m', 'new_m_w_ple_gate': 'new_m', 'new_v_mix_norm': 'new_v', 'new_v_attn_w_qkv': 'new_v', 'new_v_attn_q_gain': 'new_v', 'new_v_attn_k_gain': 'new_v', 'new_v_attn_w_o': 'new_v', 'new_v_dn_w_in': 'new_v', 'new_v_dn_conv': 'new_v', 'new_v_dn_a_log': 'new_v', 'new_v_dn_dt_bias': 'new_v', 'new_v_dn_o_gain': 'new_v', 'new_v_dn_w_o': 'new_v', 'new_v_mlp_norm': 'new_v', 'new_v_w_up': 'new_v', 'new_v_w_down': 'new_v', 'new_v_ple_norm': 'new_v', 'new_v_w_ple': 'new_v', 'new_v_w_ple_gate': 'new_v'}


def _forward(args):
    return _fwd_reference(*[args[k] for k in FWD_PARAMS])


def _output_shape():
    def fwd():
        inp = _fwd_setup_inputs(0)
        return _fwd_reference(*[inp[k] for k in FWD_PARAMS])
    out = _jax.eval_shape(fwd)
    return out.shape, out.dtype

N_MICROBATCH = 1
ADAM_LR = 0.001
ADAM_B1 = 0.9
ADAM_B2 = 0.999
ADAM_EPS = 1e-08
ADAM_WD = 0.01
ADAM_STEP = 10
PER_EXAMPLE_BATCH_AXIS = {'x': 0, 'p': 1, 'positions': 0, 'loss_target': 0}
SHARED_INPUTS = []
_WEIGHT_DTYPES = {'mix_norm': _jnp.float32, 'attn_w_qkv': _jnp.float32, 'attn_q_gain': _jnp.float32, 'attn_k_gain': _jnp.float32, 'attn_w_o': _jnp.float32, 'dn_w_in': _jnp.float32, 'dn_conv': _jnp.float32, 'dn_a_log': _jnp.float32, 'dn_dt_bias': _jnp.float32, 'dn_o_gain': _jnp.float32, 'dn_w_o': _jnp.float32, 'mlp_norm': _jnp.float32, 'w_up': _jnp.float32, 'w_down': _jnp.float32, 'ple_norm': _jnp.float32, 'w_ple': _jnp.float32, 'w_ple_gate': _jnp.float32}
MOMENT_SCALE = {'mix_norm': 2.418279e+01, 'attn_w_qkv': 3.517085e-01, 'attn_q_gain': 1.758551e+00, 'attn_k_gain': 1.747237e+00, 'attn_w_o': 5.619683e-01, 'dn_w_in': 8.973695e+00, 'dn_conv': 7.387927e+00, 'dn_a_log': 7.028534e+01, 'dn_dt_bias': 6.645394e+01, 'dn_o_gain': 2.420591e+02, 'dn_w_o': 1.150776e+01, 'mlp_norm': 1.945709e+02, 'w_up': 7.503387e+00, 'w_down': 3.270816e+01, 'ple_norm': 2.982791e+00, 'w_ple': 1.017329e+00, 'w_ple_gate': 2.062356e+00}


def _to_microbatches(a, axis):
    t = _jnp.moveaxis(a, axis, 0)
    t = t.reshape((N_MICROBATCH, t.shape[0] // N_MICROBATCH) + t.shape[1:])
    return _jnp.moveaxis(t, 1, axis + 1)


def setup_inputs(seed: int = 0) -> dict:
    inp = _fwd_setup_inputs(seed)
    key = _jax.random.fold_in(_jax.random.key(seed), 7919)
    shape, _ = _output_shape()
    out = dict(inp)
    out["loss_target"] = _jax.random.normal(_jax.random.fold_in(key, 0), shape, _jnp.float32)
    for i, name in enumerate(TWIN_WEIGHTS):
        w = inp[name].astype(_jnp.float32)
        if MOMENT_SCALE is None:
            s = _jnp.sqrt(_jnp.mean(_jnp.square(w)) + 1e-30)
        else:
            s = MOMENT_SCALE[name]
        km, kv = _jax.random.split(_jax.random.fold_in(key, i + 1))
        out[name] = w
        out["m_" + name] = s * _jax.random.normal(km, w.shape, _jnp.float32)
        out["v_" + name] = (s * s) * _jax.random.uniform(kv, w.shape, _jnp.float32, 0.5, 1.5)
    if N_MICROBATCH > 1:
        for name, axis in PER_EXAMPLE_BATCH_AXIS.items():
            out[name] = _to_microbatches(out[name], axis)
    return {'x': out['x'], 'p': out['p'], 'positions': out['positions'], 'mix_norm': out['mix_norm'], 'attn_w_qkv': out['attn_w_qkv'], 'attn_q_gain': out['attn_q_gain'], 'attn_k_gain': out['attn_k_gain'], 'attn_w_o': out['attn_w_o'], 'dn_w_in': out['dn_w_in'], 'dn_conv': out['dn_conv'], 'dn_a_log': out['dn_a_log'], 'dn_dt_bias': out['dn_dt_bias'], 'dn_o_gain': out['dn_o_gain'], 'dn_w_o': out['dn_w_o'], 'mlp_norm': out['mlp_norm'], 'w_up': out['w_up'], 'w_down': out['w_down'], 'ple_norm': out['ple_norm'], 'w_ple': out['w_ple'], 'w_ple_gate': out['w_ple_gate'], 'loss_target': out['loss_target'], 'm_mix_norm': out['m_mix_norm'], 'm_attn_w_qkv': out['m_attn_w_qkv'], 'm_attn_q_gain': out['m_attn_q_gain'], 'm_attn_k_gain': out['m_attn_k_gain'], 'm_attn_w_o': out['m_attn_w_o'], 'm_dn_w_in': out['m_dn_w_in'], 'm_dn_conv': out['m_dn_conv'], 'm_dn_a_log': out['m_dn_a_log'], 'm_dn_dt_bias': out['m_dn_dt_bias'], 'm_dn_o_gain': out['m_dn_o_gain'], 'm_dn_w_o': out['m_dn_w_o'], 'm_mlp_norm': out['m_mlp_norm'], 'm_w_up': out['m_w_up'], 'm_w_down': out['m_w_down'], 'm_ple_norm': out['m_ple_norm'], 'm_w_ple': out['m_w_ple'], 'm_w_ple_gate': out['m_w_ple_gate'], 'v_mix_norm': out['v_mix_norm'], 'v_attn_w_qkv': out['v_attn_w_qkv'], 'v_attn_q_gain': out['v_attn_q_gain'], 'v_attn_k_gain': out['v_attn_k_gain'], 'v_attn_w_o': out['v_attn_w_o'], 'v_dn_w_in': out['v_dn_w_in'], 'v_dn_conv': out['v_dn_conv'], 'v_dn_a_log': out['v_dn_a_log'], 'v_dn_dt_bias': out['v_dn_dt_bias'], 'v_dn_o_gain': out['v_dn_o_gain'], 'v_dn_w_o': out['v_dn_w_o'], 'v_mlp_norm': out['v_mlp_norm'], 'v_w_up': out['v_w_up'], 'v_w_down': out['v_w_down'], 'v_ple_norm': out['v_ple_norm'], 'v_w_ple': out['v_w_ple'], 'v_w_ple_gate': out['v_w_ple_gate']}


def _loss(weights, diff, rest, loss_target):
    with _jax.named_scope("forward"):
        args = {**rest, TWIN_DIFF_INPUT: diff, **{k: w.astype(_WEIGHT_DTYPES[k]) for k, w in weights.items()}}
        y = _forward(args)
    with _jax.named_scope("loss_head"):
        err = _jnp.square(y.astype(_jnp.float32) - loss_target)
        return 0.5 * _jnp.sum(_jnp.mean(err, axis=-1)) if err.ndim else 0.5 * err


def _adamw(w, g, m, v):
    m = ADAM_B1 * m + (1.0 - ADAM_B1) * g
    v = ADAM_B2 * v + (1.0 - ADAM_B2) * _jnp.square(g)
    m_hat = m / (1.0 - ADAM_B1 ** ADAM_STEP)
    v_hat = v / (1.0 - ADAM_B2 ** ADAM_STEP)
    delta = -ADAM_LR * (m_hat / (_jnp.sqrt(v_hat) + ADAM_EPS) + ADAM_WD * w)
    return delta, m, v


def reference(x, p, positions, mix_norm, attn_w_qkv, attn_q_gain, attn_k_gain, attn_w_o, dn_w_in, dn_conv, dn_a_log, dn_dt_bias, dn_o_gain, dn_w_o, mlp_norm, w_up, w_down, ple_norm, w_ple, w_ple_gate, loss_target, m_mix_norm, m_attn_w_qkv, m_attn_q_gain, m_attn_k_gain, m_attn_w_o, m_dn_w_in, m_dn_conv, m_dn_a_log, m_dn_dt_bias, m_dn_o_gain, m_dn_w_o, m_mlp_norm, m_w_up, m_w_down, m_ple_norm, m_w_ple, m_w_ple_gate, v_mix_norm, v_attn_w_qkv, v_attn_q_gain, v_attn_k_gain, v_attn_w_o, v_dn_w_in, v_dn_conv, v_dn_a_log, v_dn_dt_bias, v_dn_o_gain, v_dn_w_o, v_mlp_norm, v_w_up, v_w_down, v_ple_norm, v_w_ple, v_w_ple_gate):
    given = dict(x=x, p=p, positions=positions, mix_norm=mix_norm, attn_w_qkv=attn_w_qkv, attn_q_gain=attn_q_gain, attn_k_gain=attn_k_gain, attn_w_o=attn_w_o, dn_w_in=dn_w_in, dn_conv=dn_conv, dn_a_log=dn_a_log, dn_dt_bias=dn_dt_bias, dn_o_gain=dn_o_gain, dn_w_o=dn_w_o, mlp_norm=mlp_norm, w_up=w_up, w_down=w_down, ple_norm=ple_norm, w_ple=w_ple, w_ple_gate=w_ple_gate, loss_target=loss_target, m_mix_norm=m_mix_norm, m_attn_w_qkv=m_attn_w_qkv, m_attn_q_gain=m_attn_q_gain, m_attn_k_gain=m_attn_k_gain, m_attn_w_o=m_attn_w_o, m_dn_w_in=m_dn_w_in, m_dn_conv=m_dn_conv, m_dn_a_log=m_dn_a_log, m_dn_dt_bias=m_dn_dt_bias, m_dn_o_gain=m_dn_o_gain, m_dn_w_o=m_dn_w_o, m_mlp_norm=m_mlp_norm, m_w_up=m_w_up, m_w_down=m_w_down, m_ple_norm=m_ple_norm, m_w_ple=m_w_ple, m_w_ple_gate=m_w_ple_gate, v_mix_norm=v_mix_norm, v_attn_w_qkv=v_attn_w_qkv, v_attn_q_gain=v_attn_q_gain, v_attn_k_gain=v_attn_k_gain, v_attn_w_o=v_attn_w_o, v_dn_w_in=v_dn_w_in, v_dn_conv=v_dn_conv, v_dn_a_log=v_dn_a_log, v_dn_dt_bias=v_dn_dt_bias, v_dn_o_gain=v_dn_o_gain, v_dn_w_o=v_dn_w_o, v_mlp_norm=v_mlp_norm, v_w_up=v_w_up, v_w_down=v_w_down, v_ple_norm=v_ple_norm, v_w_ple=v_w_ple, v_w_ple_gate=v_w_ple_gate)
    weights = {n: given[n] for n in TWIN_WEIGHTS}
    shared = {n: given[n] for n in SHARED_INPUTS}
    per_example = {n: given[n] for n in ['x', 'p', 'positions']}
    grad_fn = _jax.value_and_grad(_loss, argnums=(0, 1))

    def one_microbatch(ex, loss_target):
        ex = dict(ex)
        diff = ex.pop(TWIN_DIFF_INPUT)
        return grad_fn(weights, diff, {**shared, **ex}, loss_target)

    if N_MICROBATCH == 1:
        loss, (grad_w, grad_x) = one_microbatch(per_example, given["loss_target"])
    else:
        def body(carry, xs):
            loss_sum, grad_sum = carry
            l_k, (gw_k, gx_k) = one_microbatch(xs[0], xs[1])
            with _jax.named_scope("update"):
                return (loss_sum + l_k, _jax.tree.map(_jnp.add, grad_sum, gw_k)), gx_k

        init = (_jnp.zeros((), _jnp.float32), _jax.tree.map(_jnp.zeros_like, weights))
        (loss, grad_w), grad_x = _jax.lax.scan(body, init, (per_example, given["loss_target"]))
    with _jax.named_scope("update"):
        delta_w, new_m, new_v = {}, {}, {}
        for n in TWIN_WEIGHTS:
            delta_w[n], new_m[n], new_v[n] = _adamw(weights[n], grad_w[n], given["m_" + n], given["v_" + n])
    return (loss, grad_x, *[grad_w[n] for n in TWIN_WEIGHTS], *[delta_w[n] for n in TWIN_WEIGHTS],
            *[new_m[n] for n in TWIN_WEIGHTS], *[new_v[n] for n in TWIN_WEIGHTS])
```

```python
import functools
import math

import jax
import jax.numpy as jnp
from jax import lax
from jax.experimental import pallas as pl
from jax.experimental.pallas import tpu as pltpu

F32 = jnp.float32
BF16 = jnp.bfloat16
HIGHEST = lax.Precision.HIGHEST

D_MODEL = 1024
EPS = 1e-6
SWA_GROUPS = ((128, 1), (512, 4), (2048, 16))
A_HEADS = 8
A_HEAD_DIM = 64
A_WIDTH = A_HEADS * A_HEAD_DIM
ROPE_DIM = A_HEAD_DIM // 4
ROPE_THETA = 500000.0
BAND = 128
DN_HEADS = 8
DN_HEAD_DIM = 128
DN_WIDTH = DN_HEADS * DN_HEAD_DIM
DN_QKV = 3 * DN_WIDTH
DN_AB_PAD = 128
DN_IN_PAD = DN_QKV + DN_AB_PAD + DN_WIDTH
DN_Z_BLOCK0 = (DN_QKV + DN_AB_PAD) // DN_HEAD_DIM
CONV_WIDTH = 4
CHUNK = 64
PLE_DIM = 256
D_FF = 4 * D_MODEL

ADAM_LR = 0.001
ADAM_B1 = 0.9
ADAM_B2 = 0.999
ADAM_EPS = 1e-08
ADAM_WD = 0.01
ADAM_STEP = 10

N_CHIPS = 4
N_DEV = 8
PACK_WIDTH = 1024
PACK_ROW_TILE = 256
VMEM_LIMIT = 48 * 1024 * 1024
MESH = pl.DeviceIdType.MESH

SHARDED = (
    ("attn_w_qkv", 2), ("attn_w_o", 2), ("dn_w_in", 2), ("dn_conv", 2), ("dn_w_o", 1),
    ("w_up", 2), ("w_down", 1), ("w_ple", 2), ("w_ple_gate", 1))
REPLICATED = ("mix_norm", "attn_q_gain", "attn_k_gain", "dn_a_log", "dn_dt_bias", "dn_o_gain",
              "mlp_norm", "ple_norm")
WEIGHTS = ("mix_norm", "attn_w_qkv", "attn_q_gain", "attn_k_gain", "attn_w_o", "dn_w_in", "dn_conv",
           "dn_a_log", "dn_dt_bias", "dn_o_gain", "dn_w_o", "mlp_norm", "w_up", "w_down", "ple_norm",
           "w_ple", "w_ple_gate")


def _cparams(sem=None):
    return pltpu.CompilerParams(dimension_semantics=sem, vmem_limit_bytes=VMEM_LIMIT)


def _pick(n, cap, quantum=128):
    best = None
    for t in range(quantum, min(n, cap) + 1, quantum):
        if n % t == 0:
            best = t
    return n if best is None else best


_DIMS = {"nn": ((1,), (0,)), "nt": ((1,), (1,)), "tn": ((0,), (0,))}


def _mm(name, a, b, mode, out_dtypes=(F32,), extras=(), epilogue=None):
    if mode == "nn":
        (M, K), (K2, N) = a.shape, b.shape
    elif mode == "nt":
        (M, K), (N, K2) = a.shape, b.shape
    else:
        (K, M), (K2, N) = a.shape, b.shape
    assert K == K2, (name, a.shape, b.shape)
    tm, tn, tk = _pick(M, 512), _pick(N, 1536), _pick(K, 1024 if mode == "tn" else 1536)
    nk = K // tk
    if mode == "nn":
        a_spec = pl.BlockSpec((tm, tk), lambda i, j, k: (i, k))
        b_spec = pl.BlockSpec((tk, tn), lambda i, j, k: (k, j))
    elif mode == "nt":
        a_spec = pl.BlockSpec((tm, tk), lambda i, j, k: (i, k))
        b_spec = pl.BlockSpec((tn, tk), lambda i, j, k: (j, k))
    else:
        a_spec = pl.BlockSpec((tk, tm), lambda i, j, k: (k, i))
        b_spec = pl.BlockSpec((tk, tn), lambda i, j, k: (k, j))
    o_spec = pl.BlockSpec((tm, tn), lambda i, j, k: (i, j))
    n_extra, n_out = len(extras), len(out_dtypes)
    dims = (_DIMS[mode], ((), ()))

    def body(a_ref, b_ref, *rest):
        extra_refs, out_refs, acc = rest[:n_extra], rest[n_extra:n_extra + n_out], rest[-1]
        k = pl.program_id(2)

        @pl.when(k == 0)
        def _():
            acc[...] = jnp.zeros_like(acc)

        acc[...] += lax.dot_general(a_ref[...].astype(BF16), b_ref[...].astype(BF16), dims,
                                    preferred_element_type=F32)

        @pl.when(k == nk - 1)
        def _():
            if epilogue is None:
                vals = (acc[...],)
            else:
                vals = epilogue(acc[...], *[e[...] for e in extra_refs])
            for o, v in zip(out_refs, vals):
                o[...] = v.astype(o.dtype)

    outs = pl.pallas_call(
        body, name=name, grid=(M // tm, N // tn, nk),
        in_specs=[a_spec, b_spec] + [o_spec] * n_extra,
        out_specs=[o_spec] * n_out,
        out_shape=[jax.ShapeDtypeStruct((M, N), dt) for dt in out_dtypes],
        scratch_shapes=[pltpu.VMEM((tm, tn), F32)],
        compiler_params=_cparams(("parallel", "parallel", "arbitrary")),
    )(a, b, *extras)
    return outs[0] if n_out == 1 else outs


def _rowwise(name, fn, rows, bcast, row_outs, acc_outs=(), tm=256):
    rows = [r if isinstance(r, tuple) else (r, r.shape[1], 0) for r in rows]
    S = rows[0][0].shape[0]
    tm = min(tm, S)
    assert S % tm == 0, (name, S, tm)
    n_row, n_bc, n_ro, n_acc = len(rows), len(bcast), len(row_outs), len(acc_outs)
    in_specs = [pl.BlockSpec((tm, w), functools.partial(lambda i, cb: (i, cb), cb=cb)) for _, w, cb in rows]
    in_specs += [pl.BlockSpec(b.shape, lambda i: (0, 0)) for b in bcast]
    out_specs = [pl.BlockSpec((tm, c), lambda i: (i, 0)) for c, _ in row_outs]
    out_specs += [pl.BlockSpec(s, lambda i: (0, 0)) for s in acc_outs]
    out_shape = [jax.ShapeDtypeStruct((S, c), dt) for c, dt in row_outs]
    out_shape += [jax.ShapeDtypeStruct(s, F32) for s in acc_outs]

    def body(*refs):
        ins = [r[...] for r in refs[:n_row + n_bc]]
        outs = refs[n_row + n_bc:]
        vals = fn(*ins)
        if not isinstance(vals, (tuple, list)):
            vals = (vals,)
        for o, v in zip(outs[:n_ro], vals[:n_ro]):
            o[...] = v.astype(o.dtype)
        if n_acc:
            @pl.when(pl.program_id(0) == 0)
            def _():
                for o in outs[n_ro:]:
                    o[...] = jnp.zeros_like(o)
            for o, v in zip(outs[n_ro:], vals[n_ro:]):
                o[...] += v

    outs = pl.pallas_call(
        body, name=name, grid=(S // tm,), in_specs=in_specs, out_specs=out_specs, out_shape=out_shape,
        compiler_params=_cparams(("arbitrary",) if n_acc else ("parallel",)),
    )(*[r[0] for r in rows], *bcast)
    return outs[0] if len(outs) == 1 else outs


def _sigmoid(x):
    return 1.0 / (1.0 + jnp.exp(-x))


def _silu(x):
    return x * _sigmoid(x)


def _softplus(x):
    return jnp.maximum(x, 0.0) + jnp.log(1.0 + jnp.exp(-jnp.abs(x)))


def _rms_fwd_fn(x, g):
    r = lax.rsqrt(jnp.mean(x * x, axis=-1, keepdims=True) + EPS)
    return (x * r) * g


def _rms_bwd_fn(x, dres, *rest):
    dh, g = sum(rest[:-1]), rest[-1]
    r = lax.rsqrt(jnp.mean(x * x, axis=-1, keepdims=True) + EPS)
    xh = x * r
    dxh = dh * g
    dx = r * (dxh - xh * jnp.mean(dxh * xh, axis=-1, keepdims=True))
    return dres + dx, jnp.sum(dh * xh, axis=0, keepdims=True)


def _rms_fwd(name, x, gain):
    return _rowwise(name, _rms_fwd_fn, [x], [gain.reshape(1, -1)], [(x.shape[1], BF16)])


def _rms_bwd(name, x, gain, dres, dhs):
    return _rowwise(name, _rms_bwd_fn, [x, dres] + list(dhs), [gain.reshape(1, -1)],
                    [(x.shape[1], F32)], [(1, x.shape[1])])


def _relu2_epilogue(acc):
    r = jnp.maximum(acc, 0.0)
    return acc, r * r


def _relu2_bwd_epilogue(acc, u):
    return (acc * (2.0 * jnp.maximum(u, 0.0)),)


def _ple_fwd_fn(x, pp, zg):
    return x + pp * _sigmoid(zg)


def _ple_bwd_fn(dx, pp, zg):
    gate = _sigmoid(zg)
    return dx * gate, dx * pp * gate * (1.0 - gate)


def _loss_fn(y, t):
    err = y - t
    return err * (1.0 / D_MODEL), jnp.broadcast_to(jnp.sum(err * err, keepdims=True), (1, 128))


def _adamw(w, g, m, v):
    m = ADAM_B1 * m + (1.0 - ADAM_B1) * g
    v = ADAM_B2 * v + (1.0 - ADAM_B2) * jnp.square(g)
    m_hat = m / (1.0 - ADAM_B1 ** ADAM_STEP)
    v_hat = v / (1.0 - ADAM_B2 ** ADAM_STEP)
    delta = -ADAM_LR * (m_hat / (jnp.sqrt(v_hat) + ADAM_EPS) + ADAM_WD * w)
    return delta, m, v


def _adamw_pair_fn(ga, gb, w, m, v):
    g = ga + gb
    return (g,) + _adamw(w, g, m, v)


def _adamw_slots_fn(slots, w, m, v):
    g = slots[0]
    for d in range(1, N_DEV):
        g = g + slots[d]
    return (g,) + _adamw(w, g, m, v)


def _lane_take(x, offset):
    n = x.shape[-1]
    return pltpu.roll(x, (-offset) % n, 1)


def _head_lane(shape):
    return lax.broadcasted_iota(jnp.int32, shape, 1) % A_HEAD_DIM


def _rope_partner(x):
    lane = _head_lane(x.shape)
    return jnp.where(lane < ROPE_DIM // 2, _lane_take(x, ROPE_DIM // 2),
                     jnp.where(lane < ROPE_DIM, _lane_take(x, -(ROPE_DIM // 2)), 0.0))


def _head_mean(x, bd):
    return jnp.dot(x, bd, precision=HIGHEST, preferred_element_type=F32)


def _fold_heads(row):
    out = row[:, :A_HEAD_DIM]
    for h in range(1, A_HEADS):
        out = out + row[:, h * A_HEAD_DIM:(h + 1) * A_HEAD_DIM]
    return out


def _qk_prep_fwd_fn(qkv, ct, st, gq, gk, bd):
    def one(t, g):
        n = t * lax.rsqrt(_head_mean(t * t, bd) + EPS) * g
        return n * ct + _rope_partner(n) * st
    q, k, v = qkv[:, :A_WIDTH], qkv[:, A_WIDTH:2 * A_WIDTH], qkv[:, 2 * A_WIDTH:]
    return one(q, gq), one(k, gk), v


def _qk_prep_bwd_fn(qkv, ct, st, dq, dk, dv, gq, gk, bd):
    def one(t, g, dy):
        r = lax.rsqrt(_head_mean(t * t, bd) + EPS)
        nh = t * r
        dn = dy * ct + _rope_partner(dy * st)
        dg = jnp.sum(dn * nh, axis=0, keepdims=True)
        dnh = dn * g
        return r * (dnh - nh * _head_mean(dnh * nh, bd)), _fold_heads(dg)
    q, k = qkv[:, :A_WIDTH], qkv[:, A_WIDTH:2 * A_WIDTH]
    dq_raw, dgq = one(q, gq, dq)
    dk_raw, dgk = one(k, gk, dk)
    return jnp.concatenate([dq_raw, dk_raw, dv], axis=1), dgq, dgk


def _band_masks():
    qi = lax.broadcasted_iota(jnp.int32, (BAND, BAND), 0)
    kj = lax.broadcasted_iota(jnp.int32, (BAND, BAND), 1)
    return kj <= qi, kj >= qi


def _attn_fwd(name, q, k, v, blocks_per_class):
    S = q.shape[0]
    nblk = S // BAND
    scale = A_HEAD_DIM ** -0.5

    def body(q_ref, kp_ref, kc_ref, vp_ref, vc_ref, o_ref, l_ref):
        i = pl.program_id(0)
        has_prev = (i % blocks_per_class) != 0
        m_cur, m_prev = _band_masks()
        m_prev = jnp.logical_and(m_prev, has_prev)
        for h in range(A_HEADS):
            sl = slice(h * A_HEAD_DIM, (h + 1) * A_HEAD_DIM)
            qh = q_ref[:, sl]
            s_c = lax.dot_general(qh, kc_ref[:, sl], (_DIMS["nt"], ((), ())), preferred_element_type=F32) * scale
            s_p = lax.dot_general(qh, kp_ref[:, sl], (_DIMS["nt"], ((), ())), preferred_element_type=F32) * scale
            s_c = jnp.where(m_cur, s_c, -jnp.inf)
            s_p = jnp.where(m_prev, s_p, -jnp.inf)
            m = jnp.maximum(jnp.max(s_c, axis=-1, keepdims=True), jnp.max(s_p, axis=-1, keepdims=True))
            e_c, e_p = jnp.exp(s_c - m), jnp.exp(s_p - m)
            l = jnp.sum(e_c, axis=-1, keepdims=True) + jnp.sum(e_p, axis=-1, keepdims=True)
            o = jnp.dot((e_c / l).astype(BF16), vc_ref[:, sl], preferred_element_type=F32)
            o = o + jnp.dot((e_p / l).astype(BF16), vp_ref[:, sl], preferred_element_type=F32)
            o_ref[:, sl] = o
            l_ref[:, sl] = jnp.broadcast_to(m + jnp.log(l), (BAND, A_HEAD_DIM))

    cur = pl.BlockSpec((BAND, A_WIDTH), lambda i: (i, 0))
    prev = pl.BlockSpec((BAND, A_WIDTH), lambda i: (jnp.maximum(i - 1, 0), 0))
    return pl.pallas_call(
        body, name=name, grid=(nblk,), in_specs=[cur, prev, cur, prev, cur], out_specs=[cur, cur],
        out_shape=[jax.ShapeDtypeStruct((S, A_WIDTH), F32)] * 2,
        compiler_params=_cparams(("parallel",)),
    )(q, k, k, v, v)


def _attn_bwd(name, q, k, v, o, lse, do, dlse, blocks_per_class):
    S = q.shape[0]
    nblk = S // BAND
    scale = A_HEAD_DIM ** -0.5

    def body(q_ref, kp_ref, kc_ref, vp_ref, vc_ref, o_ref, l_ref, do_ref, dl_ref,
             dq_ref, dk_ref, dv_ref, ck, cv):
        i = pl.program_id(0)

        @pl.when(i == 0)
        def _():
            ck[...] = jnp.zeros_like(ck)
            cv[...] = jnp.zeros_like(cv)

        @pl.when(i == nblk)
        def _():
            dk_ref[...] = ck[...]
            dv_ref[...] = cv[...]

        @pl.when(i < nblk)
        def _():
            has_prev = (i % blocks_per_class) != 0
            m_cur, m_prev = _band_masks()
            m_prev = jnp.logical_and(m_prev, has_prev)
            nt, tn = (_DIMS["nt"], ((), ())), (_DIMS["tn"], ((), ()))
            for h in range(A_HEADS):
                sl = slice(h * A_HEAD_DIM, (h + 1) * A_HEAD_DIM)
                qh, kc, kp, vc, vp = q_ref[:, sl], kc_ref[:, sl], kp_ref[:, sl], vc_ref[:, sl], vp_ref[:, sl]
                doh = do_ref[:, sl]
                lse_h = l_ref[:, sl][:, :1]
                s_c = lax.dot_general(qh, kc, nt, preferred_element_type=F32) * scale
                s_p = lax.dot_general(qh, kp, nt, preferred_element_type=F32) * scale
                p_c = jnp.where(m_cur, jnp.exp(s_c - lse_h), 0.0)
                p_p = jnp.where(m_prev, jnp.exp(s_p - lse_h), 0.0)
                corr = (jnp.sum(dl_ref[:, sl], axis=-1, keepdims=True)
                        - jnp.sum(doh * o_ref[:, sl], axis=-1, keepdims=True))
                dob = doh.astype(BF16)
                dp_c = lax.dot_general(dob, vc, nt, preferred_element_type=F32)
                dp_p = lax.dot_general(dob, vp, nt, preferred_element_type=F32)
                ds_c = (p_c * (dp_c + corr)).astype(BF16)
                ds_p = (p_p * (dp_p + corr)).astype(BF16)
                dq = jnp.dot(ds_c, kc, preferred_element_type=F32) + jnp.dot(ds_p, kp, preferred_element_type=F32)
                dq_ref[:, sl] = dq * scale
                dk_ref[:, sl] = ck[:, sl] + lax.dot_general(ds_p, qh, tn, preferred_element_type=F32) * scale
                dv_ref[:, sl] = cv[:, sl] + lax.dot_general(p_p.astype(BF16), dob, tn, preferred_element_type=F32)
                ck[:, sl] = lax.dot_general(ds_c, qh, tn, preferred_element_type=F32) * scale
                cv[:, sl] = lax.dot_general(p_c.astype(BF16), dob, tn, preferred_element_type=F32)

    last = nblk - 1
    cur = pl.BlockSpec((BAND, A_WIDTH), lambda i: (jnp.minimum(i, last), 0))
    prev = pl.BlockSpec((BAND, A_WIDTH), lambda i: (jnp.minimum(jnp.maximum(i - 1, 0), last), 0))
    return pl.pallas_call(
        body, name=name, grid=(nblk + 1,),
        in_specs=[cur, prev, cur, prev, cur, cur, cur, cur, cur], out_specs=[cur, prev, prev],
        out_shape=[jax.ShapeDtypeStruct((S, A_WIDTH), F32)] * 3,
        scratch_shapes=[pltpu.VMEM((BAND, A_WIDTH), F32)] * 2,
        compiler_params=_cparams(("arbitrary",)),
    )(q, k, k, v, v, o, lse, do, dlse)


def _merge_fwd_fn(o0, o1, o2, l0, l1, l2):
    m = jnp.maximum(jnp.maximum(l0, l1), l2)
    e0, e1, e2 = jnp.exp(l0 - m), jnp.exp(l1 - m), jnp.exp(l2 - m)
    return (e0 * o0 + e1 * o1 + e2 * o2) / (e0 + e1 + e2)


def _merge_bwd_fn(o0, o1, o2, l0, l1, l2, dom):
    m = jnp.maximum(jnp.maximum(l0, l1), l2)
    e0, e1, e2 = jnp.exp(l0 - m), jnp.exp(l1 - m), jnp.exp(l2 - m)
    den = e0 + e1 + e2
    w0, w1, w2 = e0 / den, e1 / den, e2 / den
    dw0, dw1, dw2 = dom * o0, dom * o1, dom * o2
    mean = w0 * dw0 + w1 * dw1 + w2 * dw2
    return w0 * dom, w1 * dom, w2 * dom, w0 * (dw0 - mean), w1 * (dw1 - mean), w2 * (dw2 - mean)


def _to_classes(t, d):
    if d == 1:
        return t
    S, C = t.shape
    return t.reshape(S // d, d, C).transpose(1, 0, 2).reshape(S, C)


def _from_classes(t, d):
    if d == 1:
        return t
    S, C = t.shape
    return t.reshape(d, S // d, C).transpose(1, 0, 2).reshape(S, C)


def _rope_lane_tables(positions):
    inv_freq = ROPE_THETA ** (-jnp.arange(0, ROPE_DIM, 2, dtype=F32) / ROPE_DIM)
    ang = positions.astype(F32)[:, None] * inv_freq
    cos, sin = jnp.cos(ang), jnp.sin(ang)
    S = positions.shape[0]
    rest = A_HEAD_DIM - ROPE_DIM
    ct = jnp.concatenate([cos, cos, jnp.ones((S, rest), F32)], axis=1)
    st = jnp.concatenate([-sin, sin, jnp.zeros((S, rest), F32)], axis=1)
    return jnp.tile(ct, (1, A_HEADS)), jnp.tile(st, (1, A_HEADS))


def _head_mean_matrix():
    r = jnp.arange(A_WIDTH) // A_HEAD_DIM
    return (r[:, None] == r[None, :]).astype(F32) * (1.0 / A_HEAD_DIM)


def _conv_fwd(name, proj, w):
    S = proj.shape[0]
    tm, tc = min(512, S), 1024
    per8 = tm // 8

    def body(x_ref, halo_ref, w_ref, o_ref, xs):
        i = pl.program_id(0)
        xs[0:8, :] = jnp.where(i > 0, halo_ref[...], 0.0)
        xs[8:, :] = x_ref[...]
        acc = w_ref[0:1, :] * xs[pl.ds(8 - 3, tm), :]
        for j in range(1, CONV_WIDTH):
            acc = acc + w_ref[j:j + 1, :] * xs[pl.ds(8 - 3 + j, tm), :]
        o_ref[...] = acc

    return pl.pallas_call(
        body, name=name, grid=(S // tm, DN_QKV // tc),
        in_specs=[pl.BlockSpec((tm, tc), lambda i, j: (i, j)),
                  pl.BlockSpec((8, tc), lambda i, j: (jnp.maximum(i * per8 - 1, 0), j)),
                  pl.BlockSpec((CONV_WIDTH, tc), lambda i, j: (0, j))],
        out_specs=pl.BlockSpec((tm, tc), lambda i, j: (i, j)),
        out_shape=jax.ShapeDtypeStruct((S, DN_QKV), F32),
        scratch_shapes=[pltpu.VMEM((tm + 8, tc), F32)],
        compiler_params=_cparams(("parallel", "parallel")),
    )(proj, proj, w)


def _conv_bwd(name, proj, dpre, w):
    S = proj.shape[0]
    tm, tc = min(512, S), 1024
    per8 = tm // 8
    last8 = S // 8 - 1
    nrow = S // tm

    def body(x_ref, xh_ref, d_ref, dh_ref, w_ref, dx_ref, dw_ref, xs, ds):
        i = pl.program_id(1)
        xs[0:8, :] = jnp.where(i > 0, xh_ref[...], 0.0)
        xs[8:, :] = x_ref[...]
        ds[0:tm, :] = d_ref[...]
        ds[tm:, :] = jnp.where(i < nrow - 1, dh_ref[...], 0.0)
        d = d_ref[...]
        acc = w_ref[0:1, :] * ds[pl.ds(3, tm), :]
        for j in range(1, CONV_WIDTH):
            acc = acc + w_ref[j:j + 1, :] * ds[pl.ds(3 - j, tm), :]
        dx_ref[...] = acc

        @pl.when(i == 0)
        def _():
            dw_ref[...] = jnp.zeros_like(dw_ref)

        for j in range(CONV_WIDTH):
            dw_ref[j:j + 1, :] += jnp.sum(d * xs[pl.ds(8 - 3 + j, tm), :], axis=0, keepdims=True)

    return pl.pallas_call(
        body, name=name, grid=(DN_QKV // tc, nrow),
        in_specs=[pl.BlockSpec((tm, tc), lambda j, i: (i, j)),
                  pl.BlockSpec((8, tc), lambda j, i: (jnp.maximum(i * per8 - 1, 0), j)),
                  pl.BlockSpec((tm, tc), lambda j, i: (i, j)),
                  pl.BlockSpec((8, tc), lambda j, i: (jnp.minimum((i + 1) * per8, last8), j)),
                  pl.BlockSpec((CONV_WIDTH, tc), lambda j, i: (0, j))],
        out_specs=[pl.BlockSpec((tm, tc), lambda j, i: (i, j)),
                   pl.BlockSpec((CONV_WIDTH, tc), lambda j, i: (0, j))],
        out_shape=[jax.ShapeDtypeStruct((S, DN_QKV), F32), jax.ShapeDtypeStruct((CONV_WIDTH, DN_QKV), F32)],
        scratch_shapes=[pltpu.VMEM((tm + 8, tc), F32)] * 2,
        compiler_params=_cparams(("parallel", "arbitrary")),
    )(proj, proj, dpre, dpre, w)


def _gate_lane(shape):
    return lax.broadcasted_iota(jnp.int32, shape, 1)


def _gates_fwd_fn(ab, alog, dt):
    g = -jnp.exp(alog) * _softplus(ab + dt)
    return jnp.where(_gate_lane(ab.shape) < DN_HEADS, g, _sigmoid(ab))


def _gates_bwd_fn(ab, dgb, alog, dt):
    lane = _gate_lane(ab.shape)
    is_g = lane < DN_HEADS
    neg_a = -jnp.exp(alog)
    sp = _softplus(ab + dt)
    dsp = _sigmoid(ab + dt)
    beta = _sigmoid(ab)
    dg = jnp.where(is_g, dgb, 0.0)
    dab = jnp.where(is_g, dg * neg_a * dsp, jnp.where(lane < 2 * DN_HEADS, dgb * beta * (1.0 - beta), 0.0))
    d_alog = jnp.sum(dg * neg_a * sp, axis=0, keepdims=True)
    d_dt = jnp.sum(dg * neg_a * dsp, axis=0, keepdims=True)
    return dab, d_alog, d_dt


def _chunk_math(precision):
    def dg(a, b, mode, prec=precision):
        return lax.dot_general(a, b, (_DIMS[mode], ((), ())), precision=prec, preferred_element_type=F32)

    @jax.custom_vjp
    def nn(a, b):
        return dg(a, b, "nn")

    @jax.custom_vjp
    def nt(a, b):
        return dg(a, b, "nt")

    @jax.custom_vjp
    def tn(a, b):
        return dg(a, b, "tn")

    nn.defvjp(lambda a, b: (nn(a, b), (a, b)), lambda r, g: (nt(g, r[1]), tn(r[0], g)))
    nt.defvjp(lambda a, b: (nt(a, b), (a, b)), lambda r, g: (nn(g, r[1]), tn(g, r[0])))
    tn.defvjp(lambda a, b: (tn(a, b), (a, b)), lambda r, g: (nt(r[1], g), nn(r[0], g)))

    @jax.custom_vjp
    def nn_exact(a, b):
        return dg(a, b, "nn", HIGHEST)

    nn_exact.defvjp(lambda a, b: (nn_exact(a, b), (a, b)),
                    lambda r, g: (dg(g, r[1], "nt", HIGHEST), dg(r[0], g, "tn", HIGHEST)))

    def unit_lower_inverse(a):
        row = lax.broadcasted_iota(jnp.int32, (CHUNK, CHUNK), 0)
        col = lax.broadcasted_iota(jnp.int32, (CHUNK, CHUNK), 1)
        x = -a
        p = jnp.where(row == col, 1.0, 0.0) + x
        for _ in range(int(math.log2(CHUNK)) - 1):
            x = dg(x, x, "nn", HIGHEST)
            p = p + dg(p, x, "nn", HIGHEST)
        return p

    @jax.custom_vjp
    def solve2(a, r1, r2):
        ti = unit_lower_inverse(a)
        return dg(ti, r1, "nn", HIGHEST), dg(ti, r2, "nn", HIGHEST)

    def solve2_fwd(a, r1, r2):
        ti = unit_lower_inverse(a)
        s1, s2 = dg(ti, r1, "nn", HIGHEST), dg(ti, r2, "nn", HIGHEST)
        return (s1, s2), (ti, s1, s2)

    def solve2_bwd(res, g):
        ti, s1, s2 = res
        d1, d2 = dg(ti, g[0], "tn", HIGHEST), dg(ti, g[1], "tn", HIGHEST)
        return -(dg(d1, s1, "nt", HIGHEST) + dg(d2, s2, "nt", HIGHEST)), d1, d2

    solve2.defvjp(solve2_fwd, solve2_bwd)

    def chunk_fn(pq, pk, pv, z, g_col, b_col, g_row, ogain, s_in):
        row = lax.broadcasted_iota(jnp.int32, (CHUNK, CHUNK), 0)
        col = lax.broadcasted_iota(jnp.int32, (CHUNK, CHUNK), 1)
        lower, strict = row >= col, row > col
        ltri = jnp.where(lower, 1.0, 0.0)
        utri = jnp.where(row <= col, 1.0, 0.0)
        q, k, v = _silu(pq), _silu(pk), _silu(pv)
        q = q * lax.rsqrt(jnp.sum(q * q, axis=-1, keepdims=True) + EPS) * (DN_HEAD_DIM ** -0.5)
        k = k * lax.rsqrt(jnp.sum(k * k, axis=-1, keepdims=True) + EPS)
        g_wide = jnp.broadcast_to(g_col, (CHUNK, DN_HEAD_DIM))
        gc_wide = nn_exact(ltri, g_wide)
        gc_i = nn_exact(ltri, jnp.broadcast_to(g_col, (CHUNK, CHUNK)))
        gc_j = nn_exact(jnp.broadcast_to(g_row, (CHUNK, CHUNK)), utri)
        g_last = jnp.sum(g_wide, axis=0, keepdims=True)
        decay = jnp.exp(jnp.where(lower, gc_i - gc_j, -jnp.inf))
        kb = k * b_col
        a_mat = jnp.where(strict, nt(kb, k) * decay, 0.0)
        eg = jnp.exp(gc_wide)
        u, w = solve2(a_mat, v * b_col, kb * eg)
        attn = nt(q, k) * decay
        q_dec = q * eg
        k_dec = k * jnp.exp(g_last - gc_wide)
        c_dec = jnp.exp(g_last)
        v_new = u - nn(w, s_in)
        o = nn(q_dec, s_in) + nn(attn, v_new)
        s_out = s_in * c_dec + tn(k_dec, v_new)
        y = o * lax.rsqrt(jnp.mean(o * o, axis=-1, keepdims=True) + EPS) * ogain * _silu(z)
        return y, s_out

    return chunk_fn


DN_PRECISION = HIGHEST


def _chunk_specs(n_of):
    hd = pl.BlockSpec((CHUNK, DN_HEAD_DIM), lambda h, n: (n_of(n), h))
    specs = dict(
        pq=hd,
        pk=pl.BlockSpec((CHUNK, DN_HEAD_DIM), lambda h, n: (n_of(n), DN_HEADS + h)),
        pv=pl.BlockSpec((CHUNK, DN_HEAD_DIM), lambda h, n: (n_of(n), 2 * DN_HEADS + h)),
        z=pl.BlockSpec((CHUNK, DN_HEAD_DIM), lambda h, n: (n_of(n), DN_Z_BLOCK0 + h)),
        col=pl.BlockSpec((None, CHUNK, 1), lambda h, n: (h, n_of(n), 0)),
        row=pl.BlockSpec((None, None, 1, CHUNK), lambda h, n: (h, n_of(n), 0, 0)),
        gain=pl.BlockSpec((1, DN_HEAD_DIM), lambda h, n: (0, 0)),
        state=pl.BlockSpec((None, None, DN_HEAD_DIM, DN_HEAD_DIM), lambda h, n: (h, n_of(n), 0, 0)),
        head=hd,
    )
    return specs


def _chunk_fwd(name, pre, proj, g_col, b_col, g_row, ogain):
    S = pre.shape[0]
    N = S // CHUNK
    chunk_fn = _chunk_math(DN_PRECISION)
    sp = _chunk_specs(lambda n: n)

    def body(pq, pk, pv, z, gc, bc, gr, og, y_ref, sin_ref, st):
        @pl.when(pl.program_id(1) == 0)
        def _():
            st[...] = jnp.zeros_like(st)

        s_in = st[...]
        sin_ref[...] = s_in
        y, s_out = chunk_fn(pq[...], pk[...], pv[...], z[...], gc[...], bc[...], gr[...], og[...], s_in)
        y_ref[...] = y.astype(y_ref.dtype)
        st[...] = s_out

    return pl.pallas_call(
        body, name=name, grid=(DN_HEADS, N),
        in_specs=[sp["pq"], sp["pk"], sp["pv"], sp["z"], sp["col"], sp["col"], sp["row"], sp["gain"]],
        out_specs=[sp["head"], sp["state"]],
        out_shape=[jax.ShapeDtypeStruct((S, DN_WIDTH), BF16),
                   jax.ShapeDtypeStruct((DN_HEADS, N, DN_HEAD_DIM, DN_HEAD_DIM), F32)],
        scratch_shapes=[pltpu.VMEM((DN_HEAD_DIM, DN_HEAD_DIM), F32)],
        compiler_params=_cparams(("parallel", "arbitrary")),
    )(pre, pre, pre, proj, g_col, b_col, g_row, ogain)


def _chunk_bwd(name, pre, proj, g_col, b_col, g_row, ogain, s_in_all, dy):
    S = pre.shape[0]
    N = S // CHUNK
    chunk_fn = _chunk_math(DN_PRECISION)
    sp = _chunk_specs(lambda n: N - 1 - n)

    def body(pq, pk, pv, z, gc, bc, gr, og, sin_ref, dy_ref,
             dq_ref, dk_ref, dv_ref, dz_ref, dgc_ref, dbc_ref, dgr_ref, dog_ref, ds):
        first = jnp.logical_and(pl.program_id(0) == 0, pl.program_id(1) == 0)

        @pl.when(pl.program_id(1) == 0)
        def _():
            ds[...] = jnp.zeros_like(ds)

        @pl.when(first)
        def _():
            dog_ref[...] = jnp.zeros_like(dog_ref)

        prim = (pq[...], pk[...], pv[...], z[...], gc[...], bc[...], gr[...], og[...], sin_ref[...])
        _, vjp = jax.vjp(chunk_fn, *prim)
        gq, gk, gv, gz, ggc, gbc, ggr, gog, gs = vjp((dy_ref[...], ds[...]))
        dq_ref[...] = gq
        dk_ref[...] = gk
        dv_ref[...] = gv
        dz_ref[...] = gz
        dgc_ref[...] = ggc
        dbc_ref[...] = gbc
        dgr_ref[...] = ggr
        dog_ref[...] += gog
        ds[...] = gs

    hd = sp["head"]
    return pl.pallas_call(
        body, name=name, grid=(DN_HEADS, N),
        in_specs=[sp["pq"], sp["pk"], sp["pv"], sp["z"], sp["col"], sp["col"], sp["row"], sp["gain"],
                  sp["state"], hd],
        out_specs=[hd, hd, hd, hd, sp["col"], sp["col"], sp["row"], sp["gain"]],
        out_shape=[jax.ShapeDtypeStruct((S, DN_WIDTH), F32)] * 4
        + [jax.ShapeDtypeStruct((DN_HEADS, S, 1), F32)] * 2
        + [jax.ShapeDtypeStruct((DN_HEADS, N, 1, CHUNK), F32), jax.ShapeDtypeStruct((1, DN_HEAD_DIM), F32)],
        scratch_shapes=[pltpu.VMEM((DN_HEAD_DIM, DN_HEAD_DIM), F32)],
        compiler_params=_cparams(("arbitrary", "arbitrary")),
    )(pre, pre, pre, proj, g_col, b_col, g_row, ogain, s_in_all, dy)


def _mlp_ple_fwd(tag, x_in, p_l, norm_mlp, w_up, w_down, norm_ple, w_ple, w_gate):
    h = _rms_fwd(f"{tag}_mlp_norm", x_in, norm_mlp)
    u, a = _mm(f"{tag}_up", h, w_up, "nn", out_dtypes=(F32, BF16), epilogue=_relu2_epilogue)
    x_mid = _mm(f"{tag}_down", a, w_down, "nn", extras=(x_in,), epilogue=lambda acc, r: (acc + r,))
    hg = _rms_fwd(f"{tag}_ple_norm", x_mid, norm_ple)
    zg = _mm(f"{tag}_gate", hg, w_gate, "nn")
    pp = _mm(f"{tag}_ple", p_l, w_ple, "nn")
    x_out = _rowwise(f"{tag}_ple_out", _ple_fwd_fn, [x_mid, pp, zg], [], [(D_MODEL, F32)])
    return x_out, dict(x_in=x_in, h=h, u=u, a=a, x_mid=x_mid, hg=hg, zg=zg, pp=pp)


def _mlp_ple_bwd(tag, dx, sv, p_l, norm_mlp, w_up, w_down, norm_ple, w_ple, w_gate):
    dpp, dzg = _rowwise(f"{tag}_ple_bwd", _ple_bwd_fn, [dx, sv["pp"], sv["zg"]], [],
                        [(D_MODEL, BF16), (D_MODEL, BF16)])
    d_w_ple = _mm(f"{tag}_d_w_ple", p_l, dpp, "tn")
    d_w_gate = _mm(f"{tag}_d_w_gate", sv["hg"], dzg, "tn")
    dhg = _mm(f"{tag}_d_hg", dzg, w_gate, "nt")
    dx_mid, d_norm_ple = _rms_bwd(f"{tag}_ple_norm_bwd", sv["x_mid"], norm_ple, dx, [dhg])
    du = _mm(f"{tag}_d_u", dx_mid, w_down, "nt", out_dtypes=(BF16,), extras=(sv["u"],),
             epilogue=_relu2_bwd_epilogue)
    d_w_down = _mm(f"{tag}_d_w_down", sv["a"], dx_mid, "tn")
    d_w_up = _mm(f"{tag}_d_w_up", sv["h"], du, "tn")
    dh = _mm(f"{tag}_d_h", du, w_up, "nt")
    dx_in, d_norm_mlp = _rms_bwd(f"{tag}_mlp_norm_bwd", sv["x_in"], norm_mlp, dx_mid, [dh])
    return dx_in, dict(mlp_norm=d_norm_mlp, w_up=d_w_up, w_down=d_w_down, ple_norm=d_norm_ple,
                       w_ple=d_w_ple, w_ple_gate=d_w_gate)


def _local_step(x, p, positions, target, small, big):
    S = x.shape[0]
    ct, st = _rope_lane_tables(positions)
    bd = _head_mean_matrix()

    h0 = _rms_fwd("l0_mix_norm", x, small["mix_norm"][0])
    attn = []
    for g, (window, d) in enumerate(SWA_GROUPS):
        assert window // d == BAND and (S // d) % BAND == 0
        h0g = _to_classes(h0, d)
        ctg, stg = _to_classes(ct, d), _to_classes(st, d)
        w_g = big["attn_w_qkv"][:, g * 3 * A_WIDTH:(g + 1) * 3 * A_WIDTH]
        gq = jnp.tile(small["attn_q_gain"][0, g], A_HEADS).reshape(1, A_WIDTH)
        gk = jnp.tile(small["attn_k_gain"][0, g], A_HEADS).reshape(1, A_WIDTH)
        qkv = _mm(f"l0_qkv{g}", h0g, w_g, "nn")
        q, k, v = _rowwise(f"l0_qk_prep{g}", _qk_prep_fwd_fn, [qkv, ctg, stg], [gq, gk, bd], [(A_WIDTH, BF16)] * 3)
        o, lse = _attn_fwd(f"l0_attn{g}", q, k, v, (S // d) // BAND)
        attn.append(dict(d=d, h0g=h0g, ct=ctg, st=stg, w=w_g, gq=gq, gk=gk, qkv=qkv, q=q, k=k, v=v, o=o, lse=lse,
                         o_tok=_from_classes(o, d), lse_tok=_from_classes(lse, d)))
    om = _rowwise("l0_merge", _merge_fwd_fn, [a["o_tok"] for a in attn] + [a["lse_tok"] for a in attn], [],
                  [(A_WIDTH, BF16)])
    x1 = _mm("l0_attn_out", om, big["attn_w_o"], "nn", extras=(x,), epilogue=lambda acc, r: (acc + r,))
    x3, sv0 = _mlp_ple_fwd("l0", x1, p[0], small["mlp_norm"][0], big["w_up"][0], big["w_down"][0],
                           small["ple_norm"][0], big["w_ple"][0], big["w_ple_gate"][0])

    N = S // CHUNK
    h3 = _rms_fwd("l1_mix_norm", x3, small["mix_norm"][1])
    proj = _mm("l1_in", h3, big["dn_w_in"], "nn")
    pre = _conv_fwd("l1_conv", proj, small["dn_conv"])
    ab = proj[:, DN_QKV:DN_QKV + DN_AB_PAD]
    lane_pad = DN_AB_PAD - DN_HEADS
    alog_row = jnp.pad(small["dn_a_log"][0], (0, lane_pad)).reshape(1, DN_AB_PAD)
    dt_row = jnp.pad(small["dn_dt_bias"][0], (0, lane_pad)).reshape(1, DN_AB_PAD)
    gb = _rowwise("l1_gates", _gates_fwd_fn, [ab], [alog_row, dt_row], [(DN_AB_PAD, F32)])
    g_t, b_t = gb[:, :DN_HEADS].T, gb[:, DN_HEADS:2 * DN_HEADS].T
    g_col, b_col = g_t.reshape(DN_HEADS, S, 1), b_t.reshape(DN_HEADS, S, 1)
    g_row = g_t.reshape(DN_HEADS, N, 1, CHUNK)
    ogain = small["dn_o_gain"][0].reshape(1, DN_HEAD_DIM)
    y, s_in_all = _chunk_fwd("l1_delta", pre, proj, g_col, b_col, g_row, ogain)
    x4 = _mm("l1_dn_out", y, big["dn_w_o"], "nn", extras=(x3,), epilogue=lambda acc, r: (acc + r,))
    x6, sv1 = _mlp_ple_fwd("l1", x4, p[1], small["mlp_norm"][1], big["w_up"][1], big["w_down"][1],
                           small["ple_norm"][1], big["w_ple"][1], big["w_ple_gate"][1])

    dy, sq = _rowwise("loss", _loss_fn, [x6, target], [], [(D_MODEL, F32)], [(1, 128)])

    dx4, gl1 = _mlp_ple_bwd("l1", dy, sv1, p[1], small["mlp_norm"][1], big["w_up"][1], big["w_down"][1],
                            small["ple_norm"][1], big["w_ple"][1], big["w_ple_gate"][1])
    d_y = _mm("l1_d_y", dx4, big["dn_w_o"], "nt")
    d_dn_w_o = _mm("l1_d_w_o", y, dx4, "tn")
    dq, dk, dv, dz, dg_col, db_col, dg_row, d_ogain = _chunk_bwd(
        "l1_delta_bwd", pre, proj, g_col, b_col, g_row, ogain, s_in_all, d_y)
    dpre = jnp.concatenate([dq, dk, dv], axis=1)
    dconv_in, d_conv_w = _conv_bwd("l1_conv_bwd", proj, dpre, small["dn_conv"])
    dg_t = dg_col.reshape(DN_HEADS, S) + dg_row.reshape(DN_HEADS, S)
    dgb = jnp.pad(jnp.concatenate([dg_t, db_col.reshape(DN_HEADS, S)], axis=0).T,
                  ((0, 0), (0, DN_AB_PAD - 2 * DN_HEADS)))
    dab, d_alog, d_dt = _rowwise("l1_gates_bwd", _gates_bwd_fn, [ab, dgb], [alog_row, dt_row],
                                 [(DN_AB_PAD, F32)], [(1, DN_AB_PAD), (1, DN_AB_PAD)])
    dproj = jnp.concatenate([dconv_in.astype(BF16), dab.astype(BF16), dz.astype(BF16)], axis=1)
    d_dn_w_in = _mm("l1_d_w_in", h3, dproj, "tn")
    dh3 = _mm("l1_d_h", dproj, big["dn_w_in"], "nt")
    dx3, d_mix1 = _rms_bwd("l1_mix_norm_bwd", x3, small["mix_norm"][1], dx4, [dh3])

    dx1, gl0 = _mlp_ple_bwd("l0", dx3, sv0, p[0], small["mlp_norm"][0], big["w_up"][0], big["w_down"][0],
                            small["ple_norm"][0], big["w_ple"][0], big["w_ple_gate"][0])
    dom = _mm("l0_d_om", dx1, big["attn_w_o"], "nt")
    d_attn_w_o = _mm("l0_d_w_o", om, dx1, "tn")
    merged = _rowwise("l0_merge_bwd", _merge_bwd_fn,
                      [a["o_tok"] for a in attn] + [a["lse_tok"] for a in attn] + [dom], [], [(A_WIDTH, F32)] * 6)
    dh0, d_w_qkv, d_gq, d_gk = [], [], [], []
    for g, a in enumerate(attn):
        do_g, dl_g = _to_classes(merged[g], a["d"]), _to_classes(merged[3 + g], a["d"])
        dqn, dkn, dvn = _attn_bwd(f"l0_attn_bwd{g}", a["q"], a["k"], a["v"], a["o"], a["lse"], do_g, dl_g,
                                  (S // a["d"]) // BAND)
        dqkv, dgq, dgk = _rowwise(f"l0_qk_prep_bwd{g}", _qk_prep_bwd_fn, [a["qkv"], a["ct"], a["st"], dqn, dkn, dvn],
                                  [a["gq"], a["gk"], bd], [(3 * A_WIDTH, BF16)], [(1, A_HEAD_DIM)] * 2)
        d_w_qkv.append(_mm(f"l0_d_w_qkv{g}", a["h0g"], dqkv, "tn"))
        dh0.append(_from_classes(_mm(f"l0_d_h{g}", dqkv, a["w"], "nt"), a["d"]))
        d_gq.append(dgq)
        d_gk.append(dgk)
    grad_x, d_mix0 = _rms_bwd("l0_mix_norm_bwd", x, small["mix_norm"][0], dx1, dh0)

    grads = dict(
        mix_norm=jnp.concatenate([d_mix0, d_mix1], axis=0),
        attn_w_qkv=jnp.concatenate(d_w_qkv, axis=1)[None],
        attn_q_gain=jnp.concatenate(d_gq, axis=0)[None],
        attn_k_gain=jnp.concatenate(d_gk, axis=0)[None],
        attn_w_o=d_attn_w_o[None],
        dn_w_in=jnp.concatenate([d_dn_w_in[:, :DN_QKV + 2 * DN_HEADS], d_dn_w_in[:, DN_QKV + DN_AB_PAD:]], axis=1)[None],
        dn_conv=d_conv_w[None],
        dn_a_log=d_alog[:, :DN_HEADS],
        dn_dt_bias=d_dt[:, :DN_HEADS],
        dn_o_gain=d_ogain,
        dn_w_o=d_dn_w_o[None],
        mlp_norm=jnp.concatenate([gl0["mlp_norm"], gl1["mlp_norm"]], axis=0),
        w_up=jnp.stack([gl0["w_up"], gl1["w_up"]]),
        w_down=jnp.stack([gl0["w_down"], gl1["w_down"]]),
        ple_norm=jnp.concatenate([gl0["ple_norm"], gl1["ple_norm"]], axis=0),
        w_ple=jnp.stack([gl0["w_ple"], gl1["w_ple"]]),
        w_ple_gate=jnp.stack([gl0["w_ple_gate"], gl1["w_ple_gate"]]),
    )
    return sq, grad_x, grads


def _chip_peer(x, y, c, t):
    return (jnp.bitwise_xor(x, t >> 1), jnp.bitwise_xor(y, t & 1), c)


def _gather_from_chips(name, slab):
    R, C = slab.shape

    def body(src, out, send_sems, recv_sems, local_sem):
        x, y, c = lax.axis_index("x"), lax.axis_index("y"), lax.axis_index("c")
        q = 2 * x + y
        local = pltpu.make_async_copy(src, out.at[q], local_sem)
        local.start()
        copies = []
        for t in range(1, N_CHIPS):
            cp = pltpu.make_async_remote_copy(src_ref=src, dst_ref=out.at[q], send_sem=send_sems.at[t - 1],
                                              recv_sem=recv_sems.at[t - 1], device_id=_chip_peer(x, y, c, t),
                                              device_id_type=MESH)
            cp.start()
            copies.append(cp)
        for cp in copies:
            cp.wait()
        local.wait()

    return pl.pallas_call(
        body, name=name, out_shape=jax.ShapeDtypeStruct((N_CHIPS, R, C), slab.dtype),
        in_specs=[pl.BlockSpec(memory_space=pl.ANY)], out_specs=pl.BlockSpec(memory_space=pl.ANY),
        scratch_shapes=[pltpu.SemaphoreType.DMA((N_CHIPS - 1,)), pltpu.SemaphoreType.DMA((N_CHIPS - 1,)),
                        pltpu.SemaphoreType.DMA],
    )(slab)


def _scatter_to_chips(name, slabs):
    _, R, C = slabs.shape

    def body(src, out, send_sems, recv_sems):
        x, y, c = lax.axis_index("x"), lax.axis_index("y"), lax.axis_index("c")
        q = 2 * x + y
        copies = []
        for t in range(1, N_CHIPS):
            cp = pltpu.make_async_remote_copy(src_ref=src.at[jnp.bitwise_xor(q, t)], dst_ref=out.at[t - 1],
                                              send_sem=send_sems.at[t - 1], recv_sem=recv_sems.at[t - 1],
                                              device_id=_chip_peer(x, y, c, t), device_id_type=MESH)
            cp.start()
            copies.append(cp)
        for cp in copies:
            cp.wait()

    return pl.pallas_call(
        body, name=name, out_shape=jax.ShapeDtypeStruct((N_CHIPS - 1, R, C), slabs.dtype),
        in_specs=[pl.BlockSpec(memory_space=pl.ANY)], out_specs=pl.BlockSpec(memory_space=pl.ANY),
        scratch_shapes=[pltpu.SemaphoreType.DMA((N_CHIPS - 1,)), pltpu.SemaphoreType.DMA((N_CHIPS - 1,))],
    )(slabs)


def _swap_with_sibling(name, slab):
    def body(src, out, send_sem, recv_sem):
        x, y, c = lax.axis_index("x"), lax.axis_index("y"), lax.axis_index("c")
        cp = pltpu.make_async_remote_copy(src_ref=src, dst_ref=out, send_sem=send_sem, recv_sem=recv_sem,
                                          device_id=(x, y, 1 - c), device_id_type=MESH)
        cp.start()
        cp.wait()

    return pl.pallas_call(
        body, name=name, out_shape=jax.ShapeDtypeStruct(slab.shape, slab.dtype),
        in_specs=[pl.BlockSpec(memory_space=pl.ANY)], out_specs=pl.BlockSpec(memory_space=pl.ANY),
        scratch_shapes=[pltpu.SemaphoreType.DMA, pltpu.SemaphoreType.DMA],
    )(slab)


def _gather_from_all(name, block):
    R, C = block.shape

    def body(src, out, send_sems, recv_sems):
        x, y, c = lax.axis_index("x"), lax.axis_index("y"), lax.axis_index("c")
        me = 4 * x + 2 * y + c
        out[me] = src[...]
        copies = []
        for r in range(1, N_DEV):
            peer = (jnp.bitwise_xor(x, r >> 2), jnp.bitwise_xor(y, (r >> 1) & 1), jnp.bitwise_xor(c, r & 1))
            cp = pltpu.make_async_remote_copy(src_ref=src, dst_ref=out.at[me], send_sem=send_sems.at[r - 1],
                                              recv_sem=recv_sems.at[r - 1], device_id=peer, device_id_type=MESH)
            cp.start()
            copies.append(cp)
        for cp in copies:
            cp.wait()

    return pl.pallas_call(
        body, name=name, out_shape=jax.ShapeDtypeStruct((N_DEV, R, C), block.dtype),
        in_specs=[pl.BlockSpec(memory_space=pltpu.VMEM)], out_specs=pl.BlockSpec(memory_space=pltpu.VMEM),
        scratch_shapes=[pltpu.SemaphoreType.DMA((N_DEV - 1,)), pltpu.SemaphoreType.DMA((N_DEV - 1,))],
    )(block)


def _pack_rows(n_elements):
    rows = -(-n_elements // PACK_WIDTH)
    return -(-rows // PACK_ROW_TILE) * PACK_ROW_TILE


def _pack(parts, dtype):
    flat = jnp.concatenate([a.reshape(-1).astype(dtype) for a in parts])
    rows = _pack_rows(flat.shape[0])
    return jnp.pad(flat, (0, rows * PACK_WIDTH - flat.shape[0])).reshape(rows, PACK_WIDTH)


def _unpack(slab, shapes):
    lead = slab.shape[:-2]
    flat = slab.reshape(lead + (-1,))
    out, off = [], 0
    for shp in shapes:
        n = math.prod(shp)
        out.append(flat[..., off:off + n].reshape(lead + tuple(shp)))
        off += n
    return out


SMALL_ROWS = 8


def _pack_small(vals):
    tail = jnp.concatenate([vals["attn_q_gain"].reshape(-1), vals["attn_k_gain"].reshape(-1),
                            vals["dn_a_log"].reshape(-1), vals["dn_dt_bias"].reshape(-1),
                            vals["dn_o_gain"].reshape(-1)])
    tail = jnp.pad(tail, (0, D_MODEL - tail.shape[0])).reshape(1, D_MODEL)
    return jnp.concatenate([vals["mix_norm"], vals["mlp_norm"], vals["ple_norm"], tail,
                            jnp.zeros((1, D_MODEL), F32)], axis=0)


def _unpack_small(block):
    nq = 3 * A_HEAD_DIM
    t = block[6]
    return dict(
        mix_norm=block[0:2], mlp_norm=block[2:4], ple_norm=block[4:6],
        attn_q_gain=t[:nq].reshape(1, 3, A_HEAD_DIM), attn_k_gain=t[nq:2 * nq].reshape(1, 3, A_HEAD_DIM),
        dn_a_log=t[2 * nq:2 * nq + DN_HEADS].reshape(1, DN_HEADS),
        dn_dt_bias=t[2 * nq + DN_HEADS:2 * nq + 2 * DN_HEADS].reshape(1, DN_HEADS),
        dn_o_gain=t[2 * nq + 2 * DN_HEADS:2 * nq + 2 * DN_HEADS + DN_HEAD_DIM].reshape(1, DN_HEAD_DIM))


def kernel(x, p, positions, mix_norm, attn_w_qkv, attn_q_gain, attn_k_gain, attn_w_o, dn_w_in, dn_conv, dn_a_log, dn_dt_bias, dn_o_gain, dn_w_o, mlp_norm, w_up, w_down, ple_norm, w_ple, w_ple_gate, loss_target, m_mix_norm, m_attn_w_qkv, m_attn_q_gain, m_attn_k_gain, m_attn_w_o, m_dn_w_in, m_dn_conv, m_dn_a_log, m_dn_dt_bias, m_dn_o_gain, m_dn_w_o, m_mlp_norm, m_w_up, m_w_down, m_ple_norm, m_w_ple, m_w_ple_gate, v_mix_norm, v_attn_w_qkv, v_attn_q_gain, v_attn_k_gain, v_attn_w_o, v_dn_w_in, v_dn_conv, v_dn_a_log, v_dn_dt_bias, v_dn_o_gain, v_dn_w_o, v_mlp_norm, v_w_up, v_w_down, v_ple_norm, v_w_ple, v_w_ple_gate):
    given = dict(locals())
    w = {n: given[n] for n in WEIGHTS}
    m = {n: given["m_" + n] for n in WEIGHTS}
    v = {n: given["v_" + n] for n in WEIGHTS}
    shard_shapes = [w[n].shape for n, _ in SHARDED]

    gathered = _gather_from_chips("gather_weights", _pack([w[n] for n, _ in SHARDED], BF16))
    full = {}
    for (n, axis), parts in zip(SHARDED, _unpack(gathered, shard_shapes)):
        full[n] = jnp.concatenate([parts[q] for q in range(N_CHIPS)], axis=axis)
    conv_block = jnp.pad(w["dn_conv"].reshape(-1), (0, SMALL_ROWS * D_MODEL - w["dn_conv"].size))
    conv_all = _gather_from_all("gather_conv", conv_block.reshape(SMALL_ROWS, D_MODEL))
    conv_all = conv_all.reshape(N_CHIPS, 2, -1)[:, 0, :w["dn_conv"].size]
    conv_full = jnp.concatenate([conv_all[q].reshape(CONV_WIDTH, -1) for q in range(N_CHIPS)], axis=1)

    w_in = full["dn_w_in"][0]
    n_ab = 2 * DN_HEADS
    w_in_pad = jnp.concatenate([w_in[:, :DN_QKV + n_ab], jnp.zeros((D_MODEL, DN_AB_PAD - n_ab), BF16),
                                w_in[:, DN_QKV + n_ab:]], axis=1)
    big = dict(attn_w_qkv=full["attn_w_qkv"][0], attn_w_o=full["attn_w_o"][0], dn_w_in=w_in_pad,
               dn_w_o=full["dn_w_o"][0], w_up=full["w_up"], w_down=full["w_down"], w_ple=full["w_ple"],
               w_ple_gate=full["w_ple_gate"])
    small = {n: w[n] for n in REPLICATED}
    small["dn_conv"] = conv_full

    sq, grad_x, grads = _local_step(x[0], p[:, 0], positions[0], loss_target[0], small, big)
    loss = lax.psum(0.5 * sq[0, 0] / D_MODEL, ("x", "y", "c"))

    per_chip = []
    for q in range(N_CHIPS):
        per_chip.append(_pack([jnp.split(grads[n], N_CHIPS, axis=axis)[q] for n, axis in SHARDED], BF16))
    send = jnp.stack(per_chip)
    recv = _scatter_to_chips("scatter_grads", send)
    x_idx, y_idx = lax.axis_index("x"), lax.axis_index("y")
    mine = lax.dynamic_index_in_dim(send, 2 * x_idx + y_idx, axis=0, keepdims=False)
    partial = _rowwise("sum_chip_grads", lambda a, b, c, d: a.astype(F32) + b + c + d,
                       [mine, recv[0], recv[1], recv[2]], [], [(PACK_WIDTH, F32)])
    other = _swap_with_sibling("swap_core_sums", partial)
    packed = [_pack([t[n] for n, _ in SHARDED], F32) for t in (w, m, v)]
    g_s, delta_s, m_s, v_s = _rowwise("adamw_sharded", _adamw_pair_fn, [partial, other] + packed, [],
                                      [(PACK_WIDTH, F32)] * 4)
    out = {}
    for kind, slab in (("grad", g_s), ("delta", delta_s), ("new_m", m_s), ("new_v", v_s)):
        for (n, _), arr in zip(SHARDED, _unpack(slab, shard_shapes)):
            out[kind + "_" + n] = arr

    slots = _gather_from_all("gather_small_grads", _pack_small(grads))

    def small_body(s_ref, w_ref, m_ref, v_ref, g_out, d_out, m_out, v_out):
        res = _adamw_slots_fn(s_ref[...], w_ref[...], m_ref[...], v_ref[...])
        for o, r in zip((g_out, d_out, m_out, v_out), res):
            o[...] = r

    res = pl.pallas_call(small_body, name="adamw_replicated",
                         out_shape=[jax.ShapeDtypeStruct((SMALL_ROWS, D_MODEL), F32)] * 4)(
        slots, _pack_small(w), _pack_small(m), _pack_small(v))
    for kind, block in zip(("grad", "delta", "new_m", "new_v"), res):
        for n, arr in _unpack_small(block).items():
            out[kind + "_" + n] = arr

    return (loss, grad_x[None],
            *[out["grad_" + n] for n in WEIGHTS], *[out["delta_" + n] for n in WEIGHTS],
            *[out["new_m_" + n] for n in WEIGHTS], *[out["new_v_" + n] for n in WEIGHTS])
```

```python
import functools
import math

import jax
import jax.numpy as jnp
from jax import lax
from jax.experimental import pallas as pl
from jax.experimental.pallas import tpu as pltpu

F32 = jnp.float32
BF16 = jnp.bfloat16
HIGHEST = lax.Precision.HIGHEST

D_MODEL = 1024
EPS = 1e-6
SWA_GROUPS = ((128, 1), (512, 4), (2048, 16))
A_HEADS = 8
A_HEAD_DIM = 64
A_WIDTH = A_HEADS * A_HEAD_DIM
ROPE_DIM = A_HEAD_DIM // 4
ROPE_THETA = 500000.0
BAND = 128
DN_HEADS = 8
DN_HEAD_DIM = 128
DN_WIDTH = DN_HEADS * DN_HEAD_DIM
DN_QKV = 3 * DN_WIDTH
DN_AB_PAD = 128
DN_IN_PAD = DN_WIDTH + DN_QKV + DN_AB_PAD
DN_QKV0 = DN_WIDTH
DN_AB0 = DN_WIDTH + DN_QKV
DN_HB = 8
CONV_WIDTH = 4
CHUNK = 64
PLE_DIM = 256
D_FF = 4 * D_MODEL

ADAM_LR = 0.001
ADAM_B1 = 0.9
ADAM_B2 = 0.999
ADAM_EPS = 1e-08
ADAM_WD = 0.01
ADAM_STEP = 10

N_CHIPS = 4
N_DEV = 8
PACK_WIDTH = 1024
PACK_ROW_TILE = 256
VMEM_LIMIT = 48 * 1024 * 1024
MESH = pl.DeviceIdType.MESH

SHARDED = (
    ("attn_w_qkv", 2), ("attn_w_o", 2), ("dn_w_in", 2), ("dn_conv", 2), ("dn_w_o", 1),
    ("w_up", 2), ("w_down", 1), ("w_ple", 2), ("w_ple_gate", 1))
REPLICATED = ("mix_norm", "attn_q_gain", "attn_k_gain", "dn_a_log", "dn_dt_bias", "dn_o_gain",
              "mlp_norm", "ple_norm")
WEIGHTS = ("mix_norm", "attn_w_qkv", "attn_q_gain", "attn_k_gain", "attn_w_o", "dn_w_in", "dn_conv",
           "dn_a_log", "dn_dt_bias", "dn_o_gain", "dn_w_o", "mlp_norm", "w_up", "w_down", "ple_norm",
           "w_ple", "w_ple_gate")


def _cparams(sem=None):
    return pltpu.CompilerParams(dimension_semantics=sem, vmem_limit_bytes=VMEM_LIMIT)


def _pick(n, cap, quantum=128):
    best = None
    for t in range(quantum, min(n, cap) + 1, quantum):
        if n % t == 0:
            best = t
    return n if best is None else best


_DIMS = {"nn": ((1,), (0,)), "nt": ((1,), (1,)), "tn": ((0,), (0,))}


def _mm(name, a, b, mode, out_dtypes=(F32,), extras=(), epilogue=None):
    if mode == "nn":
        (M, K), (K2, N) = a.shape, b.shape
    elif mode == "nt":
        (M, K), (N, K2) = a.shape, b.shape
    else:
        (K, M), (K2, N) = a.shape, b.shape
    assert K == K2, (name, a.shape, b.shape)
    tm, tn, tk = _pick(M, 512), _pick(N, 1536), _pick(K, 1024 if mode == "tn" else 1536)
    nk = K // tk
    if mode == "nn":
        a_spec = pl.BlockSpec((tm, tk), lambda i, j, k: (i, k))
        b_spec = pl.BlockSpec((tk, tn), lambda i, j, k: (k, j))
    elif mode == "nt":
        a_spec = pl.BlockSpec((tm, tk), lambda i, j, k: (i, k))
        b_spec = pl.BlockSpec((tn, tk), lambda i, j, k: (j, k))
    else:
        a_spec = pl.BlockSpec((tk, tm), lambda i, j, k: (k, i))
        b_spec = pl.BlockSpec((tk, tn), lambda i, j, k: (k, j))
    o_spec = pl.BlockSpec((tm, tn), lambda i, j, k: (i, j))
    n_extra, n_out = len(extras), len(out_dtypes)
    dims = (_DIMS[mode], ((), ()))

    def body(a_ref, b_ref, *rest):
        extra_refs, out_refs, acc = rest[:n_extra], rest[n_extra:n_extra + n_out], rest[-1]
        k = pl.program_id(2)

        @pl.when(k == 0)
        def _():
            acc[...] = jnp.zeros_like(acc)

        acc[...] += lax.dot_general(a_ref[...].astype(BF16), b_ref[...].astype(BF16), dims,
                                    preferred_element_type=F32)

        @pl.when(k == nk - 1)
        def _():
            if epilogue is None:
                vals = (acc[...],)
            else:
                vals = epilogue(acc[...], *[e[...] for e in extra_refs])
            for o, v in zip(out_refs, vals):
                o[...] = v.astype(o.dtype)

    outs = pl.pallas_call(
        body, name=name, grid=(M // tm, N // tn, nk),
        in_specs=[a_spec, b_spec] + [o_spec] * n_extra,
        out_specs=[o_spec] * n_out,
        out_shape=[jax.ShapeDtypeStruct((M, N), dt) for dt in out_dtypes],
        scratch_shapes=[pltpu.VMEM((tm, tn), F32)],
        compiler_params=_cparams(("parallel", "parallel", "arbitrary")),
    )(a, b, *extras)
    return outs[0] if n_out == 1 else outs


def _rowwise(name, fn, rows, bcast, row_outs, acc_outs=(), tm=256):
    rows = [r if isinstance(r, tuple) else (r, r.shape[1], 0) for r in rows]
    S = rows[0][0].shape[0]
    tm = min(tm, S)
    assert S % tm == 0, (name, S, tm)
    n_row, n_bc, n_ro, n_acc = len(rows), len(bcast), len(row_outs), len(acc_outs)
    in_specs = [pl.BlockSpec((tm, w), functools.partial(lambda i, cb: (i, cb), cb=cb)) for _, w, cb in rows]
    in_specs += [pl.BlockSpec(b.shape, lambda i: (0, 0)) for b in bcast]
    out_specs = [pl.BlockSpec((tm, c), lambda i: (i, 0)) for c, _ in row_outs]
    out_specs += [pl.BlockSpec(s, lambda i: (0, 0)) for s in acc_outs]
    out_shape = [jax.ShapeDtypeStruct((S, c), dt) for c, dt in row_outs]
    out_shape += [jax.ShapeDtypeStruct(s, F32) for s in acc_outs]

    def body(*refs):
        ins = [r[...] for r in refs[:n_row + n_bc]]
        outs = refs[n_row + n_bc:]
        vals = fn(*ins)
        if not isinstance(vals, (tuple, list)):
            vals = (vals,)
        for o, v in zip(outs[:n_ro], vals[:n_ro]):
            o[...] = v.astype(o.dtype)
        if n_acc:
            @pl.when(pl.program_id(0) == 0)
            def _():
                for o in outs[n_ro:]:
                    o[...] = jnp.zeros_like(o)
            for o, v in zip(outs[n_ro:], vals[n_ro:]):
                o[...] += v

    outs = pl.pallas_call(
        body, name=name, grid=(S // tm,), in_specs=in_specs, out_specs=out_specs, out_shape=out_shape,
        compiler_params=_cparams(("arbitrary",) if n_acc else ("parallel",)),
    )(*[r[0] for r in rows], *bcast)
    return outs[0] if len(outs) == 1 else outs


def _sigmoid(x):
    return 1.0 / (1.0 + jnp.exp(-x))


def _silu(x):
    return x * _sigmoid(x)


def _softplus(x):
    return jnp.maximum(x, 0.0) + jnp.log(1.0 + jnp.exp(-jnp.abs(x)))


def _rms_fwd_fn(x, g):
    r = lax.rsqrt(jnp.mean(x * x, axis=-1, keepdims=True) + EPS)
    return (x * r) * g


def _rms_bwd_fn(x, dres, *rest):
    dh, g = sum(rest[:-1]), rest[-1]
    r = lax.rsqrt(jnp.mean(x * x, axis=-1, keepdims=True) + EPS)
    xh = x * r
    dxh = dh * g
    dx = r * (dxh - xh * jnp.mean(dxh * xh, axis=-1, keepdims=True))
    return dres + dx, jnp.sum(dh * xh, axis=0, keepdims=True)


def _rms_fwd(name, x, gain):
    return _rowwise(name, _rms_fwd_fn, [x], [gain.reshape(1, -1)], [(x.shape[1], BF16)])


def _rms_bwd(name, x, gain, dres, dhs):
    return _rowwise(name, _rms_bwd_fn, [x, dres] + list(dhs), [gain.reshape(1, -1)],
                    [(x.shape[1], F32)], [(1, x.shape[1])])


def _relu2_epilogue(acc):
    r = jnp.maximum(acc, 0.0)
    return acc, r * r


def _relu2_bwd_epilogue(acc, u):
    return (acc * (2.0 * jnp.maximum(u, 0.0)),)


def _ple_fwd_fn(x, pp, zg):
    return x + pp * _sigmoid(zg)


def _ple_bwd_fn(dx, pp, zg):
    gate = _sigmoid(zg)
    return dx * gate, dx * pp * gate * (1.0 - gate)


def _loss_fn(y, t):
    err = y - t
    return err * (1.0 / D_MODEL), jnp.broadcast_to(jnp.sum(err * err, keepdims=True), (1, 128))


def _adamw(w, g, m, v):
    m = ADAM_B1 * m + (1.0 - ADAM_B1) * g
    v = ADAM_B2 * v + (1.0 - ADAM_B2) * jnp.square(g)
    m_hat = m / (1.0 - ADAM_B1 ** ADAM_STEP)
    v_hat = v / (1.0 - ADAM_B2 ** ADAM_STEP)
    delta = -ADAM_LR * (m_hat / (jnp.sqrt(v_hat) + ADAM_EPS) + ADAM_WD * w)
    return delta, m, v


def _adamw_pair_fn(ga, gb, w, m, v):
    g = ga + gb
    return (g,) + _adamw(w, g, m, v)


def _adamw_slots_fn(slots, w, m, v):
    g = slots[0]
    for d in range(1, N_DEV):
        g = g + slots[d]
    return (g,) + _adamw(w, g, m, v)


def _lane_take(x, offset):
    n = x.shape[-1]
    return pltpu.roll(x, (-offset) % n, 1)


def _head_lane(shape):
    return lax.broadcasted_iota(jnp.int32, shape, 1) % A_HEAD_DIM


def _rope_partner(x):
    lane = _head_lane(x.shape)
    return jnp.where(lane < ROPE_DIM // 2, _lane_take(x, ROPE_DIM // 2),
                     jnp.where(lane < ROPE_DIM, _lane_take(x, -(ROPE_DIM // 2)), 0.0))


def _head_mean(x, bd):
    return jnp.dot(x, bd, precision=HIGHEST, preferred_element_type=F32)


def _fold_heads(row):
    out = row[:, :A_HEAD_DIM]
    for h in range(1, A_HEADS):
        out = out + row[:, h * A_HEAD_DIM:(h + 1) * A_HEAD_DIM]
    return out


def _qk_prep_fwd_fn(qkv, ct, st, gq, gk, bd):
    def one(t, g):
        n = t * lax.rsqrt(_head_mean(t * t, bd) + EPS) * g
        return n * ct + _rope_partner(n) * st
    q, k, v = qkv[:, :A_WIDTH], qkv[:, A_WIDTH:2 * A_WIDTH], qkv[:, 2 * A_WIDTH:]
    return one(q, gq), one(k, gk), v


def _qk_prep_bwd_fn(qkv, ct, st, dq, dk, dv, gq, gk, bd):
    def one(t, g, dy):
        r = lax.rsqrt(_head_mean(t * t, bd) + EPS)
        nh = t * r
        dn = dy * ct + _rope_partner(dy * st)
        dg = jnp.sum(dn * nh, axis=0, keepdims=True)
        dnh = dn * g
        return r * (dnh - nh * _head_mean(dnh * nh, bd)), _fold_heads(dg)
    q, k = qkv[:, :A_WIDTH], qkv[:, A_WIDTH:2 * A_WIDTH]
    dq_raw, dgq = one(q, gq, dq)
    dk_raw, dgk = one(k, gk, dk)
    return jnp.concatenate([dq_raw, dk_raw, dv], axis=1), dgq, dgk


def _band_masks():
    qi = lax.broadcasted_iota(jnp.int32, (BAND, BAND), 0)
    kj = lax.broadcasted_iota(jnp.int32, (BAND, BAND), 1)
    return kj <= qi, kj >= qi


def _attn_fwd(name, q, k, v, blocks_per_class):
    S = q.shape[0]
    nblk = S // BAND
    scale = A_HEAD_DIM ** -0.5

    def body(q_ref, kp_ref, kc_ref, vp_ref, vc_ref, o_ref, l_ref):
        i = pl.program_id(0)
        has_prev = (i % blocks_per_class) != 0
        m_cur, m_prev = _band_masks()
        m_prev = jnp.logical_and(m_prev, has_prev)
        for h in range(A_HEADS):
            sl = slice(h * A_HEAD_DIM, (h + 1) * A_HEAD_DIM)
            qh = q_ref[:, sl]
            s_c = lax.dot_general(qh, kc_ref[:, sl], (_DIMS["nt"], ((), ())), preferred_element_type=F32) * scale
            s_p = lax.dot_general(qh, kp_ref[:, sl], (_DIMS["nt"], ((), ())), preferred_element_type=F32) * scale
            s_c = jnp.where(m_cur, s_c, -jnp.inf)
            s_p = jnp.where(m_prev, s_p, -jnp.inf)
            m = jnp.maximum(jnp.max(s_c, axis=-1, keepdims=True), jnp.max(s_p, axis=-1, keepdims=True))
            e_c, e_p = jnp.exp(s_c - m), jnp.exp(s_p - m)
            l = jnp.sum(e_c, axis=-1, keepdims=True) + jnp.sum(e_p, axis=-1, keepdims=True)
            o = jnp.dot((e_c / l).astype(BF16), vc_ref[:, sl], preferred_element_type=F32)
            o = o + jnp.dot((e_p / l).astype(BF16), vp_ref[:, sl], preferred_element_type=F32)
            o_ref[:, sl] = o
            l_ref[:, sl] = jnp.broadcast_to(m + jnp.log(l), (BAND, A_HEAD_DIM))

    cur = pl.BlockSpec((BAND, A_WIDTH), lambda i: (i, 0))
    prev = pl.BlockSpec((BAND, A_WIDTH), lambda i: (jnp.maximum(i - 1, 0), 0))
    return pl.pallas_call(
        body, name=name, grid=(nblk,), in_specs=[cur, prev, cur, prev, cur], out_specs=[cur, cur],
        out_shape=[jax.ShapeDtypeStruct((S, A_WIDTH), F32)] * 2,
        compiler_params=_cparams(("parallel",)),
    )(q, k, k, v, v)


def _attn_bwd(name, q, k, v, o, lse, do, dlse, blocks_per_class):
    S = q.shape[0]
    nblk = S // BAND
    scale = A_HEAD_DIM ** -0.5

    def body(q_ref, kp_ref, kc_ref, vp_ref, vc_ref, o_ref, l_ref, do_ref, dl_ref,
             dq_ref, dk_ref, dv_ref, ck, cv):
        i = pl.program_id(0)

        @pl.when(i == 0)
        def _():
            ck[...] = jnp.zeros_like(ck)
            cv[...] = jnp.zeros_like(cv)

        @pl.when(i == nblk)
        def _():
            dk_ref[...] = ck[...]
            dv_ref[...] = cv[...]

        @pl.when(i < nblk)
        def _():
            has_prev = (i % blocks_per_class) != 0
            m_cur, m_prev = _band_masks()
            m_prev = jnp.logical_and(m_prev, has_prev)
            nt, tn = (_DIMS["nt"], ((), ())), (_DIMS["tn"], ((), ()))
            for h in range(A_HEADS):
                sl = slice(h * A_HEAD_DIM, (h + 1) * A_HEAD_DIM)
                qh, kc, kp, vc, vp = q_ref[:, sl], kc_ref[:, sl], kp_ref[:, sl], vc_ref[:, sl], vp_ref[:, sl]
                doh = do_ref[:, sl]
                lse_h = l_ref[:, sl][:, :1]
                s_c = lax.dot_general(qh, kc, nt, preferred_element_type=F32) * scale
                s_p = lax.dot_general(qh, kp, nt, preferred_element_type=F32) * scale
                p_c = jnp.where(m_cur, jnp.exp(s_c - lse_h), 0.0)
                p_p = jnp.where(m_prev, jnp.exp(s_p - lse_h), 0.0)
                corr = (jnp.sum(dl_ref[:, sl], axis=-1, keepdims=True)
                        - jnp.sum(doh * o_ref[:, sl], axis=-1, keepdims=True))
                dob = doh.astype(BF16)
                dp_c = lax.dot_general(dob, vc, nt, preferred_element_type=F32)
                dp_p = lax.dot_general(dob, vp, nt, preferred_element_type=F32)
                ds_c = (p_c * (dp_c + corr)).astype(BF16)
                ds_p = (p_p * (dp_p + corr)).astype(BF16)
                dq = jnp.dot(ds_c, kc, preferred_element_type=F32) + jnp.dot(ds_p, kp, preferred_element_type=F32)
                dq_ref[:, sl] = dq * scale
                dk_ref[:, sl] = ck[:, sl] + lax.dot_general(ds_p, qh, tn, preferred_element_type=F32) * scale
                dv_ref[:, sl] = cv[:, sl] + lax.dot_general(p_p.astype(BF16), dob, tn, preferred_element_type=F32)
                ck[:, sl] = lax.dot_general(ds_c, qh, tn, preferred_element_type=F32) * scale
                cv[:, sl] = lax.dot_general(p_c.astype(BF16), dob, tn, preferred_element_type=F32)

    last = nblk - 1
    cur = pl.BlockSpec((BAND, A_WIDTH), lambda i: (jnp.minimum(i, last), 0))
    prev = pl.BlockSpec((BAND, A_WIDTH), lambda i: (jnp.minimum(jnp.maximum(i - 1, 0), last), 0))
    return pl.pallas_call(
        body, name=name, grid=(nblk + 1,),
        in_specs=[cur, prev, cur, prev, cur, cur, cur, cur, cur], out_specs=[cur, prev, prev],
        out_shape=[jax.ShapeDtypeStruct((S, A_WIDTH), F32)] * 3,
        scratch_shapes=[pltpu.VMEM((BAND, A_WIDTH), F32)] * 2,
        compiler_params=_cparams(("arbitrary",)),
    )(q, k, k, v, v, o, lse, do, dlse)


def _merge_fwd_fn(o0, o1, o2, l0, l1, l2):
    m = jnp.maximum(jnp.maximum(l0, l1), l2)
    e0, e1, e2 = jnp.exp(l0 - m), jnp.exp(l1 - m), jnp.exp(l2 - m)
    return (e0 * o0 + e1 * o1 + e2 * o2) / (e0 + e1 + e2)


def _merge_bwd_fn(o0, o1, o2, l0, l1, l2, dom):
    m = jnp.maximum(jnp.maximum(l0, l1), l2)
    e0, e1, e2 = jnp.exp(l0 - m), jnp.exp(l1 - m), jnp.exp(l2 - m)
    den = e0 + e1 + e2
    w0, w1, w2 = e0 / den, e1 / den, e2 / den
    dw0, dw1, dw2 = dom * o0, dom * o1, dom * o2
    mean = w0 * dw0 + w1 * dw1 + w2 * dw2
    return w0 * dom, w1 * dom, w2 * dom, w0 * (dw0 - mean), w1 * (dw1 - mean), w2 * (dw2 - mean)


def _to_classes(t, d):
    if d == 1:
        return t
    S, C = t.shape
    return t.reshape(S // d, d, C).transpose(1, 0, 2).reshape(S, C)


def _from_classes(t, d):
    if d == 1:
        return t
    S, C = t.shape
    return t.reshape(d, S // d, C).transpose(1, 0, 2).reshape(S, C)


def _rope_lane_tables(positions):
    inv_freq = ROPE_THETA ** (-jnp.arange(0, ROPE_DIM, 2, dtype=F32) / ROPE_DIM)
    ang = positions.astype(F32)[:, None] * inv_freq
    cos, sin = jnp.cos(ang), jnp.sin(ang)
    S = positions.shape[0]
    rest = A_HEAD_DIM - ROPE_DIM
    ct = jnp.concatenate([cos, cos, jnp.ones((S, rest), F32)], axis=1)
    st = jnp.concatenate([-sin, sin, jnp.zeros((S, rest), F32)], axis=1)
    return jnp.tile(ct, (1, A_HEADS)), jnp.tile(st, (1, A_HEADS))


def _head_mean_matrix():
    r = jnp.arange(A_WIDTH) // A_HEAD_DIM
    return (r[:, None] == r[None, :]).astype(F32) * (1.0 / A_HEAD_DIM)


def _conv_fwd(name, proj, w):
    S = proj.shape[0]
    tm, tc = min(512, S), 1024
    per8 = tm // 8
    off = DN_QKV0 // tc

    def body(x_ref, halo_ref, w_ref, o_ref, xs):
        i = pl.program_id(0)
        xs[0:8, :] = jnp.where(i > 0, halo_ref[...], 0.0)
        xs[8:, :] = x_ref[...]
        acc = w_ref[0:1, :] * xs[pl.ds(8 - 3, tm), :]
        for j in range(1, CONV_WIDTH):
            acc = acc + w_ref[j:j + 1, :] * xs[pl.ds(8 - 3 + j, tm), :]
        o_ref[...] = acc

    return pl.pallas_call(
        body, name=name, grid=(S // tm, DN_QKV // tc),
        in_specs=[pl.BlockSpec((tm, tc), lambda i, j: (i, j + off)),
                  pl.BlockSpec((8, tc), lambda i, j: (jnp.maximum(i * per8 - 1, 0), j + off)),
                  pl.BlockSpec((CONV_WIDTH, tc), lambda i, j: (0, j))],
        out_specs=pl.BlockSpec((tm, tc), lambda i, j: (i, j)),
        out_shape=jax.ShapeDtypeStruct((S, DN_QKV), F32),
        scratch_shapes=[pltpu.VMEM((tm + 8, tc), F32)],
        compiler_params=_cparams(("parallel", "parallel")),
    )(proj, proj, w)


def _conv_bwd(name, proj, dpre, w):
    S = proj.shape[0]
    tm, tc = min(512, S), 1024
    per8 = tm // 8
    off = DN_QKV0 // tc
    last8 = S // 8 - 1
    nrow = S // tm

    def body(x_ref, xh_ref, d_ref, dh_ref, w_ref, dx_ref, dw_ref, xs, ds):
        i = pl.program_id(1)
        xs[0:8, :] = jnp.where(i > 0, xh_ref[...], 0.0)
        xs[8:, :] = x_ref[...]
        ds[0:tm, :] = d_ref[...]
        ds[tm:, :] = jnp.where(i < nrow - 1, dh_ref[...], 0.0)
        d = d_ref[...]
        acc = w_ref[0:1, :] * ds[pl.ds(3, tm), :]
        for j in range(1, CONV_WIDTH):
            acc = acc + w_ref[j:j + 1, :] * ds[pl.ds(3 - j, tm), :]
        dx_ref[...] = acc

        @pl.when(i == 0)
        def _():
            dw_ref[...] = jnp.zeros_like(dw_ref)

        for j in range(CONV_WIDTH):
            dw_ref[j:j + 1, :] += jnp.sum(d * xs[pl.ds(8 - 3 + j, tm), :], axis=0, keepdims=True)

    return pl.pallas_call(
        body, name=name, grid=(DN_QKV // tc, nrow),
        in_specs=[pl.BlockSpec((tm, tc), lambda j, i: (i, j + off)),
                  pl.BlockSpec((8, tc), lambda j, i: (jnp.maximum(i * per8 - 1, 0), j + off)),
                  pl.BlockSpec((tm, tc), lambda j, i: (i, j)),
                  pl.BlockSpec((8, tc), lambda j, i: (jnp.minimum((i + 1) * per8, last8), j)),
                  pl.BlockSpec((CONV_WIDTH, tc), lambda j, i: (0, j))],
        out_specs=[pl.BlockSpec((tm, tc), lambda j, i: (i, j)),
                   pl.BlockSpec((CONV_WIDTH, tc), lambda j, i: (0, j))],
        out_shape=[jax.ShapeDtypeStruct((S, DN_QKV), F32), jax.ShapeDtypeStruct((CONV_WIDTH, DN_QKV), F32)],
        scratch_shapes=[pltpu.VMEM((tm + 8, tc), F32)] * 2,
        compiler_params=_cparams(("parallel", "arbitrary")),
    )(proj, proj, dpre, dpre, w)


def _gate_lane(shape):
    return lax.broadcasted_iota(jnp.int32, shape, 1)


GATES_ROWS = 256


def _chunk_cumsum_matrix():
    r = jnp.arange(GATES_ROWS)
    return ((r[:, None] >= r[None, :]) & (r[:, None] // CHUNK == r[None, :] // CHUNK)).astype(F32)


def _gates_fwd_fn(ab, alog, dt, cum):
    g = -jnp.exp(alog) * _softplus(ab + dt)
    gc = jnp.dot(cum, g, precision=HIGHEST, preferred_element_type=F32)
    return jnp.where(_gate_lane(ab.shape) < DN_HEADS, gc, _sigmoid(ab))


def _gates_bwd_fn(ab, dgb, alog, dt, cum):
    lane = _gate_lane(ab.shape)
    is_g = lane < DN_HEADS
    neg_a = -jnp.exp(alog)
    sp = _softplus(ab + dt)
    dsp = _sigmoid(ab + dt)
    beta = _sigmoid(ab)
    dgc = jnp.where(is_g, dgb, 0.0)
    dg = lax.dot_general(cum, dgc, (_DIMS["tn"], ((), ())), precision=HIGHEST, preferred_element_type=F32)
    dab = jnp.where(is_g, dg * neg_a * dsp, jnp.where(lane < 2 * DN_HEADS, dgb * beta * (1.0 - beta), 0.0))
    d_alog = jnp.sum(dg * neg_a * sp, axis=0, keepdims=True)
    d_dt = jnp.sum(dg * neg_a * dsp, axis=0, keepdims=True)
    return dab, d_alog, d_dt


_BATCH_DIMS = {"nn": ((2,), (1,)), "nt": ((2,), (2,)), "tn": ((1,), (1,))}


def _chunk_math(precision):
    def dg(a, b, mode, prec=precision):
        return lax.dot_general(a, b, (_BATCH_DIMS[mode], ((0,), (0,))), precision=prec,
                               preferred_element_type=F32)

    @jax.custom_vjp
    def nn(a, b):
        return dg(a, b, "nn")

    @jax.custom_vjp
    def nt(a, b):
        return dg(a, b, "nt")

    @jax.custom_vjp
    def tn(a, b):
        return dg(a, b, "tn")

    nn.defvjp(lambda a, b: (nn(a, b), (a, b)), lambda r, g: (nt(g, r[1]), tn(r[0], g)))
    nt.defvjp(lambda a, b: (nt(a, b), (a, b)), lambda r, g: (nn(g, r[1]), tn(g, r[0])))
    tn.defvjp(lambda a, b: (tn(a, b), (a, b)), lambda r, g: (nt(r[1], g), nn(r[0], g)))

    def split(x):
        hi = x.astype(BF16)
        return hi, (x - hi.astype(F32)).astype(BF16)

    def fine(a, b, mode):
        ah, al = split(a)
        bh, bl = split(b)
        return dg(ah, bh, mode, None) + (dg(ah, bl, mode, None) + dg(al, bh, mode, None))

    def unit_lower_inverse(a):
        row = lax.broadcasted_iota(jnp.int32, a.shape, 1)
        col = lax.broadcasted_iota(jnp.int32, a.shape, 2)
        x = -a
        p = jnp.where(row == col, 1.0, 0.0) + x
        for _ in range(int(math.log2(CHUNK)) - 1):
            x = fine(x, x, "nn")
            p = p + fine(p, x, "nn")
        return p

    @jax.custom_vjp
    def solve2(a, r1, r2):
        ti = unit_lower_inverse(a)
        return fine(ti, r1, "nn"), fine(ti, r2, "nn")

    def solve2_fwd(a, r1, r2):
        ti = unit_lower_inverse(a)
        s1, s2 = fine(ti, r1, "nn"), fine(ti, r2, "nn")
        return (s1, s2), (ti, s1, s2)

    def solve2_bwd(res, g):
        ti, s1, s2 = res
        d1, d2 = fine(ti, g[0], "tn"), fine(ti, g[1], "tn")
        return -(fine(d1, s1, "nt") + fine(d2, s2, "nt")), d1, d2

    solve2.defvjp(solve2_fwd, solve2_bwd)

    def chunk_fn(pq, pk, pv, z, g_col, b_col, g_row, ogain, s_in):
        nb = pq.shape[0]
        sq = (nb, CHUNK, CHUNK)
        row = lax.broadcasted_iota(jnp.int32, sq, 1)
        col = lax.broadcasted_iota(jnp.int32, sq, 2)
        lower, strict = row >= col, row > col
        q, k, v = _silu(pq), _silu(pk), _silu(pv)
        q = q * lax.rsqrt(jnp.sum(q * q, axis=-1, keepdims=True) + EPS) * (DN_HEAD_DIM ** -0.5)
        k = k * lax.rsqrt(jnp.sum(k * k, axis=-1, keepdims=True) + EPS)
        gc_wide = jnp.broadcast_to(g_col, pq.shape)
        gc_i = jnp.broadcast_to(g_col, sq)
        gc_j = jnp.broadcast_to(g_row, sq)
        is_last = lax.broadcasted_iota(jnp.int32, pq.shape, 1) == CHUNK - 1
        g_last = jnp.sum(jnp.where(is_last, gc_wide, 0.0), axis=1, keepdims=True)
        decay = jnp.exp(jnp.where(lower, gc_i - gc_j, -jnp.inf))
        kb = k * b_col
        a_mat = jnp.where(strict, nt(kb, k) * decay, 0.0)
        eg = jnp.exp(gc_wide)
        u, w = solve2(a_mat, v * b_col, kb * eg)
        attn = nt(q, k) * decay
        q_dec = q * eg
        k_dec = k * jnp.exp(g_last - gc_wide)
        c_dec = jnp.exp(g_last)
        v_new = u - nn(w, s_in)
        o = nn(q_dec, s_in) + nn(attn, v_new)
        s_out = s_in * c_dec + tn(k_dec, v_new)
        y = o * lax.rsqrt(jnp.mean(o * o, axis=-1, keepdims=True) + EPS) * ogain * _silu(z)
        return y, s_out

    return chunk_fn


DN_PRECISION = None


def _chunk_specs(n_of):
    groups = DN_HEADS // DN_HB
    wide = DN_HB * DN_HEAD_DIM
    hd = pl.BlockSpec((CHUNK, wide), lambda h, n: (n_of(n), h))
    specs = dict(
        pq=hd,
        pk=pl.BlockSpec((CHUNK, wide), lambda h, n: (n_of(n), groups + h)),
        pv=pl.BlockSpec((CHUNK, wide), lambda h, n: (n_of(n), 2 * groups + h)),
        z=hd,
        col=pl.BlockSpec((DN_HB, CHUNK, 1), lambda h, n: (h, n_of(n), 0)),
        row=pl.BlockSpec((DN_HB, None, 1, CHUNK), lambda h, n: (h, n_of(n), 0, 0)),
        gain=pl.BlockSpec((1, DN_HEAD_DIM), lambda h, n: (0, 0)),
        state=pl.BlockSpec((DN_HB, None, DN_HEAD_DIM, DN_HEAD_DIM), lambda h, n: (h, n_of(n), 0, 0)),
        head=hd,
    )
    return specs


def _head_cols(j):
    return slice(j * DN_HEAD_DIM, (j + 1) * DN_HEAD_DIM)


def _split_heads(ref):
    return jnp.stack([ref[:, _head_cols(j)] for j in range(DN_HB)])


def _chunk_fwd(name, pre, proj, g_col, b_col, g_row, ogain):
    S = pre.shape[0]
    N = S // CHUNK
    chunk_fn = _chunk_math(DN_PRECISION)
    sp = _chunk_specs(lambda n: n)

    def body(pq, pk, pv, z, gc, bc, gr, og, y_ref, sin_ref, st):
        @pl.when(pl.program_id(1) == 0)
        def _():
            st[...] = jnp.zeros_like(st)

        s_in = st[...]
        sin_ref[...] = s_in
        y, s_out = chunk_fn(_split_heads(pq), _split_heads(pk), _split_heads(pv), _split_heads(z),
                            gc[...], bc[...], gr[...], og[...], s_in)
        for j in range(DN_HB):
            y_ref[:, _head_cols(j)] = y[j].astype(y_ref.dtype)
        st[...] = s_out

    return pl.pallas_call(
        body, name=name, grid=(DN_HEADS // DN_HB, N),
        in_specs=[sp["pq"], sp["pk"], sp["pv"], sp["z"], sp["col"], sp["col"], sp["row"], sp["gain"]],
        out_specs=[sp["head"], sp["state"]],
        out_shape=[jax.ShapeDtypeStruct((S, DN_WIDTH), BF16),
                   jax.ShapeDtypeStruct((DN_HEADS, N, DN_HEAD_DIM, DN_HEAD_DIM), F32)],
        scratch_shapes=[pltpu.VMEM((DN_HB, DN_HEAD_DIM, DN_HEAD_DIM), F32)],
        compiler_params=_cparams(("parallel", "arbitrary")),
    )(pre, pre, pre, proj, g_col, b_col, g_row, ogain)


def _chunk_bwd(name, pre, proj, g_col, b_col, g_row, ogain, s_in_all, dy):
    S = pre.shape[0]
    N = S // CHUNK
    chunk_fn = _chunk_math(DN_PRECISION)
    sp = _chunk_specs(lambda n: N - 1 - n)

    def body(pq, pk, pv, z, gc, bc, gr, og, sin_ref, dy_ref,
             dq_ref, dk_ref, dv_ref, dz_ref, dgc_ref, dbc_ref, dgr_ref, dog_ref, ds):
        first = jnp.logical_and(pl.program_id(0) == 0, pl.program_id(1) == 0)

        @pl.when(pl.program_id(1) == 0)
        def _():
            ds[...] = jnp.zeros_like(ds)

        @pl.when(first)
        def _():
            dog_ref[...] = jnp.zeros_like(dog_ref)

        prim = (_split_heads(pq), _split_heads(pk), _split_heads(pv), _split_heads(z),
                gc[...], bc[...], gr[...], og[...], sin_ref[...])
        _, vjp = jax.vjp(chunk_fn, *prim)
        gq, gk, gv, gz, ggc, gbc, ggr, gog, gs = vjp((_split_heads(dy_ref), ds[...]))
        for j in range(DN_HB):
            c = _head_cols(j)
            dq_ref[:, c] = gq[j]
            dk_ref[:, c] = gk[j]
            dv_ref[:, c] = gv[j]
            dz_ref[:, c] = gz[j]
        dgc_ref[...] = ggc
        dbc_ref[...] = gbc
        dgr_ref[...] = ggr
        dog_ref[...] += gog
        ds[...] = gs

    hd = sp["head"]
    return pl.pallas_call(
        body, name=name, grid=(DN_HEADS // DN_HB, N),
        in_specs=[sp["pq"], sp["pk"], sp["pv"], sp["z"], sp["col"], sp["col"], sp["row"], sp["gain"],
                  sp["state"], hd],
        out_specs=[hd, hd, hd, hd, sp["col"], sp["col"], sp["row"], sp["gain"]],
        out_shape=[jax.ShapeDtypeStruct((S, DN_WIDTH), F32)] * 4
        + [jax.ShapeDtypeStruct((DN_HEADS, S, 1), F32)] * 2
        + [jax.ShapeDtypeStruct((DN_HEADS, N, 1, CHUNK), F32), jax.ShapeDtypeStruct((1, DN_HEAD_DIM), F32)],
        scratch_shapes=[pltpu.VMEM((DN_HB, DN_HEAD_DIM, DN_HEAD_DIM), F32)],
        compiler_params=_cparams(("arbitrary", "arbitrary")),
    )(pre, pre, pre, proj, g_col, b_col, g_row, ogain, s_in_all, dy)


def _mlp_ple_fwd(tag, x_in, p_l, norm_mlp, w_up, w_down, norm_ple, w_ple, w_gate):
    h = _rms_fwd(f"{tag}_mlp_norm", x_in, norm_mlp)
    u, a = _mm(f"{tag}_up", h, w_up, "nn", out_dtypes=(F32, BF16), epilogue=_relu2_epilogue)
    x_mid = _mm(f"{tag}_down", a, w_down, "nn", extras=(x_in,), epilogue=lambda acc, r: (acc + r,))
    hg = _rms_fwd(f"{tag}_ple_norm", x_mid, norm_ple)
    zg = _mm(f"{tag}_gate", hg, w_gate, "nn")
    pp = _mm(f"{tag}_ple", p_l, w_ple, "nn")
    x_out = _rowwise(f"{tag}_ple_out", _ple_fwd_fn, [x_mid, pp, zg], [], [(D_MODEL, F32)])
    return x_out, dict(x_in=x_in, h=h, u=u, a=a, x_mid=x_mid, hg=hg, zg=zg, pp=pp)


def _mlp_ple_bwd(tag, dx, sv, p_l, norm_mlp, w_up, w_down, norm_ple, w_ple, w_gate):
    dpp, dzg = _rowwise(f"{tag}_ple_bwd", _ple_bwd_fn, [dx, sv["pp"], sv["zg"]], [],
                        [(D_MODEL, BF16), (D_MODEL, BF16)])
    d_w_ple = _mm(f"{tag}_d_w_ple", p_l, dpp, "tn")
    d_w_gate = _mm(f"{tag}_d_w_gate", sv["hg"], dzg, "tn")
    dhg = _mm(f"{tag}_d_hg", dzg, w_gate, "nt")
    dx_mid, d_norm_ple = _rms_bwd(f"{tag}_ple_norm_bwd", sv["x_mid"], norm_ple, dx, [dhg])
    du = _mm(f"{tag}_d_u", dx_mid, w_down, "nt", out_dtypes=(BF16,), extras=(sv["u"],),
             epilogue=_relu2_bwd_epilogue)
    d_w_down = _mm(f"{tag}_d_w_down", sv["a"], dx_mid, "tn")
    d_w_up = _mm(f"{tag}_d_w_up", sv["h"], du, "tn")
    dh = _mm(f"{tag}_d_h", du, w_up, "nt")
    dx_in, d_norm_mlp = _rms_bwd(f"{tag}_mlp_norm_bwd", sv["x_in"], norm_mlp, dx_mid, [dh])
    return dx_in, dict(mlp_norm=d_norm_mlp, w_up=d_w_up, w_down=d_w_down, ple_norm=d_norm_ple,
                       w_ple=d_w_ple, w_ple_gate=d_w_gate)


def _local_step(x, p, positions, target, small, big):
    S = x.shape[0]
    ct, st = _rope_lane_tables(positions)
    bd = _head_mean_matrix()

    h0 = _rms_fwd("l0_mix_norm", x, small["mix_norm"][0])
    attn = []
    for g, (window, d) in enumerate(SWA_GROUPS):
        assert window // d == BAND and (S // d) % BAND == 0
        h0g = _to_classes(h0, d)
        ctg, stg = _to_classes(ct, d), _to_classes(st, d)
        w_g = big["attn_w_qkv"][:, g * 3 * A_WIDTH:(g + 1) * 3 * A_WIDTH]
        gq = jnp.tile(small["attn_q_gain"][0, g], A_HEADS).reshape(1, A_WIDTH)
        gk = jnp.tile(small["attn_k_gain"][0, g], A_HEADS).reshape(1, A_WIDTH)
        qkv = _mm(f"l0_qkv{g}", h0g, w_g, "nn")
        q, k, v = _rowwise(f"l0_qk_prep{g}", _qk_prep_fwd_fn, [qkv, ctg, stg], [gq, gk, bd], [(A_WIDTH, BF16)] * 3)
        o, lse = _attn_fwd(f"l0_attn{g}", q, k, v, (S // d) // BAND)
        attn.append(dict(d=d, h0g=h0g, ct=ctg, st=stg, w=w_g, gq=gq, gk=gk, qkv=qkv, q=q, k=k, v=v, o=o, lse=lse,
                         o_tok=_from_classes(o, d), lse_tok=_from_classes(lse, d)))
    om = _rowwise("l0_merge", _merge_fwd_fn, [a["o_tok"] for a in attn] + [a["lse_tok"] for a in attn], [],
                  [(A_WIDTH, BF16)])
    x1 = _mm("l0_attn_out", om, big["attn_w_o"], "nn", extras=(x,), epilogue=lambda acc, r: (acc + r,))
    x3, sv0 = _mlp_ple_fwd("l0", x1, p[0], small["mlp_norm"][0], big["w_up"][0], big["w_down"][0],
                           small["ple_norm"][0], big["w_ple"][0], big["w_ple_gate"][0])

    N = S // CHUNK
    h3 = _rms_fwd("l1_mix_norm", x3, small["mix_norm"][1])
    proj = _mm("l1_in", h3, big["dn_w_in"], "nn")
    pre = _conv_fwd("l1_conv", proj, small["dn_conv"])
    ab = proj[:, DN_AB0:DN_AB0 + DN_AB_PAD]
    lane_pad = DN_AB_PAD - DN_HEADS
    alog_row = jnp.pad(small["dn_a_log"][0], (0, lane_pad)).reshape(1, DN_AB_PAD)
    dt_row = jnp.pad(small["dn_dt_bias"][0], (0, lane_pad)).reshape(1, DN_AB_PAD)
    cum = _chunk_cumsum_matrix()
    gb = _rowwise("l1_gates", _gates_fwd_fn, [ab], [alog_row, dt_row, cum], [(DN_AB_PAD, F32)], tm=GATES_ROWS)
    g_t, b_t = gb[:, :DN_HEADS].T, gb[:, DN_HEADS:2 * DN_HEADS].T
    g_col, b_col = g_t.reshape(DN_HEADS, S, 1), b_t.reshape(DN_HEADS, S, 1)
    g_row = g_t.reshape(DN_HEADS, N, 1, CHUNK)
    ogain = small["dn_o_gain"][0].reshape(1, DN_HEAD_DIM)
    y, s_in_all = _chunk_fwd("l1_delta", pre, proj, g_col, b_col, g_row, ogain)
    x4 = _mm("l1_dn_out", y, big["dn_w_o"], "nn", extras=(x3,), epilogue=lambda acc, r: (acc + r,))
    x6, sv1 = _mlp_ple_fwd("l1", x4, p[1], small["mlp_norm"][1], big["w_up"][1], big["w_down"][1],
                           small["ple_norm"][1], big["w_ple"][1], big["w_ple_gate"][1])

    dy, sq = _rowwise("loss", _loss_fn, [x6, target], [], [(D_MODEL, F32)], [(1, 128)])

    dx4, gl1 = _mlp_ple_bwd("l1", dy, sv1, p[1], small["mlp_norm"][1], big["w_up"][1], big["w_down"][1],
                            small["ple_norm"][1], big["w_ple"][1], big["w_ple_gate"][1])
    d_y = _mm("l1_d_y", dx4, big["dn_w_o"], "nt")
    d_dn_w_o = _mm("l1_d_w_o", y, dx4, "tn")
    dq, dk, dv, dz, dg_col, db_col, dg_row, d_ogain = _chunk_bwd(
        "l1_delta_bwd", pre, proj, g_col, b_col, g_row, ogain, s_in_all, d_y)
    dpre = jnp.concatenate([dq, dk, dv], axis=1)
    dconv_in, d_conv_w = _conv_bwd("l1_conv_bwd", proj, dpre, small["dn_conv"])
    dg_t = dg_col.reshape(DN_HEADS, S) + dg_row.reshape(DN_HEADS, S)
    dgb = jnp.pad(jnp.concatenate([dg_t, db_col.reshape(DN_HEADS, S)], axis=0).T,
                  ((0, 0), (0, DN_AB_PAD - 2 * DN_HEADS)))
    dab, d_alog, d_dt = _rowwise("l1_gates_bwd", _gates_bwd_fn, [ab, dgb], [alog_row, dt_row, cum],
                                 [(DN_AB_PAD, F32)], [(1, DN_AB_PAD), (1, DN_AB_PAD)], tm=GATES_ROWS)
    dproj = jnp.concatenate([dz.astype(BF16), dconv_in.astype(BF16), dab.astype(BF16)], axis=1)
    d_dn_w_in = _mm("l1_d_w_in", h3, dproj, "tn")
    dh3 = _mm("l1_d_h", dproj, big["dn_w_in"], "nt")
    dx3, d_mix1 = _rms_bwd("l1_mix_norm_bwd", x3, small["mix_norm"][1], dx4, [dh3])

    dx1, gl0 = _mlp_ple_bwd("l0", dx3, sv0, p[0], small["mlp_norm"][0], big["w_up"][0], big["w_down"][0],
                            small["ple_norm"][0], big["w_ple"][0], big["w_ple_gate"][0])
    dom = _mm("l0_d_om", dx1, big["attn_w_o"], "nt")
    d_attn_w_o = _mm("l0_d_w_o", om, dx1, "tn")
    merged = _rowwise("l0_merge_bwd", _merge_bwd_fn,
                      [a["o_tok"] for a in attn] + [a["lse_tok"] for a in attn] + [dom], [], [(A_WIDTH, F32)] * 6)
    dh0, d_w_qkv, d_gq, d_gk = [], [], [], []
    for g, a in enumerate(attn):
        do_g, dl_g = _to_classes(merged[g], a["d"]), _to_classes(merged[3 + g], a["d"])
        dqn, dkn, dvn = _attn_bwd(f"l0_attn_bwd{g}", a["q"], a["k"], a["v"], a["o"], a["lse"], do_g, dl_g,
                                  (S // a["d"]) // BAND)
        dqkv, dgq, dgk = _rowwise(f"l0_qk_prep_bwd{g}", _qk_prep_bwd_fn, [a["qkv"], a["ct"], a["st"], dqn, dkn, dvn],
                                  [a["gq"], a["gk"], bd], [(3 * A_WIDTH, BF16)], [(1, A_HEAD_DIM)] * 2)
        d_w_qkv.append(_mm(f"l0_d_w_qkv{g}", a["h0g"], dqkv, "tn"))
        dh0.append(_from_classes(_mm(f"l0_d_h{g}", dqkv, a["w"], "nt"), a["d"]))
        d_gq.append(dgq)
        d_gk.append(dgk)
    grad_x, d_mix0 = _rms_bwd("l0_mix_norm_bwd", x, small["mix_norm"][0], dx1, dh0)

    grads = dict(
        mix_norm=jnp.concatenate([d_mix0, d_mix1], axis=0),
        attn_w_qkv=jnp.concatenate(d_w_qkv, axis=1)[None],
        attn_q_gain=jnp.concatenate(d_gq, axis=0)[None],
        attn_k_gain=jnp.concatenate(d_gk, axis=0)[None],
        attn_w_o=d_attn_w_o[None],
        dn_w_in=jnp.concatenate([d_dn_w_in[:, DN_QKV0:DN_AB0 + 2 * DN_HEADS], d_dn_w_in[:, :DN_WIDTH]], axis=1)[None],
        dn_conv=d_conv_w[None],
        dn_a_log=d_alog[:, :DN_HEADS],
        dn_dt_bias=d_dt[:, :DN_HEADS],
        dn_o_gain=d_ogain,
        dn_w_o=d_dn_w_o[None],
        mlp_norm=jnp.concatenate([gl0["mlp_norm"], gl1["mlp_norm"]], axis=0),
        w_up=jnp.stack([gl0["w_up"], gl1["w_up"]]),
        w_down=jnp.stack([gl0["w_down"], gl1["w_down"]]),
        ple_norm=jnp.concatenate([gl0["ple_norm"], gl1["ple_norm"]], axis=0),
        w_ple=jnp.stack([gl0["w_ple"], gl1["w_ple"]]),
        w_ple_gate=jnp.stack([gl0["w_ple_gate"], gl1["w_ple_gate"]]),
    )
    return sq, grad_x, grads


def _chip_peer(x, y, c, t):
    return (jnp.bitwise_xor(x, t >> 1), jnp.bitwise_xor(y, t & 1), c)


def _gather_from_chips(name, slab):
    R, C = slab.shape

    def body(src, out, send_sems, recv_sems, local_sem):
        x, y, c = lax.axis_index("x"), lax.axis_index("y"), lax.axis_index("c")
        q = 2 * x + y
        local = pltpu.make_async_copy(src, out.at[q], local_sem)
        local.start()
        copies = []
        for t in range(1, N_CHIPS):
            cp = pltpu.make_async_remote_copy(src_ref=src, dst_ref=out.at[q], send_sem=send_sems.at[t - 1],
                                              recv_sem=recv_sems.at[t - 1], device_id=_chip_peer(x, y, c, t),
                                              device_id_type=MESH)
            cp.start()
            copies.append(cp)
        for cp in copies:
            cp.wait()
        local.wait()

    return pl.pallas_call(
        body, name=name, out_shape=jax.ShapeDtypeStruct((N_CHIPS, R, C), slab.dtype),
        in_specs=[pl.BlockSpec(memory_space=pl.ANY)], out_specs=pl.BlockSpec(memory_space=pl.ANY),
        scratch_shapes=[pltpu.SemaphoreType.DMA((N_CHIPS - 1,)), pltpu.SemaphoreType.DMA((N_CHIPS - 1,)),
                        pltpu.SemaphoreType.DMA],
    )(slab)


def _scatter_to_chips(name, slabs):
    _, R, C = slabs.shape

    def body(src, out, send_sems, recv_sems):
        x, y, c = lax.axis_index("x"), lax.axis_index("y"), lax.axis_index("c")
        q = 2 * x + y
        copies = []
        for t in range(1, N_CHIPS):
            cp = pltpu.make_async_remote_copy(src_ref=src.at[jnp.bitwise_xor(q, t)], dst_ref=out.at[t - 1],
                                              send_sem=send_sems.at[t - 1], recv_sem=recv_sems.at[t - 1],
                                              device_id=_chip_peer(x, y, c, t), device_id_type=MESH)
            cp.start()
            copies.append(cp)
        for cp in copies:
            cp.wait()

    return pl.pallas_call(
        body, name=name, out_shape=jax.ShapeDtypeStruct((N_CHIPS - 1, R, C), slabs.dtype),
        in_specs=[pl.BlockSpec(memory_space=pl.ANY)], out_specs=pl.BlockSpec(memory_space=pl.ANY),
        scratch_shapes=[pltpu.SemaphoreType.DMA((N_CHIPS - 1,)), pltpu.SemaphoreType.DMA((N_CHIPS - 1,))],
    )(slabs)


def _swap_with_sibling(name, slab):
    def body(src, out, send_sem, recv_sem):
        x, y, c = lax.axis_index("x"), lax.axis_index("y"), lax.axis_index("c")
        cp = pltpu.make_async_remote_copy(src_ref=src, dst_ref=out, send_sem=send_sem, recv_sem=recv_sem,
                                          device_id=(x, y, 1 - c), device_id_type=MESH)
        cp.start()
        cp.wait()

    return pl.pallas_call(
        body, name=name, out_shape=jax.ShapeDtypeStruct(slab.shape, slab.dtype),
        in_specs=[pl.BlockSpec(memory_space=pl.ANY)], out_specs=pl.BlockSpec(memory_space=pl.ANY),
        scratch_shapes=[pltpu.SemaphoreType.DMA, pltpu.SemaphoreType.DMA],
    )(slab)


def _gather_from_all(name, block):
    R, C = block.shape

    def body(src, out, send_sems, recv_sems):
        x, y, c = lax.axis_index("x"), lax.axis_index("y"), lax.axis_index("c")
        me = 4 * x + 2 * y + c
        out[me] = src[...]
        copies = []
        for r in range(1, N_DEV):
            peer = (jnp.bitwise_xor(x, r >> 2), jnp.bitwise_xor(y, (r >> 1) & 1), jnp.bitwise_xor(c, r & 1))
            cp = pltpu.make_async_remote_copy(src_ref=src, dst_ref=out.at[me], send_sem=send_sems.at[r - 1],
                                              recv_sem=recv_sems.at[r - 1], device_id=peer, device_id_type=MESH)
            cp.start()
            copies.append(cp)
        for cp in copies:
            cp.wait()

    return pl.pallas_call(
        body, name=name, out_shape=jax.ShapeDtypeStruct((N_DEV, R, C), block.dtype),
        in_specs=[pl.BlockSpec(memory_space=pltpu.VMEM)], out_specs=pl.BlockSpec(memory_space=pltpu.VMEM),
        scratch_shapes=[pltpu.SemaphoreType.DMA((N_DEV - 1,)), pltpu.SemaphoreType.DMA((N_DEV - 1,))],
    )(block)


def _pack_rows(n_elements):
    rows = -(-n_elements // PACK_WIDTH)
    return -(-rows // PACK_ROW_TILE) * PACK_ROW_TILE


def _pack(parts, dtype):
    flat = jnp.concatenate([a.reshape(-1).astype(dtype) for a in parts])
    rows = _pack_rows(flat.shape[0])
    return jnp.pad(flat, (0, rows * PACK_WIDTH - flat.shape[0])).reshape(rows, PACK_WIDTH)


def _unpack(slab, shapes):
    lead = slab.shape[:-2]
    flat = slab.reshape(lead + (-1,))
    out, off = [], 0
    for shp in shapes:
        n = math.prod(shp)
        out.append(flat[..., off:off + n].reshape(lead + tuple(shp)))
        off += n
    return out


SMALL_ROWS = 8


def _pack_small(vals):
    tail = jnp.concatenate([vals["attn_q_gain"].reshape(-1), vals["attn_k_gain"].reshape(-1),
                            vals["dn_a_log"].reshape(-1), vals["dn_dt_bias"].reshape(-1),
                            vals["dn_o_gain"].reshape(-1)])
    tail = jnp.pad(tail, (0, D_MODEL - tail.shape[0])).reshape(1, D_MODEL)
    return jnp.concatenate([vals["mix_norm"], vals["mlp_norm"], vals["ple_norm"], tail,
                            jnp.zeros((1, D_MODEL), F32)], axis=0)


def _unpack_small(block):
    nq = 3 * A_HEAD_DIM
    t = block[6]
    return dict(
        mix_norm=block[0:2], mlp_norm=block[2:4], ple_norm=block[4:6],
        attn_q_gain=t[:nq].reshape(1, 3, A_HEAD_DIM), attn_k_gain=t[nq:2 * nq].reshape(1, 3, A_HEAD_DIM),
        dn_a_log=t[2 * nq:2 * nq + DN_HEADS].reshape(1, DN_HEADS),
        dn_dt_bias=t[2 * nq + DN_HEADS:2 * nq + 2 * DN_HEADS].reshape(1, DN_HEADS),
        dn_o_gain=t[2 * nq + 2 * DN_HEADS:2 * nq + 2 * DN_HEADS + DN_HEAD_DIM].reshape(1, DN_HEAD_DIM))


def kernel(x, p, positions, mix_norm, attn_w_qkv, attn_q_gain, attn_k_gain, attn_w_o, dn_w_in, dn_conv, dn_a_log, dn_dt_bias, dn_o_gain, dn_w_o, mlp_norm, w_up, w_down, ple_norm, w_ple, w_ple_gate, loss_target, m_mix_norm, m_attn_w_qkv, m_attn_q_gain, m_attn_k_gain, m_attn_w_o, m_dn_w_in, m_dn_conv, m_dn_a_log, m_dn_dt_bias, m_dn_o_gain, m_dn_w_o, m_mlp_norm, m_w_up, m_w_down, m_ple_norm, m_w_ple, m_w_ple_gate, v_mix_norm, v_attn_w_qkv, v_attn_q_gain, v_attn_k_gain, v_attn_w_o, v_dn_w_in, v_dn_conv, v_dn_a_log, v_dn_dt_bias, v_dn_o_gain, v_dn_w_o, v_mlp_norm, v_w_up, v_w_down, v_ple_norm, v_w_ple, v_w_ple_gate):
    given = dict(locals())
    w = {n: given[n] for n in WEIGHTS}
    m = {n: given["m_" + n] for n in WEIGHTS}
    v = {n: given["v_" + n] for n in WEIGHTS}
    shard_shapes = [w[n].shape for n, _ in SHARDED]

    gathered = _gather_from_chips("gather_weights", _pack([w[n] for n, _ in SHARDED], BF16))
    full = {}
    for (n, axis), parts in zip(SHARDED, _unpack(gathered, shard_shapes)):
        full[n] = jnp.concatenate([parts[q] for q in range(N_CHIPS)], axis=axis)
    conv_block = jnp.pad(w["dn_conv"].reshape(-1), (0, SMALL_ROWS * D_MODEL - w["dn_conv"].size))
    conv_all = _gather_from_all("gather_conv", conv_block.reshape(SMALL_ROWS, D_MODEL))
    conv_all = conv_all.reshape(N_CHIPS, 2, -1)[:, 0, :w["dn_conv"].size]
    conv_full = jnp.concatenate([conv_all[q].reshape(CONV_WIDTH, -1) for q in range(N_CHIPS)], axis=1)

    w_in = full["dn_w_in"][0]
    n_ab = 2 * DN_HEADS
    w_in_pad = jnp.concatenate([w_in[:, DN_QKV + n_ab:], w_in[:, :DN_QKV + n_ab],
                                jnp.zeros((D_MODEL, DN_AB_PAD - n_ab), BF16)], axis=1)
    big = dict(attn_w_qkv=full["attn_w_qkv"][0], attn_w_o=full["attn_w_o"][0], dn_w_in=w_in_pad,
               dn_w_o=full["dn_w_o"][0], w_up=full["w_up"], w_down=full["w_down"], w_ple=full["w_ple"],
               w_ple_gate=full["w_ple_gate"])
    small = {n: w[n] for n in REPLICATED}
    small["dn_conv"] = conv_full

    sq, grad_x, grads = _local_step(x[0], p[:, 0], positions[0], loss_target[0], small, big)
    loss = lax.psum(0.5 * sq[0, 0] / D_MODEL, ("x", "y", "c"))

    per_chip = []
    for q in range(N_CHIPS):
        per_chip.append(_pack([jnp.split(grads[n], N_CHIPS, axis=axis)[q] for n, axis in SHARDED], BF16))
    send = jnp.stack(per_chip)
    recv = _scatter_to_chips("scatter_grads", send)
    x_idx, y_idx = lax.axis_index("x"), lax.axis_index("y")
    mine = lax.dynamic_index_in_dim(send, 2 * x_idx + y_idx, axis=0, keepdims=False)
    partial = _rowwise("sum_chip_grads", lambda a, b, c, d: a.astype(F32) + b + c + d,
                       [mine, recv[0], recv[1], recv[2]], [], [(PACK_WIDTH, F32)])
    other = _swap_with_sibling("swap_core_sums", partial)
    packed = [_pack([t[n] for n, _ in SHARDED], F32) for t in (w, m, v)]
    g_s, delta_s, m_s, v_s = _rowwise("adamw_sharded", _adamw_pair_fn, [partial, other] + packed, [],
                                      [(PACK_WIDTH, F32)] * 4)
    out = {}
    for kind, slab in (("grad", g_s), ("delta", delta_s), ("new_m", m_s), ("new_v", v_s)):
        for (n, _), arr in zip(SHARDED, _unpack(slab, shard_shapes)):
            out[kind + "_" + n] = arr

    slots = _gather_from_all("gather_small_grads", _pack_small(grads))

    def small_body(s_ref, w_ref, m_ref, v_ref, g_out, d_out, m_out, v_out):
        res = _adamw_slots_fn(s_ref[...], w_ref[...], m_ref[...], v_ref[...])
        for o, r in zip((g_out, d_out, m_out, v_out), res):
            o[...] = r

    res = pl.pallas_call(small_body, name="adamw_replicated",
                         out_shape=[jax.ShapeDtypeStruct((SMALL_ROWS, D_MODEL), F32)] * 4)(
        slots, _pack_small(w), _pack_small(m), _pack_small(v))
    for kind, block in zip(("grad", "delta", "new_m", "new_v"), res):
        for n, arr in _unpack_small(block).items():
            out[kind + "_" + n] = arr

    return (loss, grad_x[None],
            *[out["grad_" + n] for n in WEIGHTS], *[out["delta_" + n] for n in WEIGHTS],
            *[out["new_m_" + n] for n in WEIGHTS], *[out["new_v_" + n] for n in WEIGHTS])
```

```python
import functools
import math

import jax
import jax.numpy as jnp
from jax import lax
from jax.experimental import pallas as pl
from jax.experimental.pallas import tpu as pltpu

F32 = jnp.float32
BF16 = jnp.bfloat16
HIGHEST = lax.Precision.HIGHEST

D_MODEL = 1024
EPS = 1e-6
SWA_GROUPS = ((128, 1), (512, 4), (2048, 16))
A_HEADS = 8
A_HEAD_DIM = 64
A_WIDTH = A_HEADS * A_HEAD_DIM
ROPE_DIM = A_HEAD_DIM // 4
ROPE_THETA = 500000.0
BAND = 128
DN_HEADS = 8
DN_HEAD_DIM = 128
DN_WIDTH = DN_HEADS * DN_HEAD_DIM
DN_QKV = 3 * DN_WIDTH
DN_AB_PAD = 128
DN_IN_PAD = DN_WIDTH + DN_QKV + DN_AB_PAD
DN_QKV0 = DN_WIDTH
DN_AB0 = DN_WIDTH + DN_QKV
DN_HB = 8
CONV_WIDTH = 4
CHUNK = 64
PLE_DIM = 256
D_FF = 4 * D_MODEL

ADAM_LR = 0.001
ADAM_B1 = 0.9
ADAM_B2 = 0.999
ADAM_EPS = 1e-08
ADAM_WD = 0.01
ADAM_STEP = 10

N_CHIPS = 4
N_DEV = 8
VMEM_LIMIT = 48 * 1024 * 1024
MESH = pl.DeviceIdType.MESH

SHARDED = (
    ("attn_w_qkv", 2), ("attn_w_o", 2), ("dn_w_in", 2), ("dn_conv", 2), ("dn_w_o", 1),
    ("w_up", 2), ("w_down", 1), ("w_ple", 2), ("w_ple_gate", 1))
REPLICATED = ("mix_norm", "attn_q_gain", "attn_k_gain", "dn_a_log", "dn_dt_bias", "dn_o_gain",
              "mlp_norm", "ple_norm")
WEIGHTS = ("mix_norm", "attn_w_qkv", "attn_q_gain", "attn_k_gain", "attn_w_o", "dn_w_in", "dn_conv",
           "dn_a_log", "dn_dt_bias", "dn_o_gain", "dn_w_o", "mlp_norm", "w_up", "w_down", "ple_norm",
           "w_ple", "w_ple_gate")


def _cparams(sem=None):
    return pltpu.CompilerParams(dimension_semantics=sem, vmem_limit_bytes=VMEM_LIMIT)


def _pick(n, cap, quantum=128):
    best = None
    for t in range(quantum, min(n, cap) + 1, quantum):
        if n % t == 0:
            best = t
    return n if best is None else best


_DIMS = {"nn": ((1,), (0,)), "nt": ((1,), (1,)), "tn": ((0,), (0,))}


def _mm(name, a, b, mode, out_dtypes=(F32,), extras=(), epilogue=None):
    if mode == "nn":
        (M, K), (K2, N) = a.shape, b.shape
    elif mode == "nt":
        (M, K), (N, K2) = a.shape, b.shape
    else:
        (K, M), (K2, N) = a.shape, b.shape
    assert K == K2, (name, a.shape, b.shape)
    tm, tn, tk = _pick(M, 512), _pick(N, 1536), _pick(K, 1024 if mode == "tn" else 1536)
    nk = K // tk
    if mode == "nn":
        a_spec = pl.BlockSpec((tm, tk), lambda i, j, k: (i, k))
        b_spec = pl.BlockSpec((tk, tn), lambda i, j, k: (k, j))
    elif mode == "nt":
        a_spec = pl.BlockSpec((tm, tk), lambda i, j, k: (i, k))
        b_spec = pl.BlockSpec((tn, tk), lambda i, j, k: (j, k))
    else:
        a_spec = pl.BlockSpec((tk, tm), lambda i, j, k: (k, i))
        b_spec = pl.BlockSpec((tk, tn), lambda i, j, k: (k, j))
    o_spec = pl.BlockSpec((tm, tn), lambda i, j, k: (i, j))
    n_extra, n_out = len(extras), len(out_dtypes)
    dims = (_DIMS[mode], ((), ()))

    def body(a_ref, b_ref, *rest):
        extra_refs, out_refs, acc = rest[:n_extra], rest[n_extra:n_extra + n_out], rest[-1]
        k = pl.program_id(2)

        @pl.when(k == 0)
        def _():
            acc[...] = jnp.zeros_like(acc)

        acc[...] += lax.dot_general(a_ref[...].astype(BF16), b_ref[...].astype(BF16), dims,
                                    preferred_element_type=F32)

        @pl.when(k == nk - 1)
        def _():
            if epilogue is None:
                vals = (acc[...],)
            else:
                vals = epilogue(acc[...], *[e[...] for e in extra_refs])
            for o, v in zip(out_refs, vals):
                o[...] = v.astype(o.dtype)

    outs = pl.pallas_call(
        body, name=name, grid=(M // tm, N // tn, nk),
        in_specs=[a_spec, b_spec] + [o_spec] * n_extra,
        out_specs=[o_spec] * n_out,
        out_shape=[jax.ShapeDtypeStruct((M, N), dt) for dt in out_dtypes],
        scratch_shapes=[pltpu.VMEM((tm, tn), F32)],
        compiler_params=_cparams(("parallel", "parallel", "arbitrary")),
    )(a, b, *extras)
    return outs[0] if n_out == 1 else outs


def _rowwise(name, fn, rows, bcast, row_outs, acc_outs=(), tm=256, n_rows=None):
    rows = [r if isinstance(r, tuple) else (r, r.shape[1], 0) for r in rows]
    rows = [r if len(r) == 4 else r + (0,) for r in rows]
    S = rows[0][0].shape[0] if n_rows is None else n_rows
    tm = min(tm, S)
    assert S % tm == 0 and all(r[3] % tm == 0 for r in rows), (name, S, tm)
    n_row, n_bc, n_ro, n_acc = len(rows), len(bcast), len(row_outs), len(acc_outs)
    in_specs = [pl.BlockSpec((tm, w), functools.partial(lambda i, cb, rb: (i + rb, cb), cb=cb, rb=r0 // tm))
                for _, w, cb, r0 in rows]
    in_specs += [pl.BlockSpec(b.shape, lambda i: (0, 0)) for b in bcast]
    out_specs = [pl.BlockSpec((tm, c), lambda i: (i, 0)) for c, _ in row_outs]
    out_specs += [pl.BlockSpec(s, lambda i: (0, 0)) for s in acc_outs]
    out_shape = [jax.ShapeDtypeStruct((S, c), dt) for c, dt in row_outs]
    out_shape += [jax.ShapeDtypeStruct(s, F32) for s in acc_outs]

    def body(*refs):
        ins = [r[...] for r in refs[:n_row + n_bc]]
        outs = refs[n_row + n_bc:]
        vals = fn(*ins)
        if not isinstance(vals, (tuple, list)):
            vals = (vals,)
        for o, v in zip(outs[:n_ro], vals[:n_ro]):
            o[...] = v.astype(o.dtype)
        if n_acc:
            @pl.when(pl.program_id(0) == 0)
            def _():
                for o in outs[n_ro:]:
                    o[...] = jnp.zeros_like(o)
            for o, v in zip(outs[n_ro:], vals[n_ro:]):
                o[...] += v

    outs = pl.pallas_call(
        body, name=name, grid=(S // tm,), in_specs=in_specs, out_specs=out_specs, out_shape=out_shape,
        compiler_params=_cparams(("arbitrary",) if n_acc else ("parallel",)),
    )(*[r[0] for r in rows], *bcast)
    return outs[0] if len(outs) == 1 else outs


def _sigmoid(x):
    return 1.0 / (1.0 + jnp.exp(-x))


def _silu(x):
    return x * _sigmoid(x)


def _softplus(x):
    return jnp.maximum(x, 0.0) + jnp.log(1.0 + jnp.exp(-jnp.abs(x)))


def _rms_fwd_fn(x, g):
    r = lax.rsqrt(jnp.mean(x * x, axis=-1, keepdims=True) + EPS)
    return (x * r) * g


def _rms_bwd_fn(x, dres, *rest):
    dh, g = sum(rest[:-1]), rest[-1]
    r = lax.rsqrt(jnp.mean(x * x, axis=-1, keepdims=True) + EPS)
    xh = x * r
    dxh = dh * g
    dx = r * (dxh - xh * jnp.mean(dxh * xh, axis=-1, keepdims=True))
    return dres + dx, jnp.sum(dh * xh, axis=0, keepdims=True)


def _rms_fwd(name, x, gain):
    return _rowwise(name, _rms_fwd_fn, [x], [gain.reshape(1, -1)], [(x.shape[1], BF16)])


def _rms_bwd(name, x, gain, dres, dhs):
    return _rowwise(name, _rms_bwd_fn, [x, dres] + list(dhs), [gain.reshape(1, -1)],
                    [(x.shape[1], F32)], [(1, x.shape[1])])


def _relu2_epilogue(acc):
    r = jnp.maximum(acc, 0.0)
    return acc, r * r


def _relu2_bwd_epilogue(acc, u):
    return (acc * (2.0 * jnp.maximum(u, 0.0)),)


def _ple_fwd_fn(x, pp, zg):
    return x + pp * _sigmoid(zg)


def _ple_bwd_fn(dx, pp, zg):
    gate = _sigmoid(zg)
    return dx * gate, dx * pp * gate * (1.0 - gate)


def _loss_fn(y, t):
    err = y - t
    return err * (1.0 / D_MODEL), jnp.broadcast_to(jnp.sum(err * err, keepdims=True), (1, 128))


def _adamw(w, g, m, v):
    m = ADAM_B1 * m + (1.0 - ADAM_B1) * g
    v = ADAM_B2 * v + (1.0 - ADAM_B2) * jnp.square(g)
    m_hat = m / (1.0 - ADAM_B1 ** ADAM_STEP)
    v_hat = v / (1.0 - ADAM_B2 ** ADAM_STEP)
    delta = -ADAM_LR * (m_hat / (jnp.sqrt(v_hat) + ADAM_EPS) + ADAM_WD * w)
    return delta, m, v


def _lane_take(x, offset):
    n = x.shape[-1]
    return pltpu.roll(x, (-offset) % n, 1)


def _head_lane(shape):
    return lax.broadcasted_iota(jnp.int32, shape, 1) % A_HEAD_DIM


def _rope_partner(x):
    lane = _head_lane(x.shape)
    return jnp.where(lane < ROPE_DIM // 2, _lane_take(x, ROPE_DIM // 2),
                     jnp.where(lane < ROPE_DIM, _lane_take(x, -(ROPE_DIM // 2)), 0.0))


def _head_mean(x, bd):
    return jnp.dot(x, bd, precision=HIGHEST, preferred_element_type=F32)


def _fold_heads(row):
    out = row[:, :A_HEAD_DIM]
    for h in range(1, A_HEADS):
        out = out + row[:, h * A_HEAD_DIM:(h + 1) * A_HEAD_DIM]
    return out


def _all_heads(t):
    return jnp.concatenate([t] * (A_WIDTH // t.shape[1]), axis=1)


def _qk_prep_fwd_fn(qkv, ct, st, gq, gk, bd):
    ct, st = _all_heads(ct), _all_heads(st)

    def one(t, g):
        n = t * lax.rsqrt(_head_mean(t * t, bd) + EPS) * g
        return n * ct + _rope_partner(n) * st
    q, k, v = qkv[:, :A_WIDTH], qkv[:, A_WIDTH:2 * A_WIDTH], qkv[:, 2 * A_WIDTH:]
    return one(q, gq), one(k, gk), v


def _qk_prep_bwd_fn(qkv, ct, st, dq, dk, dv, gq, gk, bd):
    ct, st = _all_heads(ct), _all_heads(st)

    def one(t, g, dy):
        r = lax.rsqrt(_head_mean(t * t, bd) + EPS)
        nh = t * r
        dn = dy * ct + _rope_partner(dy * st)
        dg = jnp.sum(dn * nh, axis=0, keepdims=True)
        dnh = dn * g
        return r * (dnh - nh * _head_mean(dnh * nh, bd)), _fold_heads(dg)
    q, k = qkv[:, :A_WIDTH], qkv[:, A_WIDTH:2 * A_WIDTH]
    dq_raw, dgq = one(q, gq, dq)
    dk_raw, dgk = one(k, gk, dk)
    return jnp.concatenate([dq_raw, dk_raw, dv], axis=1), dgq, dgk


def _band_masks():
    qi = lax.broadcasted_iota(jnp.int32, (BAND, BAND), 0)
    kj = lax.broadcasted_iota(jnp.int32, (BAND, BAND), 1)
    return kj <= qi, kj >= qi


def _attn_fwd(name, q, k, v, blocks_per_class):
    S = q.shape[0]
    nblk = S // BAND
    scale = A_HEAD_DIM ** -0.5

    def body(q_ref, kp_ref, kc_ref, vp_ref, vc_ref, o_ref, l_ref):
        i = pl.program_id(0)
        has_prev = (i % blocks_per_class) != 0
        m_cur, m_prev = _band_masks()
        m_prev = jnp.logical_and(m_prev, has_prev)
        for h in range(A_HEADS):
            sl = slice(h * A_HEAD_DIM, (h + 1) * A_HEAD_DIM)
            qh = q_ref[:, sl]
            s_c = lax.dot_general(qh, kc_ref[:, sl], (_DIMS["nt"], ((), ())), preferred_element_type=F32) * scale
            s_p = lax.dot_general(qh, kp_ref[:, sl], (_DIMS["nt"], ((), ())), preferred_element_type=F32) * scale
            s_c = jnp.where(m_cur, s_c, -jnp.inf)
            s_p = jnp.where(m_prev, s_p, -jnp.inf)
            m = jnp.maximum(jnp.max(s_c, axis=-1, keepdims=True), jnp.max(s_p, axis=-1, keepdims=True))
            e_c, e_p = jnp.exp(s_c - m), jnp.exp(s_p - m)
            l = jnp.sum(e_c, axis=-1, keepdims=True) + jnp.sum(e_p, axis=-1, keepdims=True)
            o = jnp.dot((e_c / l).astype(BF16), vc_ref[:, sl], preferred_element_type=F32)
            o = o + jnp.dot((e_p / l).astype(BF16), vp_ref[:, sl], preferred_element_type=F32)
            o_ref[:, sl] = o
            l_ref[:, sl] = jnp.broadcast_to(m + jnp.log(l), (BAND, A_HEAD_DIM))

    cur = pl.BlockSpec((BAND, A_WIDTH), lambda i: (i, 0))
    prev = pl.BlockSpec((BAND, A_WIDTH), lambda i: (jnp.maximum(i - 1, 0), 0))
    return pl.pallas_call(
        body, name=name, grid=(nblk,), in_specs=[cur, prev, cur, prev, cur], out_specs=[cur, cur],
        out_shape=[jax.ShapeDtypeStruct((S, A_WIDTH), F32)] * 2,
        compiler_params=_cparams(("parallel",)),
    )(q, k, k, v, v)


def _attn_bwd(name, q, k, v, o, lse, do, dlse, blocks_per_class):
    S = q.shape[0]
    nblk = S // BAND
    scale = A_HEAD_DIM ** -0.5

    def body(q_ref, kp_ref, kc_ref, vp_ref, vc_ref, o_ref, l_ref, do_ref, dl_ref,
             dq_ref, dk_ref, dv_ref, ck, cv):
        i = pl.program_id(0)

        @pl.when(i == 0)
        def _():
            ck[...] = jnp.zeros_like(ck)
            cv[...] = jnp.zeros_like(cv)

        @pl.when(i == nblk)
        def _():
            dk_ref[...] = ck[...]
            dv_ref[...] = cv[...]

        @pl.when(i < nblk)
        def _():
            has_prev = (i % blocks_per_class) != 0
            m_cur, m_prev = _band_masks()
            m_prev = jnp.logical_and(m_prev, has_prev)
            nt, tn = (_DIMS["nt"], ((), ())), (_DIMS["tn"], ((), ()))
            for h in range(A_HEADS):
                sl = slice(h * A_HEAD_DIM, (h + 1) * A_HEAD_DIM)
                qh, kc, kp, vc, vp = q_ref[:, sl], kc_ref[:, sl], kp_ref[:, sl], vc_ref[:, sl], vp_ref[:, sl]
                doh = do_ref[:, sl]
                lse_h = l_ref[:, sl][:, :1]
                s_c = lax.dot_general(qh, kc, nt, preferred_element_type=F32) * scale
                s_p = lax.dot_general(qh, kp, nt, preferred_element_type=F32) * scale
                p_c = jnp.where(m_cur, jnp.exp(s_c - lse_h), 0.0)
                p_p = jnp.where(m_prev, jnp.exp(s_p - lse_h), 0.0)
                corr = (jnp.sum(dl_ref[:, sl], axis=-1, keepdims=True)
                        - jnp.sum(doh * o_ref[:, sl], axis=-1, keepdims=True))
                dob = doh.astype(BF16)
                dp_c = lax.dot_general(dob, vc, nt, preferred_element_type=F32)
                dp_p = lax.dot_general(dob, vp, nt, preferred_element_type=F32)
                ds_c = (p_c * (dp_c + corr)).astype(BF16)
                ds_p = (p_p * (dp_p + corr)).astype(BF16)
                dq = jnp.dot(ds_c, kc, preferred_element_type=F32) + jnp.dot(ds_p, kp, preferred_element_type=F32)
                dq_ref[:, sl] = dq * scale
                dk_ref[:, sl] = ck[:, sl] + lax.dot_general(ds_p, qh, tn, preferred_element_type=F32) * scale
                dv_ref[:, sl] = cv[:, sl] + lax.dot_general(p_p.astype(BF16), dob, tn, preferred_element_type=F32)
                ck[:, sl] = lax.dot_general(ds_c, qh, tn, preferred_element_type=F32) * scale
                cv[:, sl] = lax.dot_general(p_c.astype(BF16), dob, tn, preferred_element_type=F32)

    last = nblk - 1
    cur = pl.BlockSpec((BAND, A_WIDTH), lambda i: (jnp.minimum(i, last), 0))
    prev = pl.BlockSpec((BAND, A_WIDTH), lambda i: (jnp.minimum(jnp.maximum(i - 1, 0), last), 0))
    return pl.pallas_call(
        body, name=name, grid=(nblk + 1,),
        in_specs=[cur, prev, cur, prev, cur, cur, cur, cur, cur], out_specs=[cur, prev, prev],
        out_shape=[jax.ShapeDtypeStruct((S, A_WIDTH), F32)] * 3,
        scratch_shapes=[pltpu.VMEM((BAND, A_WIDTH), F32)] * 2,
        compiler_params=_cparams(("arbitrary",)),
    )(q, k, k, v, v, o, lse, do, dlse)


def _merge_fwd_fn(o0, o1, o2, l0, l1, l2):
    m = jnp.maximum(jnp.maximum(l0, l1), l2)
    e0, e1, e2 = jnp.exp(l0 - m), jnp.exp(l1 - m), jnp.exp(l2 - m)
    return (e0 * o0 + e1 * o1 + e2 * o2) / (e0 + e1 + e2)


def _merge_bwd_fn(o0, o1, o2, l0, l1, l2, dom):
    m = jnp.maximum(jnp.maximum(l0, l1), l2)
    e0, e1, e2 = jnp.exp(l0 - m), jnp.exp(l1 - m), jnp.exp(l2 - m)
    den = e0 + e1 + e2
    w0, w1, w2 = e0 / den, e1 / den, e2 / den
    dw0, dw1, dw2 = dom * o0, dom * o1, dom * o2
    mean = w0 * dw0 + w1 * dw1 + w2 * dw2
    return w0 * dom, w1 * dom, w2 * dom, w0 * (dw0 - mean), w1 * (dw1 - mean), w2 * (dw2 - mean)


def _to_classes(t, d):
    if d == 1:
        return t
    S, C = t.shape
    return t.reshape(S // d, d, C).transpose(1, 0, 2).reshape(S, C)


def _from_classes(t, d):
    if d == 1:
        return t
    S, C = t.shape
    return t.reshape(d, S // d, C).transpose(1, 0, 2).reshape(S, C)


def _rope_lane_tables(positions):
    inv_freq = ROPE_THETA ** (-jnp.arange(0, ROPE_DIM, 2, dtype=F32) / ROPE_DIM)
    ang = positions.astype(F32)[:, None] * inv_freq
    cos, sin = jnp.cos(ang), jnp.sin(ang)
    S = positions.shape[0]
    rest = A_HEAD_DIM - ROPE_DIM
    ct = jnp.concatenate([cos, cos, jnp.ones((S, rest), F32)], axis=1)
    st = jnp.concatenate([-sin, sin, jnp.zeros((S, rest), F32)], axis=1)
    return jnp.tile(ct, (1, 2)), jnp.tile(st, (1, 2))


def _head_mean_matrix():
    r = jnp.arange(A_WIDTH) // A_HEAD_DIM
    return (r[:, None] == r[None, :]).astype(F32) * (1.0 / A_HEAD_DIM)


def _conv_fwd(name, proj, w):
    S = proj.shape[0]
    tm, tc = min(512, S), 1024
    per8 = tm // 8
    off = DN_QKV0 // tc

    def body(x_ref, halo_ref, w_ref, o_ref, xs):
        i = pl.program_id(0)
        xs[0:8, :] = jnp.where(i > 0, halo_ref[...], 0.0)
        xs[8:, :] = x_ref[...]
        acc = w_ref[0:1, :] * xs[pl.ds(8 - 3, tm), :]
        for j in range(1, CONV_WIDTH):
            acc = acc + w_ref[j:j + 1, :] * xs[pl.ds(8 - 3 + j, tm), :]
        o_ref[...] = acc

    return pl.pallas_call(
        body, name=name, grid=(S // tm, DN_QKV // tc),
        in_specs=[pl.BlockSpec((tm, tc), lambda i, j: (i, j + off)),
                  pl.BlockSpec((8, tc), lambda i, j: (jnp.maximum(i * per8 - 1, 0), j + off)),
                  pl.BlockSpec((CONV_WIDTH, tc), lambda i, j: (0, j))],
        out_specs=pl.BlockSpec((tm, tc), lambda i, j: (i, j)),
        out_shape=jax.ShapeDtypeStruct((S, DN_QKV), F32),
        scratch_shapes=[pltpu.VMEM((tm + 8, tc), F32)],
        compiler_params=_cparams(("parallel", "parallel")),
    )(proj, proj, w)


def _conv_bwd(name, proj, dpre, w):
    S = proj.shape[0]
    tm, tc = min(512, S), 1024
    per8 = tm // 8
    off = DN_QKV0 // tc
    last8 = S // 8 - 1
    nrow = S // tm

    def body(x_ref, xh_ref, d_ref, dh_ref, w_ref, dx_ref, dw_ref, xs, ds):
        i = pl.program_id(1)
        xs[0:8, :] = jnp.where(i > 0, xh_ref[...], 0.0)
        xs[8:, :] = x_ref[...]
        ds[0:tm, :] = d_ref[...]
        ds[tm:, :] = jnp.where(i < nrow - 1, dh_ref[...], 0.0)
        d = d_ref[...]
        acc = w_ref[0:1, :] * ds[pl.ds(3, tm), :]
        for j in range(1, CONV_WIDTH):
            acc = acc + w_ref[j:j + 1, :] * ds[pl.ds(3 - j, tm), :]
        dx_ref[...] = acc

        @pl.when(i == 0)
        def _():
            dw_ref[...] = jnp.zeros_like(dw_ref)

        for j in range(CONV_WIDTH):
            dw_ref[j:j + 1, :] += jnp.sum(d * xs[pl.ds(8 - 3 + j, tm), :], axis=0, keepdims=True)

    return pl.pallas_call(
        body, name=name, grid=(DN_QKV // tc, nrow),
        in_specs=[pl.BlockSpec((tm, tc), lambda j, i: (i, j + off)),
                  pl.BlockSpec((8, tc), lambda j, i: (jnp.maximum(i * per8 - 1, 0), j + off)),
                  pl.BlockSpec((tm, tc), lambda j, i: (i, j)),
                  pl.BlockSpec((8, tc), lambda j, i: (jnp.minimum((i + 1) * per8, last8), j)),
                  pl.BlockSpec((CONV_WIDTH, tc), lambda j, i: (0, j))],
        out_specs=[pl.BlockSpec((tm, tc), lambda j, i: (i, j)),
                   pl.BlockSpec((CONV_WIDTH, tc), lambda j, i: (0, j))],
        out_shape=[jax.ShapeDtypeStruct((S, DN_QKV), F32), jax.ShapeDtypeStruct((CONV_WIDTH, DN_QKV), F32)],
        scratch_shapes=[pltpu.VMEM((tm + 8, tc), F32)] * 2,
        compiler_params=_cparams(("parallel", "arbitrary")),
    )(proj, proj, dpre, dpre, w)


def _gate_lane(shape):
    return lax.broadcasted_iota(jnp.int32, shape, 1)


GATES_ROWS = 256


def _chunk_cumsum_matrix():
    r = jnp.arange(GATES_ROWS)
    return ((r[:, None] >= r[None, :]) & (r[:, None] // CHUNK == r[None, :] // CHUNK)).astype(F32)


def _gates_fwd_fn(ab, alog, dt, cum):
    g = -jnp.exp(alog) * _softplus(ab + dt)
    gc = jnp.dot(cum, g, precision=HIGHEST, preferred_element_type=F32)
    return jnp.where(_gate_lane(ab.shape) < DN_HEADS, gc, _sigmoid(ab))


def _gates_bwd_fn(ab, dgb, alog, dt, cum):
    lane = _gate_lane(ab.shape)
    is_g = lane < DN_HEADS
    neg_a = -jnp.exp(alog)
    sp = _softplus(ab + dt)
    dsp = _sigmoid(ab + dt)
    beta = _sigmoid(ab)
    dgc = jnp.where(is_g, dgb, 0.0)
    dg = lax.dot_general(cum, dgc, (_DIMS["tn"], ((), ())), precision=HIGHEST, preferred_element_type=F32)
    dab = jnp.where(is_g, dg * neg_a * dsp, jnp.where(lane < 2 * DN_HEADS, dgb * beta * (1.0 - beta), 0.0))
    d_alog = jnp.sum(dg * neg_a * sp, axis=0, keepdims=True)
    d_dt = jnp.sum(dg * neg_a * dsp, axis=0, keepdims=True)
    return dab, d_alog, d_dt


_BATCH_DIMS = {"nn": ((2,), (1,)), "nt": ((2,), (2,)), "tn": ((1,), (1,))}


def _chunk_math(precision):
    def dg(a, b, mode, prec=precision):
        return lax.dot_general(a, b, (_BATCH_DIMS[mode], ((0,), (0,))), precision=prec,
                               preferred_element_type=F32)

    @jax.custom_vjp
    def nn(a, b):
        return dg(a, b, "nn")

    @jax.custom_vjp
    def nt(a, b):
        return dg(a, b, "nt")

    @jax.custom_vjp
    def tn(a, b):
        return dg(a, b, "tn")

    nn.defvjp(lambda a, b: (nn(a, b), (a, b)), lambda r, g: (nt(g, r[1]), tn(r[0], g)))
    nt.defvjp(lambda a, b: (nt(a, b), (a, b)), lambda r, g: (nn(g, r[1]), tn(g, r[0])))
    tn.defvjp(lambda a, b: (tn(a, b), (a, b)), lambda r, g: (nt(r[1], g), nn(r[0], g)))

    def split(x):
        hi = x.astype(BF16)
        return hi, (x - hi.astype(F32)).astype(BF16)

    def fine(a, b, mode):
        ah, al = split(a)
        bh, bl = split(b)
        return dg(ah, bh, mode, None) + (dg(ah, bl, mode, None) + dg(al, bh, mode, None))

    def unit_lower_inverse(a):
        row = lax.broadcasted_iota(jnp.int32, a.shape, 1)
        col = lax.broadcasted_iota(jnp.int32, a.shape, 2)
        x = -a
        p = jnp.where(row == col, 1.0, 0.0) + x
        for _ in range(int(math.log2(CHUNK)) - 1):
            x = fine(x, x, "nn")
            p = p + fine(p, x, "nn")
        return p

    @jax.custom_vjp
    def solve2(a, r1, r2):
        ti = unit_lower_inverse(a)
        return fine(ti, r1, "nn"), fine(ti, r2, "nn")

    def solve2_fwd(a, r1, r2):
        ti = unit_lower_inverse(a)
        s1, s2 = fine(ti, r1, "nn"), fine(ti, r2, "nn")
        return (s1, s2), (ti, s1, s2)

    def solve2_bwd(res, g):
        ti, s1, s2 = res
        d1, d2 = fine(ti, g[0], "tn"), fine(ti, g[1], "tn")
        return -(fine(d1, s1, "nt") + fine(d2, s2, "nt")), d1, d2

    solve2.defvjp(solve2_fwd, solve2_bwd)

    def chunk_fn(pq, pk, pv, z, g_col, b_col, g_row, ogain, s_in):
        nb = pq.shape[0]
        sq = (nb, CHUNK, CHUNK)
        row = lax.broadcasted_iota(jnp.int32, sq, 1)
        col = lax.broadcasted_iota(jnp.int32, sq, 2)
        lower, strict = row >= col, row > col
        q, k, v = _silu(pq), _silu(pk), _silu(pv)
        q = q * lax.rsqrt(jnp.sum(q * q, axis=-1, keepdims=True) + EPS) * (DN_HEAD_DIM ** -0.5)
        k = k * lax.rsqrt(jnp.sum(k * k, axis=-1, keepdims=True) + EPS)
        gc_wide = jnp.broadcast_to(g_col, pq.shape)
        gc_i = jnp.broadcast_to(g_col, sq)
        gc_j = jnp.broadcast_to(g_row, sq)
        is_last = lax.broadcasted_iota(jnp.int32, pq.shape, 1) == CHUNK - 1
        g_last = jnp.sum(jnp.where(is_last, gc_wide, 0.0), axis=1, keepdims=True)
        decay = jnp.exp(jnp.where(lower, gc_i - gc_j, -jnp.inf))
        kb = k * b_col
        a_mat = jnp.where(strict, nt(kb, k) * decay, 0.0)
        eg = jnp.exp(gc_wide)
        u, w = solve2(a_mat, v * b_col, kb * eg)
        attn = nt(q, k) * decay
        q_dec = q * eg
        k_dec = k * jnp.exp(g_last - gc_wide)
        c_dec = jnp.exp(g_last)
        v_new = u - nn(w, s_in)
        o = nn(q_dec, s_in) + nn(attn, v_new)
        s_out = s_in * c_dec + tn(k_dec, v_new)
        y = o * lax.rsqrt(jnp.mean(o * o, axis=-1, keepdims=True) + EPS) * ogain * _silu(z)
        return y, s_out

    return chunk_fn


DN_PRECISION = None


def _chunk_specs(n_of):
    groups = DN_HEADS // DN_HB
    wide = DN_HB * DN_HEAD_DIM
    hd = pl.BlockSpec((CHUNK, wide), lambda h, n: (n_of(n), h))
    specs = dict(
        pq=hd,
        pk=pl.BlockSpec((CHUNK, wide), lambda h, n: (n_of(n), groups + h)),
        pv=pl.BlockSpec((CHUNK, wide), lambda h, n: (n_of(n), 2 * groups + h)),
        z=hd,
        col=pl.BlockSpec((DN_HB, CHUNK, 1), lambda h, n: (h, n_of(n), 0)),
        row=pl.BlockSpec((DN_HB, None, 1, CHUNK), lambda h, n: (h, n_of(n), 0, 0)),
        gain=pl.BlockSpec((1, DN_HEAD_DIM), lambda h, n: (0, 0)),
        state=pl.BlockSpec((DN_HB, None, DN_HEAD_DIM, DN_HEAD_DIM), lambda h, n: (h, n_of(n), 0, 0)),
        head=hd,
    )
    return specs


def _head_cols(j):
    return slice(j * DN_HEAD_DIM, (j + 1) * DN_HEAD_DIM)


def _split_heads(ref):
    return jnp.stack([ref[:, _head_cols(j)] for j in range(DN_HB)])


def _chunk_fwd(name, pre, proj, g_col, b_col, g_row, ogain):
    S = pre.shape[0]
    N = S // CHUNK
    chunk_fn = _chunk_math(DN_PRECISION)
    sp = _chunk_specs(lambda n: n)

    def body(pq, pk, pv, z, gc, bc, gr, og, y_ref, sin_ref, st):
        @pl.when(pl.program_id(1) == 0)
        def _():
            st[...] = jnp.zeros_like(st)

        s_in = st[...]
        sin_ref[...] = s_in
        y, s_out = chunk_fn(_split_heads(pq), _split_heads(pk), _split_heads(pv), _split_heads(z),
                            gc[...], bc[...], gr[...], og[...], s_in)
        for j in range(DN_HB):
            y_ref[:, _head_cols(j)] = y[j].astype(y_ref.dtype)
        st[...] = s_out

    return pl.pallas_call(
        body, name=name, grid=(DN_HEADS // DN_HB, N),
        in_specs=[sp["pq"], sp["pk"], sp["pv"], sp["z"], sp["col"], sp["col"], sp["row"], sp["gain"]],
        out_specs=[sp["head"], sp["state"]],
        out_shape=[jax.ShapeDtypeStruct((S, DN_WIDTH), BF16),
                   jax.ShapeDtypeStruct((DN_HEADS, N, DN_HEAD_DIM, DN_HEAD_DIM), F32)],
        scratch_shapes=[pltpu.VMEM((DN_HB, DN_HEAD_DIM, DN_HEAD_DIM), F32)],
        compiler_params=_cparams(("parallel", "arbitrary")),
    )(pre, pre, pre, proj, g_col, b_col, g_row, ogain)


def _chunk_bwd(name, pre, proj, g_col, b_col, g_row, ogain, s_in_all, dy):
    S = pre.shape[0]
    N = S // CHUNK
    chunk_fn = _chunk_math(DN_PRECISION)
    sp = _chunk_specs(lambda n: N - 1 - n)

    def body(pq, pk, pv, z, gc, bc, gr, og, sin_ref, dy_ref,
             dq_ref, dk_ref, dv_ref, dz_ref, dgc_ref, dbc_ref, dgr_ref, dog_ref, ds):
        first = jnp.logical_and(pl.program_id(0) == 0, pl.program_id(1) == 0)

        @pl.when(pl.program_id(1) == 0)
        def _():
            ds[...] = jnp.zeros_like(ds)

        @pl.when(first)
        def _():
            dog_ref[...] = jnp.zeros_like(dog_ref)

        prim = (_split_heads(pq), _split_heads(pk), _split_heads(pv), _split_heads(z),
                gc[...], bc[...], gr[...], og[...], sin_ref[...])
        _, vjp = jax.vjp(chunk_fn, *prim)
        gq, gk, gv, gz, ggc, gbc, ggr, gog, gs = vjp((_split_heads(dy_ref), ds[...]))
        for j in range(DN_HB):
            c = _head_cols(j)
            dq_ref[:, c] = gq[j]
            dk_ref[:, c] = gk[j]
            dv_ref[:, c] = gv[j]
            dz_ref[:, c] = gz[j]
        dgc_ref[...] = ggc
        dbc_ref[...] = gbc
        dgr_ref[...] = ggr
        dog_ref[...] += gog
        ds[...] = gs

    hd = sp["head"]
    return pl.pallas_call(
        body, name=name, grid=(DN_HEADS // DN_HB, N),
        in_specs=[sp["pq"], sp["pk"], sp["pv"], sp["z"], sp["col"], sp["col"], sp["row"], sp["gain"],
                  sp["state"], hd],
        out_specs=[hd, hd, hd, hd, sp["col"], sp["col"], sp["row"], sp["gain"]],
        out_shape=[jax.ShapeDtypeStruct((S, DN_WIDTH), F32)] * 4
        + [jax.ShapeDtypeStruct((DN_HEADS, S, 1), F32)] * 2
        + [jax.ShapeDtypeStruct((DN_HEADS, N, 1, CHUNK), F32), jax.ShapeDtypeStruct((1, DN_HEAD_DIM), F32)],
        scratch_shapes=[pltpu.VMEM((DN_HB, DN_HEAD_DIM, DN_HEAD_DIM), F32)],
        compiler_params=_cparams(("arbitrary", "arbitrary")),
    )(pre, pre, pre, proj, g_col, b_col, g_row, ogain, s_in_all, dy)


def _mlp_ple_fwd(tag, x_in, p_l, norm_mlp, w_up, w_down, norm_ple, w_ple, w_gate):
    h = _rms_fwd(f"{tag}_mlp_norm", x_in, norm_mlp)
    u, a = _mm(f"{tag}_up", h, w_up, "nn", out_dtypes=(F32, BF16), epilogue=_relu2_epilogue)
    x_mid = _mm(f"{tag}_down", a, w_down, "nn", extras=(x_in,), epilogue=lambda acc, r: (acc + r,))
    hg = _rms_fwd(f"{tag}_ple_norm", x_mid, norm_ple)
    zg = _mm(f"{tag}_gate", hg, w_gate, "nn")
    pp = _mm(f"{tag}_ple", p_l, w_ple, "nn")
    x_out = _rowwise(f"{tag}_ple_out", _ple_fwd_fn, [x_mid, pp, zg], [], [(D_MODEL, F32)])
    return x_out, dict(x_in=x_in, h=h, u=u, a=a, x_mid=x_mid, hg=hg, zg=zg, pp=pp)


def _mlp_ple_bwd(tag, dx, sv, p_l, norm_mlp, w_up, w_down, norm_ple, w_ple, w_gate):
    dpp, dzg = _rowwise(f"{tag}_ple_bwd", _ple_bwd_fn, [dx, sv["pp"], sv["zg"]], [],
                        [(D_MODEL, BF16), (D_MODEL, BF16)])
    d_w_ple = _mm(f"{tag}_d_w_ple", p_l, dpp, "tn", out_dtypes=(BF16,))
    d_w_gate = _mm(f"{tag}_d_w_gate", sv["hg"], dzg, "tn", out_dtypes=(BF16,))
    dhg = _mm(f"{tag}_d_hg", dzg, w_gate, "nt")
    dx_mid, d_norm_ple = _rms_bwd(f"{tag}_ple_norm_bwd", sv["x_mid"], norm_ple, dx, [dhg])
    du = _mm(f"{tag}_d_u", dx_mid, w_down, "nt", out_dtypes=(BF16,), extras=(sv["u"],),
             epilogue=_relu2_bwd_epilogue)
    d_w_down = _mm(f"{tag}_d_w_down", sv["a"], dx_mid, "tn", out_dtypes=(BF16,))
    d_w_up = _mm(f"{tag}_d_w_up", sv["h"], du, "tn", out_dtypes=(BF16,))
    dh = _mm(f"{tag}_d_h", du, w_up, "nt")
    dx_in, d_norm_mlp = _rms_bwd(f"{tag}_mlp_norm_bwd", sv["x_in"], norm_mlp, dx_mid, [dh])
    return dx_in, dict(mlp_norm=d_norm_mlp, w_up=d_w_up, w_down=d_w_down, ple_norm=d_norm_ple,
                       w_ple=d_w_ple, w_ple_gate=d_w_gate)


def _local_step(x, p, positions, target, small, big):
    S = x.shape[0]
    ct, st = _rope_lane_tables(positions)
    bd = _head_mean_matrix()

    h0 = _rms_fwd("l0_mix_norm", x, small["mix_norm"][0])
    attn = []
    for g, (window, d) in enumerate(SWA_GROUPS):
        assert window // d == BAND and (S // d) % BAND == 0
        h0g = _to_classes(h0, d)
        ctg, stg = _to_classes(ct, d), _to_classes(st, d)
        w_g = big["attn_w_qkv"][:, g * 3 * A_WIDTH:(g + 1) * 3 * A_WIDTH]
        gq = jnp.tile(small["attn_q_gain"][0, g], A_HEADS).reshape(1, A_WIDTH)
        gk = jnp.tile(small["attn_k_gain"][0, g], A_HEADS).reshape(1, A_WIDTH)
        qkv = _mm(f"l0_qkv{g}", h0g, w_g, "nn")
        q, k, v = _rowwise(f"l0_qk_prep{g}", _qk_prep_fwd_fn, [qkv, ctg, stg], [gq, gk, bd], [(A_WIDTH, BF16)] * 3)
        o, lse = _attn_fwd(f"l0_attn{g}", q, k, v, (S // d) // BAND)
        attn.append(dict(d=d, h0g=h0g, ct=ctg, st=stg, w=w_g, gq=gq, gk=gk, qkv=qkv, q=q, k=k, v=v, o=o, lse=lse,
                         o_tok=_from_classes(o, d), lse_tok=_from_classes(lse, d)))
    om = _rowwise("l0_merge", _merge_fwd_fn, [a["o_tok"] for a in attn] + [a["lse_tok"] for a in attn], [],
                  [(A_WIDTH, BF16)])
    x1 = _mm("l0_attn_out", om, big["attn_w_o"], "nn", extras=(x,), epilogue=lambda acc, r: (acc + r,))
    x3, sv0 = _mlp_ple_fwd("l0", x1, p[0], small["mlp_norm"][0], big["w_up"][0], big["w_down"][0],
                           small["ple_norm"][0], big["w_ple"][0], big["w_ple_gate"][0])

    N = S // CHUNK
    h3 = _rms_fwd("l1_mix_norm", x3, small["mix_norm"][1])
    proj = _mm("l1_in", h3, big["dn_w_in"], "nn")
    pre = _conv_fwd("l1_conv", proj, small["dn_conv"])
    ab = proj[:, DN_AB0:DN_AB0 + DN_AB_PAD]
    lane_pad = DN_AB_PAD - DN_HEADS
    alog_row = jnp.pad(small["dn_a_log"][0], (0, lane_pad)).reshape(1, DN_AB_PAD)
    dt_row = jnp.pad(small["dn_dt_bias"][0], (0, lane_pad)).reshape(1, DN_AB_PAD)
    cum = _chunk_cumsum_matrix()
    gb = _rowwise("l1_gates", _gates_fwd_fn, [ab], [alog_row, dt_row, cum], [(DN_AB_PAD, F32)], tm=GATES_ROWS)
    g_t, b_t = gb[:, :DN_HEADS].T, gb[:, DN_HEADS:2 * DN_HEADS].T
    g_col, b_col = g_t.reshape(DN_HEADS, S, 1), b_t.reshape(DN_HEADS, S, 1)
    g_row = g_t.reshape(DN_HEADS, N, 1, CHUNK)
    ogain = small["dn_o_gain"][0].reshape(1, DN_HEAD_DIM)
    y, s_in_all = _chunk_fwd("l1_delta", pre, proj, g_col, b_col, g_row, ogain)
    x4 = _mm("l1_dn_out", y, big["dn_w_o"], "nn", extras=(x3,), epilogue=lambda acc, r: (acc + r,))
    x6, sv1 = _mlp_ple_fwd("l1", x4, p[1], small["mlp_norm"][1], big["w_up"][1], big["w_down"][1],
                           small["ple_norm"][1], big["w_ple"][1], big["w_ple_gate"][1])

    dy, sq = _rowwise("loss", _loss_fn, [x6, target], [], [(D_MODEL, F32)], [(1, 128)])

    dx4, gl1 = _mlp_ple_bwd("l1", dy, sv1, p[1], small["mlp_norm"][1], big["w_up"][1], big["w_down"][1],
                            small["ple_norm"][1], big["w_ple"][1], big["w_ple_gate"][1])
    d_y = _mm("l1_d_y", dx4, big["dn_w_o"], "nt")
    d_dn_w_o = _mm("l1_d_w_o", y, dx4, "tn", out_dtypes=(BF16,))
    dq, dk, dv, dz, dg_col, db_col, dg_row, d_ogain = _chunk_bwd(
        "l1_delta_bwd", pre, proj, g_col, b_col, g_row, ogain, s_in_all, d_y)
    dpre = jnp.concatenate([dq, dk, dv], axis=1)
    dconv_in, d_conv_w = _conv_bwd("l1_conv_bwd", proj, dpre, small["dn_conv"])
    dg_t = dg_col.reshape(DN_HEADS, S) + dg_row.reshape(DN_HEADS, S)
    dgb = jnp.pad(jnp.concatenate([dg_t, db_col.reshape(DN_HEADS, S)], axis=0).T,
                  ((0, 0), (0, DN_AB_PAD - 2 * DN_HEADS)))
    dab, d_alog, d_dt = _rowwise("l1_gates_bwd", _gates_bwd_fn, [ab, dgb], [alog_row, dt_row, cum],
                                 [(DN_AB_PAD, F32)], [(1, DN_AB_PAD), (1, DN_AB_PAD)], tm=GATES_ROWS)
    dproj = jnp.concatenate([dz.astype(BF16), dconv_in.astype(BF16), dab.astype(BF16)], axis=1)
    d_dn_w_in = _mm("l1_d_w_in", h3, dproj, "tn", out_dtypes=(BF16,))
    dh3 = _mm("l1_d_h", dproj, big["dn_w_in"], "nt")
    dx3, d_mix1 = _rms_bwd("l1_mix_norm_bwd", x3, small["mix_norm"][1], dx4, [dh3])

    dx1, gl0 = _mlp_ple_bwd("l0", dx3, sv0, p[0], small["mlp_norm"][0], big["w_up"][0], big["w_down"][0],
                            small["ple_norm"][0], big["w_ple"][0], big["w_ple_gate"][0])
    dom = _mm("l0_d_om", dx1, big["attn_w_o"], "nt")
    d_attn_w_o = _mm("l0_d_w_o", om, dx1, "tn", out_dtypes=(BF16,))
    merged = _rowwise("l0_merge_bwd", _merge_bwd_fn,
                      [a["o_tok"] for a in attn] + [a["lse_tok"] for a in attn] + [dom], [], [(A_WIDTH, F32)] * 6)
    dh0, d_w_qkv, d_gq, d_gk = [], [], [], []
    for g, a in enumerate(attn):
        do_g, dl_g = _to_classes(merged[g], a["d"]), _to_classes(merged[3 + g], a["d"])
        dqn, dkn, dvn = _attn_bwd(f"l0_attn_bwd{g}", a["q"], a["k"], a["v"], a["o"], a["lse"], do_g, dl_g,
                                  (S // a["d"]) // BAND)
        dqkv, dgq, dgk = _rowwise(f"l0_qk_prep_bwd{g}", _qk_prep_bwd_fn, [a["qkv"], a["ct"], a["st"], dqn, dkn, dvn],
                                  [a["gq"], a["gk"], bd], [(3 * A_WIDTH, BF16)], [(1, A_HEAD_DIM)] * 2)
        d_w_qkv.append(_mm(f"l0_d_w_qkv{g}", a["h0g"], dqkv, "tn", out_dtypes=(BF16,)))
        dh0.append(_from_classes(_mm(f"l0_d_h{g}", dqkv, a["w"], "nt"), a["d"]))
        d_gq.append(dgq)
        d_gk.append(dgk)
    grad_x, d_mix0 = _rms_bwd("l0_mix_norm_bwd", x, small["mix_norm"][0], dx1, dh0)

    grads = dict(
        mix_norm=jnp.concatenate([d_mix0, d_mix1], axis=0),
        attn_w_qkv=jnp.concatenate(d_w_qkv, axis=1),
        attn_q_gain=jnp.concatenate(d_gq, axis=0)[None],
        attn_k_gain=jnp.concatenate(d_gk, axis=0)[None],
        attn_w_o=d_attn_w_o,
        dn_w_in=jnp.concatenate([d_dn_w_in[:, DN_QKV0:DN_AB0 + 2 * DN_HEADS], d_dn_w_in[:, :DN_WIDTH]], axis=1),
        dn_conv=d_conv_w,
        dn_a_log=d_alog[:, :DN_HEADS],
        dn_dt_bias=d_dt[:, :DN_HEADS],
        dn_o_gain=d_ogain,
        dn_w_o=d_dn_w_o,
        mlp_norm=jnp.concatenate([gl0["mlp_norm"], gl1["mlp_norm"]], axis=0),
        w_up=jnp.stack([gl0["w_up"], gl1["w_up"]]),
        w_down=jnp.stack([gl0["w_down"], gl1["w_down"]]),
        ple_norm=jnp.concatenate([gl0["ple_norm"], gl1["ple_norm"]], axis=0),
        w_ple=jnp.stack([gl0["w_ple"], gl1["w_ple"]]),
        w_ple_gate=jnp.stack([gl0["w_ple_gate"], gl1["w_ple_gate"]]),
    )
    return sq, grad_x, grads


def _chip_peer(x, y, c, t):
    return (jnp.bitwise_xor(x, t >> 1), jnp.bitwise_xor(y, t & 1), c)


def _place():
    x, y, c = lax.axis_index("x"), lax.axis_index("y"), lax.axis_index("c")
    return x, y, c, 2 * x + y, (x, y, 1 - c)


def _remote(src, dst, send_sem, recv_sem, to):
    return pltpu.make_async_remote_copy(src_ref=src, dst_ref=dst, send_sem=send_sem, recv_sem=recv_sem,
                                        device_id=to, device_id_type=MESH)


def _hbm_call(name, body, ins, out_shape, scratch_shapes):
    any_spec = pl.BlockSpec(memory_space=pl.ANY)
    return pl.pallas_call(body, name=name, out_shape=out_shape, in_specs=[any_spec] * len(ins),
                          out_specs=[any_spec] * len(out_shape), scratch_shapes=scratch_shapes)(*ins)


def _half(n0, which):
    return pl.ds(which * (n0 // 2), n0 // 2)


def _gather_shards(name, shards):
    T = len(shards)

    def body(*refs):
        ins, outs = refs[:T], refs[T:2 * T]
        send, recv, loc = refs[2 * T:]
        x, y, c, q, sibling = _place()
        locals_, sends = [], []
        for i in range(T):
            cp = pltpu.make_async_copy(ins[i], outs[i].at[q], loc.at[i])
            cp.start()
            locals_.append(cp)
        for i in range(T):
            mine = _half(ins[i].shape[0], c)
            for t in range(1, N_CHIPS):
                cp = _remote(ins[i].at[mine], outs[i].at[q, mine], send.at[i, t - 1], recv.at[i, t - 1],
                             _chip_peer(x, y, c, t))
                cp.start()
                sends.append(cp)
        for i in range(T):
            mine = _half(ins[i].shape[0], c)
            for t in range(1, N_CHIPS):
                landed = outs[i].at[jnp.bitwise_xor(q, t), mine]
                _remote(landed, landed, send.at[i, t - 1], recv.at[i, t - 1], _chip_peer(x, y, c, t)).wait_recv()
                cp = _remote(landed, landed, send.at[i, 2 + t], recv.at[i, 2 + t], sibling)
                cp.start()
                sends.append(cp)
        for i in range(T):
            theirs = _half(ins[i].shape[0], 1 - c)
            for t in range(1, N_CHIPS):
                passed = outs[i].at[jnp.bitwise_xor(q, t), theirs]
                _remote(passed, passed, send.at[i, 2 + t], recv.at[i, 2 + t], sibling).wait_recv()
        for cp in sends:
            cp.wait_send()
        for cp in locals_:
            cp.wait()

    n_rel = 2 * (N_CHIPS - 1)
    return _hbm_call(name, body, shards,
                     [jax.ShapeDtypeStruct((N_CHIPS,) + s.shape, s.dtype) for s in shards],
                     [pltpu.SemaphoreType.DMA((T, n_rel)), pltpu.SemaphoreType.DMA((T, n_rel)),
                      pltpu.SemaphoreType.DMA((T,))])


def _split_with_sibling(name, stacks):
    T = len(stacks)

    def body(*refs):
        ins, outs = refs[:T], refs[T:3 * T]
        send, recv, loc = refs[3 * T:]
        x, y, c, q, sibling = _place()
        copies = []
        for i in range(T):
            n0 = ins[i].shape[1]
            lc = pltpu.make_async_copy(ins[i].at[:, _half(n0, c)], outs[2 * i], loc.at[i])
            lc.start()
            rc = _remote(ins[i].at[:, _half(n0, 1 - c)], outs[2 * i + 1], send.at[i], recv.at[i], sibling)
            rc.start()
            copies += [lc, rc]
        for cp in copies:
            cp.wait()

    out_shape = []
    for s in stacks:
        half = jax.ShapeDtypeStruct((s.shape[0], s.shape[1] // 2) + s.shape[2:], s.dtype)
        out_shape += [half, half]
    outs = _hbm_call(name, body, stacks, out_shape,
                     [pltpu.SemaphoreType.DMA((T,)), pltpu.SemaphoreType.DMA((T,)), pltpu.SemaphoreType.DMA((T,))])
    return outs[0::2], outs[1::2]


def _scatter_to_chips(name, stacks):
    T = len(stacks)

    def body(*refs):
        ins, outs = refs[:T], refs[T:3 * T]
        send, recv, loc = refs[3 * T:]
        x, y, c, q, sibling = _place()
        copies = []
        for i in range(T):
            lc = pltpu.make_async_copy(ins[i].at[q], outs[2 * i], loc.at[i])
            lc.start()
            copies.append(lc)
            for t in range(1, N_CHIPS):
                rc = _remote(ins[i].at[jnp.bitwise_xor(q, t)], outs[2 * i + 1].at[t - 1], send.at[i, t - 1],
                             recv.at[i, t - 1], _chip_peer(x, y, c, t))
                rc.start()
                copies.append(rc)
        for cp in copies:
            cp.wait()

    out_shape = []
    for s in stacks:
        out_shape += [jax.ShapeDtypeStruct(s.shape[1:], s.dtype),
                      jax.ShapeDtypeStruct((N_CHIPS - 1,) + s.shape[1:], s.dtype)]
    outs = _hbm_call(name, body, stacks, out_shape,
                     [pltpu.SemaphoreType.DMA((T, N_CHIPS - 1)), pltpu.SemaphoreType.DMA((T, N_CHIPS - 1)),
                      pltpu.SemaphoreType.DMA((T,))])
    return outs[0::2], outs[1::2]


def _join_with_sibling(name, halves):
    T = len(halves)

    def body(*refs):
        ins, outs = refs[:T], refs[T:2 * T]
        send, recv, loc = refs[2 * T:]
        x, y, c, q, sibling = _place()
        locals_, sends = [], []
        for i in range(T):
            n0 = outs[i].shape[0]
            lc = pltpu.make_async_copy(ins[i], outs[i].at[_half(n0, c)], loc.at[i])
            lc.start()
            rc = _remote(ins[i], outs[i].at[_half(n0, c)], send.at[i], recv.at[i], sibling)
            rc.start()
            locals_.append(lc)
            sends.append(rc)
        for i in range(T):
            n0 = outs[i].shape[0]
            theirs = outs[i].at[_half(n0, 1 - c)]
            _remote(ins[i], theirs, send.at[i], recv.at[i], sibling).wait_recv()
        for cp in sends:
            cp.wait_send()
        for cp in locals_:
            cp.wait()

    return _hbm_call(name, body, halves,
                     [jax.ShapeDtypeStruct((2 * h.shape[0],) + h.shape[1:], h.dtype) for h in halves],
                     [pltpu.SemaphoreType.DMA((T,)), pltpu.SemaphoreType.DMA((T,)), pltpu.SemaphoreType.DMA((T,))])


def _gather_from_all(name, block):
    R, C = block.shape

    def body(src, out, send_sems, recv_sems):
        x, y, c = lax.axis_index("x"), lax.axis_index("y"), lax.axis_index("c")
        me = 4 * x + 2 * y + c
        out[me] = src[...]
        copies = []
        for r in range(1, N_DEV):
            peer = (jnp.bitwise_xor(x, r >> 2), jnp.bitwise_xor(y, (r >> 1) & 1), jnp.bitwise_xor(c, r & 1))
            cp = pltpu.make_async_remote_copy(src_ref=src, dst_ref=out.at[me], send_sem=send_sems.at[r - 1],
                                              recv_sem=recv_sems.at[r - 1], device_id=peer, device_id_type=MESH)
            cp.start()
            copies.append(cp)
        for cp in copies:
            cp.wait()

    return pl.pallas_call(
        body, name=name, out_shape=jax.ShapeDtypeStruct((N_DEV, R, C), block.dtype),
        in_specs=[pl.BlockSpec(memory_space=pltpu.VMEM)], out_specs=pl.BlockSpec(memory_space=pltpu.VMEM),
        scratch_shapes=[pltpu.SemaphoreType.DMA((N_DEV - 1,)), pltpu.SemaphoreType.DMA((N_DEV - 1,))],
    )(block)


def _view(a):
    return a[0] if a.shape[0] == 1 else a


def _view_axis(a, axis):
    return axis - 1 if a.shape[0] == 1 else axis


def _rows(a):
    return a.reshape(-1, a.shape[-1])


def _elementwise(name, fn, ins, out_dtypes, tm):
    specs = []
    for a in ins:
        a, row0 = a if isinstance(a, tuple) else (a, 0)
        specs.append((_rows(a), a.shape[-1], 0, row0))
    shape = ins[0][0].shape if isinstance(ins[0], tuple) else ins[0].shape
    outs = _rowwise(name, fn, specs, [], [(shape[-1], dt) for dt in out_dtypes], tm=tm, n_rows=math.prod(shape[:-1]))
    return outs.reshape(shape) if len(out_dtypes) == 1 else [o.reshape(shape) for o in outs]


SMALL_ROWS = 8
CONV_ROWS = CONV_WIDTH * DN_QKV // D_MODEL
SMALL_GRAD_ROWS = 24


def _pack_small(vals, conv=None):
    tail = jnp.concatenate([vals["attn_q_gain"].reshape(-1), vals["attn_k_gain"].reshape(-1),
                            vals["dn_a_log"].reshape(-1), vals["dn_dt_bias"].reshape(-1),
                            vals["dn_o_gain"].reshape(-1)])
    tail = jnp.pad(tail, (0, D_MODEL - tail.shape[0])).reshape(1, D_MODEL)
    rows = [vals["mix_norm"], vals["mlp_norm"], vals["ple_norm"], tail, jnp.zeros((1, D_MODEL), F32)]
    if conv is not None:
        rows += [conv.reshape(CONV_ROWS, D_MODEL),
                 jnp.zeros((SMALL_GRAD_ROWS - SMALL_ROWS - CONV_ROWS, D_MODEL), F32)]
    return jnp.concatenate(rows, axis=0)


def _unpack_small(block):
    nq = 3 * A_HEAD_DIM
    t = block[6]
    return dict(
        mix_norm=block[0:2], mlp_norm=block[2:4], ple_norm=block[4:6],
        attn_q_gain=t[:nq].reshape(1, 3, A_HEAD_DIM), attn_k_gain=t[nq:2 * nq].reshape(1, 3, A_HEAD_DIM),
        dn_a_log=t[2 * nq:2 * nq + DN_HEADS].reshape(1, DN_HEADS),
        dn_dt_bias=t[2 * nq + DN_HEADS:2 * nq + 2 * DN_HEADS].reshape(1, DN_HEADS),
        dn_o_gain=t[2 * nq + 2 * DN_HEADS:2 * nq + 2 * DN_HEADS + DN_HEAD_DIM].reshape(1, DN_HEAD_DIM))


def kernel(x, p, positions, mix_norm, attn_w_qkv, attn_q_gain, attn_k_gain, attn_w_o, dn_w_in, dn_conv, dn_a_log, dn_dt_bias, dn_o_gain, dn_w_o, mlp_norm, w_up, w_down, ple_norm, w_ple, w_ple_gate, loss_target, m_mix_norm, m_attn_w_qkv, m_attn_q_gain, m_attn_k_gain, m_attn_w_o, m_dn_w_in, m_dn_conv, m_dn_a_log, m_dn_dt_bias, m_dn_o_gain, m_dn_w_o, m_mlp_norm, m_w_up, m_w_down, m_ple_norm, m_w_ple, m_w_ple_gate, v_mix_norm, v_attn_w_qkv, v_attn_q_gain, v_attn_k_gain, v_attn_w_o, v_dn_w_in, v_dn_conv, v_dn_a_log, v_dn_dt_bias, v_dn_o_gain, v_dn_w_o, v_mlp_norm, v_w_up, v_w_down, v_ple_norm, v_w_ple, v_w_ple_gate):
    given = dict(locals())
    w = {n: given[n] for n in WEIGHTS}
    m = {n: given["m_" + n] for n in WEIGHTS}
    v = {n: given["v_" + n] for n in WEIGHTS}
    kinds = ("grad", "delta", "new_m", "new_v")
    big_names = [(n, _view_axis(w[n], axis)) for n, axis in SHARDED if n != "dn_conv"]

    gathered = _gather_shards("gather_weights", [_view(w[n]).astype(BF16) for n, _ in big_names])
    full = {n: jnp.concatenate([g[q] for q in range(N_CHIPS)], axis=axis) for (n, axis), g in zip(big_names, gathered)}
    conv_block = jnp.pad(w["dn_conv"].reshape(-1), (0, SMALL_ROWS * D_MODEL - w["dn_conv"].size))
    conv_all = _gather_from_all("gather_conv", conv_block.reshape(SMALL_ROWS, D_MODEL))
    conv_all = conv_all.reshape(N_CHIPS, 2, -1)[:, 0, :w["dn_conv"].size]
    conv_full = jnp.concatenate([conv_all[q].reshape(CONV_WIDTH, -1) for q in range(N_CHIPS)], axis=1)

    w_in = full["dn_w_in"]
    n_ab = 2 * DN_HEADS
    big = dict(full)
    big["dn_w_in"] = jnp.concatenate([w_in[:, DN_QKV + n_ab:], w_in[:, :DN_QKV + n_ab],
                                      jnp.zeros((D_MODEL, DN_AB_PAD - n_ab), BF16)], axis=1)
    small = {n: w[n] for n in REPLICATED}
    small["dn_conv"] = conv_full

    sq, grad_x, grads = _local_step(x[0], p[:, 0], positions[0], loss_target[0], small, big)
    loss = lax.psum(0.5 * sq[0, 0] / D_MODEL, ("x", "y", "c"))
    out = {}

    stacks = [jnp.stack(jnp.split(grads[n], N_CHIPS, axis=axis)) for n, axis in big_names]
    mine, theirs = _split_with_sibling("split_core_grads", stacks)
    chip_sums = [_elementwise(f"add_core_{n}", lambda a, b: a.astype(F32) + b.astype(F32), [a, b], [BF16], 128)
                 for (n, _), a, b in zip(big_names, mine, theirs)]
    own, landed = _scatter_to_chips("scatter_grads", chip_sums)
    half_sums = []
    for (n, _), o, r in zip(big_names, own, landed):
        per = math.prod(o.shape[:-1])
        half_sums.append(_elementwise(
            f"add_chips_{n}", lambda a, b, c, d: ((a.astype(F32) + b.astype(F32)) + c.astype(F32)) + d.astype(F32),
            [o, (r, 0), (r, per), (r, 2 * per)], [F32], 128))
    totals = _join_with_sibling("join_core_sums", half_sums)
    for (n, _), g in zip(big_names, totals):
        shp = w[n].shape
        res = _elementwise(f"adamw_{n}", lambda g, w_, m_, v_: (g,) + _adamw(w_, g, m_, v_),
                           [g.reshape(shp), w[n], m[n], v[n]], [F32] * 4, 256)
        for kind, arr in zip(kinds, res):
            out[kind + "_" + n] = arr.reshape(shp)

    slots = _gather_from_all("gather_small_grads", _pack_small(grads, grads["dn_conv"]))

    def small_body(s_ref, w_ref, m_ref, v_ref, sum_out, g_out, d_out, m_out, v_out):
        total = s_ref[0]
        for d in range(1, N_DEV):
            total = total + s_ref[d]
        sum_out[...] = total
        g = total[:SMALL_ROWS]
        for o, r in zip((g_out, d_out, m_out, v_out), (g,) + _adamw(w_ref[...], g, m_ref[...], v_ref[...])):
            o[...] = r

    res = pl.pallas_call(small_body, name="adamw_replicated",
                         out_shape=[jax.ShapeDtypeStruct((SMALL_GRAD_ROWS, D_MODEL), F32)]
                         + [jax.ShapeDtypeStruct((SMALL_ROWS, D_MODEL), F32)] * 4)(
        slots, _pack_small(w), _pack_small(m), _pack_small(v))
    for kind, block in zip(kinds, res[1:]):
        for n, arr in _unpack_small(block).items():
            out[kind + "_" + n] = arr
    conv_sum = res[0][SMALL_ROWS:SMALL_ROWS + CONV_ROWS].reshape(CONV_WIDTH, DN_QKV)
    cols = DN_QKV // N_CHIPS
    chip = 2 * lax.axis_index("x") + lax.axis_index("y")
    conv_mine = lax.dynamic_slice_in_dim(conv_sum, chip * cols, cols, axis=1)
    res = _elementwise("adamw_dn_conv", lambda g, w_, m_, v_: (g,) + _adamw(w_, g, m_, v_),
                       [conv_mine, w["dn_conv"][0], m["dn_conv"][0], v["dn_conv"][0]], [F32] * 4, CONV_WIDTH)
    for kind, arr in zip(kinds, res):
        out[kind + "_dn_conv"] = arr[None]

    return (loss, grad_x[None],
            *[out["grad_" + n] for n in WEIGHTS], *[out["delta_" + n] for n in WEIGHTS],
            *[out["new_m_" + n] for n in WEIGHTS], *[out["new_v_" + n] for n in WEIGHTS])
```

```python
import functools
import math

import jax
import jax.numpy as jnp
from jax import lax
from jax.experimental import pallas as pl
from jax.experimental.pallas import tpu as pltpu

F32 = jnp.float32
BF16 = jnp.bfloat16
HIGHEST = lax.Precision.HIGHEST

D_MODEL = 1024
EPS = 1e-6
SWA_GROUPS = ((128, 1), (512, 4), (2048, 16))
A_HEADS = 8
A_HEAD_DIM = 64
A_WIDTH = A_HEADS * A_HEAD_DIM
ROPE_DIM = A_HEAD_DIM // 4
ROPE_THETA = 500000.0
BAND = 128
DN_HEADS = 8
DN_HEAD_DIM = 128
DN_WIDTH = DN_HEADS * DN_HEAD_DIM
DN_QKV = 3 * DN_WIDTH
DN_AB_PAD = 128
DN_IN_PAD = DN_WIDTH + DN_QKV + DN_AB_PAD
DN_QKV0 = DN_WIDTH
DN_AB0 = DN_WIDTH + DN_QKV
DN_HB = 8
CONV_WIDTH = 4
CHUNK = 64
PLE_DIM = 256
D_FF = 4 * D_MODEL

ADAM_LR = 0.001
ADAM_B1 = 0.9
ADAM_B2 = 0.999
ADAM_EPS = 1e-08
ADAM_WD = 0.01
ADAM_STEP = 10

N_CHIPS = 4
N_DEV = 8
VMEM_LIMIT = 48 * 1024 * 1024
MESH = pl.DeviceIdType.MESH

SHARDED = (
    ("attn_w_qkv", 2), ("attn_w_o", 2), ("dn_w_in", 2), ("dn_conv", 2), ("dn_w_o", 1),
    ("w_up", 2), ("w_down", 1), ("w_ple", 2), ("w_ple_gate", 1))
REPLICATED = ("mix_norm", "attn_q_gain", "attn_k_gain", "dn_a_log", "dn_dt_bias", "dn_o_gain",
              "mlp_norm", "ple_norm")
WEIGHTS = ("mix_norm", "attn_w_qkv", "attn_q_gain", "attn_k_gain", "attn_w_o", "dn_w_in", "dn_conv",
           "dn_a_log", "dn_dt_bias", "dn_o_gain", "dn_w_o", "mlp_norm", "w_up", "w_down", "ple_norm",
           "w_ple", "w_ple_gate")


def _cparams(sem=None):
    return pltpu.CompilerParams(dimension_semantics=sem, vmem_limit_bytes=VMEM_LIMIT)


def _pick(n, cap, quantum=128):
    best = None
    for t in range(quantum, min(n, cap) + 1, quantum):
        if n % t == 0:
            best = t
    return n if best is None else best


_DIMS = {"nn": ((1,), (0,)), "nt": ((1,), (1,)), "tn": ((0,), (0,))}


def _mm(name, a, b, mode, out_dtypes=(F32,), extras=(), epilogue=None):
    if mode == "nn":
        (M, K), (K2, N) = a.shape, b.shape
    elif mode == "nt":
        (M, K), (N, K2) = a.shape, b.shape
    else:
        (K, M), (K2, N) = a.shape, b.shape
    assert K == K2, (name, a.shape, b.shape)
    tm, tn, tk = _pick(M, 512), _pick(N, 1536), _pick(K, 1024 if mode == "tn" else 1536)
    nk = K // tk
    if mode == "nn":
        a_spec = pl.BlockSpec((tm, tk), lambda i, j, k: (i, k))
        b_spec = pl.BlockSpec((tk, tn), lambda i, j, k: (k, j))
    elif mode == "nt":
        a_spec = pl.BlockSpec((tm, tk), lambda i, j, k: (i, k))
        b_spec = pl.BlockSpec((tn, tk), lambda i, j, k: (j, k))
    else:
        a_spec = pl.BlockSpec((tk, tm), lambda i, j, k: (k, i))
        b_spec = pl.BlockSpec((tk, tn), lambda i, j, k: (k, j))
    o_spec = pl.BlockSpec((tm, tn), lambda i, j, k: (i, j))
    n_extra, n_out = len(extras), len(out_dtypes)
    dims = (_DIMS[mode], ((), ()))

    def body(a_ref, b_ref, *rest):
        extra_refs, out_refs, acc = rest[:n_extra], rest[n_extra:n_extra + n_out], rest[-1]
        k = pl.program_id(2)

        @pl.when(k == 0)
        def _():
            acc[...] = jnp.zeros_like(acc)

        acc[...] += lax.dot_general(a_ref[...].astype(BF16), b_ref[...].astype(BF16), dims,
                                    preferred_element_type=F32)

        @pl.when(k == nk - 1)
        def _():
            if epilogue is None:
                vals = (acc[...],)
            else:
                vals = epilogue(acc[...], *[e[...] for e in extra_refs])
            for o, v in zip(out_refs, vals):
                o[...] = v.astype(o.dtype)

    outs = pl.pallas_call(
        body, name=name, grid=(M // tm, N // tn, nk),
        in_specs=[a_spec, b_spec] + [o_spec] * n_extra,
        out_specs=[o_spec] * n_out,
        out_shape=[jax.ShapeDtypeStruct((M, N), dt) for dt in out_dtypes],
        scratch_shapes=[pltpu.VMEM((tm, tn), F32)],
        compiler_params=_cparams(("parallel", "parallel", "arbitrary")),
    )(a, b, *extras)
    return outs[0] if n_out == 1 else outs


def _rowwise(name, fn, rows, bcast, row_outs, acc_outs=(), tm=256, n_rows=None):
    rows = [r if isinstance(r, tuple) else (r, r.shape[1], 0) for r in rows]
    rows = [r if len(r) == 4 else r + (0,) for r in rows]
    S = rows[0][0].shape[0] if n_rows is None else n_rows
    tm = min(tm, S)
    assert S % tm == 0 and all(r[3] % tm == 0 for r in rows), (name, S, tm)
    n_row, n_bc, n_ro, n_acc = len(rows), len(bcast), len(row_outs), len(acc_outs)
    in_specs = [pl.BlockSpec((tm, w), functools.partial(lambda i, cb, rb: (i + rb, cb), cb=cb, rb=r0 // tm))
                for _, w, cb, r0 in rows]
    in_specs += [pl.BlockSpec(b.shape, lambda i: (0, 0)) for b in bcast]
    out_specs = [pl.BlockSpec((tm, c), lambda i: (i, 0)) for c, _ in row_outs]
    out_specs += [pl.BlockSpec(s, lambda i: (0, 0)) for s in acc_outs]
    out_shape = [jax.ShapeDtypeStruct((S, c), dt) for c, dt in row_outs]
    out_shape += [jax.ShapeDtypeStruct(s, F32) for s in acc_outs]

    def body(*refs):
        ins = [r[...] for r in refs[:n_row + n_bc]]
        outs = refs[n_row + n_bc:]
        vals = fn(*ins)
        if not isinstance(vals, (tuple, list)):
            vals = (vals,)
        for o, v in zip(outs[:n_ro], vals[:n_ro]):
            o[...] = v.astype(o.dtype)
        if n_acc:
            @pl.when(pl.program_id(0) == 0)
            def _():
                for o in outs[n_ro:]:
                    o[...] = jnp.zeros_like(o)
            for o, v in zip(outs[n_ro:], vals[n_ro:]):
                o[...] += v

    outs = pl.pallas_call(
        body, name=name, grid=(S // tm,), in_specs=in_specs, out_specs=out_specs, out_shape=out_shape,
        compiler_params=_cparams(("arbitrary",) if n_acc else ("parallel",)),
    )(*[r[0] for r in rows], *bcast)
    return outs[0] if len(outs) == 1 else outs


def _sigmoid(x):
    return 1.0 / (1.0 + jnp.exp(-x))


def _silu(x):
    return x * _sigmoid(x)


def _softplus(x):
    return jnp.maximum(x, 0.0) + jnp.log(1.0 + jnp.exp(-jnp.abs(x)))


def _rms_fwd_fn(x, g):
    r = lax.rsqrt(jnp.mean(x * x, axis=-1, keepdims=True) + EPS)
    return (x * r) * g


def _rms_bwd_fn(x, dres, *rest):
    dh, g = sum(rest[:-1]), rest[-1]
    r = lax.rsqrt(jnp.mean(x * x, axis=-1, keepdims=True) + EPS)
    xh = x * r
    dxh = dh * g
    dx = r * (dxh - xh * jnp.mean(dxh * xh, axis=-1, keepdims=True))
    return dres + dx, jnp.sum(dh * xh, axis=0, keepdims=True)


def _rms_fwd(name, x, gain):
    return _rowwise(name, _rms_fwd_fn, [x], [gain.reshape(1, -1)], [(x.shape[1], BF16)])


def _rms_bwd(name, x, gain, dres, dhs):
    return _rowwise(name, _rms_bwd_fn, [x, dres] + list(dhs), [gain.reshape(1, -1)],
                    [(x.shape[1], F32)], [(1, x.shape[1])])


def _relu2_epilogue(acc):
    r = jnp.maximum(acc, 0.0)
    return acc, r * r


def _relu2_bwd_epilogue(acc, u):
    return (acc * (2.0 * jnp.maximum(u, 0.0)),)


def _ple_fwd_fn(x, pp, zg):
    return x + pp * _sigmoid(zg)


def _ple_bwd_fn(dx, pp, zg):
    gate = _sigmoid(zg)
    return dx * gate, dx * pp * gate * (1.0 - gate)


def _loss_fn(y, t):
    err = y - t
    return err * (1.0 / D_MODEL), jnp.broadcast_to(jnp.sum(err * err, keepdims=True), (1, 128))


def _adamw(w, g, m, v):
    m = ADAM_B1 * m + (1.0 - ADAM_B1) * g
    v = ADAM_B2 * v + (1.0 - ADAM_B2) * jnp.square(g)
    m_hat = m / (1.0 - ADAM_B1 ** ADAM_STEP)
    v_hat = v / (1.0 - ADAM_B2 ** ADAM_STEP)
    delta = -ADAM_LR * (m_hat / (jnp.sqrt(v_hat) + ADAM_EPS) + ADAM_WD * w)
    return delta, m, v


def _lane_take(x, offset):
    n = x.shape[-1]
    return pltpu.roll(x, (-offset) % n, 1)


def _head_lane(shape):
    return lax.broadcasted_iota(jnp.int32, shape, 1) % A_HEAD_DIM


def _rope_partner(x):
    lane = _head_lane(x.shape)
    return jnp.where(lane < ROPE_DIM // 2, _lane_take(x, ROPE_DIM // 2),
                     jnp.where(lane < ROPE_DIM, _lane_take(x, -(ROPE_DIM // 2)), 0.0))


def _head_mean(x, bd):
    return jnp.dot(x, bd, precision=HIGHEST, preferred_element_type=F32)


def _fold_heads(row):
    out = row[:, :A_HEAD_DIM]
    for h in range(1, A_HEADS):
        out = out + row[:, h * A_HEAD_DIM:(h + 1) * A_HEAD_DIM]
    return out


def _all_heads(t):
    return jnp.concatenate([t] * (A_WIDTH // t.shape[1]), axis=1)


def _qk_prep_fwd_fn(qkv, ct, st, gq, gk, bd):
    ct, st = _all_heads(ct), _all_heads(st)

    def one(t, g):
        n = t * lax.rsqrt(_head_mean(t * t, bd) + EPS) * g
        return n * ct + _rope_partner(n) * st
    q, k, v = qkv[:, :A_WIDTH], qkv[:, A_WIDTH:2 * A_WIDTH], qkv[:, 2 * A_WIDTH:]
    return one(q, gq), one(k, gk), v


def _qk_prep_bwd_fn(qkv, ct, st, dq, dk, dv, gq, gk, bd):
    ct, st = _all_heads(ct), _all_heads(st)

    def one(t, g, dy):
        r = lax.rsqrt(_head_mean(t * t, bd) + EPS)
        nh = t * r
        dn = dy * ct + _rope_partner(dy * st)
        dg = jnp.sum(dn * nh, axis=0, keepdims=True)
        dnh = dn * g
        return r * (dnh - nh * _head_mean(dnh * nh, bd)), _fold_heads(dg)
    q, k = qkv[:, :A_WIDTH], qkv[:, A_WIDTH:2 * A_WIDTH]
    dq_raw, dgq = one(q, gq, dq)
    dk_raw, dgk = one(k, gk, dk)
    return jnp.concatenate([dq_raw, dk_raw, dv], axis=1), dgq, dgk


def _band_masks():
    qi = lax.broadcasted_iota(jnp.int32, (BAND, BAND), 0)
    kj = lax.broadcasted_iota(jnp.int32, (BAND, BAND), 1)
    return kj <= qi, kj >= qi


def _attn_fwd(name, q, k, v, blocks_per_class):
    S = q.shape[0]
    nblk = S // BAND
    scale = A_HEAD_DIM ** -0.5

    def body(q_ref, kp_ref, kc_ref, vp_ref, vc_ref, o_ref, l_ref):
        i = pl.program_id(0)
        has_prev = (i % blocks_per_class) != 0
        m_cur, m_prev = _band_masks()
        m_prev = jnp.logical_and(m_prev, has_prev)
        for h in range(A_HEADS):
            sl = slice(h * A_HEAD_DIM, (h + 1) * A_HEAD_DIM)
            qh = q_ref[:, sl]
            s_c = lax.dot_general(qh, kc_ref[:, sl], (_DIMS["nt"], ((), ())), preferred_element_type=F32) * scale
            s_p = lax.dot_general(qh, kp_ref[:, sl], (_DIMS["nt"], ((), ())), preferred_element_type=F32) * scale
            s_c = jnp.where(m_cur, s_c, -jnp.inf)
            s_p = jnp.where(m_prev, s_p, -jnp.inf)
            m = jnp.maximum(jnp.max(s_c, axis=-1, keepdims=True), jnp.max(s_p, axis=-1, keepdims=True))
            e_c, e_p = jnp.exp(s_c - m), jnp.exp(s_p - m)
            l = jnp.sum(e_c, axis=-1, keepdims=True) + jnp.sum(e_p, axis=-1, keepdims=True)
            o = jnp.dot((e_c / l).astype(BF16), vc_ref[:, sl], preferred_element_type=F32)
            o = o + jnp.dot((e_p / l).astype(BF16), vp_ref[:, sl], preferred_element_type=F32)
            o_ref[:, sl] = o
            l_ref[:, sl] = jnp.broadcast_to(m + jnp.log(l), (BAND, A_HEAD_DIM))

    cur = pl.BlockSpec((BAND, A_WIDTH), lambda i: (i, 0))
    prev = pl.BlockSpec((BAND, A_WIDTH), lambda i: (jnp.maximum(i - 1, 0), 0))
    return pl.pallas_call(
        body, name=name, grid=(nblk,), in_specs=[cur, prev, cur, prev, cur], out_specs=[cur, cur],
        out_shape=[jax.ShapeDtypeStruct((S, A_WIDTH), F32)] * 2,
        compiler_params=_cparams(("parallel",)),
    )(q, k, k, v, v)


def _attn_bwd(name, q, k, v, o, lse, do, dlse, blocks_per_class):
    S = q.shape[0]
    nblk = S // BAND
    scale = A_HEAD_DIM ** -0.5

    def body(q_ref, kp_ref, kc_ref, vp_ref, vc_ref, o_ref, l_ref, do_ref, dl_ref,
             dq_ref, dk_ref, dv_ref, ck, cv):
        i = pl.program_id(0)

        @pl.when(i == 0)
        def _():
            ck[...] = jnp.zeros_like(ck)
            cv[...] = jnp.zeros_like(cv)

        @pl.when(i == nblk)
        def _():
            dk_ref[...] = ck[...]
            dv_ref[...] = cv[...]

        @pl.when(i < nblk)
        def _():
            has_prev = (i % blocks_per_class) != 0
            m_cur, m_prev = _band_masks()
            m_prev = jnp.logical_and(m_prev, has_prev)
            nt, tn = (_DIMS["nt"], ((), ())), (_DIMS["tn"], ((), ()))
            for h in range(A_HEADS):
                sl = slice(h * A_HEAD_DIM, (h + 1) * A_HEAD_DIM)
                qh, kc, kp, vc, vp = q_ref[:, sl], kc_ref[:, sl], kp_ref[:, sl], vc_ref[:, sl], vp_ref[:, sl]
                doh = do_ref[:, sl]
                lse_h = l_ref[:, sl][:, :1]
                s_c = lax.dot_general(qh, kc, nt, preferred_element_type=F32) * scale
                s_p = lax.dot_general(qh, kp, nt, preferred_element_type=F32) * scale
                p_c = jnp.where(m_cur, jnp.exp(s_c - lse_h), 0.0)
                p_p = jnp.where(m_prev, jnp.exp(s_p - lse_h), 0.0)
                corr = (jnp.sum(dl_ref[:, sl], axis=-1, keepdims=True)
                        - jnp.sum(doh * o_ref[:, sl], axis=-1, keepdims=True))
                dob = doh.astype(BF16)
                dp_c = lax.dot_general(dob, vc, nt, preferred_element_type=F32)
                dp_p = lax.dot_general(dob, vp, nt, preferred_element_type=F32)
                ds_c = (p_c * (dp_c + corr)).astype(BF16)
                ds_p = (p_p * (dp_p + corr)).astype(BF16)
                dq = jnp.dot(ds_c, kc, preferred_element_type=F32) + jnp.dot(ds_p, kp, preferred_element_type=F32)
                dq_ref[:, sl] = dq * scale
                dk_ref[:, sl] = ck[:, sl] + lax.dot_general(ds_p, qh, tn, preferred_element_type=F32) * scale
                dv_ref[:, sl] = cv[:, sl] + lax.dot_general(p_p.astype(BF16), dob, tn, preferred_element_type=F32)
                ck[:, sl] = lax.dot_general(ds_c, qh, tn, preferred_element_type=F32) * scale
                cv[:, sl] = lax.dot_general(p_c.astype(BF16), dob, tn, preferred_element_type=F32)

    last = nblk - 1
    cur = pl.BlockSpec((BAND, A_WIDTH), lambda i: (jnp.minimum(i, last), 0))
    prev = pl.BlockSpec((BAND, A_WIDTH), lambda i: (jnp.minimum(jnp.maximum(i - 1, 0), last), 0))
    return pl.pallas_call(
        body, name=name, grid=(nblk + 1,),
        in_specs=[cur, prev, cur, prev, cur, cur, cur, cur, cur], out_specs=[cur, prev, prev],
        out_shape=[jax.ShapeDtypeStruct((S, A_WIDTH), F32)] * 3,
        scratch_shapes=[pltpu.VMEM((BAND, A_WIDTH), F32)] * 2,
        compiler_params=_cparams(("arbitrary",)),
    )(q, k, k, v, v, o, lse, do, dlse)


def _merge_fwd_fn(o0, o1, o2, l0, l1, l2):
    m = jnp.maximum(jnp.maximum(l0, l1), l2)
    e0, e1, e2 = jnp.exp(l0 - m), jnp.exp(l1 - m), jnp.exp(l2 - m)
    return (e0 * o0 + e1 * o1 + e2 * o2) / (e0 + e1 + e2)


def _merge_bwd_fn(o0, o1, o2, l0, l1, l2, dom):
    m = jnp.maximum(jnp.maximum(l0, l1), l2)
    e0, e1, e2 = jnp.exp(l0 - m), jnp.exp(l1 - m), jnp.exp(l2 - m)
    den = e0 + e1 + e2
    w0, w1, w2 = e0 / den, e1 / den, e2 / den
    dw0, dw1, dw2 = dom * o0, dom * o1, dom * o2
    mean = w0 * dw0 + w1 * dw1 + w2 * dw2
    return w0 * dom, w1 * dom, w2 * dom, w0 * (dw0 - mean), w1 * (dw1 - mean), w2 * (dw2 - mean)


def _to_classes(t, d):
    if d == 1:
        return t
    S, C = t.shape
    return t.reshape(S // d, d, C).transpose(1, 0, 2).reshape(S, C)


def _from_classes(t, d):
    if d == 1:
        return t
    S, C = t.shape
    return t.reshape(d, S // d, C).transpose(1, 0, 2).reshape(S, C)


def _rope_lane_tables(positions):
    inv_freq = ROPE_THETA ** (-jnp.arange(0, ROPE_DIM, 2, dtype=F32) / ROPE_DIM)
    ang = positions.astype(F32)[:, None] * inv_freq
    cos, sin = jnp.cos(ang), jnp.sin(ang)
    S = positions.shape[0]
    rest = A_HEAD_DIM - ROPE_DIM
    ct = jnp.concatenate([cos, cos, jnp.ones((S, rest), F32)], axis=1)
    st = jnp.concatenate([-sin, sin, jnp.zeros((S, rest), F32)], axis=1)
    return jnp.tile(ct, (1, 2)), jnp.tile(st, (1, 2))


def _head_mean_matrix():
    r = jnp.arange(A_WIDTH) // A_HEAD_DIM
    return (r[:, None] == r[None, :]).astype(F32) * (1.0 / A_HEAD_DIM)


def _conv_fwd(name, proj, w):
    S = proj.shape[0]
    tm, tc = min(512, S), 1024
    per8 = tm // 8
    off = DN_QKV0 // tc

    def body(x_ref, halo_ref, w_ref, o_ref, xs):
        i = pl.program_id(0)
        xs[0:8, :] = jnp.where(i > 0, halo_ref[...], 0.0)
        xs[8:, :] = x_ref[...]
        acc = w_ref[0:1, :] * xs[pl.ds(8 - 3, tm), :]
        for j in range(1, CONV_WIDTH):
            acc = acc + w_ref[j:j + 1, :] * xs[pl.ds(8 - 3 + j, tm), :]
        o_ref[...] = acc

    return pl.pallas_call(
        body, name=name, grid=(S // tm, DN_QKV // tc),
        in_specs=[pl.BlockSpec((tm, tc), lambda i, j: (i, j + off)),
                  pl.BlockSpec((8, tc), lambda i, j: (jnp.maximum(i * per8 - 1, 0), j + off)),
                  pl.BlockSpec((CONV_WIDTH, tc), lambda i, j: (0, j))],
        out_specs=pl.BlockSpec((tm, tc), lambda i, j: (i, j)),
        out_shape=jax.ShapeDtypeStruct((S, DN_QKV), F32),
        scratch_shapes=[pltpu.VMEM((tm + 8, tc), F32)],
        compiler_params=_cparams(("parallel", "parallel")),
    )(proj, proj, w)


def _conv_bwd(name, proj, dpre, w):
    S = proj.shape[0]
    tm, tc = min(512, S), 1024
    per8 = tm // 8
    off = DN_QKV0 // tc
    last8 = S // 8 - 1
    nrow = S // tm

    def body(x_ref, xh_ref, d_ref, dh_ref, w_ref, dx_ref, dw_ref, xs, ds):
        i = pl.program_id(1)
        xs[0:8, :] = jnp.where(i > 0, xh_ref[...], 0.0)
        xs[8:, :] = x_ref[...]
        ds[0:tm, :] = d_ref[...]
        ds[tm:, :] = jnp.where(i < nrow - 1, dh_ref[...], 0.0)
        d = d_ref[...]
        acc = w_ref[0:1, :] * ds[pl.ds(3, tm), :]
        for j in range(1, CONV_WIDTH):
            acc = acc + w_ref[j:j + 1, :] * ds[pl.ds(3 - j, tm), :]
        dx_ref[...] = acc

        @pl.when(i == 0)
        def _():
            dw_ref[...] = jnp.zeros_like(dw_ref)

        for j in range(CONV_WIDTH):
            dw_ref[j:j + 1, :] += jnp.sum(d * xs[pl.ds(8 - 3 + j, tm), :], axis=0, keepdims=True)

    return pl.pallas_call(
        body, name=name, grid=(DN_QKV // tc, nrow),
        in_specs=[pl.BlockSpec((tm, tc), lambda j, i: (i, j + off)),
                  pl.BlockSpec((8, tc), lambda j, i: (jnp.maximum(i * per8 - 1, 0), j + off)),
                  pl.BlockSpec((tm, tc), lambda j, i: (i, j)),
                  pl.BlockSpec((8, tc), lambda j, i: (jnp.minimum((i + 1) * per8, last8), j)),
                  pl.BlockSpec((CONV_WIDTH, tc), lambda j, i: (0, j))],
        out_specs=[pl.BlockSpec((tm, tc), lambda j, i: (i, j)),
                   pl.BlockSpec((CONV_WIDTH, tc), lambda j, i: (0, j))],
        out_shape=[jax.ShapeDtypeStruct((S, DN_QKV), F32), jax.ShapeDtypeStruct((CONV_WIDTH, DN_QKV), F32)],
        scratch_shapes=[pltpu.VMEM((tm + 8, tc), F32)] * 2,
        compiler_params=_cparams(("parallel", "arbitrary")),
    )(proj, proj, dpre, dpre, w)


def _gate_lane(shape):
    return lax.broadcasted_iota(jnp.int32, shape, 1)


GATES_ROWS = 256


def _chunk_cumsum_matrix():
    r = jnp.arange(GATES_ROWS)
    return ((r[:, None] >= r[None, :]) & (r[:, None] // CHUNK == r[None, :] // CHUNK)).astype(F32)


def _gates_fwd_fn(ab, alog, dt, cum):
    g = -jnp.exp(alog) * _softplus(ab + dt)
    gc = jnp.dot(cum, g, precision=HIGHEST, preferred_element_type=F32)
    return jnp.where(_gate_lane(ab.shape) < DN_HEADS, gc, _sigmoid(ab))


def _gates_bwd_fn(ab, dgb, alog, dt, cum):
    lane = _gate_lane(ab.shape)
    is_g = lane < DN_HEADS
    neg_a = -jnp.exp(alog)
    sp = _softplus(ab + dt)
    dsp = _sigmoid(ab + dt)
    beta = _sigmoid(ab)
    dgc = jnp.where(is_g, dgb, 0.0)
    dg = lax.dot_general(cum, dgc, (_DIMS["tn"], ((), ())), precision=HIGHEST, preferred_element_type=F32)
    dab = jnp.where(is_g, dg * neg_a * dsp, jnp.where(lane < 2 * DN_HEADS, dgb * beta * (1.0 - beta), 0.0))
    d_alog = jnp.sum(dg * neg_a * sp, axis=0, keepdims=True)
    d_dt = jnp.sum(dg * neg_a * dsp, axis=0, keepdims=True)
    return dab, d_alog, d_dt


_BATCH_DIMS = {"nn": ((2,), (1,)), "nt": ((2,), (2,)), "tn": ((1,), (1,))}


def _chunk_math(precision):
    def dg(a, b, mode, prec=precision):
        return lax.dot_general(a, b, (_BATCH_DIMS[mode], ((0,), (0,))), precision=prec,
                               preferred_element_type=F32)

    @jax.custom_vjp
    def nn(a, b):
        return dg(a, b, "nn")

    @jax.custom_vjp
    def nt(a, b):
        return dg(a, b, "nt")

    @jax.custom_vjp
    def tn(a, b):
        return dg(a, b, "tn")

    nn.defvjp(lambda a, b: (nn(a, b), (a, b)), lambda r, g: (nt(g, r[1]), tn(r[0], g)))
    nt.defvjp(lambda a, b: (nt(a, b), (a, b)), lambda r, g: (nn(g, r[1]), tn(g, r[0])))
    tn.defvjp(lambda a, b: (tn(a, b), (a, b)), lambda r, g: (nt(r[1], g), nn(r[0], g)))

    def split(x):
        hi = x.astype(BF16)
        return hi, (x - hi.astype(F32)).astype(BF16)

    def fine(a, b, mode):
        ah, al = split(a)
        bh, bl = split(b)
        return dg(ah, bh, mode, None) + (dg(ah, bl, mode, None) + dg(al, bh, mode, None))

    def unit_lower_inverse(a):
        row = lax.broadcasted_iota(jnp.int32, a.shape, 1)
        col = lax.broadcasted_iota(jnp.int32, a.shape, 2)
        x = -a
        p = jnp.where(row == col, 1.0, 0.0) + x
        for _ in range(int(math.log2(CHUNK)) - 1):
            x = fine(x, x, "nn")
            p = p + fine(p, x, "nn")
        return p

    @jax.custom_vjp
    def solve2(a, r1, r2):
        ti = unit_lower_inverse(a)
        return fine(ti, r1, "nn"), fine(ti, r2, "nn")

    def solve2_fwd(a, r1, r2):
        ti = unit_lower_inverse(a)
        s1, s2 = fine(ti, r1, "nn"), fine(ti, r2, "nn")
        return (s1, s2), (ti, s1, s2)

    def solve2_bwd(res, g):
        ti, s1, s2 = res
        d1, d2 = fine(ti, g[0], "tn"), fine(ti, g[1], "tn")
        return -(fine(d1, s1, "nt") + fine(d2, s2, "nt")), d1, d2

    solve2.defvjp(solve2_fwd, solve2_bwd)

    def chunk_fn(pq, pk, pv, z, g_col, b_col, g_row, ogain, s_in):
        nb = pq.shape[0]
        sq = (nb, CHUNK, CHUNK)
        row = lax.broadcasted_iota(jnp.int32, sq, 1)
        col = lax.broadcasted_iota(jnp.int32, sq, 2)
        lower, strict = row >= col, row > col
        q, k, v = _silu(pq), _silu(pk), _silu(pv)
        q = q * lax.rsqrt(jnp.sum(q * q, axis=-1, keepdims=True) + EPS) * (DN_HEAD_DIM ** -0.5)
        k = k * lax.rsqrt(jnp.sum(k * k, axis=-1, keepdims=True) + EPS)
        gc_wide = jnp.broadcast_to(g_col, pq.shape)
        gc_i = jnp.broadcast_to(g_col, sq)
        gc_j = jnp.broadcast_to(g_row, sq)
        is_last = lax.broadcasted_iota(jnp.int32, pq.shape, 1) == CHUNK - 1
        g_last = jnp.sum(jnp.where(is_last, gc_wide, 0.0), axis=1, keepdims=True)
        decay = jnp.exp(jnp.where(lower, gc_i - gc_j, -jnp.inf))
        kb = k * b_col
        a_mat = jnp.where(strict, nt(kb, k) * decay, 0.0)
        eg = jnp.exp(gc_wide)
        u, w = solve2(a_mat, v * b_col, kb * eg)
        attn = nt(q, k) * decay
        q_dec = q * eg
        k_dec = k * jnp.exp(g_last - gc_wide)
        c_dec = jnp.exp(g_last)
        v_new = u - nn(w, s_in)
        o = nn(q_dec, s_in) + nn(attn, v_new)
        s_out = s_in * c_dec + tn(k_dec, v_new)
        y = o * lax.rsqrt(jnp.mean(o * o, axis=-1, keepdims=True) + EPS) * ogain * _silu(z)
        return y, s_out

    return chunk_fn


DN_PRECISION = None


def _chunk_specs(n_of):
    groups = DN_HEADS // DN_HB
    wide = DN_HB * DN_HEAD_DIM
    hd = pl.BlockSpec((CHUNK, wide), lambda h, n: (n_of(n), h))
    specs = dict(
        pq=hd,
        pk=pl.BlockSpec((CHUNK, wide), lambda h, n: (n_of(n), groups + h)),
        pv=pl.BlockSpec((CHUNK, wide), lambda h, n: (n_of(n), 2 * groups + h)),
        z=hd,
        col=pl.BlockSpec((DN_HB, CHUNK, 1), lambda h, n: (h, n_of(n), 0)),
        row=pl.BlockSpec((DN_HB, None, 1, CHUNK), lambda h, n: (h, n_of(n), 0, 0)),
        gain=pl.BlockSpec((1, DN_HEAD_DIM), lambda h, n: (0, 0)),
        state=pl.BlockSpec((DN_HB, None, DN_HEAD_DIM, DN_HEAD_DIM), lambda h, n: (h, n_of(n), 0, 0)),
        head=hd,
    )
    return specs


def _head_cols(j):
    return slice(j * DN_HEAD_DIM, (j + 1) * DN_HEAD_DIM)


def _split_heads(ref):
    return jnp.stack([ref[:, _head_cols(j)] for j in range(DN_HB)])


def _chunk_fwd(name, pre, proj, g_col, b_col, g_row, ogain):
    S = pre.shape[0]
    N = S // CHUNK
    chunk_fn = _chunk_math(DN_PRECISION)
    sp = _chunk_specs(lambda n: n)

    def body(pq, pk, pv, z, gc, bc, gr, og, y_ref, sin_ref, st):
        @pl.when(pl.program_id(1) == 0)
        def _():
            st[...] = jnp.zeros_like(st)

        s_in = st[...]
        sin_ref[...] = s_in
        y, s_out = chunk_fn(_split_heads(pq), _split_heads(pk), _split_heads(pv), _split_heads(z),
                            gc[...], bc[...], gr[...], og[...], s_in)
        for j in range(DN_HB):
            y_ref[:, _head_cols(j)] = y[j].astype(y_ref.dtype)
        st[...] = s_out

    return pl.pallas_call(
        body, name=name, grid=(DN_HEADS // DN_HB, N),
        in_specs=[sp["pq"], sp["pk"], sp["pv"], sp["z"], sp["col"], sp["col"], sp["row"], sp["gain"]],
        out_specs=[sp["head"], sp["state"]],
        out_shape=[jax.ShapeDtypeStruct((S, DN_WIDTH), BF16),
                   jax.ShapeDtypeStruct((DN_HEADS, N, DN_HEAD_DIM, DN_HEAD_DIM), F32)],
        scratch_shapes=[pltpu.VMEM((DN_HB, DN_HEAD_DIM, DN_HEAD_DIM), F32)],
        compiler_params=_cparams(("parallel", "arbitrary")),
    )(pre, pre, pre, proj, g_col, b_col, g_row, ogain)


def _chunk_bwd(name, pre, proj, g_col, b_col, g_row, ogain, s_in_all, dy):
    S = pre.shape[0]
    N = S // CHUNK
    chunk_fn = _chunk_math(DN_PRECISION)
    sp = _chunk_specs(lambda n: N - 1 - n)

    def body(pq, pk, pv, z, gc, bc, gr, og, sin_ref, dy_ref,
             dq_ref, dk_ref, dv_ref, dz_ref, dgc_ref, dbc_ref, dgr_ref, dog_ref, ds):
        first = jnp.logical_and(pl.program_id(0) == 0, pl.program_id(1) == 0)

        @pl.when(pl.program_id(1) == 0)
        def _():
            ds[...] = jnp.zeros_like(ds)

        @pl.when(first)
        def _():
            dog_ref[...] = jnp.zeros_like(dog_ref)

        prim = (_split_heads(pq), _split_heads(pk), _split_heads(pv), _split_heads(z),
                gc[...], bc[...], gr[...], og[...], sin_ref[...])
        _, vjp = jax.vjp(chunk_fn, *prim)
        gq, gk, gv, gz, ggc, gbc, ggr, gog, gs = vjp((_split_heads(dy_ref), ds[...]))
        for j in range(DN_HB):
            c = _head_cols(j)
            dq_ref[:, c] = gq[j]
            dk_ref[:, c] = gk[j]
            dv_ref[:, c] = gv[j]
            dz_ref[:, c] = gz[j]
        dgc_ref[...] = ggc
        dbc_ref[...] = gbc
        dgr_ref[...] = ggr
        dog_ref[...] += gog
        ds[...] = gs

    hd = sp["head"]
    return pl.pallas_call(
        body, name=name, grid=(DN_HEADS // DN_HB, N),
        in_specs=[sp["pq"], sp["pk"], sp["pv"], sp["z"], sp["col"], sp["col"], sp["row"], sp["gain"],
                  sp["state"], hd],
        out_specs=[hd, hd, hd, hd, sp["col"], sp["col"], sp["row"], sp["gain"]],
        out_shape=[jax.ShapeDtypeStruct((S, DN_WIDTH), F32)] * 4
        + [jax.ShapeDtypeStruct((DN_HEADS, S, 1), F32)] * 2
        + [jax.ShapeDtypeStruct((DN_HEADS, N, 1, CHUNK), F32), jax.ShapeDtypeStruct((1, DN_HEAD_DIM), F32)],
        scratch_shapes=[pltpu.VMEM((DN_HB, DN_HEAD_DIM, DN_HEAD_DIM), F32)],
        compiler_params=_cparams(("arbitrary", "arbitrary")),
    )(pre, pre, pre, proj, g_col, b_col, g_row, ogain, s_in_all, dy)


def _mlp_ple_fwd(tag, x_in, p_l, norm_mlp, w_up, w_down, norm_ple, w_ple, w_gate):
    h = _rms_fwd(f"{tag}_mlp_norm", x_in, norm_mlp)
    u, a = _mm(f"{tag}_up", h, w_up, "nn", out_dtypes=(F32, BF16), epilogue=_relu2_epilogue)
    x_mid = _mm(f"{tag}_down", a, w_down, "nn", extras=(x_in,), epilogue=lambda acc, r: (acc + r,))
    hg = _rms_fwd(f"{tag}_ple_norm", x_mid, norm_ple)
    zg = _mm(f"{tag}_gate", hg, w_gate, "nn")
    pp = _mm(f"{tag}_ple", p_l, w_ple, "nn")
    x_out = _rowwise(f"{tag}_ple_out", _ple_fwd_fn, [x_mid, pp, zg], [], [(D_MODEL, F32)])
    return x_out, dict(x_in=x_in, h=h, u=u, a=a, x_mid=x_mid, hg=hg, zg=zg, pp=pp)


def _mlp_ple_bwd(tag, dx, sv, p_l, norm_mlp, w_up, w_down, norm_ple, w_ple, w_gate):
    dpp, dzg = _rowwise(f"{tag}_ple_bwd", _ple_bwd_fn, [dx, sv["pp"], sv["zg"]], [],
                        [(D_MODEL, BF16), (D_MODEL, BF16)])
    d_w_ple = _mm(f"{tag}_d_w_ple", p_l, dpp, "tn", out_dtypes=(BF16,))
    d_w_gate = _mm(f"{tag}_d_w_gate", sv["hg"], dzg, "tn", out_dtypes=(BF16,))
    dhg = _mm(f"{tag}_d_hg", dzg, w_gate, "nt")
    dx_mid, d_norm_ple = _rms_bwd(f"{tag}_ple_norm_bwd", sv["x_mid"], norm_ple, dx, [dhg])
    du = _mm(f"{tag}_d_u", dx_mid, w_down, "nt", out_dtypes=(BF16,), extras=(sv["u"],),
             epilogue=_relu2_bwd_epilogue)
    d_w_down = _mm(f"{tag}_d_w_down", sv["a"], dx_mid, "tn", out_dtypes=(BF16,))
    d_w_up = _mm(f"{tag}_d_w_up", sv["h"], du, "tn", out_dtypes=(BF16,))
    dh = _mm(f"{tag}_d_h", du, w_up, "nt")
    dx_in, d_norm_mlp = _rms_bwd(f"{tag}_mlp_norm_bwd", sv["x_in"], norm_mlp, dx_mid, [dh])
    return dx_in, dict(mlp_norm=d_norm_mlp, w_up=d_w_up, w_down=d_w_down, ple_norm=d_norm_ple,
                       w_ple=d_w_ple, w_ple_gate=d_w_gate)


def _local_step(x, p, positions, target, small, big):
    S = x.shape[0]
    ct, st = _rope_lane_tables(positions)
    bd = _head_mean_matrix()

    h0 = _rms_fwd("l0_mix_norm", x, small["mix_norm"][0])
    attn = []
    for g, (window, d) in enumerate(SWA_GROUPS):
        assert window // d == BAND and (S // d) % BAND == 0
        h0g = _to_classes(h0, d)
        ctg, stg = _to_classes(ct, d), _to_classes(st, d)
        w_g = big["attn_w_qkv"][:, g * 3 * A_WIDTH:(g + 1) * 3 * A_WIDTH]
        gq = jnp.tile(small["attn_q_gain"][0, g], A_HEADS).reshape(1, A_WIDTH)
        gk = jnp.tile(small["attn_k_gain"][0, g], A_HEADS).reshape(1, A_WIDTH)
        qkv = _mm(f"l0_qkv{g}", h0g, w_g, "nn")
        q, k, v = _rowwise(f"l0_qk_prep{g}", _qk_prep_fwd_fn, [qkv, ctg, stg], [gq, gk, bd], [(A_WIDTH, BF16)] * 3)
        o, lse = _attn_fwd(f"l0_attn{g}", q, k, v, (S // d) // BAND)
        attn.append(dict(d=d, h0g=h0g, ct=ctg, st=stg, w=w_g, gq=gq, gk=gk, qkv=qkv, q=q, k=k, v=v, o=o, lse=lse,
                         o_tok=_from_classes(o, d), lse_tok=_from_classes(lse, d)))
    om = _rowwise("l0_merge", _merge_fwd_fn, [a["o_tok"] for a in attn] + [a["lse_tok"] for a in attn], [],
                  [(A_WIDTH, BF16)])
    x1 = _mm("l0_attn_out", om, big["attn_w_o"], "nn", extras=(x,), epilogue=lambda acc, r: (acc + r,))
    x3, sv0 = _mlp_ple_fwd("l0", x1, p[0], small["mlp_norm"][0], big["w_up"][0], big["w_down"][0],
                           small["ple_norm"][0], big["w_ple"][0], big["w_ple_gate"][0])

    N = S // CHUNK
    h3 = _rms_fwd("l1_mix_norm", x3, small["mix_norm"][1])
    proj = _mm("l1_in", h3, big["dn_w_in"], "nn")
    pre = _conv_fwd("l1_conv", proj, small["dn_conv"])
    ab = proj[:, DN_AB0:DN_AB0 + DN_AB_PAD]
    lane_pad = DN_AB_PAD - DN_HEADS
    alog_row = jnp.pad(small["dn_a_log"][0], (0, lane_pad)).reshape(1, DN_AB_PAD)
    dt_row = jnp.pad(small["dn_dt_bias"][0], (0, lane_pad)).reshape(1, DN_AB_PAD)
    cum = _chunk_cumsum_matrix()
    gb = _rowwise("l1_gates", _gates_fwd_fn, [ab], [alog_row, dt_row, cum], [(DN_AB_PAD, F32)], tm=GATES_ROWS)
    g_t, b_t = gb[:, :DN_HEADS].T, gb[:, DN_HEADS:2 * DN_HEADS].T
    g_col, b_col = g_t.reshape(DN_HEADS, S, 1), b_t.reshape(DN_HEADS, S, 1)
    g_row = g_t.reshape(DN_HEADS, N, 1, CHUNK)
    ogain = small["dn_o_gain"][0].reshape(1, DN_HEAD_DIM)
    y, s_in_all = _chunk_fwd("l1_delta", pre, proj, g_col, b_col, g_row, ogain)
    x4 = _mm("l1_dn_out", y, big["dn_w_o"], "nn", extras=(x3,), epilogue=lambda acc, r: (acc + r,))
    x6, sv1 = _mlp_ple_fwd("l1", x4, p[1], small["mlp_norm"][1], big["w_up"][1], big["w_down"][1],
                           small["ple_norm"][1], big["w_ple"][1], big["w_ple_gate"][1])

    dy, sq = _rowwise("loss", _loss_fn, [x6, target], [], [(D_MODEL, F32)], [(1, 128)])

    dx4, gl1 = _mlp_ple_bwd("l1", dy, sv1, p[1], small["mlp_norm"][1], big["w_up"][1], big["w_down"][1],
                            small["ple_norm"][1], big["w_ple"][1], big["w_ple_gate"][1])
    d_y = _mm("l1_d_y", dx4, big["dn_w_o"], "nt")
    d_dn_w_o = _mm("l1_d_w_o", y, dx4, "tn", out_dtypes=(BF16,))
    dq, dk, dv, dz, dg_col, db_col, dg_row, d_ogain = _chunk_bwd(
        "l1_delta_bwd", pre, proj, g_col, b_col, g_row, ogain, s_in_all, d_y)
    dpre = jnp.concatenate([dq, dk, dv], axis=1)
    dconv_in, d_conv_w = _conv_bwd("l1_conv_bwd", proj, dpre, small["dn_conv"])
    dg_t = dg_col.reshape(DN_HEADS, S) + dg_row.reshape(DN_HEADS, S)
    dgb = jnp.pad(jnp.concatenate([dg_t, db_col.reshape(DN_HEADS, S)], axis=0).T,
                  ((0, 0), (0, DN_AB_PAD - 2 * DN_HEADS)))
    dab, d_alog, d_dt = _rowwise("l1_gates_bwd", _gates_bwd_fn, [ab, dgb], [alog_row, dt_row, cum],
                                 [(DN_AB_PAD, F32)], [(1, DN_AB_PAD), (1, DN_AB_PAD)], tm=GATES_ROWS)
    dproj = jnp.concatenate([dz.astype(BF16), dconv_in.astype(BF16), dab.astype(BF16)], axis=1)
    d_dn_w_in = _mm("l1_d_w_in", h3, dproj, "tn", out_dtypes=(BF16,))
    dh3 = _mm("l1_d_h", dproj, big["dn_w_in"], "nt")
    dx3, d_mix1 = _rms_bwd("l1_mix_norm_bwd", x3, small["mix_norm"][1], dx4, [dh3])

    dx1, gl0 = _mlp_ple_bwd("l0", dx3, sv0, p[0], small["mlp_norm"][0], big["w_up"][0], big["w_down"][0],
                            small["ple_norm"][0], big["w_ple"][0], big["w_ple_gate"][0])
    dom = _mm("l0_d_om", dx1, big["attn_w_o"], "nt")
    d_attn_w_o = _mm("l0_d_w_o", om, dx1, "tn", out_dtypes=(BF16,))
    merged = _rowwise("l0_merge_bwd", _merge_bwd_fn,
                      [a["o_tok"] for a in attn] + [a["lse_tok"] for a in attn] + [dom], [], [(A_WIDTH, F32)] * 6)
    dh0, d_w_qkv, d_gq, d_gk = [], [], [], []
    for g, a in enumerate(attn):
        do_g, dl_g = _to_classes(merged[g], a["d"]), _to_classes(merged[3 + g], a["d"])
        dqn, dkn, dvn = _attn_bwd(f"l0_attn_bwd{g}", a["q"], a["k"], a["v"], a["o"], a["lse"], do_g, dl_g,
                                  (S // a["d"]) // BAND)
        dqkv, dgq, dgk = _rowwise(f"l0_qk_prep_bwd{g}", _qk_prep_bwd_fn, [a["qkv"], a["ct"], a["st"], dqn, dkn, dvn],
                                  [a["gq"], a["gk"], bd], [(3 * A_WIDTH, BF16)], [(1, A_HEAD_DIM)] * 2)
        d_w_qkv.append(_mm(f"l0_d_w_qkv{g}", a["h0g"], dqkv, "tn", out_dtypes=(BF16,)))
        dh0.append(_from_classes(_mm(f"l0_d_h{g}", dqkv, a["w"], "nt"), a["d"]))
        d_gq.append(dgq)
        d_gk.append(dgk)
    grad_x, d_mix0 = _rms_bwd("l0_mix_norm_bwd", x, small["mix_norm"][0], dx1, dh0)

    grads = dict(
        mix_norm=jnp.concatenate([d_mix0, d_mix1], axis=0),
        attn_w_qkv=jnp.concatenate(d_w_qkv, axis=1),
        attn_q_gain=jnp.concatenate(d_gq, axis=0)[None],
        attn_k_gain=jnp.concatenate(d_gk, axis=0)[None],
        attn_w_o=d_attn_w_o,
        dn_w_in=jnp.concatenate([d_dn_w_in[:, DN_QKV0:DN_AB0 + 2 * DN_HEADS], d_dn_w_in[:, :DN_WIDTH]], axis=1),
        dn_conv=d_conv_w,
        dn_a_log=d_alog[:, :DN_HEADS],
        dn_dt_bias=d_dt[:, :DN_HEADS],
        dn_o_gain=d_ogain,
        dn_w_o=d_dn_w_o,
        mlp_norm=jnp.concatenate([gl0["mlp_norm"], gl1["mlp_norm"]], axis=0),
        w_up=jnp.stack([gl0["w_up"], gl1["w_up"]]),
        w_down=jnp.stack([gl0["w_down"], gl1["w_down"]]),
        ple_norm=jnp.concatenate([gl0["ple_norm"], gl1["ple_norm"]], axis=0),
        w_ple=jnp.stack([gl0["w_ple"], gl1["w_ple"]]),
        w_ple_gate=jnp.stack([gl0["w_ple_gate"], gl1["w_ple_gate"]]),
    )
    return sq, grad_x, grads


def _chip_peer(x, y, c, t):
    return (jnp.bitwise_xor(x, t >> 1), jnp.bitwise_xor(y, t & 1), c)


def _place():
    x, y, c = lax.axis_index("x"), lax.axis_index("y"), lax.axis_index("c")
    return x, y, c, 2 * x + y, (x, y, 1 - c)


def _remote(src, dst, send_sem, recv_sem, to):
    return pltpu.make_async_remote_copy(src_ref=src, dst_ref=dst, send_sem=send_sem, recv_sem=recv_sem,
                                        device_id=to, device_id_type=MESH)


def _hbm_call(name, body, ins, out_shape, scratch_shapes):
    any_spec = pl.BlockSpec(memory_space=pl.ANY)
    return pl.pallas_call(body, name=name, out_shape=out_shape, in_specs=[any_spec] * len(ins),
                          out_specs=[any_spec] * len(out_shape), scratch_shapes=scratch_shapes)(*ins)


def _half(n0, which):
    return pl.ds(which * (n0 // 2), n0 // 2)


def _gather_shards(name, shards):
    T = len(shards)

    def body(*refs):
        ins, outs = refs[:T], refs[T:2 * T]
        send, recv = refs[2 * T:]
        x, y, c, q, sibling = _place()
        sends = []
        for i in range(T):
            mine = _half(ins[i].shape[0], c)
            for t in range(1, N_CHIPS):
                cp = _remote(ins[i].at[mine], outs[i].at[q, mine], send.at[i, t - 1], recv.at[i, t - 1],
                             _chip_peer(x, y, c, t))
                cp.start()
                sends.append(cp)
        for i in range(T):
            mine = _half(ins[i].shape[0], c)
            for t in range(1, N_CHIPS):
                landed = outs[i].at[jnp.bitwise_xor(q, t), mine]
                _remote(landed, landed, send.at[i, t - 1], recv.at[i, t - 1], _chip_peer(x, y, c, t)).wait_recv()
                cp = _remote(landed, landed, send.at[i, 2 + t], recv.at[i, 2 + t], sibling)
                cp.start()
                sends.append(cp)
        for i in range(T):
            theirs = _half(ins[i].shape[0], 1 - c)
            for t in range(1, N_CHIPS):
                passed = outs[i].at[jnp.bitwise_xor(q, t), theirs]
                _remote(passed, passed, send.at[i, 2 + t], recv.at[i, 2 + t], sibling).wait_recv()
        for cp in sends:
            cp.wait_send()

    n_rel = 2 * (N_CHIPS - 1)
    return _hbm_call(name, body, shards,
                     [jax.ShapeDtypeStruct((N_CHIPS,) + s.shape, s.dtype) for s in shards],
                     [pltpu.SemaphoreType.DMA((T, n_rel)), pltpu.SemaphoreType.DMA((T, n_rel))])


def _other_half_from_sibling(name, stacks):
    T = len(stacks)

    def body(*refs):
        ins, outs = refs[:T], refs[T:2 * T]
        send, recv = refs[2 * T:]
        x, y, c, q, sibling = _place()
        copies = []
        for i in range(T):
            rc = _remote(ins[i].at[:, _half(ins[i].shape[1], 1 - c)], outs[i], send.at[i], recv.at[i], sibling)
            rc.start()
            copies.append(rc)
        for cp in copies:
            cp.wait()

    return _hbm_call(name, body, stacks,
                     [jax.ShapeDtypeStruct((s.shape[0], s.shape[1] // 2) + s.shape[2:], s.dtype) for s in stacks],
                     [pltpu.SemaphoreType.DMA((T,)), pltpu.SemaphoreType.DMA((T,))])


def _scatter_to_chips(name, stacks):
    T = len(stacks)

    def body(*refs):
        ins, outs = refs[:T], refs[T:2 * T]
        send, recv = refs[2 * T:]
        x, y, c, q, sibling = _place()
        copies = []
        for i in range(T):
            for t in range(1, N_CHIPS):
                rc = _remote(ins[i].at[jnp.bitwise_xor(q, t)], outs[i].at[t - 1], send.at[i, t - 1],
                             recv.at[i, t - 1], _chip_peer(x, y, c, t))
                rc.start()
                copies.append(rc)
        for cp in copies:
            cp.wait()

    return _hbm_call(name, body, stacks,
                     [jax.ShapeDtypeStruct((N_CHIPS - 1,) + s.shape[1:], s.dtype) for s in stacks],
                     [pltpu.SemaphoreType.DMA((T, N_CHIPS - 1)), pltpu.SemaphoreType.DMA((T, N_CHIPS - 1))])


def _swap_with_sibling(name, arrays):
    T = len(arrays)

    def body(*refs):
        ins, outs = refs[:T], refs[T:2 * T]
        send, recv = refs[2 * T:]
        x, y, c, q, sibling = _place()
        copies = []
        for i in range(T):
            rc = _remote(ins[i], outs[i], send.at[i], recv.at[i], sibling)
            rc.start()
            copies.append(rc)
        for cp in copies:
            cp.wait()

    return _hbm_call(name, body, arrays, [jax.ShapeDtypeStruct(a.shape, a.dtype) for a in arrays],
                     [pltpu.SemaphoreType.DMA((T,)), pltpu.SemaphoreType.DMA((T,))])


def _gather_from_all(name, block):
    R, C = block.shape

    def body(src, out, send_sems, recv_sems):
        x, y, c = lax.axis_index("x"), lax.axis_index("y"), lax.axis_index("c")
        me = 4 * x + 2 * y + c
        out[me] = src[...]
        copies = []
        for r in range(1, N_DEV):
            peer = (jnp.bitwise_xor(x, r >> 2), jnp.bitwise_xor(y, (r >> 1) & 1), jnp.bitwise_xor(c, r & 1))
            cp = pltpu.make_async_remote_copy(src_ref=src, dst_ref=out.at[me], send_sem=send_sems.at[r - 1],
                                              recv_sem=recv_sems.at[r - 1], device_id=peer, device_id_type=MESH)
            cp.start()
            copies.append(cp)
        for cp in copies:
            cp.wait()

    return pl.pallas_call(
        body, name=name, out_shape=jax.ShapeDtypeStruct((N_DEV, R, C), block.dtype),
        in_specs=[pl.BlockSpec(memory_space=pltpu.VMEM)], out_specs=pl.BlockSpec(memory_space=pltpu.VMEM),
        scratch_shapes=[pltpu.SemaphoreType.DMA((N_DEV - 1,)), pltpu.SemaphoreType.DMA((N_DEV - 1,))],
    )(block)


def _view(a):
    return a[0] if a.shape[0] == 1 else a


def _view_axis(a, axis):
    return axis - 1 if a.shape[0] == 1 else axis


def _rows(a):
    return a.reshape(-1, a.shape[-1])


def _elementwise(name, fn, ins, out_dtypes, tm):
    specs = []
    for a in ins:
        a, row0 = a if isinstance(a, tuple) else (a, 0)
        specs.append((_rows(a), a.shape[-1], 0, row0))
    shape = ins[0][0].shape if isinstance(ins[0], tuple) else ins[0].shape
    outs = _rowwise(name, fn, specs, [], [(shape[-1], dt) for dt in out_dtypes], tm=tm, n_rows=math.prod(shape[:-1]))
    return outs.reshape(shape) if len(out_dtypes) == 1 else [o.reshape(shape) for o in outs]


SMALL_ROWS = 8
CONV_ROWS = CONV_WIDTH * DN_QKV // D_MODEL
SMALL_GRAD_ROWS = 24


def _pack_small(vals, conv=None):
    tail = jnp.concatenate([vals["attn_q_gain"].reshape(-1), vals["attn_k_gain"].reshape(-1),
                            vals["dn_a_log"].reshape(-1), vals["dn_dt_bias"].reshape(-1),
                            vals["dn_o_gain"].reshape(-1)])
    tail = jnp.pad(tail, (0, D_MODEL - tail.shape[0])).reshape(1, D_MODEL)
    rows = [vals["mix_norm"], vals["mlp_norm"], vals["ple_norm"], tail, jnp.zeros((1, D_MODEL), F32)]
    if conv is not None:
        rows += [conv.reshape(CONV_ROWS, D_MODEL),
                 jnp.zeros((SMALL_GRAD_ROWS - SMALL_ROWS - CONV_ROWS, D_MODEL), F32)]
    return jnp.concatenate(rows, axis=0)


def _unpack_small(block):
    nq = 3 * A_HEAD_DIM
    t = block[6]
    return dict(
        mix_norm=block[0:2], mlp_norm=block[2:4], ple_norm=block[4:6],
        attn_q_gain=t[:nq].reshape(1, 3, A_HEAD_DIM), attn_k_gain=t[nq:2 * nq].reshape(1, 3, A_HEAD_DIM),
        dn_a_log=t[2 * nq:2 * nq + DN_HEADS].reshape(1, DN_HEADS),
        dn_dt_bias=t[2 * nq + DN_HEADS:2 * nq + 2 * DN_HEADS].reshape(1, DN_HEADS),
        dn_o_gain=t[2 * nq + 2 * DN_HEADS:2 * nq + 2 * DN_HEADS + DN_HEAD_DIM].reshape(1, DN_HEAD_DIM))


def kernel(x, p, positions, mix_norm, attn_w_qkv, attn_q_gain, attn_k_gain, attn_w_o, dn_w_in, dn_conv, dn_a_log, dn_dt_bias, dn_o_gain, dn_w_o, mlp_norm, w_up, w_down, ple_norm, w_ple, w_ple_gate, loss_target, m_mix_norm, m_attn_w_qkv, m_attn_q_gain, m_attn_k_gain, m_attn_w_o, m_dn_w_in, m_dn_conv, m_dn_a_log, m_dn_dt_bias, m_dn_o_gain, m_dn_w_o, m_mlp_norm, m_w_up, m_w_down, m_ple_norm, m_w_ple, m_w_ple_gate, v_mix_norm, v_attn_w_qkv, v_attn_q_gain, v_attn_k_gain, v_attn_w_o, v_dn_w_in, v_dn_conv, v_dn_a_log, v_dn_dt_bias, v_dn_o_gain, v_dn_w_o, v_mlp_norm, v_w_up, v_w_down, v_ple_norm, v_w_ple, v_w_ple_gate):
    given = dict(locals())
    w = {n: given[n] for n in WEIGHTS}
    m = {n: given["m_" + n] for n in WEIGHTS}
    v = {n: given["v_" + n] for n in WEIGHTS}
    kinds = ("grad", "delta", "new_m", "new_v")
    big_names = [(n, _view_axis(w[n], axis)) for n, axis in SHARDED if n != "dn_conv"]

    chip = 2 * lax.axis_index("x") + lax.axis_index("y")
    core = lax.axis_index("c")
    shards = [_view(w[n]).astype(BF16) for n, _ in big_names]
    gathered = _gather_shards("gather_weights", shards)
    full = {n: jnp.concatenate([jnp.where(chip == q, s, g[q]) for q in range(N_CHIPS)], axis=axis)
            for (n, axis), s, g in zip(big_names, shards, gathered)}
    conv_block = jnp.pad(w["dn_conv"].reshape(-1), (0, SMALL_ROWS * D_MODEL - w["dn_conv"].size))
    conv_all = _gather_from_all("gather_conv", conv_block.reshape(SMALL_ROWS, D_MODEL))
    conv_all = conv_all.reshape(N_CHIPS, 2, -1)[:, 0, :w["dn_conv"].size]
    conv_full = jnp.concatenate([conv_all[q].reshape(CONV_WIDTH, -1) for q in range(N_CHIPS)], axis=1)

    w_in = full["dn_w_in"]
    n_ab = 2 * DN_HEADS
    big = dict(full)
    big["dn_w_in"] = jnp.concatenate([w_in[:, DN_QKV + n_ab:], w_in[:, :DN_QKV + n_ab],
                                      jnp.zeros((D_MODEL, DN_AB_PAD - n_ab), BF16)], axis=1)
    small = {n: w[n] for n in REPLICATED}
    small["dn_conv"] = conv_full

    sq, grad_x, grads = _local_step(x[0], p[:, 0], positions[0], loss_target[0], small, big)
    loss = lax.psum(0.5 * sq[0, 0] / D_MODEL, ("x", "y", "c"))
    out = {}

    stacks = [jnp.stack(jnp.split(grads[n], N_CHIPS, axis=axis)) for n, axis in big_names]
    mine = [lax.dynamic_slice_in_dim(s, core * (s.shape[1] // 2), s.shape[1] // 2, axis=1) for s in stacks]
    theirs = _other_half_from_sibling("split_core_grads", stacks)
    chip_sums = [_elementwise(f"add_core_{n}", lambda a, b: a.astype(F32) + b.astype(F32), [a, b], [BF16], 128)
                 for (n, _), a, b in zip(big_names, mine, theirs)]
    landed = _scatter_to_chips("scatter_grads", chip_sums)
    half_sums = []
    for (n, _), s, r in zip(big_names, chip_sums, landed):
        o = lax.dynamic_index_in_dim(s, chip, axis=0, keepdims=False)
        per = math.prod(o.shape[:-1])
        half_sums.append(_elementwise(
            f"add_chips_{n}", lambda a, b, c, d: ((a.astype(F32) + b.astype(F32)) + c.astype(F32)) + d.astype(F32),
            [o, (r, 0), (r, per), (r, 2 * per)], [F32], 128))
    other_halves = _swap_with_sibling("join_core_sums", half_sums)
    for (n, _), a, b in zip(big_names, half_sums, other_halves):
        g = jnp.where(core == 0, jnp.concatenate([a, b], axis=0), jnp.concatenate([b, a], axis=0))
        shp = w[n].shape
        res = _elementwise(f"adamw_{n}", lambda g, w_, m_, v_: (g,) + _adamw(w_, g, m_, v_),
                           [g.reshape(shp), w[n], m[n], v[n]], [F32] * 4, 256)
        for kind, arr in zip(kinds, res):
            out[kind + "_" + n] = arr.reshape(shp)

    slots = _gather_from_all("gather_small_grads", _pack_small(grads, grads["dn_conv"]))

    def small_body(s_ref, w_ref, m_ref, v_ref, sum_out, g_out, d_out, m_out, v_out):
        total = s_ref[0]
        for d in range(1, N_DEV):
            total = total + s_ref[d]
        sum_out[...] = total
        g = total[:SMALL_ROWS]
        for o, r in zip((g_out, d_out, m_out, v_out), (g,) + _adamw(w_ref[...], g, m_ref[...], v_ref[...])):
            o[...] = r

    res = pl.pallas_call(small_body, name="adamw_replicated",
                         out_shape=[jax.ShapeDtypeStruct((SMALL_GRAD_ROWS, D_MODEL), F32)]
                         + [jax.ShapeDtypeStruct((SMALL_ROWS, D_MODEL), F32)] * 4)(
        slots, _pack_small(w), _pack_small(m), _pack_small(v))
    for kind, block in zip(kinds, res[1:]):
        for n, arr in _unpack_small(block).items():
            out[kind + "_" + n] = arr
    conv_sum = res[0][SMALL_ROWS:SMALL_ROWS + CONV_ROWS].reshape(CONV_WIDTH, DN_QKV)
    cols = DN_QKV // N_CHIPS
    chip = 2 * lax.axis_index("x") + lax.axis_index("y")
    conv_mine = lax.dynamic_slice_in_dim(conv_sum, chip * cols, cols, axis=1)
    res = _elementwise("adamw_dn_conv", lambda g, w_, m_, v_: (g,) + _adamw(w_, g, m_, v_),
                       [conv_mine, w["dn_conv"][0], m["dn_conv"][0], v["dn_conv"][0]], [F32] * 4, CONV_WIDTH)
    for kind, arr in zip(kinds, res):
        out[kind + "_dn_conv"] = arr[None]

    return (loss, grad_x[None],
            *[out["grad_" + n] for n in WEIGHTS], *[out["delta_" + n] for n in WEIGHTS],
            *[out["new_m_" + n] for n in WEIGHTS], *[out["new_v_" + n] for n in WEIGHTS])
```

```python
import functools
import math

import jax
import jax.numpy as jnp
from jax import lax
from jax.experimental import pallas as pl
from jax.experimental.pallas import tpu as pltpu

F32 = jnp.float32
BF16 = jnp.bfloat16
HIGHEST = lax.Precision.HIGHEST

D_MODEL = 1024
EPS = 1e-6
SWA_GROUPS = ((128, 1), (512, 4), (2048, 16))
A_HEADS = 8
A_HEAD_DIM = 64
A_WIDTH = A_HEADS * A_HEAD_DIM
ROPE_DIM = A_HEAD_DIM // 4
ROPE_THETA = 500000.0
BAND = 128
DN_HEADS = 8
DN_HEAD_DIM = 128
DN_WIDTH = DN_HEADS * DN_HEAD_DIM
DN_QKV = 3 * DN_WIDTH
DN_AB_PAD = 128
DN_IN_PAD = DN_WIDTH + DN_QKV + DN_AB_PAD
DN_QKV0 = DN_WIDTH
DN_AB0 = DN_WIDTH + DN_QKV
DN_HB = 8
CONV_WIDTH = 4
CHUNK = 64
PLE_DIM = 256
D_FF = 4 * D_MODEL

ADAM_LR = 0.001
ADAM_B1 = 0.9
ADAM_B2 = 0.999
ADAM_EPS = 1e-08
ADAM_WD = 0.01
ADAM_STEP = 10

N_CHIPS = 4
N_DEV = 8
VMEM_LIMIT = 48 * 1024 * 1024
MESH = pl.DeviceIdType.MESH

SHARDED = (
    ("attn_w_qkv", 2), ("attn_w_o", 2), ("dn_w_in", 2), ("dn_conv", 2), ("dn_w_o", 1),
    ("w_up", 2), ("w_down", 1), ("w_ple", 2), ("w_ple_gate", 1))
REPLICATED = ("mix_norm", "attn_q_gain", "attn_k_gain", "dn_a_log", "dn_dt_bias", "dn_o_gain",
              "mlp_norm", "ple_norm")
WEIGHTS = ("mix_norm", "attn_w_qkv", "attn_q_gain", "attn_k_gain", "attn_w_o", "dn_w_in", "dn_conv",
           "dn_a_log", "dn_dt_bias", "dn_o_gain", "dn_w_o", "mlp_norm", "w_up", "w_down", "ple_norm",
           "w_ple", "w_ple_gate")


def _cparams(sem=None):
    return pltpu.CompilerParams(dimension_semantics=sem, vmem_limit_bytes=VMEM_LIMIT)


def _pick(n, cap, quantum=128):
    best = None
    for t in range(quantum, min(n, cap) + 1, quantum):
        if n % t == 0:
            best = t
    return n if best is None else best


_DIMS = {"nn": ((1,), (0,)), "nt": ((1,), (1,)), "tn": ((0,), (0,))}


def _mm(name, a, b, mode, out_dtypes=(F32,), extras=(), epilogue=None):
    if mode == "nn":
        (M, K), (K2, N) = a.shape, b.shape
    elif mode == "nt":
        (M, K), (N, K2) = a.shape, b.shape
    else:
        (K, M), (K2, N) = a.shape, b.shape
    assert K == K2, (name, a.shape, b.shape)
    tm, tn, tk = _pick(M, 512), _pick(N, 1536), _pick(K, 1024 if mode == "tn" else 1536)
    nk = K // tk
    if mode == "nn":
        a_spec = pl.BlockSpec((tm, tk), lambda i, j, k: (i, k))
        b_spec = pl.BlockSpec((tk, tn), lambda i, j, k: (k, j))
    elif mode == "nt":
        a_spec = pl.BlockSpec((tm, tk), lambda i, j, k: (i, k))
        b_spec = pl.BlockSpec((tn, tk), lambda i, j, k: (j, k))
    else:
        a_spec = pl.BlockSpec((tk, tm), lambda i, j, k: (k, i))
        b_spec = pl.BlockSpec((tk, tn), lambda i, j, k: (k, j))
    o_spec = pl.BlockSpec((tm, tn), lambda i, j, k: (i, j))
    n_extra, n_out = len(extras), len(out_dtypes)
    dims = (_DIMS[mode], ((), ()))

    def body(a_ref, b_ref, *rest):
        extra_refs, out_refs, acc = rest[:n_extra], rest[n_extra:n_extra + n_out], rest[-1]
        k = pl.program_id(2)

        @pl.when(k == 0)
        def _():
            acc[...] = jnp.zeros_like(acc)

        acc[...] += lax.dot_general(a_ref[...].astype(BF16), b_ref[...].astype(BF16), dims,
                                    preferred_element_type=F32)

        @pl.when(k == nk - 1)
        def _():
            if epilogue is None:
                vals = (acc[...],)
            else:
                vals = epilogue(acc[...], *[e[...] for e in extra_refs])
            for o, v in zip(out_refs, vals):
                o[...] = v.astype(o.dtype)

    outs = pl.pallas_call(
        body, name=name, grid=(M // tm, N // tn, nk),
        in_specs=[a_spec, b_spec] + [o_spec] * n_extra,
        out_specs=[o_spec] * n_out,
        out_shape=[jax.ShapeDtypeStruct((M, N), dt) for dt in out_dtypes],
        scratch_shapes=[pltpu.VMEM((tm, tn), F32)],
        compiler_params=_cparams(("parallel", "parallel", "arbitrary")),
    )(a, b, *extras)
    return outs[0] if n_out == 1 else outs


def _rowwise(name, fn, rows, bcast, row_outs, acc_outs=(), tm=256, n_rows=None):
    rows = [r if isinstance(r, tuple) else (r, r.shape[1], 0) for r in rows]
    rows = [r if len(r) == 4 else r + (0,) for r in rows]
    S = rows[0][0].shape[0] if n_rows is None else n_rows
    tm = min(tm, S)
    assert S % tm == 0 and all(r[3] % tm == 0 for r in rows), (name, S, tm)
    n_row, n_bc, n_ro, n_acc = len(rows), len(bcast), len(row_outs), len(acc_outs)
    in_specs = [pl.BlockSpec((tm, w), functools.partial(lambda i, cb, rb: (i + rb, cb), cb=cb, rb=r0 // tm))
                for _, w, cb, r0 in rows]
    in_specs += [pl.BlockSpec(b.shape, lambda i: (0, 0)) for b in bcast]
    out_specs = [pl.BlockSpec((tm, c), lambda i: (i, 0)) for c, _ in row_outs]
    out_specs += [pl.BlockSpec(s, lambda i: (0, 0)) for s in acc_outs]
    out_shape = [jax.ShapeDtypeStruct((S, c), dt) for c, dt in row_outs]
    out_shape += [jax.ShapeDtypeStruct(s, F32) for s in acc_outs]

    def body(*refs):
        ins = [r[...] for r in refs[:n_row + n_bc]]
        outs = refs[n_row + n_bc:]
        vals = fn(*ins)
        if not isinstance(vals, (tuple, list)):
            vals = (vals,)
        for o, v in zip(outs[:n_ro], vals[:n_ro]):
            o[...] = v.astype(o.dtype)
        if n_acc:
            @pl.when(pl.program_id(0) == 0)
            def _():
                for o in outs[n_ro:]:
                    o[...] = jnp.zeros_like(o)
            for o, v in zip(outs[n_ro:], vals[n_ro:]):
                o[...] += v

    outs = pl.pallas_call(
        body, name=name, grid=(S // tm,), in_specs=in_specs, out_specs=out_specs, out_shape=out_shape,
        compiler_params=_cparams(("arbitrary",) if n_acc else ("parallel",)),
    )(*[r[0] for r in rows], *bcast)
    return outs[0] if len(outs) == 1 else outs


def _sigmoid(x):
    return 1.0 / (1.0 + jnp.exp(-x))


def _silu(x):
    return x * _sigmoid(x)


def _softplus(x):
    return jnp.maximum(x, 0.0) + jnp.log(1.0 + jnp.exp(-jnp.abs(x)))


def _rms_fwd_fn(x, g):
    r = lax.rsqrt(jnp.mean(x * x, axis=-1, keepdims=True) + EPS)
    return (x * r) * g


def _rms_bwd_fn(x, dres, *rest):
    dh, g = sum(rest[:-1]), rest[-1]
    r = lax.rsqrt(jnp.mean(x * x, axis=-1, keepdims=True) + EPS)
    xh = x * r
    dxh = dh * g
    dx = r * (dxh - xh * jnp.mean(dxh * xh, axis=-1, keepdims=True))
    return dres + dx, jnp.sum(dh * xh, axis=0, keepdims=True)


def _rms_fwd(name, x, gain):
    return _rowwise(name, _rms_fwd_fn, [x], [gain.reshape(1, -1)], [(x.shape[1], BF16)])


def _rms_bwd(name, x, gain, dres, dhs):
    return _rowwise(name, _rms_bwd_fn, [x, dres] + list(dhs), [gain.reshape(1, -1)],
                    [(x.shape[1], F32)], [(1, x.shape[1])])


def _relu2_epilogue(acc):
    r = jnp.maximum(acc, 0.0)
    return acc, r * r


def _relu2_bwd_epilogue(acc, u):
    return (acc * (2.0 * jnp.maximum(u, 0.0)),)


def _ple_fwd_fn(x, pp, zg):
    return x + pp * _sigmoid(zg)


def _ple_bwd_fn(dx, pp, zg):
    gate = _sigmoid(zg)
    return dx * gate, dx * pp * gate * (1.0 - gate)


def _loss_fn(y, t):
    err = y - t
    return err * (1.0 / D_MODEL), jnp.broadcast_to(jnp.sum(err * err, keepdims=True), (1, 128))


def _adamw(w, g, m, v):
    m = ADAM_B1 * m + (1.0 - ADAM_B1) * g
    v = ADAM_B2 * v + (1.0 - ADAM_B2) * jnp.square(g)
    m_hat = m / (1.0 - ADAM_B1 ** ADAM_STEP)
    v_hat = v / (1.0 - ADAM_B2 ** ADAM_STEP)
    delta = -ADAM_LR * (m_hat / (jnp.sqrt(v_hat) + ADAM_EPS) + ADAM_WD * w)
    return delta, m, v


def _lane_take(x, offset):
    n = x.shape[-1]
    return pltpu.roll(x, (-offset) % n, 1)


def _head_lane(shape):
    return lax.broadcasted_iota(jnp.int32, shape, 1) % A_HEAD_DIM


def _rope_partner(x):
    lane = _head_lane(x.shape)
    return jnp.where(lane < ROPE_DIM // 2, _lane_take(x, ROPE_DIM // 2),
                     jnp.where(lane < ROPE_DIM, _lane_take(x, -(ROPE_DIM // 2)), 0.0))


def _head_mean(x, bd):
    hi = x.astype(BF16)
    lo = (x - hi.astype(F32)).astype(BF16)
    b = bd.astype(BF16)
    return jnp.dot(hi, b, preferred_element_type=F32) + jnp.dot(lo, b, preferred_element_type=F32)


def _fold_heads(row):
    out = row[:, :A_HEAD_DIM]
    for h in range(1, A_HEADS):
        out = out + row[:, h * A_HEAD_DIM:(h + 1) * A_HEAD_DIM]
    return out


def _all_heads(t):
    return jnp.concatenate([t] * (A_WIDTH // t.shape[1]), axis=1)


def _qk_prep_fwd_fn(qkv, ct, st, gq, gk, bd):
    ct, st = _all_heads(ct), _all_heads(st)

    def one(t, g):
        n = t * lax.rsqrt(_head_mean(t * t, bd) + EPS) * g
        return n * ct + _rope_partner(n) * st
    q, k, v = qkv[:, :A_WIDTH], qkv[:, A_WIDTH:2 * A_WIDTH], qkv[:, 2 * A_WIDTH:]
    return one(q, gq), one(k, gk), v


def _qk_prep_bwd_fn(qkv, ct, st, dq, dk, dv, gq, gk, bd):
    ct, st = _all_heads(ct), _all_heads(st)

    def one(t, g, dy):
        r = lax.rsqrt(_head_mean(t * t, bd) + EPS)
        nh = t * r
        dn = dy * ct + _rope_partner(dy * st)
        dg = jnp.sum(dn * nh, axis=0, keepdims=True)
        dnh = dn * g
        return r * (dnh - nh * _head_mean(dnh * nh, bd)), _fold_heads(dg)
    q, k = qkv[:, :A_WIDTH], qkv[:, A_WIDTH:2 * A_WIDTH]
    dq_raw, dgq = one(q, gq, dq)
    dk_raw, dgk = one(k, gk, dk)
    return jnp.concatenate([dq_raw, dk_raw, dv], axis=1), dgq, dgk


_BATCH_DIMS = {"nn": ((2,), (1,)), "nt": ((2,), (2,)), "tn": ((1,), (1,))}


def _bdot(a, b, mode, precision=None):
    return lax.dot_general(a, b, (_BATCH_DIMS[mode], ((0,), (0,))), precision=precision,
                           preferred_element_type=F32)


def _attn_cols(h):
    return slice(h * A_HEAD_DIM, (h + 1) * A_HEAD_DIM)


def _attn_heads(ref):
    return jnp.stack([ref[:, _attn_cols(h)] for h in range(A_HEADS)])


def _band_masks():
    qi = lax.broadcasted_iota(jnp.int32, (BAND, BAND), 0)
    kj = lax.broadcasted_iota(jnp.int32, (BAND, BAND), 1)
    return kj <= qi, kj >= qi


def _attn_fwd(name, q, k, v, blocks_per_class):
    S = q.shape[0]
    nblk = S // BAND
    scale = A_HEAD_DIM ** -0.5

    def body(q_ref, kp_ref, kc_ref, vp_ref, vc_ref, o_ref, l_ref):
        i = pl.program_id(0)
        has_prev = (i % blocks_per_class) != 0
        m_cur, m_prev = _band_masks()
        m_prev = jnp.logical_and(m_prev, has_prev)
        q, kc, kp, vc, vp = (_attn_heads(r) for r in (q_ref, kc_ref, kp_ref, vc_ref, vp_ref))
        s_c = jnp.where(m_cur[None], _bdot(q, kc, "nt") * scale, -jnp.inf)
        s_p = jnp.where(m_prev[None], _bdot(q, kp, "nt") * scale, -jnp.inf)
        m = jnp.maximum(jnp.max(s_c, axis=-1, keepdims=True), jnp.max(s_p, axis=-1, keepdims=True))
        e_c, e_p = jnp.exp(s_c - m), jnp.exp(s_p - m)
        l = jnp.sum(e_c, axis=-1, keepdims=True) + jnp.sum(e_p, axis=-1, keepdims=True)
        o = _bdot((e_c / l).astype(BF16), vc, "nn") + _bdot((e_p / l).astype(BF16), vp, "nn")
        lse = m + jnp.log(l)
        for h in range(A_HEADS):
            o_ref[:, _attn_cols(h)] = o[h]
            l_ref[:, _attn_cols(h)] = jnp.broadcast_to(lse[h], (BAND, A_HEAD_DIM))

    cur = pl.BlockSpec((BAND, A_WIDTH), lambda i: (i, 0))
    prev = pl.BlockSpec((BAND, A_WIDTH), lambda i: (jnp.maximum(i - 1, 0), 0))
    return pl.pallas_call(
        body, name=name, grid=(nblk,), in_specs=[cur, prev, cur, prev, cur], out_specs=[cur, cur],
        out_shape=[jax.ShapeDtypeStruct((S, A_WIDTH), F32)] * 2,
        compiler_params=_cparams(("parallel",)),
    )(q, k, k, v, v)


def _attn_bwd(name, q, k, v, o, lse, do, dlse, blocks_per_class):
    S = q.shape[0]
    nblk = S // BAND
    scale = A_HEAD_DIM ** -0.5

    def body(q_ref, kp_ref, kc_ref, vp_ref, vc_ref, o_ref, l_ref, do_ref, dl_ref,
             dq_ref, dk_ref, dv_ref, ck, cv):
        i = pl.program_id(0)

        @pl.when(i == 0)
        def _():
            ck[...] = jnp.zeros_like(ck)
            cv[...] = jnp.zeros_like(cv)

        @pl.when(i == nblk)
        def _():
            dk_ref[...] = ck[...]
            dv_ref[...] = cv[...]

        @pl.when(i < nblk)
        def _():
            has_prev = (i % blocks_per_class) != 0
            m_cur, m_prev = _band_masks()
            m_prev = jnp.logical_and(m_prev, has_prev)
            q, kc, kp, vc, vp = (_attn_heads(r) for r in (q_ref, kc_ref, kp_ref, vc_ref, vp_ref))
            do, o, dl = _attn_heads(do_ref), _attn_heads(o_ref), _attn_heads(dl_ref)
            lse = jnp.max(_attn_heads(l_ref), axis=-1, keepdims=True)
            p_c = jnp.where(m_cur[None], jnp.exp(_bdot(q, kc, "nt") * scale - lse), 0.0)
            p_p = jnp.where(m_prev[None], jnp.exp(_bdot(q, kp, "nt") * scale - lse), 0.0)
            corr = jnp.sum(dl, axis=-1, keepdims=True) - jnp.sum(do * o, axis=-1, keepdims=True)
            dob = do.astype(BF16)
            ds_c = (p_c * (_bdot(dob, vc, "nt") + corr)).astype(BF16)
            ds_p = (p_p * (_bdot(dob, vp, "nt") + corr)).astype(BF16)
            dq = (_bdot(ds_c, kc, "nn") + _bdot(ds_p, kp, "nn")) * scale
            dk_p, dk_c = _bdot(ds_p, q, "tn") * scale, _bdot(ds_c, q, "tn") * scale
            dv_p, dv_c = _bdot(p_p.astype(BF16), dob, "tn"), _bdot(p_c.astype(BF16), dob, "tn")
            for h in range(A_HEADS):
                sl = _attn_cols(h)
                dq_ref[:, sl] = dq[h]
                dk_ref[:, sl] = ck[:, sl] + dk_p[h]
                dv_ref[:, sl] = cv[:, sl] + dv_p[h]
                ck[:, sl] = dk_c[h]
                cv[:, sl] = dv_c[h]

    last = nblk - 1
    cur = pl.BlockSpec((BAND, A_WIDTH), lambda i: (jnp.minimum(i, last), 0))
    prev = pl.BlockSpec((BAND, A_WIDTH), lambda i: (jnp.minimum(jnp.maximum(i - 1, 0), last), 0))
    return pl.pallas_call(
        body, name=name, grid=(nblk + 1,),
        in_specs=[cur, prev, cur, prev, cur, cur, cur, cur, cur], out_specs=[cur, prev, prev],
        out_shape=[jax.ShapeDtypeStruct((S, A_WIDTH), F32)] * 3,
        scratch_shapes=[pltpu.VMEM((BAND, A_WIDTH), F32)] * 2,
        compiler_params=_cparams(("arbitrary",)),
    )(q, k, k, v, v, o, lse, do, dlse)


def _merge_fwd_fn(o0, o1, o2, l0, l1, l2):
    m = jnp.maximum(jnp.maximum(l0, l1), l2)
    e0, e1, e2 = jnp.exp(l0 - m), jnp.exp(l1 - m), jnp.exp(l2 - m)
    return (e0 * o0 + e1 * o1 + e2 * o2) / (e0 + e1 + e2)


def _merge_bwd_fn(o0, o1, o2, l0, l1, l2, dom):
    m = jnp.maximum(jnp.maximum(l0, l1), l2)
    e0, e1, e2 = jnp.exp(l0 - m), jnp.exp(l1 - m), jnp.exp(l2 - m)
    den = e0 + e1 + e2
    w0, w1, w2 = e0 / den, e1 / den, e2 / den
    dw0, dw1, dw2 = dom * o0, dom * o1, dom * o2
    mean = w0 * dw0 + w1 * dw1 + w2 * dw2
    return w0 * dom, w1 * dom, w2 * dom, w0 * (dw0 - mean), w1 * (dw1 - mean), w2 * (dw2 - mean)


def _to_classes(t, d):
    if d == 1:
        return t
    S, C = t.shape
    return t.reshape(S // d, d, C).transpose(1, 0, 2).reshape(S, C)


def _from_classes(t, d):
    if d == 1:
        return t
    S, C = t.shape
    return t.reshape(d, S // d, C).transpose(1, 0, 2).reshape(S, C)


def _rope_lane_tables(positions):
    inv_freq = ROPE_THETA ** (-jnp.arange(0, ROPE_DIM, 2, dtype=F32) / ROPE_DIM)
    ang = positions.astype(F32)[:, None] * inv_freq
    cos, sin = jnp.cos(ang), jnp.sin(ang)
    S = positions.shape[0]
    rest = A_HEAD_DIM - ROPE_DIM
    ct = jnp.concatenate([cos, cos, jnp.ones((S, rest), F32)], axis=1)
    st = jnp.concatenate([-sin, sin, jnp.zeros((S, rest), F32)], axis=1)
    return jnp.tile(ct, (1, 2)), jnp.tile(st, (1, 2))


def _head_mean_matrix():
    r = jnp.arange(A_WIDTH) // A_HEAD_DIM
    return (r[:, None] == r[None, :]).astype(F32) * (1.0 / A_HEAD_DIM)


def _conv_fwd(name, proj, w):
    S = proj.shape[0]
    tm, tc = min(512, S), 1024
    per8 = tm // 8
    off = DN_QKV0 // tc

    def body(x_ref, halo_ref, w_ref, o_ref, xs):
        i = pl.program_id(0)
        xs[0:8, :] = jnp.where(i > 0, halo_ref[...], 0.0)
        xs[8:, :] = x_ref[...]
        acc = w_ref[0:1, :] * xs[pl.ds(8 - 3, tm), :]
        for j in range(1, CONV_WIDTH):
            acc = acc + w_ref[j:j + 1, :] * xs[pl.ds(8 - 3 + j, tm), :]
        o_ref[...] = acc

    return pl.pallas_call(
        body, name=name, grid=(S // tm, DN_QKV // tc),
        in_specs=[pl.BlockSpec((tm, tc), lambda i, j: (i, j + off)),
                  pl.BlockSpec((8, tc), lambda i, j: (jnp.maximum(i * per8 - 1, 0), j + off)),
                  pl.BlockSpec((CONV_WIDTH, tc), lambda i, j: (0, j))],
        out_specs=pl.BlockSpec((tm, tc), lambda i, j: (i, j)),
        out_shape=jax.ShapeDtypeStruct((S, DN_QKV), F32),
        scratch_shapes=[pltpu.VMEM((tm + 8, tc), F32)],
        compiler_params=_cparams(("parallel", "parallel")),
    )(proj, proj, w)


def _conv_bwd(name, proj, dpre, w):
    S = proj.shape[0]
    tm, tc = min(512, S), 1024
    per8 = tm // 8
    off = DN_QKV0 // tc
    last8 = S // 8 - 1
    nrow = S // tm

    def body(x_ref, xh_ref, d_ref, dh_ref, w_ref, dx_ref, dw_ref, xs, ds):
        i = pl.program_id(1)
        xs[0:8, :] = jnp.where(i > 0, xh_ref[...], 0.0)
        xs[8:, :] = x_ref[...]
        ds[0:tm, :] = d_ref[...]
        ds[tm:, :] = jnp.where(i < nrow - 1, dh_ref[...], 0.0)
        d = d_ref[...]
        acc = w_ref[0:1, :] * ds[pl.ds(3, tm), :]
        for j in range(1, CONV_WIDTH):
            acc = acc + w_ref[j:j + 1, :] * ds[pl.ds(3 - j, tm), :]
        dx_ref[...] = acc.astype(dx_ref.dtype)

        @pl.when(i == 0)
        def _():
            dw_ref[...] = jnp.zeros_like(dw_ref)

        for j in range(CONV_WIDTH):
            dw_ref[j:j + 1, :] += jnp.sum(d * xs[pl.ds(8 - 3 + j, tm), :], axis=0, keepdims=True)

    return pl.pallas_call(
        body, name=name, grid=(DN_QKV // tc, nrow),
        in_specs=[pl.BlockSpec((tm, tc), lambda j, i: (i, j + off)),
                  pl.BlockSpec((8, tc), lambda j, i: (jnp.maximum(i * per8 - 1, 0), j + off)),
                  pl.BlockSpec((tm, tc), lambda j, i: (i, j)),
                  pl.BlockSpec((8, tc), lambda j, i: (jnp.minimum((i + 1) * per8, last8), j)),
                  pl.BlockSpec((CONV_WIDTH, tc), lambda j, i: (0, j))],
        out_specs=[pl.BlockSpec((tm, tc), lambda j, i: (i, j)),
                   pl.BlockSpec((CONV_WIDTH, tc), lambda j, i: (0, j))],
        out_shape=[jax.ShapeDtypeStruct((S, DN_QKV), BF16), jax.ShapeDtypeStruct((CONV_WIDTH, DN_QKV), F32)],
        scratch_shapes=[pltpu.VMEM((tm + 8, tc), F32)] * 2,
        compiler_params=_cparams(("parallel", "arbitrary")),
    )(proj, proj, dpre, dpre, w)


def _gate_lane(shape):
    return lax.broadcasted_iota(jnp.int32, shape, 1)


GATES_ROWS = 256


def _chunk_cumsum_matrix():
    r = jnp.arange(GATES_ROWS)
    return ((r[:, None] >= r[None, :]) & (r[:, None] // CHUNK == r[None, :] // CHUNK)).astype(F32)


def _gates_fwd_fn(ab, alog, dt, cum):
    g = -jnp.exp(alog) * _softplus(ab + dt)
    gc = jnp.dot(cum, g, precision=HIGHEST, preferred_element_type=F32)
    return jnp.where(_gate_lane(ab.shape) < DN_HEADS, gc, _sigmoid(ab))


def _gates_bwd_fn(ab, dgb, alog, dt, cum):
    lane = _gate_lane(ab.shape)
    is_g = lane < DN_HEADS
    neg_a = -jnp.exp(alog)
    sp = _softplus(ab + dt)
    dsp = _sigmoid(ab + dt)
    beta = _sigmoid(ab)
    dgc = jnp.where(is_g, dgb, 0.0)
    dg = lax.dot_general(cum, dgc, (_DIMS["tn"], ((), ())), precision=HIGHEST, preferred_element_type=F32)
    dab = jnp.where(is_g, dg * neg_a * dsp, jnp.where(lane < 2 * DN_HEADS, dgb * beta * (1.0 - beta), 0.0))
    d_alog = jnp.sum(dg * neg_a * sp, axis=0, keepdims=True)
    d_dt = jnp.sum(dg * neg_a * dsp, axis=0, keepdims=True)
    return dab, d_alog, d_dt


def _chunk_math(precision):
    def dg(a, b, mode, prec=precision):
        return _bdot(a, b, mode, prec)

    @jax.custom_vjp
    def nn(a, b):
        return dg(a, b, "nn")

    @jax.custom_vjp
    def nt(a, b):
        return dg(a, b, "nt")

    @jax.custom_vjp
    def tn(a, b):
        return dg(a, b, "tn")

    nn.defvjp(lambda a, b: (nn(a, b), (a, b)), lambda r, g: (nt(g, r[1]), tn(r[0], g)))
    nt.defvjp(lambda a, b: (nt(a, b), (a, b)), lambda r, g: (nn(g, r[1]), tn(g, r[0])))
    tn.defvjp(lambda a, b: (tn(a, b), (a, b)), lambda r, g: (nt(r[1], g), nn(r[0], g)))

    def split(x):
        hi = x.astype(BF16)
        return hi, (x - hi.astype(F32)).astype(BF16)

    def fine(a, b, mode):
        ah, al = split(a)
        bh, bl = split(b)
        return dg(ah, bh, mode, None) + (dg(ah, bl, mode, None) + dg(al, bh, mode, None))

    def unit_lower_inverse(a):
        row = lax.broadcasted_iota(jnp.int32, a.shape, 1)
        col = lax.broadcasted_iota(jnp.int32, a.shape, 2)
        x = -a
        p = jnp.where(row == col, 1.0, 0.0) + x
        for _ in range(int(math.log2(CHUNK)) - 1):
            x = fine(x, x, "nn")
            p = p + fine(p, x, "nn")
        return p

    @jax.custom_vjp
    def solve2(a, ti, r1, r2):
        return fine(ti, r1, "nn"), fine(ti, r2, "nn")

    def solve2_fwd(a, ti, r1, r2):
        s1, s2 = fine(ti, r1, "nn"), fine(ti, r2, "nn")
        return (s1, s2), (ti, s1, s2)

    def solve2_bwd(res, g):
        ti, s1, s2 = res
        d1, d2 = fine(ti, g[0], "tn"), fine(ti, g[1], "tn")
        return -(fine(d1, s1, "nt") + fine(d2, s2, "nt")), jnp.zeros_like(ti), d1, d2

    solve2.defvjp(solve2_fwd, solve2_bwd)

    def chunk_fn(pq, pk, pv, z, g_col, b_col, g_row, ogain, s_in, inverse=None):
        nb = pq.shape[0]
        sq = (nb, CHUNK, CHUNK)
        row = lax.broadcasted_iota(jnp.int32, sq, 1)
        col = lax.broadcasted_iota(jnp.int32, sq, 2)
        lower, strict = row >= col, row > col
        q, k, v = _silu(pq), _silu(pk), _silu(pv)
        q = q * lax.rsqrt(jnp.sum(q * q, axis=-1, keepdims=True) + EPS) * (DN_HEAD_DIM ** -0.5)
        k = k * lax.rsqrt(jnp.sum(k * k, axis=-1, keepdims=True) + EPS)
        gc_wide = jnp.broadcast_to(g_col, pq.shape)
        gc_i = jnp.broadcast_to(g_col, sq)
        gc_j = jnp.broadcast_to(g_row, sq)
        is_last = lax.broadcasted_iota(jnp.int32, pq.shape, 1) == CHUNK - 1
        g_last = jnp.sum(jnp.where(is_last, gc_wide, 0.0), axis=1, keepdims=True)
        decay = jnp.exp(jnp.where(lower, gc_i - gc_j, -jnp.inf))
        kb = k * b_col
        a_mat = jnp.where(strict, nt(kb, k) * decay, 0.0)
        eg = jnp.exp(gc_wide)
        ti = unit_lower_inverse(a_mat) if inverse is None else inverse
        u, w = solve2(a_mat, ti, v * b_col, kb * eg)
        attn = nt(q, k) * decay
        q_dec = q * eg
        k_dec = k * jnp.exp(g_last - gc_wide)
        c_dec = jnp.exp(g_last)
        v_new = u - nn(w, s_in)
        o = nn(q_dec, s_in) + nn(attn, v_new)
        s_out = s_in * c_dec + tn(k_dec, v_new)
        y = o * lax.rsqrt(jnp.mean(o * o, axis=-1, keepdims=True) + EPS) * ogain * _silu(z)
        return (y, s_out, ti) if inverse is None else (y, s_out)

    return chunk_fn


DN_PRECISION = None


def _chunk_specs(n_of):
    groups = DN_HEADS // DN_HB
    wide = DN_HB * DN_HEAD_DIM
    hd = pl.BlockSpec((CHUNK, wide), lambda h, n: (n_of(n), h))
    specs = dict(
        pq=hd,
        pk=pl.BlockSpec((CHUNK, wide), lambda h, n: (n_of(n), groups + h)),
        pv=pl.BlockSpec((CHUNK, wide), lambda h, n: (n_of(n), 2 * groups + h)),
        z=hd,
        col=pl.BlockSpec((DN_HB, CHUNK, 1), lambda h, n: (h, n_of(n), 0)),
        row=pl.BlockSpec((DN_HB, None, 1, CHUNK), lambda h, n: (h, n_of(n), 0, 0)),
        gain=pl.BlockSpec((1, DN_HEAD_DIM), lambda h, n: (0, 0)),
        state=pl.BlockSpec((DN_HB, None, DN_HEAD_DIM, DN_HEAD_DIM), lambda h, n: (h, n_of(n), 0, 0)),
        inverse=pl.BlockSpec((DN_HB, None, CHUNK, CHUNK), lambda h, n: (h, n_of(n), 0, 0)),
        qkv=pl.BlockSpec((CHUNK, DN_QKV), lambda h, n: (n_of(n), 0)),
        head=hd,
    )
    return specs


def _head_cols(j):
    return slice(j * DN_HEAD_DIM, (j + 1) * DN_HEAD_DIM)


def _split_heads(ref):
    return jnp.stack([ref[:, _head_cols(j)] for j in range(DN_HB)])


def _chunk_fwd(name, pre, proj, g_col, b_col, g_row, ogain):
    S = pre.shape[0]
    N = S // CHUNK
    chunk_fn = _chunk_math(DN_PRECISION)
    sp = _chunk_specs(lambda n: n)

    def body(pq, pk, pv, z, gc, bc, gr, og, y_ref, sin_ref, inv_ref, st):
        @pl.when(pl.program_id(1) == 0)
        def _():
            st[...] = jnp.zeros_like(st)

        s_in = st[...]
        sin_ref[...] = s_in
        y, s_out, inverse = chunk_fn(_split_heads(pq), _split_heads(pk), _split_heads(pv), _split_heads(z),
                                     gc[...], bc[...], gr[...], og[...], s_in)
        for j in range(DN_HB):
            y_ref[:, _head_cols(j)] = y[j].astype(y_ref.dtype)
        inv_ref[...] = inverse
        st[...] = s_out

    return pl.pallas_call(
        body, name=name, grid=(DN_HEADS // DN_HB, N),
        in_specs=[sp["pq"], sp["pk"], sp["pv"], sp["z"], sp["col"], sp["col"], sp["row"], sp["gain"]],
        out_specs=[sp["head"], sp["state"], sp["inverse"]],
        out_shape=[jax.ShapeDtypeStruct((S, DN_WIDTH), BF16),
                   jax.ShapeDtypeStruct((DN_HEADS, N, DN_HEAD_DIM, DN_HEAD_DIM), F32),
                   jax.ShapeDtypeStruct((DN_HEADS, N, CHUNK, CHUNK), F32)],
        scratch_shapes=[pltpu.VMEM((DN_HB, DN_HEAD_DIM, DN_HEAD_DIM), F32)],
        compiler_params=_cparams(("parallel", "arbitrary")),
    )(pre, pre, pre, proj, g_col, b_col, g_row, ogain)


def _chunk_bwd(name, pre, proj, g_col, b_col, g_row, ogain, s_in_all, inverse_all, dy):
    assert DN_HB == DN_HEADS
    S = pre.shape[0]
    N = S // CHUNK
    chunk_fn = _chunk_math(DN_PRECISION)
    sp = _chunk_specs(lambda n: N - 1 - n)

    def body(pq, pk, pv, z, gc, bc, gr, og, sin_ref, inv_ref, dy_ref,
             dpre_ref, dz_ref, dgc_ref, dbc_ref, dgr_ref, dog_ref, ds):
        @pl.when(pl.program_id(1) == 0)
        def _():
            ds[...] = jnp.zeros_like(ds)
            dog_ref[...] = jnp.zeros_like(dog_ref)

        inverse = inv_ref[...]
        prim = (_split_heads(pq), _split_heads(pk), _split_heads(pv), _split_heads(z),
                gc[...], bc[...], gr[...], og[...], sin_ref[...])
        _, vjp = jax.vjp(lambda *a: chunk_fn(*a, inverse=inverse), *prim)
        gq, gk, gv, gz, ggc, gbc, ggr, gog, gs = vjp((_split_heads(dy_ref), ds[...]))
        for j in range(DN_HB):
            for part, g in enumerate((gq, gk, gv)):
                dpre_ref[:, pl.ds(part * DN_WIDTH + j * DN_HEAD_DIM, DN_HEAD_DIM)] = g[j]
            dz_ref[:, _head_cols(j)] = gz[j]
        dgc_ref[...] = ggc
        dbc_ref[...] = gbc
        dgr_ref[...] = ggr
        dog_ref[...] += gog
        ds[...] = gs

    hd = sp["head"]
    return pl.pallas_call(
        body, name=name, grid=(1, N),
        in_specs=[sp["pq"], sp["pk"], sp["pv"], sp["z"], sp["col"], sp["col"], sp["row"], sp["gain"],
                  sp["state"], sp["inverse"], hd],
        out_specs=[sp["qkv"], hd, sp["col"], sp["col"], sp["row"], sp["gain"]],
        out_shape=[jax.ShapeDtypeStruct((S, DN_QKV), F32), jax.ShapeDtypeStruct((S, DN_WIDTH), F32)]
        + [jax.ShapeDtypeStruct((DN_HEADS, S, 1), F32)] * 2
        + [jax.ShapeDtypeStruct((DN_HEADS, N, 1, CHUNK), F32), jax.ShapeDtypeStruct((1, DN_HEAD_DIM), F32)],
        scratch_shapes=[pltpu.VMEM((DN_HB, DN_HEAD_DIM, DN_HEAD_DIM), F32)],
        compiler_params=_cparams(("arbitrary", "arbitrary")),
    )(pre, pre, pre, proj, g_col, b_col, g_row, ogain, s_in_all, inverse_all, dy)


def _mlp_ple_fwd(tag, x_in, p_l, norm_mlp, w_up, w_down, norm_ple, w_ple, w_gate):
    h = _rms_fwd(f"{tag}_mlp_norm", x_in, norm_mlp)
    u, a = _mm(f"{tag}_up", h, w_up, "nn", out_dtypes=(F32, BF16), epilogue=_relu2_epilogue)
    x_mid = _mm(f"{tag}_down", a, w_down, "nn", extras=(x_in,), epilogue=lambda acc, r: (acc + r,))
    hg = _rms_fwd(f"{tag}_ple_norm", x_mid, norm_ple)
    zg = _mm(f"{tag}_gate", hg, w_gate, "nn")
    pp = _mm(f"{tag}_ple", p_l, w_ple, "nn")
    x_out = _rowwise(f"{tag}_ple_out", _ple_fwd_fn, [x_mid, pp, zg], [], [(D_MODEL, F32)])
    return x_out, dict(x_in=x_in, h=h, u=u, a=a, x_mid=x_mid, hg=hg, zg=zg, pp=pp)


def _mlp_ple_bwd(tag, dx, sv, p_l, norm_mlp, w_up, w_down, norm_ple, w_ple, w_gate):
    dpp, dzg = _rowwise(f"{tag}_ple_bwd", _ple_bwd_fn, [dx, sv["pp"], sv["zg"]], [],
                        [(D_MODEL, BF16), (D_MODEL, BF16)])
    d_w_ple = _mm(f"{tag}_d_w_ple", p_l, dpp, "tn", out_dtypes=(BF16,))
    d_w_gate = _mm(f"{tag}_d_w_gate", sv["hg"], dzg, "tn", out_dtypes=(BF16,))
    dhg = _mm(f"{tag}_d_hg", dzg, w_gate, "nt")
    dx_mid, d_norm_ple = _rms_bwd(f"{tag}_ple_norm_bwd", sv["x_mid"], norm_ple, dx, [dhg])
    du = _mm(f"{tag}_d_u", dx_mid, w_down, "nt", out_dtypes=(BF16,), extras=(sv["u"],),
             epilogue=_relu2_bwd_epilogue)
    d_w_down = _mm(f"{tag}_d_w_down", sv["a"], dx_mid, "tn", out_dtypes=(BF16,))
    d_w_up = _mm(f"{tag}_d_w_up", sv["h"], du, "tn", out_dtypes=(BF16,))
    dh = _mm(f"{tag}_d_h", du, w_up, "nt")
    dx_in, d_norm_mlp = _rms_bwd(f"{tag}_mlp_norm_bwd", sv["x_in"], norm_mlp, dx_mid, [dh])
    return dx_in, dict(mlp_norm=d_norm_mlp, w_up=d_w_up, w_down=d_w_down, ple_norm=d_norm_ple,
                       w_ple=d_w_ple, w_ple_gate=d_w_gate)


def _local_step(x, p, positions, target, small, big):
    S = x.shape[0]
    ct, st = _rope_lane_tables(positions)
    bd = _head_mean_matrix()

    h0 = _rms_fwd("l0_mix_norm", x, small["mix_norm"][0])
    attn = []
    for g, (window, d) in enumerate(SWA_GROUPS):
        assert window // d == BAND and (S // d) % BAND == 0
        h0g = _to_classes(h0, d)
        ctg, stg = _to_classes(ct, d), _to_classes(st, d)
        w_g = big["attn_w_qkv"][:, g * 3 * A_WIDTH:(g + 1) * 3 * A_WIDTH]
        gq = jnp.tile(small["attn_q_gain"][0, g], A_HEADS).reshape(1, A_WIDTH)
        gk = jnp.tile(small["attn_k_gain"][0, g], A_HEADS).reshape(1, A_WIDTH)
        qkv = _mm(f"l0_qkv{g}", h0g, w_g, "nn")
        q, k, v = _rowwise(f"l0_qk_prep{g}", _qk_prep_fwd_fn, [qkv, ctg, stg], [gq, gk, bd], [(A_WIDTH, BF16)] * 3)
        o, lse = _attn_fwd(f"l0_attn{g}", q, k, v, (S // d) // BAND)
        attn.append(dict(d=d, h0g=h0g, ct=ctg, st=stg, w=w_g, gq=gq, gk=gk, qkv=qkv, q=q, k=k, v=v, o=o, lse=lse,
                         o_tok=_from_classes(o, d), lse_tok=_from_classes(lse, d)))
    om = _rowwise("l0_merge", _merge_fwd_fn, [a["o_tok"] for a in attn] + [a["lse_tok"] for a in attn], [],
                  [(A_WIDTH, BF16)])
    x1 = _mm("l0_attn_out", om, big["attn_w_o"], "nn", extras=(x,), epilogue=lambda acc, r: (acc + r,))
    x3, sv0 = _mlp_ple_fwd("l0", x1, p[0], small["mlp_norm"][0], big["w_up"][0], big["w_down"][0],
                           small["ple_norm"][0], big["w_ple"][0], big["w_ple_gate"][0])

    N = S // CHUNK
    h3 = _rms_fwd("l1_mix_norm", x3, small["mix_norm"][1])
    proj = _mm("l1_in", h3, big["dn_w_in"], "nn")
    pre = _conv_fwd("l1_conv", proj, small["dn_conv"])
    ab = proj[:, DN_AB0:DN_AB0 + DN_AB_PAD]
    lane_pad = DN_AB_PAD - DN_HEADS
    alog_row = jnp.pad(small["dn_a_log"][0], (0, lane_pad)).reshape(1, DN_AB_PAD)
    dt_row = jnp.pad(small["dn_dt_bias"][0], (0, lane_pad)).reshape(1, DN_AB_PAD)
    cum = _chunk_cumsum_matrix()
    gb = _rowwise("l1_gates", _gates_fwd_fn, [ab], [alog_row, dt_row, cum], [(DN_AB_PAD, F32)], tm=GATES_ROWS)
    g_t, b_t = gb[:, :DN_HEADS].T, gb[:, DN_HEADS:2 * DN_HEADS].T
    g_col, b_col = g_t.reshape(DN_HEADS, S, 1), b_t.reshape(DN_HEADS, S, 1)
    g_row = g_t.reshape(DN_HEADS, N, 1, CHUNK)
    ogain = small["dn_o_gain"][0].reshape(1, DN_HEAD_DIM)
    y, s_in_all, inverse_all = _chunk_fwd("l1_delta", pre, proj, g_col, b_col, g_row, ogain)
    x4 = _mm("l1_dn_out", y, big["dn_w_o"], "nn", extras=(x3,), epilogue=lambda acc, r: (acc + r,))
    x6, sv1 = _mlp_ple_fwd("l1", x4, p[1], small["mlp_norm"][1], big["w_up"][1], big["w_down"][1],
                           small["ple_norm"][1], big["w_ple"][1], big["w_ple_gate"][1])

    dy, sq = _rowwise("loss", _loss_fn, [x6, target], [], [(D_MODEL, F32)], [(1, 128)])

    dx4, gl1 = _mlp_ple_bwd("l1", dy, sv1, p[1], small["mlp_norm"][1], big["w_up"][1], big["w_down"][1],
                            small["ple_norm"][1], big["w_ple"][1], big["w_ple_gate"][1])
    d_y = _mm("l1_d_y", dx4, big["dn_w_o"], "nt")
    d_dn_w_o = _mm("l1_d_w_o", y, dx4, "tn", out_dtypes=(BF16,))
    dpre, dz, dg_col, db_col, dg_row, d_ogain = _chunk_bwd(
        "l1_delta_bwd", pre, proj, g_col, b_col, g_row, ogain, s_in_all, inverse_all, d_y)
    dconv_in, d_conv_w = _conv_bwd("l1_conv_bwd", proj, dpre, small["dn_conv"])
    dg_t = dg_col.reshape(DN_HEADS, S) + dg_row.reshape(DN_HEADS, S)
    dgb = jnp.pad(jnp.concatenate([dg_t, db_col.reshape(DN_HEADS, S)], axis=0).T,
                  ((0, 0), (0, DN_AB_PAD - 2 * DN_HEADS)))
    dab, d_alog, d_dt = _rowwise("l1_gates_bwd", _gates_bwd_fn, [ab, dgb], [alog_row, dt_row, cum],
                                 [(DN_AB_PAD, F32)], [(1, DN_AB_PAD), (1, DN_AB_PAD)], tm=GATES_ROWS)
    dproj = jnp.concatenate([dz.astype(BF16), dconv_in, dab.astype(BF16)], axis=1)
    d_dn_w_in = _mm("l1_d_w_in", h3, dproj, "tn", out_dtypes=(BF16,))
    dh3 = _mm("l1_d_h", dproj, big["dn_w_in"], "nt")
    dx3, d_mix1 = _rms_bwd("l1_mix_norm_bwd", x3, small["mix_norm"][1], dx4, [dh3])

    dx1, gl0 = _mlp_ple_bwd("l0", dx3, sv0, p[0], small["mlp_norm"][0], big["w_up"][0], big["w_down"][0],
                            small["ple_norm"][0], big["w_ple"][0], big["w_ple_gate"][0])
    dom = _mm("l0_d_om", dx1, big["attn_w_o"], "nt")
    d_attn_w_o = _mm("l0_d_w_o", om, dx1, "tn", out_dtypes=(BF16,))
    merged = _rowwise("l0_merge_bwd", _merge_bwd_fn,
                      [a["o_tok"] for a in attn] + [a["lse_tok"] for a in attn] + [dom], [], [(A_WIDTH, F32)] * 6)
    dh0, d_w_qkv, d_gq, d_gk = [], [], [], []
    for g, a in enumerate(attn):
        do_g, dl_g = _to_classes(merged[g], a["d"]), _to_classes(merged[3 + g], a["d"])
        dqn, dkn, dvn = _attn_bwd(f"l0_attn_bwd{g}", a["q"], a["k"], a["v"], a["o"], a["lse"], do_g, dl_g,
                                  (S // a["d"]) // BAND)
        dqkv, dgq, dgk = _rowwise(f"l0_qk_prep_bwd{g}", _qk_prep_bwd_fn, [a["qkv"], a["ct"], a["st"], dqn, dkn, dvn],
                                  [a["gq"], a["gk"], bd], [(3 * A_WIDTH, BF16)], [(1, A_HEAD_DIM)] * 2)
        d_w_qkv.append(_mm(f"l0_d_w_qkv{g}", a["h0g"], dqkv, "tn", out_dtypes=(BF16,)))
        dh0.append(_from_classes(_mm(f"l0_d_h{g}", dqkv, a["w"], "nt"), a["d"]))
        d_gq.append(dgq)
        d_gk.append(dgk)
    grad_x, d_mix0 = _rms_bwd("l0_mix_norm_bwd", x, small["mix_norm"][0], dx1, dh0)

    grads = dict(
        mix_norm=jnp.concatenate([d_mix0, d_mix1], axis=0),
        attn_w_qkv=jnp.concatenate(d_w_qkv, axis=1),
        attn_q_gain=jnp.concatenate(d_gq, axis=0)[None],
        attn_k_gain=jnp.concatenate(d_gk, axis=0)[None],
        attn_w_o=d_attn_w_o,
        dn_w_in=jnp.concatenate([d_dn_w_in[:, DN_QKV0:DN_AB0 + 2 * DN_HEADS], d_dn_w_in[:, :DN_WIDTH]], axis=1),
        dn_conv=d_conv_w,
        dn_a_log=d_alog[:, :DN_HEADS],
        dn_dt_bias=d_dt[:, :DN_HEADS],
        dn_o_gain=d_ogain,
        dn_w_o=d_dn_w_o,
        mlp_norm=jnp.concatenate([gl0["mlp_norm"], gl1["mlp_norm"]], axis=0),
        w_up=jnp.stack([gl0["w_up"], gl1["w_up"]]),
        w_down=jnp.stack([gl0["w_down"], gl1["w_down"]]),
        ple_norm=jnp.concatenate([gl0["ple_norm"], gl1["ple_norm"]], axis=0),
        w_ple=jnp.stack([gl0["w_ple"], gl1["w_ple"]]),
        w_ple_gate=jnp.stack([gl0["w_ple_gate"], gl1["w_ple_gate"]]),
    )
    return sq, grad_x, grads


def _chip_peer(x, y, c, t):
    return (jnp.bitwise_xor(x, t >> 1), jnp.bitwise_xor(y, t & 1), c)


def _place():
    x, y, c = lax.axis_index("x"), lax.axis_index("y"), lax.axis_index("c")
    return x, y, c, 2 * x + y, (x, y, 1 - c)


def _remote(src, dst, send_sem, recv_sem, to):
    return pltpu.make_async_remote_copy(src_ref=src, dst_ref=dst, send_sem=send_sem, recv_sem=recv_sem,
                                        device_id=to, device_id_type=MESH)


def _hbm_call(name, body, ins, out_shape, scratch_shapes):
    any_spec = pl.BlockSpec(memory_space=pl.ANY)
    return pl.pallas_call(body, name=name, out_shape=out_shape, in_specs=[any_spec] * len(ins),
                          out_specs=[any_spec] * len(out_shape), scratch_shapes=scratch_shapes)(*ins)


def _half(n0, which):
    return pl.ds(which * (n0 // 2), n0 // 2)


def _gather_shards(name, shards):
    T = len(shards)

    def body(*refs):
        ins, outs = refs[:T], refs[T:2 * T]
        send, recv = refs[2 * T:]
        x, y, c, q, sibling = _place()
        sends = []
        for i in range(T):
            mine = _half(ins[i].shape[0], c)
            for t in range(1, N_CHIPS):
                cp = _remote(ins[i].at[mine], outs[i].at[q, mine], send.at[i, t - 1], recv.at[i, t - 1],
                             _chip_peer(x, y, c, t))
                cp.start()
                sends.append(cp)
        for i in range(T):
            mine = _half(ins[i].shape[0], c)
            for t in range(1, N_CHIPS):
                landed = outs[i].at[jnp.bitwise_xor(q, t), mine]
                _remote(landed, landed, send.at[i, t - 1], recv.at[i, t - 1], _chip_peer(x, y, c, t)).wait_recv()
                cp = _remote(landed, landed, send.at[i, 2 + t], recv.at[i, 2 + t], sibling)
                cp.start()
                sends.append(cp)
        for i in range(T):
            theirs = _half(ins[i].shape[0], 1 - c)
            for t in range(1, N_CHIPS):
                passed = outs[i].at[jnp.bitwise_xor(q, t), theirs]
                _remote(passed, passed, send.at[i, 2 + t], recv.at[i, 2 + t], sibling).wait_recv()
        for cp in sends:
            cp.wait_send()

    n_rel = 2 * (N_CHIPS - 1)
    return _hbm_call(name, body, shards,
                     [jax.ShapeDtypeStruct((N_CHIPS,) + s.shape, s.dtype) for s in shards],
                     [pltpu.SemaphoreType.DMA((T, n_rel)), pltpu.SemaphoreType.DMA((T, n_rel))])


def _other_half_from_sibling(name, stacks):
    T = len(stacks)

    def body(*refs):
        ins, outs = refs[:T], refs[T:2 * T]
        send, recv = refs[2 * T:]
        x, y, c, q, sibling = _place()
        copies = []
        for i in range(T):
            rc = _remote(ins[i].at[:, _half(ins[i].shape[1], 1 - c)], outs[i], send.at[i], recv.at[i], sibling)
            rc.start()
            copies.append(rc)
        for cp in copies:
            cp.wait()

    return _hbm_call(name, body, stacks,
                     [jax.ShapeDtypeStruct((s.shape[0], s.shape[1] // 2) + s.shape[2:], s.dtype) for s in stacks],
                     [pltpu.SemaphoreType.DMA((T,)), pltpu.SemaphoreType.DMA((T,))])


def _scatter_to_chips(name, stacks):
    T = len(stacks)

    def body(*refs):
        ins, outs = refs[:T], refs[T:2 * T]
        send, recv = refs[2 * T:]
        x, y, c, q, sibling = _place()
        copies = []
        for i in range(T):
            for t in range(1, N_CHIPS):
                rc = _remote(ins[i].at[jnp.bitwise_xor(q, t)], outs[i].at[t - 1], send.at[i, t - 1],
                             recv.at[i, t - 1], _chip_peer(x, y, c, t))
                rc.start()
                copies.append(rc)
        for cp in copies:
            cp.wait()

    return _hbm_call(name, body, stacks,
                     [jax.ShapeDtypeStruct((N_CHIPS - 1,) + s.shape[1:], s.dtype) for s in stacks],
                     [pltpu.SemaphoreType.DMA((T, N_CHIPS - 1)), pltpu.SemaphoreType.DMA((T, N_CHIPS - 1))])


def _swap_with_sibling(name, arrays):
    T = len(arrays)

    def body(*refs):
        ins, outs = refs[:T], refs[T:2 * T]
        send, recv = refs[2 * T:]
        x, y, c, q, sibling = _place()
        copies = []
        for i in range(T):
            rc = _remote(ins[i], outs[i], send.at[i], recv.at[i], sibling)
            rc.start()
            copies.append(rc)
        for cp in copies:
            cp.wait()

    return _hbm_call(name, body, arrays, [jax.ShapeDtypeStruct(a.shape, a.dtype) for a in arrays],
                     [pltpu.SemaphoreType.DMA((T,)), pltpu.SemaphoreType.DMA((T,))])


def _gather_from_all(name, block):
    R, C = block.shape

    def body(src, out, send_sems, recv_sems):
        x, y, c = lax.axis_index("x"), lax.axis_index("y"), lax.axis_index("c")
        me = 4 * x + 2 * y + c
        out[me] = src[...]
        copies = []
        for r in range(1, N_DEV):
            peer = (jnp.bitwise_xor(x, r >> 2), jnp.bitwise_xor(y, (r >> 1) & 1), jnp.bitwise_xor(c, r & 1))
            cp = pltpu.make_async_remote_copy(src_ref=src, dst_ref=out.at[me], send_sem=send_sems.at[r - 1],
                                              recv_sem=recv_sems.at[r - 1], device_id=peer, device_id_type=MESH)
            cp.start()
            copies.append(cp)
        for cp in copies:
            cp.wait()

    return pl.pallas_call(
        body, name=name, out_shape=jax.ShapeDtypeStruct((N_DEV, R, C), block.dtype),
        in_specs=[pl.BlockSpec(memory_space=pltpu.VMEM)], out_specs=pl.BlockSpec(memory_space=pltpu.VMEM),
        scratch_shapes=[pltpu.SemaphoreType.DMA((N_DEV - 1,)), pltpu.SemaphoreType.DMA((N_DEV - 1,))],
    )(block)


def _view(a):
    return a[0] if a.shape[0] == 1 else a


def _view_axis(a, axis):
    return axis - 1 if a.shape[0] == 1 else axis


def _rows(a):
    return a.reshape(-1, a.shape[-1])


def _elementwise(name, fn, ins, out_dtypes, tm):
    specs = []
    for a in ins:
        a, row0 = a if isinstance(a, tuple) else (a, 0)
        specs.append((_rows(a), a.shape[-1], 0, row0))
    shape = ins[0][0].shape if isinstance(ins[0], tuple) else ins[0].shape
    outs = _rowwise(name, fn, specs, [], [(shape[-1], dt) for dt in out_dtypes], tm=tm, n_rows=math.prod(shape[:-1]))
    return outs.reshape(shape) if len(out_dtypes) == 1 else [o.reshape(shape) for o in outs]


SMALL_ROWS = 8
CONV_ROWS = CONV_WIDTH * DN_QKV // D_MODEL
SMALL_GRAD_ROWS = 24


def _pack_small(vals, conv=None):
    tail = jnp.concatenate([vals["attn_q_gain"].reshape(-1), vals["attn_k_gain"].reshape(-1),
                            vals["dn_a_log"].reshape(-1), vals["dn_dt_bias"].reshape(-1),
                            vals["dn_o_gain"].reshape(-1)])
    tail = jnp.pad(tail, (0, D_MODEL - tail.shape[0])).reshape(1, D_MODEL)
    rows = [vals["mix_norm"], vals["mlp_norm"], vals["ple_norm"], tail, jnp.zeros((1, D_MODEL), F32)]
    if conv is not None:
        rows += [conv.reshape(CONV_ROWS, D_MODEL),
                 jnp.zeros((SMALL_GRAD_ROWS - SMALL_ROWS - CONV_ROWS, D_MODEL), F32)]
    return jnp.concatenate(rows, axis=0)


def _unpack_small(block):
    nq = 3 * A_HEAD_DIM
    t = block[6]
    return dict(
        mix_norm=block[0:2], mlp_norm=block[2:4], ple_norm=block[4:6],
        attn_q_gain=t[:nq].reshape(1, 3, A_HEAD_DIM), attn_k_gain=t[nq:2 * nq].reshape(1, 3, A_HEAD_DIM),
        dn_a_log=t[2 * nq:2 * nq + DN_HEADS].reshape(1, DN_HEADS),
        dn_dt_bias=t[2 * nq + DN_HEADS:2 * nq + 2 * DN_HEADS].reshape(1, DN_HEADS),
        dn_o_gain=t[2 * nq + 2 * DN_HEADS:2 * nq + 2 * DN_HEADS + DN_HEAD_DIM].reshape(1, DN_HEAD_DIM))


def kernel(x, p, positions, mix_norm, attn_w_qkv, attn_q_gain, attn_k_gain, attn_w_o, dn_w_in, dn_conv, dn_a_log, dn_dt_bias, dn_o_gain, dn_w_o, mlp_norm, w_up, w_down, ple_norm, w_ple, w_ple_gate, loss_target, m_mix_norm, m_attn_w_qkv, m_attn_q_gain, m_attn_k_gain, m_attn_w_o, m_dn_w_in, m_dn_conv, m_dn_a_log, m_dn_dt_bias, m_dn_o_gain, m_dn_w_o, m_mlp_norm, m_w_up, m_w_down, m_ple_norm, m_w_ple, m_w_ple_gate, v_mix_norm, v_attn_w_qkv, v_attn_q_gain, v_attn_k_gain, v_attn_w_o, v_dn_w_in, v_dn_conv, v_dn_a_log, v_dn_dt_bias, v_dn_o_gain, v_dn_w_o, v_mlp_norm, v_w_up, v_w_down, v_ple_norm, v_w_ple, v_w_ple_gate):
    given = dict(locals())
    w = {n: given[n] for n in WEIGHTS}
    m = {n: given["m_" + n] for n in WEIGHTS}
    v = {n: given["v_" + n] for n in WEIGHTS}
    kinds = ("grad", "delta", "new_m", "new_v")
    big_names = [(n, _view_axis(w[n], axis)) for n, axis in SHARDED if n != "dn_conv"]

    chip = 2 * lax.axis_index("x") + lax.axis_index("y")
    core = lax.axis_index("c")
    shards = [_view(w[n]).astype(BF16) for n, _ in big_names]
    gathered = _gather_shards("gather_weights", shards)
    full = {n: jnp.concatenate([jnp.where(chip == q, s, g[q]) for q in range(N_CHIPS)], axis=axis)
            for (n, axis), s, g in zip(big_names, shards, gathered)}
    conv_block = jnp.pad(w["dn_conv"].reshape(-1), (0, SMALL_ROWS * D_MODEL - w["dn_conv"].size))
    conv_all = _gather_from_all("gather_conv", conv_block.reshape(SMALL_ROWS, D_MODEL))
    conv_all = conv_all.reshape(N_CHIPS, 2, -1)[:, 0, :w["dn_conv"].size]
    conv_full = jnp.concatenate([conv_all[q].reshape(CONV_WIDTH, -1) for q in range(N_CHIPS)], axis=1)

    w_in = full["dn_w_in"]
    n_ab = 2 * DN_HEADS
    big = dict(full)
    big["dn_w_in"] = jnp.concatenate([w_in[:, DN_QKV + n_ab:], w_in[:, :DN_QKV + n_ab],
                                      jnp.zeros((D_MODEL, DN_AB_PAD - n_ab), BF16)], axis=1)
    small = {n: w[n] for n in REPLICATED}
    small["dn_conv"] = conv_full

    sq, grad_x, grads = _local_step(x[0], p[:, 0], positions[0], loss_target[0], small, big)
    loss = lax.psum(0.5 * sq[0, 0] / D_MODEL, ("x", "y", "c"))
    out = {}

    stacks = [jnp.stack(jnp.split(grads[n], N_CHIPS, axis=axis)) for n, axis in big_names]
    mine = [lax.dynamic_slice_in_dim(s, core * (s.shape[1] // 2), s.shape[1] // 2, axis=1) for s in stacks]
    theirs = _other_half_from_sibling("split_core_grads", stacks)
    chip_sums = [_elementwise(f"add_core_{n}", lambda a, b: a.astype(F32) + b.astype(F32), [a, b], [BF16], 128)
                 for (n, _), a, b in zip(big_names, mine, theirs)]
    landed = _scatter_to_chips("scatter_grads", chip_sums)
    half_sums = []
    for (n, _), s, r in zip(big_names, chip_sums, landed):
        o = lax.dynamic_index_in_dim(s, chip, axis=0, keepdims=False)
        per = math.prod(o.shape[:-1])
        half_sums.append(_elementwise(
            f"add_chips_{n}", lambda a, b, c, d: ((a.astype(F32) + b.astype(F32)) + c.astype(F32)) + d.astype(F32),
            [o, (r, 0), (r, per), (r, 2 * per)], [F32], 128))
    other_halves = _swap_with_sibling("join_core_sums", half_sums)
    for (n, _), a, b in zip(big_names, half_sums, other_halves):
        g = jnp.where(core == 0, jnp.concatenate([a, b], axis=0), jnp.concatenate([b, a], axis=0))
        shp = w[n].shape
        res = _elementwise(f"adamw_{n}", lambda g, w_, m_, v_: (g,) + _adamw(w_, g, m_, v_),
                           [g.reshape(shp), w[n], m[n], v[n]], [F32] * 4, 256)
        for kind, arr in zip(kinds, res):
            out[kind + "_" + n] = arr.reshape(shp)

    slots = _gather_from_all("gather_small_grads", _pack_small(grads, grads["dn_conv"]))

    def small_body(s_ref, w_ref, m_ref, v_ref, sum_out, g_out, d_out, m_out, v_out):
        total = s_ref[0]
        for d in range(1, N_DEV):
            total = total + s_ref[d]
        sum_out[...] = total
        g = total[:SMALL_ROWS]
        for o, r in zip((g_out, d_out, m_out, v_out), (g,) + _adamw(w_ref[...], g, m_ref[...], v_ref[...])):
            o[...] = r

    res = pl.pallas_call(small_body, name="adamw_replicated",
                         out_shape=[jax.ShapeDtypeStruct((SMALL_GRAD_ROWS, D_MODEL), F32)]
                         + [jax.ShapeDtypeStruct((SMALL_ROWS, D_MODEL), F32)] * 4)(
        slots, _pack_small(w), _pack_small(m), _pack_small(v))
    for kind, block in zip(kinds, res[1:]):
        for n, arr in _unpack_small(block).items():
            out[kind + "_" + n] = arr
    conv_sum = res[0][SMALL_ROWS:SMALL_ROWS + CONV_ROWS].reshape(CONV_WIDTH, DN_QKV)
    cols = DN_QKV // N_CHIPS
    chip = 2 * lax.axis_index("x") + lax.axis_index("y")
    conv_mine = lax.dynamic_slice_in_dim(conv_sum, chip * cols, cols, axis=1)
    res = _elementwise("adamw_dn_conv", lambda g, w_, m_, v_: (g,) + _adamw(w_, g, m_, v_),
                       [conv_mine, w["dn_conv"][0], m["dn_conv"][0], v["dn_conv"][0]], [F32] * 4, CONV_WIDTH)
    for kind, arr in zip(kinds, res):
        out[kind + "_dn_conv"] = arr[None]

    return (loss, grad_x[None],
            *[out["grad_" + n] for n in WEIGHTS], *[out["delta_" + n] for n in WEIGHTS],
            *[out["new_m_" + n] for n in WEIGHTS], *[out["new_v_" + n] for n in WEIGHTS])
```

```python
import functools
import math

import jax
import jax.numpy as jnp
from jax import lax
from jax.experimental import pallas as pl
from jax.experimental.pallas import tpu as pltpu

F32 = jnp.float32
BF16 = jnp.bfloat16
HIGHEST = lax.Precision.HIGHEST

D_MODEL = 1024
EPS = 1e-6
SWA_GROUPS = ((128, 1), (512, 4), (2048, 16))
A_HEADS = 8
A_HEAD_DIM = 64
A_WIDTH = A_HEADS * A_HEAD_DIM
ROPE_DIM = A_HEAD_DIM // 4
ROPE_THETA = 500000.0
BAND = 128
DN_HEADS = 8
DN_HEAD_DIM = 128
DN_WIDTH = DN_HEADS * DN_HEAD_DIM
DN_QKV = 3 * DN_WIDTH
DN_AB_PAD = 128
DN_IN_PAD = DN_WIDTH + DN_QKV + DN_AB_PAD
DN_QKV0 = DN_WIDTH
DN_AB0 = DN_WIDTH + DN_QKV
DN_HB = 8
CONV_WIDTH = 4
CHUNK = 64
PLE_DIM = 256
D_FF = 4 * D_MODEL

ADAM_LR = 0.001
ADAM_B1 = 0.9
ADAM_B2 = 0.999
ADAM_EPS = 1e-08
ADAM_WD = 0.01
ADAM_STEP = 10

N_CHIPS = 4
N_DEV = 8
VMEM_LIMIT = 48 * 1024 * 1024
MESH = pl.DeviceIdType.MESH

SHARDED = (
    ("attn_w_qkv", 2), ("attn_w_o", 2), ("dn_w_in", 2), ("dn_conv", 2), ("dn_w_o", 1),
    ("w_up", 2), ("w_down", 1), ("w_ple", 2), ("w_ple_gate", 1))
REPLICATED = ("mix_norm", "attn_q_gain", "attn_k_gain", "dn_a_log", "dn_dt_bias", "dn_o_gain",
              "mlp_norm", "ple_norm")
WEIGHTS = ("mix_norm", "attn_w_qkv", "attn_q_gain", "attn_k_gain", "attn_w_o", "dn_w_in", "dn_conv",
           "dn_a_log", "dn_dt_bias", "dn_o_gain", "dn_w_o", "mlp_norm", "w_up", "w_down", "ple_norm",
           "w_ple", "w_ple_gate")


def _cparams(sem=None):
    return pltpu.CompilerParams(dimension_semantics=sem, vmem_limit_bytes=VMEM_LIMIT)


def _pick(n, cap, quantum=128):
    best = None
    for t in range(quantum, min(n, cap) + 1, quantum):
        if n % t == 0:
            best = t
    return n if best is None else best


_DIMS = {"nn": ((1,), (0,)), "nt": ((1,), (1,)), "tn": ((0,), (0,))}


def _mm(name, a, b, mode, out_dtypes=(F32,), extras=(), epilogue=None):
    if mode == "nn":
        (M, K), (K2, N) = a.shape, b.shape
    elif mode == "nt":
        (M, K), (N, K2) = a.shape, b.shape
    else:
        (K, M), (K2, N) = a.shape, b.shape
    assert K == K2, (name, a.shape, b.shape)
    tm, tn, tk = _pick(M, 1024), _pick(N, 1536), _pick(K, 1024 if mode == "tn" else 1536)
    nk = K // tk
    if mode == "nn":
        a_spec = pl.BlockSpec((tm, tk), lambda i, j, k: (i, k))
        b_spec = pl.BlockSpec((tk, tn), lambda i, j, k: (k, j))
    elif mode == "nt":
        a_spec = pl.BlockSpec((tm, tk), lambda i, j, k: (i, k))
        b_spec = pl.BlockSpec((tn, tk), lambda i, j, k: (j, k))
    else:
        a_spec = pl.BlockSpec((tk, tm), lambda i, j, k: (k, i))
        b_spec = pl.BlockSpec((tk, tn), lambda i, j, k: (k, j))
    o_spec = pl.BlockSpec((tm, tn), lambda i, j, k: (i, j))
    n_extra, n_out = len(extras), len(out_dtypes)
    dims = (_DIMS[mode], ((), ()))

    def body(a_ref, b_ref, *rest):
        extra_refs, out_refs = rest[:n_extra], rest[n_extra:n_extra + n_out]
        k = pl.program_id(2)
        part = lax.dot_general(a_ref[...].astype(BF16), b_ref[...].astype(BF16), dims, preferred_element_type=F32)

        def finish(total):
            vals = (total,) if epilogue is None else epilogue(total, *[e[...] for e in extra_refs])
            for o, v in zip(out_refs, vals):
                o[...] = v.astype(o.dtype)

        if nk == 1:
            finish(part)
            return
        acc = rest[-1]

        @pl.when(k == 0)
        def _():
            acc[...] = part

        @pl.when(jnp.logical_and(k > 0, k < nk - 1))
        def _():
            acc[...] += part

        @pl.when(k == nk - 1)
        def _():
            finish(acc[...] + part)

    outs = pl.pallas_call(
        body, name=name, grid=(M // tm, N // tn, nk),
        in_specs=[a_spec, b_spec] + [o_spec] * n_extra,
        out_specs=[o_spec] * n_out,
        out_shape=[jax.ShapeDtypeStruct((M, N), dt) for dt in out_dtypes],
        scratch_shapes=[pltpu.VMEM((tm, tn), F32)] if nk > 1 else [],
        compiler_params=_cparams(("parallel", "parallel", "arbitrary")),
    )(a, b, *extras)
    return outs[0] if n_out == 1 else outs


def _rowwise(name, fn, rows, bcast, row_outs, acc_outs=(), tm=256, n_rows=None):
    rows = [r if isinstance(r, tuple) else (r, r.shape[1], 0) for r in rows]
    rows = [r if len(r) == 4 else r + (0,) for r in rows]
    S = rows[0][0].shape[0] if n_rows is None else n_rows
    tm = min(tm, S)
    assert S % tm == 0 and all(r[3] % tm == 0 for r in rows), (name, S, tm)
    n_row, n_bc, n_ro, n_acc = len(rows), len(bcast), len(row_outs), len(acc_outs)
    in_specs = [pl.BlockSpec((tm, w), functools.partial(lambda i, cb, rb: (i + rb, cb), cb=cb, rb=r0 // tm))
                for _, w, cb, r0 in rows]
    in_specs += [pl.BlockSpec(b.shape, lambda i: (0, 0)) for b in bcast]
    out_specs = [pl.BlockSpec((tm, c), lambda i: (i, 0)) for c, _ in row_outs]
    out_specs += [pl.BlockSpec(s, lambda i: (0, 0)) for s in acc_outs]
    out_shape = [jax.ShapeDtypeStruct((S, c), dt) for c, dt in row_outs]
    out_shape += [jax.ShapeDtypeStruct(s, F32) for s in acc_outs]

    def body(*refs):
        ins = [r[...] for r in refs[:n_row + n_bc]]
        outs = refs[n_row + n_bc:]
        vals = fn(*ins)
        if not isinstance(vals, (tuple, list)):
            vals = (vals,)
        for o, v in zip(outs[:n_ro], vals[:n_ro]):
            o[...] = v.astype(o.dtype)
        if n_acc:
            @pl.when(pl.program_id(0) == 0)
            def _():
                for o in outs[n_ro:]:
                    o[...] = jnp.zeros_like(o)
            for o, v in zip(outs[n_ro:], vals[n_ro:]):
                o[...] += v

    outs = pl.pallas_call(
        body, name=name, grid=(S // tm,), in_specs=in_specs, out_specs=out_specs, out_shape=out_shape,
        compiler_params=_cparams(("arbitrary",) if n_acc else ("parallel",)),
    )(*[r[0] for r in rows], *bcast)
    return outs[0] if len(outs) == 1 else outs


def _sigmoid(x):
    return 1.0 / (1.0 + jnp.exp(-x))


def _silu(x):
    return x * _sigmoid(x)


def _softplus(x):
    return jnp.maximum(x, 0.0) + jnp.log(1.0 + jnp.exp(-jnp.abs(x)))


def _rms_fwd_fn(x, g):
    r = lax.rsqrt(jnp.mean(x * x, axis=-1, keepdims=True) + EPS)
    return (x * r) * g


def _rms_bwd_fn(x, dres, *rest):
    dh, g = sum(rest[:-1]), rest[-1]
    r = lax.rsqrt(jnp.mean(x * x, axis=-1, keepdims=True) + EPS)
    xh = x * r
    dxh = dh * g
    dx = dres + r * (dxh - xh * jnp.mean(dxh * xh, axis=-1, keepdims=True))
    return dx, dx, jnp.sum(dh * xh, axis=0, keepdims=True)


def _rms_fwd(name, x, gain):
    return _rowwise(name, _rms_fwd_fn, [x], [gain.reshape(1, -1)], [(x.shape[1], BF16)])


def _rms_bwd(name, x, gain, dres, dhs):
    return _rowwise(name, _rms_bwd_fn, [x, dres] + list(dhs), [gain.reshape(1, -1)],
                    [(x.shape[1], F32), (x.shape[1], BF16)], [(1, x.shape[1])])


def _relu2_epilogue(acc):
    r = jnp.maximum(acc, 0.0)
    return (r * r,)


def _relu2_bwd_epilogue(acc, a):
    return (acc * (2.0 * jnp.sqrt(a.astype(F32))),)


def _ple_fwd_fn(x, pp, zg):
    return x + pp * _sigmoid(zg)


def _ple_bwd_fn(dx, pp, zg):
    gate = _sigmoid(zg)
    return dx * gate, dx * pp * gate * (1.0 - gate)


def _loss_fn(y, t):
    err = y - t
    return err * (1.0 / D_MODEL), jnp.broadcast_to(jnp.sum(err * err, keepdims=True), (1, 128))


def _adamw(w, g, m, v):
    m = ADAM_B1 * m + (1.0 - ADAM_B1) * g
    v = ADAM_B2 * v + (1.0 - ADAM_B2) * jnp.square(g)
    m_hat = m / (1.0 - ADAM_B1 ** ADAM_STEP)
    v_hat = v / (1.0 - ADAM_B2 ** ADAM_STEP)
    delta = -ADAM_LR * (m_hat / (jnp.sqrt(v_hat) + ADAM_EPS) + ADAM_WD * w)
    return delta, m, v


def _lane_take(x, offset):
    n = x.shape[-1]
    return pltpu.roll(x, (-offset) % n, 1)


def _head_lane(shape):
    return lax.broadcasted_iota(jnp.int32, shape, 1) % A_HEAD_DIM


def _rope_partner(x):
    lane = _head_lane(x.shape)
    return jnp.where(lane < ROPE_DIM // 2, _lane_take(x, ROPE_DIM // 2),
                     jnp.where(lane < ROPE_DIM, _lane_take(x, -(ROPE_DIM // 2)), 0.0))


def _head_mean(x, bd):
    hi = x.astype(BF16)
    lo = (x - hi.astype(F32)).astype(BF16)
    b = bd.astype(BF16)
    return jnp.dot(hi, b, preferred_element_type=F32) + jnp.dot(lo, b, preferred_element_type=F32)


def _fold_heads(row):
    out = row[:, :A_HEAD_DIM]
    for h in range(1, A_HEADS):
        out = out + row[:, h * A_HEAD_DIM:(h + 1) * A_HEAD_DIM]
    return out


def _all_heads(t):
    return jnp.concatenate([t] * (A_WIDTH // t.shape[1]), axis=1)


def _qk_prep_fwd_fn(qkv, ct, st, gq, gk, bd):
    ct, st = _all_heads(ct), _all_heads(st)

    def one(t, g):
        n = t * lax.rsqrt(_head_mean(t * t, bd) + EPS) * g
        return n * ct + _rope_partner(n) * st
    q, k, v = qkv[:, :A_WIDTH], qkv[:, A_WIDTH:2 * A_WIDTH], qkv[:, 2 * A_WIDTH:]
    return one(q, gq), one(k, gk), v


def _qk_prep_bwd_fn(qkv, ct, st, dq, dk, dv, gq, gk, bd):
    ct, st = _all_heads(ct), _all_heads(st)

    def one(t, g, dy):
        r = lax.rsqrt(_head_mean(t * t, bd) + EPS)
        nh = t * r
        dn = dy * ct + _rope_partner(dy * st)
        dg = jnp.sum(dn * nh, axis=0, keepdims=True)
        dnh = dn * g
        return r * (dnh - nh * _head_mean(dnh * nh, bd)), _fold_heads(dg)
    q, k = qkv[:, :A_WIDTH], qkv[:, A_WIDTH:2 * A_WIDTH]
    dq_raw, dgq = one(q, gq, dq)
    dk_raw, dgk = one(k, gk, dk)
    return jnp.concatenate([dq_raw, dk_raw, dv], axis=1), dgq, dgk


_BATCH_DIMS = {"nn": ((2,), (1,)), "nt": ((2,), (2,)), "tn": ((1,), (1,))}


def _bdot(a, b, mode, precision=None):
    return lax.dot_general(a, b, (_BATCH_DIMS[mode], ((0,), (0,))), precision=precision,
                           preferred_element_type=F32)


def _attn_cols(h):
    return slice(h * A_HEAD_DIM, (h + 1) * A_HEAD_DIM)


def _attn_heads(ref):
    return jnp.stack([ref[:, _attn_cols(h)] for h in range(A_HEADS)])


def _band_masks():
    qi = lax.broadcasted_iota(jnp.int32, (BAND, BAND), 0)
    kj = lax.broadcasted_iota(jnp.int32, (BAND, BAND), 1)
    return kj <= qi, kj >= qi


def _attn_fwd(name, q, k, v, blocks_per_class):
    S = q.shape[0]
    nblk = S // BAND
    scale = A_HEAD_DIM ** -0.5

    def body(q_ref, kp_ref, kc_ref, vp_ref, vc_ref, o_ref, l_ref):
        i = pl.program_id(0)
        has_prev = (i % blocks_per_class) != 0
        m_cur, m_prev = _band_masks()
        m_prev = jnp.logical_and(m_prev, has_prev)
        q, kc, kp, vc, vp = (_attn_heads(r) for r in (q_ref, kc_ref, kp_ref, vc_ref, vp_ref))
        s_c = jnp.where(m_cur[None], _bdot(q, kc, "nt") * scale, -jnp.inf)
        s_p = jnp.where(m_prev[None], _bdot(q, kp, "nt") * scale, -jnp.inf)
        m = jnp.maximum(jnp.max(s_c, axis=-1, keepdims=True), jnp.max(s_p, axis=-1, keepdims=True))
        e_c, e_p = jnp.exp(s_c - m), jnp.exp(s_p - m)
        l = jnp.sum(e_c, axis=-1, keepdims=True) + jnp.sum(e_p, axis=-1, keepdims=True)
        o = _bdot((e_c / l).astype(BF16), vc, "nn") + _bdot((e_p / l).astype(BF16), vp, "nn")
        lse = m + jnp.log(l)
        for h in range(A_HEADS):
            o_ref[:, _attn_cols(h)] = o[h]
            l_ref[:, _attn_cols(h)] = jnp.broadcast_to(lse[h], (BAND, A_HEAD_DIM))

    cur = pl.BlockSpec((BAND, A_WIDTH), lambda i: (i, 0))
    prev = pl.BlockSpec((BAND, A_WIDTH), lambda i: (jnp.maximum(i - 1, 0), 0))
    return pl.pallas_call(
        body, name=name, grid=(nblk,), in_specs=[cur, prev, cur, prev, cur], out_specs=[cur, cur],
        out_shape=[jax.ShapeDtypeStruct((S, A_WIDTH), F32)] * 2,
        compiler_params=_cparams(("parallel",)),
    )(q, k, k, v, v)


def _attn_bwd(name, q, k, v, o, lse, do, dlse, blocks_per_class):
    S = q.shape[0]
    nblk = S // BAND
    scale = A_HEAD_DIM ** -0.5

    def body(q_ref, kp_ref, kc_ref, vp_ref, vc_ref, o_ref, l_ref, do_ref, dl_ref,
             dq_ref, dk_ref, dv_ref, ck, cv):
        i = pl.program_id(0)

        @pl.when(i == 0)
        def _():
            ck[...] = jnp.zeros_like(ck)
            cv[...] = jnp.zeros_like(cv)

        @pl.when(i == nblk)
        def _():
            dk_ref[...] = ck[...]
            dv_ref[...] = cv[...]

        @pl.when(i < nblk)
        def _():
            has_prev = (i % blocks_per_class) != 0
            m_cur, m_prev = _band_masks()
            m_prev = jnp.logical_and(m_prev, has_prev)
            q, kc, kp, vc, vp = (_attn_heads(r) for r in (q_ref, kc_ref, kp_ref, vc_ref, vp_ref))
            do, o, dl = _attn_heads(do_ref), _attn_heads(o_ref), _attn_heads(dl_ref)
            lse = jnp.max(_attn_heads(l_ref), axis=-1, keepdims=True)
            p_c = jnp.where(m_cur[None], jnp.exp(_bdot(q, kc, "nt") * scale - lse), 0.0)
            p_p = jnp.where(m_prev[None], jnp.exp(_bdot(q, kp, "nt") * scale - lse), 0.0)
            corr = jnp.sum(dl, axis=-1, keepdims=True) - jnp.sum(do * o, axis=-1, keepdims=True)
            dob = do.astype(BF16)
            ds_c = (p_c * (_bdot(dob, vc, "nt") + corr)).astype(BF16)
            ds_p = (p_p * (_bdot(dob, vp, "nt") + corr)).astype(BF16)
            dq = (_bdot(ds_c, kc, "nn") + _bdot(ds_p, kp, "nn")) * scale
            dk_p, dk_c = _bdot(ds_p, q, "tn") * scale, _bdot(ds_c, q, "tn") * scale
            dv_p, dv_c = _bdot(p_p.astype(BF16), dob, "tn"), _bdot(p_c.astype(BF16), dob, "tn")
            for h in range(A_HEADS):
                sl = _attn_cols(h)
                dq_ref[:, sl] = dq[h]
                dk_ref[:, sl] = ck[:, sl] + dk_p[h]
                dv_ref[:, sl] = cv[:, sl] + dv_p[h]
                ck[:, sl] = dk_c[h]
                cv[:, sl] = dv_c[h]

    last = nblk - 1
    cur = pl.BlockSpec((BAND, A_WIDTH), lambda i: (jnp.minimum(i, last), 0))
    prev = pl.BlockSpec((BAND, A_WIDTH), lambda i: (jnp.minimum(jnp.maximum(i - 1, 0), last), 0))
    return pl.pallas_call(
        body, name=name, grid=(nblk + 1,),
        in_specs=[cur, prev, cur, prev, cur, cur, cur, cur, cur], out_specs=[cur, prev, prev],
        out_shape=[jax.ShapeDtypeStruct((S, A_WIDTH), F32)] * 3,
        scratch_shapes=[pltpu.VMEM((BAND, A_WIDTH), F32)] * 2,
        compiler_params=_cparams(("arbitrary",)),
    )(q, k, k, v, v, o, lse, do, dlse)


def _merge_fwd_fn(o0, o1, o2, l0, l1, l2):
    m = jnp.maximum(jnp.maximum(l0, l1), l2)
    e0, e1, e2 = jnp.exp(l0 - m), jnp.exp(l1 - m), jnp.exp(l2 - m)
    return (e0 * o0 + e1 * o1 + e2 * o2) / (e0 + e1 + e2)


def _merge_bwd_fn(o0, o1, o2, l0, l1, l2, dom):
    m = jnp.maximum(jnp.maximum(l0, l1), l2)
    e0, e1, e2 = jnp.exp(l0 - m), jnp.exp(l1 - m), jnp.exp(l2 - m)
    den = e0 + e1 + e2
    w0, w1, w2 = e0 / den, e1 / den, e2 / den
    dw0, dw1, dw2 = dom * o0, dom * o1, dom * o2
    mean = w0 * dw0 + w1 * dw1 + w2 * dw2
    return w0 * dom, w1 * dom, w2 * dom, w0 * (dw0 - mean), w1 * (dw1 - mean), w2 * (dw2 - mean)


def _to_classes(t, d):
    if d == 1:
        return t
    S, C = t.shape
    return t.reshape(S // d, d, C).transpose(1, 0, 2).reshape(S, C)


def _from_classes(t, d):
    if d == 1:
        return t
    S, C = t.shape
    return t.reshape(d, S // d, C).transpose(1, 0, 2).reshape(S, C)


def _rope_lane_tables(positions):
    inv_freq = ROPE_THETA ** (-jnp.arange(0, ROPE_DIM, 2, dtype=F32) / ROPE_DIM)
    ang = positions.astype(F32)[:, None] * inv_freq
    cos, sin = jnp.cos(ang), jnp.sin(ang)
    S = positions.shape[0]
    rest = A_HEAD_DIM - ROPE_DIM
    ct = jnp.concatenate([cos, cos, jnp.ones((S, rest), F32)], axis=1)
    st = jnp.concatenate([-sin, sin, jnp.zeros((S, rest), F32)], axis=1)
    return jnp.tile(ct, (1, 2)), jnp.tile(st, (1, 2))


def _head_mean_matrix():
    r = jnp.arange(A_WIDTH) // A_HEAD_DIM
    return (r[:, None] == r[None, :]).astype(F32) * (1.0 / A_HEAD_DIM)


def _conv_fwd(name, proj, w):
    S = proj.shape[0]
    tm, tc = min(512, S), 1024
    per8 = tm // 8
    off = DN_QKV0 // tc

    def body(x_ref, halo_ref, w_ref, o_ref, xs):
        i = pl.program_id(0)
        xs[0:8, :] = jnp.where(i > 0, halo_ref[...], 0.0)
        xs[8:, :] = x_ref[...]
        acc = w_ref[0:1, :] * xs[pl.ds(8 - 3, tm), :]
        for j in range(1, CONV_WIDTH):
            acc = acc + w_ref[j:j + 1, :] * xs[pl.ds(8 - 3 + j, tm), :]
        o_ref[...] = acc

    return pl.pallas_call(
        body, name=name, grid=(S // tm, DN_QKV // tc),
        in_specs=[pl.BlockSpec((tm, tc), lambda i, j: (i, j + off)),
                  pl.BlockSpec((8, tc), lambda i, j: (jnp.maximum(i * per8 - 1, 0), j + off)),
                  pl.BlockSpec((CONV_WIDTH, tc), lambda i, j: (0, j))],
        out_specs=pl.BlockSpec((tm, tc), lambda i, j: (i, j)),
        out_shape=jax.ShapeDtypeStruct((S, DN_QKV), F32),
        scratch_shapes=[pltpu.VMEM((tm + 8, tc), F32)],
        compiler_params=_cparams(("parallel", "parallel")),
    )(proj, proj, w)


def _conv_bwd(name, proj, dpre, w):
    S = proj.shape[0]
    tm, tc = min(512, S), 1024
    per8 = tm // 8
    off = DN_QKV0 // tc
    last8 = S // 8 - 1
    nrow = S // tm

    def body(x_ref, xh_ref, d_ref, dh_ref, w_ref, dx_ref, dw_ref, xs, ds):
        i = pl.program_id(1)
        xs[0:8, :] = jnp.where(i > 0, xh_ref[...], 0.0)
        xs[8:, :] = x_ref[...]
        ds[0:tm, :] = d_ref[...]
        ds[tm:, :] = jnp.where(i < nrow - 1, dh_ref[...], 0.0)
        d = d_ref[...]
        acc = w_ref[0:1, :] * ds[pl.ds(3, tm), :]
        for j in range(1, CONV_WIDTH):
            acc = acc + w_ref[j:j + 1, :] * ds[pl.ds(3 - j, tm), :]
        dx_ref[...] = acc.astype(dx_ref.dtype)

        @pl.when(i == 0)
        def _():
            dw_ref[...] = jnp.zeros_like(dw_ref)

        for j in range(CONV_WIDTH):
            dw_ref[j:j + 1, :] += jnp.sum(d * xs[pl.ds(8 - 3 + j, tm), :], axis=0, keepdims=True)

    return pl.pallas_call(
        body, name=name, grid=(DN_QKV // tc, nrow),
        in_specs=[pl.BlockSpec((tm, tc), lambda j, i: (i, j + off)),
                  pl.BlockSpec((8, tc), lambda j, i: (jnp.maximum(i * per8 - 1, 0), j + off)),
                  pl.BlockSpec((tm, tc), lambda j, i: (i, j)),
                  pl.BlockSpec((8, tc), lambda j, i: (jnp.minimum((i + 1) * per8, last8), j)),
                  pl.BlockSpec((CONV_WIDTH, tc), lambda j, i: (0, j))],
        out_specs=[pl.BlockSpec((tm, tc), lambda j, i: (i, j)),
                   pl.BlockSpec((CONV_WIDTH, tc), lambda j, i: (0, j))],
        out_shape=[jax.ShapeDtypeStruct((S, DN_QKV), BF16), jax.ShapeDtypeStruct((CONV_WIDTH, DN_QKV), F32)],
        scratch_shapes=[pltpu.VMEM((tm + 8, tc), F32)] * 2,
        compiler_params=_cparams(("parallel", "arbitrary")),
    )(proj, proj, dpre, dpre, w)


def _gate_lane(shape):
    return lax.broadcasted_iota(jnp.int32, shape, 1)


GATES_ROWS = 256


def _chunk_cumsum_matrix():
    r = jnp.arange(GATES_ROWS)
    return ((r[:, None] >= r[None, :]) & (r[:, None] // CHUNK == r[None, :] // CHUNK)).astype(F32)


def _gates_fwd_fn(ab, alog, dt, cum):
    g = -jnp.exp(alog) * _softplus(ab + dt)
    gc = jnp.dot(cum, g, precision=HIGHEST, preferred_element_type=F32)
    return jnp.where(_gate_lane(ab.shape) < DN_HEADS, gc, _sigmoid(ab))


def _gates_bwd_fn(ab, dgb, alog, dt, cum):
    lane = _gate_lane(ab.shape)
    is_g = lane < DN_HEADS
    neg_a = -jnp.exp(alog)
    sp = _softplus(ab + dt)
    dsp = _sigmoid(ab + dt)
    beta = _sigmoid(ab)
    dgc = jnp.where(is_g, dgb, 0.0)
    dg = lax.dot_general(cum, dgc, (_DIMS["tn"], ((), ())), precision=HIGHEST, preferred_element_type=F32)
    dab = jnp.where(is_g, dg * neg_a * dsp, jnp.where(lane < 2 * DN_HEADS, dgb * beta * (1.0 - beta), 0.0))
    d_alog = jnp.sum(dg * neg_a * sp, axis=0, keepdims=True)
    d_dt = jnp.sum(dg * neg_a * dsp, axis=0, keepdims=True)
    return dab, d_alog, d_dt


def _chunk_math(precision):
    def dg(a, b, mode, prec=precision):
        return _bdot(a, b, mode, prec)

    @jax.custom_vjp
    def nn(a, b):
        return dg(a, b, "nn")

    @jax.custom_vjp
    def nt(a, b):
        return dg(a, b, "nt")

    @jax.custom_vjp
    def tn(a, b):
        return dg(a, b, "tn")

    nn.defvjp(lambda a, b: (nn(a, b), (a, b)), lambda r, g: (nt(g, r[1]), tn(r[0], g)))
    nt.defvjp(lambda a, b: (nt(a, b), (a, b)), lambda r, g: (nn(g, r[1]), tn(g, r[0])))
    tn.defvjp(lambda a, b: (tn(a, b), (a, b)), lambda r, g: (nt(r[1], g), nn(r[0], g)))

    def split(x):
        hi = x.astype(BF16)
        return hi, (x - hi.astype(F32)).astype(BF16)

    def fine(a, b, mode):
        ah, al = split(a)
        bh, bl = split(b)
        return dg(ah, bh, mode, None) + (dg(ah, bl, mode, None) + dg(al, bh, mode, None))

    def unit_lower_inverse(a):
        row = lax.broadcasted_iota(jnp.int32, a.shape, 1)
        col = lax.broadcasted_iota(jnp.int32, a.shape, 2)
        x = -a
        p = jnp.where(row == col, 1.0, 0.0) + x
        for _ in range(int(math.log2(CHUNK)) - 1):
            x = fine(x, x, "nn")
            p = p + fine(p, x, "nn")
        return p

    @jax.custom_vjp
    def solve2(a, ti, r1, r2):
        return fine(ti, r1, "nn"), fine(ti, r2, "nn")

    def solve2_fwd(a, ti, r1, r2):
        s1, s2 = fine(ti, r1, "nn"), fine(ti, r2, "nn")
        return (s1, s2), (ti, s1, s2)

    def solve2_bwd(res, g):
        ti, s1, s2 = res
        d1, d2 = fine(ti, g[0], "tn"), fine(ti, g[1], "tn")
        return -(fine(d1, s1, "nt") + fine(d2, s2, "nt")), jnp.zeros_like(ti), d1, d2

    solve2.defvjp(solve2_fwd, solve2_bwd)

    def chunk_fn(pq, pk, pv, z, g_col, b_col, g_row, ogain, s_in, inverse=None):
        nb = pq.shape[0]
        sq = (nb, CHUNK, CHUNK)
        row = lax.broadcasted_iota(jnp.int32, sq, 1)
        col = lax.broadcasted_iota(jnp.int32, sq, 2)
        lower, strict = row >= col, row > col
        q, k, v = _silu(pq), _silu(pk), _silu(pv)
        q = q * lax.rsqrt(jnp.sum(q * q, axis=-1, keepdims=True) + EPS) * (DN_HEAD_DIM ** -0.5)
        k = k * lax.rsqrt(jnp.sum(k * k, axis=-1, keepdims=True) + EPS)
        gc_wide = jnp.broadcast_to(g_col, pq.shape)
        gc_i = jnp.broadcast_to(g_col, sq)
        gc_j = jnp.broadcast_to(g_row, sq)
        is_last = lax.broadcasted_iota(jnp.int32, pq.shape, 1) == CHUNK - 1
        g_last = jnp.sum(jnp.where(is_last, gc_wide, 0.0), axis=1, keepdims=True)
        decay = jnp.exp(jnp.where(lower, gc_i - gc_j, -jnp.inf))
        kb = k * b_col
        a_mat = jnp.where(strict, nt(kb, k) * decay, 0.0)
        eg = jnp.exp(gc_wide)
        ti = unit_lower_inverse(a_mat) if inverse is None else inverse
        u, w = solve2(a_mat, ti, v * b_col, kb * eg)
        attn = nt(q, k) * decay
        q_dec = q * eg
        k_dec = k * jnp.exp(g_last - gc_wide)
        c_dec = jnp.exp(g_last)
        v_new = u - nn(w, s_in)
        o = nn(q_dec, s_in) + nn(attn, v_new)
        s_out = s_in * c_dec + tn(k_dec, v_new)
        y = o * lax.rsqrt(jnp.mean(o * o, axis=-1, keepdims=True) + EPS) * ogain * _silu(z)
        return (y, s_out, ti) if inverse is None else (y, s_out)

    return chunk_fn


DN_PRECISION = None


def _chunk_specs(n_of):
    groups = DN_HEADS // DN_HB
    wide = DN_HB * DN_HEAD_DIM
    hd = pl.BlockSpec((CHUNK, wide), lambda h, n: (n_of(n), h))
    specs = dict(
        pq=hd,
        pk=pl.BlockSpec((CHUNK, wide), lambda h, n: (n_of(n), groups + h)),
        pv=pl.BlockSpec((CHUNK, wide), lambda h, n: (n_of(n), 2 * groups + h)),
        z=hd,
        col=pl.BlockSpec((DN_HB, CHUNK, 1), lambda h, n: (h, n_of(n), 0)),
        row=pl.BlockSpec((DN_HB, None, 1, CHUNK), lambda h, n: (h, n_of(n), 0, 0)),
        gain=pl.BlockSpec((1, DN_HEAD_DIM), lambda h, n: (0, 0)),
        state=pl.BlockSpec((DN_HB, None, DN_HEAD_DIM, DN_HEAD_DIM), lambda h, n: (h, n_of(n), 0, 0)),
        inverse=pl.BlockSpec((DN_HB, None, CHUNK, CHUNK), lambda h, n: (h, n_of(n), 0, 0)),
        qkv=pl.BlockSpec((CHUNK, DN_QKV), lambda h, n: (n_of(n), 0)),
        head=hd,
    )
    return specs


def _head_cols(j):
    return slice(j * DN_HEAD_DIM, (j + 1) * DN_HEAD_DIM)


def _split_heads(ref):
    return jnp.stack([ref[:, _head_cols(j)] for j in range(DN_HB)])


def _chunk_fwd(name, pre, proj, g_col, b_col, g_row, ogain):
    S = pre.shape[0]
    N = S // CHUNK
    chunk_fn = _chunk_math(DN_PRECISION)
    sp = _chunk_specs(lambda n: n)

    def body(pq, pk, pv, z, gc, bc, gr, og, y_ref, sin_ref, inv_ref, st):
        @pl.when(pl.program_id(1) == 0)
        def _():
            st[...] = jnp.zeros_like(st)

        s_in = st[...]
        sin_ref[...] = s_in
        y, s_out, inverse = chunk_fn(_split_heads(pq), _split_heads(pk), _split_heads(pv), _split_heads(z),
                                     gc[...], bc[...], gr[...], og[...], s_in)
        for j in range(DN_HB):
            y_ref[:, _head_cols(j)] = y[j].astype(y_ref.dtype)
        inv_ref[...] = inverse
        st[...] = s_out

    return pl.pallas_call(
        body, name=name, grid=(DN_HEADS // DN_HB, N),
        in_specs=[sp["pq"], sp["pk"], sp["pv"], sp["z"], sp["col"], sp["col"], sp["row"], sp["gain"]],
        out_specs=[sp["head"], sp["state"], sp["inverse"]],
        out_shape=[jax.ShapeDtypeStruct((S, DN_WIDTH), BF16),
                   jax.ShapeDtypeStruct((DN_HEADS, N, DN_HEAD_DIM, DN_HEAD_DIM), F32),
                   jax.ShapeDtypeStruct((DN_HEADS, N, CHUNK, CHUNK), F32)],
        scratch_shapes=[pltpu.VMEM((DN_HB, DN_HEAD_DIM, DN_HEAD_DIM), F32)],
        compiler_params=_cparams(("parallel", "arbitrary")),
    )(pre, pre, pre, proj, g_col, b_col, g_row, ogain)


def _chunk_bwd(name, pre, proj, g_col, b_col, g_row, ogain, s_in_all, inverse_all, dy):
    assert DN_HB == DN_HEADS
    S = pre.shape[0]
    N = S // CHUNK
    chunk_fn = _chunk_math(DN_PRECISION)
    sp = _chunk_specs(lambda n: N - 1 - n)

    def body(pq, pk, pv, z, gc, bc, gr, og, sin_ref, inv_ref, dy_ref,
             dpre_ref, dz_ref, dgc_ref, dbc_ref, dgr_ref, dog_ref, ds):
        @pl.when(pl.program_id(1) == 0)
        def _():
            ds[...] = jnp.zeros_like(ds)
            dog_ref[...] = jnp.zeros_like(dog_ref)

        inverse = inv_ref[...]
        prim = (_split_heads(pq), _split_heads(pk), _split_heads(pv), _split_heads(z),
                gc[...], bc[...], gr[...], og[...], sin_ref[...])
        _, vjp = jax.vjp(lambda *a: chunk_fn(*a, inverse=inverse), *prim)
        gq, gk, gv, gz, ggc, gbc, ggr, gog, gs = vjp((_split_heads(dy_ref), ds[...]))
        for j in range(DN_HB):
            for part, g in enumerate((gq, gk, gv)):
                dpre_ref[:, pl.ds(part * DN_WIDTH + j * DN_HEAD_DIM, DN_HEAD_DIM)] = g[j]
            dz_ref[:, _head_cols(j)] = gz[j]
        dgc_ref[...] = ggc
        dbc_ref[...] = gbc
        dgr_ref[...] = ggr
        dog_ref[...] += gog
        ds[...] = gs

    hd = sp["head"]
    return pl.pallas_call(
        body, name=name, grid=(1, N),
        in_specs=[sp["pq"], sp["pk"], sp["pv"], sp["z"], sp["col"], sp["col"], sp["row"], sp["gain"],
                  sp["state"], sp["inverse"], hd],
        out_specs=[sp["qkv"], hd, sp["col"], sp["col"], sp["row"], sp["gain"]],
        out_shape=[jax.ShapeDtypeStruct((S, DN_QKV), F32), jax.ShapeDtypeStruct((S, DN_WIDTH), F32)]
        + [jax.ShapeDtypeStruct((DN_HEADS, S, 1), F32)] * 2
        + [jax.ShapeDtypeStruct((DN_HEADS, N, 1, CHUNK), F32), jax.ShapeDtypeStruct((1, DN_HEAD_DIM), F32)],
        scratch_shapes=[pltpu.VMEM((DN_HB, DN_HEAD_DIM, DN_HEAD_DIM), F32)],
        compiler_params=_cparams(("arbitrary", "arbitrary")),
    )(pre, pre, pre, proj, g_col, b_col, g_row, ogain, s_in_all, inverse_all, dy)


def _mlp_ple_fwd(tag, x_in, p_l, norm_mlp, w_up, w_down, norm_ple, w_ple, w_gate):
    h = _rms_fwd(f"{tag}_mlp_norm", x_in, norm_mlp)
    a = _mm(f"{tag}_up", h, w_up, "nn", out_dtypes=(BF16,), epilogue=_relu2_epilogue)
    x_mid = _mm(f"{tag}_down", a, w_down, "nn", extras=(x_in,), epilogue=lambda acc, r: (acc + r,))
    hg = _rms_fwd(f"{tag}_ple_norm", x_mid, norm_ple)
    zg = _mm(f"{tag}_gate", hg, w_gate, "nn")
    pp = _mm(f"{tag}_ple", p_l, w_ple, "nn")
    x_out = _rowwise(f"{tag}_ple_out", _ple_fwd_fn, [x_mid, pp, zg], [], [(D_MODEL, F32)])
    return x_out, dict(x_in=x_in, h=h, a=a, x_mid=x_mid, hg=hg, zg=zg, pp=pp)


def _mlp_ple_bwd(tag, dx, sv, p_l, norm_mlp, w_up, w_down, norm_ple, w_ple, w_gate):
    dpp, dzg = _rowwise(f"{tag}_ple_bwd", _ple_bwd_fn, [dx, sv["pp"], sv["zg"]], [],
                        [(D_MODEL, BF16), (D_MODEL, BF16)])
    d_w_ple = _mm(f"{tag}_d_w_ple", p_l, dpp, "tn", out_dtypes=(BF16,))
    d_w_gate = _mm(f"{tag}_d_w_gate", sv["hg"], dzg, "tn", out_dtypes=(BF16,))
    dhg = _mm(f"{tag}_d_hg", dzg, w_gate, "nt")
    dx_mid, dx_mid_b, d_norm_ple = _rms_bwd(f"{tag}_ple_norm_bwd", sv["x_mid"], norm_ple, dx, [dhg])
    du = _mm(f"{tag}_d_u", dx_mid_b, w_down, "nt", out_dtypes=(BF16,), extras=(sv["a"],),
             epilogue=_relu2_bwd_epilogue)
    d_w_down = _mm(f"{tag}_d_w_down", sv["a"], dx_mid_b, "tn", out_dtypes=(BF16,))
    d_w_up = _mm(f"{tag}_d_w_up", sv["h"], du, "tn", out_dtypes=(BF16,))
    dh = _mm(f"{tag}_d_h", du, w_up, "nt")
    dx_in, dx_in_b, d_norm_mlp = _rms_bwd(f"{tag}_mlp_norm_bwd", sv["x_in"], norm_mlp, dx_mid, [dh])
    return dx_in, dx_in_b, dict(mlp_norm=d_norm_mlp, w_up=d_w_up, w_down=d_w_down, ple_norm=d_norm_ple,
                                w_ple=d_w_ple, w_ple_gate=d_w_gate)


def _local_step(x, p, positions, target, small, big):
    S = x.shape[0]
    ct, st = _rope_lane_tables(positions)
    bd = _head_mean_matrix()

    h0 = _rms_fwd("l0_mix_norm", x, small["mix_norm"][0])
    attn = []
    for g, (window, d) in enumerate(SWA_GROUPS):
        assert window // d == BAND and (S // d) % BAND == 0
        h0g = _to_classes(h0, d)
        ctg, stg = _to_classes(ct, d), _to_classes(st, d)
        w_g = big["attn_w_qkv"][:, g * 3 * A_WIDTH:(g + 1) * 3 * A_WIDTH]
        gq = jnp.tile(small["attn_q_gain"][0, g], A_HEADS).reshape(1, A_WIDTH)
        gk = jnp.tile(small["attn_k_gain"][0, g], A_HEADS).reshape(1, A_WIDTH)
        qkv = _mm(f"l0_qkv{g}", h0g, w_g, "nn")
        q, k, v = _rowwise(f"l0_qk_prep{g}", _qk_prep_fwd_fn, [qkv, ctg, stg], [gq, gk, bd], [(A_WIDTH, BF16)] * 3)
        o, lse = _attn_fwd(f"l0_attn{g}", q, k, v, (S // d) // BAND)
        attn.append(dict(d=d, h0g=h0g, ct=ctg, st=stg, w=w_g, gq=gq, gk=gk, qkv=qkv, q=q, k=k, v=v, o=o, lse=lse,
                         o_tok=_from_classes(o, d), lse_tok=_from_classes(lse, d)))
    om = _rowwise("l0_merge", _merge_fwd_fn, [a["o_tok"] for a in attn] + [a["lse_tok"] for a in attn], [],
                  [(A_WIDTH, BF16)])
    x1 = _mm("l0_attn_out", om, big["attn_w_o"], "nn", extras=(x,), epilogue=lambda acc, r: (acc + r,))
    x3, sv0 = _mlp_ple_fwd("l0", x1, p[0], small["mlp_norm"][0], big["w_up"][0], big["w_down"][0],
                           small["ple_norm"][0], big["w_ple"][0], big["w_ple_gate"][0])

    N = S // CHUNK
    h3 = _rms_fwd("l1_mix_norm", x3, small["mix_norm"][1])
    proj = _mm("l1_in", h3, big["dn_w_in"], "nn")
    pre = _conv_fwd("l1_conv", proj, small["dn_conv"])
    ab = proj[:, DN_AB0:DN_AB0 + DN_AB_PAD]
    lane_pad = DN_AB_PAD - DN_HEADS
    alog_row = jnp.pad(small["dn_a_log"][0], (0, lane_pad)).reshape(1, DN_AB_PAD)
    dt_row = jnp.pad(small["dn_dt_bias"][0], (0, lane_pad)).reshape(1, DN_AB_PAD)
    cum = _chunk_cumsum_matrix()
    gb = _rowwise("l1_gates", _gates_fwd_fn, [ab], [alog_row, dt_row, cum], [(DN_AB_PAD, F32)], tm=GATES_ROWS)
    g_t, b_t = gb[:, :DN_HEADS].T, gb[:, DN_HEADS:2 * DN_HEADS].T
    g_col, b_col = g_t.reshape(DN_HEADS, S, 1), b_t.reshape(DN_HEADS, S, 1)
    g_row = g_t.reshape(DN_HEADS, N, 1, CHUNK)
    ogain = small["dn_o_gain"][0].reshape(1, DN_HEAD_DIM)
    y, s_in_all, inverse_all = _chunk_fwd("l1_delta", pre, proj, g_col, b_col, g_row, ogain)
    x4 = _mm("l1_dn_out", y, big["dn_w_o"], "nn", extras=(x3,), epilogue=lambda acc, r: (acc + r,))
    x6, sv1 = _mlp_ple_fwd("l1", x4, p[1], small["mlp_norm"][1], big["w_up"][1], big["w_down"][1],
                           small["ple_norm"][1], big["w_ple"][1], big["w_ple_gate"][1])

    dy, sq = _rowwise("loss", _loss_fn, [x6, target], [], [(D_MODEL, F32)], [(1, 128)])

    dx4, dx4_b, gl1 = _mlp_ple_bwd("l1", dy, sv1, p[1], small["mlp_norm"][1], big["w_up"][1], big["w_down"][1],
                            small["ple_norm"][1], big["w_ple"][1], big["w_ple_gate"][1])
    d_y = _mm("l1_d_y", dx4_b, big["dn_w_o"], "nt")
    d_dn_w_o = _mm("l1_d_w_o", y, dx4_b, "tn", out_dtypes=(BF16,))
    dpre, dz, dg_col, db_col, dg_row, d_ogain = _chunk_bwd(
        "l1_delta_bwd", pre, proj, g_col, b_col, g_row, ogain, s_in_all, inverse_all, d_y)
    dconv_in, d_conv_w = _conv_bwd("l1_conv_bwd", proj, dpre, small["dn_conv"])
    dg_t = dg_col.reshape(DN_HEADS, S) + dg_row.reshape(DN_HEADS, S)
    dgb = jnp.pad(jnp.concatenate([dg_t, db_col.reshape(DN_HEADS, S)], axis=0).T,
                  ((0, 0), (0, DN_AB_PAD - 2 * DN_HEADS)))
    dab, d_alog, d_dt = _rowwise("l1_gates_bwd", _gates_bwd_fn, [ab, dgb], [alog_row, dt_row, cum],
                                 [(DN_AB_PAD, F32)], [(1, DN_AB_PAD), (1, DN_AB_PAD)], tm=GATES_ROWS)
    dproj = jnp.concatenate([dz.astype(BF16), dconv_in, dab.astype(BF16)], axis=1)
    d_dn_w_in = _mm("l1_d_w_in", h3, dproj, "tn", out_dtypes=(BF16,))
    dh3 = _mm("l1_d_h", dproj, big["dn_w_in"], "nt")
    dx3, _, d_mix1 = _rms_bwd("l1_mix_norm_bwd", x3, small["mix_norm"][1], dx4, [dh3])

    dx1, dx1_b, gl0 = _mlp_ple_bwd("l0", dx3, sv0, p[0], small["mlp_norm"][0], big["w_up"][0], big["w_down"][0],
                            small["ple_norm"][0], big["w_ple"][0], big["w_ple_gate"][0])
    dom = _mm("l0_d_om", dx1_b, big["attn_w_o"], "nt")
    d_attn_w_o = _mm("l0_d_w_o", om, dx1_b, "tn", out_dtypes=(BF16,))
    merged = _rowwise("l0_merge_bwd", _merge_bwd_fn,
                      [a["o_tok"] for a in attn] + [a["lse_tok"] for a in attn] + [dom], [], [(A_WIDTH, F32)] * 6)
    dh0, d_w_qkv, d_gq, d_gk = [], [], [], []
    for g, a in enumerate(attn):
        do_g, dl_g = _to_classes(merged[g], a["d"]), _to_classes(merged[3 + g], a["d"])
        dqn, dkn, dvn = _attn_bwd(f"l0_attn_bwd{g}", a["q"], a["k"], a["v"], a["o"], a["lse"], do_g, dl_g,
                                  (S // a["d"]) // BAND)
        dqkv, dgq, dgk = _rowwise(f"l0_qk_prep_bwd{g}", _qk_prep_bwd_fn, [a["qkv"], a["ct"], a["st"], dqn, dkn, dvn],
                                  [a["gq"], a["gk"], bd], [(3 * A_WIDTH, BF16)], [(1, A_HEAD_DIM)] * 2)
        d_w_qkv.append(_mm(f"l0_d_w_qkv{g}", a["h0g"], dqkv, "tn", out_dtypes=(BF16,)))
        dh0.append(_from_classes(_mm(f"l0_d_h{g}", dqkv, a["w"], "nt"), a["d"]))
        d_gq.append(dgq)
        d_gk.append(dgk)
    grad_x, _, d_mix0 = _rms_bwd("l0_mix_norm_bwd", x, small["mix_norm"][0], dx1, dh0)

    grads = dict(
        mix_norm=jnp.concatenate([d_mix0, d_mix1], axis=0),
        attn_w_qkv=jnp.concatenate(d_w_qkv, axis=1),
        attn_q_gain=jnp.concatenate(d_gq, axis=0)[None],
        attn_k_gain=jnp.concatenate(d_gk, axis=0)[None],
        attn_w_o=d_attn_w_o,
        dn_w_in=jnp.concatenate([d_dn_w_in[:, DN_QKV0:DN_AB0 + 2 * DN_HEADS], d_dn_w_in[:, :DN_WIDTH]], axis=1),
        dn_conv=d_conv_w,
        dn_a_log=d_alog[:, :DN_HEADS],
        dn_dt_bias=d_dt[:, :DN_HEADS],
        dn_o_gain=d_ogain,
        dn_w_o=d_dn_w_o,
        mlp_norm=jnp.concatenate([gl0["mlp_norm"], gl1["mlp_norm"]], axis=0),
        w_up=jnp.stack([gl0["w_up"], gl1["w_up"]]),
        w_down=jnp.stack([gl0["w_down"], gl1["w_down"]]),
        ple_norm=jnp.concatenate([gl0["ple_norm"], gl1["ple_norm"]], axis=0),
        w_ple=jnp.stack([gl0["w_ple"], gl1["w_ple"]]),
        w_ple_gate=jnp.stack([gl0["w_ple_gate"], gl1["w_ple_gate"]]),
    )
    return sq, grad_x, grads


def _chip_peer(x, y, c, t):
    return (jnp.bitwise_xor(x, t >> 1), jnp.bitwise_xor(y, t & 1), c)


def _place():
    x, y, c = lax.axis_index("x"), lax.axis_index("y"), lax.axis_index("c")
    return x, y, c, 2 * x + y, (x, y, 1 - c)


def _remote(src, dst, send_sem, recv_sem, to):
    return pltpu.make_async_remote_copy(src_ref=src, dst_ref=dst, send_sem=send_sem, recv_sem=recv_sem,
                                        device_id=to, device_id_type=MESH)


def _hbm_call(name, body, ins, out_shape, scratch_shapes):
    any_spec = pl.BlockSpec(memory_space=pl.ANY)
    return pl.pallas_call(body, name=name, out_shape=out_shape, in_specs=[any_spec] * len(ins),
                          out_specs=[any_spec] * len(out_shape), scratch_shapes=scratch_shapes)(*ins)


def _half(n0, which):
    return pl.ds(which * (n0 // 2), n0 // 2)


def _gather_shards(name, shards):
    T = len(shards)

    def body(*refs):
        ins, outs = refs[:T], refs[T:2 * T]
        send, recv = refs[2 * T:]
        x, y, c, q, sibling = _place()
        sends = []
        for i in range(T):
            mine = _half(ins[i].shape[0], c)
            for t in range(1, N_CHIPS):
                cp = _remote(ins[i].at[mine], outs[i].at[q, mine], send.at[i, t - 1], recv.at[i, t - 1],
                             _chip_peer(x, y, c, t))
                cp.start()
                sends.append(cp)
        for i in range(T):
            mine = _half(ins[i].shape[0], c)
            for t in range(1, N_CHIPS):
                landed = outs[i].at[jnp.bitwise_xor(q, t), mine]
                _remote(landed, landed, send.at[i, t - 1], recv.at[i, t - 1], _chip_peer(x, y, c, t)).wait_recv()
                cp = _remote(landed, landed, send.at[i, 2 + t], recv.at[i, 2 + t], sibling)
                cp.start()
                sends.append(cp)
        for i in range(T):
            theirs = _half(ins[i].shape[0], 1 - c)
            for t in range(1, N_CHIPS):
                passed = outs[i].at[jnp.bitwise_xor(q, t), theirs]
                _remote(passed, passed, send.at[i, 2 + t], recv.at[i, 2 + t], sibling).wait_recv()
        for cp in sends:
            cp.wait_send()

    n_rel = 2 * (N_CHIPS - 1)
    return _hbm_call(name, body, shards,
                     [jax.ShapeDtypeStruct((N_CHIPS,) + s.shape, s.dtype) for s in shards],
                     [pltpu.SemaphoreType.DMA((T, n_rel)), pltpu.SemaphoreType.DMA((T, n_rel))])


def _other_half_from_sibling(name, stacks):
    T = len(stacks)

    def body(*refs):
        ins, outs = refs[:T], refs[T:2 * T]
        send, recv = refs[2 * T:]
        x, y, c, q, sibling = _place()
        copies = []
        for i in range(T):
            rc = _remote(ins[i].at[:, _half(ins[i].shape[1], 1 - c)], outs[i], send.at[i], recv.at[i], sibling)
            rc.start()
            copies.append(rc)
        for cp in copies:
            cp.wait()

    return _hbm_call(name, body, stacks,
                     [jax.ShapeDtypeStruct((s.shape[0], s.shape[1] // 2) + s.shape[2:], s.dtype) for s in stacks],
                     [pltpu.SemaphoreType.DMA((T,)), pltpu.SemaphoreType.DMA((T,))])


def _scatter_to_chips(name, stacks):
    T = len(stacks)

    def body(*refs):
        ins, outs = refs[:T], refs[T:2 * T]
        send, recv = refs[2 * T:]
        x, y, c, q, sibling = _place()
        copies = []
        for i in range(T):
            for t in range(1, N_CHIPS):
                rc = _remote(ins[i].at[jnp.bitwise_xor(q, t)], outs[i].at[t - 1], send.at[i, t - 1],
                             recv.at[i, t - 1], _chip_peer(x, y, c, t))
                rc.start()
                copies.append(rc)
        for cp in copies:
            cp.wait()

    return _hbm_call(name, body, stacks,
                     [jax.ShapeDtypeStruct((N_CHIPS - 1,) + s.shape[1:], s.dtype) for s in stacks],
                     [pltpu.SemaphoreType.DMA((T, N_CHIPS - 1)), pltpu.SemaphoreType.DMA((T, N_CHIPS - 1))])


def _swap_with_sibling(name, arrays):
    T = len(arrays)

    def body(*refs):
        ins, outs = refs[:T], refs[T:2 * T]
        send, recv = refs[2 * T:]
        x, y, c, q, sibling = _place()
        copies = []
        for i in range(T):
            rc = _remote(ins[i], outs[i], send.at[i], recv.at[i], sibling)
            rc.start()
            copies.append(rc)
        for cp in copies:
            cp.wait()

    return _hbm_call(name, body, arrays, [jax.ShapeDtypeStruct(a.shape, a.dtype) for a in arrays],
                     [pltpu.SemaphoreType.DMA((T,)), pltpu.SemaphoreType.DMA((T,))])


def _gather_from_all(name, block):
    R, C = block.shape

    def body(src, out, send_sems, recv_sems):
        x, y, c = lax.axis_index("x"), lax.axis_index("y"), lax.axis_index("c")
        me = 4 * x + 2 * y + c
        out[me] = src[...]
        copies = []
        for r in range(1, N_DEV):
            peer = (jnp.bitwise_xor(x, r >> 2), jnp.bitwise_xor(y, (r >> 1) & 1), jnp.bitwise_xor(c, r & 1))
            cp = pltpu.make_async_remote_copy(src_ref=src, dst_ref=out.at[me], send_sem=send_sems.at[r - 1],
                                              recv_sem=recv_sems.at[r - 1], device_id=peer, device_id_type=MESH)
            cp.start()
            copies.append(cp)
        for cp in copies:
            cp.wait()

    return pl.pallas_call(
        body, name=name, out_shape=jax.ShapeDtypeStruct((N_DEV, R, C), block.dtype),
        in_specs=[pl.BlockSpec(memory_space=pltpu.VMEM)], out_specs=pl.BlockSpec(memory_space=pltpu.VMEM),
        scratch_shapes=[pltpu.SemaphoreType.DMA((N_DEV - 1,)), pltpu.SemaphoreType.DMA((N_DEV - 1,))],
    )(block)


def _view(a):
    return a[0] if a.shape[0] == 1 else a


def _view_axis(a, axis):
    return axis - 1 if a.shape[0] == 1 else axis


def _rows(a):
    return a.reshape(-1, a.shape[-1])


def _elementwise(name, fn, ins, out_dtypes, tm):
    specs = []
    for a in ins:
        a, row0 = a if isinstance(a, tuple) else (a, 0)
        specs.append((_rows(a), a.shape[-1], 0, row0))
    shape = ins[0][0].shape if isinstance(ins[0], tuple) else ins[0].shape
    outs = _rowwise(name, fn, specs, [], [(shape[-1], dt) for dt in out_dtypes], tm=tm, n_rows=math.prod(shape[:-1]))
    return outs.reshape(shape) if len(out_dtypes) == 1 else [o.reshape(shape) for o in outs]


SMALL_ROWS = 8
CONV_ROWS = CONV_WIDTH * DN_QKV // D_MODEL
SMALL_GRAD_ROWS = 24


def _pack_small(vals, conv=None):
    tail = jnp.concatenate([vals["attn_q_gain"].reshape(-1), vals["attn_k_gain"].reshape(-1),
                            vals["dn_a_log"].reshape(-1), vals["dn_dt_bias"].reshape(-1),
                            vals["dn_o_gain"].reshape(-1)])
    tail = jnp.pad(tail, (0, D_MODEL - tail.shape[0])).reshape(1, D_MODEL)
    rows = [vals["mix_norm"], vals["mlp_norm"], vals["ple_norm"], tail, jnp.zeros((1, D_MODEL), F32)]
    if conv is not None:
        rows += [conv.reshape(CONV_ROWS, D_MODEL),
                 jnp.zeros((SMALL_GRAD_ROWS - SMALL_ROWS - CONV_ROWS, D_MODEL), F32)]
    return jnp.concatenate(rows, axis=0)


def _unpack_small(block):
    nq = 3 * A_HEAD_DIM
    t = block[6]
    return dict(
        mix_norm=block[0:2], mlp_norm=block[2:4], ple_norm=block[4:6],
        attn_q_gain=t[:nq].reshape(1, 3, A_HEAD_DIM), attn_k_gain=t[nq:2 * nq].reshape(1, 3, A_HEAD_DIM),
        dn_a_log=t[2 * nq:2 * nq + DN_HEADS].reshape(1, DN_HEADS),
        dn_dt_bias=t[2 * nq + DN_HEADS:2 * nq + 2 * DN_HEADS].reshape(1, DN_HEADS),
        dn_o_gain=t[2 * nq + 2 * DN_HEADS:2 * nq + 2 * DN_HEADS + DN_HEAD_DIM].reshape(1, DN_HEAD_DIM))


def kernel(x, p, positions, mix_norm, attn_w_qkv, attn_q_gain, attn_k_gain, attn_w_o, dn_w_in, dn_conv, dn_a_log, dn_dt_bias, dn_o_gain, dn_w_o, mlp_norm, w_up, w_down, ple_norm, w_ple, w_ple_gate, loss_target, m_mix_norm, m_attn_w_qkv, m_attn_q_gain, m_attn_k_gain, m_attn_w_o, m_dn_w_in, m_dn_conv, m_dn_a_log, m_dn_dt_bias, m_dn_o_gain, m_dn_w_o, m_mlp_norm, m_w_up, m_w_down, m_ple_norm, m_w_ple, m_w_ple_gate, v_mix_norm, v_attn_w_qkv, v_attn_q_gain, v_attn_k_gain, v_attn_w_o, v_dn_w_in, v_dn_conv, v_dn_a_log, v_dn_dt_bias, v_dn_o_gain, v_dn_w_o, v_mlp_norm, v_w_up, v_w_down, v_ple_norm, v_w_ple, v_w_ple_gate):
    given = dict(locals())
    w = {n: given[n] for n in WEIGHTS}
    m = {n: given["m_" + n] for n in WEIGHTS}
    v = {n: given["v_" + n] for n in WEIGHTS}
    kinds = ("grad", "delta", "new_m", "new_v")
    big_names = [(n, _view_axis(w[n], axis)) for n, axis in SHARDED if n != "dn_conv"]

    chip = 2 * lax.axis_index("x") + lax.axis_index("y")
    core = lax.axis_index("c")
    shards = [_view(w[n]).astype(BF16) for n, _ in big_names]
    gathered = _gather_shards("gather_weights", shards)
    full = {n: jnp.concatenate([jnp.where(chip == q, s, g[q]) for q in range(N_CHIPS)], axis=axis)
            for (n, axis), s, g in zip(big_names, shards, gathered)}
    conv_block = jnp.pad(w["dn_conv"].reshape(-1), (0, SMALL_ROWS * D_MODEL - w["dn_conv"].size))
    conv_all = _gather_from_all("gather_conv", conv_block.reshape(SMALL_ROWS, D_MODEL))
    conv_all = conv_all.reshape(N_CHIPS, 2, -1)[:, 0, :w["dn_conv"].size]
    conv_full = jnp.concatenate([conv_all[q].reshape(CONV_WIDTH, -1) for q in range(N_CHIPS)], axis=1)

    w_in = full["dn_w_in"]
    n_ab = 2 * DN_HEADS
    big = dict(full)
    big["dn_w_in"] = jnp.concatenate([w_in[:, DN_QKV + n_ab:], w_in[:, :DN_QKV + n_ab],
                                      jnp.zeros((D_MODEL, DN_AB_PAD - n_ab), BF16)], axis=1)
    small = {n: w[n] for n in REPLICATED}
    small["dn_conv"] = conv_full

    sq, grad_x, grads = _local_step(x[0], p[:, 0], positions[0], loss_target[0], small, big)
    loss = lax.psum(0.5 * sq[0, 0] / D_MODEL, ("x", "y", "c"))
    out = {}

    stacks = [jnp.stack(jnp.split(grads[n], N_CHIPS, axis=axis)) for n, axis in big_names]
    mine = [lax.dynamic_slice_in_dim(s, core * (s.shape[1] // 2), s.shape[1] // 2, axis=1) for s in stacks]
    theirs = _other_half_from_sibling("split_core_grads", stacks)
    chip_sums = [_elementwise(f"add_core_{n}", lambda a, b: a.astype(F32) + b.astype(F32), [a, b], [BF16], 128)
                 for (n, _), a, b in zip(big_names, mine, theirs)]
    landed = _scatter_to_chips("scatter_grads", chip_sums)
    half_sums = []
    for (n, _), s, r in zip(big_names, chip_sums, landed):
        o = lax.dynamic_index_in_dim(s, chip, axis=0, keepdims=False)
        per = math.prod(o.shape[:-1])
        half_sums.append(_elementwise(
            f"add_chips_{n}", lambda a, b, c, d: ((a.astype(F32) + b.astype(F32)) + c.astype(F32)) + d.astype(F32),
            [o, (r, 0), (r, per), (r, 2 * per)], [F32], 128))
    other_halves = _swap_with_sibling("join_core_sums", half_sums)
    for (n, _), a, b in zip(big_names, half_sums, other_halves):
        g = jnp.where(core == 0, jnp.concatenate([a, b], axis=0), jnp.concatenate([b, a], axis=0))
        shp = w[n].shape
        res = _elementwise(f"adamw_{n}", lambda g, w_, m_, v_: (g,) + _adamw(w_, g, m_, v_),
                           [g.reshape(shp), w[n], m[n], v[n]], [F32] * 4, 256)
        for kind, arr in zip(kinds, res):
            out[kind + "_" + n] = arr.reshape(shp)

    slots = _gather_from_all("gather_small_grads", _pack_small(grads, grads["dn_conv"]))

    def small_body(s_ref, w_ref, m_ref, v_ref, sum_out, g_out, d_out, m_out, v_out):
        total = s_ref[0]
        for d in range(1, N_DEV):
            total = total + s_ref[d]
        sum_out[...] = total
        g = total[:SMALL_ROWS]
        for o, r in zip((g_out, d_out, m_out, v_out), (g,) + _adamw(w_ref[...], g, m_ref[...], v_ref[...])):
            o[...] = r

    res = pl.pallas_call(small_body, name="adamw_replicated",
                         out_shape=[jax.ShapeDtypeStruct((SMALL_GRAD_ROWS, D_MODEL), F32)]
                         + [jax.ShapeDtypeStruct((SMALL_ROWS, D_MODEL), F32)] * 4)(
        slots, _pack_small(w), _pack_small(m), _pack_small(v))
    for kind, block in zip(kinds, res[1:]):
        for n, arr in _unpack_small(block).items():
            out[kind + "_" + n] = arr
    conv_sum = res[0][SMALL_ROWS:SMALL_ROWS + CONV_ROWS].reshape(CONV_WIDTH, DN_QKV)
    cols = DN_QKV // N_CHIPS
    chip = 2 * lax.axis_index("x") + lax.axis_index("y")
    conv_mine = lax.dynamic_slice_in_dim(conv_sum, chip * cols, cols, axis=1)
    res = _elementwise("adamw_dn_conv", lambda g, w_, m_, v_: (g,) + _adamw(w_, g, m_, v_),
                       [conv_mine, w["dn_conv"][0], m["dn_conv"][0], v["dn_conv"][0]], [F32] * 4, CONV_WIDTH)
    for kind, arr in zip(kinds, res):
        out[kind + "_dn_conv"] = arr[None]

    return (loss, grad_x[None],
            *[out["grad_" + n] for n in WEIGHTS], *[out["delta_" + n] for n in WEIGHTS],
            *[out["new_m_" + n] for n in WEIGHTS], *[out["new_v_" + n] for n in WEIGHTS])
```

```python
import functools
import math

import jax
import jax.numpy as jnp
from jax import lax
from jax.experimental import pallas as pl
from jax.experimental.pallas import tpu as pltpu

F32 = jnp.float32
BF16 = jnp.bfloat16
HIGHEST = lax.Precision.HIGHEST

D_MODEL = 1024
EPS = 1e-6
SWA_GROUPS = ((128, 1), (512, 4), (2048, 16))
A_HEADS = 8
A_HEAD_DIM = 64
A_WIDTH = A_HEADS * A_HEAD_DIM
ROPE_DIM = A_HEAD_DIM // 4
ROPE_THETA = 500000.0
BAND = 128
DN_HEADS = 8
DN_HEAD_DIM = 128
DN_WIDTH = DN_HEADS * DN_HEAD_DIM
DN_QKV = 3 * DN_WIDTH
DN_AB_PAD = 128
DN_IN_PAD = DN_WIDTH + DN_QKV + DN_AB_PAD
DN_QKV0 = DN_WIDTH
DN_AB0 = DN_WIDTH + DN_QKV
DN_HB = 8
CONV_WIDTH = 4
CHUNK = 64
PLE_DIM = 256
D_FF = 4 * D_MODEL

ADAM_LR = 0.001
ADAM_B1 = 0.9
ADAM_B2 = 0.999
ADAM_EPS = 1e-08
ADAM_WD = 0.01
ADAM_STEP = 10

N_CHIPS = 4
N_DEV = 8
VMEM_LIMIT = 48 * 1024 * 1024
MESH = pl.DeviceIdType.MESH

SHARDED = (
    ("attn_w_qkv", 2), ("attn_w_o", 2), ("dn_w_in", 2), ("dn_conv", 2), ("dn_w_o", 1),
    ("w_up", 2), ("w_down", 1), ("w_ple", 2), ("w_ple_gate", 1))
ATTN_MATRICES = ("attn_w_qkv", "attn_w_o")
CARGO_GROUPS = (("w_up",), ("w_down",), ("dn_w_in", "dn_w_o", "w_ple", "w_ple_gate"))
REPLICATED = ("mix_norm", "attn_q_gain", "attn_k_gain", "dn_a_log", "dn_dt_bias", "dn_o_gain",
              "mlp_norm", "ple_norm")
WEIGHTS = ("mix_norm", "attn_w_qkv", "attn_q_gain", "attn_k_gain", "attn_w_o", "dn_w_in", "dn_conv",
           "dn_a_log", "dn_dt_bias", "dn_o_gain", "dn_w_o", "mlp_norm", "w_up", "w_down", "ple_norm",
           "w_ple", "w_ple_gate")


def _cparams(sem=None):
    return pltpu.CompilerParams(dimension_semantics=sem, vmem_limit_bytes=VMEM_LIMIT)


def _pick(n, cap, quantum=128):
    best = None
    for t in range(quantum, min(n, cap) + 1, quantum):
        if n % t == 0:
            best = t
    return n if best is None else best


_DIMS = {"nn": ((1,), (0,)), "nt": ((1,), (1,)), "tn": ((0,), (0,))}


def _mm(name, a, b, mode, out_dtypes=(F32,), extras=(), epilogue=None, rows=()):
    if mode == "nn":
        (M, K), (K2, N) = a.shape, b.shape
    elif mode == "nt":
        (M, K), (N, K2) = a.shape, b.shape
    else:
        (K, M), (K2, N) = a.shape, b.shape
    assert K == K2, (name, a.shape, b.shape)
    tm, tn, tk = _pick(M, 1024), _pick(N, 1536), _pick(K, 1024 if mode == "tn" else 1536)
    nk = K // tk
    if mode == "nn":
        a_spec = pl.BlockSpec((tm, tk), lambda i, j, k: (i, k))
        b_spec = pl.BlockSpec((tk, tn), lambda i, j, k: (k, j))
    elif mode == "nt":
        a_spec = pl.BlockSpec((tm, tk), lambda i, j, k: (i, k))
        b_spec = pl.BlockSpec((tn, tk), lambda i, j, k: (j, k))
    else:
        a_spec = pl.BlockSpec((tk, tm), lambda i, j, k: (k, i))
        b_spec = pl.BlockSpec((tk, tn), lambda i, j, k: (k, j))
    o_spec = pl.BlockSpec((tm, tn), lambda i, j, k: (i, j))
    r_spec = pl.BlockSpec((1, tn), lambda i, j, k: (0, j))
    n_extra, n_out = len(extras) + len(rows), len(out_dtypes)
    dims = (_DIMS[mode], ((), ()))

    def body(a_ref, b_ref, *rest):
        extra_refs, out_refs = rest[:n_extra], rest[n_extra:n_extra + n_out]
        k = pl.program_id(2)
        part = lax.dot_general(a_ref[...].astype(BF16), b_ref[...].astype(BF16), dims, preferred_element_type=F32)

        def finish(total):
            vals = (total,) if epilogue is None else epilogue(total, *[e[...] for e in extra_refs])
            for o, v in zip(out_refs, vals):
                o[...] = v.astype(o.dtype)

        if nk == 1:
            finish(part)
            return
        acc = rest[-1]

        @pl.when(k == 0)
        def _():
            acc[...] = part

        @pl.when(jnp.logical_and(k > 0, k < nk - 1))
        def _():
            acc[...] += part

        @pl.when(k == nk - 1)
        def _():
            finish(acc[...] + part)

    outs = pl.pallas_call(
        body, name=name, grid=(M // tm, N // tn, nk),
        in_specs=[a_spec, b_spec] + [o_spec] * len(extras) + [r_spec] * len(rows),
        out_specs=[o_spec] * n_out,
        out_shape=[jax.ShapeDtypeStruct((M, N), dt) for dt in out_dtypes],
        scratch_shapes=[pltpu.VMEM((tm, tn), F32)] if nk > 1 else [],
        compiler_params=_cparams(("parallel", "parallel", "arbitrary")),
    )(a, b, *extras, *rows)
    return outs[0] if n_out == 1 else outs


def _rowwise(name, fn, rows, bcast, row_outs, acc_outs=(), tm=256, n_rows=None):
    rows = [r if isinstance(r, tuple) else (r, r.shape[1], 0) for r in rows]
    rows = [r if len(r) == 4 else r + (0,) for r in rows]
    S = rows[0][0].shape[0] if n_rows is None else n_rows
    tm = min(tm, S)
    assert S % tm == 0 and all(r[3] % tm == 0 for r in rows), (name, S, tm)
    n_row, n_bc, n_ro, n_acc = len(rows), len(bcast), len(row_outs), len(acc_outs)
    in_specs = [pl.BlockSpec((tm, w), functools.partial(lambda i, cb, rb: (i + rb, cb), cb=cb, rb=r0 // tm))
                for _, w, cb, r0 in rows]
    in_specs += [pl.BlockSpec(b.shape, lambda i: (0, 0)) for b in bcast]
    out_specs = [pl.BlockSpec((tm, c), lambda i: (i, 0)) for c, _ in row_outs]
    out_specs += [pl.BlockSpec(s, lambda i: (0, 0)) for s in acc_outs]
    out_shape = [jax.ShapeDtypeStruct((S, c), dt) for c, dt in row_outs]
    out_shape += [jax.ShapeDtypeStruct(s, F32) for s in acc_outs]

    def body(*refs):
        ins = [r[...] for r in refs[:n_row + n_bc]]
        outs = refs[n_row + n_bc:]
        vals = fn(*ins)
        if not isinstance(vals, (tuple, list)):
            vals = (vals,)
        for o, v in zip(outs[:n_ro], vals[:n_ro]):
            o[...] = v.astype(o.dtype)
        if n_acc:
            @pl.when(pl.program_id(0) == 0)
            def _():
                for o in outs[n_ro:]:
                    o[...] = jnp.zeros_like(o)
            for o, v in zip(outs[n_ro:], vals[n_ro:]):
                o[...] += v

    outs = pl.pallas_call(
        body, name=name, grid=(S // tm,), in_specs=in_specs, out_specs=out_specs, out_shape=out_shape,
        compiler_params=_cparams(("arbitrary",) if n_acc else ("parallel",)),
    )(*[r[0] for r in rows], *bcast)
    return outs[0] if len(outs) == 1 else outs


def _sigmoid(x):
    return 1.0 / (1.0 + jnp.exp(-x))


def _silu(x):
    return x * _sigmoid(x)


def _softplus(x):
    return jnp.maximum(x, 0.0) + jnp.log(1.0 + jnp.exp(-jnp.abs(x)))


def _rms_fwd_fn(x, g):
    r = lax.rsqrt(jnp.mean(x * x, axis=-1, keepdims=True) + EPS)
    return (x * r) * g


def _rms_bwd_fn(x, dres, *rest):
    dh, g = sum(rest[:-1]), rest[-1]
    r = lax.rsqrt(jnp.mean(x * x, axis=-1, keepdims=True) + EPS)
    xh = x * r
    dxh = dh * g
    dx = dres + r * (dxh - xh * jnp.mean(dxh * xh, axis=-1, keepdims=True))
    return dx, dx, jnp.sum(dh * xh, axis=0, keepdims=True)


def _rms_fwd(name, x, gain):
    return _rowwise(name, _rms_fwd_fn, [x], [gain.reshape(1, -1)], [(x.shape[1], BF16)])


def _rms_bwd(name, x, gain, dres, dhs):
    return _rowwise(name, _rms_bwd_fn, [x, dres] + list(dhs), [gain.reshape(1, -1)],
                    [(x.shape[1], F32), (x.shape[1], BF16)], [(1, x.shape[1])])


def _relu2_epilogue(acc):
    r = jnp.maximum(acc, 0.0)
    return (r * r,)


def _relu2_bwd_epilogue(acc, a):
    return (acc * (2.0 * jnp.sqrt(a.astype(F32))),)


def _ple_fwd_fn(x, pp, zg):
    return x + pp * _sigmoid(zg)


def _ple_norm_fwd_fn(x, pp, zg, gain):
    out = _ple_fwd_fn(x, pp, zg)
    return out, _rms_fwd_fn(out, gain)


def _ple_bwd_fn(dx, pp, zg):
    gate = _sigmoid(zg)
    return dx * gate, dx * pp * gate * (1.0 - gate)


def _loss_fn(y, t):
    err = y - t
    return err * (1.0 / D_MODEL), jnp.broadcast_to(jnp.sum(err * err, keepdims=True), (1, 128))


def _adamw(w, g, m, v):
    m = ADAM_B1 * m + (1.0 - ADAM_B1) * g
    v = ADAM_B2 * v + (1.0 - ADAM_B2) * jnp.square(g)
    m_hat = m / (1.0 - ADAM_B1 ** ADAM_STEP)
    v_hat = v / (1.0 - ADAM_B2 ** ADAM_STEP)
    delta = -ADAM_LR * (m_hat / (jnp.sqrt(v_hat) + ADAM_EPS) + ADAM_WD * w)
    return delta, m, v


def _lane_take(x, offset):
    n = x.shape[-1]
    return pltpu.roll(x, (-offset) % n, 1)


def _head_lane(shape):
    return lax.broadcasted_iota(jnp.int32, shape, 1) % A_HEAD_DIM


def _rope_partner(x):
    lane = _head_lane(x.shape)
    return jnp.where(lane < ROPE_DIM // 2, _lane_take(x, ROPE_DIM // 2),
                     jnp.where(lane < ROPE_DIM, _lane_take(x, -(ROPE_DIM // 2)), 0.0))


def _head_mean(x, bd):
    hi = x.astype(BF16)
    lo = (x - hi.astype(F32)).astype(BF16)
    b = bd.astype(BF16)
    return jnp.dot(hi, b, preferred_element_type=F32) + jnp.dot(lo, b, preferred_element_type=F32)


def _fold_heads(row):
    out = row[:, :A_HEAD_DIM]
    for h in range(1, A_HEADS):
        out = out + row[:, h * A_HEAD_DIM:(h + 1) * A_HEAD_DIM]
    return out


def _all_heads(t):
    return jnp.concatenate([t] * (A_WIDTH // t.shape[1]), axis=1)


def _qk_prep_fwd_fn(qkv, ct, st, gq, gk, bd):
    ct, st = _all_heads(ct), _all_heads(st)

    def one(t, g):
        n = t * lax.rsqrt(_head_mean(t * t, bd) + EPS) * g
        return n * ct + _rope_partner(n) * st
    q, k, v = qkv[:, :A_WIDTH], qkv[:, A_WIDTH:2 * A_WIDTH], qkv[:, 2 * A_WIDTH:]
    return one(q, gq), one(k, gk), v


def _qk_prep_bwd_fn(qkv, ct, st, dq, dk, dv, gq, gk, bd):
    ct, st = _all_heads(ct), _all_heads(st)

    def one(t, g, dy):
        r = lax.rsqrt(_head_mean(t * t, bd) + EPS)
        nh = t * r
        dn = dy * ct + _rope_partner(dy * st)
        dg = jnp.sum(dn * nh, axis=0, keepdims=True)
        dnh = dn * g
        return r * (dnh - nh * _head_mean(dnh * nh, bd)), _fold_heads(dg)
    q, k = qkv[:, :A_WIDTH], qkv[:, A_WIDTH:2 * A_WIDTH]
    dq_raw, dgq = one(q, gq, dq)
    dk_raw, dgk = one(k, gk, dk)
    return jnp.concatenate([dq_raw, dk_raw, dv], axis=1), dgq, dgk


_BATCH_DIMS = {"nn": ((2,), (1,)), "nt": ((2,), (2,)), "tn": ((1,), (1,))}


def _bdot(a, b, mode, precision=None):
    return lax.dot_general(a, b, (_BATCH_DIMS[mode], ((0,), (0,))), precision=precision,
                           preferred_element_type=F32)


def _attn_cols(h):
    return slice(h * A_HEAD_DIM, (h + 1) * A_HEAD_DIM)


def _attn_heads(ref):
    return jnp.stack([ref[:, _attn_cols(h)] for h in range(A_HEADS)])


def _band_masks():
    qi = lax.broadcasted_iota(jnp.int32, (BAND, BAND), 0)
    kj = lax.broadcasted_iota(jnp.int32, (BAND, BAND), 1)
    return kj <= qi, kj >= qi


def _attn_fwd(name, q, k, v, blocks_per_class, cargo=None):
    S = q.shape[0]
    nblk = S // BAND
    scale = A_HEAD_DIM ** -0.5

    def body(q_ref, kp_ref, kc_ref, vp_ref, vc_ref, o_ref, l_ref):
        i = pl.program_id(0)
        has_prev = (i % blocks_per_class) != 0
        m_cur, m_prev = _band_masks()
        m_prev = jnp.logical_and(m_prev, has_prev)
        q, kc, kp, vc, vp = (_attn_heads(r) for r in (q_ref, kc_ref, kp_ref, vc_ref, vp_ref))
        s_c = jnp.where(m_cur[None], _bdot(q, kc, "nt") * scale, -jnp.inf)
        s_p = jnp.where(m_prev[None], _bdot(q, kp, "nt") * scale, -jnp.inf)
        m = jnp.maximum(jnp.max(s_c, axis=-1, keepdims=True), jnp.max(s_p, axis=-1, keepdims=True))
        e_c, e_p = jnp.exp(s_c - m), jnp.exp(s_p - m)
        l = jnp.sum(e_c, axis=-1, keepdims=True) + jnp.sum(e_p, axis=-1, keepdims=True)
        o = _bdot((e_c / l).astype(BF16), vc, "nn") + _bdot((e_p / l).astype(BF16), vp, "nn")
        lse = m + jnp.log(l)
        for h in range(A_HEADS):
            o_ref[:, _attn_cols(h)] = o[h]
            l_ref[:, _attn_cols(h)] = jnp.broadcast_to(lse[h], (BAND, A_HEAD_DIM))

    cur = pl.BlockSpec((BAND, A_WIDTH), lambda i: (i, 0))
    prev = pl.BlockSpec((BAND, A_WIDTH), lambda i: (jnp.maximum(i - 1, 0), 0))
    body, c_in_specs, c_out_specs, c_out_shape, c_scratch, c_ins = _carry(cargo, 5, 2, 0, body, nblk)
    outs = pl.pallas_call(
        body, name=name, grid=(nblk,), in_specs=[cur, prev, cur, prev, cur] + c_in_specs,
        out_specs=[cur, cur] + c_out_specs,
        out_shape=[jax.ShapeDtypeStruct((S, A_WIDTH), F32)] * 2 + c_out_shape, scratch_shapes=c_scratch,
        compiler_params=_cparams(("arbitrary",)),
    )(q, k, k, v, v, *c_ins)
    return outs[0], outs[1], outs[2:]


def _carry(cargo, n_in, n_out, n_scratch, body, steps):
    if cargo is None:
        return body, [], [], [], [], []
    n_ci, n_co = len(cargo.ins), len(cargo.out_shape)

    def carrying(*refs):
        refs = list(refs)
        ins, refs = refs[:n_in], refs[n_in:]
        c_ins, refs = refs[:n_ci], refs[n_ci:]
        outs, refs = refs[:n_out], refs[n_out:]
        c_outs, refs = refs[:n_co], refs[n_co:]
        scratch, sems = refs[:n_scratch], refs[n_scratch:]

        @pl.when(pl.program_id(0) == 0)
        def _():
            cargo.start(c_ins, c_outs, sems)

        body(*ins, *outs, *scratch)

        @pl.when(pl.program_id(0) == steps - 1)
        def _():
            cargo.finish(c_ins, c_outs, sems)

    any_spec = pl.BlockSpec(memory_space=pl.ANY)
    return carrying, [any_spec] * n_ci, [any_spec] * n_co, list(cargo.out_shape), list(cargo.scratch), list(cargo.ins)


def _attn_bwd(name, q, k, v, o, lse, do, dlse, blocks_per_class, cargo=None):
    S = q.shape[0]
    nblk = S // BAND
    scale = A_HEAD_DIM ** -0.5

    def body(q_ref, kp_ref, kc_ref, vp_ref, vc_ref, o_ref, l_ref, do_ref, dl_ref,
             dq_ref, dk_ref, dv_ref, ck, cv):
        i = pl.program_id(0)

        @pl.when(i == 0)
        def _():
            ck[...] = jnp.zeros_like(ck)
            cv[...] = jnp.zeros_like(cv)

        @pl.when(i == nblk)
        def _():
            dk_ref[...] = ck[...]
            dv_ref[...] = cv[...]

        @pl.when(i < nblk)
        def _():
            has_prev = (i % blocks_per_class) != 0
            m_cur, m_prev = _band_masks()
            m_prev = jnp.logical_and(m_prev, has_prev)
            q, kc, kp, vc, vp = (_attn_heads(r) for r in (q_ref, kc_ref, kp_ref, vc_ref, vp_ref))
            do, o, dl = _attn_heads(do_ref), _attn_heads(o_ref), _attn_heads(dl_ref)
            lse = jnp.max(_attn_heads(l_ref), axis=-1, keepdims=True)
            p_c = jnp.where(m_cur[None], jnp.exp(_bdot(q, kc, "nt") * scale - lse), 0.0)
            p_p = jnp.where(m_prev[None], jnp.exp(_bdot(q, kp, "nt") * scale - lse), 0.0)
            corr = jnp.sum(dl, axis=-1, keepdims=True) - jnp.sum(do * o, axis=-1, keepdims=True)
            dob = do.astype(BF16)
            ds_c = (p_c * (_bdot(dob, vc, "nt") + corr)).astype(BF16)
            ds_p = (p_p * (_bdot(dob, vp, "nt") + corr)).astype(BF16)
            dq = (_bdot(ds_c, kc, "nn") + _bdot(ds_p, kp, "nn")) * scale
            dk_p, dk_c = _bdot(ds_p, q, "tn") * scale, _bdot(ds_c, q, "tn") * scale
            dv_p, dv_c = _bdot(p_p.astype(BF16), dob, "tn"), _bdot(p_c.astype(BF16), dob, "tn")
            for h in range(A_HEADS):
                sl = _attn_cols(h)
                dq_ref[:, sl] = dq[h]
                dk_ref[:, sl] = ck[:, sl] + dk_p[h]
                dv_ref[:, sl] = cv[:, sl] + dv_p[h]
                ck[:, sl] = dk_c[h]
                cv[:, sl] = dv_c[h]

    last = nblk - 1
    cur = pl.BlockSpec((BAND, A_WIDTH), lambda i: (jnp.minimum(i, last), 0))
    prev = pl.BlockSpec((BAND, A_WIDTH), lambda i: (jnp.minimum(jnp.maximum(i - 1, 0), last), 0))
    body, c_in_specs, c_out_specs, c_out_shape, c_scratch, c_ins = _carry(cargo, 9, 3, 2, body, nblk + 1)
    outs = pl.pallas_call(
        body, name=name, grid=(nblk + 1,),
        in_specs=[cur, prev, cur, prev, cur, cur, cur, cur, cur] + c_in_specs,
        out_specs=[cur, prev, prev] + c_out_specs,
        out_shape=[jax.ShapeDtypeStruct((S, A_WIDTH), F32)] * 3 + c_out_shape,
        scratch_shapes=[pltpu.VMEM((BAND, A_WIDTH), F32)] * 2 + c_scratch,
        compiler_params=_cparams(("arbitrary",)),
    )(q, k, k, v, v, o, lse, do, dlse, *c_ins)
    return outs[0], outs[1], outs[2], outs[3:]


def _merge_fwd_fn(o0, o1, o2, l0, l1, l2):
    m = jnp.maximum(jnp.maximum(l0, l1), l2)
    e0, e1, e2 = jnp.exp(l0 - m), jnp.exp(l1 - m), jnp.exp(l2 - m)
    return (e0 * o0 + e1 * o1 + e2 * o2) / (e0 + e1 + e2)


def _merge_bwd_fn(o0, o1, o2, l0, l1, l2, dom):
    m = jnp.maximum(jnp.maximum(l0, l1), l2)
    e0, e1, e2 = jnp.exp(l0 - m), jnp.exp(l1 - m), jnp.exp(l2 - m)
    den = e0 + e1 + e2
    w0, w1, w2 = e0 / den, e1 / den, e2 / den
    dw0, dw1, dw2 = dom * o0, dom * o1, dom * o2
    mean = w0 * dw0 + w1 * dw1 + w2 * dw2
    return w0 * dom, w1 * dom, w2 * dom, w0 * (dw0 - mean), w1 * (dw1 - mean), w2 * (dw2 - mean)


def _to_classes(t, d):
    if d == 1:
        return t
    S, C = t.shape
    return t.reshape(S // d, d, C).transpose(1, 0, 2).reshape(S, C)


def _from_classes(t, d):
    if d == 1:
        return t
    S, C = t.shape
    return t.reshape(d, S // d, C).transpose(1, 0, 2).reshape(S, C)


def _rope_lane_tables(positions):
    inv_freq = ROPE_THETA ** (-jnp.arange(0, ROPE_DIM, 2, dtype=F32) / ROPE_DIM)
    ang = positions.astype(F32)[:, None] * inv_freq
    cos, sin = jnp.cos(ang), jnp.sin(ang)
    S = positions.shape[0]
    rest = A_HEAD_DIM - ROPE_DIM
    ct = jnp.concatenate([cos, cos, jnp.ones((S, rest), F32)], axis=1)
    st = jnp.concatenate([-sin, sin, jnp.zeros((S, rest), F32)], axis=1)
    return jnp.tile(ct, (1, 2)), jnp.tile(st, (1, 2))


def _head_mean_matrix():
    r = jnp.arange(A_WIDTH) // A_HEAD_DIM
    return (r[:, None] == r[None, :]).astype(F32) * (1.0 / A_HEAD_DIM)


def _conv_fwd(name, proj, w):
    S = proj.shape[0]
    tm, tc = min(512, S), 1024
    per8 = tm // 8
    off = DN_QKV0 // tc

    def body(x_ref, halo_ref, w_ref, o_ref, xs):
        i = pl.program_id(0)
        xs[0:8, :] = jnp.where(i > 0, halo_ref[...], 0.0)
        xs[8:, :] = x_ref[...]
        acc = w_ref[0:1, :] * xs[pl.ds(8 - 3, tm), :]
        for j in range(1, CONV_WIDTH):
            acc = acc + w_ref[j:j + 1, :] * xs[pl.ds(8 - 3 + j, tm), :]
        o_ref[...] = acc

    return pl.pallas_call(
        body, name=name, grid=(S // tm, DN_QKV // tc),
        in_specs=[pl.BlockSpec((tm, tc), lambda i, j: (i, j + off)),
                  pl.BlockSpec((8, tc), lambda i, j: (jnp.maximum(i * per8 - 1, 0), j + off)),
                  pl.BlockSpec((CONV_WIDTH, tc), lambda i, j: (0, j))],
        out_specs=pl.BlockSpec((tm, tc), lambda i, j: (i, j)),
        out_shape=jax.ShapeDtypeStruct((S, DN_QKV), F32),
        scratch_shapes=[pltpu.VMEM((tm + 8, tc), F32)],
        compiler_params=_cparams(("parallel", "parallel")),
    )(proj, proj, w)


def _conv_bwd(name, proj, dpre, w):
    S = proj.shape[0]
    tm, tc = min(512, S), 1024
    per8 = tm // 8
    off = DN_QKV0 // tc
    last8 = S // 8 - 1
    nrow = S // tm

    def body(x_ref, xh_ref, d_ref, dh_ref, w_ref, dx_ref, dw_ref, xs, ds):
        i = pl.program_id(1)
        xs[0:8, :] = jnp.where(i > 0, xh_ref[...], 0.0)
        xs[8:, :] = x_ref[...]
        ds[0:tm, :] = d_ref[...]
        ds[tm:, :] = jnp.where(i < nrow - 1, dh_ref[...], 0.0)
        d = d_ref[...]
        acc = w_ref[0:1, :] * ds[pl.ds(3, tm), :]
        for j in range(1, CONV_WIDTH):
            acc = acc + w_ref[j:j + 1, :] * ds[pl.ds(3 - j, tm), :]
        dx_ref[...] = acc.astype(dx_ref.dtype)

        @pl.when(i == 0)
        def _():
            dw_ref[...] = jnp.zeros_like(dw_ref)

        for j in range(CONV_WIDTH):
            dw_ref[j:j + 1, :] += jnp.sum(d * xs[pl.ds(8 - 3 + j, tm), :], axis=0, keepdims=True)

    return pl.pallas_call(
        body, name=name, grid=(DN_QKV // tc, nrow),
        in_specs=[pl.BlockSpec((tm, tc), lambda j, i: (i, j + off)),
                  pl.BlockSpec((8, tc), lambda j, i: (jnp.maximum(i * per8 - 1, 0), j + off)),
                  pl.BlockSpec((tm, tc), lambda j, i: (i, j)),
                  pl.BlockSpec((8, tc), lambda j, i: (jnp.minimum((i + 1) * per8, last8), j)),
                  pl.BlockSpec((CONV_WIDTH, tc), lambda j, i: (0, j))],
        out_specs=[pl.BlockSpec((tm, tc), lambda j, i: (i, j)),
                   pl.BlockSpec((CONV_WIDTH, tc), lambda j, i: (0, j))],
        out_shape=[jax.ShapeDtypeStruct((S, DN_QKV), BF16), jax.ShapeDtypeStruct((CONV_WIDTH, DN_QKV), F32)],
        scratch_shapes=[pltpu.VMEM((tm + 8, tc), F32)] * 2,
        compiler_params=_cparams(("parallel", "arbitrary")),
    )(proj, proj, dpre, dpre, w)


def _gate_lane(shape):
    return lax.broadcasted_iota(jnp.int32, shape, 1)


GATES_ROWS = 256


def _chunk_cumsum_matrix():
    r = jnp.arange(GATES_ROWS)
    return ((r[:, None] >= r[None, :]) & (r[:, None] // CHUNK == r[None, :] // CHUNK)).astype(F32)


def _gates_fwd_fn(ab, alog, dt, cum):
    g = -jnp.exp(alog) * _softplus(ab + dt)
    gc = jnp.dot(cum, g, precision=HIGHEST, preferred_element_type=F32)
    return jnp.where(_gate_lane(ab.shape) < DN_HEADS, gc, _sigmoid(ab))


def _gates_bwd_fn(ab, dgb, alog, dt, cum):
    lane = _gate_lane(ab.shape)
    is_g = lane < DN_HEADS
    neg_a = -jnp.exp(alog)
    sp = _softplus(ab + dt)
    dsp = _sigmoid(ab + dt)
    beta = _sigmoid(ab)
    dgc = jnp.where(is_g, dgb, 0.0)
    dg = lax.dot_general(cum, dgc, (_DIMS["tn"], ((), ())), precision=HIGHEST, preferred_element_type=F32)
    dab = jnp.where(is_g, dg * neg_a * dsp, jnp.where(lane < 2 * DN_HEADS, dgb * beta * (1.0 - beta), 0.0))
    d_alog = jnp.sum(dg * neg_a * sp, axis=0, keepdims=True)
    d_dt = jnp.sum(dg * neg_a * dsp, axis=0, keepdims=True)
    return dab, d_alog, d_dt


def _chunk_math(precision):
    def dg(a, b, mode, prec=precision):
        return _bdot(a, b, mode, prec)

    @jax.custom_vjp
    def nn(a, b):
        return dg(a, b, "nn")

    @jax.custom_vjp
    def nt(a, b):
        return dg(a, b, "nt")

    @jax.custom_vjp
    def tn(a, b):
        return dg(a, b, "tn")

    nn.defvjp(lambda a, b: (nn(a, b), (a, b)), lambda r, g: (nt(g, r[1]), tn(r[0], g)))
    nt.defvjp(lambda a, b: (nt(a, b), (a, b)), lambda r, g: (nn(g, r[1]), tn(g, r[0])))
    tn.defvjp(lambda a, b: (tn(a, b), (a, b)), lambda r, g: (nt(r[1], g), nn(r[0], g)))

    def split(x):
        hi = x.astype(BF16)
        return hi, (x - hi.astype(F32)).astype(BF16)

    def fine(a, b, mode):
        ah, al = split(a)
        bh, bl = split(b)
        return dg(ah, bh, mode, None) + (dg(ah, bl, mode, None) + dg(al, bh, mode, None))

    def unit_lower_inverse(a):
        row = lax.broadcasted_iota(jnp.int32, a.shape, 1)
        col = lax.broadcasted_iota(jnp.int32, a.shape, 2)
        x = -a
        p = jnp.where(row == col, 1.0, 0.0) + x
        for _ in range(int(math.log2(CHUNK)) - 1):
            x = fine(x, x, "nn")
            p = p + fine(p, x, "nn")
        return p

    @jax.custom_vjp
    def solve2(a, ti, r1, r2):
        return fine(ti, r1, "nn"), fine(ti, r2, "nn")

    def solve2_fwd(a, ti, r1, r2):
        s1, s2 = fine(ti, r1, "nn"), fine(ti, r2, "nn")
        return (s1, s2), (ti, s1, s2)

    def solve2_bwd(res, g):
        ti, s1, s2 = res
        d1, d2 = fine(ti, g[0], "tn"), fine(ti, g[1], "tn")
        return -(fine(d1, s1, "nt") + fine(d2, s2, "nt")), jnp.zeros_like(ti), d1, d2

    solve2.defvjp(solve2_fwd, solve2_bwd)

    def chunk_fn(pq, pk, pv, z, g_col, b_col, g_row, ogain, s_in, inverse=None):
        nb = pq.shape[0]
        sq = (nb, CHUNK, CHUNK)
        row = lax.broadcasted_iota(jnp.int32, sq, 1)
        col = lax.broadcasted_iota(jnp.int32, sq, 2)
        lower, strict = row >= col, row > col
        q, k, v = _silu(pq), _silu(pk), _silu(pv)
        q = q * lax.rsqrt(jnp.sum(q * q, axis=-1, keepdims=True) + EPS) * (DN_HEAD_DIM ** -0.5)
        k = k * lax.rsqrt(jnp.sum(k * k, axis=-1, keepdims=True) + EPS)
        gc_wide = jnp.broadcast_to(g_col, pq.shape)
        gc_i = jnp.broadcast_to(g_col, sq)
        gc_j = jnp.broadcast_to(g_row, sq)
        is_last = lax.broadcasted_iota(jnp.int32, pq.shape, 1) == CHUNK - 1
        g_last = jnp.sum(jnp.where(is_last, gc_wide, 0.0), axis=1, keepdims=True)
        decay = jnp.exp(jnp.where(lower, gc_i - gc_j, -jnp.inf))
        kb = k * b_col
        a_mat = jnp.where(strict, nt(kb, k) * decay, 0.0)
        eg = jnp.exp(gc_wide)
        ti = unit_lower_inverse(a_mat) if inverse is None else inverse
        u, w = solve2(a_mat, ti, v * b_col, kb * eg)
        attn = nt(q, k) * decay
        q_dec = q * eg
        k_dec = k * jnp.exp(g_last - gc_wide)
        c_dec = jnp.exp(g_last)
        v_new = u - nn(w, s_in)
        o = nn(q_dec, s_in) + nn(attn, v_new)
        s_out = s_in * c_dec + tn(k_dec, v_new)
        y = o * lax.rsqrt(jnp.mean(o * o, axis=-1, keepdims=True) + EPS) * ogain * _silu(z)
        return (y, s_out, ti) if inverse is None else (y, s_out)

    return chunk_fn


DN_PRECISION = None


def _chunk_specs(n_of):
    groups = DN_HEADS // DN_HB
    wide = DN_HB * DN_HEAD_DIM
    hd = pl.BlockSpec((CHUNK, wide), lambda h, n: (n_of(n), h))
    specs = dict(
        pq=hd,
        pk=pl.BlockSpec((CHUNK, wide), lambda h, n: (n_of(n), groups + h)),
        pv=pl.BlockSpec((CHUNK, wide), lambda h, n: (n_of(n), 2 * groups + h)),
        z=hd,
        col=pl.BlockSpec((DN_HB, CHUNK, 1), lambda h, n: (h, n_of(n), 0)),
        row=pl.BlockSpec((DN_HB, None, 1, CHUNK), lambda h, n: (h, n_of(n), 0, 0)),
        gain=pl.BlockSpec((1, DN_HEAD_DIM), lambda h, n: (0, 0)),
        state=pl.BlockSpec((DN_HB, None, DN_HEAD_DIM, DN_HEAD_DIM), lambda h, n: (h, n_of(n), 0, 0)),
        inverse=pl.BlockSpec((DN_HB, None, CHUNK, CHUNK), lambda h, n: (h, n_of(n), 0, 0)),
        qkv=pl.BlockSpec((CHUNK, DN_QKV), lambda h, n: (n_of(n), 0)),
        head=hd,
    )
    return specs


def _head_cols(j):
    return slice(j * DN_HEAD_DIM, (j + 1) * DN_HEAD_DIM)


def _split_heads(ref):
    return jnp.stack([ref[:, _head_cols(j)] for j in range(DN_HB)])


def _chunk_fwd(name, pre, proj, g_col, b_col, g_row, ogain):
    S = pre.shape[0]
    N = S // CHUNK
    chunk_fn = _chunk_math(DN_PRECISION)
    sp = _chunk_specs(lambda n: n)

    def body(pq, pk, pv, z, gc, bc, gr, og, y_ref, sin_ref, inv_ref, st):
        @pl.when(pl.program_id(1) == 0)
        def _():
            st[...] = jnp.zeros_like(st)

        s_in = st[...]
        sin_ref[...] = s_in
        y, s_out, inverse = chunk_fn(_split_heads(pq), _split_heads(pk), _split_heads(pv), _split_heads(z),
                                     gc[...], bc[...], gr[...], og[...], s_in)
        for j in range(DN_HB):
            y_ref[:, _head_cols(j)] = y[j].astype(y_ref.dtype)
        inv_ref[...] = inverse
        st[...] = s_out

    return pl.pallas_call(
        body, name=name, grid=(DN_HEADS // DN_HB, N),
        in_specs=[sp["pq"], sp["pk"], sp["pv"], sp["z"], sp["col"], sp["col"], sp["row"], sp["gain"]],
        out_specs=[sp["head"], sp["state"], sp["inverse"]],
        out_shape=[jax.ShapeDtypeStruct((S, DN_WIDTH), BF16),
                   jax.ShapeDtypeStruct((DN_HEADS, N, DN_HEAD_DIM, DN_HEAD_DIM), F32),
                   jax.ShapeDtypeStruct((DN_HEADS, N, CHUNK, CHUNK), F32)],
        scratch_shapes=[pltpu.VMEM((DN_HB, DN_HEAD_DIM, DN_HEAD_DIM), F32)],
        compiler_params=_cparams(("parallel", "arbitrary")),
    )(pre, pre, pre, proj, g_col, b_col, g_row, ogain)


def _chunk_bwd(name, pre, proj, g_col, b_col, g_row, ogain, s_in_all, inverse_all, dy):
    assert DN_HB == DN_HEADS
    S = pre.shape[0]
    N = S // CHUNK
    chunk_fn = _chunk_math(DN_PRECISION)
    sp = _chunk_specs(lambda n: N - 1 - n)

    def body(pq, pk, pv, z, gc, bc, gr, og, sin_ref, inv_ref, dy_ref,
             dpre_ref, dz_ref, dgc_ref, dbc_ref, dgr_ref, dog_ref, ds):
        @pl.when(pl.program_id(1) == 0)
        def _():
            ds[...] = jnp.zeros_like(ds)
            dog_ref[...] = jnp.zeros_like(dog_ref)

        inverse = inv_ref[...]
        prim = (_split_heads(pq), _split_heads(pk), _split_heads(pv), _split_heads(z),
                gc[...], bc[...], gr[...], og[...], sin_ref[...])
        _, vjp = jax.vjp(lambda *a: chunk_fn(*a, inverse=inverse), *prim)
        gq, gk, gv, gz, ggc, gbc, ggr, gog, gs = vjp((_split_heads(dy_ref), ds[...]))
        for j in range(DN_HB):
            for part, g in enumerate((gq, gk, gv)):
                dpre_ref[:, pl.ds(part * DN_WIDTH + j * DN_HEAD_DIM, DN_HEAD_DIM)] = g[j]
            dz_ref[:, _head_cols(j)] = gz[j]
        dgc_ref[...] = ggc
        dbc_ref[...] = gbc
        dgr_ref[...] = ggr
        dog_ref[...] += gog
        ds[...] = gs

    hd = sp["head"]
    return pl.pallas_call(
        body, name=name, grid=(1, N),
        in_specs=[sp["pq"], sp["pk"], sp["pv"], sp["z"], sp["col"], sp["col"], sp["row"], sp["gain"],
                  sp["state"], sp["inverse"], hd],
        out_specs=[sp["qkv"], hd, sp["col"], sp["col"], sp["row"], sp["gain"]],
        out_shape=[jax.ShapeDtypeStruct((S, DN_QKV), F32), jax.ShapeDtypeStruct((S, DN_WIDTH), F32)]
        + [jax.ShapeDtypeStruct((DN_HEADS, S, 1), F32)] * 2
        + [jax.ShapeDtypeStruct((DN_HEADS, N, 1, CHUNK), F32), jax.ShapeDtypeStruct((1, DN_HEAD_DIM), F32)],
        scratch_shapes=[pltpu.VMEM((DN_HB, DN_HEAD_DIM, DN_HEAD_DIM), F32)],
        compiler_params=_cparams(("arbitrary", "arbitrary")),
    )(pre, pre, pre, proj, g_col, b_col, g_row, ogain, s_in_all, inverse_all, dy)


def _residual_norm_epilogue(acc, res, gain):
    x = acc + res
    return x, _rms_fwd_fn(x, gain)


def _mlp_ple_fwd(tag, x_in, h, p_l, w_up, w_down, norm_ple, w_ple, w_gate, next_gain=None):
    a = _mm(f"{tag}_up", h, w_up, "nn", out_dtypes=(BF16,), epilogue=_relu2_epilogue)
    x_mid, hg = _mm(f"{tag}_down", a, w_down, "nn", out_dtypes=(F32, BF16), extras=(x_in,),
                    rows=(norm_ple.reshape(1, -1),), epilogue=_residual_norm_epilogue)
    zg = _mm(f"{tag}_gate", hg, w_gate, "nn")
    pp = _mm(f"{tag}_ple", p_l, w_ple, "nn")
    sv = dict(x_in=x_in, h=h, a=a, x_mid=x_mid, hg=hg, zg=zg, pp=pp)
    if next_gain is None:
        return _rowwise(f"{tag}_ple_out", _ple_fwd_fn, [x_mid, pp, zg], [], [(D_MODEL, F32)]), None, sv
    x_out, h_next = _rowwise(f"{tag}_ple_out", _ple_norm_fwd_fn, [x_mid, pp, zg], [next_gain.reshape(1, -1)],
                             [(D_MODEL, F32), (D_MODEL, BF16)])
    return x_out, h_next, sv


def _mlp_ple_bwd(tag, dx, sv, p_l, norm_mlp, w_up, w_down, norm_ple, w_ple, w_gate):
    dpp, dzg = _rowwise(f"{tag}_ple_bwd", _ple_bwd_fn, [dx, sv["pp"], sv["zg"]], [],
                        [(D_MODEL, BF16), (D_MODEL, BF16)])
    d_w_ple = _mm(f"{tag}_d_w_ple", p_l, dpp, "tn", out_dtypes=(BF16,))
    d_w_gate = _mm(f"{tag}_d_w_gate", sv["hg"], dzg, "tn", out_dtypes=(BF16,))
    dhg = _mm(f"{tag}_d_hg", dzg, w_gate, "nt")
    dx_mid, dx_mid_b, d_norm_ple = _rms_bwd(f"{tag}_ple_norm_bwd", sv["x_mid"], norm_ple, dx, [dhg])
    du = _mm(f"{tag}_d_u", dx_mid_b, w_down, "nt", out_dtypes=(BF16,), extras=(sv["a"],),
             epilogue=_relu2_bwd_epilogue)
    d_w_down = _mm(f"{tag}_d_w_down", sv["a"], dx_mid_b, "tn", out_dtypes=(BF16,))
    d_w_up = _mm(f"{tag}_d_w_up", sv["h"], du, "tn", out_dtypes=(BF16,))
    dh = _mm(f"{tag}_d_h", du, w_up, "nt")
    dx_in, dx_in_b, d_norm_mlp = _rms_bwd(f"{tag}_mlp_norm_bwd", sv["x_in"], norm_mlp, dx_mid, [dh])
    return dx_in, dx_in_b, dict(mlp_norm=d_norm_mlp, w_up=d_w_up, w_down=d_w_down, ple_norm=d_norm_ple,
                                w_ple=d_w_ple, w_ple_gate=d_w_gate)


class _NoHooks:
    fwd_cargo = (None,) * len(SWA_GROUPS)

    def weights_from(self, results):
        return {}

    def bwd_cargo(self, early_grads):
        return (None,) * len(SWA_GROUPS)


def _local_step(x, p, positions, target, small, big, hooks=_NoHooks()):
    S = x.shape[0]
    ct, st = _rope_lane_tables(positions)
    bd = _head_mean_matrix()

    h0 = _rms_fwd("l0_mix_norm", x, small["mix_norm"][0])
    attn, brought = [], []
    for g, (window, d) in enumerate(SWA_GROUPS):
        assert window // d == BAND and (S // d) % BAND == 0
        h0g = _to_classes(h0, d)
        ctg, stg = _to_classes(ct, d), _to_classes(st, d)
        w_g = big["attn_w_qkv"][:, g * 3 * A_WIDTH:(g + 1) * 3 * A_WIDTH]
        gq = jnp.tile(small["attn_q_gain"][0, g], A_HEADS).reshape(1, A_WIDTH)
        gk = jnp.tile(small["attn_k_gain"][0, g], A_HEADS).reshape(1, A_WIDTH)
        qkv = _mm(f"l0_qkv{g}", h0g, w_g, "nn")
        q, k, v = _rowwise(f"l0_qk_prep{g}", _qk_prep_fwd_fn, [qkv, ctg, stg], [gq, gk, bd], [(A_WIDTH, BF16)] * 3)
        o, lse, cargo_out = _attn_fwd(f"l0_attn{g}", q, k, v, (S // d) // BAND, cargo=hooks.fwd_cargo[g])
        brought.append(cargo_out)
        attn.append(dict(d=d, h0g=h0g, ct=ctg, st=stg, w=w_g, gq=gq, gk=gk, qkv=qkv, q=q, k=k, v=v, o=o, lse=lse,
                         o_tok=_from_classes(o, d), lse_tok=_from_classes(lse, d)))
    big = {**big, **hooks.weights_from(brought)}
    om = _rowwise("l0_merge", _merge_fwd_fn, [a["o_tok"] for a in attn] + [a["lse_tok"] for a in attn], [],
                  [(A_WIDTH, BF16)])
    x1, h1 = _mm("l0_attn_out", om, big["attn_w_o"], "nn", out_dtypes=(F32, BF16), extras=(x,),
                 rows=(small["mlp_norm"][0].reshape(1, -1),), epilogue=_residual_norm_epilogue)
    x3, h3, sv0 = _mlp_ple_fwd("l0", x1, h1, p[0], big["w_up"][0], big["w_down"][0], small["ple_norm"][0],
                               big["w_ple"][0], big["w_ple_gate"][0], next_gain=small["mix_norm"][1])

    N = S // CHUNK
    proj = _mm("l1_in", h3, big["dn_w_in"], "nn")
    pre = _conv_fwd("l1_conv", proj, small["dn_conv"])
    ab = proj[:, DN_AB0:DN_AB0 + DN_AB_PAD]
    lane_pad = DN_AB_PAD - DN_HEADS
    alog_row = jnp.pad(small["dn_a_log"][0], (0, lane_pad)).reshape(1, DN_AB_PAD)
    dt_row = jnp.pad(small["dn_dt_bias"][0], (0, lane_pad)).reshape(1, DN_AB_PAD)
    cum = _chunk_cumsum_matrix()
    gb = _rowwise("l1_gates", _gates_fwd_fn, [ab], [alog_row, dt_row, cum], [(DN_AB_PAD, F32)], tm=GATES_ROWS)
    g_t, b_t = gb[:, :DN_HEADS].T, gb[:, DN_HEADS:2 * DN_HEADS].T
    g_col, b_col = g_t.reshape(DN_HEADS, S, 1), b_t.reshape(DN_HEADS, S, 1)
    g_row = g_t.reshape(DN_HEADS, N, 1, CHUNK)
    ogain = small["dn_o_gain"][0].reshape(1, DN_HEAD_DIM)
    y, s_in_all, inverse_all = _chunk_fwd("l1_delta", pre, proj, g_col, b_col, g_row, ogain)
    x4, h4 = _mm("l1_dn_out", y, big["dn_w_o"], "nn", out_dtypes=(F32, BF16), extras=(x3,),
                 rows=(small["mlp_norm"][1].reshape(1, -1),), epilogue=_residual_norm_epilogue)
    x6, _, sv1 = _mlp_ple_fwd("l1", x4, h4, p[1], big["w_up"][1], big["w_down"][1], small["ple_norm"][1],
                              big["w_ple"][1], big["w_ple_gate"][1])

    dy, sq = _rowwise("loss", _loss_fn, [x6, target], [], [(D_MODEL, F32)], [(1, 128)])

    dx4, dx4_b, gl1 = _mlp_ple_bwd("l1", dy, sv1, p[1], small["mlp_norm"][1], big["w_up"][1], big["w_down"][1],
                            small["ple_norm"][1], big["w_ple"][1], big["w_ple_gate"][1])
    d_y = _mm("l1_d_y", dx4_b, big["dn_w_o"], "nt")
    d_dn_w_o = _mm("l1_d_w_o", y, dx4_b, "tn", out_dtypes=(BF16,))
    dpre, dz, dg_col, db_col, dg_row, d_ogain = _chunk_bwd(
        "l1_delta_bwd", pre, proj, g_col, b_col, g_row, ogain, s_in_all, inverse_all, d_y)
    dconv_in, d_conv_w = _conv_bwd("l1_conv_bwd", proj, dpre, small["dn_conv"])
    dg_t = dg_col.reshape(DN_HEADS, S) + dg_row.reshape(DN_HEADS, S)
    dgb = jnp.pad(jnp.concatenate([dg_t, db_col.reshape(DN_HEADS, S)], axis=0).T,
                  ((0, 0), (0, DN_AB_PAD - 2 * DN_HEADS)))
    dab, d_alog, d_dt = _rowwise("l1_gates_bwd", _gates_bwd_fn, [ab, dgb], [alog_row, dt_row, cum],
                                 [(DN_AB_PAD, F32)], [(1, DN_AB_PAD), (1, DN_AB_PAD)], tm=GATES_ROWS)
    dproj = jnp.concatenate([dz.astype(BF16), dconv_in, dab.astype(BF16)], axis=1)
    d_dn_w_in = _mm("l1_d_w_in", h3, dproj, "tn", out_dtypes=(BF16,))
    dh3 = _mm("l1_d_h", dproj, big["dn_w_in"], "nt")
    dx3, _, d_mix1 = _rms_bwd("l1_mix_norm_bwd", x3, small["mix_norm"][1], dx4, [dh3])

    dx1, dx1_b, gl0 = _mlp_ple_bwd("l0", dx3, sv0, p[0], small["mlp_norm"][0], big["w_up"][0], big["w_down"][0],
                            small["ple_norm"][0], big["w_ple"][0], big["w_ple_gate"][0])
    early = dict(
        dn_w_in=jnp.concatenate([d_dn_w_in[:, DN_QKV0:DN_AB0 + 2 * DN_HEADS], d_dn_w_in[:, :DN_WIDTH]], axis=1),
        dn_w_o=d_dn_w_o,
        w_up=jnp.stack([gl0["w_up"], gl1["w_up"]]),
        w_down=jnp.stack([gl0["w_down"], gl1["w_down"]]),
        w_ple=jnp.stack([gl0["w_ple"], gl1["w_ple"]]),
        w_ple_gate=jnp.stack([gl0["w_ple_gate"], gl1["w_ple_gate"]]))
    bwd_cargo = hooks.bwd_cargo(early)
    dom = _mm("l0_d_om", dx1_b, big["attn_w_o"], "nt")
    d_attn_w_o = _mm("l0_d_w_o", om, dx1_b, "tn", out_dtypes=(BF16,))
    merged = _rowwise("l0_merge_bwd", _merge_bwd_fn,
                      [a["o_tok"] for a in attn] + [a["lse_tok"] for a in attn] + [dom], [], [(A_WIDTH, F32)] * 6)
    dh0, d_w_qkv, d_gq, d_gk, brought_bwd = [], [], [], [], []
    for g, a in enumerate(attn):
        do_g, dl_g = _to_classes(merged[g], a["d"]), _to_classes(merged[3 + g], a["d"])
        dqn, dkn, dvn, cargo_out = _attn_bwd(f"l0_attn_bwd{g}", a["q"], a["k"], a["v"], a["o"], a["lse"], do_g, dl_g,
                                             (S // a["d"]) // BAND, cargo=bwd_cargo[g])
        brought_bwd.append(cargo_out)
        dqkv, dgq, dgk = _rowwise(f"l0_qk_prep_bwd{g}", _qk_prep_bwd_fn, [a["qkv"], a["ct"], a["st"], dqn, dkn, dvn],
                                  [a["gq"], a["gk"], bd], [(3 * A_WIDTH, BF16)], [(1, A_HEAD_DIM)] * 2)
        d_w_qkv.append(_mm(f"l0_d_w_qkv{g}", a["h0g"], dqkv, "tn", out_dtypes=(BF16,)))
        dh0.append(_from_classes(_mm(f"l0_d_h{g}", dqkv, a["w"], "nt"), a["d"]))
        d_gq.append(dgq)
        d_gk.append(dgk)
    grad_x, _, d_mix0 = _rms_bwd("l0_mix_norm_bwd", x, small["mix_norm"][0], dx1, dh0)

    grads = dict(
        mix_norm=jnp.concatenate([d_mix0, d_mix1], axis=0),
        attn_w_qkv=jnp.concatenate(d_w_qkv, axis=1),
        attn_q_gain=jnp.concatenate(d_gq, axis=0)[None],
        attn_k_gain=jnp.concatenate(d_gk, axis=0)[None],
        attn_w_o=d_attn_w_o,
        dn_conv=d_conv_w,
        dn_a_log=d_alog[:, :DN_HEADS],
        dn_dt_bias=d_dt[:, :DN_HEADS],
        dn_o_gain=d_ogain,
        mlp_norm=jnp.concatenate([gl0["mlp_norm"], gl1["mlp_norm"]], axis=0),
        ple_norm=jnp.concatenate([gl0["ple_norm"], gl1["ple_norm"]], axis=0),
        **early,
    )
    return sq, grad_x, grads, brought_bwd


def _chip_peer(x, y, c, t):
    return (jnp.bitwise_xor(x, t >> 1), jnp.bitwise_xor(y, t & 1), c)


def _place():
    x, y, c = lax.axis_index("x"), lax.axis_index("y"), lax.axis_index("c")
    return x, y, c, 2 * x + y, (x, y, 1 - c)


def _remote(src, dst, send_sem, recv_sem, to):
    return pltpu.make_async_remote_copy(src_ref=src, dst_ref=dst, send_sem=send_sem, recv_sem=recv_sem,
                                        device_id=to, device_id_type=MESH)


def _hbm_call(name, body, ins, out_shape, scratch_shapes):
    any_spec = pl.BlockSpec(memory_space=pl.ANY)
    return pl.pallas_call(body, name=name, out_shape=out_shape, in_specs=[any_spec] * len(ins),
                          out_specs=[any_spec] * len(out_shape), scratch_shapes=scratch_shapes)(*ins)


def _half(n0, which):
    return pl.ds(which * (n0 // 2), n0 // 2)


class _Exchange:
    def __init__(self, ins, out_shape, scratch, start, finish):
        self.ins, self.out_shape, self.scratch, self.start, self.finish = ins, out_shape, scratch, start, finish


def _run_exchange(name, ex):
    n_in, n_out = len(ex.ins), len(ex.out_shape)

    def body(*refs):
        ins, outs, sems = refs[:n_in], refs[n_in:n_in + n_out], refs[n_in + n_out:]
        ex.start(ins, outs, sems)
        ex.finish(ins, outs, sems)

    return _hbm_call(name, body, ex.ins, ex.out_shape, ex.scratch)


def _gather_exchange(shards):
    T = len(shards)

    def copies(ins, outs, sems):
        send, recv = sems
        x, y, c, q, sibling = _place()
        over_ici, over_d2d, arriving = [], [], []
        for i in range(T):
            mine, theirs = _half(ins[i].shape[0], c), _half(ins[i].shape[0], 1 - c)
            for t in range(1, N_CHIPS):
                peer = _chip_peer(x, y, c, t)
                landed = outs[i].at[jnp.bitwise_xor(q, t), mine]
                passed = outs[i].at[jnp.bitwise_xor(q, t), theirs]
                over_ici.append((_remote(ins[i].at[mine], outs[i].at[q, mine], send.at[i, t - 1], recv.at[i, t - 1], peer),
                                 _remote(landed, landed, send.at[i, t - 1], recv.at[i, t - 1], peer)))
                over_d2d.append(_remote(landed, landed, send.at[i, 2 + t], recv.at[i, 2 + t], sibling))
                arriving.append(_remote(passed, passed, send.at[i, 2 + t], recv.at[i, 2 + t], sibling))
        return over_ici, over_d2d, arriving

    def start(ins, outs, sems):
        for mine, _ in copies(ins, outs, sems)[0]:
            mine.start()

    def finish(ins, outs, sems):
        over_ici, over_d2d, arriving = copies(ins, outs, sems)
        for (_, landing), forward in zip(over_ici, over_d2d):
            landing.wait_recv()
            forward.start()
        for cp in arriving:
            cp.wait_recv()
        for (mine, _), forward in zip(over_ici, over_d2d):
            mine.wait_send()
            forward.wait_send()

    n_rel = 2 * (N_CHIPS - 1)
    return _Exchange(list(shards), [jax.ShapeDtypeStruct((N_CHIPS,) + s.shape, s.dtype) for s in shards],
                     [pltpu.SemaphoreType.DMA((T, n_rel)), pltpu.SemaphoreType.DMA((T, n_rel))], start, finish)


def _scatter_exchange(stacks):
    T = len(stacks)

    def copies(ins, outs, sems):
        send, recv = sems
        x, y, c, q, sibling = _place()
        return [_remote(ins[i].at[jnp.bitwise_xor(q, t)], outs[i].at[t - 1], send.at[i, t - 1], recv.at[i, t - 1],
                        _chip_peer(x, y, c, t)) for i in range(T) for t in range(1, N_CHIPS)]

    def start(ins, outs, sems):
        for cp in copies(ins, outs, sems):
            cp.start()

    def finish(ins, outs, sems):
        for cp in copies(ins, outs, sems):
            cp.wait()

    return _Exchange(list(stacks), [jax.ShapeDtypeStruct((N_CHIPS - 1,) + s.shape[1:], s.dtype) for s in stacks],
                     [pltpu.SemaphoreType.DMA((T, N_CHIPS - 1)), pltpu.SemaphoreType.DMA((T, N_CHIPS - 1))],
                     start, finish)


def _other_half_from_sibling(name, stacks):
    T = len(stacks)

    def body(*refs):
        ins, outs = refs[:T], refs[T:2 * T]
        send, recv = refs[2 * T:]
        x, y, c, q, sibling = _place()
        copies = []
        for i in range(T):
            rc = _remote(ins[i].at[:, _half(ins[i].shape[1], 1 - c)], outs[i], send.at[i], recv.at[i], sibling)
            rc.start()
            copies.append(rc)
        for cp in copies:
            cp.wait()

    return _hbm_call(name, body, stacks,
                     [jax.ShapeDtypeStruct((s.shape[0], s.shape[1] // 2) + s.shape[2:], s.dtype) for s in stacks],
                     [pltpu.SemaphoreType.DMA((T,)), pltpu.SemaphoreType.DMA((T,))])


def _swap_with_sibling(name, arrays):
    T = len(arrays)

    def body(*refs):
        ins, outs = refs[:T], refs[T:2 * T]
        send, recv = refs[2 * T:]
        x, y, c, q, sibling = _place()
        copies = []
        for i in range(T):
            rc = _remote(ins[i], outs[i], send.at[i], recv.at[i], sibling)
            rc.start()
            copies.append(rc)
        for cp in copies:
            cp.wait()

    return _hbm_call(name, body, arrays, [jax.ShapeDtypeStruct(a.shape, a.dtype) for a in arrays],
                     [pltpu.SemaphoreType.DMA((T,)), pltpu.SemaphoreType.DMA((T,))])


def _gather_from_all(name, block):
    R, C = block.shape

    def body(src, out, send_sems, recv_sems):
        x, y, c = lax.axis_index("x"), lax.axis_index("y"), lax.axis_index("c")
        me = 4 * x + 2 * y + c
        out[me] = src[...]
        copies = []
        for r in range(1, N_DEV):
            peer = (jnp.bitwise_xor(x, r >> 2), jnp.bitwise_xor(y, (r >> 1) & 1), jnp.bitwise_xor(c, r & 1))
            cp = pltpu.make_async_remote_copy(src_ref=src, dst_ref=out.at[me], send_sem=send_sems.at[r - 1],
                                              recv_sem=recv_sems.at[r - 1], device_id=peer, device_id_type=MESH)
            cp.start()
            copies.append(cp)
        for cp in copies:
            cp.wait()

    return pl.pallas_call(
        body, name=name, out_shape=jax.ShapeDtypeStruct((N_DEV, R, C), block.dtype),
        in_specs=[pl.BlockSpec(memory_space=pltpu.VMEM)], out_specs=pl.BlockSpec(memory_space=pltpu.VMEM),
        scratch_shapes=[pltpu.SemaphoreType.DMA((N_DEV - 1,)), pltpu.SemaphoreType.DMA((N_DEV - 1,))],
    )(block)


def _view(a):
    return a[0] if a.shape[0] == 1 else a


def _view_axis(a, axis):
    return axis - 1 if a.shape[0] == 1 else axis


def _rows(a):
    return a.reshape(-1, a.shape[-1])


def _elementwise(name, fn, ins, out_dtypes, tm):
    specs = []
    for a in ins:
        a, row0 = a if isinstance(a, tuple) else (a, 0)
        specs.append((_rows(a), a.shape[-1], 0, row0))
    shape = ins[0][0].shape if isinstance(ins[0], tuple) else ins[0].shape
    outs = _rowwise(name, fn, specs, [], [(shape[-1], dt) for dt in out_dtypes], tm=tm, n_rows=math.prod(shape[:-1]))
    return outs.reshape(shape) if len(out_dtypes) == 1 else [o.reshape(shape) for o in outs]


SMALL_ROWS = 8
CONV_ROWS = CONV_WIDTH * DN_QKV // D_MODEL
SMALL_GRAD_ROWS = 24


def _pack_small(vals, conv=None):
    tail = jnp.concatenate([vals["attn_q_gain"].reshape(-1), vals["attn_k_gain"].reshape(-1),
                            vals["dn_a_log"].reshape(-1), vals["dn_dt_bias"].reshape(-1),
                            vals["dn_o_gain"].reshape(-1)])
    tail = jnp.pad(tail, (0, D_MODEL - tail.shape[0])).reshape(1, D_MODEL)
    rows = [vals["mix_norm"], vals["mlp_norm"], vals["ple_norm"], tail, jnp.zeros((1, D_MODEL), F32)]
    if conv is not None:
        rows += [conv.reshape(CONV_ROWS, D_MODEL),
                 jnp.zeros((SMALL_GRAD_ROWS - SMALL_ROWS - CONV_ROWS, D_MODEL), F32)]
    return jnp.concatenate(rows, axis=0)


def _unpack_small(block):
    nq = 3 * A_HEAD_DIM
    t = block[6]
    return dict(
        mix_norm=block[0:2], mlp_norm=block[2:4], ple_norm=block[4:6],
        attn_q_gain=t[:nq].reshape(1, 3, A_HEAD_DIM), attn_k_gain=t[nq:2 * nq].reshape(1, 3, A_HEAD_DIM),
        dn_a_log=t[2 * nq:2 * nq + DN_HEADS].reshape(1, DN_HEADS),
        dn_dt_bias=t[2 * nq + DN_HEADS:2 * nq + 2 * DN_HEADS].reshape(1, DN_HEADS),
        dn_o_gain=t[2 * nq + 2 * DN_HEADS:2 * nq + 2 * DN_HEADS + DN_HEAD_DIM].reshape(1, DN_HEAD_DIM))


def kernel(x, p, positions, mix_norm, attn_w_qkv, attn_q_gain, attn_k_gain, attn_w_o, dn_w_in, dn_conv, dn_a_log, dn_dt_bias, dn_o_gain, dn_w_o, mlp_norm, w_up, w_down, ple_norm, w_ple, w_ple_gate, loss_target, m_mix_norm, m_attn_w_qkv, m_attn_q_gain, m_attn_k_gain, m_attn_w_o, m_dn_w_in, m_dn_conv, m_dn_a_log, m_dn_dt_bias, m_dn_o_gain, m_dn_w_o, m_mlp_norm, m_w_up, m_w_down, m_ple_norm, m_w_ple, m_w_ple_gate, v_mix_norm, v_attn_w_qkv, v_attn_q_gain, v_attn_k_gain, v_attn_w_o, v_dn_w_in, v_dn_conv, v_dn_a_log, v_dn_dt_bias, v_dn_o_gain, v_dn_w_o, v_mlp_norm, v_w_up, v_w_down, v_ple_norm, v_w_ple, v_w_ple_gate):
    given = dict(locals())
    w = {n: given[n] for n in WEIGHTS}
    m = {n: given["m_" + n] for n in WEIGHTS}
    v = {n: given["v_" + n] for n in WEIGHTS}
    kinds = ("grad", "delta", "new_m", "new_v")
    axes = {n: _view_axis(w[n], axis) for n, axis in SHARDED if n != "dn_conv"}
    chip = 2 * lax.axis_index("x") + lax.axis_index("y")
    core = lax.axis_index("c")
    shards = {n: _view(w[n]).astype(BF16) for n in axes}

    def whole(n, slots):
        return jnp.concatenate([jnp.where(chip == q, shards[n], slots[q]) for q in range(N_CHIPS)], axis=axes[n])

    def chip_sums_of(tag, grads_of):
        names = list(grads_of)
        stacks = [jnp.stack(jnp.split(grads_of[n], N_CHIPS, axis=axes[n])) for n in names]
        mine = [lax.dynamic_slice_in_dim(s, core * (s.shape[1] // 2), s.shape[1] // 2, axis=1) for s in stacks]
        theirs = _other_half_from_sibling(f"split_core_grads_{tag}", stacks)
        return {n: _elementwise(f"add_core_{n}", lambda a, b: a.astype(F32) + b.astype(F32), [a, b], [BF16], 128)
                for n, a, b in zip(names, mine, theirs)}

    class Hooks:
        fwd_cargo = [_gather_exchange([shards[n] for n in group]) for group in CARGO_GROUPS]
        chip_sums = {}

        def weights_from(self, results):
            full = {n: whole(n, slots) for group, res in zip(CARGO_GROUPS, results) for n, slots in zip(group, res)}
            w_in, n_ab = full["dn_w_in"], 2 * DN_HEADS
            full["dn_w_in"] = jnp.concatenate([w_in[:, DN_QKV + n_ab:], w_in[:, :DN_QKV + n_ab],
                                               jnp.zeros((D_MODEL, DN_AB_PAD - n_ab), BF16)], axis=1)
            return full

        def bwd_cargo(self, early_grads):
            self.chip_sums.update(chip_sums_of("early", early_grads))
            return [_scatter_exchange([self.chip_sums[n] for n in group]) for group in CARGO_GROUPS]

    hooks = Hooks()
    gathered = _run_exchange("gather_attn_weights", _gather_exchange([shards[n] for n in ATTN_MATRICES]))
    big = {n: whole(n, slots) for n, slots in zip(ATTN_MATRICES, gathered)}
    conv_block = jnp.pad(w["dn_conv"].reshape(-1), (0, SMALL_ROWS * D_MODEL - w["dn_conv"].size))
    conv_all = _gather_from_all("gather_conv", conv_block.reshape(SMALL_ROWS, D_MODEL))
    conv_all = conv_all.reshape(N_CHIPS, 2, -1)[:, 0, :w["dn_conv"].size]
    conv_full = jnp.concatenate([conv_all[q].reshape(CONV_WIDTH, -1) for q in range(N_CHIPS)], axis=1)
    small = {n: w[n] for n in REPLICATED}
    small["dn_conv"] = conv_full

    sq, grad_x, grads, brought = _local_step(x[0], p[:, 0], positions[0], loss_target[0], small, big, hooks)
    loss = lax.psum(0.5 * sq[0, 0] / D_MODEL, ("x", "y", "c"))
    out = {}

    landed = {n: r for group, res in zip(CARGO_GROUPS, brought) for n, r in zip(group, res)}
    hooks.chip_sums.update(chip_sums_of("attn", {n: grads[n] for n in ATTN_MATRICES}))
    late = _run_exchange("scatter_attn_grads", _scatter_exchange([hooks.chip_sums[n] for n in ATTN_MATRICES]))
    landed.update(zip(ATTN_MATRICES, late))
    half_sums = []
    for n in axes:
        o = lax.dynamic_index_in_dim(hooks.chip_sums[n], chip, axis=0, keepdims=False)
        per = math.prod(o.shape[:-1])
        r = landed[n]
        half_sums.append(_elementwise(
            f"add_chips_{n}", lambda a, b, c, d: ((a.astype(F32) + b.astype(F32)) + c.astype(F32)) + d.astype(F32),
            [o, (r, 0), (r, per), (r, 2 * per)], [F32], 128))
    other_halves = _swap_with_sibling("join_core_sums", half_sums)
    for n, a, b in zip(axes, half_sums, other_halves):
        g = jnp.where(core == 0, jnp.concatenate([a, b], axis=0), jnp.concatenate([b, a], axis=0))
        shp = w[n].shape
        res = _elementwise(f"adamw_{n}", lambda g, w_, m_, v_: (g,) + _adamw(w_, g, m_, v_),
                           [g.reshape(shp), w[n], m[n], v[n]], [F32] * 4, 256)
        for kind, arr in zip(kinds, res):
            out[kind + "_" + n] = arr.reshape(shp)

    slots = _gather_from_all("gather_small_grads", _pack_small(grads, grads["dn_conv"]))

    def small_body(s_ref, w_ref, m_ref, v_ref, sum_out, g_out, d_out, m_out, v_out):
        total = s_ref[0]
        for d in range(1, N_DEV):
            total = total + s_ref[d]
        sum_out[...] = total
        g = total[:SMALL_ROWS]
        for o, r in zip((g_out, d_out, m_out, v_out), (g,) + _adamw(w_ref[...], g, m_ref[...], v_ref[...])):
            o[...] = r

    res = pl.pallas_call(small_body, name="adamw_replicated",
                         out_shape=[jax.ShapeDtypeStruct((SMALL_GRAD_ROWS, D_MODEL), F32)]
                         + [jax.ShapeDtypeStruct((SMALL_ROWS, D_MODEL), F32)] * 4)(
        slots, _pack_small(w), _pack_small(m), _pack_small(v))
    for kind, block in zip(kinds, res[1:]):
        for n, arr in _unpack_small(block).items():
            out[kind + "_" + n] = arr
    conv_sum = res[0][SMALL_ROWS:SMALL_ROWS + CONV_ROWS].reshape(CONV_WIDTH, DN_QKV)
    cols = DN_QKV // N_CHIPS
    chip = 2 * lax.axis_index("x") + lax.axis_index("y")
    conv_mine = lax.dynamic_slice_in_dim(conv_sum, chip * cols, cols, axis=1)
    res = _elementwise("adamw_dn_conv", lambda g, w_, m_, v_: (g,) + _adamw(w_, g, m_, v_),
                       [conv_mine, w["dn_conv"][0], m["dn_conv"][0], v["dn_conv"][0]], [F32] * 4, CONV_WIDTH)
    for kind, arr in zip(kinds, res):
        out[kind + "_dn_conv"] = arr[None]

    return (loss, grad_x[None],
            *[out["grad_" + n] for n in WEIGHTS], *[out["delta_" + n] for n in WEIGHTS],
            *[out["new_m_" + n] for n in WEIGHTS], *[out["new_v_" + n] for n in WEIGHTS])
```

```python
import functools
import math

import jax
import jax.numpy as jnp
from jax import lax
from jax.experimental import pallas as pl
from jax.experimental.pallas import tpu as pltpu

F32 = jnp.float32
BF16 = jnp.bfloat16
HIGHEST = lax.Precision.HIGHEST

D_MODEL = 1024
EPS = 1e-6
SWA_GROUPS = ((128, 1), (512, 4), (2048, 16))
A_HEADS = 8
A_HEAD_DIM = 64
A_WIDTH = A_HEADS * A_HEAD_DIM
ROPE_DIM = A_HEAD_DIM // 4
ROPE_THETA = 500000.0
BAND = 128
DN_HEADS = 8
DN_HEAD_DIM = 128
DN_WIDTH = DN_HEADS * DN_HEAD_DIM
DN_QKV = 3 * DN_WIDTH
DN_AB_PAD = 128
DN_IN_PAD = DN_WIDTH + DN_QKV + DN_AB_PAD
DN_QKV0 = DN_WIDTH
DN_AB0 = DN_WIDTH + DN_QKV
DN_HB = 8
CONV_WIDTH = 4
CHUNK = 64
PLE_DIM = 256
D_FF = 4 * D_MODEL

ADAM_LR = 0.001
ADAM_B1 = 0.9
ADAM_B2 = 0.999
ADAM_EPS = 1e-08
ADAM_WD = 0.01
ADAM_STEP = 10

N_CHIPS = 4
N_DEV = 8
VMEM_LIMIT = 48 * 1024 * 1024
MESH = pl.DeviceIdType.MESH

SHARDED = (
    ("attn_w_qkv", 2), ("attn_w_o", 2), ("dn_w_in", 2), ("dn_conv", 2), ("dn_w_o", 1),
    ("w_up", 2), ("w_down", 1), ("w_ple", 2), ("w_ple_gate", 1))
ATTN_MATRICES = ("attn_w_qkv", "attn_w_o")
CARGO_GROUPS = (("w_up",), ("w_down",), ("dn_w_in", "dn_w_o", "w_ple", "w_ple_gate"))
REPLICATED = ("mix_norm", "attn_q_gain", "attn_k_gain", "dn_a_log", "dn_dt_bias", "dn_o_gain",
              "mlp_norm", "ple_norm")
WEIGHTS = ("mix_norm", "attn_w_qkv", "attn_q_gain", "attn_k_gain", "attn_w_o", "dn_w_in", "dn_conv",
           "dn_a_log", "dn_dt_bias", "dn_o_gain", "dn_w_o", "mlp_norm", "w_up", "w_down", "ple_norm",
           "w_ple", "w_ple_gate")


def _cparams(sem=None):
    return pltpu.CompilerParams(dimension_semantics=sem, vmem_limit_bytes=VMEM_LIMIT)


def _pick(n, cap, quantum=128):
    best = None
    for t in range(quantum, min(n, cap) + 1, quantum):
        if n % t == 0:
            best = t
    return n if best is None else best


_DIMS = {"nn": ((1,), (0,)), "nt": ((1,), (1,)), "tn": ((0,), (0,))}


def _mm(name, a, b, mode, out_dtypes=(F32,), extras=(), epilogue=None, rows=(), n_sums=0, tm_cap=1024):
    if mode == "nn":
        (M, K), (K2, N) = a.shape, b.shape
    elif mode == "nt":
        (M, K), (N, K2) = a.shape, b.shape
    else:
        (K, M), (K2, N) = a.shape, b.shape
    assert K == K2, (name, a.shape, b.shape)
    tm, tn, tk = _pick(M, tm_cap), _pick(N, 1536), _pick(K, 1024 if mode == "tn" else 1536)
    nk = K // tk
    assert n_sums == 0 or tn == N, name
    if mode == "nn":
        a_spec = pl.BlockSpec((tm, tk), lambda i, j, k: (i, k))
        b_spec = pl.BlockSpec((tk, tn), lambda i, j, k: (k, j))
    elif mode == "nt":
        a_spec = pl.BlockSpec((tm, tk), lambda i, j, k: (i, k))
        b_spec = pl.BlockSpec((tn, tk), lambda i, j, k: (j, k))
    else:
        a_spec = pl.BlockSpec((tk, tm), lambda i, j, k: (k, i))
        b_spec = pl.BlockSpec((tk, tn), lambda i, j, k: (k, j))
    o_spec = pl.BlockSpec((tm, tn), lambda i, j, k: (i, j))
    r_spec = pl.BlockSpec((1, tn), lambda i, j, k: (0, j))
    n_extra, n_out = len(extras) + len(rows), len(out_dtypes)
    dims = (_DIMS[mode], ((), ()))

    def body(a_ref, b_ref, *rest):
        extra_refs, out_refs = rest[:n_extra], rest[n_extra:n_extra + n_out]
        sum_refs = rest[n_extra + n_out:n_extra + n_out + n_sums]
        i, k = pl.program_id(0), pl.program_id(2)
        part = lax.dot_general(a_ref[...].astype(BF16), b_ref[...].astype(BF16), dims, preferred_element_type=F32)

        def finish(total):
            vals = (total,) if epilogue is None else epilogue(total, *[e[...] for e in extra_refs])
            for o, v in zip(out_refs, vals[:n_out]):
                o[...] = v.astype(o.dtype)
            for s, v in zip(sum_refs, vals[n_out:]):
                @pl.when(i == 0)
                def _():
                    s[...] = v

                @pl.when(i > 0)
                def _():
                    s[...] += v

        if nk == 1:
            finish(part)
            return
        acc = rest[-1]

        @pl.when(k == 0)
        def _():
            acc[...] = part

        @pl.when(jnp.logical_and(k > 0, k < nk - 1))
        def _():
            acc[...] += part

        @pl.when(k == nk - 1)
        def _():
            finish(acc[...] + part)

    outs = pl.pallas_call(
        body, name=name, grid=(M // tm, N // tn, nk),
        in_specs=[a_spec, b_spec] + [o_spec] * len(extras) + [r_spec] * len(rows),
        out_specs=[o_spec] * n_out + [r_spec] * n_sums,
        out_shape=[jax.ShapeDtypeStruct((M, N), dt) for dt in out_dtypes]
        + [jax.ShapeDtypeStruct((1, N), F32)] * n_sums,
        scratch_shapes=[pltpu.VMEM((tm, tn), F32)] if nk > 1 else [],
        compiler_params=_cparams(("arbitrary" if n_sums else "parallel", "parallel", "arbitrary")),
    )(a, b, *extras, *rows)
    return outs[0] if n_out + n_sums == 1 else outs


def _rowwise(name, fn, rows, bcast, row_outs, acc_outs=(), tm=256, n_rows=None):
    rows = [r if isinstance(r, tuple) else (r, r.shape[1], 0) for r in rows]
    rows = [r if len(r) == 4 else r + (0,) for r in rows]
    S = rows[0][0].shape[0] if n_rows is None else n_rows
    tm = min(tm, S)
    assert S % tm == 0 and all(r[3] % tm == 0 for r in rows), (name, S, tm)
    n_row, n_bc, n_ro, n_acc = len(rows), len(bcast), len(row_outs), len(acc_outs)
    in_specs = [pl.BlockSpec((tm, w), functools.partial(lambda i, cb, rb: (i + rb, cb), cb=cb, rb=r0 // tm))
                for _, w, cb, r0 in rows]
    in_specs += [pl.BlockSpec(b.shape, lambda i: (0, 0)) for b in bcast]
    out_specs = [pl.BlockSpec((tm, c), lambda i: (i, 0)) for c, _ in row_outs]
    out_specs += [pl.BlockSpec(s, lambda i: (0, 0)) for s in acc_outs]
    out_shape = [jax.ShapeDtypeStruct((S, c), dt) for c, dt in row_outs]
    out_shape += [jax.ShapeDtypeStruct(s, F32) for s in acc_outs]

    def body(*refs):
        ins = [r[...] for r in refs[:n_row + n_bc]]
        outs = refs[n_row + n_bc:]
        vals = fn(*ins)
        if not isinstance(vals, (tuple, list)):
            vals = (vals,)
        for o, v in zip(outs[:n_ro], vals[:n_ro]):
            o[...] = v.astype(o.dtype)
        if n_acc:
            @pl.when(pl.program_id(0) == 0)
            def _():
                for o in outs[n_ro:]:
                    o[...] = jnp.zeros_like(o)
            for o, v in zip(outs[n_ro:], vals[n_ro:]):
                o[...] += v

    outs = pl.pallas_call(
        body, name=name, grid=(S // tm,), in_specs=in_specs, out_specs=out_specs, out_shape=out_shape,
        compiler_params=_cparams(("arbitrary",) if n_acc else ("parallel",)),
    )(*[r[0] for r in rows], *bcast)
    return outs[0] if len(outs) == 1 else outs


def _sigmoid(x):
    return 1.0 / (1.0 + jnp.exp(-x))


def _silu(x):
    return x * _sigmoid(x)


def _softplus(x):
    return jnp.maximum(x, 0.0) + jnp.log(1.0 + jnp.exp(-jnp.abs(x)))


def _rms_fwd_fn(x, g):
    r = lax.rsqrt(jnp.mean(x * x, axis=-1, keepdims=True) + EPS)
    return (x * r) * g


def _rms_bwd_fn(x, dres, *rest):
    dh, g = sum(rest[:-1]), rest[-1]
    r = lax.rsqrt(jnp.mean(x * x, axis=-1, keepdims=True) + EPS)
    xh = x * r
    dxh = dh * g
    dx = dres + r * (dxh - xh * jnp.mean(dxh * xh, axis=-1, keepdims=True))
    return dx, dx, jnp.sum(dh * xh, axis=0, keepdims=True)


def _rms_fwd(name, x, gain):
    return _rowwise(name, _rms_fwd_fn, [x], [gain.reshape(1, -1)], [(x.shape[1], BF16)])


def _rms_bwd(name, x, gain, dres, dhs):
    return _rowwise(name, _rms_bwd_fn, [x, dres] + list(dhs), [gain.reshape(1, -1)],
                    [(x.shape[1], F32), (x.shape[1], BF16)], [(1, x.shape[1])])


def _relu2_epilogue(acc):
    r = jnp.maximum(acc, 0.0)
    return (r * r,)


def _relu2_bwd_epilogue(acc, a):
    return (acc * (2.0 * jnp.sqrt(a.astype(F32))),)


def _ple_fwd_fn(x, pp, zg):
    return x + pp * _sigmoid(zg)


def _ple_norm_fwd_fn(x, pp, zg, gain):
    out = _ple_fwd_fn(x, pp, zg)
    return out, _rms_fwd_fn(out, gain)


def _ple_bwd_fn(dx, pp, zg):
    gate = _sigmoid(zg)
    return dx * gate, dx * pp * gate * (1.0 - gate)


def _adamw(w, g, m, v):
    m = ADAM_B1 * m + (1.0 - ADAM_B1) * g
    v = ADAM_B2 * v + (1.0 - ADAM_B2) * jnp.square(g)
    m_hat = m / (1.0 - ADAM_B1 ** ADAM_STEP)
    v_hat = v / (1.0 - ADAM_B2 ** ADAM_STEP)
    delta = -ADAM_LR * (m_hat / (jnp.sqrt(v_hat) + ADAM_EPS) + ADAM_WD * w)
    return delta, m, v


def _lane_take(x, offset):
    n = x.shape[-1]
    return pltpu.roll(x, (-offset) % n, 1)


def _head_lane(shape):
    return lax.broadcasted_iota(jnp.int32, shape, 1) % A_HEAD_DIM


def _rope_partner(x):
    lane = _head_lane(x.shape)
    return jnp.where(lane < ROPE_DIM // 2, _lane_take(x, ROPE_DIM // 2),
                     jnp.where(lane < ROPE_DIM, _lane_take(x, -(ROPE_DIM // 2)), 0.0))


def _head_mean(x, bd):
    hi = x.astype(BF16)
    lo = (x - hi.astype(F32)).astype(BF16)
    b = bd.astype(BF16)
    return jnp.dot(hi, b, preferred_element_type=F32) + jnp.dot(lo, b, preferred_element_type=F32)


def _fold_heads(row):
    out = row[:, :A_HEAD_DIM]
    for h in range(1, A_HEADS):
        out = out + row[:, h * A_HEAD_DIM:(h + 1) * A_HEAD_DIM]
    return out


def _all_heads(t):
    return jnp.concatenate([t] * (A_WIDTH // t.shape[1]), axis=1)


def _qk_prep_fwd_fn(qkv, ct, st, gq, gk, bd):
    ct, st = _all_heads(ct), _all_heads(st)

    def one(t, g):
        n = t * lax.rsqrt(_head_mean(t * t, bd) + EPS) * g
        return n * ct + _rope_partner(n) * st
    q, k, v = qkv[:, :A_WIDTH], qkv[:, A_WIDTH:2 * A_WIDTH], qkv[:, 2 * A_WIDTH:]
    return one(q, gq), one(k, gk), v


def _qk_prep_bwd_fn(qkv, ct, st, dq, dk, dv, gq, gk, bd):
    ct, st = _all_heads(ct), _all_heads(st)

    def one(t, g, dy):
        r = lax.rsqrt(_head_mean(t * t, bd) + EPS)
        nh = t * r
        dn = dy * ct + _rope_partner(dy * st)
        dg = jnp.sum(dn * nh, axis=0, keepdims=True)
        dnh = dn * g
        return r * (dnh - nh * _head_mean(dnh * nh, bd)), _fold_heads(dg)
    q, k = qkv[:, :A_WIDTH], qkv[:, A_WIDTH:2 * A_WIDTH]
    dq_raw, dgq = one(q, gq, dq)
    dk_raw, dgk = one(k, gk, dk)
    return jnp.concatenate([dq_raw, dk_raw, dv], axis=1), dgq, dgk


_BATCH_DIMS = {"nn": ((2,), (1,)), "nt": ((2,), (2,)), "tn": ((1,), (1,))}


def _bdot(a, b, mode, precision=None):
    return lax.dot_general(a, b, (_BATCH_DIMS[mode], ((0,), (0,))), precision=precision,
                           preferred_element_type=F32)


def _attn_cols(h):
    return slice(h * A_HEAD_DIM, (h + 1) * A_HEAD_DIM)


def _attn_heads(ref):
    return jnp.stack([ref[:, _attn_cols(h)] for h in range(A_HEADS)])


def _band_masks():
    qi = lax.broadcasted_iota(jnp.int32, (BAND, BAND), 0)
    kj = lax.broadcasted_iota(jnp.int32, (BAND, BAND), 1)
    return kj <= qi, kj >= qi


def _attn_fwd(name, q, k, v, blocks_per_class, cargo=None):
    S = q.shape[0]
    nblk = S // BAND
    scale = A_HEAD_DIM ** -0.5

    def body(q_ref, kp_ref, kc_ref, vp_ref, vc_ref, o_ref, l_ref):
        i = pl.program_id(0)
        has_prev = (i % blocks_per_class) != 0
        m_cur, m_prev = _band_masks()
        m_prev = jnp.logical_and(m_prev, has_prev)
        q, kc, kp, vc, vp = (_attn_heads(r) for r in (q_ref, kc_ref, kp_ref, vc_ref, vp_ref))
        s_c = jnp.where(m_cur[None], _bdot(q, kc, "nt") * scale, -jnp.inf)
        s_p = jnp.where(m_prev[None], _bdot(q, kp, "nt") * scale, -jnp.inf)
        m = jnp.maximum(jnp.max(s_c, axis=-1, keepdims=True), jnp.max(s_p, axis=-1, keepdims=True))
        e_c, e_p = jnp.exp(s_c - m), jnp.exp(s_p - m)
        l = jnp.sum(e_c, axis=-1, keepdims=True) + jnp.sum(e_p, axis=-1, keepdims=True)
        o = _bdot((e_c / l).astype(BF16), vc, "nn") + _bdot((e_p / l).astype(BF16), vp, "nn")
        lse = m + jnp.log(l)
        for h in range(A_HEADS):
            o_ref[:, _attn_cols(h)] = o[h]
            l_ref[:, _attn_cols(h)] = jnp.broadcast_to(lse[h], (BAND, A_HEAD_DIM))

    cur = pl.BlockSpec((BAND, A_WIDTH), lambda i: (i, 0))
    prev = pl.BlockSpec((BAND, A_WIDTH), lambda i: (jnp.maximum(i - 1, 0), 0))
    body, c_in_specs, c_out_specs, c_out_shape, c_scratch, c_ins = _carry(cargo, 5, 2, 0, body, nblk)
    outs = pl.pallas_call(
        body, name=name, grid=(nblk,), in_specs=[cur, prev, cur, prev, cur] + c_in_specs,
        out_specs=[cur, cur] + c_out_specs,
        out_shape=[jax.ShapeDtypeStruct((S, A_WIDTH), F32)] * 2 + c_out_shape, scratch_shapes=c_scratch,
        compiler_params=_cparams(("arbitrary",)),
    )(q, k, k, v, v, *c_ins)
    return outs[0], outs[1], outs[2:]


def _carry(cargo, n_in, n_out, n_scratch, body, steps):
    if cargo is None:
        return body, [], [], [], [], []
    n_ci, n_co = len(cargo.ins), len(cargo.out_shape)

    def carrying(*refs):
        refs = list(refs)
        ins, refs = refs[:n_in], refs[n_in:]
        c_ins, refs = refs[:n_ci], refs[n_ci:]
        outs, refs = refs[:n_out], refs[n_out:]
        c_outs, refs = refs[:n_co], refs[n_co:]
        scratch, sems = refs[:n_scratch], refs[n_scratch:]

        @pl.when(pl.program_id(0) == 0)
        def _():
            cargo.start(c_ins, c_outs, sems)

        body(*ins, *outs, *scratch)

        @pl.when(pl.program_id(0) == steps - 1)
        def _():
            cargo.finish(c_ins, c_outs, sems)

    any_spec = pl.BlockSpec(memory_space=pl.ANY)
    return carrying, [any_spec] * n_ci, [any_spec] * n_co, list(cargo.out_shape), list(cargo.scratch), list(cargo.ins)


def _attn_bwd(name, q, k, v, o, lse, do, dlse, blocks_per_class, cargo=None):
    S = q.shape[0]
    nblk = S // BAND
    scale = A_HEAD_DIM ** -0.5

    def body(q_ref, kp_ref, kc_ref, vp_ref, vc_ref, o_ref, l_ref, do_ref, dl_ref,
             dq_ref, dk_ref, dv_ref, ck, cv):
        i = pl.program_id(0)

        @pl.when(i == 0)
        def _():
            ck[...] = jnp.zeros_like(ck)
            cv[...] = jnp.zeros_like(cv)

        @pl.when(i == nblk)
        def _():
            dk_ref[...] = ck[...]
            dv_ref[...] = cv[...]

        @pl.when(i < nblk)
        def _():
            has_prev = (i % blocks_per_class) != 0
            m_cur, m_prev = _band_masks()
            m_prev = jnp.logical_and(m_prev, has_prev)
            q, kc, kp, vc, vp = (_attn_heads(r) for r in (q_ref, kc_ref, kp_ref, vc_ref, vp_ref))
            do, o, dl = _attn_heads(do_ref), _attn_heads(o_ref), _attn_heads(dl_ref)
            lse = jnp.max(_attn_heads(l_ref), axis=-1, keepdims=True)
            p_c = jnp.where(m_cur[None], jnp.exp(_bdot(q, kc, "nt") * scale - lse), 0.0)
            p_p = jnp.where(m_prev[None], jnp.exp(_bdot(q, kp, "nt") * scale - lse), 0.0)
            corr = jnp.sum(dl, axis=-1, keepdims=True) - jnp.sum(do * o, axis=-1, keepdims=True)
            dob = do.astype(BF16)
            ds_c = (p_c * (_bdot(dob, vc, "nt") + corr)).astype(BF16)
            ds_p = (p_p * (_bdot(dob, vp, "nt") + corr)).astype(BF16)
            dq = (_bdot(ds_c, kc, "nn") + _bdot(ds_p, kp, "nn")) * scale
            dk_p, dk_c = _bdot(ds_p, q, "tn") * scale, _bdot(ds_c, q, "tn") * scale
            dv_p, dv_c = _bdot(p_p.astype(BF16), dob, "tn"), _bdot(p_c.astype(BF16), dob, "tn")
            for h in range(A_HEADS):
                sl = _attn_cols(h)
                dq_ref[:, sl] = dq[h]
                dk_ref[:, sl] = ck[:, sl] + dk_p[h]
                dv_ref[:, sl] = cv[:, sl] + dv_p[h]
                ck[:, sl] = dk_c[h]
                cv[:, sl] = dv_c[h]

    last = nblk - 1
    cur = pl.BlockSpec((BAND, A_WIDTH), lambda i: (jnp.minimum(i, last), 0))
    prev = pl.BlockSpec((BAND, A_WIDTH), lambda i: (jnp.minimum(jnp.maximum(i - 1, 0), last), 0))
    body, c_in_specs, c_out_specs, c_out_shape, c_scratch, c_ins = _carry(cargo, 9, 3, 2, body, nblk + 1)
    outs = pl.pallas_call(
        body, name=name, grid=(nblk + 1,),
        in_specs=[cur, prev, cur, prev, cur, cur, cur, cur, cur] + c_in_specs,
        out_specs=[cur, prev, prev] + c_out_specs,
        out_shape=[jax.ShapeDtypeStruct((S, A_WIDTH), F32)] * 3 + c_out_shape,
        scratch_shapes=[pltpu.VMEM((BAND, A_WIDTH), F32)] * 2 + c_scratch,
        compiler_params=_cparams(("arbitrary",)),
    )(q, k, k, v, v, o, lse, do, dlse, *c_ins)
    return outs[0], outs[1], outs[2], outs[3:]


def _merge_fwd_fn(o0, o1, o2, l0, l1, l2):
    m = jnp.maximum(jnp.maximum(l0, l1), l2)
    e0, e1, e2 = jnp.exp(l0 - m), jnp.exp(l1 - m), jnp.exp(l2 - m)
    return (e0 * o0 + e1 * o1 + e2 * o2) / (e0 + e1 + e2)


def _merge_bwd_fn(o0, o1, o2, l0, l1, l2, dom):
    m = jnp.maximum(jnp.maximum(l0, l1), l2)
    e0, e1, e2 = jnp.exp(l0 - m), jnp.exp(l1 - m), jnp.exp(l2 - m)
    den = e0 + e1 + e2
    w0, w1, w2 = e0 / den, e1 / den, e2 / den
    dw0, dw1, dw2 = dom * o0, dom * o1, dom * o2
    mean = w0 * dw0 + w1 * dw1 + w2 * dw2
    return w0 * dom, w1 * dom, w2 * dom, w0 * (dw0 - mean), w1 * (dw1 - mean), w2 * (dw2 - mean)


def _to_classes(t, d):
    if d == 1:
        return t
    S, C = t.shape
    return t.reshape(S // d, d, C).transpose(1, 0, 2).reshape(S, C)


def _from_classes(t, d):
    if d == 1:
        return t
    S, C = t.shape
    return t.reshape(d, S // d, C).transpose(1, 0, 2).reshape(S, C)


def _rope_lane_tables(positions):
    inv_freq = ROPE_THETA ** (-jnp.arange(0, ROPE_DIM, 2, dtype=F32) / ROPE_DIM)
    ang = positions.astype(F32)[:, None] * inv_freq
    cos, sin = jnp.cos(ang), jnp.sin(ang)
    S = positions.shape[0]
    rest = A_HEAD_DIM - ROPE_DIM
    ct = jnp.concatenate([cos, cos, jnp.ones((S, rest), F32)], axis=1)
    st = jnp.concatenate([-sin, sin, jnp.zeros((S, rest), F32)], axis=1)
    return jnp.tile(ct, (1, 2)), jnp.tile(st, (1, 2))


def _head_mean_matrix():
    r = jnp.arange(A_WIDTH) // A_HEAD_DIM
    return (r[:, None] == r[None, :]).astype(F32) * (1.0 / A_HEAD_DIM)


def _conv_fwd(name, proj, w):
    S = proj.shape[0]
    tm, tc = min(512, S), 1024
    per8 = tm // 8
    off = DN_QKV0 // tc

    def body(x_ref, halo_ref, w_ref, o_ref, xs):
        i = pl.program_id(0)
        xs[0:8, :] = jnp.where(i > 0, halo_ref[...], 0.0)
        xs[8:, :] = x_ref[...]
        acc = w_ref[0:1, :] * xs[pl.ds(8 - 3, tm), :]
        for j in range(1, CONV_WIDTH):
            acc = acc + w_ref[j:j + 1, :] * xs[pl.ds(8 - 3 + j, tm), :]
        o_ref[...] = acc

    return pl.pallas_call(
        body, name=name, grid=(S // tm, DN_QKV // tc),
        in_specs=[pl.BlockSpec((tm, tc), lambda i, j: (i, j + off)),
                  pl.BlockSpec((8, tc), lambda i, j: (jnp.maximum(i * per8 - 1, 0), j + off)),
                  pl.BlockSpec((CONV_WIDTH, tc), lambda i, j: (0, j))],
        out_specs=pl.BlockSpec((tm, tc), lambda i, j: (i, j)),
        out_shape=jax.ShapeDtypeStruct((S, DN_QKV), F32),
        scratch_shapes=[pltpu.VMEM((tm + 8, tc), F32)],
        compiler_params=_cparams(("parallel", "parallel")),
    )(proj, proj, w)


def _conv_bwd(name, proj, dpre, w):
    S = proj.shape[0]
    tm, tc = min(512, S), 1024
    per8 = tm // 8
    off = DN_QKV0 // tc
    last8 = S // 8 - 1
    nrow = S // tm

    def body(x_ref, xh_ref, d_ref, dh_ref, w_ref, dx_ref, dw_ref, xs, ds):
        i = pl.program_id(1)
        xs[0:8, :] = jnp.where(i > 0, xh_ref[...], 0.0)
        xs[8:, :] = x_ref[...]
        ds[0:tm, :] = d_ref[...]
        ds[tm:, :] = jnp.where(i < nrow - 1, dh_ref[...], 0.0)
        d = d_ref[...]
        acc = w_ref[0:1, :] * ds[pl.ds(3, tm), :]
        for j in range(1, CONV_WIDTH):
            acc = acc + w_ref[j:j + 1, :] * ds[pl.ds(3 - j, tm), :]
        dx_ref[...] = acc.astype(dx_ref.dtype)

        @pl.when(i == 0)
        def _():
            dw_ref[...] = jnp.zeros_like(dw_ref)

        for j in range(CONV_WIDTH):
            dw_ref[j:j + 1, :] += jnp.sum(d * xs[pl.ds(8 - 3 + j, tm), :], axis=0, keepdims=True)

    return pl.pallas_call(
        body, name=name, grid=(DN_QKV // tc, nrow),
        in_specs=[pl.BlockSpec((tm, tc), lambda j, i: (i, j + off)),
                  pl.BlockSpec((8, tc), lambda j, i: (jnp.maximum(i * per8 - 1, 0), j + off)),
                  pl.BlockSpec((tm, tc), lambda j, i: (i, j)),
                  pl.BlockSpec((8, tc), lambda j, i: (jnp.minimum((i + 1) * per8, last8), j)),
                  pl.BlockSpec((CONV_WIDTH, tc), lambda j, i: (0, j))],
        out_specs=[pl.BlockSpec((tm, tc), lambda j, i: (i, j)),
                   pl.BlockSpec((CONV_WIDTH, tc), lambda j, i: (0, j))],
        out_shape=[jax.ShapeDtypeStruct((S, DN_QKV), BF16), jax.ShapeDtypeStruct((CONV_WIDTH, DN_QKV), F32)],
        scratch_shapes=[pltpu.VMEM((tm + 8, tc), F32)] * 2,
        compiler_params=_cparams(("parallel", "arbitrary")),
    )(proj, proj, dpre, dpre, w)


def _gate_lane(shape):
    return lax.broadcasted_iota(jnp.int32, shape, 1)


GATES_ROWS = 256


def _chunk_cumsum_matrix():
    r = jnp.arange(GATES_ROWS)
    return ((r[:, None] >= r[None, :]) & (r[:, None] // CHUNK == r[None, :] // CHUNK)).astype(F32)


def _gates_fwd_fn(ab, alog, dt, cum):
    g = -jnp.exp(alog) * _softplus(ab + dt)
    gc = jnp.dot(cum, g, precision=HIGHEST, preferred_element_type=F32)
    return jnp.where(_gate_lane(ab.shape) < DN_HEADS, gc, _sigmoid(ab))


def _gates_bwd_fn(ab, dgb, alog, dt, cum):
    lane = _gate_lane(ab.shape)
    is_g = lane < DN_HEADS
    neg_a = -jnp.exp(alog)
    sp = _softplus(ab + dt)
    dsp = _sigmoid(ab + dt)
    beta = _sigmoid(ab)
    dgc = jnp.where(is_g, dgb, 0.0)
    dg = lax.dot_general(cum, dgc, (_DIMS["tn"], ((), ())), precision=HIGHEST, preferred_element_type=F32)
    dab = jnp.where(is_g, dg * neg_a * dsp, jnp.where(lane < 2 * DN_HEADS, dgb * beta * (1.0 - beta), 0.0))
    d_alog = jnp.sum(dg * neg_a * sp, axis=0, keepdims=True)
    d_dt = jnp.sum(dg * neg_a * dsp, axis=0, keepdims=True)
    return dab, d_alog, d_dt


def _chunk_math(precision):
    def dg(a, b, mode, prec=precision):
        return _bdot(a, b, mode, prec)

    @jax.custom_vjp
    def nn(a, b):
        return dg(a, b, "nn")

    @jax.custom_vjp
    def nt(a, b):
        return dg(a, b, "nt")

    @jax.custom_vjp
    def tn(a, b):
        return dg(a, b, "tn")

    nn.defvjp(lambda a, b: (nn(a, b), (a, b)), lambda r, g: (nt(g, r[1]), tn(r[0], g)))
    nt.defvjp(lambda a, b: (nt(a, b), (a, b)), lambda r, g: (nn(g, r[1]), tn(g, r[0])))
    tn.defvjp(lambda a, b: (tn(a, b), (a, b)), lambda r, g: (nt(r[1], g), nn(r[0], g)))

    def split(x):
        hi = x.astype(BF16)
        return hi, (x - hi.astype(F32)).astype(BF16)

    def fine(a, b, mode):
        ah, al = split(a)
        bh, bl = split(b)
        return dg(ah, bh, mode, None) + (dg(ah, bl, mode, None) + dg(al, bh, mode, None))

    def unit_lower_inverse(a):
        row = lax.broadcasted_iota(jnp.int32, a.shape, 1)
        col = lax.broadcasted_iota(jnp.int32, a.shape, 2)
        x = -a
        p = jnp.where(row == col, 1.0, 0.0) + x
        for _ in range(int(math.log2(CHUNK)) - 1):
            x = fine(x, x, "nn")
            p = p + fine(p, x, "nn")
        return p

    @jax.custom_vjp
    def solve2(a, ti, r1, r2):
        return fine(ti, r1, "nn"), fine(ti, r2, "nn")

    def solve2_fwd(a, ti, r1, r2):
        s1, s2 = fine(ti, r1, "nn"), fine(ti, r2, "nn")
        return (s1, s2), (ti, s1, s2)

    def solve2_bwd(res, g):
        ti, s1, s2 = res
        d1, d2 = fine(ti, g[0], "tn"), fine(ti, g[1], "tn")
        return -(fine(d1, s1, "nt") + fine(d2, s2, "nt")), jnp.zeros_like(ti), d1, d2

    solve2.defvjp(solve2_fwd, solve2_bwd)

    def chunk_fn(pq, pk, pv, z, g_col, b_col, g_row, ogain, s_in, inverse=None):
        nb = pq.shape[0]
        sq = (nb, CHUNK, CHUNK)
        row = lax.broadcasted_iota(jnp.int32, sq, 1)
        col = lax.broadcasted_iota(jnp.int32, sq, 2)
        lower, strict = row >= col, row > col
        q, k, v = _silu(pq), _silu(pk), _silu(pv)
        q = q * lax.rsqrt(jnp.sum(q * q, axis=-1, keepdims=True) + EPS) * (DN_HEAD_DIM ** -0.5)
        k = k * lax.rsqrt(jnp.sum(k * k, axis=-1, keepdims=True) + EPS)
        gc_wide = jnp.broadcast_to(g_col, pq.shape)
        gc_i = jnp.broadcast_to(g_col, sq)
        gc_j = jnp.broadcast_to(g_row, sq)
        is_last = lax.broadcasted_iota(jnp.int32, pq.shape, 1) == CHUNK - 1
        g_last = jnp.sum(jnp.where(is_last, gc_wide, 0.0), axis=1, keepdims=True)
        decay = jnp.exp(jnp.where(lower, gc_i - gc_j, -jnp.inf))
        kb = k * b_col
        a_mat = jnp.where(strict, nt(kb, k) * decay, 0.0)
        eg = jnp.exp(gc_wide)
        ti = unit_lower_inverse(a_mat) if inverse is None else inverse
        u, w = solve2(a_mat, ti, v * b_col, kb * eg)
        attn = nt(q, k) * decay
        q_dec = q * eg
        k_dec = k * jnp.exp(g_last - gc_wide)
        c_dec = jnp.exp(g_last)
        v_new = u - nn(w, s_in)
        o = nn(q_dec, s_in) + nn(attn, v_new)
        s_out = s_in * c_dec + tn(k_dec, v_new)
        y = o * lax.rsqrt(jnp.mean(o * o, axis=-1, keepdims=True) + EPS) * ogain * _silu(z)
        return (y, s_out, ti) if inverse is None else (y, s_out)

    return chunk_fn


DN_PRECISION = None


def _chunk_specs(n_of):
    groups = DN_HEADS // DN_HB
    wide = DN_HB * DN_HEAD_DIM
    hd = pl.BlockSpec((CHUNK, wide), lambda h, n: (n_of(n), h))
    specs = dict(
        pq=hd,
        pk=pl.BlockSpec((CHUNK, wide), lambda h, n: (n_of(n), groups + h)),
        pv=pl.BlockSpec((CHUNK, wide), lambda h, n: (n_of(n), 2 * groups + h)),
        z=hd,
        gates=pl.BlockSpec((CHUNK, DN_AB_PAD), lambda h, n: (n_of(n), 0)),
        row=pl.BlockSpec((DN_HB, None, 1, CHUNK), lambda h, n: (h, n_of(n), 0, 0)),
        gain=pl.BlockSpec((1, DN_HEAD_DIM), lambda h, n: (0, 0)),
        state=pl.BlockSpec((DN_HB, None, DN_HEAD_DIM, DN_HEAD_DIM), lambda h, n: (h, n_of(n), 0, 0)),
        inverse=pl.BlockSpec((DN_HB, None, CHUNK, CHUNK), lambda h, n: (h, n_of(n), 0, 0)),
        qkv=pl.BlockSpec((CHUNK, DN_QKV), lambda h, n: (n_of(n), 0)),
        head=hd,
    )
    return specs


def _head_cols(j):
    return slice(j * DN_HEAD_DIM, (j + 1) * DN_HEAD_DIM)


def _split_heads(ref):
    return jnp.stack([ref[:, _head_cols(j)] for j in range(DN_HB)])


def _gate_columns(gates, first_lane):
    lane = lax.broadcasted_iota(jnp.int32, gates.shape, 1)
    return jnp.stack([jnp.sum(jnp.where(lane == first_lane + h, gates, 0.0), axis=-1, keepdims=True)
                      for h in range(DN_HEADS)])


def _gate_lanes(columns, first_lane):
    shape = (columns.shape[1], DN_AB_PAD)
    lane = lax.broadcasted_iota(jnp.int32, shape, 1)
    out = jnp.zeros(shape, F32)
    for h in range(DN_HEADS):
        out = out + jnp.where(lane == first_lane + h, columns[h], 0.0)
    return out


def _chunk_fwd(name, pre, proj, gates, g_row, ogain):
    assert DN_HB == DN_HEADS
    S = pre.shape[0]
    N = S // CHUNK
    chunk_fn = _chunk_math(DN_PRECISION)
    sp = _chunk_specs(lambda n: n)

    def body(pq, pk, pv, z, gb, gr, og, y_ref, sin_ref, inv_ref, st):
        @pl.when(pl.program_id(1) == 0)
        def _():
            st[...] = jnp.zeros_like(st)

        s_in = st[...]
        sin_ref[...] = s_in
        y, s_out, inverse = chunk_fn(_split_heads(pq), _split_heads(pk), _split_heads(pv), _split_heads(z),
                                     _gate_columns(gb[...], 0), _gate_columns(gb[...], DN_HEADS), gr[...],
                                     og[...], s_in)
        for j in range(DN_HB):
            y_ref[:, _head_cols(j)] = y[j].astype(y_ref.dtype)
        inv_ref[...] = inverse
        st[...] = s_out

    return pl.pallas_call(
        body, name=name, grid=(DN_HEADS // DN_HB, N),
        in_specs=[sp["pq"], sp["pk"], sp["pv"], sp["z"], sp["gates"], sp["row"], sp["gain"]],
        out_specs=[sp["head"], sp["state"], sp["inverse"]],
        out_shape=[jax.ShapeDtypeStruct((S, DN_WIDTH), BF16),
                   jax.ShapeDtypeStruct((DN_HEADS, N, DN_HEAD_DIM, DN_HEAD_DIM), F32),
                   jax.ShapeDtypeStruct((DN_HEADS, N, CHUNK, CHUNK), F32)],
        scratch_shapes=[pltpu.VMEM((DN_HB, DN_HEAD_DIM, DN_HEAD_DIM), F32)],
        compiler_params=_cparams(("parallel", "arbitrary")),
    )(pre, pre, pre, proj, gates, g_row, ogain)


def _chunk_bwd(name, pre, proj, gates, g_row, ogain, s_in_all, inverse_all, dy):
    assert DN_HB == DN_HEADS
    S = pre.shape[0]
    N = S // CHUNK
    chunk_fn = _chunk_math(DN_PRECISION)
    sp = _chunk_specs(lambda n: N - 1 - n)

    def body(pq, pk, pv, z, gb, gr, og, sin_ref, inv_ref, dy_ref,
             dpre_ref, dz_ref, dgb_ref, dgr_ref, dog_ref, ds):
        @pl.when(pl.program_id(1) == 0)
        def _():
            ds[...] = jnp.zeros_like(ds)
            dog_ref[...] = jnp.zeros_like(dog_ref)

        inverse = inv_ref[...]
        prim = (_split_heads(pq), _split_heads(pk), _split_heads(pv), _split_heads(z),
                _gate_columns(gb[...], 0), _gate_columns(gb[...], DN_HEADS), gr[...], og[...], sin_ref[...])
        _, vjp = jax.vjp(lambda *a: chunk_fn(*a, inverse=inverse), *prim)
        gq, gk, gv, gz, ggc, gbc, ggr, gog, gs = vjp((_split_heads(dy_ref), ds[...]))
        for j in range(DN_HB):
            for part, g in enumerate((gq, gk, gv)):
                dpre_ref[:, pl.ds(part * DN_WIDTH + j * DN_HEAD_DIM, DN_HEAD_DIM)] = g[j]
            dz_ref[:, _head_cols(j)] = gz[j]
        dgb_ref[...] = _gate_lanes(ggc, 0) + _gate_lanes(gbc, DN_HEADS)
        dgr_ref[...] = ggr
        dog_ref[...] += gog
        ds[...] = gs

    hd = sp["head"]
    return pl.pallas_call(
        body, name=name, grid=(1, N),
        in_specs=[sp["pq"], sp["pk"], sp["pv"], sp["z"], sp["gates"], sp["row"], sp["gain"],
                  sp["state"], sp["inverse"], hd],
        out_specs=[sp["qkv"], hd, sp["gates"], sp["row"], sp["gain"]],
        out_shape=[jax.ShapeDtypeStruct((S, DN_QKV), F32), jax.ShapeDtypeStruct((S, DN_WIDTH), F32),
                   jax.ShapeDtypeStruct((S, DN_AB_PAD), F32),
                   jax.ShapeDtypeStruct((DN_HEADS, N, 1, CHUNK), F32), jax.ShapeDtypeStruct((1, DN_HEAD_DIM), F32)],
        scratch_shapes=[pltpu.VMEM((DN_HB, DN_HEAD_DIM, DN_HEAD_DIM), F32)],
        compiler_params=_cparams(("arbitrary", "arbitrary")),
    )(pre, pre, pre, proj, gates, g_row, ogain, s_in_all, inverse_all, dy)


def _mm_rms_bwd(name, d_out, w, x, gain, dres):
    return _mm(name, d_out, w, "nt", out_dtypes=(F32, BF16), extras=(x, dres), rows=(gain.reshape(1, -1),),
               epilogue=lambda acc, x_, dres_, g: _rms_bwd_fn(x_, dres_, acc, g), n_sums=1, tm_cap=512)


def _residual_norm_epilogue(acc, res, gain):
    x = acc + res
    return x, _rms_fwd_fn(x, gain)


def _ple_loss_fn(x_mid, pp, zg, t):
    gate = _sigmoid(zg)
    err = x_mid + pp * gate - t
    dy = err * (1.0 / D_MODEL)
    return dy, dy * gate, dy * pp * gate * (1.0 - gate), jnp.broadcast_to(jnp.sum(err * err, keepdims=True), (1, 128))


def _mlp_ple_fwd(tag, x_in, h, p_l, w_up, w_down, norm_ple, w_ple, w_gate, next_gain=None, target=None):
    a = _mm(f"{tag}_up", h, w_up, "nn", out_dtypes=(BF16,), epilogue=_relu2_epilogue)
    x_mid, hg = _mm(f"{tag}_down", a, w_down, "nn", out_dtypes=(F32, BF16), extras=(x_in,),
                    rows=(norm_ple.reshape(1, -1),), epilogue=_residual_norm_epilogue)
    zg = _mm(f"{tag}_gate", hg, w_gate, "nn")
    pp = _mm(f"{tag}_ple", p_l, w_ple, "nn")
    sv = dict(x_in=x_in, h=h, a=a, x_mid=x_mid, hg=hg, zg=zg, pp=pp)
    if target is not None:
        sv["dy"], sv["dpp"], sv["dzg"], sv["sq"] = _rowwise(
            f"{tag}_ple_loss", _ple_loss_fn, [x_mid, pp, zg, target], [],
            [(D_MODEL, F32), (D_MODEL, BF16), (D_MODEL, BF16)], [(1, 128)])
        return None, None, sv
    if next_gain is None:
        return _rowwise(f"{tag}_ple_out", _ple_fwd_fn, [x_mid, pp, zg], [], [(D_MODEL, F32)]), None, sv
    x_out, h_next = _rowwise(f"{tag}_ple_out", _ple_norm_fwd_fn, [x_mid, pp, zg], [next_gain.reshape(1, -1)],
                             [(D_MODEL, F32), (D_MODEL, BF16)])
    return x_out, h_next, sv


def _mlp_ple_bwd(tag, dx, sv, p_l, norm_mlp, w_up, w_down, norm_ple, w_ple, w_gate):
    if "dpp" in sv:
        dpp, dzg = sv["dpp"], sv["dzg"]
    else:
        dpp, dzg = _rowwise(f"{tag}_ple_bwd", _ple_bwd_fn, [dx, sv["pp"], sv["zg"]], [],
                            [(D_MODEL, BF16), (D_MODEL, BF16)])
    d_w_ple = _mm(f"{tag}_d_w_ple", p_l, dpp, "tn", out_dtypes=(BF16,))
    d_w_gate = _mm(f"{tag}_d_w_gate", sv["hg"], dzg, "tn", out_dtypes=(BF16,))
    dx_mid, dx_mid_b, d_norm_ple = _mm_rms_bwd(f"{tag}_d_hg", dzg, w_gate, sv["x_mid"], norm_ple, dx)
    du = _mm(f"{tag}_d_u", dx_mid_b, w_down, "nt", out_dtypes=(BF16,), extras=(sv["a"],),
             epilogue=_relu2_bwd_epilogue)
    d_w_down = _mm(f"{tag}_d_w_down", sv["a"], dx_mid_b, "tn", out_dtypes=(BF16,))
    d_w_up = _mm(f"{tag}_d_w_up", sv["h"], du, "tn", out_dtypes=(BF16,))
    dx_in, dx_in_b, d_norm_mlp = _mm_rms_bwd(f"{tag}_d_h", du, w_up, sv["x_in"], norm_mlp, dx_mid)
    return dx_in, dx_in_b, dict(mlp_norm=d_norm_mlp, w_up=d_w_up, w_down=d_w_down, ple_norm=d_norm_ple,
                                w_ple=d_w_ple, w_ple_gate=d_w_gate)


class _NoHooks:
    fwd_cargo = (None,) * len(SWA_GROUPS)

    def weights_from(self, results):
        return {}

    def bwd_cargo(self, early_grads):
        return (None,) * len(SWA_GROUPS)


def _local_step(x, p, positions, target, small, big, hooks=_NoHooks()):
    S = x.shape[0]
    ct, st = _rope_lane_tables(positions)
    bd = _head_mean_matrix()

    h0 = _rms_fwd("l0_mix_norm", x, small["mix_norm"][0])
    attn, brought = [], []
    for g, (window, d) in enumerate(SWA_GROUPS):
        assert window // d == BAND and (S // d) % BAND == 0
        h0g = _to_classes(h0, d)
        ctg, stg = _to_classes(ct, d), _to_classes(st, d)
        w_g = big["attn_w_qkv"][:, g * 3 * A_WIDTH:(g + 1) * 3 * A_WIDTH]
        gq = jnp.tile(small["attn_q_gain"][0, g], A_HEADS).reshape(1, A_WIDTH)
        gk = jnp.tile(small["attn_k_gain"][0, g], A_HEADS).reshape(1, A_WIDTH)
        qkv = _mm(f"l0_qkv{g}", h0g, w_g, "nn")
        q, k, v = _rowwise(f"l0_qk_prep{g}", _qk_prep_fwd_fn, [qkv, ctg, stg], [gq, gk, bd], [(A_WIDTH, BF16)] * 3)
        o, lse, cargo_out = _attn_fwd(f"l0_attn{g}", q, k, v, (S // d) // BAND, cargo=hooks.fwd_cargo[g])
        brought.append(cargo_out)
        attn.append(dict(d=d, h0g=h0g, ct=ctg, st=stg, w=w_g, gq=gq, gk=gk, qkv=qkv, q=q, k=k, v=v, o=o, lse=lse,
                         o_tok=_from_classes(o, d), lse_tok=_from_classes(lse, d)))
    big = {**big, **hooks.weights_from(brought)}
    om = _rowwise("l0_merge", _merge_fwd_fn, [a["o_tok"] for a in attn] + [a["lse_tok"] for a in attn], [],
                  [(A_WIDTH, BF16)])
    x1, h1 = _mm("l0_attn_out", om, big["attn_w_o"], "nn", out_dtypes=(F32, BF16), extras=(x,),
                 rows=(small["mlp_norm"][0].reshape(1, -1),), epilogue=_residual_norm_epilogue)
    x3, h3, sv0 = _mlp_ple_fwd("l0", x1, h1, p[0], big["w_up"][0], big["w_down"][0], small["ple_norm"][0],
                               big["w_ple"][0], big["w_ple_gate"][0], next_gain=small["mix_norm"][1])

    N = S // CHUNK
    proj = _mm("l1_in", h3, big["dn_w_in"], "nn")
    pre = _conv_fwd("l1_conv", proj, small["dn_conv"])
    ab = proj[:, DN_AB0:DN_AB0 + DN_AB_PAD]
    lane_pad = DN_AB_PAD - DN_HEADS
    alog_row = jnp.pad(small["dn_a_log"][0], (0, lane_pad)).reshape(1, DN_AB_PAD)
    dt_row = jnp.pad(small["dn_dt_bias"][0], (0, lane_pad)).reshape(1, DN_AB_PAD)
    cum = _chunk_cumsum_matrix()
    gb = _rowwise("l1_gates", _gates_fwd_fn, [ab], [alog_row, dt_row, cum], [(DN_AB_PAD, F32)], tm=GATES_ROWS)
    g_row = gb[:, :DN_HEADS].T.reshape(DN_HEADS, N, 1, CHUNK)
    ogain = small["dn_o_gain"][0].reshape(1, DN_HEAD_DIM)
    y, s_in_all, inverse_all = _chunk_fwd("l1_delta", pre, proj, gb, g_row, ogain)
    x4, h4 = _mm("l1_dn_out", y, big["dn_w_o"], "nn", out_dtypes=(F32, BF16), extras=(x3,),
                 rows=(small["mlp_norm"][1].reshape(1, -1),), epilogue=_residual_norm_epilogue)
    _, _, sv1 = _mlp_ple_fwd("l1", x4, h4, p[1], big["w_up"][1], big["w_down"][1], small["ple_norm"][1],
                             big["w_ple"][1], big["w_ple_gate"][1], target=target)
    dy, sq = sv1["dy"], sv1["sq"]

    dx4, dx4_b, gl1 = _mlp_ple_bwd("l1", dy, sv1, p[1], small["mlp_norm"][1], big["w_up"][1], big["w_down"][1],
                            small["ple_norm"][1], big["w_ple"][1], big["w_ple_gate"][1])
    d_y = _mm("l1_d_y", dx4_b, big["dn_w_o"], "nt")
    d_dn_w_o = _mm("l1_d_w_o", y, dx4_b, "tn", out_dtypes=(BF16,))
    dpre, dz, dgb_cols, dg_row, d_ogain = _chunk_bwd(
        "l1_delta_bwd", pre, proj, gb, g_row, ogain, s_in_all, inverse_all, d_y)
    dconv_in, d_conv_w = _conv_bwd("l1_conv_bwd", proj, dpre, small["dn_conv"])
    dgb = dgb_cols + jnp.pad(dg_row.reshape(DN_HEADS, S).T, ((0, 0), (0, DN_AB_PAD - DN_HEADS)))
    dab, d_alog, d_dt = _rowwise("l1_gates_bwd", _gates_bwd_fn, [ab, dgb], [alog_row, dt_row, cum],
                                 [(DN_AB_PAD, F32)], [(1, DN_AB_PAD), (1, DN_AB_PAD)], tm=GATES_ROWS)
    dproj = jnp.concatenate([dz.astype(BF16), dconv_in, dab.astype(BF16)], axis=1)
    d_dn_w_in = _mm("l1_d_w_in", h3, dproj, "tn", out_dtypes=(BF16,))
    dx3, _, d_mix1 = _mm_rms_bwd("l1_d_h", dproj, big["dn_w_in"], x3, small["mix_norm"][1], dx4)

    dx1, dx1_b, gl0 = _mlp_ple_bwd("l0", dx3, sv0, p[0], small["mlp_norm"][0], big["w_up"][0], big["w_down"][0],
                            small["ple_norm"][0], big["w_ple"][0], big["w_ple_gate"][0])
    early = dict(
        dn_w_in=jnp.concatenate([d_dn_w_in[:, DN_QKV0:DN_AB0 + 2 * DN_HEADS], d_dn_w_in[:, :DN_WIDTH]], axis=1),
        dn_w_o=d_dn_w_o,
        w_up=jnp.stack([gl0["w_up"], gl1["w_up"]]),
        w_down=jnp.stack([gl0["w_down"], gl1["w_down"]]),
        w_ple=jnp.stack([gl0["w_ple"], gl1["w_ple"]]),
        w_ple_gate=jnp.stack([gl0["w_ple_gate"], gl1["w_ple_gate"]]))
    bwd_cargo = hooks.bwd_cargo(early)
    dom = _mm("l0_d_om", dx1_b, big["attn_w_o"], "nt")
    d_attn_w_o = _mm("l0_d_w_o", om, dx1_b, "tn", out_dtypes=(BF16,))
    merged = _rowwise("l0_merge_bwd", _merge_bwd_fn,
                      [a["o_tok"] for a in attn] + [a["lse_tok"] for a in attn] + [dom], [], [(A_WIDTH, F32)] * 6)
    dh0, d_w_qkv, d_gq, d_gk, brought_bwd = [], [], [], [], []
    for g, a in enumerate(attn):
        do_g, dl_g = _to_classes(merged[g], a["d"]), _to_classes(merged[3 + g], a["d"])
        dqn, dkn, dvn, cargo_out = _attn_bwd(f"l0_attn_bwd{g}", a["q"], a["k"], a["v"], a["o"], a["lse"], do_g, dl_g,
                                             (S // a["d"]) // BAND, cargo=bwd_cargo[g])
        brought_bwd.append(cargo_out)
        dqkv, dgq, dgk = _rowwise(f"l0_qk_prep_bwd{g}", _qk_prep_bwd_fn, [a["qkv"], a["ct"], a["st"], dqn, dkn, dvn],
                                  [a["gq"], a["gk"], bd], [(3 * A_WIDTH, BF16)], [(1, A_HEAD_DIM)] * 2)
        d_w_qkv.append(_mm(f"l0_d_w_qkv{g}", a["h0g"], dqkv, "tn", out_dtypes=(BF16,)))
        dh0.append(_from_classes(_mm(f"l0_d_h{g}", dqkv, a["w"], "nt"), a["d"]))
        d_gq.append(dgq)
        d_gk.append(dgk)
    grad_x, _, d_mix0 = _rms_bwd("l0_mix_norm_bwd", x, small["mix_norm"][0], dx1, dh0)

    grads = dict(
        mix_norm=jnp.concatenate([d_mix0, d_mix1], axis=0),
        attn_w_qkv=jnp.concatenate(d_w_qkv, axis=1),
        attn_q_gain=jnp.concatenate(d_gq, axis=0)[None],
        attn_k_gain=jnp.concatenate(d_gk, axis=0)[None],
        attn_w_o=d_attn_w_o,
        dn_conv=d_conv_w,
        dn_a_log=d_alog[:, :DN_HEADS],
        dn_dt_bias=d_dt[:, :DN_HEADS],
        dn_o_gain=d_ogain,
        mlp_norm=jnp.concatenate([gl0["mlp_norm"], gl1["mlp_norm"]], axis=0),
        ple_norm=jnp.concatenate([gl0["ple_norm"], gl1["ple_norm"]], axis=0),
        **early,
    )
    return sq, grad_x, grads, brought_bwd


def _chip_peer(x, y, c, t):
    return (jnp.bitwise_xor(x, t >> 1), jnp.bitwise_xor(y, t & 1), c)


def _place():
    x, y, c = lax.axis_index("x"), lax.axis_index("y"), lax.axis_index("c")
    return x, y, c, 2 * x + y, (x, y, 1 - c)


def _remote(src, dst, send_sem, recv_sem, to):
    return pltpu.make_async_remote_copy(src_ref=src, dst_ref=dst, send_sem=send_sem, recv_sem=recv_sem,
                                        device_id=to, device_id_type=MESH)


def _hbm_call(name, body, ins, out_shape, scratch_shapes):
    any_spec = pl.BlockSpec(memory_space=pl.ANY)
    return pl.pallas_call(body, name=name, out_shape=out_shape, in_specs=[any_spec] * len(ins),
                          out_specs=[any_spec] * len(out_shape), scratch_shapes=scratch_shapes)(*ins)


def _half(n0, which):
    return pl.ds(which * (n0 // 2), n0 // 2)


class _Exchange:
    def __init__(self, ins, out_shape, scratch, start, finish):
        self.ins, self.out_shape, self.scratch, self.start, self.finish = ins, out_shape, scratch, start, finish


def _run_exchange(name, ex):
    n_in, n_out = len(ex.ins), len(ex.out_shape)

    def body(*refs):
        ins, outs, sems = refs[:n_in], refs[n_in:n_in + n_out], refs[n_in + n_out:]
        ex.start(ins, outs, sems)
        ex.finish(ins, outs, sems)

    return _hbm_call(name, body, ex.ins, ex.out_shape, ex.scratch)


def _gather_exchange(shards):
    T = len(shards)

    def copies(ins, outs, sems):
        send, recv = sems
        x, y, c, q, sibling = _place()
        over_ici, over_d2d, arriving = [], [], []
        for i in range(T):
            mine, theirs = _half(ins[i].shape[0], c), _half(ins[i].shape[0], 1 - c)
            for t in range(1, N_CHIPS):
                peer = _chip_peer(x, y, c, t)
                landed = outs[i].at[jnp.bitwise_xor(q, t), mine]
                passed = outs[i].at[jnp.bitwise_xor(q, t), theirs]
                over_ici.append((_remote(ins[i].at[mine], outs[i].at[q, mine], send.at[i, t - 1], recv.at[i, t - 1], peer),
                                 _remote(landed, landed, send.at[i, t - 1], recv.at[i, t - 1], peer)))
                over_d2d.append(_remote(landed, landed, send.at[i, 2 + t], recv.at[i, 2 + t], sibling))
                arriving.append(_remote(passed, passed, send.at[i, 2 + t], recv.at[i, 2 + t], sibling))
        return over_ici, over_d2d, arriving

    def start(ins, outs, sems):
        for mine, _ in copies(ins, outs, sems)[0]:
            mine.start()

    def finish(ins, outs, sems):
        over_ici, over_d2d, arriving = copies(ins, outs, sems)
        for (_, landing), forward in zip(over_ici, over_d2d):
            landing.wait_recv()
            forward.start()
        for cp in arriving:
            cp.wait_recv()
        for (mine, _), forward in zip(over_ici, over_d2d):
            mine.wait_send()
            forward.wait_send()

    n_rel = 2 * (N_CHIPS - 1)
    return _Exchange(list(shards), [jax.ShapeDtypeStruct((N_CHIPS,) + s.shape, s.dtype) for s in shards],
                     [pltpu.SemaphoreType.DMA((T, n_rel)), pltpu.SemaphoreType.DMA((T, n_rel))], start, finish)


def _scatter_exchange(stacks):
    T = len(stacks)

    def copies(ins, outs, sems):
        send, recv = sems
        x, y, c, q, sibling = _place()
        return [_remote(ins[i].at[jnp.bitwise_xor(q, t)], outs[i].at[t - 1], send.at[i, t - 1], recv.at[i, t - 1],
                        _chip_peer(x, y, c, t)) for i in range(T) for t in range(1, N_CHIPS)]

    def start(ins, outs, sems):
        for cp in copies(ins, outs, sems):
            cp.start()

    def finish(ins, outs, sems):
        for cp in copies(ins, outs, sems):
            cp.wait()

    return _Exchange(list(stacks), [jax.ShapeDtypeStruct((N_CHIPS - 1,) + s.shape[1:], s.dtype) for s in stacks],
                     [pltpu.SemaphoreType.DMA((T, N_CHIPS - 1)), pltpu.SemaphoreType.DMA((T, N_CHIPS - 1))],
                     start, finish)


def _other_half_from_sibling(name, stacks):
    T = len(stacks)

    def body(*refs):
        ins, outs = refs[:T], refs[T:2 * T]
        send, recv = refs[2 * T:]
        x, y, c, q, sibling = _place()
        copies = []
        for i in range(T):
            rc = _remote(ins[i].at[:, _half(ins[i].shape[1], 1 - c)], outs[i], send.at[i], recv.at[i], sibling)
            rc.start()
            copies.append(rc)
        for cp in copies:
            cp.wait()

    return _hbm_call(name, body, stacks,
                     [jax.ShapeDtypeStruct((s.shape[0], s.shape[1] // 2) + s.shape[2:], s.dtype) for s in stacks],
                     [pltpu.SemaphoreType.DMA((T,)), pltpu.SemaphoreType.DMA((T,))])


def _swap_with_sibling(name, arrays):
    T = len(arrays)

    def body(*refs):
        ins, outs = refs[:T], refs[T:2 * T]
        send, recv = refs[2 * T:]
        x, y, c, q, sibling = _place()
        copies = []
        for i in range(T):
            rc = _remote(ins[i], outs[i], send.at[i], recv.at[i], sibling)
            rc.start()
            copies.append(rc)
        for cp in copies:
            cp.wait()

    return _hbm_call(name, body, arrays, [jax.ShapeDtypeStruct(a.shape, a.dtype) for a in arrays],
                     [pltpu.SemaphoreType.DMA((T,)), pltpu.SemaphoreType.DMA((T,))])


def _gather_from_all(name, block):
    R, C = block.shape

    def body(src, out, send_sems, recv_sems):
        x, y, c = lax.axis_index("x"), lax.axis_index("y"), lax.axis_index("c")
        me = 4 * x + 2 * y + c
        out[me] = src[...]
        copies = []
        for r in range(1, N_DEV):
            peer = (jnp.bitwise_xor(x, r >> 2), jnp.bitwise_xor(y, (r >> 1) & 1), jnp.bitwise_xor(c, r & 1))
            cp = pltpu.make_async_remote_copy(src_ref=src, dst_ref=out.at[me], send_sem=send_sems.at[r - 1],
                                              recv_sem=recv_sems.at[r - 1], device_id=peer, device_id_type=MESH)
            cp.start()
            copies.append(cp)
        for cp in copies:
            cp.wait()

    return pl.pallas_call(
        body, name=name, out_shape=jax.ShapeDtypeStruct((N_DEV, R, C), block.dtype),
        in_specs=[pl.BlockSpec(memory_space=pltpu.VMEM)], out_specs=pl.BlockSpec(memory_space=pltpu.VMEM),
        scratch_shapes=[pltpu.SemaphoreType.DMA((N_DEV - 1,)), pltpu.SemaphoreType.DMA((N_DEV - 1,))],
    )(block)


def _view(a):
    return a[0] if a.shape[0] == 1 else a


def _view_axis(a, axis):
    return axis - 1 if a.shape[0] == 1 else axis


def _rows(a):
    return a.reshape(-1, a.shape[-1])


def _elementwise(name, fn, ins, out_dtypes, tm):
    specs = []
    for a in ins:
        a, row0 = a if isinstance(a, tuple) else (a, 0)
        specs.append((_rows(a), a.shape[-1], 0, row0))
    shape = ins[0][0].shape if isinstance(ins[0], tuple) else ins[0].shape
    outs = _rowwise(name, fn, specs, [], [(shape[-1], dt) for dt in out_dtypes], tm=tm, n_rows=math.prod(shape[:-1]))
    return outs.reshape(shape) if len(out_dtypes) == 1 else [o.reshape(shape) for o in outs]


SMALL_ROWS = 8
CONV_ROWS = CONV_WIDTH * DN_QKV // D_MODEL
SMALL_GRAD_ROWS = 24


def _pack_small(vals, conv=None):
    tail = jnp.concatenate([vals["attn_q_gain"].reshape(-1), vals["attn_k_gain"].reshape(-1),
                            vals["dn_a_log"].reshape(-1), vals["dn_dt_bias"].reshape(-1),
                            vals["dn_o_gain"].reshape(-1)])
    tail = jnp.pad(tail, (0, D_MODEL - tail.shape[0])).reshape(1, D_MODEL)
    rows = [vals["mix_norm"], vals["mlp_norm"], vals["ple_norm"], tail, jnp.zeros((1, D_MODEL), F32)]
    if conv is not None:
        rows += [conv.reshape(CONV_ROWS, D_MODEL),
                 jnp.zeros((SMALL_GRAD_ROWS - SMALL_ROWS - CONV_ROWS, D_MODEL), F32)]
    return jnp.concatenate(rows, axis=0)


def _unpack_small(block):
    nq = 3 * A_HEAD_DIM
    t = block[6]
    return dict(
        mix_norm=block[0:2], mlp_norm=block[2:4], ple_norm=block[4:6],
        attn_q_gain=t[:nq].reshape(1, 3, A_HEAD_DIM), attn_k_gain=t[nq:2 * nq].reshape(1, 3, A_HEAD_DIM),
        dn_a_log=t[2 * nq:2 * nq + DN_HEADS].reshape(1, DN_HEADS),
        dn_dt_bias=t[2 * nq + DN_HEADS:2 * nq + 2 * DN_HEADS].reshape(1, DN_HEADS),
        dn_o_gain=t[2 * nq + 2 * DN_HEADS:2 * nq + 2 * DN_HEADS + DN_HEAD_DIM].reshape(1, DN_HEAD_DIM))


def kernel(x, p, positions, mix_norm, attn_w_qkv, attn_q_gain, attn_k_gain, attn_w_o, dn_w_in, dn_conv, dn_a_log, dn_dt_bias, dn_o_gain, dn_w_o, mlp_norm, w_up, w_down, ple_norm, w_ple, w_ple_gate, loss_target, m_mix_norm, m_attn_w_qkv, m_attn_q_gain, m_attn_k_gain, m_attn_w_o, m_dn_w_in, m_dn_conv, m_dn_a_log, m_dn_dt_bias, m_dn_o_gain, m_dn_w_o, m_mlp_norm, m_w_up, m_w_down, m_ple_norm, m_w_ple, m_w_ple_gate, v_mix_norm, v_attn_w_qkv, v_attn_q_gain, v_attn_k_gain, v_attn_w_o, v_dn_w_in, v_dn_conv, v_dn_a_log, v_dn_dt_bias, v_dn_o_gain, v_dn_w_o, v_mlp_norm, v_w_up, v_w_down, v_ple_norm, v_w_ple, v_w_ple_gate):
    given = dict(locals())
    w = {n: given[n] for n in WEIGHTS}
    m = {n: given["m_" + n] for n in WEIGHTS}
    v = {n: given["v_" + n] for n in WEIGHTS}
    kinds = ("grad", "delta", "new_m", "new_v")
    axes = {n: _view_axis(w[n], axis) for n, axis in SHARDED if n != "dn_conv"}
    chip = 2 * lax.axis_index("x") + lax.axis_index("y")
    core = lax.axis_index("c")
    shards = {n: _view(w[n]).astype(BF16) for n in axes}

    def whole(n, slots):
        return jnp.concatenate([jnp.where(chip == q, shards[n], slots[q]) for q in range(N_CHIPS)], axis=axes[n])

    def chip_sums_of(tag, grads_of):
        names = list(grads_of)
        stacks = [jnp.stack(jnp.split(grads_of[n], N_CHIPS, axis=axes[n])) for n in names]
        mine = [lax.dynamic_slice_in_dim(s, core * (s.shape[1] // 2), s.shape[1] // 2, axis=1) for s in stacks]
        theirs = _other_half_from_sibling(f"split_core_grads_{tag}", stacks)
        return {n: _elementwise(f"add_core_{n}", lambda a, b: a.astype(F32) + b.astype(F32), [a, b], [BF16], 128)
                for n, a, b in zip(names, mine, theirs)}

    class Hooks:
        fwd_cargo = [_gather_exchange([shards[n] for n in group]) for group in CARGO_GROUPS]
        chip_sums = {}

        def weights_from(self, results):
            full = {n: whole(n, slots) for group, res in zip(CARGO_GROUPS, results) for n, slots in zip(group, res)}
            w_in, n_ab = full["dn_w_in"], 2 * DN_HEADS
            full["dn_w_in"] = jnp.concatenate([w_in[:, DN_QKV + n_ab:], w_in[:, :DN_QKV + n_ab],
                                               jnp.zeros((D_MODEL, DN_AB_PAD - n_ab), BF16)], axis=1)
            return full

        def bwd_cargo(self, early_grads):
            self.chip_sums.update(chip_sums_of("early", early_grads))
            return [_scatter_exchange([self.chip_sums[n] for n in group]) for group in CARGO_GROUPS]

    hooks = Hooks()
    gathered = _run_exchange("gather_attn_weights", _gather_exchange([shards[n] for n in ATTN_MATRICES]))
    big = {n: whole(n, slots) for n, slots in zip(ATTN_MATRICES, gathered)}
    conv_block = jnp.pad(w["dn_conv"].reshape(-1), (0, SMALL_ROWS * D_MODEL - w["dn_conv"].size))
    conv_all = _gather_from_all("gather_conv", conv_block.reshape(SMALL_ROWS, D_MODEL))
    conv_all = conv_all.reshape(N_CHIPS, 2, -1)[:, 0, :w["dn_conv"].size]
    conv_full = jnp.concatenate([conv_all[q].reshape(CONV_WIDTH, -1) for q in range(N_CHIPS)], axis=1)
    small = {n: w[n] for n in REPLICATED}
    small["dn_conv"] = conv_full

    sq, grad_x, grads, brought = _local_step(x[0], p[:, 0], positions[0], loss_target[0], small, big, hooks)
    loss = lax.psum(0.5 * sq[0, 0] / D_MODEL, ("x", "y", "c"))
    out = {}

    landed = {n: r for group, res in zip(CARGO_GROUPS, brought) for n, r in zip(group, res)}
    hooks.chip_sums.update(chip_sums_of("attn", {n: grads[n] for n in ATTN_MATRICES}))
    late = _run_exchange("scatter_attn_grads", _scatter_exchange([hooks.chip_sums[n] for n in ATTN_MATRICES]))
    landed.update(zip(ATTN_MATRICES, late))
    half_sums = []
    for n in axes:
        o = lax.dynamic_index_in_dim(hooks.chip_sums[n], chip, axis=0, keepdims=False)
        per = math.prod(o.shape[:-1])
        r = landed[n]
        half_sums.append(_elementwise(
            f"add_chips_{n}", lambda a, b, c, d: ((a.astype(F32) + b.astype(F32)) + c.astype(F32)) + d.astype(F32),
            [o, (r, 0), (r, per), (r, 2 * per)], [F32], 128))
    other_halves = _swap_with_sibling("join_core_sums", half_sums)
    for n, a, b in zip(axes, half_sums, other_halves):
        g = jnp.where(core == 0, jnp.concatenate([a, b], axis=0), jnp.concatenate([b, a], axis=0))
        shp = w[n].shape
        res = _elementwise(f"adamw_{n}", lambda g, w_, m_, v_: (g,) + _adamw(w_, g, m_, v_),
                           [g.reshape(shp), w[n], m[n], v[n]], [F32] * 4, 256)
        for kind, arr in zip(kinds, res):
            out[kind + "_" + n] = arr.reshape(shp)

    slots = _gather_from_all("gather_small_grads", _pack_small(grads, grads["dn_conv"]))

    def small_body(s_ref, w_ref, m_ref, v_ref, sum_out, g_out, d_out, m_out, v_out):
        total = s_ref[0]
        for d in range(1, N_DEV):
            total = total + s_ref[d]
        sum_out[...] = total
        g = total[:SMALL_ROWS]
        for o, r in zip((g_out, d_out, m_out, v_out), (g,) + _adamw(w_ref[...], g, m_ref[...], v_ref[...])):
            o[...] = r

    res = pl.pallas_call(small_body, name="adamw_replicated",
                         out_shape=[jax.ShapeDtypeStruct((SMALL_GRAD_ROWS, D_MODEL), F32)]
                         + [jax.ShapeDtypeStruct((SMALL_ROWS, D_MODEL), F32)] * 4)(
        slots, _pack_small(w), _pack_small(m), _pack_small(v))
    for kind, block in zip(kinds, res[1:]):
        for n, arr in _unpack_small(block).items():
            out[kind + "_" + n] = arr
    conv_sum = res[0][SMALL_ROWS:SMALL_ROWS + CONV_ROWS].reshape(CONV_WIDTH, DN_QKV)
    cols = DN_QKV // N_CHIPS
    chip = 2 * lax.axis_index("x") + lax.axis_index("y")
    conv_mine = lax.dynamic_slice_in_dim(conv_sum, chip * cols, cols, axis=1)
    res = _elementwise("adamw_dn_conv", lambda g, w_, m_, v_: (g,) + _adamw(w_, g, m_, v_),
                       [conv_mine, w["dn_conv"][0], m["dn_conv"][0], v["dn_conv"][0]], [F32] * 4, CONV_WIDTH)
    for kind, arr in zip(kinds, res):
        out[kind + "_dn_conv"] = arr[None]

    return (loss, grad_x[None],
            *[out["grad_" + n] for n in WEIGHTS], *[out["delta_" + n] for n in WEIGHTS],
            *[out["new_m_" + n] for n in WEIGHTS], *[out["new_v_" + n] for n in WEIGHTS])
```

```python
import functools
import math

import jax
import jax.numpy as jnp
from jax import lax
from jax.experimental import pallas as pl
from jax.experimental.pallas import tpu as pltpu

F32 = jnp.float32
BF16 = jnp.bfloat16
HIGHEST = lax.Precision.HIGHEST

D_MODEL = 1024
EPS = 1e-6
SWA_GROUPS = ((128, 1), (512, 4), (2048, 16))
A_HEADS = 8
A_HEAD_DIM = 64
A_WIDTH = A_HEADS * A_HEAD_DIM
ROPE_DIM = A_HEAD_DIM // 4
ROPE_THETA = 500000.0
BAND = 128
DN_HEADS = 8
DN_HEAD_DIM = 128
DN_WIDTH = DN_HEADS * DN_HEAD_DIM
DN_QKV = 3 * DN_WIDTH
DN_AB_PAD = 128
DN_IN_PAD = DN_WIDTH + DN_QKV + DN_AB_PAD
DN_QKV0 = DN_WIDTH
DN_AB0 = DN_WIDTH + DN_QKV
DN_HB = 8
CONV_WIDTH = 4
CHUNK = 64
PLE_DIM = 256
D_FF = 4 * D_MODEL

ADAM_LR = 0.001
ADAM_B1 = 0.9
ADAM_B2 = 0.999
ADAM_EPS = 1e-08
ADAM_WD = 0.01
ADAM_STEP = 10

N_CHIPS = 4
N_DEV = 8
VMEM_LIMIT = 48 * 1024 * 1024
MESH = pl.DeviceIdType.MESH

SHARDED = (
    ("attn_w_qkv", 2), ("attn_w_o", 2), ("dn_w_in", 2), ("dn_conv", 2), ("dn_w_o", 1),
    ("w_up", 2), ("w_down", 1), ("w_ple", 2), ("w_ple_gate", 1))
ATTN_MATRICES = ("attn_w_qkv", "attn_w_o")
CARGO_GROUPS = (("w_up",), ("w_down",), ("dn_w_in", "dn_w_o", "w_ple", "w_ple_gate"))
REPLICATED = ("mix_norm", "attn_q_gain", "attn_k_gain", "dn_a_log", "dn_dt_bias", "dn_o_gain",
              "mlp_norm", "ple_norm")
WEIGHTS = ("mix_norm", "attn_w_qkv", "attn_q_gain", "attn_k_gain", "attn_w_o", "dn_w_in", "dn_conv",
           "dn_a_log", "dn_dt_bias", "dn_o_gain", "dn_w_o", "mlp_norm", "w_up", "w_down", "ple_norm",
           "w_ple", "w_ple_gate")


def _cparams(sem=None):
    return pltpu.CompilerParams(dimension_semantics=sem, vmem_limit_bytes=VMEM_LIMIT)


def _pick(n, cap, quantum=128):
    best = None
    for t in range(quantum, min(n, cap) + 1, quantum):
        if n % t == 0:
            best = t
    return n if best is None else best


_DIMS = {"nn": ((1,), (0,)), "nt": ((1,), (1,)), "tn": ((0,), (0,))}


def _mm(name, a, b, mode, out_dtypes=(F32,), extras=(), epilogue=None, rows=(), n_sums=0, tm_cap=1024):
    if mode == "nn":
        (M, K), (K2, N) = a.shape, b.shape
    elif mode == "nt":
        (M, K), (N, K2) = a.shape, b.shape
    else:
        (K, M), (K2, N) = a.shape, b.shape
    assert K == K2, (name, a.shape, b.shape)
    tn = _pick(N, 1536)
    if mode == "tn":
        tm, tk = _pick(M, tm_cap), _pick(K, 2048)
    elif tn == N and K > 1536:
        tm, tk = _pick(M, min(tm_cap, 512)), K
    else:
        tm, tk = _pick(M, tm_cap), _pick(K, 1536)
    nk = K // tk
    assert n_sums == 0 or tn == N, name
    if mode == "nn":
        a_spec = pl.BlockSpec((tm, tk), lambda i, j, k: (i, k))
        b_spec = pl.BlockSpec((tk, tn), lambda i, j, k: (k, j))
    elif mode == "nt":
        a_spec = pl.BlockSpec((tm, tk), lambda i, j, k: (i, k))
        b_spec = pl.BlockSpec((tn, tk), lambda i, j, k: (j, k))
    else:
        a_spec = pl.BlockSpec((tk, tm), lambda i, j, k: (k, i))
        b_spec = pl.BlockSpec((tk, tn), lambda i, j, k: (k, j))
    o_spec = pl.BlockSpec((tm, tn), lambda i, j, k: (i, j))
    r_spec = pl.BlockSpec((1, tn), lambda i, j, k: (0, j))
    n_extra, n_out = len(extras) + len(rows), len(out_dtypes)
    dims = (_DIMS[mode], ((), ()))

    def body(a_ref, b_ref, *rest):
        extra_refs, out_refs = rest[:n_extra], rest[n_extra:n_extra + n_out]
        sum_refs = rest[n_extra + n_out:n_extra + n_out + n_sums]
        i, k = pl.program_id(0), pl.program_id(2)
        part = lax.dot_general(a_ref[...].astype(BF16), b_ref[...].astype(BF16), dims, preferred_element_type=F32)

        def finish(total):
            vals = (total,) if epilogue is None else epilogue(total, *[e[...] for e in extra_refs])
            for o, v in zip(out_refs, vals[:n_out]):
                o[...] = v.astype(o.dtype)
            for s, v in zip(sum_refs, vals[n_out:]):
                @pl.when(i == 0)
                def _():
                    s[...] = v

                @pl.when(i > 0)
                def _():
                    s[...] += v

        if nk == 1:
            finish(part)
            return
        acc = rest[-1]

        @pl.when(k == 0)
        def _():
            acc[...] = part

        @pl.when(jnp.logical_and(k > 0, k < nk - 1))
        def _():
            acc[...] += part

        @pl.when(k == nk - 1)
        def _():
            finish(acc[...] + part)

    outs = pl.pallas_call(
        body, name=name, grid=(M // tm, N // tn, nk),
        in_specs=[a_spec, b_spec] + [o_spec] * len(extras) + [r_spec] * len(rows),
        out_specs=[o_spec] * n_out + [r_spec] * n_sums,
        out_shape=[jax.ShapeDtypeStruct((M, N), dt) for dt in out_dtypes]
        + [jax.ShapeDtypeStruct((1, N), F32)] * n_sums,
        scratch_shapes=[pltpu.VMEM((tm, tn), F32)] if nk > 1 else [],
        compiler_params=_cparams(("arbitrary" if n_sums else "parallel", "parallel", "arbitrary")),
    )(a, b, *extras, *rows)
    return outs[0] if n_out + n_sums == 1 else outs


def _rowwise(name, fn, rows, bcast, row_outs, acc_outs=(), tm=256, n_rows=None, cargo=None):
    rows = [r if isinstance(r, tuple) else (r, r.shape[1], 0) for r in rows]
    rows = [r if len(r) == 4 else r + (0,) for r in rows]
    S = rows[0][0].shape[0] if n_rows is None else n_rows
    tm = min(tm, S)
    assert S % tm == 0 and all(r[3] % tm == 0 for r in rows), (name, S, tm)
    n_row, n_bc, n_ro, n_acc = len(rows), len(bcast), len(row_outs), len(acc_outs)
    in_specs = [pl.BlockSpec((tm, w), functools.partial(lambda i, cb, rb: (i + rb, cb), cb=cb, rb=r0 // tm))
                for _, w, cb, r0 in rows]
    in_specs += [pl.BlockSpec(b.shape, lambda i: (0, 0)) for b in bcast]
    out_specs = [pl.BlockSpec((tm, c), lambda i: (i, 0)) for c, _ in row_outs]
    out_specs += [pl.BlockSpec(s, lambda i: (0, 0)) for s in acc_outs]
    out_shape = [jax.ShapeDtypeStruct((S, c), dt) for c, dt in row_outs]
    out_shape += [jax.ShapeDtypeStruct(s, F32) for s in acc_outs]

    def body(*refs):
        ins = [r[...] for r in refs[:n_row + n_bc]]
        outs = refs[n_row + n_bc:]
        vals = fn(*ins)
        if not isinstance(vals, (tuple, list)):
            vals = (vals,)
        for o, v in zip(outs[:n_ro], vals[:n_ro]):
            o[...] = v.astype(o.dtype)
        if n_acc:
            @pl.when(pl.program_id(0) == 0)
            def _():
                for o in outs[n_ro:]:
                    o[...] = jnp.zeros_like(o)
            for o, v in zip(outs[n_ro:], vals[n_ro:]):
                o[...] += v

    n_own = n_ro + n_acc
    body, c_in_specs, c_out_specs, c_out_shape, c_scratch, c_ins = _carry(cargo, n_row + n_bc, n_own, 0, body, S // tm)
    outs = pl.pallas_call(
        body, name=name, grid=(S // tm,), in_specs=in_specs + c_in_specs, out_specs=out_specs + c_out_specs,
        out_shape=out_shape + c_out_shape, scratch_shapes=c_scratch,
        compiler_params=_cparams(("arbitrary",) if n_acc or cargo is not None else ("parallel",)),
    )(*[r[0] for r in rows], *bcast, *c_ins)
    own = outs[0] if n_own == 1 else outs[:n_own]
    return own if cargo is None else (own, outs[n_own:])


def _sigmoid(x):
    return 1.0 / (1.0 + jnp.exp(-x))


def _silu(x):
    return x * _sigmoid(x)


def _softplus(x):
    return jnp.maximum(x, 0.0) + jnp.log(1.0 + jnp.exp(-jnp.abs(x)))


def _rms_fwd_fn(x, g):
    r = lax.rsqrt(jnp.mean(x * x, axis=-1, keepdims=True) + EPS)
    return (x * r) * g


def _rms_bwd_fn(x, dres, *rest):
    dh, g = sum(rest[:-1]), rest[-1]
    r = lax.rsqrt(jnp.mean(x * x, axis=-1, keepdims=True) + EPS)
    xh = x * r
    dxh = dh * g
    dx = dres + r * (dxh - xh * jnp.mean(dxh * xh, axis=-1, keepdims=True))
    return dx, dx, jnp.sum(dh * xh, axis=0, keepdims=True)


def _rms_fwd(name, x, gain, cargo=None):
    return _rowwise(name, _rms_fwd_fn, [x], [gain.reshape(1, -1)], [(x.shape[1], BF16)], cargo=cargo)


def _rms_bwd(name, x, gain, dres, dhs, cargo=None):
    return _rowwise(name, _rms_bwd_fn, [x, dres] + list(dhs), [gain.reshape(1, -1)],
                    [(x.shape[1], F32), (x.shape[1], BF16)], [(1, x.shape[1])], cargo=cargo)


def _relu2_epilogue(acc):
    r = jnp.maximum(acc, 0.0)
    return (r * r,)


def _relu2_bwd_epilogue(acc, a):
    return (acc * (2.0 * jnp.sqrt(a.astype(F32))),)


def _ple_fwd_fn(x, pp, zg):
    return x + pp * _sigmoid(zg)


def _ple_norm_fwd_fn(x, pp, zg, gain):
    out = _ple_fwd_fn(x, pp, zg)
    return out, _rms_fwd_fn(out, gain)


def _ple_bwd_fn(dx, pp, zg):
    gate = _sigmoid(zg)
    return dx * gate, dx * pp * gate * (1.0 - gate)


def _adamw(w, g, m, v):
    m = ADAM_B1 * m + (1.0 - ADAM_B1) * g
    v = ADAM_B2 * v + (1.0 - ADAM_B2) * jnp.square(g)
    m_hat = m / (1.0 - ADAM_B1 ** ADAM_STEP)
    v_hat = v / (1.0 - ADAM_B2 ** ADAM_STEP)
    delta = -ADAM_LR * (m_hat / (jnp.sqrt(v_hat) + ADAM_EPS) + ADAM_WD * w)
    return delta, m, v


def _lane_take(x, offset):
    n = x.shape[-1]
    return pltpu.roll(x, (-offset) % n, 1)


def _head_lane(shape):
    return lax.broadcasted_iota(jnp.int32, shape, 1) % A_HEAD_DIM


def _rope_partner(x):
    lane = _head_lane(x.shape)
    return jnp.where(lane < ROPE_DIM // 2, _lane_take(x, ROPE_DIM // 2),
                     jnp.where(lane < ROPE_DIM, _lane_take(x, -(ROPE_DIM // 2)), 0.0))


def _head_mean(x, bd):
    hi = x.astype(BF16)
    lo = (x - hi.astype(F32)).astype(BF16)
    b = bd.astype(BF16)
    return jnp.dot(hi, b, preferred_element_type=F32) + jnp.dot(lo, b, preferred_element_type=F32)


def _fold_heads(row):
    out = row[:, :A_HEAD_DIM]
    for h in range(1, A_HEADS):
        out = out + row[:, h * A_HEAD_DIM:(h + 1) * A_HEAD_DIM]
    return out


def _all_heads(t):
    return jnp.concatenate([t] * (A_WIDTH // t.shape[1]), axis=1)


def _qk_prep_fwd_fn(qkv, ct, st, gq, gk, bd):
    ct, st = _all_heads(ct), _all_heads(st)

    def one(t, g):
        n = t * lax.rsqrt(_head_mean(t * t, bd) + EPS) * g
        return n * ct + _rope_partner(n) * st
    q, k, v = qkv[:, :A_WIDTH], qkv[:, A_WIDTH:2 * A_WIDTH], qkv[:, 2 * A_WIDTH:]
    return one(q, gq), one(k, gk), v


def _qk_prep_bwd_fn(qkv, ct, st, dq, dk, dv, gq, gk, bd):
    ct, st = _all_heads(ct), _all_heads(st)

    def one(t, g, dy):
        r = lax.rsqrt(_head_mean(t * t, bd) + EPS)
        nh = t * r
        dn = dy * ct + _rope_partner(dy * st)
        dg = jnp.sum(dn * nh, axis=0, keepdims=True)
        dnh = dn * g
        return r * (dnh - nh * _head_mean(dnh * nh, bd)), _fold_heads(dg)
    q, k = qkv[:, :A_WIDTH], qkv[:, A_WIDTH:2 * A_WIDTH]
    dq_raw, dgq = one(q, gq, dq)
    dk_raw, dgk = one(k, gk, dk)
    return jnp.concatenate([dq_raw, dk_raw, dv], axis=1), dgq, dgk


_BATCH_DIMS = {"nn": ((2,), (1,)), "nt": ((2,), (2,)), "tn": ((1,), (1,))}


def _bdot(a, b, mode, precision=None):
    return lax.dot_general(a, b, (_BATCH_DIMS[mode], ((0,), (0,))), precision=precision,
                           preferred_element_type=F32)


def _attn_cols(h):
    return slice(h * A_HEAD_DIM, (h + 1) * A_HEAD_DIM)


def _attn_heads(ref):
    return jnp.stack([ref[:, _attn_cols(h)] for h in range(A_HEADS)])


def _band_masks():
    qi = lax.broadcasted_iota(jnp.int32, (BAND, BAND), 0)
    kj = lax.broadcasted_iota(jnp.int32, (BAND, BAND), 1)
    return kj <= qi, kj >= qi


def _attn_fwd(name, q, k, v, blocks_per_class, cargo=None):
    S = q.shape[0]
    nblk = S // BAND
    scale = A_HEAD_DIM ** -0.5

    def body(q_ref, kp_ref, kc_ref, vp_ref, vc_ref, o_ref, l_ref):
        i = pl.program_id(0)
        has_prev = (i % blocks_per_class) != 0
        m_cur, m_prev = _band_masks()
        m_prev = jnp.logical_and(m_prev, has_prev)
        q, kc, kp, vc, vp = (_attn_heads(r) for r in (q_ref, kc_ref, kp_ref, vc_ref, vp_ref))
        s_c = jnp.where(m_cur[None], _bdot(q, kc, "nt") * scale, -jnp.inf)
        s_p = jnp.where(m_prev[None], _bdot(q, kp, "nt") * scale, -jnp.inf)
        m = jnp.maximum(jnp.max(s_c, axis=-1, keepdims=True), jnp.max(s_p, axis=-1, keepdims=True))
        e_c, e_p = jnp.exp(s_c - m), jnp.exp(s_p - m)
        l = jnp.sum(e_c, axis=-1, keepdims=True) + jnp.sum(e_p, axis=-1, keepdims=True)
        o = _bdot((e_c / l).astype(BF16), vc, "nn") + _bdot((e_p / l).astype(BF16), vp, "nn")
        lse = m + jnp.log(l)
        for h in range(A_HEADS):
            o_ref[:, _attn_cols(h)] = o[h]
            l_ref[:, _attn_cols(h)] = jnp.broadcast_to(lse[h], (BAND, A_HEAD_DIM))

    cur = pl.BlockSpec((BAND, A_WIDTH), lambda i: (i, 0))
    prev = pl.BlockSpec((BAND, A_WIDTH), lambda i: (jnp.maximum(i - 1, 0), 0))
    body, c_in_specs, c_out_specs, c_out_shape, c_scratch, c_ins = _carry(cargo, 5, 2, 0, body, nblk)
    outs = pl.pallas_call(
        body, name=name, grid=(nblk,), in_specs=[cur, prev, cur, prev, cur] + c_in_specs,
        out_specs=[cur, cur] + c_out_specs,
        out_shape=[jax.ShapeDtypeStruct((S, A_WIDTH), F32)] * 2 + c_out_shape, scratch_shapes=c_scratch,
        compiler_params=_cparams(("arbitrary",)),
    )(q, k, k, v, v, *c_ins)
    return outs[0], outs[1], outs[2:]


def _carry(cargo, n_in, n_out, n_scratch, body, steps):
    if cargo is None:
        return body, [], [], [], [], []
    n_ci, n_co = len(cargo.ins), len(cargo.out_shape)

    def carrying(*refs):
        refs = list(refs)
        ins, refs = refs[:n_in], refs[n_in:]
        c_ins, refs = refs[:n_ci], refs[n_ci:]
        outs, refs = refs[:n_out], refs[n_out:]
        c_outs, refs = refs[:n_co], refs[n_co:]
        scratch, sems = refs[:n_scratch], refs[n_scratch:]

        @pl.when(pl.program_id(0) == 0)
        def _():
            cargo.start(c_ins, c_outs, sems)

        body(*ins, *outs, *scratch)

        @pl.when(pl.program_id(0) == steps - 1)
        def _():
            cargo.finish(c_ins, c_outs, sems)

    any_spec = pl.BlockSpec(memory_space=pl.ANY)
    return carrying, [any_spec] * n_ci, [any_spec] * n_co, list(cargo.out_shape), list(cargo.scratch), list(cargo.ins)


def _attn_bwd(name, q, k, v, o, lse, do, dlse, blocks_per_class, cargo=None):
    S = q.shape[0]
    nblk = S // BAND
    scale = A_HEAD_DIM ** -0.5

    def body(q_ref, kp_ref, kc_ref, vp_ref, vc_ref, o_ref, l_ref, do_ref, dl_ref,
             dq_ref, dk_ref, dv_ref, ck, cv):
        i = pl.program_id(0)

        @pl.when(i == 0)
        def _():
            ck[...] = jnp.zeros_like(ck)
            cv[...] = jnp.zeros_like(cv)

        @pl.when(i == nblk)
        def _():
            dk_ref[...] = ck[...]
            dv_ref[...] = cv[...]

        @pl.when(i < nblk)
        def _():
            has_prev = (i % blocks_per_class) != 0
            m_cur, m_prev = _band_masks()
            m_prev = jnp.logical_and(m_prev, has_prev)
            q, kc, kp, vc, vp = (_attn_heads(r) for r in (q_ref, kc_ref, kp_ref, vc_ref, vp_ref))
            do, o, dl = _attn_heads(do_ref), _attn_heads(o_ref), _attn_heads(dl_ref)
            lse = jnp.max(_attn_heads(l_ref), axis=-1, keepdims=True)
            p_c = jnp.where(m_cur[None], jnp.exp(_bdot(q, kc, "nt") * scale - lse), 0.0)
            p_p = jnp.where(m_prev[None], jnp.exp(_bdot(q, kp, "nt") * scale - lse), 0.0)
            corr = jnp.sum(dl, axis=-1, keepdims=True) - jnp.sum(do * o, axis=-1, keepdims=True)
            dob = do.astype(BF16)
            ds_c = (p_c * (_bdot(dob, vc, "nt") + corr)).astype(BF16)
            ds_p = (p_p * (_bdot(dob, vp, "nt") + corr)).astype(BF16)
            dq = (_bdot(ds_c, kc, "nn") + _bdot(ds_p, kp, "nn")) * scale
            dk_p, dk_c = _bdot(ds_p, q, "tn") * scale, _bdot(ds_c, q, "tn") * scale
            dv_p, dv_c = _bdot(p_p.astype(BF16), dob, "tn"), _bdot(p_c.astype(BF16), dob, "tn")
            for h in range(A_HEADS):
                sl = _attn_cols(h)
                dq_ref[:, sl] = dq[h]
                dk_ref[:, sl] = ck[:, sl] + dk_p[h]
                dv_ref[:, sl] = cv[:, sl] + dv_p[h]
                ck[:, sl] = dk_c[h]
                cv[:, sl] = dv_c[h]

    last = nblk - 1
    cur = pl.BlockSpec((BAND, A_WIDTH), lambda i: (jnp.minimum(i, last), 0))
    prev = pl.BlockSpec((BAND, A_WIDTH), lambda i: (jnp.minimum(jnp.maximum(i - 1, 0), last), 0))
    body, c_in_specs, c_out_specs, c_out_shape, c_scratch, c_ins = _carry(cargo, 9, 3, 2, body, nblk + 1)
    outs = pl.pallas_call(
        body, name=name, grid=(nblk + 1,),
        in_specs=[cur, prev, cur, prev, cur, cur, cur, cur, cur] + c_in_specs,
        out_specs=[cur, prev, prev] + c_out_specs,
        out_shape=[jax.ShapeDtypeStruct((S, A_WIDTH), F32)] * 3 + c_out_shape,
        scratch_shapes=[pltpu.VMEM((BAND, A_WIDTH), F32)] * 2 + c_scratch,
        compiler_params=_cparams(("arbitrary",)),
    )(q, k, k, v, v, o, lse, do, dlse, *c_ins)
    return outs[0], outs[1], outs[2], outs[3:]


def _merge_fwd_fn(o0, o1, o2, l0, l1, l2):
    m = jnp.maximum(jnp.maximum(l0, l1), l2)
    e0, e1, e2 = jnp.exp(l0 - m), jnp.exp(l1 - m), jnp.exp(l2 - m)
    return (e0 * o0 + e1 * o1 + e2 * o2) / (e0 + e1 + e2)


def _merge_bwd_fn(o0, o1, o2, l0, l1, l2, dom):
    m = jnp.maximum(jnp.maximum(l0, l1), l2)
    e0, e1, e2 = jnp.exp(l0 - m), jnp.exp(l1 - m), jnp.exp(l2 - m)
    den = e0 + e1 + e2
    w0, w1, w2 = e0 / den, e1 / den, e2 / den
    dw0, dw1, dw2 = dom * o0, dom * o1, dom * o2
    mean = w0 * dw0 + w1 * dw1 + w2 * dw2
    return w0 * dom, w1 * dom, w2 * dom, w0 * (dw0 - mean), w1 * (dw1 - mean), w2 * (dw2 - mean)


def _to_classes(t, d):
    if d == 1:
        return t
    S, C = t.shape
    return t.reshape(S // d, d, C).transpose(1, 0, 2).reshape(S, C)


def _from_classes(t, d):
    if d == 1:
        return t
    S, C = t.shape
    return t.reshape(d, S // d, C).transpose(1, 0, 2).reshape(S, C)


def _rope_lane_tables(positions):
    inv_freq = ROPE_THETA ** (-jnp.arange(0, ROPE_DIM, 2, dtype=F32) / ROPE_DIM)
    ang = positions.astype(F32)[:, None] * inv_freq
    cos, sin = jnp.cos(ang), jnp.sin(ang)
    S = positions.shape[0]
    rest = A_HEAD_DIM - ROPE_DIM
    ct = jnp.concatenate([cos, cos, jnp.ones((S, rest), F32)], axis=1)
    st = jnp.concatenate([-sin, sin, jnp.zeros((S, rest), F32)], axis=1)
    return jnp.tile(ct, (1, 2)), jnp.tile(st, (1, 2))


def _head_mean_matrix():
    r = jnp.arange(A_WIDTH) // A_HEAD_DIM
    return (r[:, None] == r[None, :]).astype(F32) * (1.0 / A_HEAD_DIM)


def _conv_fwd(name, proj, w):
    S = proj.shape[0]
    tm, tc = min(512, S), 1024
    per8 = tm // 8
    off = DN_QKV0 // tc

    def body(x_ref, halo_ref, w_ref, o_ref, xs):
        i = pl.program_id(0)
        xs[0:8, :] = jnp.where(i > 0, halo_ref[...], 0.0)
        xs[8:, :] = x_ref[...]
        acc = w_ref[0:1, :] * xs[pl.ds(8 - 3, tm), :]
        for j in range(1, CONV_WIDTH):
            acc = acc + w_ref[j:j + 1, :] * xs[pl.ds(8 - 3 + j, tm), :]
        o_ref[...] = acc

    return pl.pallas_call(
        body, name=name, grid=(S // tm, DN_QKV // tc),
        in_specs=[pl.BlockSpec((tm, tc), lambda i, j: (i, j + off)),
                  pl.BlockSpec((8, tc), lambda i, j: (jnp.maximum(i * per8 - 1, 0), j + off)),
                  pl.BlockSpec((CONV_WIDTH, tc), lambda i, j: (0, j))],
        out_specs=pl.BlockSpec((tm, tc), lambda i, j: (i, j)),
        out_shape=jax.ShapeDtypeStruct((S, DN_QKV), F32),
        scratch_shapes=[pltpu.VMEM((tm + 8, tc), F32)],
        compiler_params=_cparams(("parallel", "parallel")),
    )(proj, proj, w)


def _conv_bwd(name, proj, dpre, w):
    S = proj.shape[0]
    tm, tc = min(512, S), 1024
    per8 = tm // 8
    off = DN_QKV0 // tc
    last8 = S // 8 - 1
    nrow = S // tm

    def body(x_ref, xh_ref, d_ref, dh_ref, w_ref, dx_ref, dw_ref, xs, ds):
        i = pl.program_id(1)
        xs[0:8, :] = jnp.where(i > 0, xh_ref[...], 0.0)
        xs[8:, :] = x_ref[...]
        ds[0:tm, :] = d_ref[...]
        ds[tm:, :] = jnp.where(i < nrow - 1, dh_ref[...], 0.0)
        d = d_ref[...]
        acc = w_ref[0:1, :] * ds[pl.ds(3, tm), :]
        for j in range(1, CONV_WIDTH):
            acc = acc + w_ref[j:j + 1, :] * ds[pl.ds(3 - j, tm), :]
        dx_ref[...] = acc.astype(dx_ref.dtype)

        @pl.when(i == 0)
        def _():
            dw_ref[...] = jnp.zeros_like(dw_ref)

        for j in range(CONV_WIDTH):
            dw_ref[j:j + 1, :] += jnp.sum(d * xs[pl.ds(8 - 3 + j, tm), :], axis=0, keepdims=True)

    return pl.pallas_call(
        body, name=name, grid=(DN_QKV // tc, nrow),
        in_specs=[pl.BlockSpec((tm, tc), lambda j, i: (i, j + off)),
                  pl.BlockSpec((8, tc), lambda j, i: (jnp.maximum(i * per8 - 1, 0), j + off)),
                  pl.BlockSpec((tm, tc), lambda j, i: (i, j)),
                  pl.BlockSpec((8, tc), lambda j, i: (jnp.minimum((i + 1) * per8, last8), j)),
                  pl.BlockSpec((CONV_WIDTH, tc), lambda j, i: (0, j))],
        out_specs=[pl.BlockSpec((tm, tc), lambda j, i: (i, j)),
                   pl.BlockSpec((CONV_WIDTH, tc), lambda j, i: (0, j))],
        out_shape=[jax.ShapeDtypeStruct((S, DN_QKV), BF16), jax.ShapeDtypeStruct((CONV_WIDTH, DN_QKV), F32)],
        scratch_shapes=[pltpu.VMEM((tm + 8, tc), F32)] * 2,
        compiler_params=_cparams(("parallel", "arbitrary")),
    )(proj, proj, dpre, dpre, w)


def _gate_lane(shape):
    return lax.broadcasted_iota(jnp.int32, shape, 1)


GATES_ROWS = 256


def _chunk_cumsum_matrix():
    r = jnp.arange(GATES_ROWS)
    return ((r[:, None] >= r[None, :]) & (r[:, None] // CHUNK == r[None, :] // CHUNK)).astype(F32)


def _gates_fwd_fn(ab, alog, dt, cum):
    g = -jnp.exp(alog) * _softplus(ab + dt)
    gc = jnp.dot(cum, g, precision=HIGHEST, preferred_element_type=F32)
    return jnp.where(_gate_lane(ab.shape) < DN_HEADS, gc, _sigmoid(ab))


def _gates_bwd_fn(ab, dgb, alog, dt, cum):
    lane = _gate_lane(ab.shape)
    is_g = lane < DN_HEADS
    neg_a = -jnp.exp(alog)
    sp = _softplus(ab + dt)
    dsp = _sigmoid(ab + dt)
    beta = _sigmoid(ab)
    dgc = jnp.where(is_g, dgb, 0.0)
    dg = lax.dot_general(cum, dgc, (_DIMS["tn"], ((), ())), precision=HIGHEST, preferred_element_type=F32)
    dab = jnp.where(is_g, dg * neg_a * dsp, jnp.where(lane < 2 * DN_HEADS, dgb * beta * (1.0 - beta), 0.0))
    d_alog = jnp.sum(dg * neg_a * sp, axis=0, keepdims=True)
    d_dt = jnp.sum(dg * neg_a * dsp, axis=0, keepdims=True)
    return dab, d_alog, d_dt


def _chunk_math(precision):
    def dg(a, b, mode, prec=precision):
        return _bdot(a, b, mode, prec)

    @jax.custom_vjp
    def nn(a, b):
        return dg(a, b, "nn")

    @jax.custom_vjp
    def nt(a, b):
        return dg(a, b, "nt")

    @jax.custom_vjp
    def tn(a, b):
        return dg(a, b, "tn")

    nn.defvjp(lambda a, b: (nn(a, b), (a, b)), lambda r, g: (nt(g, r[1]), tn(r[0], g)))
    nt.defvjp(lambda a, b: (nt(a, b), (a, b)), lambda r, g: (nn(g, r[1]), tn(g, r[0])))
    tn.defvjp(lambda a, b: (tn(a, b), (a, b)), lambda r, g: (nt(r[1], g), nn(r[0], g)))

    def split(x):
        hi = x.astype(BF16)
        return hi, (x - hi.astype(F32)).astype(BF16)

    def fine(a, b, mode):
        ah, al = split(a)
        bh, bl = split(b)
        return dg(ah, bh, mode, None) + (dg(ah, bl, mode, None) + dg(al, bh, mode, None))

    def unit_lower_inverse(a):
        row = lax.broadcasted_iota(jnp.int32, a.shape, 1)
        col = lax.broadcasted_iota(jnp.int32, a.shape, 2)
        x = -a
        p = jnp.where(row == col, 1.0, 0.0) + x
        for _ in range(int(math.log2(CHUNK)) - 1):
            x = fine(x, x, "nn")
            p = p + fine(p, x, "nn")
        return p

    @jax.custom_vjp
    def solve2(a, ti, r1, r2):
        return fine(ti, r1, "nn"), fine(ti, r2, "nn")

    def solve2_fwd(a, ti, r1, r2):
        s1, s2 = fine(ti, r1, "nn"), fine(ti, r2, "nn")
        return (s1, s2), (ti, s1, s2)

    def solve2_bwd(res, g):
        ti, s1, s2 = res
        d1, d2 = fine(ti, g[0], "tn"), fine(ti, g[1], "tn")
        return -(fine(d1, s1, "nt") + fine(d2, s2, "nt")), jnp.zeros_like(ti), d1, d2

    solve2.defvjp(solve2_fwd, solve2_bwd)

    def chunk_fn(pq, pk, pv, z, g_col, b_col, g_row, ogain, s_in, inverse=None):
        nb = pq.shape[0]
        sq = (nb, CHUNK, CHUNK)
        row = lax.broadcasted_iota(jnp.int32, sq, 1)
        col = lax.broadcasted_iota(jnp.int32, sq, 2)
        lower, strict = row >= col, row > col
        q, k, v = _silu(pq), _silu(pk), _silu(pv)
        q = q * lax.rsqrt(jnp.sum(q * q, axis=-1, keepdims=True) + EPS) * (DN_HEAD_DIM ** -0.5)
        k = k * lax.rsqrt(jnp.sum(k * k, axis=-1, keepdims=True) + EPS)
        gc_wide = jnp.broadcast_to(g_col, pq.shape)
        gc_i = jnp.broadcast_to(g_col, sq)
        gc_j = jnp.broadcast_to(g_row, sq)
        is_last = lax.broadcasted_iota(jnp.int32, pq.shape, 1) == CHUNK - 1
        g_last = jnp.sum(jnp.where(is_last, gc_wide, 0.0), axis=1, keepdims=True)
        decay = jnp.exp(jnp.where(lower, gc_i - gc_j, -jnp.inf))
        kb = k * b_col
        a_mat = jnp.where(strict, nt(kb, k) * decay, 0.0)
        eg = jnp.exp(gc_wide)
        ti = unit_lower_inverse(a_mat) if inverse is None else inverse
        u, w = solve2(a_mat, ti, v * b_col, kb * eg)
        attn = nt(q, k) * decay
        q_dec = q * eg
        k_dec = k * jnp.exp(g_last - gc_wide)
        c_dec = jnp.exp(g_last)
        v_new = u - nn(w, s_in)
        o = nn(q_dec, s_in) + nn(attn, v_new)
        s_out = s_in * c_dec + tn(k_dec, v_new)
        y = o * lax.rsqrt(jnp.mean(o * o, axis=-1, keepdims=True) + EPS) * ogain * _silu(z)
        return (y, s_out, ti) if inverse is None else (y, s_out)

    return chunk_fn


DN_PRECISION = None


def _chunk_specs(n_of):
    groups = DN_HEADS // DN_HB
    wide = DN_HB * DN_HEAD_DIM
    hd = pl.BlockSpec((CHUNK, wide), lambda h, n: (n_of(n), h))
    specs = dict(
        pq=hd,
        pk=pl.BlockSpec((CHUNK, wide), lambda h, n: (n_of(n), groups + h)),
        pv=pl.BlockSpec((CHUNK, wide), lambda h, n: (n_of(n), 2 * groups + h)),
        z=hd,
        gates=pl.BlockSpec((CHUNK, DN_AB_PAD), lambda h, n: (n_of(n), 0)),
        row=pl.BlockSpec((DN_HB, None, 1, CHUNK), lambda h, n: (h, n_of(n), 0, 0)),
        gain=pl.BlockSpec((1, DN_HEAD_DIM), lambda h, n: (0, 0)),
        state=pl.BlockSpec((DN_HB, None, DN_HEAD_DIM, DN_HEAD_DIM), lambda h, n: (h, n_of(n), 0, 0)),
        inverse=pl.BlockSpec((DN_HB, None, CHUNK, CHUNK), lambda h, n: (h, n_of(n), 0, 0)),
        qkv=pl.BlockSpec((CHUNK, DN_QKV), lambda h, n: (n_of(n), 0)),
        head=hd,
    )
    return specs


def _head_cols(j):
    return slice(j * DN_HEAD_DIM, (j + 1) * DN_HEAD_DIM)


def _split_heads(ref):
    return jnp.stack([ref[:, _head_cols(j)] for j in range(DN_HB)])


def _gate_columns(gates, first_lane):
    lane = lax.broadcasted_iota(jnp.int32, gates.shape, 1)
    return jnp.stack([jnp.sum(jnp.where(lane == first_lane + h, gates, 0.0), axis=-1, keepdims=True)
                      for h in range(DN_HEADS)])


def _gate_lanes(columns, first_lane):
    shape = (columns.shape[1], DN_AB_PAD)
    lane = lax.broadcasted_iota(jnp.int32, shape, 1)
    out = jnp.zeros(shape, F32)
    for h in range(DN_HEADS):
        out = out + jnp.where(lane == first_lane + h, columns[h], 0.0)
    return out


def _chunk_fwd(name, pre, proj, gates, g_row, ogain):
    assert DN_HB == DN_HEADS
    S = pre.shape[0]
    N = S // CHUNK
    chunk_fn = _chunk_math(DN_PRECISION)
    sp = _chunk_specs(lambda n: n)

    def body(pq, pk, pv, z, gb, gr, og, y_ref, sin_ref, inv_ref, st):
        @pl.when(pl.program_id(1) == 0)
        def _():
            st[...] = jnp.zeros_like(st)

        s_in = st[...]
        sin_ref[...] = s_in
        y, s_out, inverse = chunk_fn(_split_heads(pq), _split_heads(pk), _split_heads(pv), _split_heads(z),
                                     _gate_columns(gb[...], 0), _gate_columns(gb[...], DN_HEADS), gr[...],
                                     og[...], s_in)
        for j in range(DN_HB):
            y_ref[:, _head_cols(j)] = y[j].astype(y_ref.dtype)
        inv_ref[...] = inverse
        st[...] = s_out

    return pl.pallas_call(
        body, name=name, grid=(DN_HEADS // DN_HB, N),
        in_specs=[sp["pq"], sp["pk"], sp["pv"], sp["z"], sp["gates"], sp["row"], sp["gain"]],
        out_specs=[sp["head"], sp["state"], sp["inverse"]],
        out_shape=[jax.ShapeDtypeStruct((S, DN_WIDTH), BF16),
                   jax.ShapeDtypeStruct((DN_HEADS, N, DN_HEAD_DIM, DN_HEAD_DIM), F32),
                   jax.ShapeDtypeStruct((DN_HEADS, N, CHUNK, CHUNK), F32)],
        scratch_shapes=[pltpu.VMEM((DN_HB, DN_HEAD_DIM, DN_HEAD_DIM), F32)],
        compiler_params=_cparams(("parallel", "arbitrary")),
    )(pre, pre, pre, proj, gates, g_row, ogain)


def _chunk_bwd(name, pre, proj, gates, g_row, ogain, s_in_all, inverse_all, dy):
    assert DN_HB == DN_HEADS
    S = pre.shape[0]
    N = S // CHUNK
    chunk_fn = _chunk_math(DN_PRECISION)
    sp = _chunk_specs(lambda n: N - 1 - n)

    def body(pq, pk, pv, z, gb, gr, og, sin_ref, inv_ref, dy_ref,
             dpre_ref, dz_ref, dgb_ref, dgr_ref, dog_ref, ds):
        @pl.when(pl.program_id(1) == 0)
        def _():
            ds[...] = jnp.zeros_like(ds)
            dog_ref[...] = jnp.zeros_like(dog_ref)

        inverse = inv_ref[...]
        prim = (_split_heads(pq), _split_heads(pk), _split_heads(pv), _split_heads(z),
                _gate_columns(gb[...], 0), _gate_columns(gb[...], DN_HEADS), gr[...], og[...], sin_ref[...])
        _, vjp = jax.vjp(lambda *a: chunk_fn(*a, inverse=inverse), *prim)
        gq, gk, gv, gz, ggc, gbc, ggr, gog, gs = vjp((_split_heads(dy_ref), ds[...]))
        for j in range(DN_HB):
            for part, g in enumerate((gq, gk, gv)):
                dpre_ref[:, pl.ds(part * DN_WIDTH + j * DN_HEAD_DIM, DN_HEAD_DIM)] = g[j]
            dz_ref[:, _head_cols(j)] = gz[j]
        dgb_ref[...] = _gate_lanes(ggc, 0) + _gate_lanes(gbc, DN_HEADS)
        dgr_ref[...] = ggr
        dog_ref[...] += gog
        ds[...] = gs

    hd = sp["head"]
    return pl.pallas_call(
        body, name=name, grid=(1, N),
        in_specs=[sp["pq"], sp["pk"], sp["pv"], sp["z"], sp["gates"], sp["row"], sp["gain"],
                  sp["state"], sp["inverse"], hd],
        out_specs=[sp["qkv"], hd, sp["gates"], sp["row"], sp["gain"]],
        out_shape=[jax.ShapeDtypeStruct((S, DN_QKV), F32), jax.ShapeDtypeStruct((S, DN_WIDTH), F32),
                   jax.ShapeDtypeStruct((S, DN_AB_PAD), F32),
                   jax.ShapeDtypeStruct((DN_HEADS, N, 1, CHUNK), F32), jax.ShapeDtypeStruct((1, DN_HEAD_DIM), F32)],
        scratch_shapes=[pltpu.VMEM((DN_HB, DN_HEAD_DIM, DN_HEAD_DIM), F32)],
        compiler_params=_cparams(("arbitrary", "arbitrary")),
    )(pre, pre, pre, proj, gates, g_row, ogain, s_in_all, inverse_all, dy)


def _mm_rms_bwd(name, d_out, w, x, gain, dres):
    return _mm(name, d_out, w, "nt", out_dtypes=(F32, BF16), extras=(x, dres), rows=(gain.reshape(1, -1),),
               epilogue=lambda acc, x_, dres_, g: _rms_bwd_fn(x_, dres_, acc, g), n_sums=1, tm_cap=512)


def _residual_norm_epilogue(acc, res, gain):
    x = acc + res
    return x, _rms_fwd_fn(x, gain)


def _ple_loss_fn(x_mid, pp, zg, t):
    gate = _sigmoid(zg)
    err = x_mid + pp * gate - t
    dy = err * (1.0 / D_MODEL)
    return dy, dy * gate, dy * pp * gate * (1.0 - gate), jnp.broadcast_to(jnp.sum(err * err, keepdims=True), (1, 128))


def _mlp_ple_fwd(tag, x_in, h, p_l, w_up, w_down, norm_ple, w_ple, w_gate, next_gain=None, target=None):
    a = _mm(f"{tag}_up", h, w_up, "nn", out_dtypes=(BF16,), epilogue=_relu2_epilogue)
    x_mid, hg = _mm(f"{tag}_down", a, w_down, "nn", out_dtypes=(F32, BF16), extras=(x_in,),
                    rows=(norm_ple.reshape(1, -1),), epilogue=_residual_norm_epilogue)
    zg = _mm(f"{tag}_gate", hg, w_gate, "nn")
    pp = _mm(f"{tag}_ple", p_l, w_ple, "nn")
    sv = dict(x_in=x_in, h=h, a=a, x_mid=x_mid, hg=hg, zg=zg, pp=pp)
    if target is not None:
        sv["dy"], sv["dpp"], sv["dzg"], sv["sq"] = _rowwise(
            f"{tag}_ple_loss", _ple_loss_fn, [x_mid, pp, zg, target], [],
            [(D_MODEL, F32), (D_MODEL, BF16), (D_MODEL, BF16)], [(1, 128)])
        return None, None, sv
    if next_gain is None:
        return _rowwise(f"{tag}_ple_out", _ple_fwd_fn, [x_mid, pp, zg], [], [(D_MODEL, F32)]), None, sv
    x_out, h_next = _rowwise(f"{tag}_ple_out", _ple_norm_fwd_fn, [x_mid, pp, zg], [next_gain.reshape(1, -1)],
                             [(D_MODEL, F32), (D_MODEL, BF16)])
    return x_out, h_next, sv


def _mlp_ple_bwd(tag, dx, sv, p_l, norm_mlp, w_up, w_down, norm_ple, w_ple, w_gate):
    if "dpp" in sv:
        dpp, dzg = sv["dpp"], sv["dzg"]
    else:
        dpp, dzg = _rowwise(f"{tag}_ple_bwd", _ple_bwd_fn, [dx, sv["pp"], sv["zg"]], [],
                            [(D_MODEL, BF16), (D_MODEL, BF16)])
    d_w_ple = _mm(f"{tag}_d_w_ple", p_l, dpp, "tn", out_dtypes=(BF16,))
    d_w_gate = _mm(f"{tag}_d_w_gate", sv["hg"], dzg, "tn", out_dtypes=(BF16,))
    dx_mid, dx_mid_b, d_norm_ple = _mm_rms_bwd(f"{tag}_d_hg", dzg, w_gate, sv["x_mid"], norm_ple, dx)
    du = _mm(f"{tag}_d_u", dx_mid_b, w_down, "nt", out_dtypes=(BF16,), extras=(sv["a"],),
             epilogue=_relu2_bwd_epilogue)
    d_w_down = _mm(f"{tag}_d_w_down", sv["a"], dx_mid_b, "tn", out_dtypes=(BF16,))
    d_w_up = _mm(f"{tag}_d_w_up", sv["h"], du, "tn", out_dtypes=(BF16,))
    dx_in, dx_in_b, d_norm_mlp = _mm_rms_bwd(f"{tag}_d_h", du, w_up, sv["x_in"], norm_mlp, dx_mid)
    return dx_in, dx_in_b, dict(mlp_norm=d_norm_mlp, w_up=d_w_up, w_down=d_w_down, ple_norm=d_norm_ple,
                                w_ple=d_w_ple, w_ple_gate=d_w_gate)


class _NoHooks:
    first_cargo = None
    fwd_cargo = (None,) * len(SWA_GROUPS)

    def first_weights(self, results):
        return {}

    def weights_from(self, results):
        return {}

    def split_cargo(self, early_grads):
        return None

    def bwd_cargo(self, results):
        return (None,) * len(SWA_GROUPS)

    def last_cargo(self, attn_grads):
        return None


def _with_cargo(result, cargo):
    return (result, ()) if cargo is None else result


def _local_step(x, p, positions, target, small, big, hooks=_NoHooks()):
    S = x.shape[0]
    ct, st = _rope_lane_tables(positions)
    bd = _head_mean_matrix()

    h0, first = _with_cargo(_rms_fwd("l0_mix_norm", x, small["mix_norm"][0], cargo=hooks.first_cargo),
                            hooks.first_cargo)
    big = {**big, **hooks.first_weights(first)}
    attn, brought = [], []
    for g, (window, d) in enumerate(SWA_GROUPS):
        assert window // d == BAND and (S // d) % BAND == 0
        h0g = _to_classes(h0, d)
        ctg, stg = _to_classes(ct, d), _to_classes(st, d)
        w_g = big["attn_w_qkv"][:, g * 3 * A_WIDTH:(g + 1) * 3 * A_WIDTH]
        gq = jnp.tile(small["attn_q_gain"][0, g], A_HEADS).reshape(1, A_WIDTH)
        gk = jnp.tile(small["attn_k_gain"][0, g], A_HEADS).reshape(1, A_WIDTH)
        qkv = _mm(f"l0_qkv{g}", h0g, w_g, "nn")
        q, k, v = _rowwise(f"l0_qk_prep{g}", _qk_prep_fwd_fn, [qkv, ctg, stg], [gq, gk, bd], [(A_WIDTH, BF16)] * 3)
        o, lse, cargo_out = _attn_fwd(f"l0_attn{g}", q, k, v, (S // d) // BAND, cargo=hooks.fwd_cargo[g])
        brought.append(cargo_out)
        attn.append(dict(d=d, h0g=h0g, ct=ctg, st=stg, w=w_g, gq=gq, gk=gk, qkv=qkv, q=q, k=k, v=v, o=o, lse=lse,
                         o_tok=_from_classes(o, d), lse_tok=_from_classes(lse, d)))
    big = {**big, **hooks.weights_from(brought)}
    om = _rowwise("l0_merge", _merge_fwd_fn, [a["o_tok"] for a in attn] + [a["lse_tok"] for a in attn], [],
                  [(A_WIDTH, BF16)])
    x1, h1 = _mm("l0_attn_out", om, big["attn_w_o"], "nn", out_dtypes=(F32, BF16), extras=(x,),
                 rows=(small["mlp_norm"][0].reshape(1, -1),), epilogue=_residual_norm_epilogue)
    x3, h3, sv0 = _mlp_ple_fwd("l0", x1, h1, p[0], big["w_up"][0], big["w_down"][0], small["ple_norm"][0],
                               big["w_ple"][0], big["w_ple_gate"][0], next_gain=small["mix_norm"][1])

    N = S // CHUNK
    proj = _mm("l1_in", h3, big["dn_w_in"], "nn")
    pre = _conv_fwd("l1_conv", proj, small["dn_conv"])
    ab = proj[:, DN_AB0:DN_AB0 + DN_AB_PAD]
    lane_pad = DN_AB_PAD - DN_HEADS
    alog_row = jnp.pad(small["dn_a_log"][0], (0, lane_pad)).reshape(1, DN_AB_PAD)
    dt_row = jnp.pad(small["dn_dt_bias"][0], (0, lane_pad)).reshape(1, DN_AB_PAD)
    cum = _chunk_cumsum_matrix()
    gb = _rowwise("l1_gates", _gates_fwd_fn, [ab], [alog_row, dt_row, cum], [(DN_AB_PAD, F32)], tm=GATES_ROWS)
    g_row = gb[:, :DN_HEADS].T.reshape(DN_HEADS, N, 1, CHUNK)
    ogain = small["dn_o_gain"][0].reshape(1, DN_HEAD_DIM)
    y, s_in_all, inverse_all = _chunk_fwd("l1_delta", pre, proj, gb, g_row, ogain)
    x4, h4 = _mm("l1_dn_out", y, big["dn_w_o"], "nn", out_dtypes=(F32, BF16), extras=(x3,),
                 rows=(small["mlp_norm"][1].reshape(1, -1),), epilogue=_residual_norm_epilogue)
    _, _, sv1 = _mlp_ple_fwd("l1", x4, h4, p[1], big["w_up"][1], big["w_down"][1], small["ple_norm"][1],
                             big["w_ple"][1], big["w_ple_gate"][1], target=target)
    dy, sq = sv1["dy"], sv1["sq"]

    dx4, dx4_b, gl1 = _mlp_ple_bwd("l1", dy, sv1, p[1], small["mlp_norm"][1], big["w_up"][1], big["w_down"][1],
                            small["ple_norm"][1], big["w_ple"][1], big["w_ple_gate"][1])
    d_y = _mm("l1_d_y", dx4_b, big["dn_w_o"], "nt")
    d_dn_w_o = _mm("l1_d_w_o", y, dx4_b, "tn", out_dtypes=(BF16,))
    dpre, dz, dgb_cols, dg_row, d_ogain = _chunk_bwd(
        "l1_delta_bwd", pre, proj, gb, g_row, ogain, s_in_all, inverse_all, d_y)
    dconv_in, d_conv_w = _conv_bwd("l1_conv_bwd", proj, dpre, small["dn_conv"])
    dgb = dgb_cols + jnp.pad(dg_row.reshape(DN_HEADS, S).T, ((0, 0), (0, DN_AB_PAD - DN_HEADS)))
    dab, d_alog, d_dt = _rowwise("l1_gates_bwd", _gates_bwd_fn, [ab, dgb], [alog_row, dt_row, cum],
                                 [(DN_AB_PAD, F32)], [(1, DN_AB_PAD), (1, DN_AB_PAD)], tm=GATES_ROWS)
    dproj = jnp.concatenate([dz.astype(BF16), dconv_in, dab.astype(BF16)], axis=1)
    d_dn_w_in = _mm("l1_d_w_in", h3, dproj, "tn", out_dtypes=(BF16,))
    dx3, _, d_mix1 = _mm_rms_bwd("l1_d_h", dproj, big["dn_w_in"], x3, small["mix_norm"][1], dx4)

    dx1, dx1_b, gl0 = _mlp_ple_bwd("l0", dx3, sv0, p[0], small["mlp_norm"][0], big["w_up"][0], big["w_down"][0],
                            small["ple_norm"][0], big["w_ple"][0], big["w_ple_gate"][0])
    early = dict(
        dn_w_in=jnp.concatenate([d_dn_w_in[:, DN_QKV0:DN_AB0 + 2 * DN_HEADS], d_dn_w_in[:, :DN_WIDTH]], axis=1),
        dn_w_o=d_dn_w_o,
        w_up=jnp.stack([gl0["w_up"], gl1["w_up"]]),
        w_down=jnp.stack([gl0["w_down"], gl1["w_down"]]),
        w_ple=jnp.stack([gl0["w_ple"], gl1["w_ple"]]),
        w_ple_gate=jnp.stack([gl0["w_ple_gate"], gl1["w_ple_gate"]]))
    split_cargo = hooks.split_cargo(early)
    dom = _mm("l0_d_om", dx1_b, big["attn_w_o"], "nt")
    d_attn_w_o = _mm("l0_d_w_o", om, dx1_b, "tn", out_dtypes=(BF16,))
    merged, split = _with_cargo(
        _rowwise("l0_merge_bwd", _merge_bwd_fn, [a["o_tok"] for a in attn] + [a["lse_tok"] for a in attn] + [dom], [],
                 [(A_WIDTH, F32)] * 6, cargo=split_cargo), split_cargo)
    bwd_cargo = hooks.bwd_cargo(split)
    dh0, d_w_qkv, d_gq, d_gk, brought_bwd = [], [], [], [], []
    for g, a in enumerate(attn):
        do_g, dl_g = _to_classes(merged[g], a["d"]), _to_classes(merged[3 + g], a["d"])
        dqn, dkn, dvn, cargo_out = _attn_bwd(f"l0_attn_bwd{g}", a["q"], a["k"], a["v"], a["o"], a["lse"], do_g, dl_g,
                                             (S // a["d"]) // BAND, cargo=bwd_cargo[g])
        brought_bwd.append(cargo_out)
        dqkv, dgq, dgk = _rowwise(f"l0_qk_prep_bwd{g}", _qk_prep_bwd_fn, [a["qkv"], a["ct"], a["st"], dqn, dkn, dvn],
                                  [a["gq"], a["gk"], bd], [(3 * A_WIDTH, BF16)], [(1, A_HEAD_DIM)] * 2)
        d_w_qkv.append(_mm(f"l0_d_w_qkv{g}", a["h0g"], dqkv, "tn", out_dtypes=(BF16,)))
        dh0.append(_from_classes(_mm(f"l0_d_h{g}", dqkv, a["w"], "nt"), a["d"]))
        d_gq.append(dgq)
        d_gk.append(dgk)
    attn_grads = dict(attn_w_qkv=jnp.concatenate(d_w_qkv, axis=1), attn_w_o=d_attn_w_o)
    last_cargo = hooks.last_cargo(attn_grads)
    (grad_x, _, d_mix0), last = _with_cargo(
        _rms_bwd("l0_mix_norm_bwd", x, small["mix_norm"][0], dx1, dh0, cargo=last_cargo), last_cargo)
    brought_bwd.append(last)

    grads = dict(
        mix_norm=jnp.concatenate([d_mix0, d_mix1], axis=0),
        attn_q_gain=jnp.concatenate(d_gq, axis=0)[None],
        attn_k_gain=jnp.concatenate(d_gk, axis=0)[None],
        **attn_grads,
        dn_conv=d_conv_w,
        dn_a_log=d_alog[:, :DN_HEADS],
        dn_dt_bias=d_dt[:, :DN_HEADS],
        dn_o_gain=d_ogain,
        mlp_norm=jnp.concatenate([gl0["mlp_norm"], gl1["mlp_norm"]], axis=0),
        ple_norm=jnp.concatenate([gl0["ple_norm"], gl1["ple_norm"]], axis=0),
        **early,
    )
    return sq, grad_x, grads, brought_bwd


def _chip_peer(x, y, c, t):
    return (jnp.bitwise_xor(x, t >> 1), jnp.bitwise_xor(y, t & 1), c)


def _place():
    x, y, c = lax.axis_index("x"), lax.axis_index("y"), lax.axis_index("c")
    return x, y, c, 2 * x + y, (x, y, 1 - c)


def _remote(src, dst, send_sem, recv_sem, to):
    return pltpu.make_async_remote_copy(src_ref=src, dst_ref=dst, send_sem=send_sem, recv_sem=recv_sem,
                                        device_id=to, device_id_type=MESH)


def _hbm_call(name, body, ins, out_shape, scratch_shapes):
    any_spec = pl.BlockSpec(memory_space=pl.ANY)
    return pl.pallas_call(body, name=name, out_shape=out_shape, in_specs=[any_spec] * len(ins),
                          out_specs=[any_spec] * len(out_shape), scratch_shapes=scratch_shapes)(*ins)


def _half(n0, which):
    return pl.ds(which * (n0 // 2), n0 // 2)


class _Exchange:
    def __init__(self, ins, out_shape, scratch, start, finish):
        self.ins, self.out_shape, self.scratch, self.start, self.finish = ins, out_shape, scratch, start, finish


def _run_exchange(name, ex):
    n_in, n_out = len(ex.ins), len(ex.out_shape)

    def body(*refs):
        ins, outs, sems = refs[:n_in], refs[n_in:n_in + n_out], refs[n_in + n_out:]
        ex.start(ins, outs, sems)
        ex.finish(ins, outs, sems)

    return _hbm_call(name, body, ex.ins, ex.out_shape, ex.scratch)


def _gather_exchange(shards):
    T = len(shards)

    pairs = [(i, t) for i in range(T) for t in range(1, N_CHIPS)]

    def copies(ins, outs, sems):
        send, recv = sems
        x, y, c, q, sibling = _place()

        def half(i, which):
            return _half(ins[i].shape[0], which)

        def over_ici(i, t):
            return _remote(ins[i].at[half(i, c)], outs[i].at[q, half(i, c)], send.at[i, t - 1], recv.at[i, t - 1],
                           _chip_peer(x, y, c, t))

        def landing(i, t):
            spot = outs[i].at[jnp.bitwise_xor(q, t), half(i, c)]
            return _remote(spot, spot, send.at[i, t - 1], recv.at[i, t - 1], _chip_peer(x, y, c, t))

        def forward(i, t):
            spot = outs[i].at[jnp.bitwise_xor(q, t), half(i, c)]
            return _remote(spot, spot, send.at[i, 2 + t], recv.at[i, 2 + t], sibling)

        def forwarded(i, t):
            spot = outs[i].at[jnp.bitwise_xor(q, t), half(i, 1 - c)]
            return _remote(spot, spot, send.at[i, 2 + t], recv.at[i, 2 + t], sibling)

        return over_ici, landing, forward, forwarded

    def start(ins, outs, sems):
        over_ici = copies(ins, outs, sems)[0]
        for i, t in pairs:
            over_ici(i, t).start()

    def finish(ins, outs, sems):
        over_ici, landing, forward, forwarded = copies(ins, outs, sems)
        for i, t in pairs:
            landing(i, t).wait_recv()
            forward(i, t).start()
        for i, t in pairs:
            forwarded(i, t).wait_recv()
        for i, t in pairs:
            over_ici(i, t).wait_send()
            forward(i, t).wait_send()

    n_rel = 2 * (N_CHIPS - 1)
    return _Exchange(list(shards), [jax.ShapeDtypeStruct((N_CHIPS,) + s.shape, s.dtype) for s in shards],
                     [pltpu.SemaphoreType.DMA((T, n_rel)), pltpu.SemaphoreType.DMA((T, n_rel))], start, finish)


def _scatter_exchange(stacks):
    T = len(stacks)

    def copies(ins, outs, sems):
        send, recv = sems
        x, y, c, q, sibling = _place()
        return [_remote(ins[i].at[jnp.bitwise_xor(q, t)], outs[i].at[t - 1], send.at[i, t - 1], recv.at[i, t - 1],
                        _chip_peer(x, y, c, t)) for i in range(T) for t in range(1, N_CHIPS)]

    def start(ins, outs, sems):
        for cp in copies(ins, outs, sems):
            cp.start()

    def finish(ins, outs, sems):
        for cp in copies(ins, outs, sems):
            cp.wait()

    return _Exchange(list(stacks), [jax.ShapeDtypeStruct((N_CHIPS - 1,) + s.shape[1:], s.dtype) for s in stacks],
                     [pltpu.SemaphoreType.DMA((T, N_CHIPS - 1)), pltpu.SemaphoreType.DMA((T, N_CHIPS - 1))],
                     start, finish)


def _other_half_exchange(stacks):
    T = len(stacks)

    def copies(ins, outs, sems):
        send, recv = sems
        x, y, c, q, sibling = _place()
        return [_remote(ins[i].at[:, _half(ins[i].shape[1], 1 - c)], outs[i], send.at[i], recv.at[i], sibling)
                for i in range(T)]

    def start(ins, outs, sems):
        for cp in copies(ins, outs, sems):
            cp.start()

    def finish(ins, outs, sems):
        for cp in copies(ins, outs, sems):
            cp.wait()

    return _Exchange(list(stacks),
                     [jax.ShapeDtypeStruct((s.shape[0], s.shape[1] // 2) + s.shape[2:], s.dtype) for s in stacks],
                     [pltpu.SemaphoreType.DMA((T,)), pltpu.SemaphoreType.DMA((T,))], start, finish)


def _swap_with_sibling(name, arrays):
    T = len(arrays)

    def body(*refs):
        ins, outs = refs[:T], refs[T:2 * T]
        send, recv = refs[2 * T:]
        x, y, c, q, sibling = _place()
        copies = []
        for i in range(T):
            rc = _remote(ins[i], outs[i], send.at[i], recv.at[i], sibling)
            rc.start()
            copies.append(rc)
        for cp in copies:
            cp.wait()

    return _hbm_call(name, body, arrays, [jax.ShapeDtypeStruct(a.shape, a.dtype) for a in arrays],
                     [pltpu.SemaphoreType.DMA((T,)), pltpu.SemaphoreType.DMA((T,))])


def _gather_from_all(name, block):
    R, C = block.shape

    def body(src, out, send_sems, recv_sems):
        x, y, c = lax.axis_index("x"), lax.axis_index("y"), lax.axis_index("c")
        me = 4 * x + 2 * y + c
        out[me] = src[...]
        copies = []
        for r in range(1, N_DEV):
            peer = (jnp.bitwise_xor(x, r >> 2), jnp.bitwise_xor(y, (r >> 1) & 1), jnp.bitwise_xor(c, r & 1))
            cp = pltpu.make_async_remote_copy(src_ref=src, dst_ref=out.at[me], send_sem=send_sems.at[r - 1],
                                              recv_sem=recv_sems.at[r - 1], device_id=peer, device_id_type=MESH)
            cp.start()
            copies.append(cp)
        for cp in copies:
            cp.wait()

    return pl.pallas_call(
        body, name=name, out_shape=jax.ShapeDtypeStruct((N_DEV, R, C), block.dtype),
        in_specs=[pl.BlockSpec(memory_space=pltpu.VMEM)], out_specs=pl.BlockSpec(memory_space=pltpu.VMEM),
        scratch_shapes=[pltpu.SemaphoreType.DMA((N_DEV - 1,)), pltpu.SemaphoreType.DMA((N_DEV - 1,))],
    )(block)


def _view(a):
    return a[0] if a.shape[0] == 1 else a


def _view_axis(a, axis):
    return axis - 1 if a.shape[0] == 1 else axis


def _rows(a):
    return a.reshape(-1, a.shape[-1])


def _elementwise(name, fn, ins, out_dtypes, tm):
    specs = []
    for a in ins:
        a, row0 = a if isinstance(a, tuple) else (a, 0)
        specs.append((_rows(a), a.shape[-1], 0, row0))
    shape = ins[0][0].shape if isinstance(ins[0], tuple) else ins[0].shape
    outs = _rowwise(name, fn, specs, [], [(shape[-1], dt) for dt in out_dtypes], tm=tm, n_rows=math.prod(shape[:-1]))
    return outs.reshape(shape) if len(out_dtypes) == 1 else [o.reshape(shape) for o in outs]


SMALL_ROWS = 8
CONV_ROWS = CONV_WIDTH * DN_QKV // D_MODEL
SMALL_GRAD_ROWS = 24


def _pack_small(vals, conv=None):
    tail = jnp.concatenate([vals["attn_q_gain"].reshape(-1), vals["attn_k_gain"].reshape(-1),
                            vals["dn_a_log"].reshape(-1), vals["dn_dt_bias"].reshape(-1),
                            vals["dn_o_gain"].reshape(-1)])
    tail = jnp.pad(tail, (0, D_MODEL - tail.shape[0])).reshape(1, D_MODEL)
    rows = [vals["mix_norm"], vals["mlp_norm"], vals["ple_norm"], tail, jnp.zeros((1, D_MODEL), F32)]
    if conv is not None:
        rows += [conv.reshape(CONV_ROWS, D_MODEL),
                 jnp.zeros((SMALL_GRAD_ROWS - SMALL_ROWS - CONV_ROWS, D_MODEL), F32)]
    return jnp.concatenate(rows, axis=0)


def _unpack_small(block):
    nq = 3 * A_HEAD_DIM
    t = block[6]
    return dict(
        mix_norm=block[0:2], mlp_norm=block[2:4], ple_norm=block[4:6],
        attn_q_gain=t[:nq].reshape(1, 3, A_HEAD_DIM), attn_k_gain=t[nq:2 * nq].reshape(1, 3, A_HEAD_DIM),
        dn_a_log=t[2 * nq:2 * nq + DN_HEADS].reshape(1, DN_HEADS),
        dn_dt_bias=t[2 * nq + DN_HEADS:2 * nq + 2 * DN_HEADS].reshape(1, DN_HEADS),
        dn_o_gain=t[2 * nq + 2 * DN_HEADS:2 * nq + 2 * DN_HEADS + DN_HEAD_DIM].reshape(1, DN_HEAD_DIM))


def kernel(x, p, positions, mix_norm, attn_w_qkv, attn_q_gain, attn_k_gain, attn_w_o, dn_w_in, dn_conv, dn_a_log, dn_dt_bias, dn_o_gain, dn_w_o, mlp_norm, w_up, w_down, ple_norm, w_ple, w_ple_gate, loss_target, m_mix_norm, m_attn_w_qkv, m_attn_q_gain, m_attn_k_gain, m_attn_w_o, m_dn_w_in, m_dn_conv, m_dn_a_log, m_dn_dt_bias, m_dn_o_gain, m_dn_w_o, m_mlp_norm, m_w_up, m_w_down, m_ple_norm, m_w_ple, m_w_ple_gate, v_mix_norm, v_attn_w_qkv, v_attn_q_gain, v_attn_k_gain, v_attn_w_o, v_dn_w_in, v_dn_conv, v_dn_a_log, v_dn_dt_bias, v_dn_o_gain, v_dn_w_o, v_mlp_norm, v_w_up, v_w_down, v_ple_norm, v_w_ple, v_w_ple_gate):
    given = dict(locals())
    w = {n: given[n] for n in WEIGHTS}
    m = {n: given["m_" + n] for n in WEIGHTS}
    v = {n: given["v_" + n] for n in WEIGHTS}
    kinds = ("grad", "delta", "new_m", "new_v")
    axes = {n: _view_axis(w[n], axis) for n, axis in SHARDED if n != "dn_conv"}
    chip = 2 * lax.axis_index("x") + lax.axis_index("y")
    core = lax.axis_index("c")
    shards = {n: _view(w[n]).astype(BF16) for n in axes}

    def whole(n, slots):
        return jnp.concatenate([jnp.where(chip == q, shards[n], slots[q]) for q in range(N_CHIPS)], axis=axes[n])

    def stacks_of(grads_of):
        return [jnp.stack(jnp.split(g, N_CHIPS, axis=axes[n])) for n, g in grads_of.items()]

    def chip_sums_of(names, stacks, theirs):
        mine = [lax.dynamic_slice_in_dim(s, core * (s.shape[1] // 2), s.shape[1] // 2, axis=1) for s in stacks]
        return {n: _elementwise(f"add_core_{n}", lambda a, b: a.astype(F32) + b.astype(F32), [a, b], [BF16], 128)
                for n, a, b in zip(names, mine, theirs)}

    class Hooks:
        first_cargo = _gather_exchange([shards[n] for n in ATTN_MATRICES])
        fwd_cargo = [_gather_exchange([shards[n] for n in group]) for group in CARGO_GROUPS]
        chip_sums = {}
        early = None

        def first_weights(self, results):
            return {n: whole(n, slots) for n, slots in zip(ATTN_MATRICES, results)}

        def weights_from(self, results):
            full = {n: whole(n, slots) for group, res in zip(CARGO_GROUPS, results) for n, slots in zip(group, res)}
            w_in, n_ab = full["dn_w_in"], 2 * DN_HEADS
            full["dn_w_in"] = jnp.concatenate([w_in[:, DN_QKV + n_ab:], w_in[:, :DN_QKV + n_ab],
                                               jnp.zeros((D_MODEL, DN_AB_PAD - n_ab), BF16)], axis=1)
            return full

        def split_cargo(self, early_grads):
            self.early = (list(early_grads), stacks_of(early_grads))
            return _other_half_exchange(self.early[1])

        def bwd_cargo(self, results):
            self.chip_sums.update(chip_sums_of(*self.early, results))
            return [_scatter_exchange([self.chip_sums[n] for n in group]) for group in CARGO_GROUPS]

        def last_cargo(self, attn_grads):
            stacks = stacks_of(attn_grads)
            theirs = _run_exchange("split_core_grads_attn", _other_half_exchange(stacks))
            self.chip_sums.update(chip_sums_of(list(attn_grads), stacks, theirs))
            return _scatter_exchange([self.chip_sums[n] for n in attn_grads])

    hooks = Hooks()
    big = {}
    conv_block = jnp.pad(w["dn_conv"].reshape(-1), (0, SMALL_ROWS * D_MODEL - w["dn_conv"].size))
    conv_all = _gather_from_all("gather_conv", conv_block.reshape(SMALL_ROWS, D_MODEL))
    conv_all = conv_all.reshape(N_CHIPS, 2, -1)[:, 0, :w["dn_conv"].size]
    conv_full = jnp.concatenate([conv_all[q].reshape(CONV_WIDTH, -1) for q in range(N_CHIPS)], axis=1)
    small = {n: w[n] for n in REPLICATED}
    small["dn_conv"] = conv_full

    sq, grad_x, grads, brought = _local_step(x[0], p[:, 0], positions[0], loss_target[0], small, big, hooks)
    loss = lax.psum(0.5 * sq[0, 0] / D_MODEL, ("x", "y", "c"))
    out = {}

    landed = {n: r for group, res in zip(CARGO_GROUPS + (ATTN_MATRICES,), brought) for n, r in zip(group, res)}
    half_sums = []
    for n in axes:
        o = lax.dynamic_index_in_dim(hooks.chip_sums[n], chip, axis=0, keepdims=False)
        per = math.prod(o.shape[:-1])
        r = landed[n]
        half_sums.append(_elementwise(
            f"add_chips_{n}", lambda a, b, c, d: ((a.astype(F32) + b.astype(F32)) + c.astype(F32)) + d.astype(F32),
            [o, (r, 0), (r, per), (r, 2 * per)], [F32], 128))
    other_halves = _swap_with_sibling("join_core_sums", half_sums)
    for n, a, b in zip(axes, half_sums, other_halves):
        g = jnp.where(core == 0, jnp.concatenate([a, b], axis=0), jnp.concatenate([b, a], axis=0))
        shp = w[n].shape
        res = _elementwise(f"adamw_{n}", lambda g, w_, m_, v_: (g,) + _adamw(w_, g, m_, v_),
                           [g.reshape(shp), w[n], m[n], v[n]], [F32] * 4, 256)
        for kind, arr in zip(kinds, res):
            out[kind + "_" + n] = arr.reshape(shp)

    slots = _gather_from_all("gather_small_grads", _pack_small(grads, grads["dn_conv"]))

    def small_body(s_ref, w_ref, m_ref, v_ref, sum_out, g_out, d_out, m_out, v_out):
        total = s_ref[0]
        for d in range(1, N_DEV):
            total = total + s_ref[d]
        sum_out[...] = total
        g = total[:SMALL_ROWS]
        for o, r in zip((g_out, d_out, m_out, v_out), (g,) + _adamw(w_ref[...], g, m_ref[...], v_ref[...])):
            o[...] = r

    res = pl.pallas_call(small_body, name="adamw_replicated",
                         out_shape=[jax.ShapeDtypeStruct((SMALL_GRAD_ROWS, D_MODEL), F32)]
                         + [jax.ShapeDtypeStruct((SMALL_ROWS, D_MODEL), F32)] * 4)(
        slots, _pack_small(w), _pack_small(m), _pack_small(v))
    for kind, block in zip(kinds, res[1:]):
        for n, arr in _unpack_small(block).items():
            out[kind + "_" + n] = arr
    conv_sum = res[0][SMALL_ROWS:SMALL_ROWS + CONV_ROWS].reshape(CONV_WIDTH, DN_QKV)
    cols = DN_QKV // N_CHIPS
    chip = 2 * lax.axis_index("x") + lax.axis_index("y")
    conv_mine = lax.dynamic_slice_in_dim(conv_sum, chip * cols, cols, axis=1)
    res = _elementwise("adamw_dn_conv", lambda g, w_, m_, v_: (g,) + _adamw(w_, g, m_, v_),
                       [conv_mine, w["dn_conv"][0], m["dn_conv"][0], v["dn_conv"][0]], [F32] * 4, CONV_WIDTH)
    for kind, arr in zip(kinds, res):
        out[kind + "_dn_conv"] = arr[None]

    return (loss, grad_x[None],
            *[out["grad_" + n] for n in WEIGHTS], *[out["delta_" + n] for n in WEIGHTS],
            *[out["new_m_" + n] for n in WEIGHTS], *[out["new_v_" + n] for n in WEIGHTS])
```

```python
import functools
import math

import jax
import jax.numpy as jnp
from jax import lax
from jax.experimental import pallas as pl
from jax.experimental.pallas import tpu as pltpu

F32 = jnp.float32
BF16 = jnp.bfloat16
HIGHEST = lax.Precision.HIGHEST

D_MODEL = 1024
EPS = 1e-6
SWA_GROUPS = ((128, 1), (512, 4), (2048, 16))
A_HEADS = 8
A_HEAD_DIM = 64
A_WIDTH = A_HEADS * A_HEAD_DIM
ROPE_DIM = A_HEAD_DIM // 4
ROPE_THETA = 500000.0
BAND = 128
DN_HEADS = 8
DN_HEAD_DIM = 128
DN_WIDTH = DN_HEADS * DN_HEAD_DIM
DN_QKV = 3 * DN_WIDTH
DN_AB_PAD = 128
DN_IN_PAD = DN_WIDTH + DN_QKV + DN_AB_PAD
DN_QKV0 = DN_WIDTH
DN_AB0 = DN_WIDTH + DN_QKV
DN_HB = 8
CONV_WIDTH = 4
CHUNK = 64
PLE_DIM = 256
D_FF = 4 * D_MODEL

ADAM_LR = 0.001
ADAM_B1 = 0.9
ADAM_B2 = 0.999
ADAM_EPS = 1e-08
ADAM_WD = 0.01
ADAM_STEP = 10

N_CHIPS = 4
N_DEV = 8
VMEM_LIMIT = 48 * 1024 * 1024
MESH = pl.DeviceIdType.MESH

SHARDED = (
    ("attn_w_qkv", 2), ("attn_w_o", 2), ("dn_w_in", 2), ("dn_conv", 2), ("dn_w_o", 1),
    ("w_up", 2), ("w_down", 1), ("w_ple", 2), ("w_ple_gate", 1))
ATTN_MATRICES = ("attn_w_qkv", "attn_w_o")
CARGO_GROUPS = (("w_up",), ("w_down",), ("dn_w_in", "dn_w_o", "w_ple", "w_ple_gate"))
REPLICATED = ("mix_norm", "attn_q_gain", "attn_k_gain", "dn_a_log", "dn_dt_bias", "dn_o_gain",
              "mlp_norm", "ple_norm")
WEIGHTS = ("mix_norm", "attn_w_qkv", "attn_q_gain", "attn_k_gain", "attn_w_o", "dn_w_in", "dn_conv",
           "dn_a_log", "dn_dt_bias", "dn_o_gain", "dn_w_o", "mlp_norm", "w_up", "w_down", "ple_norm",
           "w_ple", "w_ple_gate")


def _cparams(sem=None):
    return pltpu.CompilerParams(dimension_semantics=sem, vmem_limit_bytes=VMEM_LIMIT)


def _pick(n, cap, quantum=128):
    best = None
    for t in range(quantum, min(n, cap) + 1, quantum):
        if n % t == 0:
            best = t
    return n if best is None else best


_DIMS = {"nn": ((1,), (0,)), "nt": ((1,), (1,)), "tn": ((0,), (0,))}


def _mm(name, a, b, mode, out_dtypes=(F32,), extras=(), epilogue=None, rows=(), n_sums=0, tm_cap=1024):
    if mode == "nn":
        (M, K), (K2, N) = a.shape, b.shape
    elif mode == "nt":
        (M, K), (N, K2) = a.shape, b.shape
    else:
        (K, M), (K2, N) = a.shape, b.shape
    assert K == K2, (name, a.shape, b.shape)
    tn = _pick(N, 1536)
    if mode == "tn":
        tm, tk = _pick(M, tm_cap), _pick(K, 2048)
    elif tn == N and K > 1536:
        tm, tk = _pick(M, min(tm_cap, 512)), K
    else:
        tm, tk = _pick(M, tm_cap), _pick(K, 1536)
    nk = K // tk
    assert n_sums == 0 or tn == N, name
    if mode == "nn":
        a_spec = pl.BlockSpec((tm, tk), lambda i, j, k: (i, k))
        b_spec = pl.BlockSpec((tk, tn), lambda i, j, k: (k, j))
    elif mode == "nt":
        a_spec = pl.BlockSpec((tm, tk), lambda i, j, k: (i, k))
        b_spec = pl.BlockSpec((tn, tk), lambda i, j, k: (j, k))
    else:
        a_spec = pl.BlockSpec((tk, tm), lambda i, j, k: (k, i))
        b_spec = pl.BlockSpec((tk, tn), lambda i, j, k: (k, j))
    o_spec = pl.BlockSpec((tm, tn), lambda i, j, k: (i, j))
    r_spec = pl.BlockSpec((1, tn), lambda i, j, k: (0, j))
    n_extra, n_out = len(extras) + len(rows), len(out_dtypes)
    dims = (_DIMS[mode], ((), ()))

    def body(a_ref, b_ref, *rest):
        extra_refs, out_refs = rest[:n_extra], rest[n_extra:n_extra + n_out]
        sum_refs = rest[n_extra + n_out:n_extra + n_out + n_sums]
        i, k = pl.program_id(0), pl.program_id(2)
        part = lax.dot_general(a_ref[...].astype(BF16), b_ref[...].astype(BF16), dims, preferred_element_type=F32)

        def finish(total):
            vals = (total,) if epilogue is None else epilogue(total, *[e[...] for e in extra_refs])
            for o, v in zip(out_refs, vals[:n_out]):
                o[...] = v.astype(o.dtype)
            for s, v in zip(sum_refs, vals[n_out:]):
                @pl.when(i == 0)
                def _():
                    s[...] = v

                @pl.when(i > 0)
                def _():
                    s[...] += v

        if nk == 1:
            finish(part)
            return
        acc = rest[-1]

        @pl.when(k == 0)
        def _():
            acc[...] = part

        @pl.when(jnp.logical_and(k > 0, k < nk - 1))
        def _():
            acc[...] += part

        @pl.when(k == nk - 1)
        def _():
            finish(acc[...] + part)

    outs = pl.pallas_call(
        body, name=name, grid=(M // tm, N // tn, nk),
        in_specs=[a_spec, b_spec] + [o_spec] * len(extras) + [r_spec] * len(rows),
        out_specs=[o_spec] * n_out + [r_spec] * n_sums,
        out_shape=[jax.ShapeDtypeStruct((M, N), dt) for dt in out_dtypes]
        + [jax.ShapeDtypeStruct((1, N), F32)] * n_sums,
        scratch_shapes=[pltpu.VMEM((tm, tn), F32)] if nk > 1 else [],
        compiler_params=_cparams(("arbitrary" if n_sums else "parallel", "parallel", "arbitrary")),
    )(a, b, *extras, *rows)
    return outs[0] if n_out + n_sums == 1 else outs


def _rowwise(name, fn, rows, bcast, row_outs, acc_outs=(), tm=256, n_rows=None, cargo=None):
    rows = [r if isinstance(r, tuple) else (r, r.shape[1], 0) for r in rows]
    rows = [r if len(r) == 4 else r + (0,) for r in rows]
    S = rows[0][0].shape[0] if n_rows is None else n_rows
    tm = min(tm, S)
    assert S % tm == 0 and all(r[3] % tm == 0 for r in rows), (name, S, tm)
    n_row, n_bc, n_ro, n_acc = len(rows), len(bcast), len(row_outs), len(acc_outs)
    in_specs = [pl.BlockSpec((tm, w), functools.partial(lambda i, cb, rb: (i + rb, cb), cb=cb, rb=r0 // tm))
                for _, w, cb, r0 in rows]
    in_specs += [pl.BlockSpec(b.shape, lambda i: (0, 0)) for b in bcast]
    out_specs = [pl.BlockSpec((tm, c), lambda i: (i, 0)) for c, _ in row_outs]
    out_specs += [pl.BlockSpec(s, lambda i: (0, 0)) for s in acc_outs]
    out_shape = [jax.ShapeDtypeStruct((S, c), dt) for c, dt in row_outs]
    out_shape += [jax.ShapeDtypeStruct(s, F32) for s in acc_outs]

    def body(*refs):
        ins = [r[...] for r in refs[:n_row + n_bc]]
        outs = refs[n_row + n_bc:]
        vals = fn(*ins)
        if not isinstance(vals, (tuple, list)):
            vals = (vals,)
        for o, v in zip(outs[:n_ro], vals[:n_ro]):
            o[...] = v.astype(o.dtype)
        if n_acc:
            @pl.when(pl.program_id(0) == 0)
            def _():
                for o in outs[n_ro:]:
                    o[...] = jnp.zeros_like(o)
            for o, v in zip(outs[n_ro:], vals[n_ro:]):
                o[...] += v

    n_own = n_ro + n_acc
    body, c_in_specs, c_out_specs, c_out_shape, c_scratch, c_ins = _carry(cargo, n_row + n_bc, n_own, 0, body, S // tm)
    outs = pl.pallas_call(
        body, name=name, grid=(S // tm,), in_specs=in_specs + c_in_specs, out_specs=out_specs + c_out_specs,
        out_shape=out_shape + c_out_shape, scratch_shapes=c_scratch,
        compiler_params=_cparams(("arbitrary",) if n_acc or cargo is not None else ("parallel",)),
    )(*[r[0] for r in rows], *bcast, *c_ins)
    own = outs[0] if n_own == 1 else outs[:n_own]
    return own if cargo is None else (own, outs[n_own:])


def _sigmoid(x):
    return 1.0 / (1.0 + jnp.exp(-x))


def _silu(x):
    return x * _sigmoid(x)


def _softplus(x):
    return jnp.maximum(x, 0.0) + jnp.log(1.0 + jnp.exp(-jnp.abs(x)))


def _rms_fwd_fn(x, g):
    r = lax.rsqrt(jnp.mean(x * x, axis=-1, keepdims=True) + EPS)
    return (x * r) * g


def _rms_bwd_fn(x, dres, *rest):
    dh, g = sum(rest[:-1]), rest[-1]
    r = lax.rsqrt(jnp.mean(x * x, axis=-1, keepdims=True) + EPS)
    xh = x * r
    dxh = dh * g
    dx = dres + r * (dxh - xh * jnp.mean(dxh * xh, axis=-1, keepdims=True))
    return dx, dx, jnp.sum(dh * xh, axis=0, keepdims=True)


def _rms_fwd(name, x, gain, cargo=None):
    return _rowwise(name, _rms_fwd_fn, [x], [gain.reshape(1, -1)], [(x.shape[1], BF16)], cargo=cargo)


def _rms_bwd(name, x, gain, dres, dhs, cargo=None):
    return _rowwise(name, _rms_bwd_fn, [x, dres] + list(dhs), [gain.reshape(1, -1)],
                    [(x.shape[1], F32), (x.shape[1], BF16)], [(1, x.shape[1])], cargo=cargo)


def _relu2_epilogue(acc):
    r = jnp.maximum(acc, 0.0)
    return (r * r,)


def _relu2_bwd_epilogue(acc, a):
    return (acc * (2.0 * jnp.sqrt(a.astype(F32))),)


def _ple_fwd_fn(x, pp, zg):
    return x + pp * _sigmoid(zg)


def _ple_norm_fwd_fn(x, pp, zg, gain):
    out = _ple_fwd_fn(x, pp, zg)
    return out, _rms_fwd_fn(out, gain)


def _ple_bwd_fn(dx, pp, zg):
    gate = _sigmoid(zg)
    return dx * gate, dx * pp * gate * (1.0 - gate)


def _adamw(w, g, m, v):
    m = ADAM_B1 * m + (1.0 - ADAM_B1) * g
    v = ADAM_B2 * v + (1.0 - ADAM_B2) * jnp.square(g)
    m_hat = m / (1.0 - ADAM_B1 ** ADAM_STEP)
    v_hat = v / (1.0 - ADAM_B2 ** ADAM_STEP)
    delta = -ADAM_LR * (m_hat / (jnp.sqrt(v_hat) + ADAM_EPS) + ADAM_WD * w)
    return delta, m, v


def _lane_take(x, offset):
    n = x.shape[-1]
    return pltpu.roll(x, (-offset) % n, 1)


def _head_lane(shape):
    return lax.broadcasted_iota(jnp.int32, shape, 1) % A_HEAD_DIM


def _rope_partner(x):
    lane = _head_lane(x.shape)
    return jnp.where(lane < ROPE_DIM // 2, _lane_take(x, ROPE_DIM // 2),
                     jnp.where(lane < ROPE_DIM, _lane_take(x, -(ROPE_DIM // 2)), 0.0))


def _head_mean(x, bd):
    hi = x.astype(BF16)
    lo = (x - hi.astype(F32)).astype(BF16)
    b = bd.astype(BF16)
    return jnp.dot(hi, b, preferred_element_type=F32) + jnp.dot(lo, b, preferred_element_type=F32)


def _fold_heads(row):
    out = row[:, :A_HEAD_DIM]
    for h in range(1, A_HEADS):
        out = out + row[:, h * A_HEAD_DIM:(h + 1) * A_HEAD_DIM]
    return out


def _all_heads(t):
    return jnp.concatenate([t] * (A_WIDTH // t.shape[1]), axis=1)


def _qk_prep_fwd_fn(qkv, ct, st, gq, gk, bd):
    ct, st = _all_heads(ct), _all_heads(st)

    def one(t, g):
        n = t * lax.rsqrt(_head_mean(t * t, bd) + EPS) * g
        return n * ct + _rope_partner(n) * st
    q, k, v = qkv[:, :A_WIDTH], qkv[:, A_WIDTH:2 * A_WIDTH], qkv[:, 2 * A_WIDTH:]
    return one(q, gq), one(k, gk), v


def _qk_prep_bwd_fn(qkv, ct, st, dq, dk, dv, gq, gk, bd):
    ct, st = _all_heads(ct), _all_heads(st)

    def one(t, g, dy):
        r = lax.rsqrt(_head_mean(t * t, bd) + EPS)
        nh = t * r
        dn = dy * ct + _rope_partner(dy * st)
        dg = jnp.sum(dn * nh, axis=0, keepdims=True)
        dnh = dn * g
        return r * (dnh - nh * _head_mean(dnh * nh, bd)), _fold_heads(dg)
    q, k = qkv[:, :A_WIDTH], qkv[:, A_WIDTH:2 * A_WIDTH]
    dq_raw, dgq = one(q, gq, dq)
    dk_raw, dgk = one(k, gk, dk)
    return jnp.concatenate([dq_raw, dk_raw, dv], axis=1), dgq, dgk


_BATCH_DIMS = {"nn": ((2,), (1,)), "nt": ((2,), (2,)), "tn": ((1,), (1,))}


def _bdot(a, b, mode, precision=None):
    return lax.dot_general(a, b, (_BATCH_DIMS[mode], ((0,), (0,))), precision=precision,
                           preferred_element_type=F32)


def _attn_cols(h):
    return slice(h * A_HEAD_DIM, (h + 1) * A_HEAD_DIM)


def _attn_heads(ref):
    return jnp.stack([ref[:, _attn_cols(h)] for h in range(A_HEADS)])


def _band_masks():
    qi = lax.broadcasted_iota(jnp.int32, (BAND, BAND), 0)
    kj = lax.broadcasted_iota(jnp.int32, (BAND, BAND), 1)
    return kj <= qi, kj >= qi


def _attn_fwd(name, q, k, v, blocks_per_class, cargo=None):
    S = q.shape[0]
    nblk = S // BAND
    scale = A_HEAD_DIM ** -0.5

    def body(q_ref, kp_ref, kc_ref, vp_ref, vc_ref, o_ref, l_ref):
        i = pl.program_id(0)
        has_prev = (i % blocks_per_class) != 0
        m_cur, m_prev = _band_masks()
        m_prev = jnp.logical_and(m_prev, has_prev)
        q, kc, kp, vc, vp = (_attn_heads(r) for r in (q_ref, kc_ref, kp_ref, vc_ref, vp_ref))
        s_c = jnp.where(m_cur[None], _bdot(q, kc, "nt") * scale, -jnp.inf)
        s_p = jnp.where(m_prev[None], _bdot(q, kp, "nt") * scale, -jnp.inf)
        m = jnp.maximum(jnp.max(s_c, axis=-1, keepdims=True), jnp.max(s_p, axis=-1, keepdims=True))
        e_c, e_p = jnp.exp(s_c - m), jnp.exp(s_p - m)
        l = jnp.sum(e_c, axis=-1, keepdims=True) + jnp.sum(e_p, axis=-1, keepdims=True)
        o = _bdot((e_c / l).astype(BF16), vc, "nn") + _bdot((e_p / l).astype(BF16), vp, "nn")
        lse = m + jnp.log(l)
        for h in range(A_HEADS):
            o_ref[:, _attn_cols(h)] = o[h]
            l_ref[:, _attn_cols(h)] = jnp.broadcast_to(lse[h], (BAND, A_HEAD_DIM))

    cur = pl.BlockSpec((BAND, A_WIDTH), lambda i: (i, 0))
    prev = pl.BlockSpec((BAND, A_WIDTH), lambda i: (jnp.maximum(i - 1, 0), 0))
    body, c_in_specs, c_out_specs, c_out_shape, c_scratch, c_ins = _carry(cargo, 5, 2, 0, body, nblk)
    outs = pl.pallas_call(
        body, name=name, grid=(nblk,), in_specs=[cur, prev, cur, prev, cur] + c_in_specs,
        out_specs=[cur, cur] + c_out_specs,
        out_shape=[jax.ShapeDtypeStruct((S, A_WIDTH), F32)] * 2 + c_out_shape, scratch_shapes=c_scratch,
        compiler_params=_cparams(("arbitrary",)),
    )(q, k, k, v, v, *c_ins)
    return outs[0], outs[1], outs[2:]


def _carry(cargo, n_in, n_out, n_scratch, body, steps):
    if cargo is None:
        return body, [], [], [], [], []
    n_ci, n_co = len(cargo.ins), len(cargo.out_shape)

    def carrying(*refs):
        refs = list(refs)
        ins, refs = refs[:n_in], refs[n_in:]
        c_ins, refs = refs[:n_ci], refs[n_ci:]
        outs, refs = refs[:n_out], refs[n_out:]
        c_outs, refs = refs[:n_co], refs[n_co:]
        scratch, sems = refs[:n_scratch], refs[n_scratch:]

        @pl.when(pl.program_id(0) == 0)
        def _():
            cargo.start(c_ins, c_outs, sems)

        body(*ins, *outs, *scratch)

        @pl.when(pl.program_id(0) == steps - 1)
        def _():
            cargo.finish(c_ins, c_outs, sems)

    any_spec = pl.BlockSpec(memory_space=pl.ANY)
    return carrying, [any_spec] * n_ci, [any_spec] * n_co, list(cargo.out_shape), list(cargo.scratch), list(cargo.ins)


def _attn_bwd(name, q, k, v, o, lse, do, dlse, blocks_per_class, cargo=None):
    S = q.shape[0]
    nblk = S // BAND
    scale = A_HEAD_DIM ** -0.5

    def body(q_ref, kp_ref, kc_ref, vp_ref, vc_ref, o_ref, l_ref, do_ref, dl_ref,
             dq_ref, dk_ref, dv_ref, ck, cv):
        i = pl.program_id(0)

        @pl.when(i == 0)
        def _():
            ck[...] = jnp.zeros_like(ck)
            cv[...] = jnp.zeros_like(cv)

        @pl.when(i == nblk)
        def _():
            dk_ref[...] = ck[...]
            dv_ref[...] = cv[...]

        @pl.when(i < nblk)
        def _():
            has_prev = (i % blocks_per_class) != 0
            m_cur, m_prev = _band_masks()
            m_prev = jnp.logical_and(m_prev, has_prev)
            q, kc, kp, vc, vp = (_attn_heads(r) for r in (q_ref, kc_ref, kp_ref, vc_ref, vp_ref))
            do, o, dl = _attn_heads(do_ref), _attn_heads(o_ref), _attn_heads(dl_ref)
            lse = jnp.max(_attn_heads(l_ref), axis=-1, keepdims=True)
            p_c = jnp.where(m_cur[None], jnp.exp(_bdot(q, kc, "nt") * scale - lse), 0.0)
            p_p = jnp.where(m_prev[None], jnp.exp(_bdot(q, kp, "nt") * scale - lse), 0.0)
            corr = jnp.sum(dl, axis=-1, keepdims=True) - jnp.sum(do * o, axis=-1, keepdims=True)
            dob = do.astype(BF16)
            ds_c = (p_c * (_bdot(dob, vc, "nt") + corr)).astype(BF16)
            ds_p = (p_p * (_bdot(dob, vp, "nt") + corr)).astype(BF16)
            dq = (_bdot(ds_c, kc, "nn") + _bdot(ds_p, kp, "nn")) * scale
            dk_p, dk_c = _bdot(ds_p, q, "tn") * scale, _bdot(ds_c, q, "tn") * scale
            dv_p, dv_c = _bdot(p_p.astype(BF16), dob, "tn"), _bdot(p_c.astype(BF16), dob, "tn")
            for h in range(A_HEADS):
                sl = _attn_cols(h)
                dq_ref[:, sl] = dq[h]
                dk_ref[:, sl] = ck[:, sl] + dk_p[h]
                dv_ref[:, sl] = cv[:, sl] + dv_p[h]
                ck[:, sl] = dk_c[h]
                cv[:, sl] = dv_c[h]

    last = nblk - 1
    cur = pl.BlockSpec((BAND, A_WIDTH), lambda i: (jnp.minimum(i, last), 0))
    prev = pl.BlockSpec((BAND, A_WIDTH), lambda i: (jnp.minimum(jnp.maximum(i - 1, 0), last), 0))
    body, c_in_specs, c_out_specs, c_out_shape, c_scratch, c_ins = _carry(cargo, 9, 3, 2, body, nblk + 1)
    outs = pl.pallas_call(
        body, name=name, grid=(nblk + 1,),
        in_specs=[cur, prev, cur, prev, cur, cur, cur, cur, cur] + c_in_specs,
        out_specs=[cur, prev, prev] + c_out_specs,
        out_shape=[jax.ShapeDtypeStruct((S, A_WIDTH), F32)] * 3 + c_out_shape,
        scratch_shapes=[pltpu.VMEM((BAND, A_WIDTH), F32)] * 2 + c_scratch,
        compiler_params=_cparams(("arbitrary",)),
    )(q, k, k, v, v, o, lse, do, dlse, *c_ins)
    return outs[0], outs[1], outs[2], outs[3:]


def _merge_fwd_fn(o0, o1, o2, l0, l1, l2):
    m = jnp.maximum(jnp.maximum(l0, l1), l2)
    e0, e1, e2 = jnp.exp(l0 - m), jnp.exp(l1 - m), jnp.exp(l2 - m)
    return (e0 * o0 + e1 * o1 + e2 * o2) / (e0 + e1 + e2)


def _merge_bwd_fn(o0, o1, o2, l0, l1, l2, dom):
    m = jnp.maximum(jnp.maximum(l0, l1), l2)
    e0, e1, e2 = jnp.exp(l0 - m), jnp.exp(l1 - m), jnp.exp(l2 - m)
    den = e0 + e1 + e2
    w0, w1, w2 = e0 / den, e1 / den, e2 / den
    dw0, dw1, dw2 = dom * o0, dom * o1, dom * o2
    mean = w0 * dw0 + w1 * dw1 + w2 * dw2
    return w0 * dom, w1 * dom, w2 * dom, w0 * (dw0 - mean), w1 * (dw1 - mean), w2 * (dw2 - mean)


def _to_classes(t, d):
    if d == 1:
        return t
    S, C = t.shape
    return t.reshape(S // d, d, C).transpose(1, 0, 2).reshape(S, C)


def _from_classes(t, d):
    if d == 1:
        return t
    S, C = t.shape
    return t.reshape(d, S // d, C).transpose(1, 0, 2).reshape(S, C)


def _rope_lane_tables(positions):
    inv_freq = ROPE_THETA ** (-jnp.arange(0, ROPE_DIM, 2, dtype=F32) / ROPE_DIM)
    ang = positions.astype(F32)[:, None] * inv_freq
    cos, sin = jnp.cos(ang), jnp.sin(ang)
    S = positions.shape[0]
    rest = A_HEAD_DIM - ROPE_DIM
    ct = jnp.concatenate([cos, cos, jnp.ones((S, rest), F32)], axis=1)
    st = jnp.concatenate([-sin, sin, jnp.zeros((S, rest), F32)], axis=1)
    return jnp.tile(ct, (1, 2)), jnp.tile(st, (1, 2))


def _head_mean_matrix():
    r = jnp.arange(A_WIDTH) // A_HEAD_DIM
    return (r[:, None] == r[None, :]).astype(F32) * (1.0 / A_HEAD_DIM)


CONV_STRIP = 16
CONV_LANES = 512
PAST = CONV_WIDTH - 1


def _shifted_copies(padded, shifted, first):
    tm = shifted.shape[1]
    for j in range(PAST):
        shifted[j] = padded[pl.ds(first + j, tm), :]


def _strips(tm, step, carry):
    def run(s, c):
        return step(pl.ds(pl.multiple_of(s * CONV_STRIP, CONV_STRIP), CONV_STRIP), c)
    return lax.fori_loop(0, tm // CONV_STRIP, run, carry)


def _conv_fwd(name, proj, w):
    S = proj.shape[0]
    tm, tc = min(512, S), CONV_LANES
    per8 = tm // 8
    off = DN_QKV0 // tc

    def body(x_ref, halo_ref, w_ref, o_ref, xs, back):
        i = pl.program_id(0)
        xs[0:8, :] = jnp.where(i > 0, halo_ref[...], 0.0)
        xs[8:, :] = x_ref[...]
        _shifted_copies(xs, back, 8 - PAST)

        def step(rows, _):
            acc = w_ref[PAST:CONV_WIDTH, :] * x_ref[rows, :]
            for j in range(PAST):
                acc = acc + w_ref[j:j + 1, :] * back[j, rows, :]
            o_ref[rows, :] = acc
            return 0

        _strips(tm, step, 0)

    return pl.pallas_call(
        body, name=name, grid=(S // tm, DN_QKV // tc),
        in_specs=[pl.BlockSpec((tm, tc), lambda i, j: (i, j + off)),
                  pl.BlockSpec((8, tc), lambda i, j: (jnp.maximum(i * per8 - 1, 0), j + off)),
                  pl.BlockSpec((CONV_WIDTH, tc), lambda i, j: (0, j))],
        out_specs=pl.BlockSpec((tm, tc), lambda i, j: (i, j)),
        out_shape=jax.ShapeDtypeStruct((S, DN_QKV), F32),
        scratch_shapes=[pltpu.VMEM((tm + 8, tc), F32), pltpu.VMEM((PAST, tm, tc), F32)],
        compiler_params=_cparams(("parallel", "parallel")),
    )(proj, proj, w)


def _conv_bwd(name, proj, dpre, w):
    S = proj.shape[0]
    tm, tc = min(512, S), CONV_LANES
    per8 = tm // 8
    off = DN_QKV0 // tc
    last8 = S // 8 - 1
    nrow = S // tm

    def body(x_ref, xh_ref, d_ref, dh_ref, w_ref, dx_ref, dw_ref, xs, ds, back, ahead):
        i = pl.program_id(1)
        xs[0:8, :] = jnp.where(i > 0, xh_ref[...], 0.0)
        xs[8:, :] = x_ref[...]
        ds[0:tm, :] = d_ref[...]
        ds[tm:, :] = jnp.where(i < nrow - 1, dh_ref[...], 0.0)
        _shifted_copies(xs, back, 8 - PAST)
        _shifted_copies(ds, ahead, 1)

        def step(rows, sums):
            d = d_ref[rows, :]
            acc = w_ref[PAST:CONV_WIDTH, :] * d
            for j in range(PAST):
                acc = acc + w_ref[PAST - 1 - j:PAST - j, :] * ahead[j, rows, :]
            dx_ref[rows, :] = acc.astype(dx_ref.dtype)
            taps = [back[j, rows, :] for j in range(PAST)] + [x_ref[rows, :]]
            prods = [d * t for t in taps]
            return tuple(s + sum(p[r:r + 8] for r in range(0, CONV_STRIP, 8)) for s, p in zip(sums, prods))

        sums = _strips(tm, step, (jnp.zeros((8, tc), F32),) * CONV_WIDTH)

        @pl.when(i == 0)
        def _():
            dw_ref[...] = jnp.zeros_like(dw_ref)

        for j in range(CONV_WIDTH):
            dw_ref[j:j + 1, :] += jnp.sum(sums[j], axis=0, keepdims=True)

    return pl.pallas_call(
        body, name=name, grid=(DN_QKV // tc, nrow),
        in_specs=[pl.BlockSpec((tm, tc), lambda j, i: (i, j + off)),
                  pl.BlockSpec((8, tc), lambda j, i: (jnp.maximum(i * per8 - 1, 0), j + off)),
                  pl.BlockSpec((tm, tc), lambda j, i: (i, j)),
                  pl.BlockSpec((8, tc), lambda j, i: (jnp.minimum((i + 1) * per8, last8), j)),
                  pl.BlockSpec((CONV_WIDTH, tc), lambda j, i: (0, j))],
        out_specs=[pl.BlockSpec((tm, tc), lambda j, i: (i, j)),
                   pl.BlockSpec((CONV_WIDTH, tc), lambda j, i: (0, j))],
        out_shape=[jax.ShapeDtypeStruct((S, DN_QKV), BF16), jax.ShapeDtypeStruct((CONV_WIDTH, DN_QKV), F32)],
        scratch_shapes=[pltpu.VMEM((tm + 8, tc), F32)] * 2 + [pltpu.VMEM((PAST, tm, tc), F32)] * 2,
        compiler_params=_cparams(("parallel", "arbitrary")),
    )(proj, proj, dpre, dpre, w)


def _gate_lane(shape):
    return lax.broadcasted_iota(jnp.int32, shape, 1)


GATES_ROWS = 256


def _chunk_cumsum_matrix():
    r = jnp.arange(GATES_ROWS)
    return ((r[:, None] >= r[None, :]) & (r[:, None] // CHUNK == r[None, :] // CHUNK)).astype(F32)


def _gates_fwd_fn(ab, alog, dt, cum):
    g = -jnp.exp(alog) * _softplus(ab + dt)
    gc = jnp.dot(cum, g, precision=HIGHEST, preferred_element_type=F32)
    return jnp.where(_gate_lane(ab.shape) < DN_HEADS, gc, _sigmoid(ab))


def _gates_bwd_fn(ab, dgb, alog, dt, cum):
    lane = _gate_lane(ab.shape)
    is_g = lane < DN_HEADS
    neg_a = -jnp.exp(alog)
    sp = _softplus(ab + dt)
    dsp = _sigmoid(ab + dt)
    beta = _sigmoid(ab)
    dgc = jnp.where(is_g, dgb, 0.0)
    dg = lax.dot_general(cum, dgc, (_DIMS["tn"], ((), ())), precision=HIGHEST, preferred_element_type=F32)
    dab = jnp.where(is_g, dg * neg_a * dsp, jnp.where(lane < 2 * DN_HEADS, dgb * beta * (1.0 - beta), 0.0))
    d_alog = jnp.sum(dg * neg_a * sp, axis=0, keepdims=True)
    d_dt = jnp.sum(dg * neg_a * dsp, axis=0, keepdims=True)
    return dab, d_alog, d_dt


def _chunk_math(precision):
    def dg(a, b, mode, prec=precision):
        return _bdot(a, b, mode, prec)

    @jax.custom_vjp
    def nn(a, b):
        return dg(a, b, "nn")

    @jax.custom_vjp
    def nt(a, b):
        return dg(a, b, "nt")

    @jax.custom_vjp
    def tn(a, b):
        return dg(a, b, "tn")

    nn.defvjp(lambda a, b: (nn(a, b), (a, b)), lambda r, g: (nt(g, r[1]), tn(r[0], g)))
    nt.defvjp(lambda a, b: (nt(a, b), (a, b)), lambda r, g: (nn(g, r[1]), tn(g, r[0])))
    tn.defvjp(lambda a, b: (tn(a, b), (a, b)), lambda r, g: (nt(r[1], g), nn(r[0], g)))

    def split(x):
        hi = x.astype(BF16)
        return hi, (x - hi.astype(F32)).astype(BF16)

    def fine(a, b, mode):
        ah, al = split(a)
        bh, bl = split(b)
        return dg(ah, bh, mode, None) + (dg(ah, bl, mode, None) + dg(al, bh, mode, None))

    def unit_lower_inverse(a):
        row = lax.broadcasted_iota(jnp.int32, a.shape, 1)
        col = lax.broadcasted_iota(jnp.int32, a.shape, 2)
        x = -a
        p = jnp.where(row == col, 1.0, 0.0) + x
        for _ in range(int(math.log2(CHUNK)) - 1):
            x = fine(x, x, "nn")
            p = p + fine(p, x, "nn")
        return p

    @jax.custom_vjp
    def solve2(a, ti, r1, r2):
        return fine(ti, r1, "nn"), fine(ti, r2, "nn")

    def solve2_fwd(a, ti, r1, r2):
        s1, s2 = fine(ti, r1, "nn"), fine(ti, r2, "nn")
        return (s1, s2), (ti, s1, s2)

    def solve2_bwd(res, g):
        ti, s1, s2 = res
        d1, d2 = fine(ti, g[0], "tn"), fine(ti, g[1], "tn")
        return -(fine(d1, s1, "nt") + fine(d2, s2, "nt")), jnp.zeros_like(ti), d1, d2

    solve2.defvjp(solve2_fwd, solve2_bwd)

    def chunk_fn(pq, pk, pv, z, g_col, b_col, g_row, ogain, s_in, inverse=None):
        nb = pq.shape[0]
        sq = (nb, CHUNK, CHUNK)
        row = lax.broadcasted_iota(jnp.int32, sq, 1)
        col = lax.broadcasted_iota(jnp.int32, sq, 2)
        lower, strict = row >= col, row > col
        q, k, v = _silu(pq), _silu(pk), _silu(pv)
        q = q * lax.rsqrt(jnp.sum(q * q, axis=-1, keepdims=True) + EPS) * (DN_HEAD_DIM ** -0.5)
        k = k * lax.rsqrt(jnp.sum(k * k, axis=-1, keepdims=True) + EPS)
        gc_wide = jnp.broadcast_to(g_col, pq.shape)
        gc_i = jnp.broadcast_to(g_col, sq)
        gc_j = jnp.broadcast_to(g_row, sq)
        is_last = lax.broadcasted_iota(jnp.int32, pq.shape, 1) == CHUNK - 1
        g_last = jnp.sum(jnp.where(is_last, gc_wide, 0.0), axis=1, keepdims=True)
        decay = jnp.exp(jnp.where(lower, gc_i - gc_j, -jnp.inf))
        kb = k * b_col
        a_mat = jnp.where(strict, nt(kb, k) * decay, 0.0)
        eg = jnp.exp(gc_wide)
        ti = unit_lower_inverse(a_mat) if inverse is None else inverse
        u, w = solve2(a_mat, ti, v * b_col, kb * eg)
        attn = nt(q, k) * decay
        q_dec = q * eg
        k_dec = k * jnp.exp(g_last - gc_wide)
        c_dec = jnp.exp(g_last)
        v_new = u - nn(w, s_in)
        o = nn(q_dec, s_in) + nn(attn, v_new)
        s_out = s_in * c_dec + tn(k_dec, v_new)
        y = o * lax.rsqrt(jnp.mean(o * o, axis=-1, keepdims=True) + EPS) * ogain * _silu(z)
        return (y, s_out, ti) if inverse is None else (y, s_out)

    return chunk_fn


DN_PRECISION = None


def _chunk_specs(n_of):
    groups = DN_HEADS // DN_HB
    wide = DN_HB * DN_HEAD_DIM
    hd = pl.BlockSpec((CHUNK, wide), lambda h, n: (n_of(n), h))
    specs = dict(
        pq=hd,
        pk=pl.BlockSpec((CHUNK, wide), lambda h, n: (n_of(n), groups + h)),
        pv=pl.BlockSpec((CHUNK, wide), lambda h, n: (n_of(n), 2 * groups + h)),
        z=hd,
        gates=pl.BlockSpec((CHUNK, DN_AB_PAD), lambda h, n: (n_of(n), 0)),
        row=pl.BlockSpec((DN_HB, None, 1, CHUNK), lambda h, n: (h, n_of(n), 0, 0)),
        gain=pl.BlockSpec((1, DN_HEAD_DIM), lambda h, n: (0, 0)),
        state=pl.BlockSpec((DN_HB, None, DN_HEAD_DIM, DN_HEAD_DIM), lambda h, n: (h, n_of(n), 0, 0)),
        inverse=pl.BlockSpec((DN_HB, None, CHUNK, CHUNK), lambda h, n: (h, n_of(n), 0, 0)),
        qkv=pl.BlockSpec((CHUNK, DN_QKV), lambda h, n: (n_of(n), 0)),
        head=hd,
    )
    return specs


def _head_cols(j):
    return slice(j * DN_HEAD_DIM, (j + 1) * DN_HEAD_DIM)


def _split_heads(ref):
    return jnp.stack([ref[:, _head_cols(j)] for j in range(DN_HB)])


def _gate_columns(gates, first_lane):
    lane = lax.broadcasted_iota(jnp.int32, gates.shape, 1)
    return jnp.stack([jnp.sum(jnp.where(lane == first_lane + h, gates, 0.0), axis=-1, keepdims=True)
                      for h in range(DN_HEADS)])


def _gate_lanes(columns, first_lane):
    shape = (columns.shape[1], DN_AB_PAD)
    lane = lax.broadcasted_iota(jnp.int32, shape, 1)
    out = jnp.zeros(shape, F32)
    for h in range(DN_HEADS):
        out = out + jnp.where(lane == first_lane + h, columns[h], 0.0)
    return out


def _chunk_fwd(name, pre, proj, gates, g_row, ogain):
    assert DN_HB == DN_HEADS
    S = pre.shape[0]
    N = S // CHUNK
    chunk_fn = _chunk_math(DN_PRECISION)
    sp = _chunk_specs(lambda n: n)

    def body(pq, pk, pv, z, gb, gr, og, y_ref, sin_ref, inv_ref, st):
        @pl.when(pl.program_id(1) == 0)
        def _():
            st[...] = jnp.zeros_like(st)

        s_in = st[...]
        sin_ref[...] = s_in
        y, s_out, inverse = chunk_fn(_split_heads(pq), _split_heads(pk), _split_heads(pv), _split_heads(z),
                                     _gate_columns(gb[...], 0), _gate_columns(gb[...], DN_HEADS), gr[...],
                                     og[...], s_in)
        for j in range(DN_HB):
            y_ref[:, _head_cols(j)] = y[j].astype(y_ref.dtype)
        inv_ref[...] = inverse
        st[...] = s_out

    return pl.pallas_call(
        body, name=name, grid=(DN_HEADS // DN_HB, N),
        in_specs=[sp["pq"], sp["pk"], sp["pv"], sp["z"], sp["gates"], sp["row"], sp["gain"]],
        out_specs=[sp["head"], sp["state"], sp["inverse"]],
        out_shape=[jax.ShapeDtypeStruct((S, DN_WIDTH), BF16),
                   jax.ShapeDtypeStruct((DN_HEADS, N, DN_HEAD_DIM, DN_HEAD_DIM), F32),
                   jax.ShapeDtypeStruct((DN_HEADS, N, CHUNK, CHUNK), F32)],
        scratch_shapes=[pltpu.VMEM((DN_HB, DN_HEAD_DIM, DN_HEAD_DIM), F32)],
        compiler_params=_cparams(("parallel", "arbitrary")),
    )(pre, pre, pre, proj, gates, g_row, ogain)


def _chunk_bwd(name, pre, proj, gates, g_row, ogain, s_in_all, inverse_all, dy):
    assert DN_HB == DN_HEADS
    S = pre.shape[0]
    N = S // CHUNK
    chunk_fn = _chunk_math(DN_PRECISION)
    sp = _chunk_specs(lambda n: N - 1 - n)

    def body(pq, pk, pv, z, gb, gr, og, sin_ref, inv_ref, dy_ref,
             dpre_ref, dz_ref, dgb_ref, dgr_ref, dog_ref, ds):
        @pl.when(pl.program_id(1) == 0)
        def _():
            ds[...] = jnp.zeros_like(ds)
            dog_ref[...] = jnp.zeros_like(dog_ref)

        inverse = inv_ref[...]
        prim = (_split_heads(pq), _split_heads(pk), _split_heads(pv), _split_heads(z),
                _gate_columns(gb[...], 0), _gate_columns(gb[...], DN_HEADS), gr[...], og[...], sin_ref[...])
        _, vjp = jax.vjp(lambda *a: chunk_fn(*a, inverse=inverse), *prim)
        gq, gk, gv, gz, ggc, gbc, ggr, gog, gs = vjp((_split_heads(dy_ref), ds[...]))
        for j in range(DN_HB):
            for part, g in enumerate((gq, gk, gv)):
                dpre_ref[:, pl.ds(part * DN_WIDTH + j * DN_HEAD_DIM, DN_HEAD_DIM)] = g[j]
            dz_ref[:, _head_cols(j)] = gz[j]
        dgb_ref[...] = _gate_lanes(ggc, 0) + _gate_lanes(gbc, DN_HEADS)
        dgr_ref[...] = ggr
        dog_ref[...] += gog
        ds[...] = gs

    hd = sp["head"]
    return pl.pallas_call(
        body, name=name, grid=(1, N),
        in_specs=[sp["pq"], sp["pk"], sp["pv"], sp["z"], sp["gates"], sp["row"], sp["gain"],
                  sp["state"], sp["inverse"], hd],
        out_specs=[sp["qkv"], hd, sp["gates"], sp["row"], sp["gain"]],
        out_shape=[jax.ShapeDtypeStruct((S, DN_QKV), F32), jax.ShapeDtypeStruct((S, DN_WIDTH), F32),
                   jax.ShapeDtypeStruct((S, DN_AB_PAD), F32),
                   jax.ShapeDtypeStruct((DN_HEADS, N, 1, CHUNK), F32), jax.ShapeDtypeStruct((1, DN_HEAD_DIM), F32)],
        scratch_shapes=[pltpu.VMEM((DN_HB, DN_HEAD_DIM, DN_HEAD_DIM), F32)],
        compiler_params=_cparams(("arbitrary", "arbitrary")),
    )(pre, pre, pre, proj, gates, g_row, ogain, s_in_all, inverse_all, dy)


def _mm_rms_bwd(name, d_out, w, x, gain, dres):
    return _mm(name, d_out, w, "nt", out_dtypes=(F32, BF16), extras=(x, dres), rows=(gain.reshape(1, -1),),
               epilogue=lambda acc, x_, dres_, g: _rms_bwd_fn(x_, dres_, acc, g), n_sums=1, tm_cap=512)


def _residual_norm_epilogue(acc, res, gain):
    x = acc + res
    return x, _rms_fwd_fn(x, gain)


def _ple_loss_fn(x_mid, pp, zg, t):
    gate = _sigmoid(zg)
    err = x_mid + pp * gate - t
    dy = err * (1.0 / D_MODEL)
    return dy, dy * gate, dy * pp * gate * (1.0 - gate), jnp.broadcast_to(jnp.sum(err * err, keepdims=True), (1, 128))


def _mlp_ple_fwd(tag, x_in, h, p_l, w_up, w_down, norm_ple, w_ple, w_gate, next_gain=None, target=None):
    a = _mm(f"{tag}_up", h, w_up, "nn", out_dtypes=(BF16,), epilogue=_relu2_epilogue)
    x_mid, hg = _mm(f"{tag}_down", a, w_down, "nn", out_dtypes=(F32, BF16), extras=(x_in,),
                    rows=(norm_ple.reshape(1, -1),), epilogue=_residual_norm_epilogue)
    zg = _mm(f"{tag}_gate", hg, w_gate, "nn")
    pp = _mm(f"{tag}_ple", p_l, w_ple, "nn")
    sv = dict(x_in=x_in, h=h, a=a, x_mid=x_mid, hg=hg, zg=zg, pp=pp)
    if target is not None:
        sv["dy"], sv["dpp"], sv["dzg"], sv["sq"] = _rowwise(
            f"{tag}_ple_loss", _ple_loss_fn, [x_mid, pp, zg, target], [],
            [(D_MODEL, F32), (D_MODEL, BF16), (D_MODEL, BF16)], [(1, 128)])
        return None, None, sv
    if next_gain is None:
        return _rowwise(f"{tag}_ple_out", _ple_fwd_fn, [x_mid, pp, zg], [], [(D_MODEL, F32)]), None, sv
    x_out, h_next = _rowwise(f"{tag}_ple_out", _ple_norm_fwd_fn, [x_mid, pp, zg], [next_gain.reshape(1, -1)],
                             [(D_MODEL, F32), (D_MODEL, BF16)])
    return x_out, h_next, sv


def _mlp_ple_bwd(tag, dx, sv, p_l, norm_mlp, w_up, w_down, norm_ple, w_ple, w_gate):
    if "dpp" in sv:
        dpp, dzg = sv["dpp"], sv["dzg"]
    else:
        dpp, dzg = _rowwise(f"{tag}_ple_bwd", _ple_bwd_fn, [dx, sv["pp"], sv["zg"]], [],
                            [(D_MODEL, BF16), (D_MODEL, BF16)])
    d_w_ple = _mm(f"{tag}_d_w_ple", p_l, dpp, "tn", out_dtypes=(BF16,))
    d_w_gate = _mm(f"{tag}_d_w_gate", sv["hg"], dzg, "tn", out_dtypes=(BF16,))
    dx_mid, dx_mid_b, d_norm_ple = _mm_rms_bwd(f"{tag}_d_hg", dzg, w_gate, sv["x_mid"], norm_ple, dx)
    du = _mm(f"{tag}_d_u", dx_mid_b, w_down, "nt", out_dtypes=(BF16,), extras=(sv["a"],),
             epilogue=_relu2_bwd_epilogue)
    d_w_down = _mm(f"{tag}_d_w_down", sv["a"], dx_mid_b, "tn", out_dtypes=(BF16,))
    d_w_up = _mm(f"{tag}_d_w_up", sv["h"], du, "tn", out_dtypes=(BF16,))
    dx_in, dx_in_b, d_norm_mlp = _mm_rms_bwd(f"{tag}_d_h", du, w_up, sv["x_in"], norm_mlp, dx_mid)
    return dx_in, dx_in_b, dict(mlp_norm=d_norm_mlp, w_up=d_w_up, w_down=d_w_down, ple_norm=d_norm_ple,
                                w_ple=d_w_ple, w_ple_gate=d_w_gate)


class _NoHooks:
    first_cargo = None
    fwd_cargo = (None,) * len(SWA_GROUPS)

    def first_weights(self, results):
        return {}

    def weights_from(self, results):
        return {}

    def split_cargo(self, early_grads):
        return None

    def bwd_cargo(self, results):
        return (None,) * len(SWA_GROUPS)

    def last_cargo(self, attn_grads):
        return None


def _with_cargo(result, cargo):
    return (result, ()) if cargo is None else result


def _local_step(x, p, positions, target, small, big, hooks=_NoHooks()):
    S = x.shape[0]
    ct, st = _rope_lane_tables(positions)
    bd = _head_mean_matrix()

    h0, first = _with_cargo(_rms_fwd("l0_mix_norm", x, small["mix_norm"][0], cargo=hooks.first_cargo),
                            hooks.first_cargo)
    big = {**big, **hooks.first_weights(first)}
    attn, brought = [], []
    for g, (window, d) in enumerate(SWA_GROUPS):
        assert window // d == BAND and (S // d) % BAND == 0
        h0g = _to_classes(h0, d)
        ctg, stg = _to_classes(ct, d), _to_classes(st, d)
        w_g = big["attn_w_qkv"][:, g * 3 * A_WIDTH:(g + 1) * 3 * A_WIDTH]
        gq = jnp.tile(small["attn_q_gain"][0, g], A_HEADS).reshape(1, A_WIDTH)
        gk = jnp.tile(small["attn_k_gain"][0, g], A_HEADS).reshape(1, A_WIDTH)
        qkv = _mm(f"l0_qkv{g}", h0g, w_g, "nn")
        q, k, v = _rowwise(f"l0_qk_prep{g}", _qk_prep_fwd_fn, [qkv, ctg, stg], [gq, gk, bd], [(A_WIDTH, BF16)] * 3)
        o, lse, cargo_out = _attn_fwd(f"l0_attn{g}", q, k, v, (S // d) // BAND, cargo=hooks.fwd_cargo[g])
        brought.append(cargo_out)
        attn.append(dict(d=d, h0g=h0g, ct=ctg, st=stg, w=w_g, gq=gq, gk=gk, qkv=qkv, q=q, k=k, v=v, o=o, lse=lse,
                         o_tok=_from_classes(o, d), lse_tok=_from_classes(lse, d)))
    big = {**big, **hooks.weights_from(brought)}
    om = _rowwise("l0_merge", _merge_fwd_fn, [a["o_tok"] for a in attn] + [a["lse_tok"] for a in attn], [],
                  [(A_WIDTH, BF16)])
    x1, h1 = _mm("l0_attn_out", om, big["attn_w_o"], "nn", out_dtypes=(F32, BF16), extras=(x,),
                 rows=(small["mlp_norm"][0].reshape(1, -1),), epilogue=_residual_norm_epilogue)
    x3, h3, sv0 = _mlp_ple_fwd("l0", x1, h1, p[0], big["w_up"][0], big["w_down"][0], small["ple_norm"][0],
                               big["w_ple"][0], big["w_ple_gate"][0], next_gain=small["mix_norm"][1])

    N = S // CHUNK
    proj = _mm("l1_in", h3, big["dn_w_in"], "nn")
    pre = _conv_fwd("l1_conv", proj, small["dn_conv"])
    ab = proj[:, DN_AB0:DN_AB0 + DN_AB_PAD]
    lane_pad = DN_AB_PAD - DN_HEADS
    alog_row = jnp.pad(small["dn_a_log"][0], (0, lane_pad)).reshape(1, DN_AB_PAD)
    dt_row = jnp.pad(small["dn_dt_bias"][0], (0, lane_pad)).reshape(1, DN_AB_PAD)
    cum = _chunk_cumsum_matrix()
    gb = _rowwise("l1_gates", _gates_fwd_fn, [ab], [alog_row, dt_row, cum], [(DN_AB_PAD, F32)], tm=GATES_ROWS)
    g_row = gb[:, :DN_HEADS].T.reshape(DN_HEADS, N, 1, CHUNK)
    ogain = small["dn_o_gain"][0].reshape(1, DN_HEAD_DIM)
    y, s_in_all, inverse_all = _chunk_fwd("l1_delta", pre, proj, gb, g_row, ogain)
    x4, h4 = _mm("l1_dn_out", y, big["dn_w_o"], "nn", out_dtypes=(F32, BF16), extras=(x3,),
                 rows=(small["mlp_norm"][1].reshape(1, -1),), epilogue=_residual_norm_epilogue)
    _, _, sv1 = _mlp_ple_fwd("l1", x4, h4, p[1], big["w_up"][1], big["w_down"][1], small["ple_norm"][1],
                             big["w_ple"][1], big["w_ple_gate"][1], target=target)
    dy, sq = sv1["dy"], sv1["sq"]

    dx4, dx4_b, gl1 = _mlp_ple_bwd("l1", dy, sv1, p[1], small["mlp_norm"][1], big["w_up"][1], big["w_down"][1],
                            small["ple_norm"][1], big["w_ple"][1], big["w_ple_gate"][1])
    d_y = _mm("l1_d_y", dx4_b, big["dn_w_o"], "nt")
    d_dn_w_o = _mm("l1_d_w_o", y, dx4_b, "tn", out_dtypes=(BF16,))
    dpre, dz, dgb_cols, dg_row, d_ogain = _chunk_bwd(
        "l1_delta_bwd", pre, proj, gb, g_row, ogain, s_in_all, inverse_all, d_y)
    dconv_in, d_conv_w = _conv_bwd("l1_conv_bwd", proj, dpre, small["dn_conv"])
    dgb = dgb_cols + jnp.pad(dg_row.reshape(DN_HEADS, S).T, ((0, 0), (0, DN_AB_PAD - DN_HEADS)))
    dab, d_alog, d_dt = _rowwise("l1_gates_bwd", _gates_bwd_fn, [ab, dgb], [alog_row, dt_row, cum],
                                 [(DN_AB_PAD, F32)], [(1, DN_AB_PAD), (1, DN_AB_PAD)], tm=GATES_ROWS)
    dproj = jnp.concatenate([dz.astype(BF16), dconv_in, dab.astype(BF16)], axis=1)
    d_dn_w_in = _mm("l1_d_w_in", h3, dproj, "tn", out_dtypes=(BF16,))
    dx3, _, d_mix1 = _mm_rms_bwd("l1_d_h", dproj, big["dn_w_in"], x3, small["mix_norm"][1], dx4)

    dx1, dx1_b, gl0 = _mlp_ple_bwd("l0", dx3, sv0, p[0], small["mlp_norm"][0], big["w_up"][0], big["w_down"][0],
                            small["ple_norm"][0], big["w_ple"][0], big["w_ple_gate"][0])
    early = dict(
        dn_w_in=jnp.concatenate([d_dn_w_in[:, DN_QKV0:DN_AB0 + 2 * DN_HEADS], d_dn_w_in[:, :DN_WIDTH]], axis=1),
        dn_w_o=d_dn_w_o,
        w_up=jnp.stack([gl0["w_up"], gl1["w_up"]]),
        w_down=jnp.stack([gl0["w_down"], gl1["w_down"]]),
        w_ple=jnp.stack([gl0["w_ple"], gl1["w_ple"]]),
        w_ple_gate=jnp.stack([gl0["w_ple_gate"], gl1["w_ple_gate"]]))
    split_cargo = hooks.split_cargo(early)
    dom = _mm("l0_d_om", dx1_b, big["attn_w_o"], "nt")
    d_attn_w_o = _mm("l0_d_w_o", om, dx1_b, "tn", out_dtypes=(BF16,))
    merged, split = _with_cargo(
        _rowwise("l0_merge_bwd", _merge_bwd_fn, [a["o_tok"] for a in attn] + [a["lse_tok"] for a in attn] + [dom], [],
                 [(A_WIDTH, F32)] * 6, cargo=split_cargo), split_cargo)
    bwd_cargo = hooks.bwd_cargo(split)
    dh0, d_w_qkv, d_gq, d_gk, brought_bwd = [], [], [], [], []
    for g, a in enumerate(attn):
        do_g, dl_g = _to_classes(merged[g], a["d"]), _to_classes(merged[3 + g], a["d"])
        dqn, dkn, dvn, cargo_out = _attn_bwd(f"l0_attn_bwd{g}", a["q"], a["k"], a["v"], a["o"], a["lse"], do_g, dl_g,
                                             (S // a["d"]) // BAND, cargo=bwd_cargo[g])
        brought_bwd.append(cargo_out)
        dqkv, dgq, dgk = _rowwise(f"l0_qk_prep_bwd{g}", _qk_prep_bwd_fn, [a["qkv"], a["ct"], a["st"], dqn, dkn, dvn],
                                  [a["gq"], a["gk"], bd], [(3 * A_WIDTH, BF16)], [(1, A_HEAD_DIM)] * 2)
        d_w_qkv.append(_mm(f"l0_d_w_qkv{g}", a["h0g"], dqkv, "tn", out_dtypes=(BF16,)))
        dh0.append(_from_classes(_mm(f"l0_d_h{g}", dqkv, a["w"], "nt"), a["d"]))
        d_gq.append(dgq)
        d_gk.append(dgk)
    attn_grads = dict(attn_w_qkv=jnp.concatenate(d_w_qkv, axis=1), attn_w_o=d_attn_w_o)
    last_cargo = hooks.last_cargo(attn_grads)
    (grad_x, _, d_mix0), last = _with_cargo(
        _rms_bwd("l0_mix_norm_bwd", x, small["mix_norm"][0], dx1, dh0, cargo=last_cargo), last_cargo)
    brought_bwd.append(last)

    grads = dict(
        mix_norm=jnp.concatenate([d_mix0, d_mix1], axis=0),
        attn_q_gain=jnp.concatenate(d_gq, axis=0)[None],
        attn_k_gain=jnp.concatenate(d_gk, axis=0)[None],
        **attn_grads,
        dn_conv=d_conv_w,
        dn_a_log=d_alog[:, :DN_HEADS],
        dn_dt_bias=d_dt[:, :DN_HEADS],
        dn_o_gain=d_ogain,
        mlp_norm=jnp.concatenate([gl0["mlp_norm"], gl1["mlp_norm"]], axis=0),
        ple_norm=jnp.concatenate([gl0["ple_norm"], gl1["ple_norm"]], axis=0),
        **early,
    )
    return sq, grad_x, grads, brought_bwd


def _chip_peer(x, y, c, t):
    return (jnp.bitwise_xor(x, t >> 1), jnp.bitwise_xor(y, t & 1), c)


def _place():
    x, y, c = lax.axis_index("x"), lax.axis_index("y"), lax.axis_index("c")
    return x, y, c, 2 * x + y, (x, y, 1 - c)


def _remote(src, dst, send_sem, recv_sem, to):
    return pltpu.make_async_remote_copy(src_ref=src, dst_ref=dst, send_sem=send_sem, recv_sem=recv_sem,
                                        device_id=to, device_id_type=MESH)


def _hbm_call(name, body, ins, out_shape, scratch_shapes):
    any_spec = pl.BlockSpec(memory_space=pl.ANY)
    return pl.pallas_call(body, name=name, out_shape=out_shape, in_specs=[any_spec] * len(ins),
                          out_specs=[any_spec] * len(out_shape), scratch_shapes=scratch_shapes)(*ins)


def _half(n0, which):
    return pl.ds(which * (n0 // 2), n0 // 2)


class _Exchange:
    def __init__(self, ins, out_shape, scratch, start, finish):
        self.ins, self.out_shape, self.scratch, self.start, self.finish = ins, out_shape, scratch, start, finish


def _run_exchange(name, ex):
    n_in, n_out = len(ex.ins), len(ex.out_shape)

    def body(*refs):
        ins, outs, sems = refs[:n_in], refs[n_in:n_in + n_out], refs[n_in + n_out:]
        ex.start(ins, outs, sems)
        ex.finish(ins, outs, sems)

    return _hbm_call(name, body, ex.ins, ex.out_shape, ex.scratch)


def _gather_exchange(shards):
    T = len(shards)

    pairs = [(i, t) for i in range(T) for t in range(1, N_CHIPS)]

    def copies(ins, outs, sems):
        send, recv = sems
        x, y, c, q, sibling = _place()

        def half(i, which):
            return _half(ins[i].shape[0], which)

        def over_ici(i, t):
            return _remote(ins[i].at[half(i, c)], outs[i].at[q, half(i, c)], send.at[i, t - 1], recv.at[i, t - 1],
                           _chip_peer(x, y, c, t))

        def landing(i, t):
            spot = outs[i].at[jnp.bitwise_xor(q, t), half(i, c)]
            return _remote(spot, spot, send.at[i, t - 1], recv.at[i, t - 1], _chip_peer(x, y, c, t))

        def forward(i, t):
            spot = outs[i].at[jnp.bitwise_xor(q, t), half(i, c)]
            return _remote(spot, spot, send.at[i, 2 + t], recv.at[i, 2 + t], sibling)

        def forwarded(i, t):
            spot = outs[i].at[jnp.bitwise_xor(q, t), half(i, 1 - c)]
            return _remote(spot, spot, send.at[i, 2 + t], recv.at[i, 2 + t], sibling)

        return over_ici, landing, forward, forwarded

    def start(ins, outs, sems):
        over_ici = copies(ins, outs, sems)[0]
        for i, t in pairs:
            over_ici(i, t).start()

    def finish(ins, outs, sems):
        over_ici, landing, forward, forwarded = copies(ins, outs, sems)
        for i, t in pairs:
            landing(i, t).wait_recv()
            forward(i, t).start()
        for i, t in pairs:
            forwarded(i, t).wait_recv()
        for i, t in pairs:
            over_ici(i, t).wait_send()
            forward(i, t).wait_send()

    n_rel = 2 * (N_CHIPS - 1)
    return _Exchange(list(shards), [jax.ShapeDtypeStruct((N_CHIPS,) + s.shape, s.dtype) for s in shards],
                     [pltpu.SemaphoreType.DMA((T, n_rel)), pltpu.SemaphoreType.DMA((T, n_rel))], start, finish)


def _scatter_exchange(stacks):
    T = len(stacks)

    def copies(ins, outs, sems):
        send, recv = sems
        x, y, c, q, sibling = _place()
        return [_remote(ins[i].at[jnp.bitwise_xor(q, t)], outs[i].at[t - 1], send.at[i, t - 1], recv.at[i, t - 1],
                        _chip_peer(x, y, c, t)) for i in range(T) for t in range(1, N_CHIPS)]

    def start(ins, outs, sems):
        for cp in copies(ins, outs, sems):
            cp.start()

    def finish(ins, outs, sems):
        for cp in copies(ins, outs, sems):
            cp.wait()

    return _Exchange(list(stacks), [jax.ShapeDtypeStruct((N_CHIPS - 1,) + s.shape[1:], s.dtype) for s in stacks],
                     [pltpu.SemaphoreType.DMA((T, N_CHIPS - 1)), pltpu.SemaphoreType.DMA((T, N_CHIPS - 1))],
                     start, finish)


def _other_half_exchange(stacks):
    T = len(stacks)

    def copies(ins, outs, sems):
        send, recv = sems
        x, y, c, q, sibling = _place()
        return [_remote(ins[i].at[:, _half(ins[i].shape[1], 1 - c)], outs[i], send.at[i], recv.at[i], sibling)
                for i in range(T)]

    def start(ins, outs, sems):
        for cp in copies(ins, outs, sems):
            cp.start()

    def finish(ins, outs, sems):
        for cp in copies(ins, outs, sems):
            cp.wait()

    return _Exchange(list(stacks),
                     [jax.ShapeDtypeStruct((s.shape[0], s.shape[1] // 2) + s.shape[2:], s.dtype) for s in stacks],
                     [pltpu.SemaphoreType.DMA((T,)), pltpu.SemaphoreType.DMA((T,))], start, finish)


def _swap_with_sibling(name, arrays):
    T = len(arrays)

    def body(*refs):
        ins, outs = refs[:T], refs[T:2 * T]
        send, recv = refs[2 * T:]
        x, y, c, q, sibling = _place()
        copies = []
        for i in range(T):
            rc = _remote(ins[i], outs[i], send.at[i], recv.at[i], sibling)
            rc.start()
            copies.append(rc)
        for cp in copies:
            cp.wait()

    return _hbm_call(name, body, arrays, [jax.ShapeDtypeStruct(a.shape, a.dtype) for a in arrays],
                     [pltpu.SemaphoreType.DMA((T,)), pltpu.SemaphoreType.DMA((T,))])


def _gather_from_all(name, block):
    R, C = block.shape

    def body(src, out, send_sems, recv_sems):
        x, y, c = lax.axis_index("x"), lax.axis_index("y"), lax.axis_index("c")
        me = 4 * x + 2 * y + c
        out[me] = src[...]
        copies = []
        for r in range(1, N_DEV):
            peer = (jnp.bitwise_xor(x, r >> 2), jnp.bitwise_xor(y, (r >> 1) & 1), jnp.bitwise_xor(c, r & 1))
            cp = pltpu.make_async_remote_copy(src_ref=src, dst_ref=out.at[me], send_sem=send_sems.at[r - 1],
                                              recv_sem=recv_sems.at[r - 1], device_id=peer, device_id_type=MESH)
            cp.start()
            copies.append(cp)
        for cp in copies:
            cp.wait()

    return pl.pallas_call(
        body, name=name, out_shape=jax.ShapeDtypeStruct((N_DEV, R, C), block.dtype),
        in_specs=[pl.BlockSpec(memory_space=pltpu.VMEM)], out_specs=pl.BlockSpec(memory_space=pltpu.VMEM),
        scratch_shapes=[pltpu.SemaphoreType.DMA((N_DEV - 1,)), pltpu.SemaphoreType.DMA((N_DEV - 1,))],
    )(block)


def _view(a):
    return a[0] if a.shape[0] == 1 else a


def _view_axis(a, axis):
    return axis - 1 if a.shape[0] == 1 else axis


def _rows(a):
    return a.reshape(-1, a.shape[-1])


def _elementwise(name, fn, ins, out_dtypes, tm):
    specs = []
    for a in ins:
        a, row0 = a if isinstance(a, tuple) else (a, 0)
        specs.append((_rows(a), a.shape[-1], 0, row0))
    shape = ins[0][0].shape if isinstance(ins[0], tuple) else ins[0].shape
    outs = _rowwise(name, fn, specs, [], [(shape[-1], dt) for dt in out_dtypes], tm=tm, n_rows=math.prod(shape[:-1]))
    return outs.reshape(shape) if len(out_dtypes) == 1 else [o.reshape(shape) for o in outs]


SMALL_ROWS = 8
CONV_ROWS = CONV_WIDTH * DN_QKV // D_MODEL
SMALL_GRAD_ROWS = 24


def _pack_small(vals, conv=None):
    tail = jnp.concatenate([vals["attn_q_gain"].reshape(-1), vals["attn_k_gain"].reshape(-1),
                            vals["dn_a_log"].reshape(-1), vals["dn_dt_bias"].reshape(-1),
                            vals["dn_o_gain"].reshape(-1)])
    tail = jnp.pad(tail, (0, D_MODEL - tail.shape[0])).reshape(1, D_MODEL)
    rows = [vals["mix_norm"], vals["mlp_norm"], vals["ple_norm"], tail, jnp.zeros((1, D_MODEL), F32)]
    if conv is not None:
        rows += [conv.reshape(CONV_ROWS, D_MODEL),
                 jnp.zeros((SMALL_GRAD_ROWS - SMALL_ROWS - CONV_ROWS, D_MODEL), F32)]
    return jnp.concatenate(rows, axis=0)


def _unpack_small(block):
    nq = 3 * A_HEAD_DIM
    t = block[6]
    return dict(
        mix_norm=block[0:2], mlp_norm=block[2:4], ple_norm=block[4:6],
        attn_q_gain=t[:nq].reshape(1, 3, A_HEAD_DIM), attn_k_gain=t[nq:2 * nq].reshape(1, 3, A_HEAD_DIM),
        dn_a_log=t[2 * nq:2 * nq + DN_HEADS].reshape(1, DN_HEADS),
        dn_dt_bias=t[2 * nq + DN_HEADS:2 * nq + 2 * DN_HEADS].reshape(1, DN_HEADS),
        dn_o_gain=t[2 * nq + 2 * DN_HEADS:2 * nq + 2 * DN_HEADS + DN_HEAD_DIM].reshape(1, DN_HEAD_DIM))


def kernel(x, p, positions, mix_norm, attn_w_qkv, attn_q_gain, attn_k_gain, attn_w_o, dn_w_in, dn_conv, dn_a_log, dn_dt_bias, dn_o_gain, dn_w_o, mlp_norm, w_up, w_down, ple_norm, w_ple, w_ple_gate, loss_target, m_mix_norm, m_attn_w_qkv, m_attn_q_gain, m_attn_k_gain, m_attn_w_o, m_dn_w_in, m_dn_conv, m_dn_a_log, m_dn_dt_bias, m_dn_o_gain, m_dn_w_o, m_mlp_norm, m_w_up, m_w_down, m_ple_norm, m_w_ple, m_w_ple_gate, v_mix_norm, v_attn_w_qkv, v_attn_q_gain, v_attn_k_gain, v_attn_w_o, v_dn_w_in, v_dn_conv, v_dn_a_log, v_dn_dt_bias, v_dn_o_gain, v_dn_w_o, v_mlp_norm, v_w_up, v_w_down, v_ple_norm, v_w_ple, v_w_ple_gate):
    given = dict(locals())
    w = {n: given[n] for n in WEIGHTS}
    m = {n: given["m_" + n] for n in WEIGHTS}
    v = {n: given["v_" + n] for n in WEIGHTS}
    kinds = ("grad", "delta", "new_m", "new_v")
    axes = {n: _view_axis(w[n], axis) for n, axis in SHARDED if n != "dn_conv"}
    chip = 2 * lax.axis_index("x") + lax.axis_index("y")
    core = lax.axis_index("c")
    shards = {n: _view(w[n]).astype(BF16) for n in axes}

    def whole(n, slots):
        return jnp.concatenate([jnp.where(chip == q, shards[n], slots[q]) for q in range(N_CHIPS)], axis=axes[n])

    def stacks_of(grads_of):
        return [jnp.stack(jnp.split(g, N_CHIPS, axis=axes[n])) for n, g in grads_of.items()]

    def chip_sums_of(names, stacks, theirs):
        mine = [lax.dynamic_slice_in_dim(s, core * (s.shape[1] // 2), s.shape[1] // 2, axis=1) for s in stacks]
        return {n: _elementwise(f"add_core_{n}", lambda a, b: a.astype(F32) + b.astype(F32), [a, b], [BF16], 128)
                for n, a, b in zip(names, mine, theirs)}

    class Hooks:
        first_cargo = _gather_exchange([shards[n] for n in ATTN_MATRICES])
        fwd_cargo = [_gather_exchange([shards[n] for n in group]) for group in CARGO_GROUPS]
        chip_sums = {}
        early = None

        def first_weights(self, results):
            return {n: whole(n, slots) for n, slots in zip(ATTN_MATRICES, results)}

        def weights_from(self, results):
            full = {n: whole(n, slots) for group, res in zip(CARGO_GROUPS, results) for n, slots in zip(group, res)}
            w_in, n_ab = full["dn_w_in"], 2 * DN_HEADS
            full["dn_w_in"] = jnp.concatenate([w_in[:, DN_QKV + n_ab:], w_in[:, :DN_QKV + n_ab],
                                               jnp.zeros((D_MODEL, DN_AB_PAD - n_ab), BF16)], axis=1)
            return full

        def split_cargo(self, early_grads):
            self.early = (list(early_grads), stacks_of(early_grads))
            return _other_half_exchange(self.early[1])

        def bwd_cargo(self, results):
            self.chip_sums.update(chip_sums_of(*self.early, results))
            return [_scatter_exchange([self.chip_sums[n] for n in group]) for group in CARGO_GROUPS]

        def last_cargo(self, attn_grads):
            stacks = stacks_of(attn_grads)
            theirs = _run_exchange("split_core_grads_attn", _other_half_exchange(stacks))
            self.chip_sums.update(chip_sums_of(list(attn_grads), stacks, theirs))
            return _scatter_exchange([self.chip_sums[n] for n in attn_grads])

    hooks = Hooks()
    big = {}
    conv_block = jnp.pad(w["dn_conv"].reshape(-1), (0, SMALL_ROWS * D_MODEL - w["dn_conv"].size))
    conv_all = _gather_from_all("gather_conv", conv_block.reshape(SMALL_ROWS, D_MODEL))
    conv_all = conv_all.reshape(N_CHIPS, 2, -1)[:, 0, :w["dn_conv"].size]
    conv_full = jnp.concatenate([conv_all[q].reshape(CONV_WIDTH, -1) for q in range(N_CHIPS)], axis=1)
    small = {n: w[n] for n in REPLICATED}
    small["dn_conv"] = conv_full

    sq, grad_x, grads, brought = _local_step(x[0], p[:, 0], positions[0], loss_target[0], small, big, hooks)
    loss = lax.psum(0.5 * sq[0, 0] / D_MODEL, ("x", "y", "c"))
    out = {}

    landed = {n: r for group, res in zip(CARGO_GROUPS + (ATTN_MATRICES,), brought) for n, r in zip(group, res)}
    half_sums = []
    for n in axes:
        o = lax.dynamic_index_in_dim(hooks.chip_sums[n], chip, axis=0, keepdims=False)
        per = math.prod(o.shape[:-1])
        r = landed[n]
        half_sums.append(_elementwise(
            f"add_chips_{n}", lambda a, b, c, d: ((a.astype(F32) + b.astype(F32)) + c.astype(F32)) + d.astype(F32),
            [o, (r, 0), (r, per), (r, 2 * per)], [F32], 128))
    other_halves = _swap_with_sibling("join_core_sums", half_sums)
    for n, a, b in zip(axes, half_sums, other_halves):
        g = jnp.where(core == 0, jnp.concatenate([a, b], axis=0), jnp.concatenate([b, a], axis=0))
        shp = w[n].shape
        res = _elementwise(f"adamw_{n}", lambda g, w_, m_, v_: (g,) + _adamw(w_, g, m_, v_),
                           [g.reshape(shp), w[n], m[n], v[n]], [F32] * 4, 256)
        for kind, arr in zip(kinds, res):
            out[kind + "_" + n] = arr.reshape(shp)

    slots = _gather_from_all("gather_small_grads", _pack_small(grads, grads["dn_conv"]))

    def small_body(s_ref, w_ref, m_ref, v_ref, sum_out, g_out, d_out, m_out, v_out):
        total = s_ref[0]
        for d in range(1, N_DEV):
            total = total + s_ref[d]
        sum_out[...] = total
        g = total[:SMALL_ROWS]
        for o, r in zip((g_out, d_out, m_out, v_out), (g,) + _adamw(w_ref[...], g, m_ref[...], v_ref[...])):
            o[...] = r

    res = pl.pallas_call(small_body, name="adamw_replicated",
                         out_shape=[jax.ShapeDtypeStruct((SMALL_GRAD_ROWS, D_MODEL), F32)]
                         + [jax.ShapeDtypeStruct((SMALL_ROWS, D_MODEL), F32)] * 4)(
        slots, _pack_small(w), _pack_small(m), _pack_small(v))
    for kind, block in zip(kinds, res[1:]):
        for n, arr in _unpack_small(block).items():
            out[kind + "_" + n] = arr
    conv_sum = res[0][SMALL_ROWS:SMALL_ROWS + CONV_ROWS].reshape(CONV_WIDTH, DN_QKV)
    cols = DN_QKV // N_CHIPS
    chip = 2 * lax.axis_index("x") + lax.axis_index("y")
    conv_mine = lax.dynamic_slice_in_dim(conv_sum, chip * cols, cols, axis=1)
    res = _elementwise("adamw_dn_conv", lambda g, w_, m_, v_: (g,) + _adamw(w_, g, m_, v_),
                       [conv_mine, w["dn_conv"][0], m["dn_conv"][0], v["dn_conv"][0]], [F32] * 4, CONV_WIDTH)
    for kind, arr in zip(kinds, res):
        out[kind + "_dn_conv"] = arr[None]

    return (loss, grad_x[None],
            *[out["grad_" + n] for n in WEIGHTS], *[out["delta_" + n] for n in WEIGHTS],
            *[out["new_m_" + n] for n in WEIGHTS], *[out["new_v_" + n] for n in WEIGHTS])
```

```python
import functools
import math

import jax
import jax.numpy as jnp
from jax import lax
from jax.experimental import pallas as pl
from jax.experimental.pallas import tpu as pltpu

F32 = jnp.float32
BF16 = jnp.bfloat16
HIGHEST = lax.Precision.HIGHEST

D_MODEL = 1024
EPS = 1e-6
SWA_GROUPS = ((128, 1), (512, 4), (2048, 16))
A_HEADS = 8
A_HEAD_DIM = 64
A_WIDTH = A_HEADS * A_HEAD_DIM
ROPE_DIM = A_HEAD_DIM // 4
ROPE_THETA = 500000.0
BAND = 128
DN_HEADS = 8
DN_HEAD_DIM = 128
DN_WIDTH = DN_HEADS * DN_HEAD_DIM
DN_QKV = 3 * DN_WIDTH
DN_AB_PAD = 128
DN_IN_PAD = DN_WIDTH + DN_QKV + DN_AB_PAD
DN_QKV0 = DN_WIDTH
DN_AB0 = DN_WIDTH + DN_QKV
DN_HB = 8
CONV_WIDTH = 4
CHUNK = 64
PLE_DIM = 256
D_FF = 4 * D_MODEL

ADAM_LR = 0.001
ADAM_B1 = 0.9
ADAM_B2 = 0.999
ADAM_EPS = 1e-08
ADAM_WD = 0.01
ADAM_STEP = 10

N_CHIPS = 4
N_DEV = 8
VMEM_LIMIT = 48 * 1024 * 1024
MESH = pl.DeviceIdType.MESH

SHARDED = (
    ("attn_w_qkv", 2), ("attn_w_o", 2), ("dn_w_in", 2), ("dn_conv", 2), ("dn_w_o", 1),
    ("w_up", 2), ("w_down", 1), ("w_ple", 2), ("w_ple_gate", 1))
ATTN_MATRICES = ("attn_w_qkv", "attn_w_o")
CARGO_GROUPS = (("w_up",), ("w_down",), ("dn_w_in", "dn_w_o", "w_ple", "w_ple_gate"))
REPLICATED = ("mix_norm", "attn_q_gain", "attn_k_gain", "dn_a_log", "dn_dt_bias", "dn_o_gain",
              "mlp_norm", "ple_norm")
WEIGHTS = ("mix_norm", "attn_w_qkv", "attn_q_gain", "attn_k_gain", "attn_w_o", "dn_w_in", "dn_conv",
           "dn_a_log", "dn_dt_bias", "dn_o_gain", "dn_w_o", "mlp_norm", "w_up", "w_down", "ple_norm",
           "w_ple", "w_ple_gate")


def _cparams(sem=None):
    return pltpu.CompilerParams(dimension_semantics=sem, vmem_limit_bytes=VMEM_LIMIT)


def _pick(n, cap, quantum=128):
    best = None
    for t in range(quantum, min(n, cap) + 1, quantum):
        if n % t == 0:
            best = t
    return n if best is None else best


_DIMS = {"nn": ((1,), (0,)), "nt": ((1,), (1,)), "tn": ((0,), (0,))}


def _mm(name, a, b, mode, out_dtypes=(F32,), extras=(), epilogue=None, rows=(), n_sums=0, tm_cap=1024):
    if mode == "nn":
        (M, K), (K2, N) = a.shape, b.shape
    elif mode == "nt":
        (M, K), (N, K2) = a.shape, b.shape
    else:
        (K, M), (K2, N) = a.shape, b.shape
    assert K == K2, (name, a.shape, b.shape)
    tn = _pick(N, 1536)
    if mode == "tn":
        tm, tk = _pick(M, tm_cap), _pick(K, 2048)
    elif tn == N and K > 1536:
        tm, tk = _pick(M, min(tm_cap, 512)), K
    else:
        tm, tk = _pick(M, tm_cap), _pick(K, 1536)
    nk = K // tk
    assert n_sums == 0 or tn == N, name
    if mode == "nn":
        a_spec = pl.BlockSpec((tm, tk), lambda i, j, k: (i, k))
        b_spec = pl.BlockSpec((tk, tn), lambda i, j, k: (k, j))
    elif mode == "nt":
        a_spec = pl.BlockSpec((tm, tk), lambda i, j, k: (i, k))
        b_spec = pl.BlockSpec((tn, tk), lambda i, j, k: (j, k))
    else:
        a_spec = pl.BlockSpec((tk, tm), lambda i, j, k: (k, i))
        b_spec = pl.BlockSpec((tk, tn), lambda i, j, k: (k, j))
    o_spec = pl.BlockSpec((tm, tn), lambda i, j, k: (i, j))
    r_spec = pl.BlockSpec((1, tn), lambda i, j, k: (0, j))
    n_extra, n_out = len(extras) + len(rows), len(out_dtypes)
    dims = (_DIMS[mode], ((), ()))

    def body(a_ref, b_ref, *rest):
        extra_refs, out_refs = rest[:n_extra], rest[n_extra:n_extra + n_out]
        sum_refs = rest[n_extra + n_out:n_extra + n_out + n_sums]
        i, k = pl.program_id(0), pl.program_id(2)
        part = lax.dot_general(a_ref[...].astype(BF16), b_ref[...].astype(BF16), dims, preferred_element_type=F32)

        def finish(total):
            vals = (total,) if epilogue is None else epilogue(total, *[e[...] for e in extra_refs])
            for o, v in zip(out_refs, vals[:n_out]):
                o[...] = v.astype(o.dtype)
            for s, v in zip(sum_refs, vals[n_out:]):
                @pl.when(i == 0)
                def _():
                    s[...] = v

                @pl.when(i > 0)
                def _():
                    s[...] += v

        if nk == 1:
            finish(part)
            return
        acc = rest[-1]

        @pl.when(k == 0)
        def _():
            acc[...] = part

        @pl.when(jnp.logical_and(k > 0, k < nk - 1))
        def _():
            acc[...] += part

        @pl.when(k == nk - 1)
        def _():
            finish(acc[...] + part)

    outs = pl.pallas_call(
        body, name=name, grid=(M // tm, N // tn, nk),
        in_specs=[a_spec, b_spec] + [o_spec] * len(extras) + [r_spec] * len(rows),
        out_specs=[o_spec] * n_out + [r_spec] * n_sums,
        out_shape=[jax.ShapeDtypeStruct((M, N), dt) for dt in out_dtypes]
        + [jax.ShapeDtypeStruct((1, N), F32)] * n_sums,
        scratch_shapes=[pltpu.VMEM((tm, tn), F32)] if nk > 1 else [],
        compiler_params=_cparams(("arbitrary" if n_sums else "parallel", "parallel", "arbitrary")),
    )(a, b, *extras, *rows)
    return outs[0] if n_out + n_sums == 1 else outs


def _rowwise(name, fn, rows, bcast, row_outs, acc_outs=(), tm=256, n_rows=None, cargo=None):
    rows = [r if isinstance(r, tuple) else (r, r.shape[1], 0) for r in rows]
    rows = [r if len(r) == 4 else r + (0,) for r in rows]
    S = rows[0][0].shape[0] if n_rows is None else n_rows
    tm = min(tm, S)
    assert S % tm == 0 and all(r[3] % tm == 0 for r in rows), (name, S, tm)
    n_row, n_bc, n_ro, n_acc = len(rows), len(bcast), len(row_outs), len(acc_outs)
    in_specs = [pl.BlockSpec((tm, w), functools.partial(lambda i, cb, rb: (i + rb, cb), cb=cb, rb=r0 // tm))
                for _, w, cb, r0 in rows]
    in_specs += [pl.BlockSpec(b.shape, lambda i: (0, 0)) for b in bcast]
    out_specs = [pl.BlockSpec((tm, c), lambda i: (i, 0)) for c, _ in row_outs]
    out_specs += [pl.BlockSpec(s, lambda i: (0, 0)) for s in acc_outs]
    out_shape = [jax.ShapeDtypeStruct((S, c), dt) for c, dt in row_outs]
    out_shape += [jax.ShapeDtypeStruct(s, F32) for s in acc_outs]

    def body(*refs):
        ins = [r[...] for r in refs[:n_row + n_bc]]
        outs = refs[n_row + n_bc:]
        vals = fn(*ins)
        if not isinstance(vals, (tuple, list)):
            vals = (vals,)
        for o, v in zip(outs[:n_ro], vals[:n_ro]):
            o[...] = v.astype(o.dtype)
        if n_acc:
            @pl.when(pl.program_id(0) == 0)
            def _():
                for o in outs[n_ro:]:
                    o[...] = jnp.zeros_like(o)
            for o, v in zip(outs[n_ro:], vals[n_ro:]):
                o[...] += v

    n_own = n_ro + n_acc
    body, c_in_specs, c_out_specs, c_out_shape, c_scratch, c_ins = _carry(cargo, n_row + n_bc, n_own, 0, body, S // tm)
    outs = pl.pallas_call(
        body, name=name, grid=(S // tm,), in_specs=in_specs + c_in_specs, out_specs=out_specs + c_out_specs,
        out_shape=out_shape + c_out_shape, scratch_shapes=c_scratch,
        compiler_params=_cparams(("arbitrary",) if n_acc or cargo is not None else ("parallel",)),
    )(*[r[0] for r in rows], *bcast, *c_ins)
    own = outs[0] if n_own == 1 else outs[:n_own]
    return own if cargo is None else (own, outs[n_own:])


def _sigmoid(x):
    return 1.0 / (1.0 + jnp.exp(-x))


def _silu(x):
    return x * _sigmoid(x)


def _softplus(x):
    return jnp.maximum(x, 0.0) + jnp.log(1.0 + jnp.exp(-jnp.abs(x)))


def _rms_fwd_fn(x, g):
    r = lax.rsqrt(jnp.mean(x * x, axis=-1, keepdims=True) + EPS)
    return (x * r) * g


def _rms_bwd_fn(x, dres, *rest):
    dh, g = sum(rest[:-1]), rest[-1]
    r = lax.rsqrt(jnp.mean(x * x, axis=-1, keepdims=True) + EPS)
    xh = x * r
    dxh = dh * g
    dx = dres + r * (dxh - xh * jnp.mean(dxh * xh, axis=-1, keepdims=True))
    return dx, dx, jnp.sum(dh * xh, axis=0, keepdims=True)


def _rms_fwd(name, x, gain, cargo=None):
    return _rowwise(name, _rms_fwd_fn, [x], [gain.reshape(1, -1)], [(x.shape[1], BF16)], cargo=cargo)


def _rms_bwd(name, x, gain, dres, dhs, cargo=None):
    return _rowwise(name, _rms_bwd_fn, [x, dres] + list(dhs), [gain.reshape(1, -1)],
                    [(x.shape[1], F32), (x.shape[1], BF16)], [(1, x.shape[1])], cargo=cargo)


def _relu2_epilogue(acc):
    r = jnp.maximum(acc, 0.0)
    return (r * r,)


def _relu2_bwd_epilogue(acc, a):
    return (acc * (2.0 * jnp.sqrt(a.astype(F32))),)


def _ple_fwd_fn(x, pp, zg):
    return x + pp * _sigmoid(zg)


def _ple_norm_fwd_fn(x, pp, zg, gain):
    out = _ple_fwd_fn(x, pp, zg)
    return out, _rms_fwd_fn(out, gain)


def _ple_bwd_fn(dx, pp, zg):
    gate = _sigmoid(zg)
    return dx * gate, dx * pp * gate * (1.0 - gate)


def _adamw(w, g, m, v):
    m = ADAM_B1 * m + (1.0 - ADAM_B1) * g
    v = ADAM_B2 * v + (1.0 - ADAM_B2) * jnp.square(g)
    m_hat = m / (1.0 - ADAM_B1 ** ADAM_STEP)
    v_hat = v / (1.0 - ADAM_B2 ** ADAM_STEP)
    delta = -ADAM_LR * (m_hat / (jnp.sqrt(v_hat) + ADAM_EPS) + ADAM_WD * w)
    return delta, m, v


def _lane_take(x, offset):
    n = x.shape[-1]
    return pltpu.roll(x, (-offset) % n, 1)


def _head_lane(shape):
    return lax.broadcasted_iota(jnp.int32, shape, 1) % A_HEAD_DIM


def _rope_partner(x):
    lane = _head_lane(x.shape)
    return jnp.where(lane < ROPE_DIM // 2, _lane_take(x, ROPE_DIM // 2),
                     jnp.where(lane < ROPE_DIM, _lane_take(x, -(ROPE_DIM // 2)), 0.0))


def _head_mean(x, bd):
    hi = x.astype(BF16)
    lo = (x - hi.astype(F32)).astype(BF16)
    b = bd.astype(BF16)
    return jnp.dot(hi, b, preferred_element_type=F32) + jnp.dot(lo, b, preferred_element_type=F32)


def _fold_heads(row):
    out = row[:, :A_HEAD_DIM]
    for h in range(1, A_HEADS):
        out = out + row[:, h * A_HEAD_DIM:(h + 1) * A_HEAD_DIM]
    return out


def _all_heads(t):
    return jnp.concatenate([t] * (A_WIDTH // t.shape[1]), axis=1)


def _qk_prep_fwd_fn(qkv, ct, st, gq, gk, bd):
    ct, st = _all_heads(ct), _all_heads(st)

    def one(t, g):
        n = t * lax.rsqrt(_head_mean(t * t, bd) + EPS) * g
        return n * ct + _rope_partner(n) * st
    q, k, v = qkv[:, :A_WIDTH], qkv[:, A_WIDTH:2 * A_WIDTH], qkv[:, 2 * A_WIDTH:]
    return one(q, gq), one(k, gk), v


def _qk_prep_bwd_fn(qkv, ct, st, dq, dk, dv, gq, gk, bd):
    ct, st = _all_heads(ct), _all_heads(st)

    def one(t, g, dy):
        r = lax.rsqrt(_head_mean(t * t, bd) + EPS)
        nh = t * r
        dn = dy * ct + _rope_partner(dy * st)
        dg = jnp.sum(dn * nh, axis=0, keepdims=True)
        dnh = dn * g
        return r * (dnh - nh * _head_mean(dnh * nh, bd)), _fold_heads(dg)
    q, k = qkv[:, :A_WIDTH], qkv[:, A_WIDTH:2 * A_WIDTH]
    dq_raw, dgq = one(q, gq, dq)
    dk_raw, dgk = one(k, gk, dk)
    return jnp.concatenate([dq_raw, dk_raw, dv], axis=1), dgq, dgk


_BATCH_DIMS = {"nn": ((2,), (1,)), "nt": ((2,), (2,)), "tn": ((1,), (1,))}


def _bdot(a, b, mode, precision=None):
    return lax.dot_general(a, b, (_BATCH_DIMS[mode], ((0,), (0,))), precision=precision,
                           preferred_element_type=F32)


def _attn_cols(h):
    return slice(h * A_HEAD_DIM, (h + 1) * A_HEAD_DIM)


def _attn_heads(ref):
    return jnp.stack([ref[:, _attn_cols(h)] for h in range(A_HEADS)])


def _band_masks():
    qi = lax.broadcasted_iota(jnp.int32, (BAND, BAND), 0)
    kj = lax.broadcasted_iota(jnp.int32, (BAND, BAND), 1)
    return kj <= qi, kj >= qi


def _attn_fwd(name, q, k, v, blocks_per_class, cargo=None):
    S = q.shape[0]
    nblk = S // BAND
    scale = A_HEAD_DIM ** -0.5

    def body(q_ref, kp_ref, kc_ref, vp_ref, vc_ref, o_ref, l_ref):
        i = pl.program_id(0)
        has_prev = (i % blocks_per_class) != 0
        m_cur, m_prev = _band_masks()
        m_prev = jnp.logical_and(m_prev, has_prev)
        q, kc, kp, vc, vp = (_attn_heads(r) for r in (q_ref, kc_ref, kp_ref, vc_ref, vp_ref))
        s_c = jnp.where(m_cur[None], _bdot(q, kc, "nt") * scale, -jnp.inf)
        s_p = jnp.where(m_prev[None], _bdot(q, kp, "nt") * scale, -jnp.inf)
        m = jnp.maximum(jnp.max(s_c, axis=-1, keepdims=True), jnp.max(s_p, axis=-1, keepdims=True))
        e_c, e_p = jnp.exp(s_c - m), jnp.exp(s_p - m)
        l = jnp.sum(e_c, axis=-1, keepdims=True) + jnp.sum(e_p, axis=-1, keepdims=True)
        o = _bdot((e_c / l).astype(BF16), vc, "nn") + _bdot((e_p / l).astype(BF16), vp, "nn")
        lse = m + jnp.log(l)
        for h in range(A_HEADS):
            o_ref[:, _attn_cols(h)] = o[h]
            l_ref[:, _attn_cols(h)] = jnp.broadcast_to(lse[h], (BAND, A_HEAD_DIM))

    cur = pl.BlockSpec((BAND, A_WIDTH), lambda i: (i, 0))
    prev = pl.BlockSpec((BAND, A_WIDTH), lambda i: (jnp.maximum(i - 1, 0), 0))
    body, c_in_specs, c_out_specs, c_out_shape, c_scratch, c_ins = _carry(cargo, 5, 2, 0, body, nblk)
    outs = pl.pallas_call(
        body, name=name, grid=(nblk,), in_specs=[cur, prev, cur, prev, cur] + c_in_specs,
        out_specs=[cur, cur] + c_out_specs,
        out_shape=[jax.ShapeDtypeStruct((S, A_WIDTH), F32)] * 2 + c_out_shape, scratch_shapes=c_scratch,
        compiler_params=_cparams(("arbitrary",)),
    )(q, k, k, v, v, *c_ins)
    return outs[0], outs[1], outs[2:]


def _carry(cargo, n_in, n_out, n_scratch, body, steps):
    if cargo is None:
        return body, [], [], [], [], []
    n_ci, n_co = len(cargo.ins), len(cargo.out_shape)

    def carrying(*refs):
        refs = list(refs)
        ins, refs = refs[:n_in], refs[n_in:]
        c_ins, refs = refs[:n_ci], refs[n_ci:]
        outs, refs = refs[:n_out], refs[n_out:]
        c_outs, refs = refs[:n_co], refs[n_co:]
        scratch, sems = refs[:n_scratch], refs[n_scratch:]

        @pl.when(pl.program_id(0) == 0)
        def _():
            cargo.start(c_ins, c_outs, sems)

        body(*ins, *outs, *scratch)

        @pl.when(pl.program_id(0) == steps - 1)
        def _():
            cargo.finish(c_ins, c_outs, sems)

    any_spec = pl.BlockSpec(memory_space=pl.ANY)
    return carrying, [any_spec] * n_ci, [any_spec] * n_co, list(cargo.out_shape), list(cargo.scratch), list(cargo.ins)


def _attn_bwd(name, q, k, v, o, lse, do, dlse, blocks_per_class, cargo=None):
    S = q.shape[0]
    nblk = S // BAND
    scale = A_HEAD_DIM ** -0.5

    def body(q_ref, kp_ref, kc_ref, vp_ref, vc_ref, o_ref, l_ref, do_ref, dl_ref,
             dq_ref, dk_ref, dv_ref, ck, cv):
        i = pl.program_id(0)

        @pl.when(i == 0)
        def _():
            ck[...] = jnp.zeros_like(ck)
            cv[...] = jnp.zeros_like(cv)

        @pl.when(i == nblk)
        def _():
            dk_ref[...] = ck[...]
            dv_ref[...] = cv[...]

        @pl.when(i < nblk)
        def _():
            has_prev = (i % blocks_per_class) != 0
            m_cur, m_prev = _band_masks()
            m_prev = jnp.logical_and(m_prev, has_prev)
            q, kc, kp, vc, vp = (_attn_heads(r) for r in (q_ref, kc_ref, kp_ref, vc_ref, vp_ref))
            do, o, dl = _attn_heads(do_ref), _attn_heads(o_ref), _attn_heads(dl_ref)
            lse = jnp.max(_attn_heads(l_ref), axis=-1, keepdims=True)
            p_c = jnp.where(m_cur[None], jnp.exp(_bdot(q, kc, "nt") * scale - lse), 0.0)
            p_p = jnp.where(m_prev[None], jnp.exp(_bdot(q, kp, "nt") * scale - lse), 0.0)
            corr = jnp.sum(dl, axis=-1, keepdims=True) - jnp.sum(do * o, axis=-1, keepdims=True)
            dob = do.astype(BF16)
            ds_c = (p_c * (_bdot(dob, vc, "nt") + corr)).astype(BF16)
            ds_p = (p_p * (_bdot(dob, vp, "nt") + corr)).astype(BF16)
            dq = (_bdot(ds_c, kc, "nn") + _bdot(ds_p, kp, "nn")) * scale
            dk_p, dk_c = _bdot(ds_p, q, "tn") * scale, _bdot(ds_c, q, "tn") * scale
            dv_p, dv_c = _bdot(p_p.astype(BF16), dob, "tn"), _bdot(p_c.astype(BF16), dob, "tn")
            for h in range(A_HEADS):
                sl = _attn_cols(h)
                dq_ref[:, sl] = dq[h]
                dk_ref[:, sl] = ck[:, sl] + dk_p[h]
                dv_ref[:, sl] = cv[:, sl] + dv_p[h]
                ck[:, sl] = dk_c[h]
                cv[:, sl] = dv_c[h]

    last = nblk - 1
    cur = pl.BlockSpec((BAND, A_WIDTH), lambda i: (jnp.minimum(i, last), 0))
    prev = pl.BlockSpec((BAND, A_WIDTH), lambda i: (jnp.minimum(jnp.maximum(i - 1, 0), last), 0))
    body, c_in_specs, c_out_specs, c_out_shape, c_scratch, c_ins = _carry(cargo, 9, 3, 2, body, nblk + 1)
    outs = pl.pallas_call(
        body, name=name, grid=(nblk + 1,),
        in_specs=[cur, prev, cur, prev, cur, cur, cur, cur, cur] + c_in_specs,
        out_specs=[cur, prev, prev] + c_out_specs,
        out_shape=[jax.ShapeDtypeStruct((S, A_WIDTH), F32)] * 3 + c_out_shape,
        scratch_shapes=[pltpu.VMEM((BAND, A_WIDTH), F32)] * 2 + c_scratch,
        compiler_params=_cparams(("arbitrary",)),
    )(q, k, k, v, v, o, lse, do, dlse, *c_ins)
    return outs[0], outs[1], outs[2], outs[3:]


def _merge_fwd_fn(o0, o1, o2, l0, l1, l2):
    m = jnp.maximum(jnp.maximum(l0, l1), l2)
    e0, e1, e2 = jnp.exp(l0 - m), jnp.exp(l1 - m), jnp.exp(l2 - m)
    return (e0 * o0 + e1 * o1 + e2 * o2) / (e0 + e1 + e2)


def _merge_bwd_fn(o0, o1, o2, l0, l1, l2, dom):
    m = jnp.maximum(jnp.maximum(l0, l1), l2)
    e0, e1, e2 = jnp.exp(l0 - m), jnp.exp(l1 - m), jnp.exp(l2 - m)
    den = e0 + e1 + e2
    w0, w1, w2 = e0 / den, e1 / den, e2 / den
    dw0, dw1, dw2 = dom * o0, dom * o1, dom * o2
    mean = w0 * dw0 + w1 * dw1 + w2 * dw2
    return w0 * dom, w1 * dom, w2 * dom, w0 * (dw0 - mean), w1 * (dw1 - mean), w2 * (dw2 - mean)


def _to_classes(t, d):
    if d == 1:
        return t
    S, C = t.shape
    return t.reshape(S // d, d, C).transpose(1, 0, 2).reshape(S, C)


def _from_classes(t, d):
    if d == 1:
        return t
    S, C = t.shape
    return t.reshape(d, S // d, C).transpose(1, 0, 2).reshape(S, C)


def _rope_lane_tables(positions):
    inv_freq = ROPE_THETA ** (-jnp.arange(0, ROPE_DIM, 2, dtype=F32) / ROPE_DIM)
    ang = positions.astype(F32)[:, None] * inv_freq
    cos, sin = jnp.cos(ang), jnp.sin(ang)
    S = positions.shape[0]
    rest = A_HEAD_DIM - ROPE_DIM
    ct = jnp.concatenate([cos, cos, jnp.ones((S, rest), F32)], axis=1)
    st = jnp.concatenate([-sin, sin, jnp.zeros((S, rest), F32)], axis=1)
    return jnp.tile(ct, (1, 2)), jnp.tile(st, (1, 2))


def _head_mean_matrix():
    r = jnp.arange(A_WIDTH) // A_HEAD_DIM
    return (r[:, None] == r[None, :]).astype(F32) * (1.0 / A_HEAD_DIM)


CONV_STRIP = 16
CONV_LANES = 512
PAST = CONV_WIDTH - 1


def _strip_starts(tm):
    return range(0, tm, CONV_STRIP)


def _conv_fwd(name, proj, w):
    S = proj.shape[0]
    tm, tc = min(512, S), CONV_LANES
    per8 = tm // 8
    off = DN_QKV0 // tc

    def body(x_ref, halo_ref, w_ref, o_ref, xs):
        i = pl.program_id(0)
        xs[0:8, :] = jnp.where(i > 0, halo_ref[...], 0.0)
        xs[8:, :] = x_ref[...]
        for r0 in _strip_starts(tm):
            acc = w_ref[PAST:CONV_WIDTH, :] * x_ref[pl.ds(r0, CONV_STRIP), :]
            for j in range(PAST):
                acc = acc + w_ref[j:j + 1, :] * xs[pl.ds(8 - PAST + j + r0, CONV_STRIP), :]
            o_ref[pl.ds(r0, CONV_STRIP), :] = acc

    return pl.pallas_call(
        body, name=name, grid=(S // tm, DN_QKV // tc),
        in_specs=[pl.BlockSpec((tm, tc), lambda i, j: (i, j + off)),
                  pl.BlockSpec((8, tc), lambda i, j: (jnp.maximum(i * per8 - 1, 0), j + off)),
                  pl.BlockSpec((CONV_WIDTH, tc), lambda i, j: (0, j))],
        out_specs=pl.BlockSpec((tm, tc), lambda i, j: (i, j)),
        out_shape=jax.ShapeDtypeStruct((S, DN_QKV), F32),
        scratch_shapes=[pltpu.VMEM((tm + 8, tc), F32)],
        compiler_params=_cparams(("parallel", "parallel")),
    )(proj, proj, w)


def _conv_bwd(name, proj, dpre, w):
    S = proj.shape[0]
    tm, tc = min(512, S), CONV_LANES
    per8 = tm // 8
    off = DN_QKV0 // tc
    last8 = S // 8 - 1
    nrow = S // tm

    def body(x_ref, xh_ref, d_ref, dh_ref, w_ref, dx_ref, dw_ref, xs, ds):
        i = pl.program_id(1)
        xs[0:8, :] = jnp.where(i > 0, xh_ref[...], 0.0)
        xs[8:, :] = x_ref[...]
        ds[0:tm, :] = d_ref[...]
        ds[tm:, :] = jnp.where(i < nrow - 1, dh_ref[...], 0.0)
        sums = [jnp.zeros((8, tc), F32)] * CONV_WIDTH
        for r0 in _strip_starts(tm):
            rows = pl.ds(r0, CONV_STRIP)
            d = d_ref[rows, :]
            acc = w_ref[PAST:CONV_WIDTH, :] * d
            for j in range(PAST):
                acc = acc + w_ref[j:j + 1, :] * ds[pl.ds(r0 + PAST - j, CONV_STRIP), :]
            dx_ref[rows, :] = acc.astype(dx_ref.dtype)
            taps = [xs[pl.ds(8 - PAST + j + r0, CONV_STRIP), :] for j in range(PAST)] + [x_ref[rows, :]]
            for j, tap in enumerate(taps):
                prod = d * tap
                sums[j] = sums[j] + sum(prod[r:r + 8] for r in range(0, CONV_STRIP, 8))

        @pl.when(i == 0)
        def _():
            dw_ref[...] = jnp.zeros_like(dw_ref)

        for j in range(CONV_WIDTH):
            dw_ref[j:j + 1, :] += jnp.sum(sums[j], axis=0, keepdims=True)

    return pl.pallas_call(
        body, name=name, grid=(DN_QKV // tc, nrow),
        in_specs=[pl.BlockSpec((tm, tc), lambda j, i: (i, j + off)),
                  pl.BlockSpec((8, tc), lambda j, i: (jnp.maximum(i * per8 - 1, 0), j + off)),
                  pl.BlockSpec((tm, tc), lambda j, i: (i, j)),
                  pl.BlockSpec((8, tc), lambda j, i: (jnp.minimum((i + 1) * per8, last8), j)),
                  pl.BlockSpec((CONV_WIDTH, tc), lambda j, i: (0, j))],
        out_specs=[pl.BlockSpec((tm, tc), lambda j, i: (i, j)),
                   pl.BlockSpec((CONV_WIDTH, tc), lambda j, i: (0, j))],
        out_shape=[jax.ShapeDtypeStruct((S, DN_QKV), BF16), jax.ShapeDtypeStruct((CONV_WIDTH, DN_QKV), F32)],
        scratch_shapes=[pltpu.VMEM((tm + 8, tc), F32)] * 2,
        compiler_params=_cparams(("parallel", "arbitrary")),
    )(proj, proj, dpre, dpre, w)


def _gate_lane(shape):
    return lax.broadcasted_iota(jnp.int32, shape, 1)


GATES_ROWS = 256


def _chunk_cumsum_matrix():
    r = jnp.arange(GATES_ROWS)
    return ((r[:, None] >= r[None, :]) & (r[:, None] // CHUNK == r[None, :] // CHUNK)).astype(F32)


def _gates_fwd_fn(ab, alog, dt, cum):
    g = -jnp.exp(alog) * _softplus(ab + dt)
    gc = jnp.dot(cum, g, precision=HIGHEST, preferred_element_type=F32)
    return jnp.where(_gate_lane(ab.shape) < DN_HEADS, gc, _sigmoid(ab))


def _gates_bwd_fn(ab, dgb, alog, dt, cum):
    lane = _gate_lane(ab.shape)
    is_g = lane < DN_HEADS
    neg_a = -jnp.exp(alog)
    sp = _softplus(ab + dt)
    dsp = _sigmoid(ab + dt)
    beta = _sigmoid(ab)
    dgc = jnp.where(is_g, dgb, 0.0)
    dg = lax.dot_general(cum, dgc, (_DIMS["tn"], ((), ())), precision=HIGHEST, preferred_element_type=F32)
    dab = jnp.where(is_g, dg * neg_a * dsp, jnp.where(lane < 2 * DN_HEADS, dgb * beta * (1.0 - beta), 0.0))
    d_alog = jnp.sum(dg * neg_a * sp, axis=0, keepdims=True)
    d_dt = jnp.sum(dg * neg_a * dsp, axis=0, keepdims=True)
    return dab, d_alog, d_dt


def _chunk_math(precision):
    def dg(a, b, mode, prec=precision):
        return _bdot(a, b, mode, prec)

    @jax.custom_vjp
    def nn(a, b):
        return dg(a, b, "nn")

    @jax.custom_vjp
    def nt(a, b):
        return dg(a, b, "nt")

    @jax.custom_vjp
    def tn(a, b):
        return dg(a, b, "tn")

    nn.defvjp(lambda a, b: (nn(a, b), (a, b)), lambda r, g: (nt(g, r[1]), tn(r[0], g)))
    nt.defvjp(lambda a, b: (nt(a, b), (a, b)), lambda r, g: (nn(g, r[1]), tn(g, r[0])))
    tn.defvjp(lambda a, b: (tn(a, b), (a, b)), lambda r, g: (nt(r[1], g), nn(r[0], g)))

    def split(x):
        hi = x.astype(BF16)
        return hi, (x - hi.astype(F32)).astype(BF16)

    def fine(a, b, mode):
        ah, al = split(a)
        bh, bl = split(b)
        return dg(ah, bh, mode, None) + (dg(ah, bl, mode, None) + dg(al, bh, mode, None))

    def unit_lower_inverse(a):
        row = lax.broadcasted_iota(jnp.int32, a.shape, 1)
        col = lax.broadcasted_iota(jnp.int32, a.shape, 2)
        x = -a
        p = jnp.where(row == col, 1.0, 0.0) + x
        for _ in range(int(math.log2(CHUNK)) - 1):
            x = fine(x, x, "nn")
            p = p + fine(p, x, "nn")
        return p

    @jax.custom_vjp
    def solve2(a, ti, r1, r2):
        return fine(ti, r1, "nn"), fine(ti, r2, "nn")

    def solve2_fwd(a, ti, r1, r2):
        s1, s2 = fine(ti, r1, "nn"), fine(ti, r2, "nn")
        return (s1, s2), (ti, s1, s2)

    def solve2_bwd(res, g):
        ti, s1, s2 = res
        d1, d2 = fine(ti, g[0], "tn"), fine(ti, g[1], "tn")
        return -(fine(d1, s1, "nt") + fine(d2, s2, "nt")), jnp.zeros_like(ti), d1, d2

    solve2.defvjp(solve2_fwd, solve2_bwd)

    def chunk_fn(pq, pk, pv, z, g_col, b_col, g_row, ogain, s_in, inverse=None):
        nb = pq.shape[0]
        sq = (nb, CHUNK, CHUNK)
        row = lax.broadcasted_iota(jnp.int32, sq, 1)
        col = lax.broadcasted_iota(jnp.int32, sq, 2)
        lower, strict = row >= col, row > col
        q, k, v = _silu(pq), _silu(pk), _silu(pv)
        q = q * lax.rsqrt(jnp.sum(q * q, axis=-1, keepdims=True) + EPS) * (DN_HEAD_DIM ** -0.5)
        k = k * lax.rsqrt(jnp.sum(k * k, axis=-1, keepdims=True) + EPS)
        gc_wide = jnp.broadcast_to(g_col, pq.shape)
        gc_i = jnp.broadcast_to(g_col, sq)
        gc_j = jnp.broadcast_to(g_row, sq)
        is_last = lax.broadcasted_iota(jnp.int32, pq.shape, 1) == CHUNK - 1
        g_last = jnp.sum(jnp.where(is_last, gc_wide, 0.0), axis=1, keepdims=True)
        decay = jnp.exp(jnp.where(lower, gc_i - gc_j, -jnp.inf))
        kb = k * b_col
        a_mat = jnp.where(strict, nt(kb, k) * decay, 0.0)
        eg = jnp.exp(gc_wide)
        ti = unit_lower_inverse(a_mat) if inverse is None else inverse
        u, w = solve2(a_mat, ti, v * b_col, kb * eg)
        attn = nt(q, k) * decay
        q_dec = q * eg
        k_dec = k * jnp.exp(g_last - gc_wide)
        c_dec = jnp.exp(g_last)
        v_new = u - nn(w, s_in)
        o = nn(q_dec, s_in) + nn(attn, v_new)
        s_out = s_in * c_dec + tn(k_dec, v_new)
        y = o * lax.rsqrt(jnp.mean(o * o, axis=-1, keepdims=True) + EPS) * ogain * _silu(z)
        return (y, s_out, ti) if inverse is None else (y, s_out)

    return chunk_fn


DN_PRECISION = None


def _chunk_specs(n_of):
    groups = DN_HEADS // DN_HB
    wide = DN_HB * DN_HEAD_DIM
    hd = pl.BlockSpec((CHUNK, wide), lambda h, n: (n_of(n), h))
    specs = dict(
        pq=hd,
        pk=pl.BlockSpec((CHUNK, wide), lambda h, n: (n_of(n), groups + h)),
        pv=pl.BlockSpec((CHUNK, wide), lambda h, n: (n_of(n), 2 * groups + h)),
        z=hd,
        gates=pl.BlockSpec((CHUNK, DN_AB_PAD), lambda h, n: (n_of(n), 0)),
        row=pl.BlockSpec((DN_HB, None, 1, CHUNK), lambda h, n: (h, n_of(n), 0, 0)),
        gain=pl.BlockSpec((1, DN_HEAD_DIM), lambda h, n: (0, 0)),
        state=pl.BlockSpec((DN_HB, None, DN_HEAD_DIM, DN_HEAD_DIM), lambda h, n: (h, n_of(n), 0, 0)),
        inverse=pl.BlockSpec((DN_HB, None, CHUNK, CHUNK), lambda h, n: (h, n_of(n), 0, 0)),
        qkv=pl.BlockSpec((CHUNK, DN_QKV), lambda h, n: (n_of(n), 0)),
        head=hd,
    )
    return specs


def _head_cols(j):
    return slice(j * DN_HEAD_DIM, (j + 1) * DN_HEAD_DIM)


def _split_heads(ref):
    return jnp.stack([ref[:, _head_cols(j)] for j in range(DN_HB)])


def _gate_columns(gates, first_lane):
    lane = lax.broadcasted_iota(jnp.int32, gates.shape, 1)
    return jnp.stack([jnp.sum(jnp.where(lane == first_lane + h, gates, 0.0), axis=-1, keepdims=True)
                      for h in range(DN_HEADS)])


def _gate_lanes(columns, first_lane):
    shape = (columns.shape[1], DN_AB_PAD)
    lane = lax.broadcasted_iota(jnp.int32, shape, 1)
    out = jnp.zeros(shape, F32)
    for h in range(DN_HEADS):
        out = out + jnp.where(lane == first_lane + h, columns[h], 0.0)
    return out


def _chunk_fwd(name, pre, proj, gates, g_row, ogain):
    assert DN_HB == DN_HEADS
    S = pre.shape[0]
    N = S // CHUNK
    chunk_fn = _chunk_math(DN_PRECISION)
    sp = _chunk_specs(lambda n: n)

    def body(pq, pk, pv, z, gb, gr, og, y_ref, sin_ref, inv_ref, st):
        @pl.when(pl.program_id(1) == 0)
        def _():
            st[...] = jnp.zeros_like(st)

        s_in = st[...]
        sin_ref[...] = s_in
        y, s_out, inverse = chunk_fn(_split_heads(pq), _split_heads(pk), _split_heads(pv), _split_heads(z),
                                     _gate_columns(gb[...], 0), _gate_columns(gb[...], DN_HEADS), gr[...],
                                     og[...], s_in)
        for j in range(DN_HB):
            y_ref[:, _head_cols(j)] = y[j].astype(y_ref.dtype)
        inv_ref[...] = inverse
        st[...] = s_out

    return pl.pallas_call(
        body, name=name, grid=(DN_HEADS // DN_HB, N),
        in_specs=[sp["pq"], sp["pk"], sp["pv"], sp["z"], sp["gates"], sp["row"], sp["gain"]],
        out_specs=[sp["head"], sp["state"], sp["inverse"]],
        out_shape=[jax.ShapeDtypeStruct((S, DN_WIDTH), BF16),
                   jax.ShapeDtypeStruct((DN_HEADS, N, DN_HEAD_DIM, DN_HEAD_DIM), F32),
                   jax.ShapeDtypeStruct((DN_HEADS, N, CHUNK, CHUNK), F32)],
        scratch_shapes=[pltpu.VMEM((DN_HB, DN_HEAD_DIM, DN_HEAD_DIM), F32)],
        compiler_params=_cparams(("parallel", "arbitrary")),
    )(pre, pre, pre, proj, gates, g_row, ogain)


def _chunk_bwd(name, pre, proj, gates, g_row, ogain, s_in_all, inverse_all, dy):
    assert DN_HB == DN_HEADS
    S = pre.shape[0]
    N = S // CHUNK
    chunk_fn = _chunk_math(DN_PRECISION)
    sp = _chunk_specs(lambda n: N - 1 - n)

    def body(pq, pk, pv, z, gb, gr, og, sin_ref, inv_ref, dy_ref,
             dpre_ref, dz_ref, dgb_ref, dgr_ref, dog_ref, ds):
        @pl.when(pl.program_id(1) == 0)
        def _():
            ds[...] = jnp.zeros_like(ds)
            dog_ref[...] = jnp.zeros_like(dog_ref)

        inverse = inv_ref[...]
        prim = (_split_heads(pq), _split_heads(pk), _split_heads(pv), _split_heads(z),
                _gate_columns(gb[...], 0), _gate_columns(gb[...], DN_HEADS), gr[...], og[...], sin_ref[...])
        _, vjp = jax.vjp(lambda *a: chunk_fn(*a, inverse=inverse), *prim)
        gq, gk, gv, gz, ggc, gbc, ggr, gog, gs = vjp((_split_heads(dy_ref), ds[...]))
        for j in range(DN_HB):
            for part, g in enumerate((gq, gk, gv)):
                dpre_ref[:, pl.ds(part * DN_WIDTH + j * DN_HEAD_DIM, DN_HEAD_DIM)] = g[j]
            dz_ref[:, _head_cols(j)] = gz[j]
        dgb_ref[...] = _gate_lanes(ggc, 0) + _gate_lanes(gbc, DN_HEADS)
        dgr_ref[...] = ggr
        dog_ref[...] += gog
        ds[...] = gs

    hd = sp["head"]
    return pl.pallas_call(
        body, name=name, grid=(1, N),
        in_specs=[sp["pq"], sp["pk"], sp["pv"], sp["z"], sp["gates"], sp["row"], sp["gain"],
                  sp["state"], sp["inverse"], hd],
        out_specs=[sp["qkv"], hd, sp["gates"], sp["row"], sp["gain"]],
        out_shape=[jax.ShapeDtypeStruct((S, DN_QKV), F32), jax.ShapeDtypeStruct((S, DN_WIDTH), F32),
                   jax.ShapeDtypeStruct((S, DN_AB_PAD), F32),
                   jax.ShapeDtypeStruct((DN_HEADS, N, 1, CHUNK), F32), jax.ShapeDtypeStruct((1, DN_HEAD_DIM), F32)],
        scratch_shapes=[pltpu.VMEM((DN_HB, DN_HEAD_DIM, DN_HEAD_DIM), F32)],
        compiler_params=_cparams(("arbitrary", "arbitrary")),
    )(pre, pre, pre, proj, gates, g_row, ogain, s_in_all, inverse_all, dy)


def _mm_rms_bwd(name, d_out, w, x, gain, dres):
    return _mm(name, d_out, w, "nt", out_dtypes=(F32, BF16), extras=(x, dres), rows=(gain.reshape(1, -1),),
               epilogue=lambda acc, x_, dres_, g: _rms_bwd_fn(x_, dres_, acc, g), n_sums=1, tm_cap=512)


def _residual_norm_epilogue(acc, res, gain):
    x = acc + res
    return x, _rms_fwd_fn(x, gain)


def _ple_loss_fn(x_mid, pp, zg, t):
    gate = _sigmoid(zg)
    err = x_mid + pp * gate - t
    dy = err * (1.0 / D_MODEL)
    return dy, dy * gate, dy * pp * gate * (1.0 - gate), jnp.broadcast_to(jnp.sum(err * err, keepdims=True), (1, 128))


def _mlp_ple_fwd(tag, x_in, h, p_l, w_up, w_down, norm_ple, w_ple, w_gate, next_gain=None, target=None):
    a = _mm(f"{tag}_up", h, w_up, "nn", out_dtypes=(BF16,), epilogue=_relu2_epilogue)
    x_mid, hg = _mm(f"{tag}_down", a, w_down, "nn", out_dtypes=(F32, BF16), extras=(x_in,),
                    rows=(norm_ple.reshape(1, -1),), epilogue=_residual_norm_epilogue)
    zg = _mm(f"{tag}_gate", hg, w_gate, "nn")
    pp = _mm(f"{tag}_ple", p_l, w_ple, "nn")
    sv = dict(x_in=x_in, h=h, a=a, x_mid=x_mid, hg=hg, zg=zg, pp=pp)
    if target is not None:
        sv["dy"], sv["dpp"], sv["dzg"], sv["sq"] = _rowwise(
            f"{tag}_ple_loss", _ple_loss_fn, [x_mid, pp, zg, target], [],
            [(D_MODEL, F32), (D_MODEL, BF16), (D_MODEL, BF16)], [(1, 128)])
        return None, None, sv
    if next_gain is None:
        return _rowwise(f"{tag}_ple_out", _ple_fwd_fn, [x_mid, pp, zg], [], [(D_MODEL, F32)]), None, sv
    x_out, h_next = _rowwise(f"{tag}_ple_out", _ple_norm_fwd_fn, [x_mid, pp, zg], [next_gain.reshape(1, -1)],
                             [(D_MODEL, F32), (D_MODEL, BF16)])
    return x_out, h_next, sv


def _mlp_ple_bwd(tag, dx, sv, p_l, norm_mlp, w_up, w_down, norm_ple, w_ple, w_gate):
    if "dpp" in sv:
        dpp, dzg = sv["dpp"], sv["dzg"]
    else:
        dpp, dzg = _rowwise(f"{tag}_ple_bwd", _ple_bwd_fn, [dx, sv["pp"], sv["zg"]], [],
                            [(D_MODEL, BF16), (D_MODEL, BF16)])
    d_w_ple = _mm(f"{tag}_d_w_ple", p_l, dpp, "tn", out_dtypes=(BF16,))
    d_w_gate = _mm(f"{tag}_d_w_gate", sv["hg"], dzg, "tn", out_dtypes=(BF16,))
    dx_mid, dx_mid_b, d_norm_ple = _mm_rms_bwd(f"{tag}_d_hg", dzg, w_gate, sv["x_mid"], norm_ple, dx)
    du = _mm(f"{tag}_d_u", dx_mid_b, w_down, "nt", out_dtypes=(BF16,), extras=(sv["a"],),
             epilogue=_relu2_bwd_epilogue)
    d_w_down = _mm(f"{tag}_d_w_down", sv["a"], dx_mid_b, "tn", out_dtypes=(BF16,))
    d_w_up = _mm(f"{tag}_d_w_up", sv["h"], du, "tn", out_dtypes=(BF16,))
    dx_in, dx_in_b, d_norm_mlp = _mm_rms_bwd(f"{tag}_d_h", du, w_up, sv["x_in"], norm_mlp, dx_mid)
    return dx_in, dx_in_b, dict(mlp_norm=d_norm_mlp, w_up=d_w_up, w_down=d_w_down, ple_norm=d_norm_ple,
                                w_ple=d_w_ple, w_ple_gate=d_w_gate)


class _NoHooks:
    first_cargo = None
    fwd_cargo = (None,) * len(SWA_GROUPS)

    def first_weights(self, results):
        return {}

    def weights_from(self, results):
        return {}

    def split_cargo(self, early_grads):
        return None

    def bwd_cargo(self, results):
        return (None,) * len(SWA_GROUPS)

    def last_cargo(self, attn_grads):
        return None


def _with_cargo(result, cargo):
    return (result, ()) if cargo is None else result


def _local_step(x, p, positions, target, small, big, hooks=_NoHooks()):
    S = x.shape[0]
    ct, st = _rope_lane_tables(positions)
    bd = _head_mean_matrix()

    h0, first = _with_cargo(_rms_fwd("l0_mix_norm", x, small["mix_norm"][0], cargo=hooks.first_cargo),
                            hooks.first_cargo)
    big = {**big, **hooks.first_weights(first)}
    attn, brought = [], []
    for g, (window, d) in enumerate(SWA_GROUPS):
        assert window // d == BAND and (S // d) % BAND == 0
        h0g = _to_classes(h0, d)
        ctg, stg = _to_classes(ct, d), _to_classes(st, d)
        w_g = big["attn_w_qkv"][:, g * 3 * A_WIDTH:(g + 1) * 3 * A_WIDTH]
        gq = jnp.tile(small["attn_q_gain"][0, g], A_HEADS).reshape(1, A_WIDTH)
        gk = jnp.tile(small["attn_k_gain"][0, g], A_HEADS).reshape(1, A_WIDTH)
        qkv = _mm(f"l0_qkv{g}", h0g, w_g, "nn")
        q, k, v = _rowwise(f"l0_qk_prep{g}", _qk_prep_fwd_fn, [qkv, ctg, stg], [gq, gk, bd], [(A_WIDTH, BF16)] * 3)
        o, lse, cargo_out = _attn_fwd(f"l0_attn{g}", q, k, v, (S // d) // BAND, cargo=hooks.fwd_cargo[g])
        brought.append(cargo_out)
        attn.append(dict(d=d, h0g=h0g, ct=ctg, st=stg, w=w_g, gq=gq, gk=gk, qkv=qkv, q=q, k=k, v=v, o=o, lse=lse,
                         o_tok=_from_classes(o, d), lse_tok=_from_classes(lse, d)))
    big = {**big, **hooks.weights_from(brought)}
    om = _rowwise("l0_merge", _merge_fwd_fn, [a["o_tok"] for a in attn] + [a["lse_tok"] for a in attn], [],
                  [(A_WIDTH, BF16)])
    x1, h1 = _mm("l0_attn_out", om, big["attn_w_o"], "nn", out_dtypes=(F32, BF16), extras=(x,),
                 rows=(small["mlp_norm"][0].reshape(1, -1),), epilogue=_residual_norm_epilogue)
    x3, h3, sv0 = _mlp_ple_fwd("l0", x1, h1, p[0], big["w_up"][0], big["w_down"][0], small["ple_norm"][0],
                               big["w_ple"][0], big["w_ple_gate"][0], next_gain=small["mix_norm"][1])

    N = S // CHUNK
    proj = _mm("l1_in", h3, big["dn_w_in"], "nn")
    pre = _conv_fwd("l1_conv", proj, small["dn_conv"])
    ab = proj[:, DN_AB0:DN_AB0 + DN_AB_PAD]
    lane_pad = DN_AB_PAD - DN_HEADS
    alog_row = jnp.pad(small["dn_a_log"][0], (0, lane_pad)).reshape(1, DN_AB_PAD)
    dt_row = jnp.pad(small["dn_dt_bias"][0], (0, lane_pad)).reshape(1, DN_AB_PAD)
    cum = _chunk_cumsum_matrix()
    gb = _rowwise("l1_gates", _gates_fwd_fn, [ab], [alog_row, dt_row, cum], [(DN_AB_PAD, F32)], tm=GATES_ROWS)
    g_row = gb[:, :DN_HEADS].T.reshape(DN_HEADS, N, 1, CHUNK)
    ogain = small["dn_o_gain"][0].reshape(1, DN_HEAD_DIM)
    y, s_in_all, inverse_all = _chunk_fwd("l1_delta", pre, proj, gb, g_row, ogain)
    x4, h4 = _mm("l1_dn_out", y, big["dn_w_o"], "nn", out_dtypes=(F32, BF16), extras=(x3,),
                 rows=(small["mlp_norm"][1].reshape(1, -1),), epilogue=_residual_norm_epilogue)
    _, _, sv1 = _mlp_ple_fwd("l1", x4, h4, p[1], big["w_up"][1], big["w_down"][1], small["ple_norm"][1],
                             big["w_ple"][1], big["w_ple_gate"][1], target=target)
    dy, sq = sv1["dy"], sv1["sq"]

    dx4, dx4_b, gl1 = _mlp_ple_bwd("l1", dy, sv1, p[1], small["mlp_norm"][1], big["w_up"][1], big["w_down"][1],
                            small["ple_norm"][1], big["w_ple"][1], big["w_ple_gate"][1])
    d_y = _mm("l1_d_y", dx4_b, big["dn_w_o"], "nt")
    d_dn_w_o = _mm("l1_d_w_o", y, dx4_b, "tn", out_dtypes=(BF16,))
    dpre, dz, dgb_cols, dg_row, d_ogain = _chunk_bwd(
        "l1_delta_bwd", pre, proj, gb, g_row, ogain, s_in_all, inverse_all, d_y)
    dconv_in, d_conv_w = _conv_bwd("l1_conv_bwd", proj, dpre, small["dn_conv"])
    dgb = dgb_cols + jnp.pad(dg_row.reshape(DN_HEADS, S).T, ((0, 0), (0, DN_AB_PAD - DN_HEADS)))
    dab, d_alog, d_dt = _rowwise("l1_gates_bwd", _gates_bwd_fn, [ab, dgb], [alog_row, dt_row, cum],
                                 [(DN_AB_PAD, F32)], [(1, DN_AB_PAD), (1, DN_AB_PAD)], tm=GATES_ROWS)
    dproj = jnp.concatenate([dz.astype(BF16), dconv_in, dab.astype(BF16)], axis=1)
    d_dn_w_in = _mm("l1_d_w_in", h3, dproj, "tn", out_dtypes=(BF16,))
    dx3, _, d_mix1 = _mm_rms_bwd("l1_d_h", dproj, big["dn_w_in"], x3, small["mix_norm"][1], dx4)

    dx1, dx1_b, gl0 = _mlp_ple_bwd("l0", dx3, sv0, p[0], small["mlp_norm"][0], big["w_up"][0], big["w_down"][0],
                            small["ple_norm"][0], big["w_ple"][0], big["w_ple_gate"][0])
    early = dict(
        dn_w_in=jnp.concatenate([d_dn_w_in[:, DN_QKV0:DN_AB0 + 2 * DN_HEADS], d_dn_w_in[:, :DN_WIDTH]], axis=1),
        dn_w_o=d_dn_w_o,
        w_up=jnp.stack([gl0["w_up"], gl1["w_up"]]),
        w_down=jnp.stack([gl0["w_down"], gl1["w_down"]]),
        w_ple=jnp.stack([gl0["w_ple"], gl1["w_ple"]]),
        w_ple_gate=jnp.stack([gl0["w_ple_gate"], gl1["w_ple_gate"]]))
    split_cargo = hooks.split_cargo(early)
    dom = _mm("l0_d_om", dx1_b, big["attn_w_o"], "nt")
    d_attn_w_o = _mm("l0_d_w_o", om, dx1_b, "tn", out_dtypes=(BF16,))
    merged, split = _with_cargo(
        _rowwise("l0_merge_bwd", _merge_bwd_fn, [a["o_tok"] for a in attn] + [a["lse_tok"] for a in attn] + [dom], [],
                 [(A_WIDTH, F32)] * 6, cargo=split_cargo), split_cargo)
    bwd_cargo = hooks.bwd_cargo(split)
    dh0, d_w_qkv, d_gq, d_gk, brought_bwd = [], [], [], [], []
    for g, a in enumerate(attn):
        do_g, dl_g = _to_classes(merged[g], a["d"]), _to_classes(merged[3 + g], a["d"])
        dqn, dkn, dvn, cargo_out = _attn_bwd(f"l0_attn_bwd{g}", a["q"], a["k"], a["v"], a["o"], a["lse"], do_g, dl_g,
                                             (S // a["d"]) // BAND, cargo=bwd_cargo[g])
        brought_bwd.append(cargo_out)
        dqkv, dgq, dgk = _rowwise(f"l0_qk_prep_bwd{g}", _qk_prep_bwd_fn, [a["qkv"], a["ct"], a["st"], dqn, dkn, dvn],
                                  [a["gq"], a["gk"], bd], [(3 * A_WIDTH, BF16)], [(1, A_HEAD_DIM)] * 2)
        d_w_qkv.append(_mm(f"l0_d_w_qkv{g}", a["h0g"], dqkv, "tn", out_dtypes=(BF16,)))
        dh0.append(_from_classes(_mm(f"l0_d_h{g}", dqkv, a["w"], "nt"), a["d"]))
        d_gq.append(dgq)
        d_gk.append(dgk)
    attn_grads = dict(attn_w_qkv=jnp.concatenate(d_w_qkv, axis=1), attn_w_o=d_attn_w_o)
    last_cargo = hooks.last_cargo(attn_grads)
    (grad_x, _, d_mix0), last = _with_cargo(
        _rms_bwd("l0_mix_norm_bwd", x, small["mix_norm"][0], dx1, dh0, cargo=last_cargo), last_cargo)
    brought_bwd.append(last)

    grads = dict(
        mix_norm=jnp.concatenate([d_mix0, d_mix1], axis=0),
        attn_q_gain=jnp.concatenate(d_gq, axis=0)[None],
        attn_k_gain=jnp.concatenate(d_gk, axis=0)[None],
        **attn_grads,
        dn_conv=d_conv_w,
        dn_a_log=d_alog[:, :DN_HEADS],
        dn_dt_bias=d_dt[:, :DN_HEADS],
        dn_o_gain=d_ogain,
        mlp_norm=jnp.concatenate([gl0["mlp_norm"], gl1["mlp_norm"]], axis=0),
        ple_norm=jnp.concatenate([gl0["ple_norm"], gl1["ple_norm"]], axis=0),
        **early,
    )
    return sq, grad_x, grads, brought_bwd


def _chip_peer(x, y, c, t):
    return (jnp.bitwise_xor(x, t >> 1), jnp.bitwise_xor(y, t & 1), c)


def _place():
    x, y, c = lax.axis_index("x"), lax.axis_index("y"), lax.axis_index("c")
    return x, y, c, 2 * x + y, (x, y, 1 - c)


def _remote(src, dst, send_sem, recv_sem, to):
    return pltpu.make_async_remote_copy(src_ref=src, dst_ref=dst, send_sem=send_sem, recv_sem=recv_sem,
                                        device_id=to, device_id_type=MESH)


def _hbm_call(name, body, ins, out_shape, scratch_shapes):
    any_spec = pl.BlockSpec(memory_space=pl.ANY)
    return pl.pallas_call(body, name=name, out_shape=out_shape, in_specs=[any_spec] * len(ins),
                          out_specs=[any_spec] * len(out_shape), scratch_shapes=scratch_shapes)(*ins)


def _half(n0, which):
    return pl.ds(which * (n0 // 2), n0 // 2)


class _Exchange:
    def __init__(self, ins, out_shape, scratch, start, finish):
        self.ins, self.out_shape, self.scratch, self.start, self.finish = ins, out_shape, scratch, start, finish


def _run_exchange(name, ex):
    n_in, n_out = len(ex.ins), len(ex.out_shape)

    def body(*refs):
        ins, outs, sems = refs[:n_in], refs[n_in:n_in + n_out], refs[n_in + n_out:]
        ex.start(ins, outs, sems)
        ex.finish(ins, outs, sems)

    return _hbm_call(name, body, ex.ins, ex.out_shape, ex.scratch)


def _gather_exchange(shards):
    T = len(shards)

    pairs = [(i, t) for i in range(T) for t in range(1, N_CHIPS)]

    def copies(ins, outs, sems):
        send, recv = sems
        x, y, c, q, sibling = _place()

        def half(i, which):
            return _half(ins[i].shape[0], which)

        def over_ici(i, t):
            return _remote(ins[i].at[half(i, c)], outs[i].at[q, half(i, c)], send.at[i, t - 1], recv.at[i, t - 1],
                           _chip_peer(x, y, c, t))

        def landing(i, t):
            spot = outs[i].at[jnp.bitwise_xor(q, t), half(i, c)]
            return _remote(spot, spot, send.at[i, t - 1], recv.at[i, t - 1], _chip_peer(x, y, c, t))

        def forward(i, t):
            spot = outs[i].at[jnp.bitwise_xor(q, t), half(i, c)]
            return _remote(spot, spot, send.at[i, 2 + t], recv.at[i, 2 + t], sibling)

        def forwarded(i, t):
            spot = outs[i].at[jnp.bitwise_xor(q, t), half(i, 1 - c)]
            return _remote(spot, spot, send.at[i, 2 + t], recv.at[i, 2 + t], sibling)

        return over_ici, landing, forward, forwarded

    def start(ins, outs, sems):
        over_ici = copies(ins, outs, sems)[0]
        for i, t in pairs:
            over_ici(i, t).start()

    def finish(ins, outs, sems):
        over_ici, landing, forward, forwarded = copies(ins, outs, sems)
        for i, t in pairs:
            landing(i, t).wait_recv()
            forward(i, t).start()
        for i, t in pairs:
            forwarded(i, t).wait_recv()
        for i, t in pairs:
            over_ici(i, t).wait_send()
            forward(i, t).wait_send()

    n_rel = 2 * (N_CHIPS - 1)
    return _Exchange(list(shards), [jax.ShapeDtypeStruct((N_CHIPS,) + s.shape, s.dtype) for s in shards],
                     [pltpu.SemaphoreType.DMA((T, n_rel)), pltpu.SemaphoreType.DMA((T, n_rel))], start, finish)


def _scatter_exchange(stacks):
    T = len(stacks)

    def copies(ins, outs, sems):
        send, recv = sems
        x, y, c, q, sibling = _place()
        return [_remote(ins[i].at[jnp.bitwise_xor(q, t)], outs[i].at[t - 1], send.at[i, t - 1], recv.at[i, t - 1],
                        _chip_peer(x, y, c, t)) for i in range(T) for t in range(1, N_CHIPS)]

    def start(ins, outs, sems):
        for cp in copies(ins, outs, sems):
            cp.start()

    def finish(ins, outs, sems):
        for cp in copies(ins, outs, sems):
            cp.wait()

    return _Exchange(list(stacks), [jax.ShapeDtypeStruct((N_CHIPS - 1,) + s.shape[1:], s.dtype) for s in stacks],
                     [pltpu.SemaphoreType.DMA((T, N_CHIPS - 1)), pltpu.SemaphoreType.DMA((T, N_CHIPS - 1))],
                     start, finish)


def _other_half_exchange(stacks):
    T = len(stacks)

    def copies(ins, outs, sems):
        send, recv = sems
        x, y, c, q, sibling = _place()
        return [_remote(ins[i].at[:, _half(ins[i].shape[1], 1 - c)], outs[i], send.at[i], recv.at[i], sibling)
                for i in range(T)]

    def start(ins, outs, sems):
        for cp in copies(ins, outs, sems):
            cp.start()

    def finish(ins, outs, sems):
        for cp in copies(ins, outs, sems):
            cp.wait()

    return _Exchange(list(stacks),
                     [jax.ShapeDtypeStruct((s.shape[0], s.shape[1] // 2) + s.shape[2:], s.dtype) for s in stacks],
                     [pltpu.SemaphoreType.DMA((T,)), pltpu.SemaphoreType.DMA((T,))], start, finish)


def _swap_with_sibling(name, arrays):
    T = len(arrays)

    def body(*refs):
        ins, outs = refs[:T], refs[T:2 * T]
        send, recv = refs[2 * T:]
        x, y, c, q, sibling = _place()
        copies = []
        for i in range(T):
            rc = _remote(ins[i], outs[i], send.at[i], recv.at[i], sibling)
            rc.start()
            copies.append(rc)
        for cp in copies:
            cp.wait()

    return _hbm_call(name, body, arrays, [jax.ShapeDtypeStruct(a.shape, a.dtype) for a in arrays],
                     [pltpu.SemaphoreType.DMA((T,)), pltpu.SemaphoreType.DMA((T,))])


def _gather_from_all(name, block):
    R, C = block.shape

    def body(src, out, send_sems, recv_sems):
        x, y, c = lax.axis_index("x"), lax.axis_index("y"), lax.axis_index("c")
        me = 4 * x + 2 * y + c
        out[me] = src[...]
        copies = []
        for r in range(1, N_DEV):
            peer = (jnp.bitwise_xor(x, r >> 2), jnp.bitwise_xor(y, (r >> 1) & 1), jnp.bitwise_xor(c, r & 1))
            cp = pltpu.make_async_remote_copy(src_ref=src, dst_ref=out.at[me], send_sem=send_sems.at[r - 1],
                                              recv_sem=recv_sems.at[r - 1], device_id=peer, device_id_type=MESH)
            cp.start()
            copies.append(cp)
        for cp in copies:
            cp.wait()

    return pl.pallas_call(
        body, name=name, out_shape=jax.ShapeDtypeStruct((N_DEV, R, C), block.dtype),
        in_specs=[pl.BlockSpec(memory_space=pltpu.VMEM)], out_specs=pl.BlockSpec(memory_space=pltpu.VMEM),
        scratch_shapes=[pltpu.SemaphoreType.DMA((N_DEV - 1,)), pltpu.SemaphoreType.DMA((N_DEV - 1,))],
    )(block)


def _view(a):
    return a[0] if a.shape[0] == 1 else a


def _view_axis(a, axis):
    return axis - 1 if a.shape[0] == 1 else axis


def _rows(a):
    return a.reshape(-1, a.shape[-1])


def _elementwise(name, fn, ins, out_dtypes, tm):
    specs = []
    for a in ins:
        a, row0 = a if isinstance(a, tuple) else (a, 0)
        specs.append((_rows(a), a.shape[-1], 0, row0))
    shape = ins[0][0].shape if isinstance(ins[0], tuple) else ins[0].shape
    outs = _rowwise(name, fn, specs, [], [(shape[-1], dt) for dt in out_dtypes], tm=tm, n_rows=math.prod(shape[:-1]))
    return outs.reshape(shape) if len(out_dtypes) == 1 else [o.reshape(shape) for o in outs]


SMALL_ROWS = 8
CONV_ROWS = CONV_WIDTH * DN_QKV // D_MODEL
SMALL_GRAD_ROWS = 24


def _pack_small(vals, conv=None):
    tail = jnp.concatenate([vals["attn_q_gain"].reshape(-1), vals["attn_k_gain"].reshape(-1),
                            vals["dn_a_log"].reshape(-1), vals["dn_dt_bias"].reshape(-1),
                            vals["dn_o_gain"].reshape(-1)])
    tail = jnp.pad(tail, (0, D_MODEL - tail.shape[0])).reshape(1, D_MODEL)
    rows = [vals["mix_norm"], vals["mlp_norm"], vals["ple_norm"], tail, jnp.zeros((1, D_MODEL), F32)]
    if conv is not None:
        rows += [conv.reshape(CONV_ROWS, D_MODEL),
                 jnp.zeros((SMALL_GRAD_ROWS - SMALL_ROWS - CONV_ROWS, D_MODEL), F32)]
    return jnp.concatenate(rows, axis=0)


def _unpack_small(block):
    nq = 3 * A_HEAD_DIM
    t = block[6]
    return dict(
        mix_norm=block[0:2], mlp_norm=block[2:4], ple_norm=block[4:6],
        attn_q_gain=t[:nq].reshape(1, 3, A_HEAD_DIM), attn_k_gain=t[nq:2 * nq].reshape(1, 3, A_HEAD_DIM),
        dn_a_log=t[2 * nq:2 * nq + DN_HEADS].reshape(1, DN_HEADS),
        dn_dt_bias=t[2 * nq + DN_HEADS:2 * nq + 2 * DN_HEADS].reshape(1, DN_HEADS),
        dn_o_gain=t[2 * nq + 2 * DN_HEADS:2 * nq + 2 * DN_HEADS + DN_HEAD_DIM].reshape(1, DN_HEAD_DIM))


def kernel(x, p, positions, mix_norm, attn_w_qkv, attn_q_gain, attn_k_gain, attn_w_o, dn_w_in, dn_conv, dn_a_log, dn_dt_bias, dn_o_gain, dn_w_o, mlp_norm, w_up, w_down, ple_norm, w_ple, w_ple_gate, loss_target, m_mix_norm, m_attn_w_qkv, m_attn_q_gain, m_attn_k_gain, m_attn_w_o, m_dn_w_in, m_dn_conv, m_dn_a_log, m_dn_dt_bias, m_dn_o_gain, m_dn_w_o, m_mlp_norm, m_w_up, m_w_down, m_ple_norm, m_w_ple, m_w_ple_gate, v_mix_norm, v_attn_w_qkv, v_attn_q_gain, v_attn_k_gain, v_attn_w_o, v_dn_w_in, v_dn_conv, v_dn_a_log, v_dn_dt_bias, v_dn_o_gain, v_dn_w_o, v_mlp_norm, v_w_up, v_w_down, v_ple_norm, v_w_ple, v_w_ple_gate):
    given = dict(locals())
    w = {n: given[n] for n in WEIGHTS}
    m = {n: given["m_" + n] for n in WEIGHTS}
    v = {n: given["v_" + n] for n in WEIGHTS}
    kinds = ("grad", "delta", "new_m", "new_v")
    axes = {n: _view_axis(w[n], axis) for n, axis in SHARDED if n != "dn_conv"}
    chip = 2 * lax.axis_index("x") + lax.axis_index("y")
    core = lax.axis_index("c")
    shards = {n: _view(w[n]).astype(BF16) for n in axes}

    def whole(n, slots):
        return jnp.concatenate([jnp.where(chip == q, shards[n], slots[q]) for q in range(N_CHIPS)], axis=axes[n])

    def stacks_of(grads_of):
        return [jnp.stack(jnp.split(g, N_CHIPS, axis=axes[n])) for n, g in grads_of.items()]

    def chip_sums_of(names, stacks, theirs):
        mine = [lax.dynamic_slice_in_dim(s, core * (s.shape[1] // 2), s.shape[1] // 2, axis=1) for s in stacks]
        return {n: _elementwise(f"add_core_{n}", lambda a, b: a.astype(F32) + b.astype(F32), [a, b], [BF16], 128)
                for n, a, b in zip(names, mine, theirs)}

    class Hooks:
        first_cargo = _gather_exchange([shards[n] for n in ATTN_MATRICES])
        fwd_cargo = [_gather_exchange([shards[n] for n in group]) for group in CARGO_GROUPS]
        chip_sums = {}
        early = None

        def first_weights(self, results):
            return {n: whole(n, slots) for n, slots in zip(ATTN_MATRICES, results)}

        def weights_from(self, results):
            full = {n: whole(n, slots) for group, res in zip(CARGO_GROUPS, results) for n, slots in zip(group, res)}
            w_in, n_ab = full["dn_w_in"], 2 * DN_HEADS
            full["dn_w_in"] = jnp.concatenate([w_in[:, DN_QKV + n_ab:], w_in[:, :DN_QKV + n_ab],
                                               jnp.zeros((D_MODEL, DN_AB_PAD - n_ab), BF16)], axis=1)
            return full

        def split_cargo(self, early_grads):
            self.early = (list(early_grads), stacks_of(early_grads))
            return _other_half_exchange(self.early[1])

        def bwd_cargo(self, results):
            self.chip_sums.update(chip_sums_of(*self.early, results))
            return [_scatter_exchange([self.chip_sums[n] for n in group]) for group in CARGO_GROUPS]

        def last_cargo(self, attn_grads):
            stacks = stacks_of(attn_grads)
            theirs = _run_exchange("split_core_grads_attn", _other_half_exchange(stacks))
            self.chip_sums.update(chip_sums_of(list(attn_grads), stacks, theirs))
            return _scatter_exchange([self.chip_sums[n] for n in attn_grads])

    hooks = Hooks()
    big = {}
    conv_block = jnp.pad(w["dn_conv"].reshape(-1), (0, SMALL_ROWS * D_MODEL - w["dn_conv"].size))
    conv_all = _gather_from_all("gather_conv", conv_block.reshape(SMALL_ROWS, D_MODEL))
    conv_all = conv_all.reshape(N_CHIPS, 2, -1)[:, 0, :w["dn_conv"].size]
    conv_full = jnp.concatenate([conv_all[q].reshape(CONV_WIDTH, -1) for q in range(N_CHIPS)], axis=1)
    small = {n: w[n] for n in REPLICATED}
    small["dn_conv"] = conv_full

    sq, grad_x, grads, brought = _local_step(x[0], p[:, 0], positions[0], loss_target[0], small, big, hooks)
    loss = lax.psum(0.5 * sq[0, 0] / D_MODEL, ("x", "y", "c"))
    out = {}

    landed = {n: r for group, res in zip(CARGO_GROUPS + (ATTN_MATRICES,), brought) for n, r in zip(group, res)}
    half_sums = []
    for n in axes:
        o = lax.dynamic_index_in_dim(hooks.chip_sums[n], chip, axis=0, keepdims=False)
        per = math.prod(o.shape[:-1])
        r = landed[n]
        half_sums.append(_elementwise(
            f"add_chips_{n}", lambda a, b, c, d: ((a.astype(F32) + b.astype(F32)) + c.astype(F32)) + d.astype(F32),
            [o, (r, 0), (r, per), (r, 2 * per)], [F32], 128))
    other_halves = _swap_with_sibling("join_core_sums", half_sums)
    for n, a, b in zip(axes, half_sums, other_halves):
        g = jnp.where(core == 0, jnp.concatenate([a, b], axis=0), jnp.concatenate([b, a], axis=0))
        shp = w[n].shape
        res = _elementwise(f"adamw_{n}", lambda g, w_, m_, v_: (g,) + _adamw(w_, g, m_, v_),
                           [g.reshape(shp), w[n], m[n], v[n]], [F32] * 4, 256)
        for kind, arr in zip(kinds, res):
            out[kind + "_" + n] = arr.reshape(shp)

    slots = _gather_from_all("gather_small_grads", _pack_small(grads, grads["dn_conv"]))

    def small_body(s_ref, w_ref, m_ref, v_ref, sum_out, g_out, d_out, m_out, v_out):
        total = s_ref[0]
        for d in range(1, N_DEV):
            total = total + s_ref[d]
        sum_out[...] = total
        g = total[:SMALL_ROWS]
        for o, r in zip((g_out, d_out, m_out, v_out), (g,) + _adamw(w_ref[...], g, m_ref[...], v_ref[...])):
            o[...] = r

    res = pl.pallas_call(small_body, name="adamw_replicated",
                         out_shape=[jax.ShapeDtypeStruct((SMALL_GRAD_ROWS, D_MODEL), F32)]
                         + [jax.ShapeDtypeStruct((SMALL_ROWS, D_MODEL), F32)] * 4)(
        slots, _pack_small(w), _pack_small(m), _pack_small(v))
    for kind, block in zip(kinds, res[1:]):
        for n, arr in _unpack_small(block).items():
            out[kind + "_" + n] = arr
    conv_sum = res[0][SMALL_ROWS:SMALL_ROWS + CONV_ROWS].reshape(CONV_WIDTH, DN_QKV)
    cols = DN_QKV // N_CHIPS
    chip = 2 * lax.axis_index("x") + lax.axis_index("y")
    conv_mine = lax.dynamic_slice_in_dim(conv_sum, chip * cols, cols, axis=1)
    res = _elementwise("adamw_dn_conv", lambda g, w_, m_, v_: (g,) + _adamw(w_, g, m_, v_),
                       [conv_mine, w["dn_conv"][0], m["dn_conv"][0], v["dn_conv"][0]], [F32] * 4, CONV_WIDTH)
    for kind, arr in zip(kinds, res):
        out[kind + "_dn_conv"] = arr[None]

    return (loss, grad_x[None],
            *[out["grad_" + n] for n in WEIGHTS], *[out["delta_" + n] for n in WEIGHTS],
            *[out["new_m_" + n] for n in WEIGHTS], *[out["new_v_" + n] for n in WEIGHTS])
```

```python
import functools
import math

import jax
import jax.numpy as jnp
from jax import lax
from jax.experimental import pallas as pl
from jax.experimental.pallas import tpu as pltpu

F32 = jnp.float32
BF16 = jnp.bfloat16
HIGHEST = lax.Precision.HIGHEST

D_MODEL = 1024
EPS = 1e-6
SWA_GROUPS = ((128, 1), (512, 4), (2048, 16))
A_HEADS = 8
A_HEAD_DIM = 64
A_WIDTH = A_HEADS * A_HEAD_DIM
ROPE_DIM = A_HEAD_DIM // 4
ROPE_THETA = 500000.0
BAND = 128
DN_HEADS = 8
DN_HEAD_DIM = 128
DN_WIDTH = DN_HEADS * DN_HEAD_DIM
DN_QKV = 3 * DN_WIDTH
DN_AB_PAD = 128
DN_IN_PAD = DN_WIDTH + DN_QKV + DN_AB_PAD
DN_QKV0 = DN_WIDTH
DN_AB0 = DN_WIDTH + DN_QKV
DN_HB = 8
CONV_WIDTH = 4
CHUNK = 64
PLE_DIM = 256
D_FF = 4 * D_MODEL

ADAM_LR = 0.001
ADAM_B1 = 0.9
ADAM_B2 = 0.999
ADAM_EPS = 1e-08
ADAM_WD = 0.01
ADAM_STEP = 10

N_CHIPS = 4
N_DEV = 8
VMEM_LIMIT = 48 * 1024 * 1024
MESH = pl.DeviceIdType.MESH

SHARDED = (
    ("attn_w_qkv", 2), ("attn_w_o", 2), ("dn_w_in", 2), ("dn_conv", 2), ("dn_w_o", 1),
    ("w_up", 2), ("w_down", 1), ("w_ple", 2), ("w_ple_gate", 1))
ATTN_MATRICES = ("attn_w_qkv", "attn_w_o")
CARGO_GROUPS = (("w_up",), ("w_down",), ("dn_w_in", "dn_w_o", "w_ple", "w_ple_gate"))
REPLICATED = ("mix_norm", "attn_q_gain", "attn_k_gain", "dn_a_log", "dn_dt_bias", "dn_o_gain",
              "mlp_norm", "ple_norm")
WEIGHTS = ("mix_norm", "attn_w_qkv", "attn_q_gain", "attn_k_gain", "attn_w_o", "dn_w_in", "dn_conv",
           "dn_a_log", "dn_dt_bias", "dn_o_gain", "dn_w_o", "mlp_norm", "w_up", "w_down", "ple_norm",
           "w_ple", "w_ple_gate")


def _cparams(sem=None):
    return pltpu.CompilerParams(dimension_semantics=sem, vmem_limit_bytes=VMEM_LIMIT)


def _pick(n, cap, quantum=128):
    best = None
    for t in range(quantum, min(n, cap) + 1, quantum):
        if n % t == 0:
            best = t
    return n if best is None else best


_DIMS = {"nn": ((1,), (0,)), "nt": ((1,), (1,)), "tn": ((0,), (0,))}


def _mm(name, a, b, mode, out_dtypes=(F32,), extras=(), epilogue=None, rows=(), n_sums=0, tm_cap=1024):
    if mode == "nn":
        (M, K), (K2, N) = a.shape, b.shape
    elif mode == "nt":
        (M, K), (N, K2) = a.shape, b.shape
    else:
        (K, M), (K2, N) = a.shape, b.shape
    assert K == K2, (name, a.shape, b.shape)
    tn = _pick(N, 1536)
    if mode == "tn":
        tm, tk = _pick(M, tm_cap), _pick(K, 2048)
    elif tn == N and K > 1536:
        tm, tk = _pick(M, min(tm_cap, 512)), K
    else:
        tm, tk = _pick(M, tm_cap), _pick(K, 1536)
    nk = K // tk
    assert n_sums == 0 or tn == N, name
    if mode == "nn":
        a_spec = pl.BlockSpec((tm, tk), lambda i, j, k: (i, k))
        b_spec = pl.BlockSpec((tk, tn), lambda i, j, k: (k, j))
    elif mode == "nt":
        a_spec = pl.BlockSpec((tm, tk), lambda i, j, k: (i, k))
        b_spec = pl.BlockSpec((tn, tk), lambda i, j, k: (j, k))
    else:
        a_spec = pl.BlockSpec((tk, tm), lambda i, j, k: (k, i))
        b_spec = pl.BlockSpec((tk, tn), lambda i, j, k: (k, j))
    o_spec = pl.BlockSpec((tm, tn), lambda i, j, k: (i, j))
    r_spec = pl.BlockSpec((1, tn), lambda i, j, k: (0, j))
    n_extra, n_out = len(extras) + len(rows), len(out_dtypes)
    dims = (_DIMS[mode], ((), ()))

    def body(a_ref, b_ref, *rest):
        extra_refs, out_refs = rest[:n_extra], rest[n_extra:n_extra + n_out]
        sum_refs = rest[n_extra + n_out:n_extra + n_out + n_sums]
        i, k = pl.program_id(0), pl.program_id(2)
        part = lax.dot_general(a_ref[...].astype(BF16), b_ref[...].astype(BF16), dims, preferred_element_type=F32)

        def finish(total):
            vals = (total,) if epilogue is None else epilogue(total, *[e[...] for e in extra_refs])
            for o, v in zip(out_refs, vals[:n_out]):
                o[...] = v.astype(o.dtype)
            for s, v in zip(sum_refs, vals[n_out:]):
                @pl.when(i == 0)
                def _():
                    s[...] = v

                @pl.when(i > 0)
                def _():
                    s[...] += v

        if nk == 1:
            finish(part)
            return
        acc = rest[-1]

        @pl.when(k == 0)
        def _():
            acc[...] = part

        @pl.when(jnp.logical_and(k > 0, k < nk - 1))
        def _():
            acc[...] += part

        @pl.when(k == nk - 1)
        def _():
            finish(acc[...] + part)

    outs = pl.pallas_call(
        body, name=name, grid=(M // tm, N // tn, nk),
        in_specs=[a_spec, b_spec] + [o_spec] * len(extras) + [r_spec] * len(rows),
        out_specs=[o_spec] * n_out + [r_spec] * n_sums,
        out_shape=[jax.ShapeDtypeStruct((M, N), dt) for dt in out_dtypes]
        + [jax.ShapeDtypeStruct((1, N), F32)] * n_sums,
        scratch_shapes=[pltpu.VMEM((tm, tn), F32)] if nk > 1 else [],
        compiler_params=_cparams(("arbitrary" if n_sums else "parallel", "parallel", "arbitrary")),
    )(a, b, *extras, *rows)
    return outs[0] if n_out + n_sums == 1 else outs


def _rowwise(name, fn, rows, bcast, row_outs, acc_outs=(), tm=512, n_rows=None, cargo=None):
    rows = [r if isinstance(r, tuple) else (r, r.shape[1], 0) for r in rows]
    rows = [r if len(r) == 4 else r + (0,) for r in rows]
    S = rows[0][0].shape[0] if n_rows is None else n_rows
    tm = min(tm, S)
    assert S % tm == 0 and all(r[3] % tm == 0 for r in rows), (name, S, tm)
    n_row, n_bc, n_ro, n_acc = len(rows), len(bcast), len(row_outs), len(acc_outs)
    in_specs = [pl.BlockSpec((tm, w), functools.partial(lambda i, cb, rb: (i + rb, cb), cb=cb, rb=r0 // tm))
                for _, w, cb, r0 in rows]
    in_specs += [pl.BlockSpec(b.shape, lambda i: (0, 0)) for b in bcast]
    out_specs = [pl.BlockSpec((tm, c), lambda i: (i, 0)) for c, _ in row_outs]
    out_specs += [pl.BlockSpec(s, lambda i: (0, 0)) for s in acc_outs]
    out_shape = [jax.ShapeDtypeStruct((S, c), dt) for c, dt in row_outs]
    out_shape += [jax.ShapeDtypeStruct(s, F32) for s in acc_outs]

    def body(*refs):
        ins = [r[...] for r in refs[:n_row + n_bc]]
        outs = refs[n_row + n_bc:]
        vals = fn(*ins)
        if not isinstance(vals, (tuple, list)):
            vals = (vals,)
        for o, v in zip(outs[:n_ro], vals[:n_ro]):
            o[...] = v.astype(o.dtype)
        if n_acc:
            @pl.when(pl.program_id(0) == 0)
            def _():
                for o in outs[n_ro:]:
                    o[...] = jnp.zeros_like(o)
            for o, v in zip(outs[n_ro:], vals[n_ro:]):
                o[...] += v

    n_own = n_ro + n_acc
    body, c_in_specs, c_out_specs, c_out_shape, c_scratch, c_ins = _carry(cargo, n_row + n_bc, n_own, 0, body, S // tm)
    outs = pl.pallas_call(
        body, name=name, grid=(S // tm,), in_specs=in_specs + c_in_specs, out_specs=out_specs + c_out_specs,
        out_shape=out_shape + c_out_shape, scratch_shapes=c_scratch,
        compiler_params=_cparams(("arbitrary",) if n_acc or cargo is not None else ("parallel",)),
    )(*[r[0] for r in rows], *bcast, *c_ins)
    own = outs[0] if n_own == 1 else outs[:n_own]
    return own if cargo is None else (own, outs[n_own:])


def _sigmoid(x):
    return 1.0 / (1.0 + jnp.exp(-x))


def _silu(x):
    return x * _sigmoid(x)


def _softplus(x):
    return jnp.maximum(x, 0.0) + jnp.log(1.0 + jnp.exp(-jnp.abs(x)))


def _rms_fwd_fn(x, g):
    r = lax.rsqrt(jnp.mean(x * x, axis=-1, keepdims=True) + EPS)
    return (x * r) * g


def _rms_bwd_fn(x, dres, *rest):
    dh, g = sum(rest[:-1]), rest[-1]
    r = lax.rsqrt(jnp.mean(x * x, axis=-1, keepdims=True) + EPS)
    xh = x * r
    dxh = dh * g
    dx = dres + r * (dxh - xh * jnp.mean(dxh * xh, axis=-1, keepdims=True))
    return dx, dx, jnp.sum(dh * xh, axis=0, keepdims=True)


def _rms_fwd(name, x, gain, cargo=None):
    return _rowwise(name, _rms_fwd_fn, [x], [gain.reshape(1, -1)], [(x.shape[1], BF16)], cargo=cargo)


def _rms_bwd(name, x, gain, dres, dhs, cargo=None):
    return _rowwise(name, _rms_bwd_fn, [x, dres] + list(dhs), [gain.reshape(1, -1)],
                    [(x.shape[1], F32), (x.shape[1], BF16)], [(1, x.shape[1])], cargo=cargo)


def _relu2_epilogue(acc):
    r = jnp.maximum(acc, 0.0)
    return (r * r,)


def _relu2_bwd_epilogue(acc, a):
    return (acc * (2.0 * jnp.sqrt(a.astype(F32))),)


def _ple_fwd_fn(x, pp, zg):
    return x + pp * _sigmoid(zg)


def _ple_norm_fwd_fn(x, pp, zg, gain):
    out = _ple_fwd_fn(x, pp, zg)
    return out, _rms_fwd_fn(out, gain)


def _ple_bwd_fn(dx, pp, zg):
    gate = _sigmoid(zg)
    return dx * gate, dx * pp * gate * (1.0 - gate)


def _adamw(w, g, m, v):
    m = ADAM_B1 * m + (1.0 - ADAM_B1) * g
    v = ADAM_B2 * v + (1.0 - ADAM_B2) * jnp.square(g)
    m_hat = m / (1.0 - ADAM_B1 ** ADAM_STEP)
    v_hat = v / (1.0 - ADAM_B2 ** ADAM_STEP)
    delta = -ADAM_LR * (m_hat / (jnp.sqrt(v_hat) + ADAM_EPS) + ADAM_WD * w)
    return delta, m, v


def _lane_take(x, offset):
    n = x.shape[-1]
    return pltpu.roll(x, (-offset) % n, 1)


def _head_lane(shape):
    return lax.broadcasted_iota(jnp.int32, shape, 1) % A_HEAD_DIM


def _rope_partner(x):
    lane = _head_lane(x.shape)
    return jnp.where(lane < ROPE_DIM // 2, _lane_take(x, ROPE_DIM // 2),
                     jnp.where(lane < ROPE_DIM, _lane_take(x, -(ROPE_DIM // 2)), 0.0))


def _head_mean(x, bd):
    hi = x.astype(BF16)
    lo = (x - hi.astype(F32)).astype(BF16)
    b = bd.astype(BF16)
    return jnp.dot(hi, b, preferred_element_type=F32) + jnp.dot(lo, b, preferred_element_type=F32)


def _fold_heads(row):
    out = row[:, :A_HEAD_DIM]
    for h in range(1, A_HEADS):
        out = out + row[:, h * A_HEAD_DIM:(h + 1) * A_HEAD_DIM]
    return out


def _all_heads(t):
    return jnp.concatenate([t] * (A_WIDTH // t.shape[1]), axis=1)


def _qk_prep_fwd_fn(qkv, ct, st, gq, gk, bd):
    ct, st = _all_heads(ct), _all_heads(st)

    def one(t, g):
        n = t * lax.rsqrt(_head_mean(t * t, bd) + EPS) * g
        return n * ct + _rope_partner(n) * st
    q, k, v = qkv[:, :A_WIDTH], qkv[:, A_WIDTH:2 * A_WIDTH], qkv[:, 2 * A_WIDTH:]
    return one(q, gq), one(k, gk), v


def _qk_prep_bwd_fn(qkv, ct, st, dq, dk, dv, gq, gk, bd):
    ct, st = _all_heads(ct), _all_heads(st)

    def one(t, g, dy):
        r = lax.rsqrt(_head_mean(t * t, bd) + EPS)
        nh = t * r
        dn = dy * ct + _rope_partner(dy * st)
        dg = jnp.sum(dn * nh, axis=0, keepdims=True)
        dnh = dn * g
        return r * (dnh - nh * _head_mean(dnh * nh, bd)), _fold_heads(dg)
    q, k = qkv[:, :A_WIDTH], qkv[:, A_WIDTH:2 * A_WIDTH]
    dq_raw, dgq = one(q, gq, dq)
    dk_raw, dgk = one(k, gk, dk)
    return jnp.concatenate([dq_raw, dk_raw, dv], axis=1), dgq, dgk


_BATCH_DIMS = {"nn": ((2,), (1,)), "nt": ((2,), (2,)), "tn": ((1,), (1,))}


def _bdot(a, b, mode, precision=None):
    return lax.dot_general(a, b, (_BATCH_DIMS[mode], ((0,), (0,))), precision=precision,
                           preferred_element_type=F32)


def _attn_cols(h):
    return slice(h * A_HEAD_DIM, (h + 1) * A_HEAD_DIM)


def _attn_heads(ref):
    return jnp.stack([ref[:, _attn_cols(h)] for h in range(A_HEADS)])


def _band_masks():
    qi = lax.broadcasted_iota(jnp.int32, (BAND, BAND), 0)
    kj = lax.broadcasted_iota(jnp.int32, (BAND, BAND), 1)
    return kj <= qi, kj >= qi


def _attn_fwd(name, q, k, v, blocks_per_class, cargo=None):
    S = q.shape[0]
    nblk = S // BAND
    scale = A_HEAD_DIM ** -0.5

    def body(q_ref, kp_ref, kc_ref, vp_ref, vc_ref, o_ref, l_ref):
        i = pl.program_id(0)
        has_prev = (i % blocks_per_class) != 0
        m_cur, m_prev = _band_masks()
        m_prev = jnp.logical_and(m_prev, has_prev)
        q, kc, kp, vc, vp = (_attn_heads(r) for r in (q_ref, kc_ref, kp_ref, vc_ref, vp_ref))
        s_c = jnp.where(m_cur[None], _bdot(q, kc, "nt") * scale, -jnp.inf)
        s_p = jnp.where(m_prev[None], _bdot(q, kp, "nt") * scale, -jnp.inf)
        m = jnp.maximum(jnp.max(s_c, axis=-1, keepdims=True), jnp.max(s_p, axis=-1, keepdims=True))
        e_c, e_p = jnp.exp(s_c - m), jnp.exp(s_p - m)
        l = jnp.sum(e_c, axis=-1, keepdims=True) + jnp.sum(e_p, axis=-1, keepdims=True)
        o = _bdot((e_c / l).astype(BF16), vc, "nn") + _bdot((e_p / l).astype(BF16), vp, "nn")
        lse = m + jnp.log(l)
        for h in range(A_HEADS):
            o_ref[:, _attn_cols(h)] = o[h]
            l_ref[:, _attn_cols(h)] = jnp.broadcast_to(lse[h], (BAND, A_HEAD_DIM))

    cur = pl.BlockSpec((BAND, A_WIDTH), lambda i: (i, 0))
    prev = pl.BlockSpec((BAND, A_WIDTH), lambda i: (jnp.maximum(i - 1, 0), 0))
    body, c_in_specs, c_out_specs, c_out_shape, c_scratch, c_ins = _carry(cargo, 5, 2, 0, body, nblk)
    outs = pl.pallas_call(
        body, name=name, grid=(nblk,), in_specs=[cur, prev, cur, prev, cur] + c_in_specs,
        out_specs=[cur, cur] + c_out_specs,
        out_shape=[jax.ShapeDtypeStruct((S, A_WIDTH), F32)] * 2 + c_out_shape, scratch_shapes=c_scratch,
        compiler_params=_cparams(("arbitrary",)),
    )(q, k, k, v, v, *c_ins)
    return outs[0], outs[1], outs[2:]


def _carry(cargo, n_in, n_out, n_scratch, body, steps):
    if cargo is None:
        return body, [], [], [], [], []
    n_ci, n_co = len(cargo.ins), len(cargo.out_shape)

    def carrying(*refs):
        refs = list(refs)
        ins, refs = refs[:n_in], refs[n_in:]
        c_ins, refs = refs[:n_ci], refs[n_ci:]
        outs, refs = refs[:n_out], refs[n_out:]
        c_outs, refs = refs[:n_co], refs[n_co:]
        scratch, sems = refs[:n_scratch], refs[n_scratch:]

        @pl.when(pl.program_id(0) == 0)
        def _():
            cargo.start(c_ins, c_outs, sems)

        body(*ins, *outs, *scratch)

        @pl.when(pl.program_id(0) == steps - 1)
        def _():
            cargo.finish(c_ins, c_outs, sems)

    any_spec = pl.BlockSpec(memory_space=pl.ANY)
    return carrying, [any_spec] * n_ci, [any_spec] * n_co, list(cargo.out_shape), list(cargo.scratch), list(cargo.ins)


def _attn_bwd(name, q, k, v, o, lse, do, dlse, blocks_per_class, cargo=None):
    S = q.shape[0]
    nblk = S // BAND
    scale = A_HEAD_DIM ** -0.5

    def body(q_ref, kp_ref, kc_ref, vp_ref, vc_ref, o_ref, l_ref, do_ref, dl_ref,
             dq_ref, dk_ref, dv_ref, ck, cv):
        i = pl.program_id(0)

        @pl.when(i == 0)
        def _():
            ck[...] = jnp.zeros_like(ck)
            cv[...] = jnp.zeros_like(cv)

        @pl.when(i == nblk)
        def _():
            dk_ref[...] = ck[...]
            dv_ref[...] = cv[...]

        @pl.when(i < nblk)
        def _():
            has_prev = (i % blocks_per_class) != 0
            m_cur, m_prev = _band_masks()
            m_prev = jnp.logical_and(m_prev, has_prev)
            q, kc, kp, vc, vp = (_attn_heads(r) for r in (q_ref, kc_ref, kp_ref, vc_ref, vp_ref))
            do, o, dl = _attn_heads(do_ref), _attn_heads(o_ref), _attn_heads(dl_ref)
            lse = jnp.max(_attn_heads(l_ref), axis=-1, keepdims=True)
            p_c = jnp.where(m_cur[None], jnp.exp(_bdot(q, kc, "nt") * scale - lse), 0.0)
            p_p = jnp.where(m_prev[None], jnp.exp(_bdot(q, kp, "nt") * scale - lse), 0.0)
            corr = jnp.sum(dl, axis=-1, keepdims=True) - jnp.sum(do * o, axis=-1, keepdims=True)
            dob = do.astype(BF16)
            ds_c = (p_c * (_bdot(dob, vc, "nt") + corr)).astype(BF16)
            ds_p = (p_p * (_bdot(dob, vp, "nt") + corr)).astype(BF16)
            dq = (_bdot(ds_c, kc, "nn") + _bdot(ds_p, kp, "nn")) * scale
            dk_p, dk_c = _bdot(ds_p, q, "tn") * scale, _bdot(ds_c, q, "tn") * scale
            dv_p, dv_c = _bdot(p_p.astype(BF16), dob, "tn"), _bdot(p_c.astype(BF16), dob, "tn")
            for h in range(A_HEADS):
                sl = _attn_cols(h)
                dq_ref[:, sl] = dq[h]
                dk_ref[:, sl] = ck[:, sl] + dk_p[h]
                dv_ref[:, sl] = cv[:, sl] + dv_p[h]
                ck[:, sl] = dk_c[h]
                cv[:, sl] = dv_c[h]

    last = nblk - 1
    cur = pl.BlockSpec((BAND, A_WIDTH), lambda i: (jnp.minimum(i, last), 0))
    prev = pl.BlockSpec((BAND, A_WIDTH), lambda i: (jnp.minimum(jnp.maximum(i - 1, 0), last), 0))
    body, c_in_specs, c_out_specs, c_out_shape, c_scratch, c_ins = _carry(cargo, 9, 3, 2, body, nblk + 1)
    outs = pl.pallas_call(
        body, name=name, grid=(nblk + 1,),
        in_specs=[cur, prev, cur, prev, cur, cur, cur, cur, cur] + c_in_specs,
        out_specs=[cur, prev, prev] + c_out_specs,
        out_shape=[jax.ShapeDtypeStruct((S, A_WIDTH), F32)] * 3 + c_out_shape,
        scratch_shapes=[pltpu.VMEM((BAND, A_WIDTH), F32)] * 2 + c_scratch,
        compiler_params=_cparams(("arbitrary",)),
    )(q, k, k, v, v, o, lse, do, dlse, *c_ins)
    return outs[0], outs[1], outs[2], outs[3:]


def _merge_fwd_fn(o0, o1, o2, l0, l1, l2):
    m = jnp.maximum(jnp.maximum(l0, l1), l2)
    e0, e1, e2 = jnp.exp(l0 - m), jnp.exp(l1 - m), jnp.exp(l2 - m)
    return (e0 * o0 + e1 * o1 + e2 * o2) / (e0 + e1 + e2)


def _merge_bwd_fn(o0, o1, o2, l0, l1, l2, dom):
    m = jnp.maximum(jnp.maximum(l0, l1), l2)
    e0, e1, e2 = jnp.exp(l0 - m), jnp.exp(l1 - m), jnp.exp(l2 - m)
    den = e0 + e1 + e2
    w0, w1, w2 = e0 / den, e1 / den, e2 / den
    dw0, dw1, dw2 = dom * o0, dom * o1, dom * o2
    mean = w0 * dw0 + w1 * dw1 + w2 * dw2
    return w0 * dom, w1 * dom, w2 * dom, w0 * (dw0 - mean), w1 * (dw1 - mean), w2 * (dw2 - mean)


def _to_classes(t, d):
    if d == 1:
        return t
    S, C = t.shape
    return t.reshape(S // d, d, C).transpose(1, 0, 2).reshape(S, C)


def _from_classes(t, d):
    if d == 1:
        return t
    S, C = t.shape
    return t.reshape(d, S // d, C).transpose(1, 0, 2).reshape(S, C)


def _rope_lane_tables(positions):
    inv_freq = ROPE_THETA ** (-jnp.arange(0, ROPE_DIM, 2, dtype=F32) / ROPE_DIM)
    ang = positions.astype(F32)[:, None] * inv_freq
    cos, sin = jnp.cos(ang), jnp.sin(ang)
    S = positions.shape[0]
    rest = A_HEAD_DIM - ROPE_DIM
    ct = jnp.concatenate([cos, cos, jnp.ones((S, rest), F32)], axis=1)
    st = jnp.concatenate([-sin, sin, jnp.zeros((S, rest), F32)], axis=1)
    return jnp.tile(ct, (1, 2)), jnp.tile(st, (1, 2))


def _head_mean_matrix():
    r = jnp.arange(A_WIDTH) // A_HEAD_DIM
    return (r[:, None] == r[None, :]).astype(F32) * (1.0 / A_HEAD_DIM)


CONV_LANES = 1024
CONV_BWD_LANES = 512
PAST = CONV_WIDTH - 1


def _strip_starts(tm, strip):
    return range(0, tm, strip)


def _conv_fwd(name, proj, w):
    S = proj.shape[0]
    tm, tc = min(512, S), CONV_LANES
    per8 = tm // 8
    off = DN_QKV0 // tc

    def body(x_ref, halo_ref, w_ref, o_ref, xs):
        i = pl.program_id(0)
        xs[0:8, :] = jnp.where(i > 0, halo_ref[...], 0.0)
        xs[8:, :] = x_ref[...]
        strip = 16
        for r0 in _strip_starts(tm, strip):
            acc = w_ref[PAST:CONV_WIDTH, :] * x_ref[pl.ds(r0, strip), :]
            for j in range(PAST):
                acc = acc + w_ref[j:j + 1, :] * xs[pl.ds(8 - PAST + j + r0, strip), :]
            o_ref[pl.ds(r0, strip), :] = acc

    return pl.pallas_call(
        body, name=name, grid=(S // tm, DN_QKV // tc),
        in_specs=[pl.BlockSpec((tm, tc), lambda i, j: (i, j + off)),
                  pl.BlockSpec((8, tc), lambda i, j: (jnp.maximum(i * per8 - 1, 0), j + off)),
                  pl.BlockSpec((CONV_WIDTH, tc), lambda i, j: (0, j))],
        out_specs=pl.BlockSpec((tm, tc), lambda i, j: (i, j)),
        out_shape=jax.ShapeDtypeStruct((S, DN_QKV), F32),
        scratch_shapes=[pltpu.VMEM((tm + 8, tc), F32)],
        compiler_params=_cparams(("parallel", "parallel")),
    )(proj, proj, w)


def _conv_bwd(name, proj, dpre, w):
    S = proj.shape[0]
    tm, tc = min(512, S), CONV_BWD_LANES
    per8 = tm // 8
    off = DN_QKV0 // tc
    last8 = S // 8 - 1
    nrow = S // tm

    def body(x_ref, xh_ref, d_ref, dh_ref, w_ref, dx_ref, dw_ref, xs, ds):
        i = pl.program_id(1)
        xs[0:8, :] = jnp.where(i > 0, xh_ref[...], 0.0)
        xs[8:, :] = x_ref[...]
        ds[0:tm, :] = d_ref[...]
        ds[tm:, :] = jnp.where(i < nrow - 1, dh_ref[...], 0.0)
        strip = 16
        sums = [jnp.zeros((8, tc), F32)] * CONV_WIDTH
        for r0 in _strip_starts(tm, strip):
            rows = pl.ds(r0, strip)
            d = d_ref[rows, :]
            acc = w_ref[PAST:CONV_WIDTH, :] * d
            for j in range(PAST):
                acc = acc + w_ref[j:j + 1, :] * ds[pl.ds(r0 + PAST - j, strip), :]
            dx_ref[rows, :] = acc.astype(dx_ref.dtype)
            taps = [xs[pl.ds(8 - PAST + j + r0, strip), :] for j in range(PAST)] + [x_ref[rows, :]]
            for j, tap in enumerate(taps):
                prod = d * tap
                sums[j] = sums[j] + (prod[0:8] + prod[8:16])

        @pl.when(i == 0)
        def _():
            dw_ref[...] = jnp.zeros_like(dw_ref)

        for j in range(CONV_WIDTH):
            dw_ref[j:j + 1, :] += jnp.sum(sums[j], axis=0, keepdims=True)

    return pl.pallas_call(
        body, name=name, grid=(DN_QKV // tc, nrow),
        in_specs=[pl.BlockSpec((tm, tc), lambda j, i: (i, j + off)),
                  pl.BlockSpec((8, tc), lambda j, i: (jnp.maximum(i * per8 - 1, 0), j + off)),
                  pl.BlockSpec((tm, tc), lambda j, i: (i, j)),
                  pl.BlockSpec((8, tc), lambda j, i: (jnp.minimum((i + 1) * per8, last8), j)),
                  pl.BlockSpec((CONV_WIDTH, tc), lambda j, i: (0, j))],
        out_specs=[pl.BlockSpec((tm, tc), lambda j, i: (i, j)),
                   pl.BlockSpec((CONV_WIDTH, tc), lambda j, i: (0, j))],
        out_shape=[jax.ShapeDtypeStruct((S, DN_QKV), BF16), jax.ShapeDtypeStruct((CONV_WIDTH, DN_QKV), F32)],
        scratch_shapes=[pltpu.VMEM((tm + 8, tc), F32)] * 2,
        compiler_params=_cparams(("parallel", "arbitrary")),
    )(proj, proj, dpre, dpre, w)


def _gate_lane(shape):
    return lax.broadcasted_iota(jnp.int32, shape, 1)


GATES_ROWS = 256


def _chunk_cumsum_matrix():
    r = jnp.arange(GATES_ROWS)
    return ((r[:, None] >= r[None, :]) & (r[:, None] // CHUNK == r[None, :] // CHUNK)).astype(F32)


def _gates_fwd_fn(ab, alog, dt, cum):
    g = -jnp.exp(alog) * _softplus(ab + dt)
    gc = jnp.dot(cum, g, precision=HIGHEST, preferred_element_type=F32)
    return jnp.where(_gate_lane(ab.shape) < DN_HEADS, gc, _sigmoid(ab))


def _gates_bwd_fn(ab, dgb, alog, dt, cum):
    lane = _gate_lane(ab.shape)
    is_g = lane < DN_HEADS
    neg_a = -jnp.exp(alog)
    sp = _softplus(ab + dt)
    dsp = _sigmoid(ab + dt)
    beta = _sigmoid(ab)
    dgc = jnp.where(is_g, dgb, 0.0)
    dg = lax.dot_general(cum, dgc, (_DIMS["tn"], ((), ())), precision=HIGHEST, preferred_element_type=F32)
    dab = jnp.where(is_g, dg * neg_a * dsp, jnp.where(lane < 2 * DN_HEADS, dgb * beta * (1.0 - beta), 0.0))
    d_alog = jnp.sum(dg * neg_a * sp, axis=0, keepdims=True)
    d_dt = jnp.sum(dg * neg_a * dsp, axis=0, keepdims=True)
    return dab, d_alog, d_dt


def _chunk_math(precision):
    def dg(a, b, mode, prec=precision):
        return _bdot(a, b, mode, prec)

    @jax.custom_vjp
    def nn(a, b):
        return dg(a, b, "nn")

    @jax.custom_vjp
    def nt(a, b):
        return dg(a, b, "nt")

    @jax.custom_vjp
    def tn(a, b):
        return dg(a, b, "tn")

    nn.defvjp(lambda a, b: (nn(a, b), (a, b)), lambda r, g: (nt(g, r[1]), tn(r[0], g)))
    nt.defvjp(lambda a, b: (nt(a, b), (a, b)), lambda r, g: (nn(g, r[1]), tn(g, r[0])))
    tn.defvjp(lambda a, b: (tn(a, b), (a, b)), lambda r, g: (nt(r[1], g), nn(r[0], g)))

    def split(x):
        hi = x.astype(BF16)
        return hi, (x - hi.astype(F32)).astype(BF16)

    def fine(a, b, mode):
        ah, al = split(a)
        bh, bl = split(b)
        return dg(ah, bh, mode, None) + (dg(ah, bl, mode, None) + dg(al, bh, mode, None))

    def unit_lower_inverse(a):
        row = lax.broadcasted_iota(jnp.int32, a.shape, 1)
        col = lax.broadcasted_iota(jnp.int32, a.shape, 2)
        x = -a
        p = jnp.where(row == col, 1.0, 0.0) + x
        for _ in range(int(math.log2(CHUNK)) - 1):
            x = fine(x, x, "nn")
            p = p + fine(p, x, "nn")
        return p

    @jax.custom_vjp
    def solve2(a, ti, r1, r2):
        return fine(ti, r1, "nn"), fine(ti, r2, "nn")

    def solve2_fwd(a, ti, r1, r2):
        s1, s2 = fine(ti, r1, "nn"), fine(ti, r2, "nn")
        return (s1, s2), (ti, s1, s2)

    def solve2_bwd(res, g):
        ti, s1, s2 = res
        d1, d2 = fine(ti, g[0], "tn"), fine(ti, g[1], "tn")
        return -(fine(d1, s1, "nt") + fine(d2, s2, "nt")), jnp.zeros_like(ti), d1, d2

    solve2.defvjp(solve2_fwd, solve2_bwd)

    def chunk_fn(pq, pk, pv, z, g_col, b_col, g_row, ogain, s_in, inverse=None):
        nb = pq.shape[0]
        sq = (nb, CHUNK, CHUNK)
        row = lax.broadcasted_iota(jnp.int32, sq, 1)
        col = lax.broadcasted_iota(jnp.int32, sq, 2)
        lower, strict = row >= col, row > col
        q, k, v = _silu(pq), _silu(pk), _silu(pv)
        q = q * lax.rsqrt(jnp.sum(q * q, axis=-1, keepdims=True) + EPS) * (DN_HEAD_DIM ** -0.5)
        k = k * lax.rsqrt(jnp.sum(k * k, axis=-1, keepdims=True) + EPS)
        gc_wide = jnp.broadcast_to(g_col, pq.shape)
        gc_i = jnp.broadcast_to(g_col, sq)
        gc_j = jnp.broadcast_to(g_row, sq)
        is_last = lax.broadcasted_iota(jnp.int32, pq.shape, 1) == CHUNK - 1
        g_last = jnp.sum(jnp.where(is_last, gc_wide, 0.0), axis=1, keepdims=True)
        decay = jnp.exp(jnp.where(lower, gc_i - gc_j, -jnp.inf))
        kb = k * b_col
        a_mat = jnp.where(strict, nt(kb, k) * decay, 0.0)
        eg = jnp.exp(gc_wide)
        ti = unit_lower_inverse(a_mat) if inverse is None else inverse
        u, w = solve2(a_mat, ti, v * b_col, kb * eg)
        attn = nt(q, k) * decay
        q_dec = q * eg
        k_dec = k * jnp.exp(g_last - gc_wide)
        c_dec = jnp.exp(g_last)
        v_new = u - nn(w, s_in)
        o = nn(q_dec, s_in) + nn(attn, v_new)
        s_out = s_in * c_dec + tn(k_dec, v_new)
        y = o * lax.rsqrt(jnp.mean(o * o, axis=-1, keepdims=True) + EPS) * ogain * _silu(z)
        return (y, s_out, ti) if inverse is None else (y, s_out)

    return chunk_fn


DN_PRECISION = None


def _chunk_specs(n_of):
    groups = DN_HEADS // DN_HB
    wide = DN_HB * DN_HEAD_DIM
    hd = pl.BlockSpec((CHUNK, wide), lambda h, n: (n_of(n), h))
    specs = dict(
        pq=hd,
        pk=pl.BlockSpec((CHUNK, wide), lambda h, n: (n_of(n), groups + h)),
        pv=pl.BlockSpec((CHUNK, wide), lambda h, n: (n_of(n), 2 * groups + h)),
        z=hd,
        gates=pl.BlockSpec((CHUNK, DN_AB_PAD), lambda h, n: (n_of(n), 0)),
        row=pl.BlockSpec((DN_HB, None, 1, CHUNK), lambda h, n: (h, n_of(n), 0, 0)),
        gain=pl.BlockSpec((1, DN_HEAD_DIM), lambda h, n: (0, 0)),
        state=pl.BlockSpec((DN_HB, None, DN_HEAD_DIM, DN_HEAD_DIM), lambda h, n: (h, n_of(n), 0, 0)),
        inverse=pl.BlockSpec((DN_HB, None, CHUNK, CHUNK), lambda h, n: (h, n_of(n), 0, 0)),
        qkv=pl.BlockSpec((CHUNK, DN_QKV), lambda h, n: (n_of(n), 0)),
        head=hd,
    )
    return specs


def _head_cols(j):
    return slice(j * DN_HEAD_DIM, (j + 1) * DN_HEAD_DIM)


def _split_heads(ref):
    return jnp.stack([ref[:, _head_cols(j)] for j in range(DN_HB)])


def _gate_columns(gates, first_lane):
    lane = lax.broadcasted_iota(jnp.int32, gates.shape, 1)
    return jnp.stack([jnp.sum(jnp.where(lane == first_lane + h, gates, 0.0), axis=-1, keepdims=True)
                      for h in range(DN_HEADS)])


def _gate_lanes(columns, first_lane):
    shape = (columns.shape[1], DN_AB_PAD)
    lane = lax.broadcasted_iota(jnp.int32, shape, 1)
    out = jnp.zeros(shape, F32)
    for h in range(DN_HEADS):
        out = out + jnp.where(lane == first_lane + h, columns[h], 0.0)
    return out


def _chunk_fwd(name, pre, proj, gates, g_row, ogain):
    assert DN_HB == DN_HEADS
    S = pre.shape[0]
    N = S // CHUNK
    chunk_fn = _chunk_math(DN_PRECISION)
    sp = _chunk_specs(lambda n: n)

    def body(pq, pk, pv, z, gb, gr, og, y_ref, sin_ref, inv_ref, st):
        @pl.when(pl.program_id(1) == 0)
        def _():
            st[...] = jnp.zeros_like(st)

        s_in = st[...]
        sin_ref[...] = s_in
        y, s_out, inverse = chunk_fn(_split_heads(pq), _split_heads(pk), _split_heads(pv), _split_heads(z),
                                     _gate_columns(gb[...], 0), _gate_columns(gb[...], DN_HEADS), gr[...],
                                     og[...], s_in)
        for j in range(DN_HB):
            y_ref[:, _head_cols(j)] = y[j].astype(y_ref.dtype)
        inv_ref[...] = inverse
        st[...] = s_out

    return pl.pallas_call(
        body, name=name, grid=(DN_HEADS // DN_HB, N),
        in_specs=[sp["pq"], sp["pk"], sp["pv"], sp["z"], sp["gates"], sp["row"], sp["gain"]],
        out_specs=[sp["head"], sp["state"], sp["inverse"]],
        out_shape=[jax.ShapeDtypeStruct((S, DN_WIDTH), BF16),
                   jax.ShapeDtypeStruct((DN_HEADS, N, DN_HEAD_DIM, DN_HEAD_DIM), F32),
                   jax.ShapeDtypeStruct((DN_HEADS, N, CHUNK, CHUNK), F32)],
        scratch_shapes=[pltpu.VMEM((DN_HB, DN_HEAD_DIM, DN_HEAD_DIM), F32)],
        compiler_params=_cparams(("parallel", "arbitrary")),
    )(pre, pre, pre, proj, gates, g_row, ogain)


def _chunk_bwd(name, pre, proj, gates, g_row, ogain, s_in_all, inverse_all, dy):
    assert DN_HB == DN_HEADS
    S = pre.shape[0]
    N = S // CHUNK
    chunk_fn = _chunk_math(DN_PRECISION)
    sp = _chunk_specs(lambda n: N - 1 - n)

    def body(pq, pk, pv, z, gb, gr, og, sin_ref, inv_ref, dy_ref,
             dpre_ref, dz_ref, dgb_ref, dgr_ref, dog_ref, ds):
        @pl.when(pl.program_id(1) == 0)
        def _():
            ds[...] = jnp.zeros_like(ds)
            dog_ref[...] = jnp.zeros_like(dog_ref)

        inverse = inv_ref[...]
        prim = (_split_heads(pq), _split_heads(pk), _split_heads(pv), _split_heads(z),
                _gate_columns(gb[...], 0), _gate_columns(gb[...], DN_HEADS), gr[...], og[...], sin_ref[...])
        _, vjp = jax.vjp(lambda *a: chunk_fn(*a, inverse=inverse), *prim)
        gq, gk, gv, gz, ggc, gbc, ggr, gog, gs = vjp((_split_heads(dy_ref), ds[...]))
        for j in range(DN_HB):
            for part, g in enumerate((gq, gk, gv)):
                dpre_ref[:, pl.ds(part * DN_WIDTH + j * DN_HEAD_DIM, DN_HEAD_DIM)] = g[j]
            dz_ref[:, _head_cols(j)] = gz[j].astype(dz_ref.dtype)
        dgb_ref[...] = _gate_lanes(ggc, 0) + _gate_lanes(gbc, DN_HEADS)
        dgr_ref[...] = ggr
        dog_ref[...] += gog
        ds[...] = gs

    hd = sp["head"]
    return pl.pallas_call(
        body, name=name, grid=(1, N),
        in_specs=[sp["pq"], sp["pk"], sp["pv"], sp["z"], sp["gates"], sp["row"], sp["gain"],
                  sp["state"], sp["inverse"], hd],
        out_specs=[sp["qkv"], hd, sp["gates"], sp["row"], sp["gain"]],
        out_shape=[jax.ShapeDtypeStruct((S, DN_QKV), F32), jax.ShapeDtypeStruct((S, DN_WIDTH), BF16),
                   jax.ShapeDtypeStruct((S, DN_AB_PAD), F32),
                   jax.ShapeDtypeStruct((DN_HEADS, N, 1, CHUNK), F32), jax.ShapeDtypeStruct((1, DN_HEAD_DIM), F32)],
        scratch_shapes=[pltpu.VMEM((DN_HB, DN_HEAD_DIM, DN_HEAD_DIM), F32)],
        compiler_params=_cparams(("arbitrary", "arbitrary")),
    )(pre, pre, pre, proj, gates, g_row, ogain, s_in_all, inverse_all, dy)


def _mm_rms_bwd(name, d_out, w, x, gain, dres):
    return _mm(name, d_out, w, "nt", out_dtypes=(F32, BF16), extras=(x, dres), rows=(gain.reshape(1, -1),),
               epilogue=lambda acc, x_, dres_, g: _rms_bwd_fn(x_, dres_, acc, g), n_sums=1, tm_cap=512)


def _residual_norm_epilogue(acc, res, gain):
    x = acc + res
    return x, _rms_fwd_fn(x, gain)


def _ple_loss_fn(x_mid, pp, zg, t):
    gate = _sigmoid(zg)
    err = x_mid + pp * gate - t
    dy = err * (1.0 / D_MODEL)
    return dy, dy * gate, dy * pp * gate * (1.0 - gate), jnp.broadcast_to(jnp.sum(err * err, keepdims=True), (1, 128))


def _mlp_ple_fwd(tag, x_in, h, p_l, w_up, w_down, norm_ple, w_ple, w_gate, next_gain=None, target=None):
    a = _mm(f"{tag}_up", h, w_up, "nn", out_dtypes=(BF16,), epilogue=_relu2_epilogue)
    x_mid, hg = _mm(f"{tag}_down", a, w_down, "nn", out_dtypes=(F32, BF16), extras=(x_in,),
                    rows=(norm_ple.reshape(1, -1),), epilogue=_residual_norm_epilogue)
    zg = _mm(f"{tag}_gate", hg, w_gate, "nn")
    pp = _mm(f"{tag}_ple", p_l, w_ple, "nn")
    sv = dict(x_in=x_in, h=h, a=a, x_mid=x_mid, hg=hg, zg=zg, pp=pp)
    if target is not None:
        sv["dy"], sv["dpp"], sv["dzg"], sv["sq"] = _rowwise(
            f"{tag}_ple_loss", _ple_loss_fn, [x_mid, pp, zg, target], [],
            [(D_MODEL, F32), (D_MODEL, BF16), (D_MODEL, BF16)], [(1, 128)])
        return None, None, sv
    if next_gain is None:
        return _rowwise(f"{tag}_ple_out", _ple_fwd_fn, [x_mid, pp, zg], [], [(D_MODEL, F32)]), None, sv
    x_out, h_next = _rowwise(f"{tag}_ple_out", _ple_norm_fwd_fn, [x_mid, pp, zg], [next_gain.reshape(1, -1)],
                             [(D_MODEL, F32), (D_MODEL, BF16)])
    return x_out, h_next, sv


def _mlp_ple_bwd(tag, dx, sv, p_l, norm_mlp, w_up, w_down, norm_ple, w_ple, w_gate):
    if "dpp" in sv:
        dpp, dzg = sv["dpp"], sv["dzg"]
    else:
        dpp, dzg = _rowwise(f"{tag}_ple_bwd", _ple_bwd_fn, [dx, sv["pp"], sv["zg"]], [],
                            [(D_MODEL, BF16), (D_MODEL, BF16)])
    d_w_ple = _mm(f"{tag}_d_w_ple", p_l, dpp, "tn", out_dtypes=(BF16,))
    d_w_gate = _mm(f"{tag}_d_w_gate", sv["hg"], dzg, "tn", out_dtypes=(BF16,))
    dx_mid, dx_mid_b, d_norm_ple = _mm_rms_bwd(f"{tag}_d_hg", dzg, w_gate, sv["x_mid"], norm_ple, dx)
    du = _mm(f"{tag}_d_u", dx_mid_b, w_down, "nt", out_dtypes=(BF16,), extras=(sv["a"],),
             epilogue=_relu2_bwd_epilogue)
    d_w_down = _mm(f"{tag}_d_w_down", sv["a"], dx_mid_b, "tn", out_dtypes=(BF16,))
    d_w_up = _mm(f"{tag}_d_w_up", sv["h"], du, "tn", out_dtypes=(BF16,))
    dx_in, dx_in_b, d_norm_mlp = _mm_rms_bwd(f"{tag}_d_h", du, w_up, sv["x_in"], norm_mlp, dx_mid)
    return dx_in, dx_in_b, dict(mlp_norm=d_norm_mlp, w_up=d_w_up, w_down=d_w_down, ple_norm=d_norm_ple,
                                w_ple=d_w_ple, w_ple_gate=d_w_gate)


class _NoHooks:
    first_cargo = None
    fwd_cargo = (None,) * len(SWA_GROUPS)

    def first_weights(self, results):
        return {}

    def weights_from(self, results):
        return {}

    def split_cargo(self, early_grads):
        return None

    def bwd_cargo(self, results):
        return (None,) * len(SWA_GROUPS)

    def last_cargo(self, attn_grads):
        return None


def _with_cargo(result, cargo):
    return (result, ()) if cargo is None else result


def _local_step(x, p, positions, target, small, big, hooks=_NoHooks()):
    S = x.shape[0]
    ct, st = _rope_lane_tables(positions)
    bd = _head_mean_matrix()

    h0, first = _with_cargo(_rms_fwd("l0_mix_norm", x, small["mix_norm"][0], cargo=hooks.first_cargo),
                            hooks.first_cargo)
    big = {**big, **hooks.first_weights(first)}
    attn, brought = [], []
    for g, (window, d) in enumerate(SWA_GROUPS):
        assert window // d == BAND and (S // d) % BAND == 0
        h0g = _to_classes(h0, d)
        ctg, stg = _to_classes(ct, d), _to_classes(st, d)
        w_g = big["attn_w_qkv"][:, g * 3 * A_WIDTH:(g + 1) * 3 * A_WIDTH]
        gq = jnp.tile(small["attn_q_gain"][0, g], A_HEADS).reshape(1, A_WIDTH)
        gk = jnp.tile(small["attn_k_gain"][0, g], A_HEADS).reshape(1, A_WIDTH)
        qkv = _mm(f"l0_qkv{g}", h0g, w_g, "nn")
        q, k, v = _rowwise(f"l0_qk_prep{g}", _qk_prep_fwd_fn, [qkv, ctg, stg], [gq, gk, bd], [(A_WIDTH, BF16)] * 3)
        o, lse, cargo_out = _attn_fwd(f"l0_attn{g}", q, k, v, (S // d) // BAND, cargo=hooks.fwd_cargo[g])
        brought.append(cargo_out)
        attn.append(dict(d=d, h0g=h0g, ct=ctg, st=stg, w=w_g, gq=gq, gk=gk, qkv=qkv, q=q, k=k, v=v, o=o, lse=lse,
                         o_tok=_from_classes(o, d), lse_tok=_from_classes(lse, d)))
    big = {**big, **hooks.weights_from(brought)}
    om = _rowwise("l0_merge", _merge_fwd_fn, [a["o_tok"] for a in attn] + [a["lse_tok"] for a in attn], [],
                  [(A_WIDTH, BF16)])
    x1, h1 = _mm("l0_attn_out", om, big["attn_w_o"], "nn", out_dtypes=(F32, BF16), extras=(x,),
                 rows=(small["mlp_norm"][0].reshape(1, -1),), epilogue=_residual_norm_epilogue)
    x3, h3, sv0 = _mlp_ple_fwd("l0", x1, h1, p[0], big["w_up"][0], big["w_down"][0], small["ple_norm"][0],
                               big["w_ple"][0], big["w_ple_gate"][0], next_gain=small["mix_norm"][1])

    N = S // CHUNK
    proj = _mm("l1_in", h3, big["dn_w_in"], "nn")
    pre = _conv_fwd("l1_conv", proj, small["dn_conv"])
    ab = proj[:, DN_AB0:DN_AB0 + DN_AB_PAD]
    lane_pad = DN_AB_PAD - DN_HEADS
    alog_row = jnp.pad(small["dn_a_log"][0], (0, lane_pad)).reshape(1, DN_AB_PAD)
    dt_row = jnp.pad(small["dn_dt_bias"][0], (0, lane_pad)).reshape(1, DN_AB_PAD)
    cum = _chunk_cumsum_matrix()
    gb = _rowwise("l1_gates", _gates_fwd_fn, [ab], [alog_row, dt_row, cum], [(DN_AB_PAD, F32)], tm=GATES_ROWS)
    g_row = gb[:, :DN_HEADS].T.reshape(DN_HEADS, N, 1, CHUNK)
    ogain = small["dn_o_gain"][0].reshape(1, DN_HEAD_DIM)
    y, s_in_all, inverse_all = _chunk_fwd("l1_delta", pre, proj, gb, g_row, ogain)
    x4, h4 = _mm("l1_dn_out", y, big["dn_w_o"], "nn", out_dtypes=(F32, BF16), extras=(x3,),
                 rows=(small["mlp_norm"][1].reshape(1, -1),), epilogue=_residual_norm_epilogue)
    _, _, sv1 = _mlp_ple_fwd("l1", x4, h4, p[1], big["w_up"][1], big["w_down"][1], small["ple_norm"][1],
                             big["w_ple"][1], big["w_ple_gate"][1], target=target)
    dy, sq = sv1["dy"], sv1["sq"]

    dx4, dx4_b, gl1 = _mlp_ple_bwd("l1", dy, sv1, p[1], small["mlp_norm"][1], big["w_up"][1], big["w_down"][1],
                            small["ple_norm"][1], big["w_ple"][1], big["w_ple_gate"][1])
    d_y = _mm("l1_d_y", dx4_b, big["dn_w_o"], "nt")
    d_dn_w_o = _mm("l1_d_w_o", y, dx4_b, "tn", out_dtypes=(BF16,))
    dpre, dz, dgb_cols, dg_row, d_ogain = _chunk_bwd(
        "l1_delta_bwd", pre, proj, gb, g_row, ogain, s_in_all, inverse_all, d_y)
    dconv_in, d_conv_w = _conv_bwd("l1_conv_bwd", proj, dpre, small["dn_conv"])
    dgb = dgb_cols + jnp.pad(dg_row.reshape(DN_HEADS, S).T, ((0, 0), (0, DN_AB_PAD - DN_HEADS)))
    dab, d_alog, d_dt = _rowwise("l1_gates_bwd", _gates_bwd_fn, [ab, dgb], [alog_row, dt_row, cum],
                                 [(DN_AB_PAD, F32)], [(1, DN_AB_PAD), (1, DN_AB_PAD)], tm=GATES_ROWS)
    dproj = jnp.concatenate([dz, dconv_in, dab.astype(BF16)], axis=1)
    d_dn_w_in = _mm("l1_d_w_in", h3, dproj, "tn", out_dtypes=(BF16,))
    dx3, _, d_mix1 = _mm_rms_bwd("l1_d_h", dproj, big["dn_w_in"], x3, small["mix_norm"][1], dx4)

    dx1, dx1_b, gl0 = _mlp_ple_bwd("l0", dx3, sv0, p[0], small["mlp_norm"][0], big["w_up"][0], big["w_down"][0],
                            small["ple_norm"][0], big["w_ple"][0], big["w_ple_gate"][0])
    early = dict(
        dn_w_in=jnp.concatenate([d_dn_w_in[:, DN_QKV0:DN_AB0 + 2 * DN_HEADS], d_dn_w_in[:, :DN_WIDTH]], axis=1),
        dn_w_o=d_dn_w_o,
        w_up=jnp.stack([gl0["w_up"], gl1["w_up"]]),
        w_down=jnp.stack([gl0["w_down"], gl1["w_down"]]),
        w_ple=jnp.stack([gl0["w_ple"], gl1["w_ple"]]),
        w_ple_gate=jnp.stack([gl0["w_ple_gate"], gl1["w_ple_gate"]]))
    split_cargo = hooks.split_cargo(early)
    dom = _mm("l0_d_om", dx1_b, big["attn_w_o"], "nt")
    d_attn_w_o = _mm("l0_d_w_o", om, dx1_b, "tn", out_dtypes=(BF16,))
    merged, split = _with_cargo(
        _rowwise("l0_merge_bwd", _merge_bwd_fn, [a["o_tok"] for a in attn] + [a["lse_tok"] for a in attn] + [dom], [],
                 [(A_WIDTH, F32)] * 6, cargo=split_cargo), split_cargo)
    bwd_cargo = hooks.bwd_cargo(split)
    dh0, d_w_qkv, d_gq, d_gk, brought_bwd = [], [], [], [], []
    for g, a in enumerate(attn):
        do_g, dl_g = _to_classes(merged[g], a["d"]), _to_classes(merged[3 + g], a["d"])
        dqn, dkn, dvn, cargo_out = _attn_bwd(f"l0_attn_bwd{g}", a["q"], a["k"], a["v"], a["o"], a["lse"], do_g, dl_g,
                                             (S // a["d"]) // BAND, cargo=bwd_cargo[g])
        brought_bwd.append(cargo_out)
        dqkv, dgq, dgk = _rowwise(f"l0_qk_prep_bwd{g}", _qk_prep_bwd_fn, [a["qkv"], a["ct"], a["st"], dqn, dkn, dvn],
                                  [a["gq"], a["gk"], bd], [(3 * A_WIDTH, BF16)], [(1, A_HEAD_DIM)] * 2)
        d_w_qkv.append(_mm(f"l0_d_w_qkv{g}", a["h0g"], dqkv, "tn", out_dtypes=(BF16,)))
        dh0.append(_from_classes(_mm(f"l0_d_h{g}", dqkv, a["w"], "nt"), a["d"]))
        d_gq.append(dgq)
        d_gk.append(dgk)
    attn_grads = dict(attn_w_qkv=jnp.concatenate(d_w_qkv, axis=1), attn_w_o=d_attn_w_o)
    last_cargo = hooks.last_cargo(attn_grads)
    (grad_x, _, d_mix0), last = _with_cargo(
        _rms_bwd("l0_mix_norm_bwd", x, small["mix_norm"][0], dx1, dh0, cargo=last_cargo), last_cargo)
    brought_bwd.append(last)

    grads = dict(
        mix_norm=jnp.concatenate([d_mix0, d_mix1], axis=0),
        attn_q_gain=jnp.concatenate(d_gq, axis=0)[None],
        attn_k_gain=jnp.concatenate(d_gk, axis=0)[None],
        **attn_grads,
        dn_conv=d_conv_w,
        dn_a_log=d_alog[:, :DN_HEADS],
        dn_dt_bias=d_dt[:, :DN_HEADS],
        dn_o_gain=d_ogain,
        mlp_norm=jnp.concatenate([gl0["mlp_norm"], gl1["mlp_norm"]], axis=0),
        ple_norm=jnp.concatenate([gl0["ple_norm"], gl1["ple_norm"]], axis=0),
        **early,
    )
    return sq, grad_x, grads, brought_bwd


def _chip_peer(x, y, c, t):
    return (jnp.bitwise_xor(x, t >> 1), jnp.bitwise_xor(y, t & 1), c)


def _place():
    x, y, c = lax.axis_index("x"), lax.axis_index("y"), lax.axis_index("c")
    return x, y, c, 2 * x + y, (x, y, 1 - c)


def _remote(src, dst, send_sem, recv_sem, to):
    return pltpu.make_async_remote_copy(src_ref=src, dst_ref=dst, send_sem=send_sem, recv_sem=recv_sem,
                                        device_id=to, device_id_type=MESH)


def _hbm_call(name, body, ins, out_shape, scratch_shapes):
    any_spec = pl.BlockSpec(memory_space=pl.ANY)
    return pl.pallas_call(body, name=name, out_shape=out_shape, in_specs=[any_spec] * len(ins),
                          out_specs=[any_spec] * len(out_shape), scratch_shapes=scratch_shapes)(*ins)


def _half(n0, which):
    return pl.ds(which * (n0 // 2), n0 // 2)


class _Exchange:
    def __init__(self, ins, out_shape, scratch, start, finish):
        self.ins, self.out_shape, self.scratch, self.start, self.finish = ins, out_shape, scratch, start, finish


def _run_exchange(name, ex):
    n_in, n_out = len(ex.ins), len(ex.out_shape)

    def body(*refs):
        ins, outs, sems = refs[:n_in], refs[n_in:n_in + n_out], refs[n_in + n_out:]
        ex.start(ins, outs, sems)
        ex.finish(ins, outs, sems)

    return _hbm_call(name, body, ex.ins, ex.out_shape, ex.scratch)


def _gather_exchange(shards):
    T = len(shards)

    pairs = [(i, t) for i in range(T) for t in range(1, N_CHIPS)]

    def copies(ins, outs, sems):
        send, recv = sems
        x, y, c, q, sibling = _place()

        def half(i, which):
            return _half(ins[i].shape[0], which)

        def over_ici(i, t):
            return _remote(ins[i].at[half(i, c)], outs[i].at[q, half(i, c)], send.at[i, t - 1], recv.at[i, t - 1],
                           _chip_peer(x, y, c, t))

        def landing(i, t):
            spot = outs[i].at[jnp.bitwise_xor(q, t), half(i, c)]
            return _remote(spot, spot, send.at[i, t - 1], recv.at[i, t - 1], _chip_peer(x, y, c, t))

        def forward(i, t):
            spot = outs[i].at[jnp.bitwise_xor(q, t), half(i, c)]
            return _remote(spot, spot, send.at[i, 2 + t], recv.at[i, 2 + t], sibling)

        def forwarded(i, t):
            spot = outs[i].at[jnp.bitwise_xor(q, t), half(i, 1 - c)]
            return _remote(spot, spot, send.at[i, 2 + t], recv.at[i, 2 + t], sibling)

        return over_ici, landing, forward, forwarded

    def start(ins, outs, sems):
        over_ici = copies(ins, outs, sems)[0]
        for i, t in pairs:
            over_ici(i, t).start()

    def finish(ins, outs, sems):
        over_ici, landing, forward, forwarded = copies(ins, outs, sems)
        for i, t in pairs:
            landing(i, t).wait_recv()
            forward(i, t).start()
        for i, t in pairs:
            forwarded(i, t).wait_recv()
        for i, t in pairs:
            over_ici(i, t).wait_send()
            forward(i, t).wait_send()

    n_rel = 2 * (N_CHIPS - 1)
    return _Exchange(list(shards), [jax.ShapeDtypeStruct((N_CHIPS,) + s.shape, s.dtype) for s in shards],
                     [pltpu.SemaphoreType.DMA((T, n_rel)), pltpu.SemaphoreType.DMA((T, n_rel))], start, finish)


def _scatter_exchange(stacks):
    T = len(stacks)

    def copies(ins, outs, sems):
        send, recv = sems
        x, y, c, q, sibling = _place()
        return [_remote(ins[i].at[jnp.bitwise_xor(q, t)], outs[i].at[t - 1], send.at[i, t - 1], recv.at[i, t - 1],
                        _chip_peer(x, y, c, t)) for i in range(T) for t in range(1, N_CHIPS)]

    def start(ins, outs, sems):
        for cp in copies(ins, outs, sems):
            cp.start()

    def finish(ins, outs, sems):
        for cp in copies(ins, outs, sems):
            cp.wait()

    return _Exchange(list(stacks), [jax.ShapeDtypeStruct((N_CHIPS - 1,) + s.shape[1:], s.dtype) for s in stacks],
                     [pltpu.SemaphoreType.DMA((T, N_CHIPS - 1)), pltpu.SemaphoreType.DMA((T, N_CHIPS - 1))],
                     start, finish)


def _other_half_exchange(stacks):
    T = len(stacks)

    def copies(ins, outs, sems):
        send, recv = sems
        x, y, c, q, sibling = _place()
        return [_remote(ins[i].at[:, _half(ins[i].shape[1], 1 - c)], outs[i], send.at[i], recv.at[i], sibling)
                for i in range(T)]

    def start(ins, outs, sems):
        for cp in copies(ins, outs, sems):
            cp.start()

    def finish(ins, outs, sems):
        for cp in copies(ins, outs, sems):
            cp.wait()

    return _Exchange(list(stacks),
                     [jax.ShapeDtypeStruct((s.shape[0], s.shape[1] // 2) + s.shape[2:], s.dtype) for s in stacks],
                     [pltpu.SemaphoreType.DMA((T,)), pltpu.SemaphoreType.DMA((T,))], start, finish)


def _swap_with_sibling(name, arrays):
    T = len(arrays)

    def body(*refs):
        ins, outs = refs[:T], refs[T:2 * T]
        send, recv = refs[2 * T:]
        x, y, c, q, sibling = _place()
        copies = []
        for i in range(T):
            rc = _remote(ins[i], outs[i], send.at[i], recv.at[i], sibling)
            rc.start()
            copies.append(rc)
        for cp in copies:
            cp.wait()

    return _hbm_call(name, body, arrays, [jax.ShapeDtypeStruct(a.shape, a.dtype) for a in arrays],
                     [pltpu.SemaphoreType.DMA((T,)), pltpu.SemaphoreType.DMA((T,))])


def _gather_from_all(name, block):
    R, C = block.shape

    def body(src, out, send_sems, recv_sems):
        x, y, c = lax.axis_index("x"), lax.axis_index("y"), lax.axis_index("c")
        me = 4 * x + 2 * y + c
        out[me] = src[...]
        copies = []
        for r in range(1, N_DEV):
            peer = (jnp.bitwise_xor(x, r >> 2), jnp.bitwise_xor(y, (r >> 1) & 1), jnp.bitwise_xor(c, r & 1))
            cp = pltpu.make_async_remote_copy(src_ref=src, dst_ref=out.at[me], send_sem=send_sems.at[r - 1],
                                              recv_sem=recv_sems.at[r - 1], device_id=peer, device_id_type=MESH)
            cp.start()
            copies.append(cp)
        for cp in copies:
            cp.wait()

    return pl.pallas_call(
        body, name=name, out_shape=jax.ShapeDtypeStruct((N_DEV, R, C), block.dtype),
        in_specs=[pl.BlockSpec(memory_space=pltpu.VMEM)], out_specs=pl.BlockSpec(memory_space=pltpu.VMEM),
        scratch_shapes=[pltpu.SemaphoreType.DMA((N_DEV - 1,)), pltpu.SemaphoreType.DMA((N_DEV - 1,))],
    )(block)


def _view(a):
    return a[0] if a.shape[0] == 1 else a


def _view_axis(a, axis):
    return axis - 1 if a.shape[0] == 1 else axis


def _rows(a):
    return a.reshape(-1, a.shape[-1])


def _elementwise(name, fn, ins, out_dtypes, tm):
    specs = []
    for a in ins:
        a, row0 = a if isinstance(a, tuple) else (a, 0)
        specs.append((_rows(a), a.shape[-1], 0, row0))
    shape = ins[0][0].shape if isinstance(ins[0], tuple) else ins[0].shape
    outs = _rowwise(name, fn, specs, [], [(shape[-1], dt) for dt in out_dtypes], tm=tm, n_rows=math.prod(shape[:-1]))
    return outs.reshape(shape) if len(out_dtypes) == 1 else [o.reshape(shape) for o in outs]


SMALL_ROWS = 8
CONV_ROWS = CONV_WIDTH * DN_QKV // D_MODEL
SMALL_GRAD_ROWS = 24


def _pack_small(vals, conv=None):
    tail = jnp.concatenate([vals["attn_q_gain"].reshape(-1), vals["attn_k_gain"].reshape(-1),
                            vals["dn_a_log"].reshape(-1), vals["dn_dt_bias"].reshape(-1),
                            vals["dn_o_gain"].reshape(-1)])
    tail = jnp.pad(tail, (0, D_MODEL - tail.shape[0])).reshape(1, D_MODEL)
    rows = [vals["mix_norm"], vals["mlp_norm"], vals["ple_norm"], tail, jnp.zeros((1, D_MODEL), F32)]
    if conv is not None:
        rows += [conv.reshape(CONV_ROWS, D_MODEL),
                 jnp.zeros((SMALL_GRAD_ROWS - SMALL_ROWS - CONV_ROWS, D_MODEL), F32)]
    return jnp.concatenate(rows, axis=0)


def _unpack_small(block):
    nq = 3 * A_HEAD_DIM
    t = block[6]
    return dict(
        mix_norm=block[0:2], mlp_norm=block[2:4], ple_norm=block[4:6],
        attn_q_gain=t[:nq].reshape(1, 3, A_HEAD_DIM), attn_k_gain=t[nq:2 * nq].reshape(1, 3, A_HEAD_DIM),
        dn_a_log=t[2 * nq:2 * nq + DN_HEADS].reshape(1, DN_HEADS),
        dn_dt_bias=t[2 * nq + DN_HEADS:2 * nq + 2 * DN_HEADS].reshape(1, DN_HEADS),
        dn_o_gain=t[2 * nq + 2 * DN_HEADS:2 * nq + 2 * DN_HEADS + DN_HEAD_DIM].reshape(1, DN_HEAD_DIM))


def kernel(x, p, positions, mix_norm, attn_w_qkv, attn_q_gain, attn_k_gain, attn_w_o, dn_w_in, dn_conv, dn_a_log, dn_dt_bias, dn_o_gain, dn_w_o, mlp_norm, w_up, w_down, ple_norm, w_ple, w_ple_gate, loss_target, m_mix_norm, m_attn_w_qkv, m_attn_q_gain, m_attn_k_gain, m_attn_w_o, m_dn_w_in, m_dn_conv, m_dn_a_log, m_dn_dt_bias, m_dn_o_gain, m_dn_w_o, m_mlp_norm, m_w_up, m_w_down, m_ple_norm, m_w_ple, m_w_ple_gate, v_mix_norm, v_attn_w_qkv, v_attn_q_gain, v_attn_k_gain, v_attn_w_o, v_dn_w_in, v_dn_conv, v_dn_a_log, v_dn_dt_bias, v_dn_o_gain, v_dn_w_o, v_mlp_norm, v_w_up, v_w_down, v_ple_norm, v_w_ple, v_w_ple_gate):
    given = dict(locals())
    w = {n: given[n] for n in WEIGHTS}
    m = {n: given["m_" + n] for n in WEIGHTS}
    v = {n: given["v_" + n] for n in WEIGHTS}
    kinds = ("grad", "delta", "new_m", "new_v")
    axes = {n: _view_axis(w[n], axis) for n, axis in SHARDED if n != "dn_conv"}
    chip = 2 * lax.axis_index("x") + lax.axis_index("y")
    core = lax.axis_index("c")
    shards = {n: _view(w[n]).astype(BF16) for n in axes}

    def whole(n, slots):
        return jnp.concatenate([jnp.where(chip == q, shards[n], slots[q]) for q in range(N_CHIPS)], axis=axes[n])

    def stacks_of(grads_of):
        return [jnp.stack(jnp.split(g, N_CHIPS, axis=axes[n])) for n, g in grads_of.items()]

    def chip_sums_of(names, stacks, theirs):
        mine = [lax.dynamic_slice_in_dim(s, core * (s.shape[1] // 2), s.shape[1] // 2, axis=1) for s in stacks]
        return {n: _elementwise(f"add_core_{n}", lambda a, b: a.astype(F32) + b.astype(F32), [a, b], [BF16], 128)
                for n, a, b in zip(names, mine, theirs)}

    class Hooks:
        first_cargo = _gather_exchange([shards[n] for n in ATTN_MATRICES])
        fwd_cargo = [_gather_exchange([shards[n] for n in group]) for group in CARGO_GROUPS]
        chip_sums = {}
        early = None

        def first_weights(self, results):
            return {n: whole(n, slots) for n, slots in zip(ATTN_MATRICES, results)}

        def weights_from(self, results):
            full = {n: whole(n, slots) for group, res in zip(CARGO_GROUPS, results) for n, slots in zip(group, res)}
            w_in, n_ab = full["dn_w_in"], 2 * DN_HEADS
            full["dn_w_in"] = jnp.concatenate([w_in[:, DN_QKV + n_ab:], w_in[:, :DN_QKV + n_ab],
                                               jnp.zeros((D_MODEL, DN_AB_PAD - n_ab), BF16)], axis=1)
            return full

        def split_cargo(self, early_grads):
            self.early = (list(early_grads), stacks_of(early_grads))
            return _other_half_exchange(self.early[1])

        def bwd_cargo(self, results):
            self.chip_sums.update(chip_sums_of(*self.early, results))
            return [_scatter_exchange([self.chip_sums[n] for n in group]) for group in CARGO_GROUPS]

        def last_cargo(self, attn_grads):
            stacks = stacks_of(attn_grads)
            theirs = _run_exchange("split_core_grads_attn", _other_half_exchange(stacks))
            self.chip_sums.update(chip_sums_of(list(attn_grads), stacks, theirs))
            return _scatter_exchange([self.chip_sums[n] for n in attn_grads])

    hooks = Hooks()
    big = {}
    conv_block = jnp.pad(w["dn_conv"].reshape(-1), (0, SMALL_ROWS * D_MODEL - w["dn_conv"].size))
    conv_all = _gather_from_all("gather_conv", conv_block.reshape(SMALL_ROWS, D_MODEL))
    conv_all = conv_all.reshape(N_CHIPS, 2, -1)[:, 0, :w["dn_conv"].size]
    conv_full = jnp.concatenate([conv_all[q].reshape(CONV_WIDTH, -1) for q in range(N_CHIPS)], axis=1)
    small = {n: w[n] for n in REPLICATED}
    small["dn_conv"] = conv_full

    sq, grad_x, grads, brought = _local_step(x[0], p[:, 0], positions[0], loss_target[0], small, big, hooks)
    loss = lax.psum(0.5 * sq[0, 0] / D_MODEL, ("x", "y", "c"))
    out = {}

    landed = {n: r for group, res in zip(CARGO_GROUPS + (ATTN_MATRICES,), brought) for n, r in zip(group, res)}
    half_sums = []
    for n in axes:
        o = lax.dynamic_index_in_dim(hooks.chip_sums[n], chip, axis=0, keepdims=False)
        per = math.prod(o.shape[:-1])
        r = landed[n]
        half_sums.append(_elementwise(
            f"add_chips_{n}", lambda a, b, c, d: ((a.astype(F32) + b.astype(F32)) + c.astype(F32)) + d.astype(F32),
            [o, (r, 0), (r, per), (r, 2 * per)], [F32], 128))
    other_halves = _swap_with_sibling("join_core_sums", half_sums)
    for n, a, b in zip(axes, half_sums, other_halves):
        g = jnp.where(core == 0, jnp.concatenate([a, b], axis=0), jnp.concatenate([b, a], axis=0))
        shp = w[n].shape
        res = _elementwise(f"adamw_{n}", lambda g, w_, m_, v_: (g,) + _adamw(w_, g, m_, v_),
                           [g.reshape(shp), w[n], m[n], v[n]], [F32] * 4, 256)
        for kind, arr in zip(kinds, res):
            out[kind + "_" + n] = arr.reshape(shp)

    slots = _gather_from_all("gather_small_grads", _pack_small(grads, grads["dn_conv"]))

    def small_body(s_ref, w_ref, m_ref, v_ref, sum_out, g_out, d_out, m_out, v_out):
        total = s_ref[0]
        for d in range(1, N_DEV):
            total = total + s_ref[d]
        sum_out[...] = total
        g = total[:SMALL_ROWS]
        for o, r in zip((g_out, d_out, m_out, v_out), (g,) + _adamw(w_ref[...], g, m_ref[...], v_ref[...])):
            o[...] = r

    res = pl.pallas_call(small_body, name="adamw_replicated",
                         out_shape=[jax.ShapeDtypeStruct((SMALL_GRAD_ROWS, D_MODEL), F32)]
                         + [jax.ShapeDtypeStruct((SMALL_ROWS, D_MODEL), F32)] * 4)(
        slots, _pack_small(w), _pack_small(m), _pack_small(v))
    for kind, block in zip(kinds, res[1:]):
        for n, arr in _unpack_small(block).items():
            out[kind + "_" + n] = arr
    conv_sum = res[0][SMALL_ROWS:SMALL_ROWS + CONV_ROWS].reshape(CONV_WIDTH, DN_QKV)
    cols = DN_QKV // N_CHIPS
    chip = 2 * lax.axis_index("x") + lax.axis_index("y")
    conv_mine = lax.dynamic_slice_in_dim(conv_sum, chip * cols, cols, axis=1)
    res = _elementwise("adamw_dn_conv", lambda g, w_, m_, v_: (g,) + _adamw(w_, g, m_, v_),
                       [conv_mine, w["dn_conv"][0], m["dn_conv"][0], v["dn_conv"][0]], [F32] * 4, CONV_WIDTH)
    for kind, arr in zip(kinds, res):
        out[kind + "_dn_conv"] = arr[None]

    return (loss, grad_x[None],
            *[out["grad_" + n] for n in WEIGHTS], *[out["delta_" + n] for n in WEIGHTS],
            *[out["new_m_" + n] for n in WEIGHTS], *[out["new_v_" + n] for n in WEIGHTS])
```

```python
import functools
import math

import jax
import jax.numpy as jnp
from jax import lax
from jax.experimental import pallas as pl
from jax.experimental.pallas import tpu as pltpu

F32 = jnp.float32
BF16 = jnp.bfloat16
HIGHEST = lax.Precision.HIGHEST

D_MODEL = 1024
EPS = 1e-6
SWA_GROUPS = ((128, 1), (512, 4), (2048, 16))
A_HEADS = 8
A_HEAD_DIM = 64
A_WIDTH = A_HEADS * A_HEAD_DIM
ROPE_DIM = A_HEAD_DIM // 4
ROPE_THETA = 500000.0
BAND = 128
DN_HEADS = 8
DN_HEAD_DIM = 128
DN_WIDTH = DN_HEADS * DN_HEAD_DIM
DN_QKV = 3 * DN_WIDTH
DN_AB_PAD = 128
DN_QKV0 = DN_WIDTH
DN_AB0 = DN_WIDTH + DN_QKV
DN_HB = 8
CONV_WIDTH = 4
CHUNK = 64

ADAM_LR = 0.001
ADAM_B1 = 0.9
ADAM_B2 = 0.999
ADAM_EPS = 1e-08
ADAM_WD = 0.01
ADAM_STEP = 10

N_CHIPS = 4
N_DEV = 8
VMEM_LIMIT = 48 * 1024 * 1024
MESH = pl.DeviceIdType.MESH

SHARDED = (
    ("attn_w_qkv", 2), ("attn_w_o", 2), ("dn_w_in", 2), ("dn_conv", 2), ("dn_w_o", 1),
    ("w_up", 2), ("w_down", 1), ("w_ple", 2), ("w_ple_gate", 1))
ATTN_MATRICES = ("attn_w_qkv", "attn_w_o")
CARGO_GROUPS = (("w_up",), ("w_down",), ("dn_w_in", "dn_w_o", "w_ple", "w_ple_gate"))
REPLICATED = ("mix_norm", "attn_q_gain", "attn_k_gain", "dn_a_log", "dn_dt_bias", "dn_o_gain",
              "mlp_norm", "ple_norm")
WEIGHTS = ("mix_norm", "attn_w_qkv", "attn_q_gain", "attn_k_gain", "attn_w_o", "dn_w_in", "dn_conv",
           "dn_a_log", "dn_dt_bias", "dn_o_gain", "dn_w_o", "mlp_norm", "w_up", "w_down", "ple_norm",
           "w_ple", "w_ple_gate")


def _cparams(sem=None):
    return pltpu.CompilerParams(dimension_semantics=sem, vmem_limit_bytes=VMEM_LIMIT)


def _pick(n, cap, quantum=128):
    best = None
    for t in range(quantum, min(n, cap) + 1, quantum):
        if n % t == 0:
            best = t
    return n if best is None else best


_DIMS = {"nn": ((1,), (0,)), "nt": ((1,), (1,)), "tn": ((0,), (0,))}


def _mm(name, a, b, mode, out_dtypes=(F32,), extras=(), epilogue=None, rows=(), n_sums=0, tm_cap=1024):
    if mode == "nn":
        (M, K), (K2, N) = a.shape, b.shape
    elif mode == "nt":
        (M, K), (N, K2) = a.shape, b.shape
    else:
        (K, M), (K2, N) = a.shape, b.shape
    assert K == K2, (name, a.shape, b.shape)
    tn = _pick(N, 1536)
    if mode == "tn":
        tm, tk = _pick(M, tm_cap), _pick(K, 2048)
    elif tn == N and K > 1536:
        tm, tk = _pick(M, min(tm_cap, 512)), K
    else:
        tm, tk = _pick(M, tm_cap), _pick(K, 1536)
    nk = K // tk
    assert n_sums == 0 or tn == N, name
    if mode == "nn":
        a_spec = pl.BlockSpec((tm, tk), lambda i, j, k: (i, k))
        b_spec = pl.BlockSpec((tk, tn), lambda i, j, k: (k, j))
    elif mode == "nt":
        a_spec = pl.BlockSpec((tm, tk), lambda i, j, k: (i, k))
        b_spec = pl.BlockSpec((tn, tk), lambda i, j, k: (j, k))
    else:
        a_spec = pl.BlockSpec((tk, tm), lambda i, j, k: (k, i))
        b_spec = pl.BlockSpec((tk, tn), lambda i, j, k: (k, j))
    o_spec = pl.BlockSpec((tm, tn), lambda i, j, k: (i, j))
    r_spec = pl.BlockSpec((1, tn), lambda i, j, k: (0, j))
    n_extra, n_out = len(extras) + len(rows), len(out_dtypes)
    dims = (_DIMS[mode], ((), ()))

    def body(a_ref, b_ref, *rest):
        extra_refs, out_refs = rest[:n_extra], rest[n_extra:n_extra + n_out]
        sum_refs = rest[n_extra + n_out:n_extra + n_out + n_sums]
        i, k = pl.program_id(0), pl.program_id(2)
        part = lax.dot_general(a_ref[...].astype(BF16), b_ref[...].astype(BF16), dims, preferred_element_type=F32)

        def finish(total):
            vals = (total,) if epilogue is None else epilogue(total, *[e[...] for e in extra_refs])
            for o, v in zip(out_refs, vals[:n_out]):
                o[...] = v.astype(o.dtype)
            for s, v in zip(sum_refs, vals[n_out:]):
                @pl.when(i == 0)
                def _():
                    s[...] = v

                @pl.when(i > 0)
                def _():
                    s[...] += v

        if nk == 1:
            finish(part)
            return
        acc = rest[-1]

        @pl.when(k == 0)
        def _():
            acc[...] = part

        @pl.when(jnp.logical_and(k > 0, k < nk - 1))
        def _():
            acc[...] += part

        @pl.when(k == nk - 1)
        def _():
            finish(acc[...] + part)

    outs = pl.pallas_call(
        body, name=name, grid=(M // tm, N // tn, nk),
        in_specs=[a_spec, b_spec] + [o_spec] * len(extras) + [r_spec] * len(rows),
        out_specs=[o_spec] * n_out + [r_spec] * n_sums,
        out_shape=[jax.ShapeDtypeStruct((M, N), dt) for dt in out_dtypes]
        + [jax.ShapeDtypeStruct((1, N), F32)] * n_sums,
        scratch_shapes=[pltpu.VMEM((tm, tn), F32)] if nk > 1 else [],
        compiler_params=_cparams(("arbitrary" if n_sums else "parallel", "parallel", "arbitrary")),
    )(a, b, *extras, *rows)
    return outs[0] if n_out + n_sums == 1 else outs


def _rowwise(name, fn, rows, bcast, row_outs, acc_outs=(), tm=512, n_rows=None, cargo=None):
    rows = [r if isinstance(r, tuple) else (r, r.shape[1], 0) for r in rows]
    rows = [r if len(r) == 4 else r + (0,) for r in rows]
    S = rows[0][0].shape[0] if n_rows is None else n_rows
    tm = min(tm, S)
    assert S % tm == 0 and all(r[3] % tm == 0 for r in rows), (name, S, tm)
    n_row, n_bc, n_ro, n_acc = len(rows), len(bcast), len(row_outs), len(acc_outs)
    in_specs = [pl.BlockSpec((tm, w), functools.partial(lambda i, cb, rb: (i + rb, cb), cb=cb, rb=r0 // tm))
                for _, w, cb, r0 in rows]
    in_specs += [pl.BlockSpec(b.shape, lambda i: (0, 0)) for b in bcast]
    out_specs = [pl.BlockSpec((tm, c), lambda i: (i, 0)) for c, _ in row_outs]
    out_specs += [pl.BlockSpec(s, lambda i: (0, 0)) for s in acc_outs]
    out_shape = [jax.ShapeDtypeStruct((S, c), dt) for c, dt in row_outs]
    out_shape += [jax.ShapeDtypeStruct(s, F32) for s in acc_outs]

    def body(*refs):
        ins = [r[...] for r in refs[:n_row + n_bc]]
        outs = refs[n_row + n_bc:]
        vals = fn(*ins)
        if not isinstance(vals, (tuple, list)):
            vals = (vals,)
        for o, v in zip(outs[:n_ro], vals[:n_ro]):
            o[...] = v.astype(o.dtype)
        if n_acc:
            @pl.when(pl.program_id(0) == 0)
            def _():
                for o in outs[n_ro:]:
                    o[...] = jnp.zeros_like(o)
            for o, v in zip(outs[n_ro:], vals[n_ro:]):
                o[...] += v

    n_own = n_ro + n_acc
    body, c_in_specs, c_out_specs, c_out_shape, c_scratch, c_ins = _carry(cargo, n_row + n_bc, n_own, 0, body, S // tm)
    outs = pl.pallas_call(
        body, name=name, grid=(S // tm,), in_specs=in_specs + c_in_specs, out_specs=out_specs + c_out_specs,
        out_shape=out_shape + c_out_shape, scratch_shapes=c_scratch,
        compiler_params=_cparams(("arbitrary",) if n_acc or cargo is not None else ("parallel",)),
    )(*[r[0] for r in rows], *bcast, *c_ins)
    own = outs[0] if n_own == 1 else outs[:n_own]
    return own if cargo is None else (own, outs[n_own:])


def _sigmoid(x):
    return 1.0 / (1.0 + jnp.exp(-x))


def _silu(x):
    return x * _sigmoid(x)


def _softplus(x):
    return jnp.maximum(x, 0.0) + jnp.log(1.0 + jnp.exp(-jnp.abs(x)))


def _rms_fwd_fn(x, g):
    r = lax.rsqrt(jnp.mean(x * x, axis=-1, keepdims=True) + EPS)
    return (x * r) * g


def _rms_bwd_fn(x, dres, *rest):
    dh, g = sum(rest[:-1]), rest[-1]
    r = lax.rsqrt(jnp.mean(x * x, axis=-1, keepdims=True) + EPS)
    xh = x * r
    dxh = dh * g
    dx = dres + r * (dxh - xh * jnp.mean(dxh * xh, axis=-1, keepdims=True))
    return dx, dx, jnp.sum(dh * xh, axis=0, keepdims=True)


def _rms_fwd(name, x, gain, cargo=None):
    return _rowwise(name, _rms_fwd_fn, [x], [gain.reshape(1, -1)], [(x.shape[1], BF16)], cargo=cargo)


def _rms_bwd(name, x, gain, dres, dhs, cargo=None):
    return _rowwise(name, lambda *a: _rms_bwd_fn(*a)[1:], [x, dres] + list(dhs), [gain.reshape(1, -1)],
                    [(x.shape[1], F32)], [(1, x.shape[1])], cargo=cargo)


def _relu2_epilogue(acc):
    r = jnp.maximum(acc, 0.0)
    return (r * r,)


def _relu2_bwd_epilogue(acc, a):
    return (acc * (2.0 * jnp.sqrt(a.astype(F32))),)


def _ple_fwd_fn(x, pp, zg):
    return x + pp * _sigmoid(zg)


def _ple_norm_fwd_fn(x, pp, zg, gain):
    out = _ple_fwd_fn(x, pp, zg)
    return out, _rms_fwd_fn(out, gain)


def _ple_bwd_fn(dx, pp, zg):
    gate = _sigmoid(zg)
    return dx * gate, dx * pp * gate * (1.0 - gate)


def _adamw(w, g, m, v):
    m = ADAM_B1 * m + (1.0 - ADAM_B1) * g
    v = ADAM_B2 * v + (1.0 - ADAM_B2) * jnp.square(g)
    m_hat = m / (1.0 - ADAM_B1 ** ADAM_STEP)
    v_hat = v / (1.0 - ADAM_B2 ** ADAM_STEP)
    delta = -ADAM_LR * (m_hat / (jnp.sqrt(v_hat) + ADAM_EPS) + ADAM_WD * w)
    return delta, m, v


def _lane_take(x, offset):
    n = x.shape[-1]
    return pltpu.roll(x, (-offset) % n, 1)


def _head_lane(shape):
    return lax.broadcasted_iota(jnp.int32, shape, 1) % A_HEAD_DIM


def _rope_partner(x):
    lane = _head_lane(x.shape)
    return jnp.where(lane < ROPE_DIM // 2, _lane_take(x, ROPE_DIM // 2),
                     jnp.where(lane < ROPE_DIM, _lane_take(x, -(ROPE_DIM // 2)), 0.0))


def _head_mean(x, bd):
    hi = x.astype(BF16)
    lo = (x - hi.astype(F32)).astype(BF16)
    b = bd.astype(BF16)
    return jnp.dot(hi, b, preferred_element_type=F32) + jnp.dot(lo, b, preferred_element_type=F32)


def _fold_heads(row):
    out = row[:, :A_HEAD_DIM]
    for h in range(1, A_HEADS):
        out = out + row[:, h * A_HEAD_DIM:(h + 1) * A_HEAD_DIM]
    return out


def _all_heads(t):
    return jnp.concatenate([t] * (A_WIDTH // t.shape[1]), axis=1)


def _qk_prep_fwd_fn(qkv, ct, st, gq, gk, bd):
    ct, st = _all_heads(ct), _all_heads(st)

    def one(t, g):
        n = t * lax.rsqrt(_head_mean(t * t, bd) + EPS) * g
        return n * ct + _rope_partner(n) * st
    q, k, v = qkv[:, :A_WIDTH], qkv[:, A_WIDTH:2 * A_WIDTH], qkv[:, 2 * A_WIDTH:]
    return one(q, gq), one(k, gk), v


def _qk_prep_bwd_fn(qkv, ct, st, dq, dk, dv, gq, gk, bd):
    ct, st = _all_heads(ct), _all_heads(st)

    def one(t, g, dy):
        r = lax.rsqrt(_head_mean(t * t, bd) + EPS)
        nh = t * r
        dn = dy * ct + _rope_partner(dy * st)
        dg = jnp.sum(dn * nh, axis=0, keepdims=True)
        dnh = dn * g
        return r * (dnh - nh * _head_mean(dnh * nh, bd)), _fold_heads(dg)
    q, k = qkv[:, :A_WIDTH], qkv[:, A_WIDTH:2 * A_WIDTH]
    dq_raw, dgq = one(q, gq, dq)
    dk_raw, dgk = one(k, gk, dk)
    return jnp.concatenate([dq_raw, dk_raw, dv], axis=1), dgq, dgk


_BATCH_DIMS = {"nn": ((2,), (1,)), "nt": ((2,), (2,)), "tn": ((1,), (1,))}


def _bdot(a, b, mode, precision=None):
    return lax.dot_general(a, b, (_BATCH_DIMS[mode], ((0,), (0,))), precision=precision,
                           preferred_element_type=F32)


def _attn_cols(h):
    return slice(h * A_HEAD_DIM, (h + 1) * A_HEAD_DIM)


def _attn_heads(ref):
    return jnp.stack([ref[:, _attn_cols(h)] for h in range(A_HEADS)])


def _band_masks():
    qi = lax.broadcasted_iota(jnp.int32, (BAND, BAND), 0)
    kj = lax.broadcasted_iota(jnp.int32, (BAND, BAND), 1)
    return kj <= qi, kj >= qi


def _attn_fwd(name, q, k, v, blocks_per_class, cargo=None):
    S = q.shape[0]
    nblk = S // BAND
    scale = A_HEAD_DIM ** -0.5

    def body(q_ref, kp_ref, kc_ref, vp_ref, vc_ref, o_ref, l_ref):
        i = pl.program_id(0)
        has_prev = (i % blocks_per_class) != 0
        m_cur, m_prev = _band_masks()
        m_prev = jnp.logical_and(m_prev, has_prev)
        q, kc, kp, vc, vp = (_attn_heads(r) for r in (q_ref, kc_ref, kp_ref, vc_ref, vp_ref))
        s_c = jnp.where(m_cur[None], _bdot(q, kc, "nt") * scale, -jnp.inf)
        s_p = jnp.where(m_prev[None], _bdot(q, kp, "nt") * scale, -jnp.inf)
        m = jnp.maximum(jnp.max(s_c, axis=-1, keepdims=True), jnp.max(s_p, axis=-1, keepdims=True))
        e_c, e_p = jnp.exp(s_c - m), jnp.exp(s_p - m)
        l = jnp.sum(e_c, axis=-1, keepdims=True) + jnp.sum(e_p, axis=-1, keepdims=True)
        o = _bdot((e_c / l).astype(BF16), vc, "nn") + _bdot((e_p / l).astype(BF16), vp, "nn")
        lse = m + jnp.log(l)
        for h in range(A_HEADS):
            o_ref[:, _attn_cols(h)] = o[h]
            l_ref[:, _attn_cols(h)] = jnp.broadcast_to(lse[h], (BAND, A_HEAD_DIM))

    cur = pl.BlockSpec((BAND, A_WIDTH), lambda i: (i, 0))
    prev = pl.BlockSpec((BAND, A_WIDTH), lambda i: (jnp.maximum(i - 1, 0), 0))
    body, c_in_specs, c_out_specs, c_out_shape, c_scratch, c_ins = _carry(cargo, 5, 2, 0, body, nblk)
    outs = pl.pallas_call(
        body, name=name, grid=(nblk,), in_specs=[cur, prev, cur, prev, cur] + c_in_specs,
        out_specs=[cur, cur] + c_out_specs,
        out_shape=[jax.ShapeDtypeStruct((S, A_WIDTH), F32)] * 2 + c_out_shape, scratch_shapes=c_scratch,
        compiler_params=_cparams(("arbitrary",)),
    )(q, k, k, v, v, *c_ins)
    return outs[0], outs[1], outs[2:]


def _carry(cargo, n_in, n_out, n_scratch, body, steps):
    if cargo is None:
        return body, [], [], [], [], []
    n_ci, n_co = len(cargo.ins), len(cargo.out_shape)

    def carrying(*refs):
        refs = list(refs)
        ins, refs = refs[:n_in], refs[n_in:]
        c_ins, refs = refs[:n_ci], refs[n_ci:]
        outs, refs = refs[:n_out], refs[n_out:]
        c_outs, refs = refs[:n_co], refs[n_co:]
        scratch, sems = refs[:n_scratch], refs[n_scratch:]

        @pl.when(pl.program_id(0) == 0)
        def _():
            cargo.start(c_ins, c_outs, sems)

        body(*ins, *outs, *scratch)

        @pl.when(pl.program_id(0) == steps - 1)
        def _():
            cargo.finish(c_ins, c_outs, sems)

    any_spec = pl.BlockSpec(memory_space=pl.ANY)
    return carrying, [any_spec] * n_ci, [any_spec] * n_co, list(cargo.out_shape), list(cargo.scratch), list(cargo.ins)


def _attn_bwd(name, q, k, v, o, lse, do, dlse, blocks_per_class, cargo=None):
    S = q.shape[0]
    nblk = S // BAND
    scale = A_HEAD_DIM ** -0.5

    def body(q_ref, kp_ref, kc_ref, vp_ref, vc_ref, o_ref, l_ref, do_ref, dl_ref,
             dq_ref, dk_ref, dv_ref, ck, cv):
        i = pl.program_id(0)

        @pl.when(i == 0)
        def _():
            ck[...] = jnp.zeros_like(ck)
            cv[...] = jnp.zeros_like(cv)

        @pl.when(i == nblk)
        def _():
            dk_ref[...] = ck[...]
            dv_ref[...] = cv[...]

        @pl.when(i < nblk)
        def _():
            has_prev = (i % blocks_per_class) != 0
            m_cur, m_prev = _band_masks()
            m_prev = jnp.logical_and(m_prev, has_prev)
            q, kc, kp, vc, vp = (_attn_heads(r) for r in (q_ref, kc_ref, kp_ref, vc_ref, vp_ref))
            do, o, dl = _attn_heads(do_ref), _attn_heads(o_ref), _attn_heads(dl_ref)
            lse = jnp.max(_attn_heads(l_ref), axis=-1, keepdims=True)
            p_c = jnp.where(m_cur[None], jnp.exp(_bdot(q, kc, "nt") * scale - lse), 0.0)
            p_p = jnp.where(m_prev[None], jnp.exp(_bdot(q, kp, "nt") * scale - lse), 0.0)
            corr = jnp.sum(dl, axis=-1, keepdims=True) - jnp.sum(do * o, axis=-1, keepdims=True)
            dob = do.astype(BF16)
            ds_c = (p_c * (_bdot(dob, vc, "nt") + corr)).astype(BF16)
            ds_p = (p_p * (_bdot(dob, vp, "nt") + corr)).astype(BF16)
            dq = (_bdot(ds_c, kc, "nn") + _bdot(ds_p, kp, "nn")) * scale
            dk_p, dk_c = _bdot(ds_p, q, "tn") * scale, _bdot(ds_c, q, "tn") * scale
            dv_p, dv_c = _bdot(p_p.astype(BF16), dob, "tn"), _bdot(p_c.astype(BF16), dob, "tn")
            for h in range(A_HEADS):
                sl = _attn_cols(h)
                dq_ref[:, sl] = dq[h]
                dk_ref[:, sl] = ck[:, sl] + dk_p[h]
                dv_ref[:, sl] = cv[:, sl] + dv_p[h]
                ck[:, sl] = dk_c[h]
                cv[:, sl] = dv_c[h]

    last = nblk - 1
    cur = pl.BlockSpec((BAND, A_WIDTH), lambda i: (jnp.minimum(i, last), 0))
    prev = pl.BlockSpec((BAND, A_WIDTH), lambda i: (jnp.minimum(jnp.maximum(i - 1, 0), last), 0))
    body, c_in_specs, c_out_specs, c_out_shape, c_scratch, c_ins = _carry(cargo, 9, 3, 2, body, nblk + 1)
    outs = pl.pallas_call(
        body, name=name, grid=(nblk + 1,),
        in_specs=[cur, prev, cur, prev, cur, cur, cur, cur, cur] + c_in_specs,
        out_specs=[cur, prev, prev] + c_out_specs,
        out_shape=[jax.ShapeDtypeStruct((S, A_WIDTH), F32)] * 3 + c_out_shape,
        scratch_shapes=[pltpu.VMEM((BAND, A_WIDTH), F32)] * 2 + c_scratch,
        compiler_params=_cparams(("arbitrary",)),
    )(q, k, k, v, v, o, lse, do, dlse, *c_ins)
    return outs[0], outs[1], outs[2], outs[3:]


def _merge_fwd_fn(o0, o1, o2, l0, l1, l2):
    m = jnp.maximum(jnp.maximum(l0, l1), l2)
    e0, e1, e2 = jnp.exp(l0 - m), jnp.exp(l1 - m), jnp.exp(l2 - m)
    return (e0 * o0 + e1 * o1 + e2 * o2) / (e0 + e1 + e2)


def _merge_bwd_fn(o0, o1, o2, l0, l1, l2, dom):
    m = jnp.maximum(jnp.maximum(l0, l1), l2)
    e0, e1, e2 = jnp.exp(l0 - m), jnp.exp(l1 - m), jnp.exp(l2 - m)
    den = e0 + e1 + e2
    w0, w1, w2 = e0 / den, e1 / den, e2 / den
    dw0, dw1, dw2 = dom * o0, dom * o1, dom * o2
    mean = w0 * dw0 + w1 * dw1 + w2 * dw2
    return w0 * dom, w1 * dom, w2 * dom, w0 * (dw0 - mean), w1 * (dw1 - mean), w2 * (dw2 - mean)


def _to_classes(t, d):
    if d == 1:
        return t
    S, C = t.shape
    return t.reshape(S // d, d, C).transpose(1, 0, 2).reshape(S, C)


def _from_classes(t, d):
    if d == 1:
        return t
    S, C = t.shape
    return t.reshape(d, S // d, C).transpose(1, 0, 2).reshape(S, C)


def _rope_lane_tables(positions):
    inv_freq = ROPE_THETA ** (-jnp.arange(0, ROPE_DIM, 2, dtype=F32) / ROPE_DIM)
    ang = positions.astype(F32)[:, None] * inv_freq
    cos, sin = jnp.cos(ang), jnp.sin(ang)
    S = positions.shape[0]
    rest = A_HEAD_DIM - ROPE_DIM
    ct = jnp.concatenate([cos, cos, jnp.ones((S, rest), F32)], axis=1)
    st = jnp.concatenate([-sin, sin, jnp.zeros((S, rest), F32)], axis=1)
    return jnp.tile(ct, (1, 2)), jnp.tile(st, (1, 2))


def _head_mean_matrix():
    r = jnp.arange(A_WIDTH) // A_HEAD_DIM
    return (r[:, None] == r[None, :]).astype(F32) * (1.0 / A_HEAD_DIM)


CONV_LANES = 1024
CONV_BWD_LANES = 512
PAST = CONV_WIDTH - 1


def _strip_starts(tm, strip):
    return range(0, tm, strip)


def _conv_fwd(name, proj, w):
    S = proj.shape[0]
    tm, tc = min(512, S), CONV_LANES
    per8 = tm // 8
    off = DN_QKV0 // tc

    def body(x_ref, halo_ref, w_ref, o_ref, xs):
        i = pl.program_id(0)
        xs[0:8, :] = jnp.where(i > 0, halo_ref[...], 0.0)
        xs[8:, :] = x_ref[...]
        strip = 16
        for r0 in _strip_starts(tm, strip):
            acc = w_ref[PAST:CONV_WIDTH, :] * x_ref[pl.ds(r0, strip), :]
            for j in range(PAST):
                acc = acc + w_ref[j:j + 1, :] * xs[pl.ds(8 - PAST + j + r0, strip), :]
            o_ref[pl.ds(r0, strip), :] = acc

    return pl.pallas_call(
        body, name=name, grid=(S // tm, DN_QKV // tc),
        in_specs=[pl.BlockSpec((tm, tc), lambda i, j: (i, j + off)),
                  pl.BlockSpec((8, tc), lambda i, j: (jnp.maximum(i * per8 - 1, 0), j + off)),
                  pl.BlockSpec((CONV_WIDTH, tc), lambda i, j: (0, j))],
        out_specs=pl.BlockSpec((tm, tc), lambda i, j: (i, j)),
        out_shape=jax.ShapeDtypeStruct((S, DN_QKV), F32),
        scratch_shapes=[pltpu.VMEM((tm + 8, tc), F32)],
        compiler_params=_cparams(("parallel", "parallel")),
    )(proj, proj, w)


def _conv_bwd(name, proj, dpre, w):
    S = proj.shape[0]
    tm, tc = min(512, S), CONV_BWD_LANES
    per8 = tm // 8
    off = DN_QKV0 // tc
    last8 = S // 8 - 1
    nrow = S // tm

    def body(x_ref, xh_ref, d_ref, dh_ref, w_ref, dx_ref, dw_ref, xs, ds):
        i = pl.program_id(1)
        xs[0:8, :] = jnp.where(i > 0, xh_ref[...], 0.0)
        xs[8:, :] = x_ref[...]
        ds[0:tm, :] = d_ref[...]
        ds[tm:, :] = jnp.where(i < nrow - 1, dh_ref[...], 0.0)
        strip = 16
        sums = [jnp.zeros((8, tc), F32)] * CONV_WIDTH
        for r0 in _strip_starts(tm, strip):
            rows = pl.ds(r0, strip)
            d = d_ref[rows, :]
            acc = w_ref[PAST:CONV_WIDTH, :] * d
            for j in range(PAST):
                acc = acc + w_ref[j:j + 1, :] * ds[pl.ds(r0 + PAST - j, strip), :]
            dx_ref[rows, :] = acc.astype(dx_ref.dtype)
            taps = [xs[pl.ds(8 - PAST + j + r0, strip), :] for j in range(PAST)] + [x_ref[rows, :]]
            for j, tap in enumerate(taps):
                prod = d * tap
                sums[j] = sums[j] + (prod[0:8] + prod[8:16])

        @pl.when(i == 0)
        def _():
            dw_ref[...] = jnp.zeros_like(dw_ref)

        for j in range(CONV_WIDTH):
            dw_ref[j:j + 1, :] += jnp.sum(sums[j], axis=0, keepdims=True)

    return pl.pallas_call(
        body, name=name, grid=(DN_QKV // tc, nrow),
        in_specs=[pl.BlockSpec((tm, tc), lambda j, i: (i, j + off)),
                  pl.BlockSpec((8, tc), lambda j, i: (jnp.maximum(i * per8 - 1, 0), j + off)),
                  pl.BlockSpec((tm, tc), lambda j, i: (i, j)),
                  pl.BlockSpec((8, tc), lambda j, i: (jnp.minimum((i + 1) * per8, last8), j)),
                  pl.BlockSpec((CONV_WIDTH, tc), lambda j, i: (0, j))],
        out_specs=[pl.BlockSpec((tm, tc), lambda j, i: (i, j)),
                   pl.BlockSpec((CONV_WIDTH, tc), lambda j, i: (0, j))],
        out_shape=[jax.ShapeDtypeStruct((S, DN_QKV), BF16), jax.ShapeDtypeStruct((CONV_WIDTH, DN_QKV), F32)],
        scratch_shapes=[pltpu.VMEM((tm + 8, tc), F32)] * 2,
        compiler_params=_cparams(("parallel", "arbitrary")),
    )(proj, proj, dpre, dpre, w)


def _gate_lane(shape):
    return lax.broadcasted_iota(jnp.int32, shape, 1)


GATES_ROWS = 256


def _chunk_cumsum_matrix():
    r = jnp.arange(GATES_ROWS)
    return ((r[:, None] >= r[None, :]) & (r[:, None] // CHUNK == r[None, :] // CHUNK)).astype(F32)


def _gates_fwd_fn(ab, alog, dt, cum):
    g = -jnp.exp(alog) * _softplus(ab + dt)
    gc = jnp.dot(cum, g, precision=HIGHEST, preferred_element_type=F32)
    return jnp.where(_gate_lane(ab.shape) < DN_HEADS, gc, _sigmoid(ab))


def _gates_bwd_fn(ab, dgb, alog, dt, cum):
    lane = _gate_lane(ab.shape)
    is_g = lane < DN_HEADS
    neg_a = -jnp.exp(alog)
    sp = _softplus(ab + dt)
    dsp = _sigmoid(ab + dt)
    beta = _sigmoid(ab)
    dgc = jnp.where(is_g, dgb, 0.0)
    dg = lax.dot_general(cum, dgc, (_DIMS["tn"], ((), ())), precision=HIGHEST, preferred_element_type=F32)
    dab = jnp.where(is_g, dg * neg_a * dsp, jnp.where(lane < 2 * DN_HEADS, dgb * beta * (1.0 - beta), 0.0))
    d_alog = jnp.sum(dg * neg_a * sp, axis=0, keepdims=True)
    d_dt = jnp.sum(dg * neg_a * dsp, axis=0, keepdims=True)
    return dab, d_alog, d_dt


def _chunk_math(precision):
    def dg(a, b, mode, prec=precision):
        return _bdot(a, b, mode, prec)

    @jax.custom_vjp
    def nn(a, b):
        return dg(a, b, "nn")

    @jax.custom_vjp
    def nt(a, b):
        return dg(a, b, "nt")

    @jax.custom_vjp
    def tn(a, b):
        return dg(a, b, "tn")

    nn.defvjp(lambda a, b: (nn(a, b), (a, b)), lambda r, g: (nt(g, r[1]), tn(r[0], g)))
    nt.defvjp(lambda a, b: (nt(a, b), (a, b)), lambda r, g: (nn(g, r[1]), tn(g, r[0])))
    tn.defvjp(lambda a, b: (tn(a, b), (a, b)), lambda r, g: (nt(r[1], g), nn(r[0], g)))

    def split(x):
        hi = x.astype(BF16)
        return hi, (x - hi.astype(F32)).astype(BF16)

    def fine(a, b, mode):
        ah, al = split(a)
        bh, bl = split(b)
        return dg(ah, bh, mode, None) + (dg(ah, bl, mode, None) + dg(al, bh, mode, None))

    def unit_lower_inverse(a):
        row = lax.broadcasted_iota(jnp.int32, a.shape, 1)
        col = lax.broadcasted_iota(jnp.int32, a.shape, 2)
        x = -a
        p = jnp.where(row == col, 1.0, 0.0) + x
        for _ in range(int(math.log2(CHUNK)) - 1):
            x = fine(x, x, "nn")
            p = p + fine(p, x, "nn")
        return p

    @jax.custom_vjp
    def solve2(a, ti, r1, r2):
        return fine(ti, r1, "nn"), fine(ti, r2, "nn")

    def solve2_fwd(a, ti, r1, r2):
        s1, s2 = fine(ti, r1, "nn"), fine(ti, r2, "nn")
        return (s1, s2), (ti, s1, s2)

    def solve2_bwd(res, g):
        ti, s1, s2 = res
        d1, d2 = fine(ti, g[0], "tn"), fine(ti, g[1], "tn")
        return -(fine(d1, s1, "nt") + fine(d2, s2, "nt")), jnp.zeros_like(ti), d1, d2

    solve2.defvjp(solve2_fwd, solve2_bwd)

    def chunk_fn(pq, pk, pv, z, g_col, b_col, g_row, ogain, s_in, inverse=None):
        nb = pq.shape[0]
        sq = (nb, CHUNK, CHUNK)
        row = lax.broadcasted_iota(jnp.int32, sq, 1)
        col = lax.broadcasted_iota(jnp.int32, sq, 2)
        lower, strict = row >= col, row > col
        q, k, v = _silu(pq), _silu(pk), _silu(pv)
        q = q * lax.rsqrt(jnp.sum(q * q, axis=-1, keepdims=True) + EPS) * (DN_HEAD_DIM ** -0.5)
        k = k * lax.rsqrt(jnp.sum(k * k, axis=-1, keepdims=True) + EPS)
        gc_wide = jnp.broadcast_to(g_col, pq.shape)
        gc_i = jnp.broadcast_to(g_col, sq)
        gc_j = jnp.broadcast_to(g_row, sq)
        is_last = lax.broadcasted_iota(jnp.int32, pq.shape, 1) == CHUNK - 1
        g_last = jnp.sum(jnp.where(is_last, gc_wide, 0.0), axis=1, keepdims=True)
        decay = jnp.exp(jnp.where(lower, gc_i - gc_j, -jnp.inf))
        kb = k * b_col
        a_mat = jnp.where(strict, nt(kb, k) * decay, 0.0)
        eg = jnp.exp(gc_wide)
        ti = unit_lower_inverse(a_mat) if inverse is None else inverse
        u, w = solve2(a_mat, ti, v * b_col, kb * eg)
        attn = nt(q, k) * decay
        q_dec = q * eg
        k_dec = k * jnp.exp(g_last - gc_wide)
        c_dec = jnp.exp(g_last)
        v_new = u - nn(w, s_in)
        o = nn(q_dec, s_in) + nn(attn, v_new)
        s_out = s_in * c_dec + tn(k_dec, v_new)
        y = o * lax.rsqrt(jnp.mean(o * o, axis=-1, keepdims=True) + EPS) * ogain * _silu(z)
        return (y, s_out, ti) if inverse is None else (y, s_out)

    return chunk_fn


DN_PRECISION = None


def _chunk_specs(n_of):
    groups = DN_HEADS // DN_HB
    wide = DN_HB * DN_HEAD_DIM
    hd = pl.BlockSpec((CHUNK, wide), lambda h, n: (n_of(n), h))
    specs = dict(
        pq=hd,
        pk=pl.BlockSpec((CHUNK, wide), lambda h, n: (n_of(n), groups + h)),
        pv=pl.BlockSpec((CHUNK, wide), lambda h, n: (n_of(n), 2 * groups + h)),
        z=hd,
        gates=pl.BlockSpec((CHUNK, DN_AB_PAD), lambda h, n: (n_of(n), 0)),
        row=pl.BlockSpec((DN_HB, None, 1, CHUNK), lambda h, n: (h, n_of(n), 0, 0)),
        gain=pl.BlockSpec((1, DN_HEAD_DIM), lambda h, n: (0, 0)),
        state=pl.BlockSpec((DN_HB, None, DN_HEAD_DIM, DN_HEAD_DIM), lambda h, n: (h, n_of(n), 0, 0)),
        inverse=pl.BlockSpec((DN_HB, None, CHUNK, CHUNK), lambda h, n: (h, n_of(n), 0, 0)),
        qkv=pl.BlockSpec((CHUNK, DN_QKV), lambda h, n: (n_of(n), 0)),
        head=hd,
    )
    return specs


def _head_cols(j):
    return slice(j * DN_HEAD_DIM, (j + 1) * DN_HEAD_DIM)


def _split_heads(ref):
    return jnp.stack([ref[:, _head_cols(j)] for j in range(DN_HB)])


def _gate_columns(gates, first_lane):
    lane = lax.broadcasted_iota(jnp.int32, gates.shape, 1)
    return jnp.stack([jnp.sum(jnp.where(lane == first_lane + h, gates, 0.0), axis=-1, keepdims=True)
                      for h in range(DN_HEADS)])


def _gate_lanes(columns, first_lane):
    shape = (columns.shape[1], DN_AB_PAD)
    lane = lax.broadcasted_iota(jnp.int32, shape, 1)
    out = jnp.zeros(shape, F32)
    for h in range(DN_HEADS):
        out = out + jnp.where(lane == first_lane + h, columns[h], 0.0)
    return out


def _chunk_fwd(name, pre, proj, gates, g_row, ogain):
    assert DN_HB == DN_HEADS
    S = pre.shape[0]
    N = S // CHUNK
    chunk_fn = _chunk_math(DN_PRECISION)
    sp = _chunk_specs(lambda n: n)

    def body(pq, pk, pv, z, gb, gr, og, y_ref, sin_ref, inv_ref, st):
        @pl.when(pl.program_id(1) == 0)
        def _():
            st[...] = jnp.zeros_like(st)

        s_in = st[...]
        sin_ref[...] = s_in
        y, s_out, inverse = chunk_fn(_split_heads(pq), _split_heads(pk), _split_heads(pv), _split_heads(z),
                                     _gate_columns(gb[...], 0), _gate_columns(gb[...], DN_HEADS), gr[...],
                                     og[...], s_in)
        for j in range(DN_HB):
            y_ref[:, _head_cols(j)] = y[j].astype(y_ref.dtype)
        inv_ref[...] = inverse
        st[...] = s_out

    return pl.pallas_call(
        body, name=name, grid=(DN_HEADS // DN_HB, N),
        in_specs=[sp["pq"], sp["pk"], sp["pv"], sp["z"], sp["gates"], sp["row"], sp["gain"]],
        out_specs=[sp["head"], sp["state"], sp["inverse"]],
        out_shape=[jax.ShapeDtypeStruct((S, DN_WIDTH), BF16),
                   jax.ShapeDtypeStruct((DN_HEADS, N, DN_HEAD_DIM, DN_HEAD_DIM), F32),
                   jax.ShapeDtypeStruct((DN_HEADS, N, CHUNK, CHUNK), F32)],
        scratch_shapes=[pltpu.VMEM((DN_HB, DN_HEAD_DIM, DN_HEAD_DIM), F32)],
        compiler_params=_cparams(("parallel", "arbitrary")),
    )(pre, pre, pre, proj, gates, g_row, ogain)


def _chunk_bwd(name, pre, proj, gates, g_row, ogain, s_in_all, inverse_all, dy):
    assert DN_HB == DN_HEADS
    S = pre.shape[0]
    N = S // CHUNK
    chunk_fn = _chunk_math(DN_PRECISION)
    sp = _chunk_specs(lambda n: N - 1 - n)

    def body(pq, pk, pv, z, gb, gr, og, sin_ref, inv_ref, dy_ref,
             dpre_ref, dz_ref, dgb_ref, dgr_ref, dog_ref, ds):
        @pl.when(pl.program_id(1) == 0)
        def _():
            ds[...] = jnp.zeros_like(ds)
            dog_ref[...] = jnp.zeros_like(dog_ref)

        inverse = inv_ref[...]
        prim = (_split_heads(pq), _split_heads(pk), _split_heads(pv), _split_heads(z),
                _gate_columns(gb[...], 0), _gate_columns(gb[...], DN_HEADS), gr[...], og[...], sin_ref[...])
        _, vjp = jax.vjp(lambda *a: chunk_fn(*a, inverse=inverse), *prim)
        gq, gk, gv, gz, ggc, gbc, ggr, gog, gs = vjp((_split_heads(dy_ref), ds[...]))
        for j in range(DN_HB):
            for part, g in enumerate((gq, gk, gv)):
                dpre_ref[:, pl.ds(part * DN_WIDTH + j * DN_HEAD_DIM, DN_HEAD_DIM)] = g[j]
            dz_ref[:, _head_cols(j)] = gz[j].astype(dz_ref.dtype)
        dgb_ref[...] = _gate_lanes(ggc, 0) + _gate_lanes(gbc, DN_HEADS)
        dgr_ref[...] = ggr
        dog_ref[...] += gog
        ds[...] = gs

    hd = sp["head"]
    return pl.pallas_call(
        body, name=name, grid=(1, N),
        in_specs=[sp["pq"], sp["pk"], sp["pv"], sp["z"], sp["gates"], sp["row"], sp["gain"],
                  sp["state"], sp["inverse"], hd],
        out_specs=[sp["qkv"], hd, sp["gates"], sp["row"], sp["gain"]],
        out_shape=[jax.ShapeDtypeStruct((S, DN_QKV), F32), jax.ShapeDtypeStruct((S, DN_WIDTH), BF16),
                   jax.ShapeDtypeStruct((S, DN_AB_PAD), F32),
                   jax.ShapeDtypeStruct((DN_HEADS, N, 1, CHUNK), F32), jax.ShapeDtypeStruct((1, DN_HEAD_DIM), F32)],
        scratch_shapes=[pltpu.VMEM((DN_HB, DN_HEAD_DIM, DN_HEAD_DIM), F32)],
        compiler_params=_cparams(("arbitrary", "arbitrary")),
    )(pre, pre, pre, proj, gates, g_row, ogain, s_in_all, inverse_all, dy)


def _mm_rms_bwd(name, d_out, w, x, gain, dres, matmul_copy=True):
    first = 0 if matmul_copy else 1
    return _mm(name, d_out, w, "nt", out_dtypes=(F32, BF16)[:2 - first], extras=(x, dres),
               rows=(gain.reshape(1, -1),), n_sums=1, tm_cap=512,
               epilogue=lambda acc, x_, dres_, g: _rms_bwd_fn(x_, dres_, acc, g)[first:])


def _residual_norm_epilogue(acc, res, gain):
    x = acc + res
    return x, _rms_fwd_fn(x, gain)


def _ple_loss_fn(x_mid, pp, zg, t):
    gate = _sigmoid(zg)
    err = x_mid + pp * gate - t
    dy = err * (1.0 / D_MODEL)
    return dy, dy * gate, dy * pp * gate * (1.0 - gate), jnp.broadcast_to(jnp.sum(err * err, keepdims=True), (1, 128))


def _mlp_ple_fwd(tag, x_in, h, p_l, w_up, w_down, norm_ple, w_ple, w_gate, next_gain=None, target=None):
    a = _mm(f"{tag}_up", h, w_up, "nn", out_dtypes=(BF16,), epilogue=_relu2_epilogue)
    x_mid, hg = _mm(f"{tag}_down", a, w_down, "nn", out_dtypes=(F32, BF16), extras=(x_in,),
                    rows=(norm_ple.reshape(1, -1),), epilogue=_residual_norm_epilogue)
    zg = _mm(f"{tag}_gate", hg, w_gate, "nn")
    pp = _mm(f"{tag}_ple", p_l, w_ple, "nn")
    sv = dict(x_in=x_in, h=h, a=a, x_mid=x_mid, hg=hg, zg=zg, pp=pp)
    if target is not None:
        sv["dy"], sv["dpp"], sv["dzg"], sv["sq"] = _rowwise(
            f"{tag}_ple_loss", _ple_loss_fn, [x_mid, pp, zg, target], [],
            [(D_MODEL, F32), (D_MODEL, BF16), (D_MODEL, BF16)], [(1, 128)])
        return None, None, sv
    if next_gain is None:
        return _rowwise(f"{tag}_ple_out", _ple_fwd_fn, [x_mid, pp, zg], [], [(D_MODEL, F32)]), None, sv
    x_out, h_next = _rowwise(f"{tag}_ple_out", _ple_norm_fwd_fn, [x_mid, pp, zg], [next_gain.reshape(1, -1)],
                             [(D_MODEL, F32), (D_MODEL, BF16)])
    return x_out, h_next, sv


def _mlp_ple_bwd(tag, dx, sv, p_l, norm_mlp, w_up, w_down, norm_ple, w_ple, w_gate):
    if "dpp" in sv:
        dpp, dzg = sv["dpp"], sv["dzg"]
    else:
        dpp, dzg = _rowwise(f"{tag}_ple_bwd", _ple_bwd_fn, [dx, sv["pp"], sv["zg"]], [],
                            [(D_MODEL, BF16), (D_MODEL, BF16)])
    d_w_ple = _mm(f"{tag}_d_w_ple", p_l, dpp, "tn", out_dtypes=(BF16,))
    d_w_gate = _mm(f"{tag}_d_w_gate", sv["hg"], dzg, "tn", out_dtypes=(BF16,))
    dx_mid, dx_mid_b, d_norm_ple = _mm_rms_bwd(f"{tag}_d_hg", dzg, w_gate, sv["x_mid"], norm_ple, dx)
    du = _mm(f"{tag}_d_u", dx_mid_b, w_down, "nt", out_dtypes=(BF16,), extras=(sv["a"],),
             epilogue=_relu2_bwd_epilogue)
    d_w_down = _mm(f"{tag}_d_w_down", sv["a"], dx_mid_b, "tn", out_dtypes=(BF16,))
    d_w_up = _mm(f"{tag}_d_w_up", sv["h"], du, "tn", out_dtypes=(BF16,))
    dx_in, dx_in_b, d_norm_mlp = _mm_rms_bwd(f"{tag}_d_h", du, w_up, sv["x_in"], norm_mlp, dx_mid)
    return dx_in, dx_in_b, dict(mlp_norm=d_norm_mlp, w_up=d_w_up, w_down=d_w_down, ple_norm=d_norm_ple,
                                w_ple=d_w_ple, w_ple_gate=d_w_gate)


class _NoHooks:
    first_cargo = None
    fwd_cargo = (None,) * len(SWA_GROUPS)

    def first_weights(self, results):
        return {}

    def weights_from(self, results):
        return {}

    def split_cargo(self, early_grads):
        return None

    def bwd_cargo(self, results):
        return (None,) * len(SWA_GROUPS)

    def last_cargo(self, attn_grads):
        return None


def _with_cargo(result, cargo):
    return (result, ()) if cargo is None else result


def _local_step(x, p, positions, target, small, big, hooks=_NoHooks()):
    S = x.shape[0]
    ct, st = _rope_lane_tables(positions)
    bd = _head_mean_matrix()

    h0, first = _with_cargo(_rms_fwd("l0_mix_norm", x, small["mix_norm"][0], cargo=hooks.first_cargo),
                            hooks.first_cargo)
    big = {**big, **hooks.first_weights(first)}
    attn, brought = [], []
    for g, (window, d) in enumerate(SWA_GROUPS):
        assert window // d == BAND and (S // d) % BAND == 0
        h0g = _to_classes(h0, d)
        ctg, stg = _to_classes(ct, d), _to_classes(st, d)
        w_g = big["attn_w_qkv"][:, g * 3 * A_WIDTH:(g + 1) * 3 * A_WIDTH]
        gq = jnp.tile(small["attn_q_gain"][0, g], A_HEADS).reshape(1, A_WIDTH)
        gk = jnp.tile(small["attn_k_gain"][0, g], A_HEADS).reshape(1, A_WIDTH)
        qkv = _mm(f"l0_qkv{g}", h0g, w_g, "nn")
        q, k, v = _rowwise(f"l0_qk_prep{g}", _qk_prep_fwd_fn, [qkv, ctg, stg], [gq, gk, bd], [(A_WIDTH, BF16)] * 3)
        o, lse, cargo_out = _attn_fwd(f"l0_attn{g}", q, k, v, (S // d) // BAND, cargo=hooks.fwd_cargo[g])
        brought.append(cargo_out)
        attn.append(dict(d=d, h0g=h0g, ct=ctg, st=stg, w=w_g, gq=gq, gk=gk, qkv=qkv, q=q, k=k, v=v, o=o, lse=lse,
                         o_tok=_from_classes(o, d), lse_tok=_from_classes(lse, d)))
    big = {**big, **hooks.weights_from(brought)}
    om = _rowwise("l0_merge", _merge_fwd_fn, [a["o_tok"] for a in attn] + [a["lse_tok"] for a in attn], [],
                  [(A_WIDTH, BF16)])
    x1, h1 = _mm("l0_attn_out", om, big["attn_w_o"], "nn", out_dtypes=(F32, BF16), extras=(x,),
                 rows=(small["mlp_norm"][0].reshape(1, -1),), epilogue=_residual_norm_epilogue)
    x3, h3, sv0 = _mlp_ple_fwd("l0", x1, h1, p[0], big["w_up"][0], big["w_down"][0], small["ple_norm"][0],
                               big["w_ple"][0], big["w_ple_gate"][0], next_gain=small["mix_norm"][1])

    N = S // CHUNK
    proj = _mm("l1_in", h3, big["dn_w_in"], "nn")
    pre = _conv_fwd("l1_conv", proj, small["dn_conv"])
    ab = proj[:, DN_AB0:DN_AB0 + DN_AB_PAD]
    lane_pad = DN_AB_PAD - DN_HEADS
    alog_row = jnp.pad(small["dn_a_log"][0], (0, lane_pad)).reshape(1, DN_AB_PAD)
    dt_row = jnp.pad(small["dn_dt_bias"][0], (0, lane_pad)).reshape(1, DN_AB_PAD)
    cum = _chunk_cumsum_matrix()
    gb = _rowwise("l1_gates", _gates_fwd_fn, [ab], [alog_row, dt_row, cum], [(DN_AB_PAD, F32)], tm=GATES_ROWS)
    g_row = gb[:, :DN_HEADS].T.reshape(DN_HEADS, N, 1, CHUNK)
    ogain = small["dn_o_gain"][0].reshape(1, DN_HEAD_DIM)
    y, s_in_all, inverse_all = _chunk_fwd("l1_delta", pre, proj, gb, g_row, ogain)
    x4, h4 = _mm("l1_dn_out", y, big["dn_w_o"], "nn", out_dtypes=(F32, BF16), extras=(x3,),
                 rows=(small["mlp_norm"][1].reshape(1, -1),), epilogue=_residual_norm_epilogue)
    _, _, sv1 = _mlp_ple_fwd("l1", x4, h4, p[1], big["w_up"][1], big["w_down"][1], small["ple_norm"][1],
                             big["w_ple"][1], big["w_ple_gate"][1], target=target)
    dy, sq = sv1["dy"], sv1["sq"]

    dx4, dx4_b, gl1 = _mlp_ple_bwd("l1", dy, sv1, p[1], small["mlp_norm"][1], big["w_up"][1], big["w_down"][1],
                            small["ple_norm"][1], big["w_ple"][1], big["w_ple_gate"][1])
    d_y = _mm("l1_d_y", dx4_b, big["dn_w_o"], "nt")
    d_dn_w_o = _mm("l1_d_w_o", y, dx4_b, "tn", out_dtypes=(BF16,))
    dpre, dz, dgb_cols, dg_row, d_ogain = _chunk_bwd(
        "l1_delta_bwd", pre, proj, gb, g_row, ogain, s_in_all, inverse_all, d_y)
    dconv_in, d_conv_w = _conv_bwd("l1_conv_bwd", proj, dpre, small["dn_conv"])
    dgb = dgb_cols + jnp.pad(dg_row.reshape(DN_HEADS, S).T, ((0, 0), (0, DN_AB_PAD - DN_HEADS)))
    dab, d_alog, d_dt = _rowwise("l1_gates_bwd", _gates_bwd_fn, [ab, dgb], [alog_row, dt_row, cum],
                                 [(DN_AB_PAD, F32)], [(1, DN_AB_PAD), (1, DN_AB_PAD)], tm=GATES_ROWS)
    dproj = jnp.concatenate([dz, dconv_in, dab.astype(BF16)], axis=1)
    d_dn_w_in = _mm("l1_d_w_in", h3, dproj, "tn", out_dtypes=(BF16,))
    dx3, d_mix1 = _mm_rms_bwd("l1_d_h", dproj, big["dn_w_in"], x3, small["mix_norm"][1], dx4, matmul_copy=False)

    dx1, dx1_b, gl0 = _mlp_ple_bwd("l0", dx3, sv0, p[0], small["mlp_norm"][0], big["w_up"][0], big["w_down"][0],
                            small["ple_norm"][0], big["w_ple"][0], big["w_ple_gate"][0])
    early = dict(
        dn_w_in=jnp.concatenate([d_dn_w_in[:, DN_QKV0:DN_AB0 + 2 * DN_HEADS], d_dn_w_in[:, :DN_WIDTH]], axis=1),
        dn_w_o=d_dn_w_o,
        w_up=jnp.stack([gl0["w_up"], gl1["w_up"]]),
        w_down=jnp.stack([gl0["w_down"], gl1["w_down"]]),
        w_ple=jnp.stack([gl0["w_ple"], gl1["w_ple"]]),
        w_ple_gate=jnp.stack([gl0["w_ple_gate"], gl1["w_ple_gate"]]))
    split_cargo = hooks.split_cargo(early)
    dom = _mm("l0_d_om", dx1_b, big["attn_w_o"], "nt")
    d_attn_w_o = _mm("l0_d_w_o", om, dx1_b, "tn", out_dtypes=(BF16,))
    merged, split = _with_cargo(
        _rowwise("l0_merge_bwd", _merge_bwd_fn, [a["o_tok"] for a in attn] + [a["lse_tok"] for a in attn] + [dom], [],
                 [(A_WIDTH, F32)] * 6, cargo=split_cargo), split_cargo)
    bwd_cargo = hooks.bwd_cargo(split)
    dh0, d_w_qkv, d_gq, d_gk, brought_bwd = [], [], [], [], []
    for g, a in enumerate(attn):
        do_g, dl_g = _to_classes(merged[g], a["d"]), _to_classes(merged[3 + g], a["d"])
        dqn, dkn, dvn, cargo_out = _attn_bwd(f"l0_attn_bwd{g}", a["q"], a["k"], a["v"], a["o"], a["lse"], do_g, dl_g,
                                             (S // a["d"]) // BAND, cargo=bwd_cargo[g])
        brought_bwd.append(cargo_out)
        dqkv, dgq, dgk = _rowwise(f"l0_qk_prep_bwd{g}", _qk_prep_bwd_fn, [a["qkv"], a["ct"], a["st"], dqn, dkn, dvn],
                                  [a["gq"], a["gk"], bd], [(3 * A_WIDTH, BF16)], [(1, A_HEAD_DIM)] * 2)
        d_w_qkv.append(_mm(f"l0_d_w_qkv{g}", a["h0g"], dqkv, "tn", out_dtypes=(BF16,)))
        dh0.append(_from_classes(_mm(f"l0_d_h{g}", dqkv, a["w"], "nt"), a["d"]))
        d_gq.append(dgq)
        d_gk.append(dgk)
    attn_grads = dict(attn_w_qkv=jnp.concatenate(d_w_qkv, axis=1), attn_w_o=d_attn_w_o)
    last_cargo = hooks.last_cargo(attn_grads)
    (grad_x, d_mix0), last = _with_cargo(
        _rms_bwd("l0_mix_norm_bwd", x, small["mix_norm"][0], dx1, dh0, cargo=last_cargo), last_cargo)
    brought_bwd.append(last)

    grads = dict(
        mix_norm=jnp.concatenate([d_mix0, d_mix1], axis=0),
        attn_q_gain=jnp.concatenate(d_gq, axis=0)[None],
        attn_k_gain=jnp.concatenate(d_gk, axis=0)[None],
        **attn_grads,
        dn_conv=d_conv_w,
        dn_a_log=d_alog[:, :DN_HEADS],
        dn_dt_bias=d_dt[:, :DN_HEADS],
        dn_o_gain=d_ogain,
        mlp_norm=jnp.concatenate([gl0["mlp_norm"], gl1["mlp_norm"]], axis=0),
        ple_norm=jnp.concatenate([gl0["ple_norm"], gl1["ple_norm"]], axis=0),
        **early,
    )
    return sq, grad_x, grads, brought_bwd


def _chip_peer(x, y, c, t):
    return (jnp.bitwise_xor(x, t >> 1), jnp.bitwise_xor(y, t & 1), c)


def _place():
    x, y, c = lax.axis_index("x"), lax.axis_index("y"), lax.axis_index("c")
    return x, y, c, 2 * x + y, (x, y, 1 - c)


def _remote(src, dst, send_sem, recv_sem, to):
    return pltpu.make_async_remote_copy(src_ref=src, dst_ref=dst, send_sem=send_sem, recv_sem=recv_sem,
                                        device_id=to, device_id_type=MESH)


def _hbm_call(name, body, ins, out_shape, scratch_shapes):
    any_spec = pl.BlockSpec(memory_space=pl.ANY)
    return pl.pallas_call(body, name=name, out_shape=out_shape, in_specs=[any_spec] * len(ins),
                          out_specs=[any_spec] * len(out_shape), scratch_shapes=scratch_shapes)(*ins)


def _half(n0, which):
    return pl.ds(which * (n0 // 2), n0 // 2)


class _Exchange:
    def __init__(self, ins, out_shape, scratch, start, finish):
        self.ins, self.out_shape, self.scratch, self.start, self.finish = ins, out_shape, scratch, start, finish


def _run_exchange(name, ex):
    n_in, n_out = len(ex.ins), len(ex.out_shape)

    def body(*refs):
        ins, outs, sems = refs[:n_in], refs[n_in:n_in + n_out], refs[n_in + n_out:]
        ex.start(ins, outs, sems)
        ex.finish(ins, outs, sems)

    return _hbm_call(name, body, ex.ins, ex.out_shape, ex.scratch)


def _gather_exchange(shards):
    T = len(shards)

    pairs = [(i, t) for i in range(T) for t in range(1, N_CHIPS)]

    def copies(ins, outs, sems):
        send, recv = sems
        x, y, c, q, sibling = _place()

        def half(i, which):
            return _half(ins[i].shape[0], which)

        def over_ici(i, t):
            return _remote(ins[i].at[half(i, c)], outs[i].at[q, half(i, c)], send.at[i, t - 1], recv.at[i, t - 1],
                           _chip_peer(x, y, c, t))

        def landing(i, t):
            spot = outs[i].at[jnp.bitwise_xor(q, t), half(i, c)]
            return _remote(spot, spot, send.at[i, t - 1], recv.at[i, t - 1], _chip_peer(x, y, c, t))

        def forward(i, t):
            spot = outs[i].at[jnp.bitwise_xor(q, t), half(i, c)]
            return _remote(spot, spot, send.at[i, 2 + t], recv.at[i, 2 + t], sibling)

        def forwarded(i, t):
            spot = outs[i].at[jnp.bitwise_xor(q, t), half(i, 1 - c)]
            return _remote(spot, spot, send.at[i, 2 + t], recv.at[i, 2 + t], sibling)

        return over_ici, landing, forward, forwarded

    def start(ins, outs, sems):
        over_ici = copies(ins, outs, sems)[0]
        for i, t in pairs:
            over_ici(i, t).start()

    def finish(ins, outs, sems):
        over_ici, landing, forward, forwarded = copies(ins, outs, sems)
        for i, t in pairs:
            landing(i, t).wait_recv()
            forward(i, t).start()
        for i, t in pairs:
            forwarded(i, t).wait_recv()
        for i, t in pairs:
            over_ici(i, t).wait_send()
            forward(i, t).wait_send()

    n_rel = 2 * (N_CHIPS - 1)
    return _Exchange(list(shards), [jax.ShapeDtypeStruct((N_CHIPS,) + s.shape, s.dtype) for s in shards],
                     [pltpu.SemaphoreType.DMA((T, n_rel)), pltpu.SemaphoreType.DMA((T, n_rel))], start, finish)


def _scatter_exchange(stacks):
    T = len(stacks)

    def copies(ins, outs, sems):
        send, recv = sems
        x, y, c, q, sibling = _place()
        return [_remote(ins[i].at[jnp.bitwise_xor(q, t)], outs[i].at[t - 1], send.at[i, t - 1], recv.at[i, t - 1],
                        _chip_peer(x, y, c, t)) for i in range(T) for t in range(1, N_CHIPS)]

    def start(ins, outs, sems):
        for cp in copies(ins, outs, sems):
            cp.start()

    def finish(ins, outs, sems):
        for cp in copies(ins, outs, sems):
            cp.wait()

    return _Exchange(list(stacks), [jax.ShapeDtypeStruct((N_CHIPS - 1,) + s.shape[1:], s.dtype) for s in stacks],
                     [pltpu.SemaphoreType.DMA((T, N_CHIPS - 1)), pltpu.SemaphoreType.DMA((T, N_CHIPS - 1))],
                     start, finish)


def _other_half_exchange(stacks):
    T = len(stacks)

    def copies(ins, outs, sems):
        send, recv = sems
        x, y, c, q, sibling = _place()
        return [_remote(ins[i].at[:, _half(ins[i].shape[1], 1 - c)], outs[i], send.at[i], recv.at[i], sibling)
                for i in range(T)]

    def start(ins, outs, sems):
        for cp in copies(ins, outs, sems):
            cp.start()

    def finish(ins, outs, sems):
        for cp in copies(ins, outs, sems):
            cp.wait()

    return _Exchange(list(stacks),
                     [jax.ShapeDtypeStruct((s.shape[0], s.shape[1] // 2) + s.shape[2:], s.dtype) for s in stacks],
                     [pltpu.SemaphoreType.DMA((T,)), pltpu.SemaphoreType.DMA((T,))], start, finish)


def _swap_with_sibling(name, arrays):
    T = len(arrays)

    def body(*refs):
        ins, outs = refs[:T], refs[T:2 * T]
        send, recv = refs[2 * T:]
        x, y, c, q, sibling = _place()
        copies = []
        for i in range(T):
            rc = _remote(ins[i], outs[i], send.at[i], recv.at[i], sibling)
            rc.start()
            copies.append(rc)
        for cp in copies:
            cp.wait()

    return _hbm_call(name, body, arrays, [jax.ShapeDtypeStruct(a.shape, a.dtype) for a in arrays],
                     [pltpu.SemaphoreType.DMA((T,)), pltpu.SemaphoreType.DMA((T,))])


def _gather_from_all(name, block):
    R, C = block.shape

    def body(src, out, send_sems, recv_sems):
        x, y, c = lax.axis_index("x"), lax.axis_index("y"), lax.axis_index("c")
        me = 4 * x + 2 * y + c
        out[me] = src[...]
        copies = []
        for r in range(1, N_DEV):
            peer = (jnp.bitwise_xor(x, r >> 2), jnp.bitwise_xor(y, (r >> 1) & 1), jnp.bitwise_xor(c, r & 1))
            cp = pltpu.make_async_remote_copy(src_ref=src, dst_ref=out.at[me], send_sem=send_sems.at[r - 1],
                                              recv_sem=recv_sems.at[r - 1], device_id=peer, device_id_type=MESH)
            cp.start()
            copies.append(cp)
        for cp in copies:
            cp.wait()

    return pl.pallas_call(
        body, name=name, out_shape=jax.ShapeDtypeStruct((N_DEV, R, C), block.dtype),
        in_specs=[pl.BlockSpec(memory_space=pltpu.VMEM)], out_specs=pl.BlockSpec(memory_space=pltpu.VMEM),
        scratch_shapes=[pltpu.SemaphoreType.DMA((N_DEV - 1,)), pltpu.SemaphoreType.DMA((N_DEV - 1,))],
    )(block)


def _view(a):
    return a[0] if a.shape[0] == 1 else a


def _view_axis(a, axis):
    return axis - 1 if a.shape[0] == 1 else axis


def _rows(a):
    return a.reshape(-1, a.shape[-1])


def _elementwise(name, fn, ins, out_dtypes, tm):
    specs = []
    for a in ins:
        a, row0 = a if isinstance(a, tuple) else (a, 0)
        specs.append((_rows(a), a.shape[-1], 0, row0))
    shape = ins[0][0].shape if isinstance(ins[0], tuple) else ins[0].shape
    outs = _rowwise(name, fn, specs, [], [(shape[-1], dt) for dt in out_dtypes], tm=tm, n_rows=math.prod(shape[:-1]))
    return outs.reshape(shape) if len(out_dtypes) == 1 else [o.reshape(shape) for o in outs]


SMALL_ROWS = 8
CONV_ROWS = CONV_WIDTH * DN_QKV // D_MODEL
SMALL_GRAD_ROWS = 24


def _pack_small(vals, conv=None):
    tail = jnp.concatenate([vals["attn_q_gain"].reshape(-1), vals["attn_k_gain"].reshape(-1),
                            vals["dn_a_log"].reshape(-1), vals["dn_dt_bias"].reshape(-1),
                            vals["dn_o_gain"].reshape(-1)])
    tail = jnp.pad(tail, (0, D_MODEL - tail.shape[0])).reshape(1, D_MODEL)
    rows = [vals["mix_norm"], vals["mlp_norm"], vals["ple_norm"], tail, jnp.zeros((1, D_MODEL), F32)]
    if conv is not None:
        rows += [conv.reshape(CONV_ROWS, D_MODEL),
                 jnp.zeros((SMALL_GRAD_ROWS - SMALL_ROWS - CONV_ROWS, D_MODEL), F32)]
    return jnp.concatenate(rows, axis=0)


def _unpack_small(block):
    nq = 3 * A_HEAD_DIM
    t = block[6]
    return dict(
        mix_norm=block[0:2], mlp_norm=block[2:4], ple_norm=block[4:6],
        attn_q_gain=t[:nq].reshape(1, 3, A_HEAD_DIM), attn_k_gain=t[nq:2 * nq].reshape(1, 3, A_HEAD_DIM),
        dn_a_log=t[2 * nq:2 * nq + DN_HEADS].reshape(1, DN_HEADS),
        dn_dt_bias=t[2 * nq + DN_HEADS:2 * nq + 2 * DN_HEADS].reshape(1, DN_HEADS),
        dn_o_gain=t[2 * nq + 2 * DN_HEADS:2 * nq + 2 * DN_HEADS + DN_HEAD_DIM].reshape(1, DN_HEAD_DIM))


def kernel(x, p, positions, mix_norm, attn_w_qkv, attn_q_gain, attn_k_gain, attn_w_o, dn_w_in, dn_conv, dn_a_log, dn_dt_bias, dn_o_gain, dn_w_o, mlp_norm, w_up, w_down, ple_norm, w_ple, w_ple_gate, loss_target, m_mix_norm, m_attn_w_qkv, m_attn_q_gain, m_attn_k_gain, m_attn_w_o, m_dn_w_in, m_dn_conv, m_dn_a_log, m_dn_dt_bias, m_dn_o_gain, m_dn_w_o, m_mlp_norm, m_w_up, m_w_down, m_ple_norm, m_w_ple, m_w_ple_gate, v_mix_norm, v_attn_w_qkv, v_attn_q_gain, v_attn_k_gain, v_attn_w_o, v_dn_w_in, v_dn_conv, v_dn_a_log, v_dn_dt_bias, v_dn_o_gain, v_dn_w_o, v_mlp_norm, v_w_up, v_w_down, v_ple_norm, v_w_ple, v_w_ple_gate):
    given = dict(locals())
    w = {n: given[n] for n in WEIGHTS}
    m = {n: given["m_" + n] for n in WEIGHTS}
    v = {n: given["v_" + n] for n in WEIGHTS}
    kinds = ("grad", "delta", "new_m", "new_v")
    axes = {n: _view_axis(w[n], axis) for n, axis in SHARDED if n != "dn_conv"}
    chip = 2 * lax.axis_index("x") + lax.axis_index("y")
    core = lax.axis_index("c")
    shards = {n: _view(w[n]).astype(BF16) for n in axes}

    def whole(n, slots):
        return jnp.concatenate([jnp.where(chip == q, shards[n], slots[q]) for q in range(N_CHIPS)], axis=axes[n])

    def stacks_of(grads_of):
        return [jnp.stack(jnp.split(g, N_CHIPS, axis=axes[n])) for n, g in grads_of.items()]

    def chip_sums_of(names, stacks, theirs):
        mine = [lax.dynamic_slice_in_dim(s, core * (s.shape[1] // 2), s.shape[1] // 2, axis=1) for s in stacks]
        return {n: _elementwise(f"add_core_{n}", lambda a, b: a.astype(F32) + b.astype(F32), [a, b], [BF16], 128)
                for n, a, b in zip(names, mine, theirs)}

    class Hooks:
        first_cargo = _gather_exchange([shards[n] for n in ATTN_MATRICES])
        fwd_cargo = [_gather_exchange([shards[n] for n in group]) for group in CARGO_GROUPS]
        chip_sums = {}
        early = None

        def first_weights(self, results):
            return {n: whole(n, slots) for n, slots in zip(ATTN_MATRICES, results)}

        def weights_from(self, results):
            full = {n: whole(n, slots) for group, res in zip(CARGO_GROUPS, results) for n, slots in zip(group, res)}
            w_in, n_ab = full["dn_w_in"], 2 * DN_HEADS
            full["dn_w_in"] = jnp.concatenate([w_in[:, DN_QKV + n_ab:], w_in[:, :DN_QKV + n_ab],
                                               jnp.zeros((D_MODEL, DN_AB_PAD - n_ab), BF16)], axis=1)
            return full

        def split_cargo(self, early_grads):
            self.early = (list(early_grads), stacks_of(early_grads))
            return _other_half_exchange(self.early[1])

        def bwd_cargo(self, results):
            self.chip_sums.update(chip_sums_of(*self.early, results))
            return [_scatter_exchange([self.chip_sums[n] for n in group]) for group in CARGO_GROUPS]

        def last_cargo(self, attn_grads):
            stacks = stacks_of(attn_grads)
            theirs = _run_exchange("split_core_grads_attn", _other_half_exchange(stacks))
            self.chip_sums.update(chip_sums_of(list(attn_grads), stacks, theirs))
            return _scatter_exchange([self.chip_sums[n] for n in attn_grads])

    hooks = Hooks()
    big = {}
    conv_block = jnp.pad(w["dn_conv"].reshape(-1), (0, SMALL_ROWS * D_MODEL - w["dn_conv"].size))
    conv_all = _gather_from_all("gather_conv", conv_block.reshape(SMALL_ROWS, D_MODEL))
    conv_all = conv_all.reshape(N_CHIPS, 2, -1)[:, 0, :w["dn_conv"].size]
    conv_full = jnp.concatenate([conv_all[q].reshape(CONV_WIDTH, -1) for q in range(N_CHIPS)], axis=1)
    small = {n: w[n] for n in REPLICATED}
    small["dn_conv"] = conv_full

    sq, grad_x, grads, brought = _local_step(x[0], p[:, 0], positions[0], loss_target[0], small, big, hooks)
    loss = lax.psum(0.5 * sq[0, 0] / D_MODEL, ("x", "y", "c"))
    out = {}

    landed = {n: r for group, res in zip(CARGO_GROUPS + (ATTN_MATRICES,), brought) for n, r in zip(group, res)}
    half_sums = []
    for n in axes:
        o = lax.dynamic_index_in_dim(hooks.chip_sums[n], chip, axis=0, keepdims=False)
        per = math.prod(o.shape[:-1])
        r = landed[n]
        half_sums.append(_elementwise(
            f"add_chips_{n}", lambda a, b, c, d: ((a.astype(F32) + b.astype(F32)) + c.astype(F32)) + d.astype(F32),
            [o, (r, 0), (r, per), (r, 2 * per)], [F32], 128))
    other_halves = _swap_with_sibling("join_core_sums", half_sums)
    for n, a, b in zip(axes, half_sums, other_halves):
        g = jnp.where(core == 0, jnp.concatenate([a, b], axis=0), jnp.concatenate([b, a], axis=0))
        shp = w[n].shape
        res = _elementwise(f"adamw_{n}", lambda g, w_, m_, v_: (g,) + _adamw(w_, g, m_, v_),
                           [g.reshape(shp), w[n], m[n], v[n]], [F32] * 4, 512)
        for kind, arr in zip(kinds, res):
            out[kind + "_" + n] = arr.reshape(shp)

    slots = _gather_from_all("gather_small_grads", _pack_small(grads, grads["dn_conv"]))

    def small_body(s_ref, w_ref, m_ref, v_ref, sum_out, g_out, d_out, m_out, v_out):
        total = s_ref[0]
        for d in range(1, N_DEV):
            total = total + s_ref[d]
        sum_out[...] = total
        g = total[:SMALL_ROWS]
        for o, r in zip((g_out, d_out, m_out, v_out), (g,) + _adamw(w_ref[...], g, m_ref[...], v_ref[...])):
            o[...] = r

    res = pl.pallas_call(small_body, name="adamw_replicated",
                         out_shape=[jax.ShapeDtypeStruct((SMALL_GRAD_ROWS, D_MODEL), F32)]
                         + [jax.ShapeDtypeStruct((SMALL_ROWS, D_MODEL), F32)] * 4)(
        slots, _pack_small(w), _pack_small(m), _pack_small(v))
    for kind, block in zip(kinds, res[1:]):
        for n, arr in _unpack_small(block).items():
            out[kind + "_" + n] = arr
    conv_sum = res[0][SMALL_ROWS:SMALL_ROWS + CONV_ROWS].reshape(CONV_WIDTH, DN_QKV)
    cols = DN_QKV // N_CHIPS
    chip = 2 * lax.axis_index("x") + lax.axis_index("y")
    conv_mine = lax.dynamic_slice_in_dim(conv_sum, chip * cols, cols, axis=1)
    res = _elementwise("adamw_dn_conv", lambda g, w_, m_, v_: (g,) + _adamw(w_, g, m_, v_),
                       [conv_mine, w["dn_conv"][0], m["dn_conv"][0], v["dn_conv"][0]], [F32] * 4, CONV_WIDTH)
    for kind, arr in zip(kinds, res):
        out[kind + "_dn_conv"] = arr[None]

    return (loss, grad_x[None],
            *[out["grad_" + n] for n in WEIGHTS], *[out["delta_" + n] for n in WEIGHTS],
            *[out["new_m_" + n] for n in WEIGHTS], *[out["new_v_" + n] for n in WEIGHTS])
```

```python
import functools
import math

import jax
import jax.numpy as jnp
from jax import lax
from jax.experimental import pallas as pl
from jax.experimental.pallas import tpu as pltpu

F32 = jnp.float32
BF16 = jnp.bfloat16
HIGHEST = lax.Precision.HIGHEST

D_MODEL = 1024
EPS = 1e-6
SWA_GROUPS = ((128, 1), (512, 4), (2048, 16))
A_HEADS = 8
A_HEAD_DIM = 64
A_WIDTH = A_HEADS * A_HEAD_DIM
ROPE_DIM = A_HEAD_DIM // 4
ROPE_THETA = 500000.0
BAND = 128
DN_HEADS = 8
DN_HEAD_DIM = 128
DN_WIDTH = DN_HEADS * DN_HEAD_DIM
DN_QKV = 3 * DN_WIDTH
DN_AB_PAD = 128
DN_QKV0 = DN_WIDTH
DN_AB0 = DN_WIDTH + DN_QKV
DN_HB = 8
CONV_WIDTH = 4
CHUNK = 64

ADAM_LR = 0.001
ADAM_B1 = 0.9
ADAM_B2 = 0.999
ADAM_EPS = 1e-08
ADAM_WD = 0.01
ADAM_STEP = 10

N_CHIPS = 4
N_DEV = 8
VMEM_LIMIT = 48 * 1024 * 1024
MESH = pl.DeviceIdType.MESH

SHARDED = (
    ("attn_w_qkv", 2), ("attn_w_o", 2), ("dn_w_in", 2), ("dn_conv", 2), ("dn_w_o", 1),
    ("w_up", 2), ("w_down", 1), ("w_ple", 2), ("w_ple_gate", 1))
ATTN_MATRICES = ("attn_w_qkv", "attn_w_o")
CARGO_GROUPS = (("w_up",), ("w_down",), ("dn_w_in", "dn_w_o", "w_ple", "w_ple_gate"))
REPLICATED = ("mix_norm", "attn_q_gain", "attn_k_gain", "dn_a_log", "dn_dt_bias", "dn_o_gain",
              "mlp_norm", "ple_norm")
WEIGHTS = ("mix_norm", "attn_w_qkv", "attn_q_gain", "attn_k_gain", "attn_w_o", "dn_w_in", "dn_conv",
           "dn_a_log", "dn_dt_bias", "dn_o_gain", "dn_w_o", "mlp_norm", "w_up", "w_down", "ple_norm",
           "w_ple", "w_ple_gate")


def _cparams(sem=None):
    return pltpu.CompilerParams(dimension_semantics=sem, vmem_limit_bytes=VMEM_LIMIT)


def _pick(n, cap, quantum=128):
    best = None
    for t in range(quantum, min(n, cap) + 1, quantum):
        if n % t == 0:
            best = t
    return n if best is None else best


_DIMS = {"nn": ((1,), (0,)), "nt": ((1,), (1,)), "tn": ((0,), (0,))}


def _mm(name, a, b, mode, out_dtypes=(F32,), extras=(), epilogue=None, rows=(), n_sums=0, tm_cap=1024):
    if mode == "nn":
        (M, K), (K2, N) = a.shape, b.shape
    elif mode == "nt":
        (M, K), (N, K2) = a.shape, b.shape
    else:
        (K, M), (K2, N) = a.shape, b.shape
    assert K == K2, (name, a.shape, b.shape)
    tn = _pick(N, 1536)
    if mode == "tn":
        tm, tk = _pick(M, tm_cap), _pick(K, 2048)
    elif tn == N and K > 1536:
        tm, tk = _pick(M, min(tm_cap, 512)), K
    elif tn < N:
        tm, tk = _pick(M, 2 * tm_cap), _pick(K, 1536)
    else:
        tm, tk = _pick(M, tm_cap), _pick(K, 1536)
    nk = K // tk
    assert n_sums == 0 or tn == N, name
    if mode == "nn":
        a_spec = pl.BlockSpec((tm, tk), lambda i, j, k: (i, k))
        b_spec = pl.BlockSpec((tk, tn), lambda i, j, k: (k, j))
    elif mode == "nt":
        a_spec = pl.BlockSpec((tm, tk), lambda i, j, k: (i, k))
        b_spec = pl.BlockSpec((tn, tk), lambda i, j, k: (j, k))
    else:
        a_spec = pl.BlockSpec((tk, tm), lambda i, j, k: (k, i))
        b_spec = pl.BlockSpec((tk, tn), lambda i, j, k: (k, j))
    o_spec = pl.BlockSpec((tm, tn), lambda i, j, k: (i, j))
    r_spec = pl.BlockSpec((1, tn), lambda i, j, k: (0, j))
    n_extra, n_out = len(extras) + len(rows), len(out_dtypes)
    dims = (_DIMS[mode], ((), ()))

    def body(a_ref, b_ref, *rest):
        extra_refs, out_refs = rest[:n_extra], rest[n_extra:n_extra + n_out]
        sum_refs = rest[n_extra + n_out:n_extra + n_out + n_sums]
        i, k = pl.program_id(0), pl.program_id(2)
        part = lax.dot_general(a_ref[...].astype(BF16), b_ref[...].astype(BF16), dims, preferred_element_type=F32)

        def finish(total):
            vals = (total,) if epilogue is None else epilogue(total, *[e[...] for e in extra_refs])
            for o, v in zip(out_refs, vals[:n_out]):
                o[...] = v.astype(o.dtype)
            for s, v in zip(sum_refs, vals[n_out:]):
                @pl.when(i == 0)
                def _():
                    s[...] = v

                @pl.when(i > 0)
                def _():
                    s[...] += v

        if nk == 1:
            finish(part)
            return
        acc = rest[-1]

        @pl.when(k == 0)
        def _():
            acc[...] = part

        @pl.when(jnp.logical_and(k > 0, k < nk - 1))
        def _():
            acc[...] += part

        @pl.when(k == nk - 1)
        def _():
            finish(acc[...] + part)

    outs = pl.pallas_call(
        body, name=name, grid=(M // tm, N // tn, nk),
        in_specs=[a_spec, b_spec] + [o_spec] * len(extras) + [r_spec] * len(rows),
        out_specs=[o_spec] * n_out + [r_spec] * n_sums,
        out_shape=[jax.ShapeDtypeStruct((M, N), dt) for dt in out_dtypes]
        + [jax.ShapeDtypeStruct((1, N), F32)] * n_sums,
        scratch_shapes=[pltpu.VMEM((tm, tn), F32)] if nk > 1 else [],
        compiler_params=_cparams(("arbitrary" if n_sums else "parallel", "parallel", "arbitrary")),
    )(a, b, *extras, *rows)
    return outs[0] if n_out + n_sums == 1 else outs


def _rowwise(name, fn, rows, bcast, row_outs, acc_outs=(), tm=512, n_rows=None, cargo=None):
    rows = [r if isinstance(r, tuple) else (r, r.shape[1], 0) for r in rows]
    rows = [r if len(r) == 4 else r + (0,) for r in rows]
    S = rows[0][0].shape[0] if n_rows is None else n_rows
    tm = min(tm, S)
    assert S % tm == 0 and all(r[3] % tm == 0 for r in rows), (name, S, tm)
    n_row, n_bc, n_ro, n_acc = len(rows), len(bcast), len(row_outs), len(acc_outs)
    in_specs = [pl.BlockSpec((tm, w), functools.partial(lambda i, cb, rb: (i + rb, cb), cb=cb, rb=r0 // tm))
                for _, w, cb, r0 in rows]
    in_specs += [pl.BlockSpec(b.shape, lambda i: (0, 0)) for b in bcast]
    out_specs = [pl.BlockSpec((tm, c), lambda i: (i, 0)) for c, _ in row_outs]
    out_specs += [pl.BlockSpec(s, lambda i: (0, 0)) for s in acc_outs]
    out_shape = [jax.ShapeDtypeStruct((S, c), dt) for c, dt in row_outs]
    out_shape += [jax.ShapeDtypeStruct(s, F32) for s in acc_outs]

    def body(*refs):
        ins = [r[...] for r in refs[:n_row + n_bc]]
        outs = refs[n_row + n_bc:]
        vals = fn(*ins)
        if not isinstance(vals, (tuple, list)):
            vals = (vals,)
        for o, v in zip(outs[:n_ro], vals[:n_ro]):
            o[...] = v.astype(o.dtype)
        if n_acc:
            @pl.when(pl.program_id(0) == 0)
            def _():
                for o in outs[n_ro:]:
                    o[...] = jnp.zeros_like(o)
            for o, v in zip(outs[n_ro:], vals[n_ro:]):
                o[...] += v

    n_own = n_ro + n_acc
    body, c_in_specs, c_out_specs, c_out_shape, c_scratch, c_ins = _carry(cargo, n_row + n_bc, n_own, 0, body, S // tm)
    outs = pl.pallas_call(
        body, name=name, grid=(S // tm,), in_specs=in_specs + c_in_specs, out_specs=out_specs + c_out_specs,
        out_shape=out_shape + c_out_shape, scratch_shapes=c_scratch,
        compiler_params=_cparams(("arbitrary",) if n_acc or cargo is not None else ("parallel",)),
    )(*[r[0] for r in rows], *bcast, *c_ins)
    own = outs[0] if n_own == 1 else outs[:n_own]
    return own if cargo is None else (own, outs[n_own:])


def _sigmoid(x):
    return 1.0 / (1.0 + jnp.exp(-x))


def _silu(x):
    return x * _sigmoid(x)


def _softplus(x):
    return jnp.maximum(x, 0.0) + jnp.log(1.0 + jnp.exp(-jnp.abs(x)))


def _rms_fwd_fn(x, g):
    r = lax.rsqrt(jnp.mean(x * x, axis=-1, keepdims=True) + EPS)
    return (x * r) * g


def _rms_bwd_fn(x, dres, *rest):
    dh, g = sum(rest[:-1]), rest[-1]
    r = lax.rsqrt(jnp.mean(x * x, axis=-1, keepdims=True) + EPS)
    xh = x * r
    dxh = dh * g
    dx = dres + r * (dxh - xh * jnp.mean(dxh * xh, axis=-1, keepdims=True))
    return dx, dx, jnp.sum(dh * xh, axis=0, keepdims=True)


def _rms_fwd(name, x, gain, cargo=None):
    return _rowwise(name, _rms_fwd_fn, [x], [gain.reshape(1, -1)], [(x.shape[1], BF16)], cargo=cargo)


def _rms_bwd(name, x, gain, dres, dhs, cargo=None):
    return _rowwise(name, lambda *a: _rms_bwd_fn(*a)[1:], [x, dres] + list(dhs), [gain.reshape(1, -1)],
                    [(x.shape[1], F32)], [(1, x.shape[1])], cargo=cargo)


def _relu2_epilogue(acc):
    r = jnp.maximum(acc, 0.0)
    return (r * r,)


def _relu2_bwd_epilogue(acc, a):
    return (acc * (2.0 * jnp.sqrt(a.astype(F32))),)


def _ple_fwd_fn(x, pp, zg):
    return x + pp * _sigmoid(zg)


def _ple_norm_fwd_fn(x, pp, zg, gain):
    out = _ple_fwd_fn(x, pp, zg)
    return out, _rms_fwd_fn(out, gain)


def _ple_bwd_fn(dx, pp, zg):
    gate = _sigmoid(zg)
    return dx * gate, dx * pp * gate * (1.0 - gate)


def _adamw(w, g, m, v):
    m = ADAM_B1 * m + (1.0 - ADAM_B1) * g
    v = ADAM_B2 * v + (1.0 - ADAM_B2) * jnp.square(g)
    m_hat = m / (1.0 - ADAM_B1 ** ADAM_STEP)
    v_hat = v / (1.0 - ADAM_B2 ** ADAM_STEP)
    delta = -ADAM_LR * (m_hat / (jnp.sqrt(v_hat) + ADAM_EPS) + ADAM_WD * w)
    return delta, m, v


def _lane_take(x, offset):
    n = x.shape[-1]
    return pltpu.roll(x, (-offset) % n, 1)


def _head_lane(shape):
    return lax.broadcasted_iota(jnp.int32, shape, 1) % A_HEAD_DIM


def _rope_partner(x):
    lane = _head_lane(x.shape)
    return jnp.where(lane < ROPE_DIM // 2, _lane_take(x, ROPE_DIM // 2),
                     jnp.where(lane < ROPE_DIM, _lane_take(x, -(ROPE_DIM // 2)), 0.0))


def _head_mean(x, bd):
    hi = x.astype(BF16)
    lo = (x - hi.astype(F32)).astype(BF16)
    b = bd.astype(BF16)
    return jnp.dot(hi, b, preferred_element_type=F32) + jnp.dot(lo, b, preferred_element_type=F32)


def _fold_heads(row):
    out = row[:, :A_HEAD_DIM]
    for h in range(1, A_HEADS):
        out = out + row[:, h * A_HEAD_DIM:(h + 1) * A_HEAD_DIM]
    return out


def _all_heads(t):
    return jnp.concatenate([t] * (A_WIDTH // t.shape[1]), axis=1)


def _qk_prep_fwd_fn(qkv, ct, st, gq, gk, bd):
    ct, st = _all_heads(ct), _all_heads(st)

    def one(t, g):
        n = t * lax.rsqrt(_head_mean(t * t, bd) + EPS) * g
        return n * ct + _rope_partner(n) * st
    q, k, v = qkv[:, :A_WIDTH], qkv[:, A_WIDTH:2 * A_WIDTH], qkv[:, 2 * A_WIDTH:]
    return one(q, gq), one(k, gk), v


def _qk_prep_bwd_fn(qkv, ct, st, dq, dk, dv, gq, gk, bd):
    ct, st = _all_heads(ct), _all_heads(st)

    def one(t, g, dy):
        r = lax.rsqrt(_head_mean(t * t, bd) + EPS)
        nh = t * r
        dn = dy * ct + _rope_partner(dy * st)
        dg = jnp.sum(dn * nh, axis=0, keepdims=True)
        dnh = dn * g
        return r * (dnh - nh * _head_mean(dnh * nh, bd)), _fold_heads(dg)
    q, k = qkv[:, :A_WIDTH], qkv[:, A_WIDTH:2 * A_WIDTH]
    dq_raw, dgq = one(q, gq, dq)
    dk_raw, dgk = one(k, gk, dk)
    return jnp.concatenate([dq_raw, dk_raw, dv], axis=1), dgq, dgk


_BATCH_DIMS = {"nn": ((2,), (1,)), "nt": ((2,), (2,)), "tn": ((1,), (1,))}


def _bdot(a, b, mode, precision=None):
    return lax.dot_general(a, b, (_BATCH_DIMS[mode], ((0,), (0,))), precision=precision,
                           preferred_element_type=F32)


def _attn_cols(h):
    return slice(h * A_HEAD_DIM, (h + 1) * A_HEAD_DIM)


def _attn_heads(ref):
    return jnp.stack([ref[:, _attn_cols(h)] for h in range(A_HEADS)])


def _band_masks():
    qi = lax.broadcasted_iota(jnp.int32, (BAND, BAND), 0)
    kj = lax.broadcasted_iota(jnp.int32, (BAND, BAND), 1)
    return kj <= qi, kj >= qi


def _attn_fwd(name, q, k, v, blocks_per_class, cargo=None):
    S = q.shape[0]
    nblk = S // BAND
    scale = A_HEAD_DIM ** -0.5

    def body(q_ref, kp_ref, kc_ref, vp_ref, vc_ref, o_ref, l_ref):
        i = pl.program_id(0)
        has_prev = (i % blocks_per_class) != 0
        m_cur, m_prev = _band_masks()
        m_prev = jnp.logical_and(m_prev, has_prev)
        q, kc, kp, vc, vp = (_attn_heads(r) for r in (q_ref, kc_ref, kp_ref, vc_ref, vp_ref))
        s_c = jnp.where(m_cur[None], _bdot(q, kc, "nt") * scale, -jnp.inf)
        s_p = jnp.where(m_prev[None], _bdot(q, kp, "nt") * scale, -jnp.inf)
        m = jnp.maximum(jnp.max(s_c, axis=-1, keepdims=True), jnp.max(s_p, axis=-1, keepdims=True))
        e_c, e_p = jnp.exp(s_c - m), jnp.exp(s_p - m)
        l = jnp.sum(e_c, axis=-1, keepdims=True) + jnp.sum(e_p, axis=-1, keepdims=True)
        o = _bdot((e_c / l).astype(BF16), vc, "nn") + _bdot((e_p / l).astype(BF16), vp, "nn")
        lse = m + jnp.log(l)
        for h in range(A_HEADS):
            o_ref[:, _attn_cols(h)] = o[h]
            l_ref[:, _attn_cols(h)] = jnp.broadcast_to(lse[h], (BAND, A_HEAD_DIM))

    cur = pl.BlockSpec((BAND, A_WIDTH), lambda i: (i, 0))
    prev = pl.BlockSpec((BAND, A_WIDTH), lambda i: (jnp.maximum(i - 1, 0), 0))
    body, c_in_specs, c_out_specs, c_out_shape, c_scratch, c_ins = _carry(cargo, 5, 2, 0, body, nblk)
    outs = pl.pallas_call(
        body, name=name, grid=(nblk,), in_specs=[cur, prev, cur, prev, cur] + c_in_specs,
        out_specs=[cur, cur] + c_out_specs,
        out_shape=[jax.ShapeDtypeStruct((S, A_WIDTH), F32)] * 2 + c_out_shape, scratch_shapes=c_scratch,
        compiler_params=_cparams(("arbitrary",)),
    )(q, k, k, v, v, *c_ins)
    return outs[0], outs[1], outs[2:]


def _carry(cargo, n_in, n_out, n_scratch, body, steps):
    if cargo is None:
        return body, [], [], [], [], []
    n_ci, n_co = len(cargo.ins), len(cargo.out_shape)

    def carrying(*refs):
        refs = list(refs)
        ins, refs = refs[:n_in], refs[n_in:]
        c_ins, refs = refs[:n_ci], refs[n_ci:]
        outs, refs = refs[:n_out], refs[n_out:]
        c_outs, refs = refs[:n_co], refs[n_co:]
        scratch, sems = refs[:n_scratch], refs[n_scratch:]

        @pl.when(pl.program_id(0) == 0)
        def _():
            cargo.start(c_ins, c_outs, sems)

        body(*ins, *outs, *scratch)

        @pl.when(pl.program_id(0) == steps - 1)
        def _():
            cargo.finish(c_ins, c_outs, sems)

    any_spec = pl.BlockSpec(memory_space=pl.ANY)
    return carrying, [any_spec] * n_ci, [any_spec] * n_co, list(cargo.out_shape), list(cargo.scratch), list(cargo.ins)


def _attn_bwd(name, q, k, v, o, lse, do, dlse, blocks_per_class, cargo=None):
    S = q.shape[0]
    nblk = S // BAND
    scale = A_HEAD_DIM ** -0.5

    def body(q_ref, kp_ref, kc_ref, vp_ref, vc_ref, o_ref, l_ref, do_ref, dl_ref,
             dq_ref, dk_ref, dv_ref, ck, cv):
        i = pl.program_id(0)

        @pl.when(i == 0)
        def _():
            ck[...] = jnp.zeros_like(ck)
            cv[...] = jnp.zeros_like(cv)

        @pl.when(i == nblk)
        def _():
            dk_ref[...] = ck[...]
            dv_ref[...] = cv[...]

        @pl.when(i < nblk)
        def _():
            has_prev = (i % blocks_per_class) != 0
            m_cur, m_prev = _band_masks()
            m_prev = jnp.logical_and(m_prev, has_prev)
            q, kc, kp, vc, vp = (_attn_heads(r) for r in (q_ref, kc_ref, kp_ref, vc_ref, vp_ref))
            do, o, dl = _attn_heads(do_ref), _attn_heads(o_ref), _attn_heads(dl_ref)
            lse = jnp.max(_attn_heads(l_ref), axis=-1, keepdims=True)
            p_c = jnp.where(m_cur[None], jnp.exp(_bdot(q, kc, "nt") * scale - lse), 0.0)
            p_p = jnp.where(m_prev[None], jnp.exp(_bdot(q, kp, "nt") * scale - lse), 0.0)
            corr = jnp.sum(dl, axis=-1, keepdims=True) - jnp.sum(do * o, axis=-1, keepdims=True)
            dob = do.astype(BF16)
            ds_c = (p_c * (_bdot(dob, vc, "nt") + corr)).astype(BF16)
            ds_p = (p_p * (_bdot(dob, vp, "nt") + corr)).astype(BF16)
            dq = (_bdot(ds_c, kc, "nn") + _bdot(ds_p, kp, "nn")) * scale
            dk_p, dk_c = _bdot(ds_p, q, "tn") * scale, _bdot(ds_c, q, "tn") * scale
            dv_p, dv_c = _bdot(p_p.astype(BF16), dob, "tn"), _bdot(p_c.astype(BF16), dob, "tn")
            for h in range(A_HEADS):
                sl = _attn_cols(h)
                dq_ref[:, sl] = dq[h]
                dk_ref[:, sl] = ck[:, sl] + dk_p[h]
                dv_ref[:, sl] = cv[:, sl] + dv_p[h]
                ck[:, sl] = dk_c[h]
                cv[:, sl] = dv_c[h]

    last = nblk - 1
    cur = pl.BlockSpec((BAND, A_WIDTH), lambda i: (jnp.minimum(i, last), 0))
    prev = pl.BlockSpec((BAND, A_WIDTH), lambda i: (jnp.minimum(jnp.maximum(i - 1, 0), last), 0))
    body, c_in_specs, c_out_specs, c_out_shape, c_scratch, c_ins = _carry(cargo, 9, 3, 2, body, nblk + 1)
    outs = pl.pallas_call(
        body, name=name, grid=(nblk + 1,),
        in_specs=[cur, prev, cur, prev, cur, cur, cur, cur, cur] + c_in_specs,
        out_specs=[cur, prev, prev] + c_out_specs,
        out_shape=[jax.ShapeDtypeStruct((S, A_WIDTH), F32)] * 3 + c_out_shape,
        scratch_shapes=[pltpu.VMEM((BAND, A_WIDTH), F32)] * 2 + c_scratch,
        compiler_params=_cparams(("arbitrary",)),
    )(q, k, k, v, v, o, lse, do, dlse, *c_ins)
    return outs[0], outs[1], outs[2], outs[3:]


def _merge_fwd_fn(o0, o1, o2, l0, l1, l2):
    m = jnp.maximum(jnp.maximum(l0, l1), l2)
    e0, e1, e2 = jnp.exp(l0 - m), jnp.exp(l1 - m), jnp.exp(l2 - m)
    return (e0 * o0 + e1 * o1 + e2 * o2) / (e0 + e1 + e2)


def _merge_bwd_fn(o0, o1, o2, l0, l1, l2, dom):
    m = jnp.maximum(jnp.maximum(l0, l1), l2)
    e0, e1, e2 = jnp.exp(l0 - m), jnp.exp(l1 - m), jnp.exp(l2 - m)
    den = e0 + e1 + e2
    w0, w1, w2 = e0 / den, e1 / den, e2 / den
    dw0, dw1, dw2 = dom * o0, dom * o1, dom * o2
    mean = w0 * dw0 + w1 * dw1 + w2 * dw2
    return w0 * dom, w1 * dom, w2 * dom, w0 * (dw0 - mean), w1 * (dw1 - mean), w2 * (dw2 - mean)


def _to_classes(t, d):
    if d == 1:
        return t
    S, C = t.shape
    return t.reshape(S // d, d, C).transpose(1, 0, 2).reshape(S, C)


def _from_classes(t, d):
    if d == 1:
        return t
    S, C = t.shape
    return t.reshape(d, S // d, C).transpose(1, 0, 2).reshape(S, C)


def _rope_lane_tables(positions):
    inv_freq = ROPE_THETA ** (-jnp.arange(0, ROPE_DIM, 2, dtype=F32) / ROPE_DIM)
    ang = positions.astype(F32)[:, None] * inv_freq
    cos, sin = jnp.cos(ang), jnp.sin(ang)
    S = positions.shape[0]
    rest = A_HEAD_DIM - ROPE_DIM
    ct = jnp.concatenate([cos, cos, jnp.ones((S, rest), F32)], axis=1)
    st = jnp.concatenate([-sin, sin, jnp.zeros((S, rest), F32)], axis=1)
    return jnp.tile(ct, (1, 2)), jnp.tile(st, (1, 2))


def _head_mean_matrix():
    r = jnp.arange(A_WIDTH) // A_HEAD_DIM
    return (r[:, None] == r[None, :]).astype(F32) * (1.0 / A_HEAD_DIM)


CONV_LANES = 1024
CONV_BWD_LANES = 512
PAST = CONV_WIDTH - 1


def _strip_starts(tm, strip):
    return range(0, tm, strip)


def _conv_fwd(name, proj, w):
    S = proj.shape[0]
    tm, tc = min(512, S), CONV_LANES
    per8 = tm // 8
    off = DN_QKV0 // tc

    def body(x_ref, halo_ref, w_ref, o_ref, xs):
        i = pl.program_id(0)
        xs[0:8, :] = jnp.where(i > 0, halo_ref[...], 0.0)
        xs[8:, :] = x_ref[...]
        strip = 16
        for r0 in _strip_starts(tm, strip):
            acc = w_ref[PAST:CONV_WIDTH, :] * x_ref[pl.ds(r0, strip), :]
            for j in range(PAST):
                acc = acc + w_ref[j:j + 1, :] * xs[pl.ds(8 - PAST + j + r0, strip), :]
            o_ref[pl.ds(r0, strip), :] = acc

    return pl.pallas_call(
        body, name=name, grid=(S // tm, DN_QKV // tc),
        in_specs=[pl.BlockSpec((tm, tc), lambda i, j: (i, j + off)),
                  pl.BlockSpec((8, tc), lambda i, j: (jnp.maximum(i * per8 - 1, 0), j + off)),
                  pl.BlockSpec((CONV_WIDTH, tc), lambda i, j: (0, j))],
        out_specs=pl.BlockSpec((tm, tc), lambda i, j: (i, j)),
        out_shape=jax.ShapeDtypeStruct((S, DN_QKV), F32),
        scratch_shapes=[pltpu.VMEM((tm + 8, tc), F32)],
        compiler_params=_cparams(("parallel", "parallel")),
    )(proj, proj, w)


def _conv_bwd(name, proj, dpre, w):
    S = proj.shape[0]
    tm, tc = min(512, S), CONV_BWD_LANES
    per8 = tm // 8
    off = DN_QKV0 // tc
    last8 = S // 8 - 1
    nrow = S // tm

    def body(x_ref, xh_ref, d_ref, dh_ref, w_ref, dx_ref, dw_ref, xs, ds):
        i = pl.program_id(1)
        xs[0:8, :] = jnp.where(i > 0, xh_ref[...], 0.0)
        xs[8:, :] = x_ref[...]
        ds[0:tm, :] = d_ref[...]
        ds[tm:, :] = jnp.where(i < nrow - 1, dh_ref[...], 0.0)
        strip = 16
        sums = [jnp.zeros((8, tc), F32)] * CONV_WIDTH
        for r0 in _strip_starts(tm, strip):
            rows = pl.ds(r0, strip)
            d = d_ref[rows, :]
            acc = w_ref[PAST:CONV_WIDTH, :] * d
            for j in range(PAST):
                acc = acc + w_ref[j:j + 1, :] * ds[pl.ds(r0 + PAST - j, strip), :]
            dx_ref[rows, :] = acc.astype(dx_ref.dtype)
            taps = [xs[pl.ds(8 - PAST + j + r0, strip), :] for j in range(PAST)] + [x_ref[rows, :]]
            for j, tap in enumerate(taps):
                prod = d * tap
                sums[j] = sums[j] + (prod[0:8] + prod[8:16])

        @pl.when(i == 0)
        def _():
            dw_ref[...] = jnp.zeros_like(dw_ref)

        for j in range(CONV_WIDTH):
            dw_ref[j:j + 1, :] += jnp.sum(sums[j], axis=0, keepdims=True)

    return pl.pallas_call(
        body, name=name, grid=(DN_QKV // tc, nrow),
        in_specs=[pl.BlockSpec((tm, tc), lambda j, i: (i, j + off)),
                  pl.BlockSpec((8, tc), lambda j, i: (jnp.maximum(i * per8 - 1, 0), j + off)),
                  pl.BlockSpec((tm, tc), lambda j, i: (i, j)),
                  pl.BlockSpec((8, tc), lambda j, i: (jnp.minimum((i + 1) * per8, last8), j)),
                  pl.BlockSpec((CONV_WIDTH, tc), lambda j, i: (0, j))],
        out_specs=[pl.BlockSpec((tm, tc), lambda j, i: (i, j)),
                   pl.BlockSpec((CONV_WIDTH, tc), lambda j, i: (0, j))],
        out_shape=[jax.ShapeDtypeStruct((S, DN_QKV), BF16), jax.ShapeDtypeStruct((CONV_WIDTH, DN_QKV), F32)],
        scratch_shapes=[pltpu.VMEM((tm + 8, tc), F32)] * 2,
        compiler_params=_cparams(("parallel", "arbitrary")),
    )(proj, proj, dpre, dpre, w)


def _gate_lane(shape):
    return lax.broadcasted_iota(jnp.int32, shape, 1)


GATES_ROWS = 256


def _chunk_cumsum_matrix():
    r = jnp.arange(GATES_ROWS)
    return ((r[:, None] >= r[None, :]) & (r[:, None] // CHUNK == r[None, :] // CHUNK)).astype(F32)


def _gates_fwd_fn(ab, alog, dt, cum):
    g = -jnp.exp(alog) * _softplus(ab + dt)
    gc = jnp.dot(cum, g, precision=HIGHEST, preferred_element_type=F32)
    return jnp.where(_gate_lane(ab.shape) < DN_HEADS, gc, _sigmoid(ab))


def _gates_bwd_fn(ab, dgb, alog, dt, cum):
    lane = _gate_lane(ab.shape)
    is_g = lane < DN_HEADS
    neg_a = -jnp.exp(alog)
    sp = _softplus(ab + dt)
    dsp = _sigmoid(ab + dt)
    beta = _sigmoid(ab)
    dgc = jnp.where(is_g, dgb, 0.0)
    dg = lax.dot_general(cum, dgc, (_DIMS["tn"], ((), ())), precision=HIGHEST, preferred_element_type=F32)
    dab = jnp.where(is_g, dg * neg_a * dsp, jnp.where(lane < 2 * DN_HEADS, dgb * beta * (1.0 - beta), 0.0))
    d_alog = jnp.sum(dg * neg_a * sp, axis=0, keepdims=True)
    d_dt = jnp.sum(dg * neg_a * dsp, axis=0, keepdims=True)
    return dab, d_alog, d_dt


def _chunk_math(precision):
    def dg(a, b, mode, prec=precision):
        return _bdot(a, b, mode, prec)

    @jax.custom_vjp
    def nn(a, b):
        return dg(a, b, "nn")

    @jax.custom_vjp
    def nt(a, b):
        return dg(a, b, "nt")

    @jax.custom_vjp
    def tn(a, b):
        return dg(a, b, "tn")

    nn.defvjp(lambda a, b: (nn(a, b), (a, b)), lambda r, g: (nt(g, r[1]), tn(r[0], g)))
    nt.defvjp(lambda a, b: (nt(a, b), (a, b)), lambda r, g: (nn(g, r[1]), tn(g, r[0])))
    tn.defvjp(lambda a, b: (tn(a, b), (a, b)), lambda r, g: (nt(r[1], g), nn(r[0], g)))

    def split(x):
        hi = x.astype(BF16)
        return hi, (x - hi.astype(F32)).astype(BF16)

    def fine(a, b, mode):
        ah, al = split(a)
        bh, bl = split(b)
        return dg(ah, bh, mode, None) + (dg(ah, bl, mode, None) + dg(al, bh, mode, None))

    def unit_lower_inverse(a):
        row = lax.broadcasted_iota(jnp.int32, a.shape, 1)
        col = lax.broadcasted_iota(jnp.int32, a.shape, 2)
        x = -a
        p = jnp.where(row == col, 1.0, 0.0) + x
        for _ in range(int(math.log2(CHUNK)) - 1):
            x = fine(x, x, "nn")
            p = p + fine(p, x, "nn")
        return p

    @jax.custom_vjp
    def solve2(a, ti, r1, r2):
        return dg(ti, r1, "nn"), dg(ti, r2, "nn")

    def solve2_fwd(a, ti, r1, r2):
        s1, s2 = dg(ti, r1, "nn"), dg(ti, r2, "nn")
        return (s1, s2), (ti, s1, s2)

    def solve2_bwd(res, g):
        ti, s1, s2 = res
        d1, d2 = dg(ti, g[0], "tn"), dg(ti, g[1], "tn")
        return -(dg(d1, s1, "nt") + dg(d2, s2, "nt")), jnp.zeros_like(ti), d1, d2

    solve2.defvjp(solve2_fwd, solve2_bwd)

    def chunk_fn(pq, pk, pv, z, g_col, b_col, g_row, ogain, s_in, inverse=None):
        nb = pq.shape[0]
        sq = (nb, CHUNK, CHUNK)
        row = lax.broadcasted_iota(jnp.int32, sq, 1)
        col = lax.broadcasted_iota(jnp.int32, sq, 2)
        lower, strict = row >= col, row > col
        q, k, v = _silu(pq), _silu(pk), _silu(pv)
        q = q * lax.rsqrt(jnp.sum(q * q, axis=-1, keepdims=True) + EPS) * (DN_HEAD_DIM ** -0.5)
        k = k * lax.rsqrt(jnp.sum(k * k, axis=-1, keepdims=True) + EPS)
        gc_wide = jnp.broadcast_to(g_col, pq.shape)
        gc_i = jnp.broadcast_to(g_col, sq)
        gc_j = jnp.broadcast_to(g_row, sq)
        is_last = lax.broadcasted_iota(jnp.int32, pq.shape, 1) == CHUNK - 1
        g_last = jnp.sum(jnp.where(is_last, gc_wide, 0.0), axis=1, keepdims=True)
        decay = jnp.exp(jnp.where(lower, gc_i - gc_j, -jnp.inf))
        kb = k * b_col
        a_mat = jnp.where(strict, nt(kb, k) * decay, 0.0)
        eg = jnp.exp(gc_wide)
        ti = unit_lower_inverse(a_mat) if inverse is None else inverse
        u, w = solve2(a_mat, ti, v * b_col, kb * eg)
        attn = nt(q, k) * decay
        q_dec = q * eg
        k_dec = k * jnp.exp(g_last - gc_wide)
        c_dec = jnp.exp(g_last)
        v_new = u - nn(w, s_in)
        o = nn(q_dec, s_in) + nn(attn, v_new)
        s_out = s_in * c_dec + tn(k_dec, v_new)
        y = o * lax.rsqrt(jnp.mean(o * o, axis=-1, keepdims=True) + EPS) * ogain * _silu(z)
        return (y, s_out, ti) if inverse is None else (y, s_out)

    return chunk_fn


DN_PRECISION = None


def _chunk_specs(n_of):
    groups = DN_HEADS // DN_HB
    wide = DN_HB * DN_HEAD_DIM
    hd = pl.BlockSpec((CHUNK, wide), lambda h, n: (n_of(n), h))
    specs = dict(
        pq=hd,
        pk=pl.BlockSpec((CHUNK, wide), lambda h, n: (n_of(n), groups + h)),
        pv=pl.BlockSpec((CHUNK, wide), lambda h, n: (n_of(n), 2 * groups + h)),
        z=hd,
        gates=pl.BlockSpec((CHUNK, DN_AB_PAD), lambda h, n: (n_of(n), 0)),
        row=pl.BlockSpec((DN_HB, None, 1, CHUNK), lambda h, n: (h, n_of(n), 0, 0)),
        gain=pl.BlockSpec((1, DN_HEAD_DIM), lambda h, n: (0, 0)),
        state=pl.BlockSpec((DN_HB, None, DN_HEAD_DIM, DN_HEAD_DIM), lambda h, n: (h, n_of(n), 0, 0)),
        inverse=pl.BlockSpec((DN_HB, None, CHUNK, CHUNK), lambda h, n: (h, n_of(n), 0, 0)),
        qkv=pl.BlockSpec((CHUNK, DN_QKV), lambda h, n: (n_of(n), 0)),
        head=hd,
    )
    return specs


def _head_cols(j):
    return slice(j * DN_HEAD_DIM, (j + 1) * DN_HEAD_DIM)


def _split_heads(ref):
    return jnp.stack([ref[:, _head_cols(j)] for j in range(DN_HB)])


def _gate_columns(gates, first_lane):
    lane = lax.broadcasted_iota(jnp.int32, gates.shape, 1)
    return jnp.stack([jnp.sum(jnp.where(lane == first_lane + h, gates, 0.0), axis=-1, keepdims=True)
                      for h in range(DN_HEADS)])


def _gate_lanes(columns, first_lane):
    shape = (columns.shape[1], DN_AB_PAD)
    lane = lax.broadcasted_iota(jnp.int32, shape, 1)
    out = jnp.zeros(shape, F32)
    for h in range(DN_HEADS):
        out = out + jnp.where(lane == first_lane + h, columns[h], 0.0)
    return out


def _chunk_fwd(name, pre, proj, gates, g_row, ogain):
    assert DN_HB == DN_HEADS
    S = pre.shape[0]
    N = S // CHUNK
    chunk_fn = _chunk_math(DN_PRECISION)
    sp = _chunk_specs(lambda n: n)

    def body(pq, pk, pv, z, gb, gr, og, y_ref, sin_ref, inv_ref, st):
        @pl.when(pl.program_id(1) == 0)
        def _():
            st[...] = jnp.zeros_like(st)

        s_in = st[...]
        sin_ref[...] = s_in
        y, s_out, inverse = chunk_fn(_split_heads(pq), _split_heads(pk), _split_heads(pv), _split_heads(z),
                                     _gate_columns(gb[...], 0), _gate_columns(gb[...], DN_HEADS), gr[...],
                                     og[...], s_in)
        for j in range(DN_HB):
            y_ref[:, _head_cols(j)] = y[j].astype(y_ref.dtype)
        inv_ref[...] = inverse
        st[...] = s_out

    return pl.pallas_call(
        body, name=name, grid=(DN_HEADS // DN_HB, N),
        in_specs=[sp["pq"], sp["pk"], sp["pv"], sp["z"], sp["gates"], sp["row"], sp["gain"]],
        out_specs=[sp["head"], sp["state"], sp["inverse"]],
        out_shape=[jax.ShapeDtypeStruct((S, DN_WIDTH), BF16),
                   jax.ShapeDtypeStruct((DN_HEADS, N, DN_HEAD_DIM, DN_HEAD_DIM), F32),
                   jax.ShapeDtypeStruct((DN_HEADS, N, CHUNK, CHUNK), F32)],
        scratch_shapes=[pltpu.VMEM((DN_HB, DN_HEAD_DIM, DN_HEAD_DIM), F32)],
        compiler_params=_cparams(("parallel", "arbitrary")),
    )(pre, pre, pre, proj, gates, g_row, ogain)


def _chunk_bwd(name, pre, proj, gates, g_row, ogain, s_in_all, inverse_all, dy):
    assert DN_HB == DN_HEADS
    S = pre.shape[0]
    N = S // CHUNK
    chunk_fn = _chunk_math(DN_PRECISION)
    sp = _chunk_specs(lambda n: N - 1 - n)

    def body(pq, pk, pv, z, gb, gr, og, sin_ref, inv_ref, dy_ref,
             dpre_ref, dz_ref, dgb_ref, dgr_ref, dog_ref, ds):
        @pl.when(pl.program_id(1) == 0)
        def _():
            ds[...] = jnp.zeros_like(ds)
            dog_ref[...] = jnp.zeros_like(dog_ref)

        inverse = inv_ref[...]
        prim = (_split_heads(pq), _split_heads(pk), _split_heads(pv), _split_heads(z),
                _gate_columns(gb[...], 0), _gate_columns(gb[...], DN_HEADS), gr[...], og[...], sin_ref[...])
        _, vjp = jax.vjp(lambda *a: chunk_fn(*a, inverse=inverse), *prim)
        gq, gk, gv, gz, ggc, gbc, ggr, gog, gs = vjp((_split_heads(dy_ref), ds[...]))
        for j in range(DN_HB):
            for part, g in enumerate((gq, gk, gv)):
                dpre_ref[:, pl.ds(part * DN_WIDTH + j * DN_HEAD_DIM, DN_HEAD_DIM)] = g[j]
            dz_ref[:, _head_cols(j)] = gz[j].astype(dz_ref.dtype)
        dgb_ref[...] = _gate_lanes(ggc, 0) + _gate_lanes(gbc, DN_HEADS)
        dgr_ref[...] = ggr
        dog_ref[...] += gog
        ds[...] = gs

    hd = sp["head"]
    return pl.pallas_call(
        body, name=name, grid=(1, N),
        in_specs=[sp["pq"], sp["pk"], sp["pv"], sp["z"], sp["gates"], sp["row"], sp["gain"],
                  sp["state"], sp["inverse"], hd],
        out_specs=[sp["qkv"], hd, sp["gates"], sp["row"], sp["gain"]],
        out_shape=[jax.ShapeDtypeStruct((S, DN_QKV), F32), jax.ShapeDtypeStruct((S, DN_WIDTH), BF16),
                   jax.ShapeDtypeStruct((S, DN_AB_PAD), F32),
                   jax.ShapeDtypeStruct((DN_HEADS, N, 1, CHUNK), F32), jax.ShapeDtypeStruct((1, DN_HEAD_DIM), F32)],
        scratch_shapes=[pltpu.VMEM((DN_HB, DN_HEAD_DIM, DN_HEAD_DIM), F32)],
        compiler_params=_cparams(("arbitrary", "arbitrary")),
    )(pre, pre, pre, proj, gates, g_row, ogain, s_in_all, inverse_all, dy)


def _mm_rms_bwd(name, d_out, w, x, gain, dres, matmul_copy=True):
    first = 0 if matmul_copy else 1
    return _mm(name, d_out, w, "nt", out_dtypes=(F32, BF16)[:2 - first], extras=(x, dres),
               rows=(gain.reshape(1, -1),), n_sums=1, tm_cap=512,
               epilogue=lambda acc, x_, dres_, g: _rms_bwd_fn(x_, dres_, acc, g)[first:])


def _residual_norm_epilogue(acc, res, gain):
    x = acc + res
    return x, _rms_fwd_fn(x, gain)


def _ple_loss_fn(x_mid, pp, zg, t):
    gate = _sigmoid(zg)
    err = x_mid + pp * gate - t
    dy = err * (1.0 / D_MODEL)
    return dy, dy * gate, dy * pp * gate * (1.0 - gate), jnp.broadcast_to(jnp.sum(err * err, keepdims=True), (1, 128))


def _mlp_ple_fwd(tag, x_in, h, p_l, w_up, w_down, norm_ple, w_ple, w_gate, next_gain=None, target=None):
    a = _mm(f"{tag}_up", h, w_up, "nn", out_dtypes=(BF16,), epilogue=_relu2_epilogue)
    x_mid, hg = _mm(f"{tag}_down", a, w_down, "nn", out_dtypes=(F32, BF16), extras=(x_in,),
                    rows=(norm_ple.reshape(1, -1),), epilogue=_residual_norm_epilogue)
    zg = _mm(f"{tag}_gate", hg, w_gate, "nn")
    pp = _mm(f"{tag}_ple", p_l, w_ple, "nn")
    sv = dict(x_in=x_in, h=h, a=a, x_mid=x_mid, hg=hg, zg=zg, pp=pp)
    if target is not None:
        sv["dy"], sv["dpp"], sv["dzg"], sv["sq"] = _rowwise(
            f"{tag}_ple_loss", _ple_loss_fn, [x_mid, pp, zg, target], [],
            [(D_MODEL, F32), (D_MODEL, BF16), (D_MODEL, BF16)], [(1, 128)])
        return None, None, sv
    if next_gain is None:
        return _rowwise(f"{tag}_ple_out", _ple_fwd_fn, [x_mid, pp, zg], [], [(D_MODEL, F32)]), None, sv
    x_out, h_next = _rowwise(f"{tag}_ple_out", _ple_norm_fwd_fn, [x_mid, pp, zg], [next_gain.reshape(1, -1)],
                             [(D_MODEL, F32), (D_MODEL, BF16)])
    return x_out, h_next, sv


def _mlp_ple_bwd(tag, dx, sv, p_l, norm_mlp, w_up, w_down, norm_ple, w_ple, w_gate):
    if "dpp" in sv:
        dpp, dzg = sv["dpp"], sv["dzg"]
    else:
        dpp, dzg = _rowwise(f"{tag}_ple_bwd", _ple_bwd_fn, [dx, sv["pp"], sv["zg"]], [],
                            [(D_MODEL, BF16), (D_MODEL, BF16)])
    d_w_ple = _mm(f"{tag}_d_w_ple", p_l, dpp, "tn", out_dtypes=(BF16,))
    d_w_gate = _mm(f"{tag}_d_w_gate", sv["hg"], dzg, "tn", out_dtypes=(BF16,))
    dx_mid, dx_mid_b, d_norm_ple = _mm_rms_bwd(f"{tag}_d_hg", dzg, w_gate, sv["x_mid"], norm_ple, dx)
    du = _mm(f"{tag}_d_u", dx_mid_b, w_down, "nt", out_dtypes=(BF16,), extras=(sv["a"],),
             epilogue=_relu2_bwd_epilogue)
    d_w_down = _mm(f"{tag}_d_w_down", sv["a"], dx_mid_b, "tn", out_dtypes=(BF16,))
    d_w_up = _mm(f"{tag}_d_w_up", sv["h"], du, "tn", out_dtypes=(BF16,))
    dx_in, dx_in_b, d_norm_mlp = _mm_rms_bwd(f"{tag}_d_h", du, w_up, sv["x_in"], norm_mlp, dx_mid)
    return dx_in, dx_in_b, dict(mlp_norm=d_norm_mlp, w_up=d_w_up, w_down=d_w_down, ple_norm=d_norm_ple,
                                w_ple=d_w_ple, w_ple_gate=d_w_gate)


class _NoHooks:
    first_cargo = None
    fwd_cargo = (None,) * len(SWA_GROUPS)

    def first_weights(self, results):
        return {}

    def weights_from(self, results):
        return {}

    def split_cargo(self, early_grads):
        return None

    def bwd_cargo(self, results):
        return (None,) * len(SWA_GROUPS)

    def last_cargo(self, attn_grads):
        return None


def _with_cargo(result, cargo):
    return (result, ()) if cargo is None else result


def _local_step(x, p, positions, target, small, big, hooks=_NoHooks()):
    S = x.shape[0]
    ct, st = _rope_lane_tables(positions)
    bd = _head_mean_matrix()

    h0, first = _with_cargo(_rms_fwd("l0_mix_norm", x, small["mix_norm"][0], cargo=hooks.first_cargo),
                            hooks.first_cargo)
    big = {**big, **hooks.first_weights(first)}
    attn, brought = [], []
    for g, (window, d) in enumerate(SWA_GROUPS):
        assert window // d == BAND and (S // d) % BAND == 0
        h0g = _to_classes(h0, d)
        ctg, stg = _to_classes(ct, d), _to_classes(st, d)
        w_g = big["attn_w_qkv"][:, g * 3 * A_WIDTH:(g + 1) * 3 * A_WIDTH]
        gq = jnp.tile(small["attn_q_gain"][0, g], A_HEADS).reshape(1, A_WIDTH)
        gk = jnp.tile(small["attn_k_gain"][0, g], A_HEADS).reshape(1, A_WIDTH)
        qkv = _mm(f"l0_qkv{g}", h0g, w_g, "nn")
        q, k, v = _rowwise(f"l0_qk_prep{g}", _qk_prep_fwd_fn, [qkv, ctg, stg], [gq, gk, bd], [(A_WIDTH, BF16)] * 3)
        o, lse, cargo_out = _attn_fwd(f"l0_attn{g}", q, k, v, (S // d) // BAND, cargo=hooks.fwd_cargo[g])
        brought.append(cargo_out)
        attn.append(dict(d=d, h0g=h0g, ct=ctg, st=stg, w=w_g, gq=gq, gk=gk, qkv=qkv, q=q, k=k, v=v, o=o, lse=lse,
                         o_tok=_from_classes(o, d), lse_tok=_from_classes(lse, d)))
    big = {**big, **hooks.weights_from(brought)}
    om = _rowwise("l0_merge", _merge_fwd_fn, [a["o_tok"] for a in attn] + [a["lse_tok"] for a in attn], [],
                  [(A_WIDTH, BF16)])
    x1, h1 = _mm("l0_attn_out", om, big["attn_w_o"], "nn", out_dtypes=(F32, BF16), extras=(x,),
                 rows=(small["mlp_norm"][0].reshape(1, -1),), epilogue=_residual_norm_epilogue)
    x3, h3, sv0 = _mlp_ple_fwd("l0", x1, h1, p[0], big["w_up"][0], big["w_down"][0], small["ple_norm"][0],
                               big["w_ple"][0], big["w_ple_gate"][0], next_gain=small["mix_norm"][1])

    N = S // CHUNK
    proj = _mm("l1_in", h3, big["dn_w_in"], "nn")
    pre = _conv_fwd("l1_conv", proj, small["dn_conv"])
    ab = proj[:, DN_AB0:DN_AB0 + DN_AB_PAD]
    lane_pad = DN_AB_PAD - DN_HEADS
    alog_row = jnp.pad(small["dn_a_log"][0], (0, lane_pad)).reshape(1, DN_AB_PAD)
    dt_row = jnp.pad(small["dn_dt_bias"][0], (0, lane_pad)).reshape(1, DN_AB_PAD)
    cum = _chunk_cumsum_matrix()
    gb = _rowwise("l1_gates", _gates_fwd_fn, [ab], [alog_row, dt_row, cum], [(DN_AB_PAD, F32)], tm=GATES_ROWS)
    g_row = gb[:, :DN_HEADS].T.reshape(DN_HEADS, N, 1, CHUNK)
    ogain = small["dn_o_gain"][0].reshape(1, DN_HEAD_DIM)
    y, s_in_all, inverse_all = _chunk_fwd("l1_delta", pre, proj, gb, g_row, ogain)
    x4, h4 = _mm("l1_dn_out", y, big["dn_w_o"], "nn", out_dtypes=(F32, BF16), extras=(x3,),
                 rows=(small["mlp_norm"][1].reshape(1, -1),), epilogue=_residual_norm_epilogue)
    _, _, sv1 = _mlp_ple_fwd("l1", x4, h4, p[1], big["w_up"][1], big["w_down"][1], small["ple_norm"][1],
                             big["w_ple"][1], big["w_ple_gate"][1], target=target)
    dy, sq = sv1["dy"], sv1["sq"]

    dx4, dx4_b, gl1 = _mlp_ple_bwd("l1", dy, sv1, p[1], small["mlp_norm"][1], big["w_up"][1], big["w_down"][1],
                            small["ple_norm"][1], big["w_ple"][1], big["w_ple_gate"][1])
    d_y = _mm("l1_d_y", dx4_b, big["dn_w_o"], "nt")
    d_dn_w_o = _mm("l1_d_w_o", y, dx4_b, "tn", out_dtypes=(BF16,))
    dpre, dz, dgb_cols, dg_row, d_ogain = _chunk_bwd(
        "l1_delta_bwd", pre, proj, gb, g_row, ogain, s_in_all, inverse_all, d_y)
    dconv_in, d_conv_w = _conv_bwd("l1_conv_bwd", proj, dpre, small["dn_conv"])
    dgb = dgb_cols + jnp.pad(dg_row.reshape(DN_HEADS, S).T, ((0, 0), (0, DN_AB_PAD - DN_HEADS)))
    dab, d_alog, d_dt = _rowwise("l1_gates_bwd", _gates_bwd_fn, [ab, dgb], [alog_row, dt_row, cum],
                                 [(DN_AB_PAD, F32)], [(1, DN_AB_PAD), (1, DN_AB_PAD)], tm=GATES_ROWS)
    dproj = jnp.concatenate([dz, dconv_in, dab.astype(BF16)], axis=1)
    d_dn_w_in = _mm("l1_d_w_in", h3, dproj, "tn", out_dtypes=(BF16,))
    dx3, d_mix1 = _mm_rms_bwd("l1_d_h", dproj, big["dn_w_in"], x3, small["mix_norm"][1], dx4, matmul_copy=False)

    dx1, dx1_b, gl0 = _mlp_ple_bwd("l0", dx3, sv0, p[0], small["mlp_norm"][0], big["w_up"][0], big["w_down"][0],
                            small["ple_norm"][0], big["w_ple"][0], big["w_ple_gate"][0])
    early = dict(
        dn_w_in=jnp.concatenate([d_dn_w_in[:, DN_QKV0:DN_AB0 + 2 * DN_HEADS], d_dn_w_in[:, :DN_WIDTH]], axis=1),
        dn_w_o=d_dn_w_o,
        w_up=jnp.stack([gl0["w_up"], gl1["w_up"]]),
        w_down=jnp.stack([gl0["w_down"], gl1["w_down"]]),
        w_ple=jnp.stack([gl0["w_ple"], gl1["w_ple"]]),
        w_ple_gate=jnp.stack([gl0["w_ple_gate"], gl1["w_ple_gate"]]))
    split_cargo = hooks.split_cargo(early)
    dom = _mm("l0_d_om", dx1_b, big["attn_w_o"], "nt")
    d_attn_w_o = _mm("l0_d_w_o", om, dx1_b, "tn", out_dtypes=(BF16,))
    merged, split = _with_cargo(
        _rowwise("l0_merge_bwd", _merge_bwd_fn, [a["o_tok"] for a in attn] + [a["lse_tok"] for a in attn] + [dom], [],
                 [(A_WIDTH, F32)] * 6, cargo=split_cargo), split_cargo)
    bwd_cargo = hooks.bwd_cargo(split)
    dh0, d_w_qkv, d_gq, d_gk, brought_bwd = [], [], [], [], []
    for g, a in enumerate(attn):
        do_g, dl_g = _to_classes(merged[g], a["d"]), _to_classes(merged[3 + g], a["d"])
        dqn, dkn, dvn, cargo_out = _attn_bwd(f"l0_attn_bwd{g}", a["q"], a["k"], a["v"], a["o"], a["lse"], do_g, dl_g,
                                             (S // a["d"]) // BAND, cargo=bwd_cargo[g])
        brought_bwd.append(cargo_out)
        dqkv, dgq, dgk = _rowwise(f"l0_qk_prep_bwd{g}", _qk_prep_bwd_fn, [a["qkv"], a["ct"], a["st"], dqn, dkn, dvn],
                                  [a["gq"], a["gk"], bd], [(3 * A_WIDTH, BF16)], [(1, A_HEAD_DIM)] * 2)
        d_w_qkv.append(_mm(f"l0_d_w_qkv{g}", a["h0g"], dqkv, "tn", out_dtypes=(BF16,)))
        dh0.append(_from_classes(_mm(f"l0_d_h{g}", dqkv, a["w"], "nt"), a["d"]))
        d_gq.append(dgq)
        d_gk.append(dgk)
    attn_grads = dict(attn_w_qkv=jnp.concatenate(d_w_qkv, axis=1), attn_w_o=d_attn_w_o)
    last_cargo = hooks.last_cargo(attn_grads)
    (grad_x, d_mix0), last = _with_cargo(
        _rms_bwd("l0_mix_norm_bwd", x, small["mix_norm"][0], dx1, dh0, cargo=last_cargo), last_cargo)
    brought_bwd.append(last)

    grads = dict(
        mix_norm=jnp.concatenate([d_mix0, d_mix1], axis=0),
        attn_q_gain=jnp.concatenate(d_gq, axis=0)[None],
        attn_k_gain=jnp.concatenate(d_gk, axis=0)[None],
        **attn_grads,
        dn_conv=d_conv_w,
        dn_a_log=d_alog[:, :DN_HEADS],
        dn_dt_bias=d_dt[:, :DN_HEADS],
        dn_o_gain=d_ogain,
        mlp_norm=jnp.concatenate([gl0["mlp_norm"], gl1["mlp_norm"]], axis=0),
        ple_norm=jnp.concatenate([gl0["ple_norm"], gl1["ple_norm"]], axis=0),
        **early,
    )
    return sq, grad_x, grads, brought_bwd


def _chip_peer(x, y, c, t):
    return (jnp.bitwise_xor(x, t >> 1), jnp.bitwise_xor(y, t & 1), c)


def _place():
    x, y, c = lax.axis_index("x"), lax.axis_index("y"), lax.axis_index("c")
    return x, y, c, 2 * x + y, (x, y, 1 - c)


def _remote(src, dst, send_sem, recv_sem, to):
    return pltpu.make_async_remote_copy(src_ref=src, dst_ref=dst, send_sem=send_sem, recv_sem=recv_sem,
                                        device_id=to, device_id_type=MESH)


def _hbm_call(name, body, ins, out_shape, scratch_shapes):
    any_spec = pl.BlockSpec(memory_space=pl.ANY)
    return pl.pallas_call(body, name=name, out_shape=out_shape, in_specs=[any_spec] * len(ins),
                          out_specs=[any_spec] * len(out_shape), scratch_shapes=scratch_shapes)(*ins)


def _half(n0, which):
    return pl.ds(which * (n0 // 2), n0 // 2)


class _Exchange:
    def __init__(self, ins, out_shape, scratch, start, finish):
        self.ins, self.out_shape, self.scratch, self.start, self.finish = ins, out_shape, scratch, start, finish


def _run_exchange(name, ex):
    n_in, n_out = len(ex.ins), len(ex.out_shape)

    def body(*refs):
        ins, outs, sems = refs[:n_in], refs[n_in:n_in + n_out], refs[n_in + n_out:]
        ex.start(ins, outs, sems)
        ex.finish(ins, outs, sems)

    return _hbm_call(name, body, ex.ins, ex.out_shape, ex.scratch)


def _gather_exchange(shards):
    T = len(shards)

    pairs = [(i, t) for i in range(T) for t in range(1, N_CHIPS)]

    def copies(ins, outs, sems):
        send, recv = sems
        x, y, c, q, sibling = _place()

        def half(i, which):
            return _half(ins[i].shape[0], which)

        def over_ici(i, t):
            return _remote(ins[i].at[half(i, c)], outs[i].at[q, half(i, c)], send.at[i, t - 1], recv.at[i, t - 1],
                           _chip_peer(x, y, c, t))

        def landing(i, t):
            spot = outs[i].at[jnp.bitwise_xor(q, t), half(i, c)]
            return _remote(spot, spot, send.at[i, t - 1], recv.at[i, t - 1], _chip_peer(x, y, c, t))

        def forward(i, t):
            spot = outs[i].at[jnp.bitwise_xor(q, t), half(i, c)]
            return _remote(spot, spot, send.at[i, 2 + t], recv.at[i, 2 + t], sibling)

        def forwarded(i, t):
            spot = outs[i].at[jnp.bitwise_xor(q, t), half(i, 1 - c)]
            return _remote(spot, spot, send.at[i, 2 + t], recv.at[i, 2 + t], sibling)

        return over_ici, landing, forward, forwarded

    def start(ins, outs, sems):
        over_ici = copies(ins, outs, sems)[0]
        for i, t in pairs:
            over_ici(i, t).start()

    def finish(ins, outs, sems):
        over_ici, landing, forward, forwarded = copies(ins, outs, sems)
        for i, t in pairs:
            landing(i, t).wait_recv()
            forward(i, t).start()
        for i, t in pairs:
            forwarded(i, t).wait_recv()
        for i, t in pairs:
            over_ici(i, t).wait_send()
            forward(i, t).wait_send()

    n_rel = 2 * (N_CHIPS - 1)
    return _Exchange(list(shards), [jax.ShapeDtypeStruct((N_CHIPS,) + s.shape, s.dtype) for s in shards],
                     [pltpu.SemaphoreType.DMA((T, n_rel)), pltpu.SemaphoreType.DMA((T, n_rel))], start, finish)


def _scatter_exchange(stacks):
    T = len(stacks)

    def copies(ins, outs, sems):
        send, recv = sems
        x, y, c, q, sibling = _place()
        return [_remote(ins[i].at[jnp.bitwise_xor(q, t)], outs[i].at[t - 1], send.at[i, t - 1], recv.at[i, t - 1],
                        _chip_peer(x, y, c, t)) for i in range(T) for t in range(1, N_CHIPS)]

    def start(ins, outs, sems):
        for cp in copies(ins, outs, sems):
            cp.start()

    def finish(ins, outs, sems):
        for cp in copies(ins, outs, sems):
            cp.wait()

    return _Exchange(list(stacks), [jax.ShapeDtypeStruct((N_CHIPS - 1,) + s.shape[1:], s.dtype) for s in stacks],
                     [pltpu.SemaphoreType.DMA((T, N_CHIPS - 1)), pltpu.SemaphoreType.DMA((T, N_CHIPS - 1))],
                     start, finish)


def _other_half_exchange(stacks):
    T = len(stacks)

    def copies(ins, outs, sems):
        send, recv = sems
        x, y, c, q, sibling = _place()
        return [_remote(ins[i].at[:, _half(ins[i].shape[1], 1 - c)], outs[i], send.at[i], recv.at[i], sibling)
                for i in range(T)]

    def start(ins, outs, sems):
        for cp in copies(ins, outs, sems):
            cp.start()

    def finish(ins, outs, sems):
        for cp in copies(ins, outs, sems):
            cp.wait()

    return _Exchange(list(stacks),
                     [jax.ShapeDtypeStruct((s.shape[0], s.shape[1] // 2) + s.shape[2:], s.dtype) for s in stacks],
                     [pltpu.SemaphoreType.DMA((T,)), pltpu.SemaphoreType.DMA((T,))], start, finish)


def _swap_with_sibling(name, arrays):
    T = len(arrays)

    def body(*refs):
        ins, outs = refs[:T], refs[T:2 * T]
        send, recv = refs[2 * T:]
        x, y, c, q, sibling = _place()
        copies = []
        for i in range(T):
            rc = _remote(ins[i], outs[i], send.at[i], recv.at[i], sibling)
            rc.start()
            copies.append(rc)
        for cp in copies:
            cp.wait()

    return _hbm_call(name, body, arrays, [jax.ShapeDtypeStruct(a.shape, a.dtype) for a in arrays],
                     [pltpu.SemaphoreType.DMA((T,)), pltpu.SemaphoreType.DMA((T,))])


def _gather_from_all(name, block):
    R, C = block.shape

    def body(src, out, send_sems, recv_sems):
        x, y, c = lax.axis_index("x"), lax.axis_index("y"), lax.axis_index("c")
        me = 4 * x + 2 * y + c
        out[me] = src[...]
        copies = []
        for r in range(1, N_DEV):
            peer = (jnp.bitwise_xor(x, r >> 2), jnp.bitwise_xor(y, (r >> 1) & 1), jnp.bitwise_xor(c, r & 1))
            cp = pltpu.make_async_remote_copy(src_ref=src, dst_ref=out.at[me], send_sem=send_sems.at[r - 1],
                                              recv_sem=recv_sems.at[r - 1], device_id=peer, device_id_type=MESH)
            cp.start()
            copies.append(cp)
        for cp in copies:
            cp.wait()

    return pl.pallas_call(
        body, name=name, out_shape=jax.ShapeDtypeStruct((N_DEV, R, C), block.dtype),
        in_specs=[pl.BlockSpec(memory_space=pltpu.VMEM)], out_specs=pl.BlockSpec(memory_space=pltpu.VMEM),
        scratch_shapes=[pltpu.SemaphoreType.DMA((N_DEV - 1,)), pltpu.SemaphoreType.DMA((N_DEV - 1,))],
    )(block)


def _view(a):
    return a[0] if a.shape[0] == 1 else a


def _view_axis(a, axis):
    return axis - 1 if a.shape[0] == 1 else axis


def _rows(a):
    return a.reshape(-1, a.shape[-1])


def _elementwise(name, fn, ins, out_dtypes, tm):
    specs = []
    for a in ins:
        a, row0 = a if isinstance(a, tuple) else (a, 0)
        specs.append((_rows(a), a.shape[-1], 0, row0))
    shape = ins[0][0].shape if isinstance(ins[0], tuple) else ins[0].shape
    outs = _rowwise(name, fn, specs, [], [(shape[-1], dt) for dt in out_dtypes], tm=tm, n_rows=math.prod(shape[:-1]))
    return outs.reshape(shape) if len(out_dtypes) == 1 else [o.reshape(shape) for o in outs]


SMALL_ROWS = 8
CONV_ROWS = CONV_WIDTH * DN_QKV // D_MODEL
SMALL_GRAD_ROWS = 24


def _pack_small(vals, conv=None):
    tail = jnp.concatenate([vals["attn_q_gain"].reshape(-1), vals["attn_k_gain"].reshape(-1),
                            vals["dn_a_log"].reshape(-1), vals["dn_dt_bias"].reshape(-1),
                            vals["dn_o_gain"].reshape(-1)])
    tail = jnp.pad(tail, (0, D_MODEL - tail.shape[0])).reshape(1, D_MODEL)
    rows = [vals["mix_norm"], vals["mlp_norm"], vals["ple_norm"], tail, jnp.zeros((1, D_MODEL), F32)]
    if conv is not None:
        rows += [conv.reshape(CONV_ROWS, D_MODEL),
                 jnp.zeros((SMALL_GRAD_ROWS - SMALL_ROWS - CONV_ROWS, D_MODEL), F32)]
    return jnp.concatenate(rows, axis=0)


def _unpack_small(block):
    nq = 3 * A_HEAD_DIM
    t = block[6]
    return dict(
        mix_norm=block[0:2], mlp_norm=block[2:4], ple_norm=block[4:6],
        attn_q_gain=t[:nq].reshape(1, 3, A_HEAD_DIM), attn_k_gain=t[nq:2 * nq].reshape(1, 3, A_HEAD_DIM),
        dn_a_log=t[2 * nq:2 * nq + DN_HEADS].reshape(1, DN_HEADS),
        dn_dt_bias=t[2 * nq + DN_HEADS:2 * nq + 2 * DN_HEADS].reshape(1, DN_HEADS),
        dn_o_gain=t[2 * nq + 2 * DN_HEADS:2 * nq + 2 * DN_HEADS + DN_HEAD_DIM].reshape(1, DN_HEAD_DIM))


def kernel(x, p, positions, mix_norm, attn_w_qkv, attn_q_gain, attn_k_gain, attn_w_o, dn_w_in, dn_conv, dn_a_log, dn_dt_bias, dn_o_gain, dn_w_o, mlp_norm, w_up, w_down, ple_norm, w_ple, w_ple_gate, loss_target, m_mix_norm, m_attn_w_qkv, m_attn_q_gain, m_attn_k_gain, m_attn_w_o, m_dn_w_in, m_dn_conv, m_dn_a_log, m_dn_dt_bias, m_dn_o_gain, m_dn_w_o, m_mlp_norm, m_w_up, m_w_down, m_ple_norm, m_w_ple, m_w_ple_gate, v_mix_norm, v_attn_w_qkv, v_attn_q_gain, v_attn_k_gain, v_attn_w_o, v_dn_w_in, v_dn_conv, v_dn_a_log, v_dn_dt_bias, v_dn_o_gain, v_dn_w_o, v_mlp_norm, v_w_up, v_w_down, v_ple_norm, v_w_ple, v_w_ple_gate):
    given = dict(locals())
    w = {n: given[n] for n in WEIGHTS}
    m = {n: given["m_" + n] for n in WEIGHTS}
    v = {n: given["v_" + n] for n in WEIGHTS}
    kinds = ("grad", "delta", "new_m", "new_v")
    axes = {n: _view_axis(w[n], axis) for n, axis in SHARDED if n != "dn_conv"}
    chip = 2 * lax.axis_index("x") + lax.axis_index("y")
    core = lax.axis_index("c")
    shards = {n: _view(w[n]).astype(BF16) for n in axes}

    def whole(n, slots):
        return jnp.concatenate([jnp.where(chip == q, shards[n], slots[q]) for q in range(N_CHIPS)], axis=axes[n])

    def stacks_of(grads_of):
        return [jnp.stack(jnp.split(g, N_CHIPS, axis=axes[n])) for n, g in grads_of.items()]

    def chip_sums_of(names, stacks, theirs):
        mine = [lax.dynamic_slice_in_dim(s, core * (s.shape[1] // 2), s.shape[1] // 2, axis=1) for s in stacks]
        return {n: _elementwise(f"add_core_{n}", lambda a, b: a.astype(F32) + b.astype(F32), [a, b], [BF16], 128)
                for n, a, b in zip(names, mine, theirs)}

    class Hooks:
        first_cargo = _gather_exchange([shards[n] for n in ATTN_MATRICES])
        fwd_cargo = [_gather_exchange([shards[n] for n in group]) for group in CARGO_GROUPS]
        chip_sums = {}
        early = None

        def first_weights(self, results):
            return {n: whole(n, slots) for n, slots in zip(ATTN_MATRICES, results)}

        def weights_from(self, results):
            full = {n: whole(n, slots) for group, res in zip(CARGO_GROUPS, results) for n, slots in zip(group, res)}
            w_in, n_ab = full["dn_w_in"], 2 * DN_HEADS
            full["dn_w_in"] = jnp.concatenate([w_in[:, DN_QKV + n_ab:], w_in[:, :DN_QKV + n_ab],
                                               jnp.zeros((D_MODEL, DN_AB_PAD - n_ab), BF16)], axis=1)
            return full

        def split_cargo(self, early_grads):
            self.early = (list(early_grads), stacks_of(early_grads))
            return _other_half_exchange(self.early[1])

        def bwd_cargo(self, results):
            self.chip_sums.update(chip_sums_of(*self.early, results))
            return [_scatter_exchange([self.chip_sums[n] for n in group]) for group in CARGO_GROUPS]

        def last_cargo(self, attn_grads):
            stacks = stacks_of(attn_grads)
            theirs = _run_exchange("split_core_grads_attn", _other_half_exchange(stacks))
            self.chip_sums.update(chip_sums_of(list(attn_grads), stacks, theirs))
            return _scatter_exchange([self.chip_sums[n] for n in attn_grads])

    hooks = Hooks()
    big = {}
    conv_block = jnp.pad(w["dn_conv"].reshape(-1), (0, SMALL_ROWS * D_MODEL - w["dn_conv"].size))
    conv_all = _gather_from_all("gather_conv", conv_block.reshape(SMALL_ROWS, D_MODEL))
    conv_all = conv_all.reshape(N_CHIPS, 2, -1)[:, 0, :w["dn_conv"].size]
    conv_full = jnp.concatenate([conv_all[q].reshape(CONV_WIDTH, -1) for q in range(N_CHIPS)], axis=1)
    small = {n: w[n] for n in REPLICATED}
    small["dn_conv"] = conv_full

    sq, grad_x, grads, brought = _local_step(x[0], p[:, 0], positions[0], loss_target[0], small, big, hooks)
    loss = lax.psum(0.5 * sq[0, 0] / D_MODEL, ("x", "y", "c"))
    out = {}

    landed = {n: r for group, res in zip(CARGO_GROUPS + (ATTN_MATRICES,), brought) for n, r in zip(group, res)}
    half_sums = []
    for n in axes:
        o = lax.dynamic_index_in_dim(hooks.chip_sums[n], chip, axis=0, keepdims=False)
        per = math.prod(o.shape[:-1])
        r = landed[n]
        half_sums.append(_elementwise(
            f"add_chips_{n}", lambda a, b, c, d: ((a.astype(F32) + b.astype(F32)) + c.astype(F32)) + d.astype(F32),
            [o, (r, 0), (r, per), (r, 2 * per)], [F32], 128))
    other_halves = _swap_with_sibling("join_core_sums", half_sums)
    for n, a, b in zip(axes, half_sums, other_halves):
        g = jnp.where(core == 0, jnp.concatenate([a, b], axis=0), jnp.concatenate([b, a], axis=0))
        shp = w[n].shape
        res = _elementwise(f"adamw_{n}", lambda g, w_, m_, v_: (g,) + _adamw(w_, g, m_, v_),
                           [g.reshape(shp), w[n], m[n], v[n]], [F32] * 4, 512)
        for kind, arr in zip(kinds, res):
            out[kind + "_" + n] = arr.reshape(shp)

    slots = _gather_from_all("gather_small_grads", _pack_small(grads, grads["dn_conv"]))

    def small_body(s_ref, w_ref, m_ref, v_ref, sum_out, g_out, d_out, m_out, v_out):
        total = s_ref[0]
        for d in range(1, N_DEV):
            total = total + s_ref[d]
        sum_out[...] = total
        g = total[:SMALL_ROWS]
        for o, r in zip((g_out, d_out, m_out, v_out), (g,) + _adamw(w_ref[...], g, m_ref[...], v_ref[...])):
            o[...] = r

    res = pl.pallas_call(small_body, name="adamw_replicated",
                         out_shape=[jax.ShapeDtypeStruct((SMALL_GRAD_ROWS, D_MODEL), F32)]
                         + [jax.ShapeDtypeStruct((SMALL_ROWS, D_MODEL), F32)] * 4)(
        slots, _pack_small(w), _pack_small(m), _pack_small(v))
    for kind, block in zip(kinds, res[1:]):
        for n, arr in _unpack_small(block).items():
            out[kind + "_" + n] = arr
    conv_sum = res[0][SMALL_ROWS:SMALL_ROWS + CONV_ROWS].reshape(CONV_WIDTH, DN_QKV)
    cols = DN_QKV // N_CHIPS
    chip = 2 * lax.axis_index("x") + lax.axis_index("y")
    conv_mine = lax.dynamic_slice_in_dim(conv_sum, chip * cols, cols, axis=1)
    res = _elementwise("adamw_dn_conv", lambda g, w_, m_, v_: (g,) + _adamw(w_, g, m_, v_),
                       [conv_mine, w["dn_conv"][0], m["dn_conv"][0], v["dn_conv"][0]], [F32] * 4, CONV_WIDTH)
    for kind, arr in zip(kinds, res):
        out[kind + "_dn_conv"] = arr[None]

    return (loss, grad_x[None],
            *[out["grad_" + n] for n in WEIGHTS], *[out["delta_" + n] for n in WEIGHTS],
            *[out["new_m_" + n] for n in WEIGHTS], *[out["new_v_" + n] for n in WEIGHTS])
```

```python
import functools
import math

import jax
import jax.numpy as jnp
from jax import lax
from jax.experimental import pallas as pl
from jax.experimental.pallas import tpu as pltpu

F32 = jnp.float32
BF16 = jnp.bfloat16
HIGHEST = lax.Precision.HIGHEST

D_MODEL = 1024
EPS = 1e-6
SWA_GROUPS = ((128, 1), (512, 4), (2048, 16))
A_HEADS = 8
A_HEAD_DIM = 64
A_WIDTH = A_HEADS * A_HEAD_DIM
ROPE_DIM = A_HEAD_DIM // 4
ROPE_THETA = 500000.0
BAND = 128
DN_HEADS = 8
DN_HEAD_DIM = 128
DN_WIDTH = DN_HEADS * DN_HEAD_DIM
DN_QKV = 3 * DN_WIDTH
DN_AB_PAD = 128
DN_QKV0 = DN_WIDTH
DN_AB0 = DN_WIDTH + DN_QKV
DN_HB = 8
CONV_WIDTH = 4
CHUNK = 64

ADAM_LR = 0.001
ADAM_B1 = 0.9
ADAM_B2 = 0.999
ADAM_EPS = 1e-08
ADAM_WD = 0.01
ADAM_STEP = 10

N_CHIPS = 4
N_DEV = 8
VMEM_LIMIT = 48 * 1024 * 1024
MESH = pl.DeviceIdType.MESH

SHARDED = (
    ("attn_w_qkv", 2), ("attn_w_o", 2), ("dn_w_in", 2), ("dn_conv", 2), ("dn_w_o", 1),
    ("w_up", 2), ("w_down", 1), ("w_ple", 2), ("w_ple_gate", 1))
ATTN_MATRICES = ("attn_w_qkv", "attn_w_o")
CARGO_GROUPS = (("w_up",), ("w_down",), ("dn_w_in", "dn_w_o", "w_ple", "w_ple_gate"))
REPLICATED = ("mix_norm", "attn_q_gain", "attn_k_gain", "dn_a_log", "dn_dt_bias", "dn_o_gain",
              "mlp_norm", "ple_norm")
WEIGHTS = ("mix_norm", "attn_w_qkv", "attn_q_gain", "attn_k_gain", "attn_w_o", "dn_w_in", "dn_conv",
           "dn_a_log", "dn_dt_bias", "dn_o_gain", "dn_w_o", "mlp_norm", "w_up", "w_down", "ple_norm",
           "w_ple", "w_ple_gate")


def _cparams(sem=None):
    return pltpu.CompilerParams(dimension_semantics=sem, vmem_limit_bytes=VMEM_LIMIT)


def _pick(n, cap, quantum=128):
    best = None
    for t in range(quantum, min(n, cap) + 1, quantum):
        if n % t == 0:
            best = t
    return n if best is None else best


_DIMS = {"nn": ((1,), (0,)), "nt": ((1,), (1,)), "tn": ((0,), (0,))}


def _mm(name, a, b, mode, out_dtypes=(F32,), extras=(), epilogue=None, rows=(), n_sums=0, tm_cap=1024):
    if mode == "nn":
        (M, K), (K2, N) = a.shape, b.shape
    elif mode == "nt":
        (M, K), (N, K2) = a.shape, b.shape
    else:
        (K, M), (K2, N) = a.shape, b.shape
    assert K == K2, (name, a.shape, b.shape)
    tn = _pick(N, 1536)
    if mode == "tn":
        tm, tk = _pick(M, tm_cap), _pick(K, 2048)
    elif tn == N and K > 1536:
        tm, tk = _pick(M, min(tm_cap, 512)), K
    elif tn < N:
        tm, tk = _pick(M, 2 * tm_cap), _pick(K, 1536)
    else:
        tm, tk = _pick(M, tm_cap), _pick(K, 1536)
    nk = K // tk
    assert n_sums == 0 or tn == N, name
    if mode == "nn":
        a_spec = pl.BlockSpec((tm, tk), lambda i, j, k: (i, k))
        b_spec = pl.BlockSpec((tk, tn), lambda i, j, k: (k, j))
    elif mode == "nt":
        a_spec = pl.BlockSpec((tm, tk), lambda i, j, k: (i, k))
        b_spec = pl.BlockSpec((tn, tk), lambda i, j, k: (j, k))
    else:
        a_spec = pl.BlockSpec((tk, tm), lambda i, j, k: (k, i))
        b_spec = pl.BlockSpec((tk, tn), lambda i, j, k: (k, j))
    o_spec = pl.BlockSpec((tm, tn), lambda i, j, k: (i, j))
    r_spec = pl.BlockSpec((1, tn), lambda i, j, k: (0, j))
    n_extra, n_out = len(extras) + len(rows), len(out_dtypes)
    dims = (_DIMS[mode], ((), ()))

    def body(a_ref, b_ref, *rest):
        extra_refs, out_refs = rest[:n_extra], rest[n_extra:n_extra + n_out]
        sum_refs = rest[n_extra + n_out:n_extra + n_out + n_sums]
        i, k = pl.program_id(0), pl.program_id(2)
        part = lax.dot_general(a_ref[...].astype(BF16), b_ref[...].astype(BF16), dims, preferred_element_type=F32)

        def finish(total):
            vals = (total,) if epilogue is None else epilogue(total, *[e[...] for e in extra_refs])
            for o, v in zip(out_refs, vals[:n_out]):
                o[...] = v.astype(o.dtype)
            for s, v in zip(sum_refs, vals[n_out:]):
                @pl.when(i == 0)
                def _():
                    s[...] = v

                @pl.when(i > 0)
                def _():
                    s[...] += v

        if nk == 1:
            finish(part)
            return
        acc = rest[-1]

        @pl.when(k == 0)
        def _():
            acc[...] = part

        @pl.when(jnp.logical_and(k > 0, k < nk - 1))
        def _():
            acc[...] += part

        @pl.when(k == nk - 1)
        def _():
            finish(acc[...] + part)

    outs = pl.pallas_call(
        body, name=name, grid=(M // tm, N // tn, nk),
        in_specs=[a_spec, b_spec] + [o_spec] * len(extras) + [r_spec] * len(rows),
        out_specs=[o_spec] * n_out + [r_spec] * n_sums,
        out_shape=[jax.ShapeDtypeStruct((M, N), dt) for dt in out_dtypes]
        + [jax.ShapeDtypeStruct((1, N), F32)] * n_sums,
        scratch_shapes=[pltpu.VMEM((tm, tn), F32)] if nk > 1 else [],
        compiler_params=_cparams(("arbitrary" if n_sums else "parallel", "parallel", "arbitrary")),
    )(a, b, *extras, *rows)
    return outs[0] if n_out + n_sums == 1 else outs


def _rowwise(name, fn, rows, bcast, row_outs, acc_outs=(), tm=512, n_rows=None, cargo=None):
    rows = [r if isinstance(r, tuple) else (r, r.shape[1], 0) for r in rows]
    rows = [r if len(r) == 4 else r + (0,) for r in rows]
    S = rows[0][0].shape[0] if n_rows is None else n_rows
    tm = min(tm, S)
    assert S % tm == 0 and all(r[3] % tm == 0 for r in rows), (name, S, tm)
    n_row, n_bc, n_ro, n_acc = len(rows), len(bcast), len(row_outs), len(acc_outs)
    in_specs = [pl.BlockSpec((tm, w), functools.partial(lambda i, cb, rb: (i + rb, cb), cb=cb, rb=r0 // tm))
                for _, w, cb, r0 in rows]
    in_specs += [pl.BlockSpec(b.shape, lambda i: (0, 0)) for b in bcast]
    out_specs = [pl.BlockSpec((tm, c), lambda i: (i, 0)) for c, _ in row_outs]
    out_specs += [pl.BlockSpec(s, lambda i: (0, 0)) for s in acc_outs]
    out_shape = [jax.ShapeDtypeStruct((S, c), dt) for c, dt in row_outs]
    out_shape += [jax.ShapeDtypeStruct(s, F32) for s in acc_outs]

    def body(*refs):
        ins = [r[...] for r in refs[:n_row + n_bc]]
        outs = refs[n_row + n_bc:]
        vals = fn(*ins)
        if not isinstance(vals, (tuple, list)):
            vals = (vals,)
        for o, v in zip(outs[:n_ro], vals[:n_ro]):
            o[...] = v.astype(o.dtype)
        if n_acc:
            @pl.when(pl.program_id(0) == 0)
            def _():
                for o in outs[n_ro:]:
                    o[...] = jnp.zeros_like(o)
            for o, v in zip(outs[n_ro:], vals[n_ro:]):
                o[...] += v

    n_own = n_ro + n_acc
    body, c_in_specs, c_out_specs, c_out_shape, c_scratch, c_ins = _carry(cargo, n_row + n_bc, n_own, 0, body, S // tm)
    outs = pl.pallas_call(
        body, name=name, grid=(S // tm,), in_specs=in_specs + c_in_specs, out_specs=out_specs + c_out_specs,
        out_shape=out_shape + c_out_shape, scratch_shapes=c_scratch,
        compiler_params=_cparams(("arbitrary",) if n_acc or cargo is not None else ("parallel",)),
    )(*[r[0] for r in rows], *bcast, *c_ins)
    own = outs[0] if n_own == 1 else outs[:n_own]
    return own if cargo is None else (own, outs[n_own:])


def _sigmoid(x):
    return 1.0 / (1.0 + jnp.exp(-x))


def _silu(x):
    return x * _sigmoid(x)


def _softplus(x):
    return jnp.maximum(x, 0.0) + jnp.log(1.0 + jnp.exp(-jnp.abs(x)))


def _rms_fwd_fn(x, g):
    r = lax.rsqrt(jnp.mean(x * x, axis=-1, keepdims=True) + EPS)
    return (x * r) * g


def _rms_bwd_fn(x, dres, *rest):
    dh, g = sum(rest[:-1]), rest[-1]
    r = lax.rsqrt(jnp.mean(x * x, axis=-1, keepdims=True) + EPS)
    xh = x * r
    dxh = dh * g
    dx = dres + r * (dxh - xh * jnp.mean(dxh * xh, axis=-1, keepdims=True))
    return dx, dx, jnp.sum(dh * xh, axis=0, keepdims=True)


def _rms_fwd(name, x, gain, cargo=None):
    return _rowwise(name, _rms_fwd_fn, [x], [gain.reshape(1, -1)], [(x.shape[1], BF16)], cargo=cargo)


def _rms_bwd(name, x, gain, dres, dhs, cargo=None):
    return _rowwise(name, lambda *a: _rms_bwd_fn(*a)[1:], [x, dres] + list(dhs), [gain.reshape(1, -1)],
                    [(x.shape[1], F32)], [(1, x.shape[1])], cargo=cargo)


def _relu2_epilogue(acc):
    r = jnp.maximum(acc, 0.0)
    return (r * r,)


def _relu2_bwd_epilogue(acc, a):
    return (acc * (2.0 * jnp.sqrt(a.astype(F32))),)


def _ple_fwd_fn(x, pp, zg):
    return x + pp * _sigmoid(zg)


def _ple_norm_fwd_fn(x, pp, zg, gain):
    out = _ple_fwd_fn(x, pp, zg)
    return out, _rms_fwd_fn(out, gain)


def _ple_bwd_fn(dx, pp, zg):
    gate = _sigmoid(zg)
    return dx * gate, dx * pp * gate * (1.0 - gate)


def _adamw(w, g, m, v):
    m = ADAM_B1 * m + (1.0 - ADAM_B1) * g
    v = ADAM_B2 * v + (1.0 - ADAM_B2) * jnp.square(g)
    m_hat = m / (1.0 - ADAM_B1 ** ADAM_STEP)
    v_hat = v / (1.0 - ADAM_B2 ** ADAM_STEP)
    delta = -ADAM_LR * (m_hat / (jnp.sqrt(v_hat) + ADAM_EPS) + ADAM_WD * w)
    return delta, m, v


def _lane_take(x, offset):
    n = x.shape[-1]
    return pltpu.roll(x, (-offset) % n, 1)


def _head_lane(shape):
    return lax.broadcasted_iota(jnp.int32, shape, 1) % A_HEAD_DIM


def _rope_partner(x):
    lane = _head_lane(x.shape)
    return jnp.where(lane < ROPE_DIM // 2, _lane_take(x, ROPE_DIM // 2),
                     jnp.where(lane < ROPE_DIM, _lane_take(x, -(ROPE_DIM // 2)), 0.0))


def _head_mean(x, bd):
    hi = x.astype(BF16)
    lo = (x - hi.astype(F32)).astype(BF16)
    b = bd.astype(BF16)
    return jnp.dot(hi, b, preferred_element_type=F32) + jnp.dot(lo, b, preferred_element_type=F32)


def _fold_heads(row):
    out = row[:, :A_HEAD_DIM]
    for h in range(1, A_HEADS):
        out = out + row[:, h * A_HEAD_DIM:(h + 1) * A_HEAD_DIM]
    return out


def _all_heads(t):
    return jnp.concatenate([t] * (A_WIDTH // t.shape[1]), axis=1)


def _qk_prep_fwd_fn(qkv, ct, st, gq, gk, bd):
    ct, st = _all_heads(ct), _all_heads(st)

    def one(t, g):
        n = t * lax.rsqrt(_head_mean(t * t, bd) + EPS) * g
        return n * ct + _rope_partner(n) * st
    q, k, v = qkv[:, :A_WIDTH], qkv[:, A_WIDTH:2 * A_WIDTH], qkv[:, 2 * A_WIDTH:]
    return one(q, gq), one(k, gk), v


def _qk_prep_bwd_fn(qkv, ct, st, dq, dk, dv, gq, gk, bd):
    ct, st = _all_heads(ct), _all_heads(st)

    def one(t, g, dy):
        r = lax.rsqrt(_head_mean(t * t, bd) + EPS)
        nh = t * r
        dn = dy * ct + _rope_partner(dy * st)
        dg = jnp.sum(dn * nh, axis=0, keepdims=True)
        dnh = dn * g
        return r * (dnh - nh * _head_mean(dnh * nh, bd)), _fold_heads(dg)
    q, k = qkv[:, :A_WIDTH], qkv[:, A_WIDTH:2 * A_WIDTH]
    dq_raw, dgq = one(q, gq, dq)
    dk_raw, dgk = one(k, gk, dk)
    return jnp.concatenate([dq_raw, dk_raw, dv], axis=1), dgq, dgk


_BATCH_DIMS = {"nn": ((2,), (1,)), "nt": ((2,), (2,)), "tn": ((1,), (1,))}


def _bdot(a, b, mode, precision=None):
    return lax.dot_general(a, b, (_BATCH_DIMS[mode], ((0,), (0,))), precision=precision,
                           preferred_element_type=F32)


def _attn_cols(h):
    return slice(h * A_HEAD_DIM, (h + 1) * A_HEAD_DIM)


def _attn_heads(ref):
    return jnp.stack([ref[:, _attn_cols(h)] for h in range(A_HEADS)])


def _band_masks():
    qi = lax.broadcasted_iota(jnp.int32, (BAND, BAND), 0)
    kj = lax.broadcasted_iota(jnp.int32, (BAND, BAND), 1)
    return kj <= qi, kj >= qi


def _attn_fwd(name, q, k, v, blocks_per_class, cargo=None):
    S = q.shape[0]
    nblk = S // BAND
    scale = A_HEAD_DIM ** -0.5

    def body(q_ref, kp_ref, kc_ref, vp_ref, vc_ref, o_ref, l_ref):
        i = pl.program_id(0)
        has_prev = (i % blocks_per_class) != 0
        m_cur, m_prev = _band_masks()
        m_prev = jnp.logical_and(m_prev, has_prev)
        q, kc, kp, vc, vp = (_attn_heads(r) for r in (q_ref, kc_ref, kp_ref, vc_ref, vp_ref))
        s_c = jnp.where(m_cur[None], _bdot(q, kc, "nt") * scale, -jnp.inf)
        s_p = jnp.where(m_prev[None], _bdot(q, kp, "nt") * scale, -jnp.inf)
        m = jnp.maximum(jnp.max(s_c, axis=-1, keepdims=True), jnp.max(s_p, axis=-1, keepdims=True))
        e_c, e_p = jnp.exp(s_c - m), jnp.exp(s_p - m)
        l = jnp.sum(e_c, axis=-1, keepdims=True) + jnp.sum(e_p, axis=-1, keepdims=True)
        o = _bdot((e_c / l).astype(BF16), vc, "nn") + _bdot((e_p / l).astype(BF16), vp, "nn")
        lse = m + jnp.log(l)
        for h in range(A_HEADS):
            o_ref[:, _attn_cols(h)] = o[h]
            l_ref[:, _attn_cols(h)] = jnp.broadcast_to(lse[h], (BAND, A_HEAD_DIM))

    cur = pl.BlockSpec((BAND, A_WIDTH), lambda i: (i, 0))
    prev = pl.BlockSpec((BAND, A_WIDTH), lambda i: (jnp.maximum(i - 1, 0), 0))
    body, c_in_specs, c_out_specs, c_out_shape, c_scratch, c_ins = _carry(cargo, 5, 2, 0, body, nblk)
    outs = pl.pallas_call(
        body, name=name, grid=(nblk,), in_specs=[cur, prev, cur, prev, cur] + c_in_specs,
        out_specs=[cur, cur] + c_out_specs,
        out_shape=[jax.ShapeDtypeStruct((S, A_WIDTH), F32)] * 2 + c_out_shape, scratch_shapes=c_scratch,
        compiler_params=_cparams(("arbitrary",)),
    )(q, k, k, v, v, *c_ins)
    return outs[0], outs[1], outs[2:]


def _carry(cargo, n_in, n_out, n_scratch, body, steps):
    if cargo is None:
        return body, [], [], [], [], []
    n_ci, n_co = len(cargo.ins), len(cargo.out_shape)

    def carrying(*refs):
        refs = list(refs)
        ins, refs = refs[:n_in], refs[n_in:]
        c_ins, refs = refs[:n_ci], refs[n_ci:]
        outs, refs = refs[:n_out], refs[n_out:]
        c_outs, refs = refs[:n_co], refs[n_co:]
        scratch, sems = refs[:n_scratch], refs[n_scratch:]

        @pl.when(pl.program_id(0) == 0)
        def _():
            cargo.start(c_ins, c_outs, sems)

        body(*ins, *outs, *scratch)

        @pl.when(pl.program_id(0) == steps - 1)
        def _():
            cargo.finish(c_ins, c_outs, sems)

    any_spec = pl.BlockSpec(memory_space=pl.ANY)
    return carrying, [any_spec] * n_ci, [any_spec] * n_co, list(cargo.out_shape), list(cargo.scratch), list(cargo.ins)


def _attn_bwd(name, q, k, v, o, lse, do, dlse, blocks_per_class, cargo=None):
    S = q.shape[0]
    nblk = S // BAND
    scale = A_HEAD_DIM ** -0.5

    def body(q_ref, kp_ref, kc_ref, vp_ref, vc_ref, o_ref, l_ref, do_ref, dl_ref,
             dq_ref, dk_ref, dv_ref, ck, cv):
        i = pl.program_id(0)

        @pl.when(i == 0)
        def _():
            ck[...] = jnp.zeros_like(ck)
            cv[...] = jnp.zeros_like(cv)

        @pl.when(i == nblk)
        def _():
            dk_ref[...] = ck[...]
            dv_ref[...] = cv[...]

        @pl.when(i < nblk)
        def _():
            has_prev = (i % blocks_per_class) != 0
            m_cur, m_prev = _band_masks()
            m_prev = jnp.logical_and(m_prev, has_prev)
            q, kc, kp, vc, vp = (_attn_heads(r) for r in (q_ref, kc_ref, kp_ref, vc_ref, vp_ref))
            do, o, dl = _attn_heads(do_ref), _attn_heads(o_ref), _attn_heads(dl_ref)
            lse = jnp.max(_attn_heads(l_ref), axis=-1, keepdims=True)
            p_c = jnp.where(m_cur[None], jnp.exp(_bdot(q, kc, "nt") * scale - lse), 0.0)
            p_p = jnp.where(m_prev[None], jnp.exp(_bdot(q, kp, "nt") * scale - lse), 0.0)
            corr = jnp.sum(dl, axis=-1, keepdims=True) - jnp.sum(do * o, axis=-1, keepdims=True)
            dob = do.astype(BF16)
            ds_c = (p_c * (_bdot(dob, vc, "nt") + corr)).astype(BF16)
            ds_p = (p_p * (_bdot(dob, vp, "nt") + corr)).astype(BF16)
            dq = (_bdot(ds_c, kc, "nn") + _bdot(ds_p, kp, "nn")) * scale
            dk_p, dk_c = _bdot(ds_p, q, "tn") * scale, _bdot(ds_c, q, "tn") * scale
            dv_p, dv_c = _bdot(p_p.astype(BF16), dob, "tn"), _bdot(p_c.astype(BF16), dob, "tn")
            for h in range(A_HEADS):
                sl = _attn_cols(h)
                dq_ref[:, sl] = dq[h]
                dk_ref[:, sl] = ck[:, sl] + dk_p[h]
                dv_ref[:, sl] = cv[:, sl] + dv_p[h]
                ck[:, sl] = dk_c[h]
                cv[:, sl] = dv_c[h]

    last = nblk - 1
    cur = pl.BlockSpec((BAND, A_WIDTH), lambda i: (jnp.minimum(i, last), 0))
    prev = pl.BlockSpec((BAND, A_WIDTH), lambda i: (jnp.minimum(jnp.maximum(i - 1, 0), last), 0))
    body, c_in_specs, c_out_specs, c_out_shape, c_scratch, c_ins = _carry(cargo, 9, 3, 2, body, nblk + 1)
    outs = pl.pallas_call(
        body, name=name, grid=(nblk + 1,),
        in_specs=[cur, prev, cur, prev, cur, cur, cur, cur, cur] + c_in_specs,
        out_specs=[cur, prev, prev] + c_out_specs,
        out_shape=[jax.ShapeDtypeStruct((S, A_WIDTH), F32)] * 3 + c_out_shape,
        scratch_shapes=[pltpu.VMEM((BAND, A_WIDTH), F32)] * 2 + c_scratch,
        compiler_params=_cparams(("arbitrary",)),
    )(q, k, k, v, v, o, lse, do, dlse, *c_ins)
    return outs[0], outs[1], outs[2], outs[3:]


def _merge_fwd_fn(o0, o1, o2, l0, l1, l2):
    m = jnp.maximum(jnp.maximum(l0, l1), l2)
    e0, e1, e2 = jnp.exp(l0 - m), jnp.exp(l1 - m), jnp.exp(l2 - m)
    return (e0 * o0 + e1 * o1 + e2 * o2) / (e0 + e1 + e2)


def _merge_bwd_fn(o0, o1, o2, l0, l1, l2, dom):
    m = jnp.maximum(jnp.maximum(l0, l1), l2)
    e0, e1, e2 = jnp.exp(l0 - m), jnp.exp(l1 - m), jnp.exp(l2 - m)
    den = e0 + e1 + e2
    w0, w1, w2 = e0 / den, e1 / den, e2 / den
    dw0, dw1, dw2 = dom * o0, dom * o1, dom * o2
    mean = w0 * dw0 + w1 * dw1 + w2 * dw2
    return w0 * dom, w1 * dom, w2 * dom, w0 * (dw0 - mean), w1 * (dw1 - mean), w2 * (dw2 - mean)


def _to_classes(t, d):
    if d == 1:
        return t
    S, C = t.shape
    return t.reshape(S // d, d, C).transpose(1, 0, 2).reshape(S, C)


def _from_classes(t, d):
    if d == 1:
        return t
    S, C = t.shape
    return t.reshape(d, S // d, C).transpose(1, 0, 2).reshape(S, C)


def _rope_lane_tables(positions):
    inv_freq = ROPE_THETA ** (-jnp.arange(0, ROPE_DIM, 2, dtype=F32) / ROPE_DIM)
    ang = positions.astype(F32)[:, None] * inv_freq
    cos, sin = jnp.cos(ang), jnp.sin(ang)
    S = positions.shape[0]
    rest = A_HEAD_DIM - ROPE_DIM
    ct = jnp.concatenate([cos, cos, jnp.ones((S, rest), F32)], axis=1)
    st = jnp.concatenate([-sin, sin, jnp.zeros((S, rest), F32)], axis=1)
    return jnp.tile(ct, (1, 2)), jnp.tile(st, (1, 2))


def _head_mean_matrix():
    r = jnp.arange(A_WIDTH) // A_HEAD_DIM
    return (r[:, None] == r[None, :]).astype(F32) * (1.0 / A_HEAD_DIM)


CONV_LANES = 1024
CONV_BWD_LANES = 512
PAST = CONV_WIDTH - 1


def _strip_starts(tm, strip):
    return range(0, tm, strip)


def _conv_fwd(name, proj, w):
    S = proj.shape[0]
    tm, tc = min(512, S), CONV_LANES
    per8 = tm // 8
    off = DN_QKV0 // tc

    def body(x_ref, halo_ref, w_ref, o_ref, xs):
        i = pl.program_id(0)
        xs[0:8, :] = jnp.where(i > 0, halo_ref[...], 0.0)
        xs[8:, :] = x_ref[...]
        strip = 16
        for r0 in _strip_starts(tm, strip):
            acc = w_ref[PAST:CONV_WIDTH, :] * x_ref[pl.ds(r0, strip), :]
            for j in range(PAST):
                acc = acc + w_ref[j:j + 1, :] * xs[pl.ds(8 - PAST + j + r0, strip), :]
            o_ref[pl.ds(r0, strip), :] = acc

    return pl.pallas_call(
        body, name=name, grid=(S // tm, DN_QKV // tc),
        in_specs=[pl.BlockSpec((tm, tc), lambda i, j: (i, j + off)),
                  pl.BlockSpec((8, tc), lambda i, j: (jnp.maximum(i * per8 - 1, 0), j + off)),
                  pl.BlockSpec((CONV_WIDTH, tc), lambda i, j: (0, j))],
        out_specs=pl.BlockSpec((tm, tc), lambda i, j: (i, j)),
        out_shape=jax.ShapeDtypeStruct((S, DN_QKV), F32),
        scratch_shapes=[pltpu.VMEM((tm + 8, tc), F32)],
        compiler_params=_cparams(("parallel", "parallel")),
    )(proj, proj, w)


def _conv_bwd(name, proj, dpre, w):
    S = proj.shape[0]
    tm, tc = min(512, S), CONV_BWD_LANES
    per8 = tm // 8
    off = DN_QKV0 // tc
    last8 = S // 8 - 1
    nrow = S // tm

    def body(x_ref, xh_ref, d_ref, dh_ref, w_ref, dx_ref, dw_ref, xs, ds):
        i = pl.program_id(1)
        xs[0:8, :] = jnp.where(i > 0, xh_ref[...], 0.0)
        xs[8:, :] = x_ref[...]
        ds[0:tm, :] = d_ref[...]
        ds[tm:, :] = jnp.where(i < nrow - 1, dh_ref[...], 0.0)
        strip = 16
        sums = [jnp.zeros((8, tc), F32)] * CONV_WIDTH
        for r0 in _strip_starts(tm, strip):
            rows = pl.ds(r0, strip)
            d = d_ref[rows, :]
            acc = w_ref[PAST:CONV_WIDTH, :] * d
            for j in range(PAST):
                acc = acc + w_ref[j:j + 1, :] * ds[pl.ds(r0 + PAST - j, strip), :]
            dx_ref[rows, :] = acc.astype(dx_ref.dtype)
            taps = [xs[pl.ds(8 - PAST + j + r0, strip), :] for j in range(PAST)] + [x_ref[rows, :]]
            for j, tap in enumerate(taps):
                prod = d * tap
                sums[j] = sums[j] + (prod[0:8] + prod[8:16])

        @pl.when(i == 0)
        def _():
            dw_ref[...] = jnp.zeros_like(dw_ref)

        for j in range(CONV_WIDTH):
            dw_ref[j:j + 1, :] += jnp.sum(sums[j], axis=0, keepdims=True)

    return pl.pallas_call(
        body, name=name, grid=(DN_QKV // tc, nrow),
        in_specs=[pl.BlockSpec((tm, tc), lambda j, i: (i, j + off)),
                  pl.BlockSpec((8, tc), lambda j, i: (jnp.maximum(i * per8 - 1, 0), j + off)),
                  pl.BlockSpec((tm, tc), lambda j, i: (i, j)),
                  pl.BlockSpec((8, tc), lambda j, i: (jnp.minimum((i + 1) * per8, last8), j)),
                  pl.BlockSpec((CONV_WIDTH, tc), lambda j, i: (0, j))],
        out_specs=[pl.BlockSpec((tm, tc), lambda j, i: (i, j)),
                   pl.BlockSpec((CONV_WIDTH, tc), lambda j, i: (0, j))],
        out_shape=[jax.ShapeDtypeStruct((S, DN_QKV), BF16), jax.ShapeDtypeStruct((CONV_WIDTH, DN_QKV), F32)],
        scratch_shapes=[pltpu.VMEM((tm + 8, tc), F32)] * 2,
        compiler_params=_cparams(("parallel", "arbitrary")),
    )(proj, proj, dpre, dpre, w)


def _gate_lane(shape):
    return lax.broadcasted_iota(jnp.int32, shape, 1)


GATES_ROWS = 256


def _chunk_cumsum_matrix():
    r = jnp.arange(GATES_ROWS)
    return ((r[:, None] >= r[None, :]) & (r[:, None] // CHUNK == r[None, :] // CHUNK)).astype(F32)


def _gates_fwd_fn(ab, alog, dt, cum):
    g = -jnp.exp(alog) * _softplus(ab + dt)
    gc = jnp.dot(cum, g, precision=HIGHEST, preferred_element_type=F32)
    return jnp.where(_gate_lane(ab.shape) < DN_HEADS, gc, _sigmoid(ab))


def _gates_bwd_fn(ab, dgb, alog, dt, cum):
    lane = _gate_lane(ab.shape)
    is_g = lane < DN_HEADS
    neg_a = -jnp.exp(alog)
    sp = _softplus(ab + dt)
    dsp = _sigmoid(ab + dt)
    beta = _sigmoid(ab)
    dgc = jnp.where(is_g, dgb, 0.0)
    dg = lax.dot_general(cum, dgc, (_DIMS["tn"], ((), ())), precision=HIGHEST, preferred_element_type=F32)
    dab = jnp.where(is_g, dg * neg_a * dsp, jnp.where(lane < 2 * DN_HEADS, dgb * beta * (1.0 - beta), 0.0))
    d_alog = jnp.sum(dg * neg_a * sp, axis=0, keepdims=True)
    d_dt = jnp.sum(dg * neg_a * dsp, axis=0, keepdims=True)
    return dab, d_alog, d_dt


def _chunk_math(precision):
    def dg(a, b, mode, prec=precision):
        return _bdot(a, b, mode, prec)

    @jax.custom_vjp
    def nn(a, b):
        return dg(a, b, "nn")

    @jax.custom_vjp
    def nt(a, b):
        return dg(a, b, "nt")

    @jax.custom_vjp
    def tn(a, b):
        return dg(a, b, "tn")

    nn.defvjp(lambda a, b: (nn(a, b), (a, b)), lambda r, g: (nt(g, r[1]), tn(r[0], g)))
    nt.defvjp(lambda a, b: (nt(a, b), (a, b)), lambda r, g: (nn(g, r[1]), tn(g, r[0])))
    tn.defvjp(lambda a, b: (tn(a, b), (a, b)), lambda r, g: (nt(r[1], g), nn(r[0], g)))

    def split(x):
        hi = x.astype(BF16)
        return hi, (x - hi.astype(F32)).astype(BF16)

    def fine(a, b, mode):
        ah, al = split(a)
        bh, bl = split(b)
        return dg(ah, bh, mode, None) + (dg(ah, bl, mode, None) + dg(al, bh, mode, None))

    def unit_lower_inverse(a):
        row = lax.broadcasted_iota(jnp.int32, a.shape, 1)
        col = lax.broadcasted_iota(jnp.int32, a.shape, 2)
        x = -a
        p = jnp.where(row == col, 1.0, 0.0) + x
        for _ in range(int(math.log2(CHUNK)) - 1):
            x = fine(x, x, "nn")
            p = p + fine(p, x, "nn")
        return p

    @jax.custom_vjp
    def solve2(a, ti, r1, r2):
        return dg(ti, r1, "nn"), dg(ti, r2, "nn")

    def solve2_fwd(a, ti, r1, r2):
        s1, s2 = dg(ti, r1, "nn"), dg(ti, r2, "nn")
        return (s1, s2), (ti, s1, s2)

    def solve2_bwd(res, g):
        ti, s1, s2 = res
        d1, d2 = dg(ti, g[0], "tn"), dg(ti, g[1], "tn")
        return -(dg(d1, s1, "nt") + dg(d2, s2, "nt")), jnp.zeros_like(ti), d1, d2

    solve2.defvjp(solve2_fwd, solve2_bwd)

    def chunk_fn(pq, pk, pv, z, g_col, b_col, g_row, ogain, s_in, inverse=None):
        nb = pq.shape[0]
        sq = (nb, CHUNK, CHUNK)
        row = lax.broadcasted_iota(jnp.int32, sq, 1)
        col = lax.broadcasted_iota(jnp.int32, sq, 2)
        lower, strict = row >= col, row > col
        q, k, v = _silu(pq), _silu(pk), _silu(pv)
        q = q * lax.rsqrt(jnp.sum(q * q, axis=-1, keepdims=True) + EPS) * (DN_HEAD_DIM ** -0.5)
        k = k * lax.rsqrt(jnp.sum(k * k, axis=-1, keepdims=True) + EPS)
        gc_wide = jnp.broadcast_to(g_col, pq.shape)
        gc_i = jnp.broadcast_to(g_col, sq)
        gc_j = jnp.broadcast_to(g_row, sq)
        is_last = lax.broadcasted_iota(jnp.int32, pq.shape, 1) == CHUNK - 1
        g_last = jnp.sum(jnp.where(is_last, gc_wide, 0.0), axis=1, keepdims=True)
        decay = jnp.exp(jnp.where(lower, gc_i - gc_j, -jnp.inf))
        kb = k * b_col
        a_mat = jnp.where(strict, nt(kb, k) * decay, 0.0)
        eg = jnp.exp(gc_wide)
        ti = unit_lower_inverse(a_mat) if inverse is None else inverse
        u, w = solve2(a_mat, ti, v * b_col, kb * eg)
        attn = nt(q, k) * decay
        q_dec = q * eg
        k_dec = k * jnp.exp(g_last - gc_wide)
        c_dec = jnp.exp(g_last)
        v_new = u - nn(w, s_in)
        o = nn(q_dec, s_in) + nn(attn, v_new)
        s_out = s_in * c_dec + tn(k_dec, v_new)
        y = o * lax.rsqrt(jnp.mean(o * o, axis=-1, keepdims=True) + EPS) * ogain * _silu(z)
        return (y, s_out, ti) if inverse is None else (y, s_out)

    return chunk_fn


DN_PRECISION = None


def _chunk_specs(n_of):
    groups = DN_HEADS // DN_HB
    wide = DN_HB * DN_HEAD_DIM
    hd = pl.BlockSpec((CHUNK, wide), lambda h, n: (n_of(n), h))
    specs = dict(
        pq=hd,
        pk=pl.BlockSpec((CHUNK, wide), lambda h, n: (n_of(n), groups + h)),
        pv=pl.BlockSpec((CHUNK, wide), lambda h, n: (n_of(n), 2 * groups + h)),
        z=hd,
        gates=pl.BlockSpec((CHUNK, DN_AB_PAD), lambda h, n: (n_of(n), 0)),
        row=pl.BlockSpec((DN_HB, None, 1, CHUNK), lambda h, n: (h, n_of(n), 0, 0)),
        gain=pl.BlockSpec((1, DN_HEAD_DIM), lambda h, n: (0, 0)),
        state=pl.BlockSpec((DN_HB, None, DN_HEAD_DIM, DN_HEAD_DIM), lambda h, n: (h, n_of(n), 0, 0)),
        inverse=pl.BlockSpec((DN_HB, None, CHUNK, CHUNK), lambda h, n: (h, n_of(n), 0, 0)),
        qkv=pl.BlockSpec((CHUNK, DN_QKV), lambda h, n: (n_of(n), 0)),
        head=hd,
    )
    return specs


def _head_cols(j):
    return slice(j * DN_HEAD_DIM, (j + 1) * DN_HEAD_DIM)


def _split_heads(ref):
    return jnp.stack([ref[:, _head_cols(j)] for j in range(DN_HB)])


def _gate_columns(gates, first_lane):
    lane = lax.broadcasted_iota(jnp.int32, gates.shape, 1)
    return jnp.stack([jnp.sum(jnp.where(lane == first_lane + h, gates, 0.0), axis=-1, keepdims=True)
                      for h in range(DN_HEADS)])


def _gate_lanes(columns, first_lane):
    shape = (columns.shape[1], DN_AB_PAD)
    lane = lax.broadcasted_iota(jnp.int32, shape, 1)
    out = jnp.zeros(shape, F32)
    for h in range(DN_HEADS):
        out = out + jnp.where(lane == first_lane + h, columns[h], 0.0)
    return out


def _chunk_fwd(name, pre, proj, gates, g_row, ogain):
    assert DN_HB == DN_HEADS
    S = pre.shape[0]
    N = S // CHUNK
    chunk_fn = _chunk_math(DN_PRECISION)
    sp = _chunk_specs(lambda n: n)

    def body(pq, pk, pv, z, gb, gr, og, y_ref, sin_ref, inv_ref, st):
        @pl.when(pl.program_id(1) == 0)
        def _():
            st[...] = jnp.zeros_like(st)

        s_in = st[...]
        sin_ref[...] = s_in
        y, s_out, inverse = chunk_fn(_split_heads(pq), _split_heads(pk), _split_heads(pv), _split_heads(z),
                                     _gate_columns(gb[...], 0), _gate_columns(gb[...], DN_HEADS), gr[...],
                                     og[...], s_in)
        for j in range(DN_HB):
            y_ref[:, _head_cols(j)] = y[j].astype(y_ref.dtype)
        inv_ref[...] = inverse
        st[...] = s_out

    return pl.pallas_call(
        body, name=name, grid=(DN_HEADS // DN_HB, N),
        in_specs=[sp["pq"], sp["pk"], sp["pv"], sp["z"], sp["gates"], sp["row"], sp["gain"]],
        out_specs=[sp["head"], sp["state"], sp["inverse"]],
        out_shape=[jax.ShapeDtypeStruct((S, DN_WIDTH), BF16),
                   jax.ShapeDtypeStruct((DN_HEADS, N, DN_HEAD_DIM, DN_HEAD_DIM), F32),
                   jax.ShapeDtypeStruct((DN_HEADS, N, CHUNK, CHUNK), F32)],
        scratch_shapes=[pltpu.VMEM((DN_HB, DN_HEAD_DIM, DN_HEAD_DIM), F32)],
        compiler_params=_cparams(("parallel", "arbitrary")),
    )(pre, pre, pre, proj, gates, g_row, ogain)


def _chunk_bwd(name, pre, proj, gates, g_row, ogain, s_in_all, inverse_all, dy):
    assert DN_HB == DN_HEADS
    S = pre.shape[0]
    N = S // CHUNK
    chunk_fn = _chunk_math(DN_PRECISION)
    sp = _chunk_specs(lambda n: N - 1 - n)

    def body(pq, pk, pv, z, gb, gr, og, sin_ref, inv_ref, dy_ref,
             dpre_ref, dz_ref, dgb_ref, dgr_ref, dog_ref, ds):
        @pl.when(pl.program_id(1) == 0)
        def _():
            ds[...] = jnp.zeros_like(ds)
            dog_ref[...] = jnp.zeros_like(dog_ref)

        inverse = inv_ref[...]
        prim = (_split_heads(pq), _split_heads(pk), _split_heads(pv), _split_heads(z),
                _gate_columns(gb[...], 0), _gate_columns(gb[...], DN_HEADS), gr[...], og[...], sin_ref[...])
        _, vjp = jax.vjp(lambda *a: chunk_fn(*a, inverse=inverse), *prim)
        gq, gk, gv, gz, ggc, gbc, ggr, gog, gs = vjp((_split_heads(dy_ref), ds[...]))
        for j in range(DN_HB):
            for part, g in enumerate((gq, gk, gv)):
                dpre_ref[:, pl.ds(part * DN_WIDTH + j * DN_HEAD_DIM, DN_HEAD_DIM)] = g[j]
            dz_ref[:, _head_cols(j)] = gz[j].astype(dz_ref.dtype)
        dgb_ref[...] = _gate_lanes(ggc, 0) + _gate_lanes(gbc, DN_HEADS)
        dgr_ref[...] = ggr
        dog_ref[...] += gog
        ds[...] = gs

    hd = sp["head"]
    return pl.pallas_call(
        body, name=name, grid=(1, N),
        in_specs=[sp["pq"], sp["pk"], sp["pv"], sp["z"], sp["gates"], sp["row"], sp["gain"],
                  sp["state"], sp["inverse"], hd],
        out_specs=[sp["qkv"], hd, sp["gates"], sp["row"], sp["gain"]],
        out_shape=[jax.ShapeDtypeStruct((S, DN_QKV), F32), jax.ShapeDtypeStruct((S, DN_WIDTH), BF16),
                   jax.ShapeDtypeStruct((S, DN_AB_PAD), F32),
                   jax.ShapeDtypeStruct((DN_HEADS, N, 1, CHUNK), F32), jax.ShapeDtypeStruct((1, DN_HEAD_DIM), F32)],
        scratch_shapes=[pltpu.VMEM((DN_HB, DN_HEAD_DIM, DN_HEAD_DIM), F32)],
        compiler_params=_cparams(("arbitrary", "arbitrary")),
    )(pre, pre, pre, proj, gates, g_row, ogain, s_in_all, inverse_all, dy)


def _mm_rms_bwd(name, d_out, w, x, gain, dres, matmul_copy=True):
    first = 0 if matmul_copy else 1
    return _mm(name, d_out, w, "nt", out_dtypes=(F32, BF16)[:2 - first], extras=(x, dres),
               rows=(gain.reshape(1, -1),), n_sums=1, tm_cap=512,
               epilogue=lambda acc, x_, dres_, g: _rms_bwd_fn(x_, dres_, acc, g)[first:])


def _residual_norm_epilogue(acc, res, gain):
    x = acc + res
    return x, _rms_fwd_fn(x, gain)


def _ple_loss_fn(x_mid, pp, zg, t):
    gate = _sigmoid(zg)
    err = x_mid + pp * gate - t
    dy = err * (1.0 / D_MODEL)
    return dy, dy * gate, dy * pp * gate * (1.0 - gate), jnp.broadcast_to(jnp.sum(err * err, keepdims=True), (1, 128))


def _mlp_ple_fwd(tag, x_in, h, p_l, w_up, w_down, norm_ple, w_ple, w_gate, next_gain=None, target=None):
    a = _mm(f"{tag}_up", h, w_up, "nn", out_dtypes=(BF16,), epilogue=_relu2_epilogue)
    x_mid, hg = _mm(f"{tag}_down", a, w_down, "nn", out_dtypes=(F32, BF16), extras=(x_in,),
                    rows=(norm_ple.reshape(1, -1),), epilogue=_residual_norm_epilogue)
    zg = _mm(f"{tag}_gate", hg, w_gate, "nn")
    pp = _mm(f"{tag}_ple", p_l, w_ple, "nn")
    sv = dict(x_in=x_in, h=h, a=a, x_mid=x_mid, hg=hg, zg=zg, pp=pp)
    if target is not None:
        sv["dy"], sv["dpp"], sv["dzg"], sv["sq"] = _rowwise(
            f"{tag}_ple_loss", _ple_loss_fn, [x_mid, pp, zg, target], [],
            [(D_MODEL, F32), (D_MODEL, BF16), (D_MODEL, BF16)], [(1, 128)])
        return None, None, sv
    if next_gain is None:
        return _rowwise(f"{tag}_ple_out", _ple_fwd_fn, [x_mid, pp, zg], [], [(D_MODEL, F32)]), None, sv
    x_out, h_next = _rowwise(f"{tag}_ple_out", _ple_norm_fwd_fn, [x_mid, pp, zg], [next_gain.reshape(1, -1)],
                             [(D_MODEL, F32), (D_MODEL, BF16)])
    return x_out, h_next, sv


def _mlp_ple_bwd(tag, dx, sv, p_l, norm_mlp, w_up, w_down, norm_ple, w_ple, w_gate):
    if "dpp" in sv:
        dpp, dzg = sv["dpp"], sv["dzg"]
    else:
        dpp, dzg = _rowwise(f"{tag}_ple_bwd", _ple_bwd_fn, [dx, sv["pp"], sv["zg"]], [],
                            [(D_MODEL, BF16), (D_MODEL, BF16)])
    d_w_ple = _mm(f"{tag}_d_w_ple", p_l, dpp, "tn", out_dtypes=(BF16,))
    d_w_gate = _mm(f"{tag}_d_w_gate", sv["hg"], dzg, "tn", out_dtypes=(BF16,))
    dx_mid, dx_mid_b, d_norm_ple = _mm_rms_bwd(f"{tag}_d_hg", dzg, w_gate, sv["x_mid"], norm_ple, dx)
    du = _mm(f"{tag}_d_u", dx_mid_b, w_down, "nt", out_dtypes=(BF16,), extras=(sv["a"],),
             epilogue=_relu2_bwd_epilogue)
    d_w_down = _mm(f"{tag}_d_w_down", sv["a"], dx_mid_b, "tn", out_dtypes=(BF16,))
    d_w_up = _mm(f"{tag}_d_w_up", sv["h"], du, "tn", out_dtypes=(BF16,))
    dx_in, dx_in_b, d_norm_mlp = _mm_rms_bwd(f"{tag}_d_h", du, w_up, sv["x_in"], norm_mlp, dx_mid)
    return dx_in, dx_in_b, dict(mlp_norm=d_norm_mlp, w_up=d_w_up, w_down=d_w_down, ple_norm=d_norm_ple,
                                w_ple=d_w_ple, w_ple_gate=d_w_gate)


class _NoHooks:
    first_cargo = None
    fwd_cargo = (None,) * len(SWA_GROUPS)

    def first_weights(self, results):
        return {}

    def weights_from(self, results):
        return {}

    def split_cargo(self, early_grads):
        return None

    def bwd_cargo(self, results):
        return (None,) * len(SWA_GROUPS)

    def last_cargo(self, attn_grads):
        return None


def _with_cargo(result, cargo):
    return (result, ()) if cargo is None else result


def _local_step(x, p, positions, target, small, big, hooks=_NoHooks()):
    S = x.shape[0]
    ct, st = _rope_lane_tables(positions)
    bd = _head_mean_matrix()

    h0, first = _with_cargo(_rms_fwd("l0_mix_norm", x, small["mix_norm"][0], cargo=hooks.first_cargo),
                            hooks.first_cargo)
    big = {**big, **hooks.first_weights(first)}
    attn, brought = [], []
    for g, (window, d) in enumerate(SWA_GROUPS):
        assert window // d == BAND and (S // d) % BAND == 0
        h0g = _to_classes(h0, d)
        ctg, stg = _to_classes(ct, d), _to_classes(st, d)
        w_g = big["attn_w_qkv"][:, g * 3 * A_WIDTH:(g + 1) * 3 * A_WIDTH]
        gq = jnp.tile(small["attn_q_gain"][0, g], A_HEADS).reshape(1, A_WIDTH)
        gk = jnp.tile(small["attn_k_gain"][0, g], A_HEADS).reshape(1, A_WIDTH)
        qkv = _mm(f"l0_qkv{g}", h0g, w_g, "nn")
        q, k, v = _rowwise(f"l0_qk_prep{g}", _qk_prep_fwd_fn, [qkv, ctg, stg], [gq, gk, bd], [(A_WIDTH, BF16)] * 3)
        o, lse, cargo_out = _attn_fwd(f"l0_attn{g}", q, k, v, (S // d) // BAND, cargo=hooks.fwd_cargo[g])
        brought.append(cargo_out)
        attn.append(dict(d=d, h0g=h0g, ct=ctg, st=stg, w=w_g, gq=gq, gk=gk, qkv=qkv, q=q, k=k, v=v, o=o, lse=lse,
                         o_tok=_from_classes(o, d), lse_tok=_from_classes(lse, d)))
    big = {**big, **hooks.weights_from(brought)}
    om = _rowwise("l0_merge", _merge_fwd_fn, [a["o_tok"] for a in attn] + [a["lse_tok"] for a in attn], [],
                  [(A_WIDTH, BF16)])
    x1, h1 = _mm("l0_attn_out", om, big["attn_w_o"], "nn", out_dtypes=(F32, BF16), extras=(x,),
                 rows=(small["mlp_norm"][0].reshape(1, -1),), epilogue=_residual_norm_epilogue)
    x3, h3, sv0 = _mlp_ple_fwd("l0", x1, h1, p[0], big["w_up"][0], big["w_down"][0], small["ple_norm"][0],
                               big["w_ple"][0], big["w_ple_gate"][0], next_gain=small["mix_norm"][1])

    N = S // CHUNK
    proj = _mm("l1_in", h3, big["dn_w_in"], "nn")
    pre = _conv_fwd("l1_conv", proj, small["dn_conv"])
    ab = proj[:, DN_AB0:DN_AB0 + DN_AB_PAD]
    lane_pad = DN_AB_PAD - DN_HEADS
    alog_row = jnp.pad(small["dn_a_log"][0], (0, lane_pad)).reshape(1, DN_AB_PAD)
    dt_row = jnp.pad(small["dn_dt_bias"][0], (0, lane_pad)).reshape(1, DN_AB_PAD)
    cum = _chunk_cumsum_matrix()
    gb = _rowwise("l1_gates", _gates_fwd_fn, [ab], [alog_row, dt_row, cum], [(DN_AB_PAD, F32)], tm=GATES_ROWS)
    g_row = gb[:, :DN_HEADS].T.reshape(DN_HEADS, N, 1, CHUNK)
    ogain = small["dn_o_gain"][0].reshape(1, DN_HEAD_DIM)
    y, s_in_all, inverse_all = _chunk_fwd("l1_delta", pre, proj, gb, g_row, ogain)
    x4, h4 = _mm("l1_dn_out", y, big["dn_w_o"], "nn", out_dtypes=(F32, BF16), extras=(x3,),
                 rows=(small["mlp_norm"][1].reshape(1, -1),), epilogue=_residual_norm_epilogue)
    _, _, sv1 = _mlp_ple_fwd("l1", x4, h4, p[1], big["w_up"][1], big["w_down"][1], small["ple_norm"][1],
                             big["w_ple"][1], big["w_ple_gate"][1], target=target)
    dy, sq = sv1["dy"], sv1["sq"]

    dx4, dx4_b, gl1 = _mlp_ple_bwd("l1", dy, sv1, p[1], small["mlp_norm"][1], big["w_up"][1], big["w_down"][1],
                            small["ple_norm"][1], big["w_ple"][1], big["w_ple_gate"][1])
    d_y = _mm("l1_d_y", dx4_b, big["dn_w_o"], "nt")
    d_dn_w_o = _mm("l1_d_w_o", y, dx4_b, "tn", out_dtypes=(BF16,))
    dpre, dz, dgb_cols, dg_row, d_ogain = _chunk_bwd(
        "l1_delta_bwd", pre, proj, gb, g_row, ogain, s_in_all, inverse_all, d_y)
    dconv_in, d_conv_w = _conv_bwd("l1_conv_bwd", proj, dpre, small["dn_conv"])
    dgb = dgb_cols + jnp.pad(dg_row.reshape(DN_HEADS, S).T, ((0, 0), (0, DN_AB_PAD - DN_HEADS)))
    dab, d_alog, d_dt = _rowwise("l1_gates_bwd", _gates_bwd_fn, [ab, dgb], [alog_row, dt_row, cum],
                                 [(DN_AB_PAD, F32)], [(1, DN_AB_PAD), (1, DN_AB_PAD)], tm=GATES_ROWS)
    dproj = jnp.concatenate([dz, dconv_in, dab.astype(BF16)], axis=1)
    d_dn_w_in = _mm("l1_d_w_in", h3, dproj, "tn", out_dtypes=(BF16,))
    dx3, d_mix1 = _mm_rms_bwd("l1_d_h", dproj, big["dn_w_in"], x3, small["mix_norm"][1], dx4, matmul_copy=False)

    dx1, dx1_b, gl0 = _mlp_ple_bwd("l0", dx3, sv0, p[0], small["mlp_norm"][0], big["w_up"][0], big["w_down"][0],
                            small["ple_norm"][0], big["w_ple"][0], big["w_ple_gate"][0])
    early = dict(
        dn_w_in=jnp.concatenate([d_dn_w_in[:, DN_QKV0:DN_AB0 + 2 * DN_HEADS], d_dn_w_in[:, :DN_WIDTH]], axis=1),
        dn_w_o=d_dn_w_o,
        w_up=jnp.stack([gl0["w_up"], gl1["w_up"]]),
        w_down=jnp.stack([gl0["w_down"], gl1["w_down"]]),
        w_ple=jnp.stack([gl0["w_ple"], gl1["w_ple"]]),
        w_ple_gate=jnp.stack([gl0["w_ple_gate"], gl1["w_ple_gate"]]))
    split_cargo = hooks.split_cargo(early)
    dom = _mm("l0_d_om", dx1_b, big["attn_w_o"], "nt")
    d_attn_w_o = _mm("l0_d_w_o", om, dx1_b, "tn", out_dtypes=(BF16,))
    merged, split = _with_cargo(
        _rowwise("l0_merge_bwd", _merge_bwd_fn, [a["o_tok"] for a in attn] + [a["lse_tok"] for a in attn] + [dom], [],
                 [(A_WIDTH, F32)] * 6, cargo=split_cargo), split_cargo)
    bwd_cargo = hooks.bwd_cargo(split)
    dh0, d_w_qkv, d_gq, d_gk, brought_bwd = [], [], [], [], []
    for g, a in enumerate(attn):
        do_g, dl_g = _to_classes(merged[g], a["d"]), _to_classes(merged[3 + g], a["d"])
        dqn, dkn, dvn, cargo_out = _attn_bwd(f"l0_attn_bwd{g}", a["q"], a["k"], a["v"], a["o"], a["lse"], do_g, dl_g,
                                             (S // a["d"]) // BAND, cargo=bwd_cargo[g])
        brought_bwd.append(cargo_out)
        dqkv, dgq, dgk = _rowwise(f"l0_qk_prep_bwd{g}", _qk_prep_bwd_fn, [a["qkv"], a["ct"], a["st"], dqn, dkn, dvn],
                                  [a["gq"], a["gk"], bd], [(3 * A_WIDTH, BF16)], [(1, A_HEAD_DIM)] * 2)
        d_w_qkv.append(_mm(f"l0_d_w_qkv{g}", a["h0g"], dqkv, "tn", out_dtypes=(BF16,)))
        dh0.append(_from_classes(_mm(f"l0_d_h{g}", dqkv, a["w"], "nt"), a["d"]))
        d_gq.append(dgq)
        d_gk.append(dgk)
    attn_grads = dict(attn_w_qkv=jnp.concatenate(d_w_qkv, axis=1), attn_w_o=d_attn_w_o)
    last_cargo = hooks.last_cargo(attn_grads)
    (grad_x, d_mix0), last = _with_cargo(
        _rms_bwd("l0_mix_norm_bwd", x, small["mix_norm"][0], dx1, dh0, cargo=last_cargo), last_cargo)
    brought_bwd.append(last)

    grads = dict(
        mix_norm=jnp.concatenate([d_mix0, d_mix1], axis=0),
        attn_q_gain=jnp.concatenate(d_gq, axis=0)[None],
        attn_k_gain=jnp.concatenate(d_gk, axis=0)[None],
        **attn_grads,
        dn_conv=d_conv_w,
        dn_a_log=d_alog[:, :DN_HEADS],
        dn_dt_bias=d_dt[:, :DN_HEADS],
        dn_o_gain=d_ogain,
        mlp_norm=jnp.concatenate([gl0["mlp_norm"], gl1["mlp_norm"]], axis=0),
        ple_norm=jnp.concatenate([gl0["ple_norm"], gl1["ple_norm"]], axis=0),
        **early,
    )
    return sq, grad_x, grads, brought_bwd


def _chip_peer(x, y, c, t):
    return (jnp.bitwise_xor(x, t >> 1), jnp.bitwise_xor(y, t & 1), c)


def _place():
    x, y, c = lax.axis_index("x"), lax.axis_index("y"), lax.axis_index("c")
    return x, y, c, 2 * x + y, (x, y, 1 - c)


def _remote(src, dst, send_sem, recv_sem, to):
    return pltpu.make_async_remote_copy(src_ref=src, dst_ref=dst, send_sem=send_sem, recv_sem=recv_sem,
                                        device_id=to, device_id_type=MESH)


def _hbm_call(name, body, ins, out_shape, scratch_shapes):
    any_spec = pl.BlockSpec(memory_space=pl.ANY)
    return pl.pallas_call(body, name=name, out_shape=out_shape, in_specs=[any_spec] * len(ins),
                          out_specs=[any_spec] * len(out_shape), scratch_shapes=scratch_shapes)(*ins)


def _half(n0, which):
    return pl.ds(which * (n0 // 2), n0 // 2)


class _Exchange:
    def __init__(self, ins, out_shape, scratch, start, finish):
        self.ins, self.out_shape, self.scratch, self.start, self.finish = ins, out_shape, scratch, start, finish


def _run_exchange(name, ex):
    n_in, n_out = len(ex.ins), len(ex.out_shape)

    def body(*refs):
        ins, outs, sems = refs[:n_in], refs[n_in:n_in + n_out], refs[n_in + n_out:]
        ex.start(ins, outs, sems)
        ex.finish(ins, outs, sems)

    return _hbm_call(name, body, ex.ins, ex.out_shape, ex.scratch)


def _gather_exchange(shards):
    T = len(shards)

    pairs = [(i, t) for i in range(T) for t in range(1, N_CHIPS)]

    def copies(ins, outs, sems):
        send, recv = sems
        x, y, c, q, sibling = _place()

        def half(i, which):
            return _half(ins[i].shape[0], which)

        def over_ici(i, t):
            return _remote(ins[i].at[half(i, c)], outs[i].at[q, half(i, c)], send.at[i, t - 1], recv.at[i, t - 1],
                           _chip_peer(x, y, c, t))

        def landing(i, t):
            spot = outs[i].at[jnp.bitwise_xor(q, t), half(i, c)]
            return _remote(spot, spot, send.at[i, t - 1], recv.at[i, t - 1], _chip_peer(x, y, c, t))

        def forward(i, t):
            spot = outs[i].at[jnp.bitwise_xor(q, t), half(i, c)]
            return _remote(spot, spot, send.at[i, 2 + t], recv.at[i, 2 + t], sibling)

        def forwarded(i, t):
            spot = outs[i].at[jnp.bitwise_xor(q, t), half(i, 1 - c)]
            return _remote(spot, spot, send.at[i, 2 + t], recv.at[i, 2 + t], sibling)

        return over_ici, landing, forward, forwarded

    def start(ins, outs, sems):
        over_ici = copies(ins, outs, sems)[0]
        for i, t in pairs:
            over_ici(i, t).start()

    def finish(ins, outs, sems):
        over_ici, landing, forward, forwarded = copies(ins, outs, sems)
        for i, t in pairs:
            landing(i, t).wait_recv()
            forward(i, t).start()
        for i, t in pairs:
            forwarded(i, t).wait_recv()
        for i, t in pairs:
            over_ici(i, t).wait_send()
            forward(i, t).wait_send()

    n_rel = 2 * (N_CHIPS - 1)
    return _Exchange(list(shards), [jax.ShapeDtypeStruct((N_CHIPS,) + s.shape, s.dtype) for s in shards],
                     [pltpu.SemaphoreType.DMA((T, n_rel)), pltpu.SemaphoreType.DMA((T, n_rel))], start, finish)


def _scatter_exchange(stacks):
    T = len(stacks)

    def copies(ins, outs, sems):
        send, recv = sems
        x, y, c, q, sibling = _place()
        return [_remote(ins[i].at[jnp.bitwise_xor(q, t)], outs[i].at[t - 1], send.at[i, t - 1], recv.at[i, t - 1],
                        _chip_peer(x, y, c, t)) for i in range(T) for t in range(1, N_CHIPS)]

    def start(ins, outs, sems):
        for cp in copies(ins, outs, sems):
            cp.start()

    def finish(ins, outs, sems):
        for cp in copies(ins, outs, sems):
            cp.wait()

    return _Exchange(list(stacks), [jax.ShapeDtypeStruct((N_CHIPS - 1,) + s.shape[1:], s.dtype) for s in stacks],
                     [pltpu.SemaphoreType.DMA((T, N_CHIPS - 1)), pltpu.SemaphoreType.DMA((T, N_CHIPS - 1))],
                     start, finish)


def _other_half_exchange(stacks):
    T = len(stacks)

    def copies(ins, outs, sems):
        send, recv = sems
        x, y, c, q, sibling = _place()
        return [_remote(ins[i].at[:, _half(ins[i].shape[1], 1 - c)], outs[i], send.at[i], recv.at[i], sibling)
                for i in range(T)]

    def start(ins, outs, sems):
        for cp in copies(ins, outs, sems):
            cp.start()

    def finish(ins, outs, sems):
        for cp in copies(ins, outs, sems):
            cp.wait()

    return _Exchange(list(stacks),
                     [jax.ShapeDtypeStruct((s.shape[0], s.shape[1] // 2) + s.shape[2:], s.dtype) for s in stacks],
                     [pltpu.SemaphoreType.DMA((T,)), pltpu.SemaphoreType.DMA((T,))], start, finish)


def _swap_with_sibling(name, arrays):
    T = len(arrays)

    def body(*refs):
        ins, outs = refs[:T], refs[T:2 * T]
        send, recv = refs[2 * T:]
        x, y, c, q, sibling = _place()
        copies = []
        for i in range(T):
            rc = _remote(ins[i], outs[i], send.at[i], recv.at[i], sibling)
            rc.start()
            copies.append(rc)
        for cp in copies:
            cp.wait()

    return _hbm_call(name, body, arrays, [jax.ShapeDtypeStruct(a.shape, a.dtype) for a in arrays],
                     [pltpu.SemaphoreType.DMA((T,)), pltpu.SemaphoreType.DMA((T,))])


def _gather_from_all(name, block):
    R, C = block.shape

    def body(src, out, send_sems, recv_sems):
        x, y, c = lax.axis_index("x"), lax.axis_index("y"), lax.axis_index("c")
        me = 4 * x + 2 * y + c
        out[me] = src[...]
        copies = []
        for r in range(1, N_DEV):
            peer = (jnp.bitwise_xor(x, r >> 2), jnp.bitwise_xor(y, (r >> 1) & 1), jnp.bitwise_xor(c, r & 1))
            cp = pltpu.make_async_remote_copy(src_ref=src, dst_ref=out.at[me], send_sem=send_sems.at[r - 1],
                                              recv_sem=recv_sems.at[r - 1], device_id=peer, device_id_type=MESH)
            cp.start()
            copies.append(cp)
        for cp in copies:
            cp.wait()

    return pl.pallas_call(
        body, name=name, out_shape=jax.ShapeDtypeStruct((N_DEV, R, C), block.dtype),
        in_specs=[pl.BlockSpec(memory_space=pltpu.VMEM)], out_specs=pl.BlockSpec(memory_space=pltpu.VMEM),
        scratch_shapes=[pltpu.SemaphoreType.DMA((N_DEV - 1,)), pltpu.SemaphoreType.DMA((N_DEV - 1,))],
    )(block)


def _view(a):
    return a[0] if a.shape[0] == 1 else a


def _view_axis(a, axis):
    return axis - 1 if a.shape[0] == 1 else axis


def _rows(a):
    return a.reshape(-1, a.shape[-1])


def _elementwise(name, fn, ins, out_dtypes, tm):
    specs = []
    for a in ins:
        a, row0 = a if isinstance(a, tuple) else (a, 0)
        specs.append((_rows(a), a.shape[-1], 0, row0))
    shape = ins[0][0].shape if isinstance(ins[0], tuple) else ins[0].shape
    outs = _rowwise(name, fn, specs, [], [(shape[-1], dt) for dt in out_dtypes], tm=tm, n_rows=math.prod(shape[:-1]))
    return outs.reshape(shape) if len(out_dtypes) == 1 else [o.reshape(shape) for o in outs]


SMALL_ROWS = 8
CONV_ROWS = CONV_WIDTH * DN_QKV // D_MODEL
SMALL_GRAD_ROWS = 24


def _pack_small(vals, conv=None):
    tail = jnp.concatenate([vals["attn_q_gain"].reshape(-1), vals["attn_k_gain"].reshape(-1),
                            vals["dn_a_log"].reshape(-1), vals["dn_dt_bias"].reshape(-1),
                            vals["dn_o_gain"].reshape(-1)])
    tail = jnp.pad(tail, (0, D_MODEL - tail.shape[0])).reshape(1, D_MODEL)
    rows = [vals["mix_norm"], vals["mlp_norm"], vals["ple_norm"], tail, jnp.zeros((1, D_MODEL), F32)]
    if conv is not None:
        rows += [conv.reshape(CONV_ROWS, D_MODEL),
                 jnp.zeros((SMALL_GRAD_ROWS - SMALL_ROWS - CONV_ROWS, D_MODEL), F32)]
    return jnp.concatenate(rows, axis=0)


def _unpack_small(block):
    nq = 3 * A_HEAD_DIM
    t = block[6]
    return dict(
        mix_norm=block[0:2], mlp_norm=block[2:4], ple_norm=block[4:6],
        attn_q_gain=t[:nq].reshape(1, 3, A_HEAD_DIM), attn_k_gain=t[nq:2 * nq].reshape(1, 3, A_HEAD_DIM),
        dn_a_log=t[2 * nq:2 * nq + DN_HEADS].reshape(1, DN_HEADS),
        dn_dt_bias=t[2 * nq + DN_HEADS:2 * nq + 2 * DN_HEADS].reshape(1, DN_HEADS),
        dn_o_gain=t[2 * nq + 2 * DN_HEADS:2 * nq + 2 * DN_HEADS + DN_HEAD_DIM].reshape(1, DN_HEAD_DIM))


def kernel(x, p, positions, mix_norm, attn_w_qkv, attn_q_gain, attn_k_gain, attn_w_o, dn_w_in, dn_conv, dn_a_log, dn_dt_bias, dn_o_gain, dn_w_o, mlp_norm, w_up, w_down, ple_norm, w_ple, w_ple_gate, loss_target, m_mix_norm, m_attn_w_qkv, m_attn_q_gain, m_attn_k_gain, m_attn_w_o, m_dn_w_in, m_dn_conv, m_dn_a_log, m_dn_dt_bias, m_dn_o_gain, m_dn_w_o, m_mlp_norm, m_w_up, m_w_down, m_ple_norm, m_w_ple, m_w_ple_gate, v_mix_norm, v_attn_w_qkv, v_attn_q_gain, v_attn_k_gain, v_attn_w_o, v_dn_w_in, v_dn_conv, v_dn_a_log, v_dn_dt_bias, v_dn_o_gain, v_dn_w_o, v_mlp_norm, v_w_up, v_w_down, v_ple_norm, v_w_ple, v_w_ple_gate):
    given = dict(locals())
    w = {n: given[n] for n in WEIGHTS}
    m = {n: given["m_" + n] for n in WEIGHTS}
    v = {n: given["v_" + n] for n in WEIGHTS}
    kinds = ("grad", "delta", "new_m", "new_v")
    axes = {n: _view_axis(w[n], axis) for n, axis in SHARDED if n != "dn_conv"}
    chip = 2 * lax.axis_index("x") + lax.axis_index("y")
    core = lax.axis_index("c")
    shards = {n: _view(w[n]).astype(BF16) for n in axes}

    def whole(n, slots):
        return jnp.concatenate([jnp.where(chip == q, shards[n], slots[q]) for q in range(N_CHIPS)], axis=axes[n])

    def stacks_of(grads_of):
        return [jnp.stack(jnp.split(g, N_CHIPS, axis=axes[n])) for n, g in grads_of.items()]

    def chip_sums_of(names, stacks, theirs):
        mine = [lax.dynamic_slice_in_dim(s, core * (s.shape[1] // 2), s.shape[1] // 2, axis=1) for s in stacks]
        return {n: _elementwise(f"add_core_{n}", lambda a, b: a.astype(F32) + b.astype(F32), [a, b], [BF16], 512)
                for n, a, b in zip(names, mine, theirs)}

    class Hooks:
        first_cargo = _gather_exchange([shards[n] for n in ATTN_MATRICES])
        fwd_cargo = [_gather_exchange([shards[n] for n in group]) for group in CARGO_GROUPS]
        chip_sums = {}
        early = None

        def first_weights(self, results):
            return {n: whole(n, slots) for n, slots in zip(ATTN_MATRICES, results)}

        def weights_from(self, results):
            full = {n: whole(n, slots) for group, res in zip(CARGO_GROUPS, results) for n, slots in zip(group, res)}
            w_in, n_ab = full["dn_w_in"], 2 * DN_HEADS
            full["dn_w_in"] = jnp.concatenate([w_in[:, DN_QKV + n_ab:], w_in[:, :DN_QKV + n_ab],
                                               jnp.zeros((D_MODEL, DN_AB_PAD - n_ab), BF16)], axis=1)
            return full

        def split_cargo(self, early_grads):
            self.early = (list(early_grads), stacks_of(early_grads))
            return _other_half_exchange(self.early[1])

        def bwd_cargo(self, results):
            self.chip_sums.update(chip_sums_of(*self.early, results))
            return [_scatter_exchange([self.chip_sums[n] for n in group]) for group in CARGO_GROUPS]

        def last_cargo(self, attn_grads):
            stacks = stacks_of(attn_grads)
            theirs = _run_exchange("split_core_grads_attn", _other_half_exchange(stacks))
            self.chip_sums.update(chip_sums_of(list(attn_grads), stacks, theirs))
            return _scatter_exchange([self.chip_sums[n] for n in attn_grads])

    hooks = Hooks()
    big = {}
    conv_block = jnp.pad(w["dn_conv"].reshape(-1), (0, SMALL_ROWS * D_MODEL - w["dn_conv"].size))
    conv_all = _gather_from_all("gather_conv", conv_block.reshape(SMALL_ROWS, D_MODEL))
    conv_all = conv_all.reshape(N_CHIPS, 2, -1)[:, 0, :w["dn_conv"].size]
    conv_full = jnp.concatenate([conv_all[q].reshape(CONV_WIDTH, -1) for q in range(N_CHIPS)], axis=1)
    small = {n: w[n] for n in REPLICATED}
    small["dn_conv"] = conv_full

    sq, grad_x, grads, brought = _local_step(x[0], p[:, 0], positions[0], loss_target[0], small, big, hooks)
    loss = lax.psum(0.5 * sq[0, 0] / D_MODEL, ("x", "y", "c"))
    out = {}

    landed = {n: r for group, res in zip(CARGO_GROUPS + (ATTN_MATRICES,), brought) for n, r in zip(group, res)}
    half_sums = []
    for n in axes:
        o = lax.dynamic_index_in_dim(hooks.chip_sums[n], chip, axis=0, keepdims=False)
        per = math.prod(o.shape[:-1])
        r = landed[n]
        half_sums.append(_elementwise(
            f"add_chips_{n}", lambda a, b, c, d: ((a.astype(F32) + b.astype(F32)) + c.astype(F32)) + d.astype(F32),
            [o, (r, 0), (r, per), (r, 2 * per)], [F32], 512))
    other_halves = _swap_with_sibling("join_core_sums", half_sums)
    for n, a, b in zip(axes, half_sums, other_halves):
        g = jnp.where(core == 0, jnp.concatenate([a, b], axis=0), jnp.concatenate([b, a], axis=0))
        shp = w[n].shape
        res = _elementwise(f"adamw_{n}", lambda g, w_, m_, v_: (g,) + _adamw(w_, g, m_, v_),
                           [g.reshape(shp), w[n], m[n], v[n]], [F32] * 4, 512)
        for kind, arr in zip(kinds, res):
            out[kind + "_" + n] = arr.reshape(shp)

    slots = _gather_from_all("gather_small_grads", _pack_small(grads, grads["dn_conv"]))

    def small_body(s_ref, w_ref, m_ref, v_ref, sum_out, g_out, d_out, m_out, v_out):
        total = s_ref[0]
        for d in range(1, N_DEV):
            total = total + s_ref[d]
        sum_out[...] = total
        g = total[:SMALL_ROWS]
        for o, r in zip((g_out, d_out, m_out, v_out), (g,) + _adamw(w_ref[...], g, m_ref[...], v_ref[...])):
            o[...] = r

    res = pl.pallas_call(small_body, name="adamw_replicated",
                         out_shape=[jax.ShapeDtypeStruct((SMALL_GRAD_ROWS, D_MODEL), F32)]
                         + [jax.ShapeDtypeStruct((SMALL_ROWS, D_MODEL), F32)] * 4)(
        slots, _pack_small(w), _pack_small(m), _pack_small(v))
    for kind, block in zip(kinds, res[1:]):
        for n, arr in _unpack_small(block).items():
            out[kind + "_" + n] = arr
    conv_sum = res[0][SMALL_ROWS:SMALL_ROWS + CONV_ROWS].reshape(CONV_WIDTH, DN_QKV)
    cols = DN_QKV // N_CHIPS
    chip = 2 * lax.axis_index("x") + lax.axis_index("y")
    conv_mine = lax.dynamic_slice_in_dim(conv_sum, chip * cols, cols, axis=1)
    res = _elementwise("adamw_dn_conv", lambda g, w_, m_, v_: (g,) + _adamw(w_, g, m_, v_),
                       [conv_mine, w["dn_conv"][0], m["dn_conv"][0], v["dn_conv"][0]], [F32] * 4, CONV_WIDTH)
    for kind, arr in zip(kinds, res):
        out[kind + "_dn_conv"] = arr[None]

    return (loss, grad_x[None],
            *[out["grad_" + n] for n in WEIGHTS], *[out["delta_" + n] for n in WEIGHTS],
            *[out["new_m_" + n] for n in WEIGHTS], *[out["new_v_" + n] for n in WEIGHTS])
```

```python
import functools
import math

import jax
import jax.numpy as jnp
from jax import lax
from jax.experimental import pallas as pl
from jax.experimental.pallas import tpu as pltpu

F32 = jnp.float32
BF16 = jnp.bfloat16
HIGHEST = lax.Precision.HIGHEST

D_MODEL = 1024
EPS = 1e-6
SWA_GROUPS = ((128, 1), (512, 4), (2048, 16))
A_HEADS = 8
A_HEAD_DIM = 64
A_WIDTH = A_HEADS * A_HEAD_DIM
ROPE_DIM = A_HEAD_DIM // 4
ROPE_THETA = 500000.0
BAND = 128
DN_HEADS = 8
DN_HEAD_DIM = 128
DN_WIDTH = DN_HEADS * DN_HEAD_DIM
DN_QKV = 3 * DN_WIDTH
DN_AB_PAD = 128
DN_QKV0 = DN_WIDTH
DN_AB0 = DN_WIDTH + DN_QKV
DN_HB = 8
CONV_WIDTH = 4
CHUNK = 64

ADAM_LR = 0.001
ADAM_B1 = 0.9
ADAM_B2 = 0.999
ADAM_EPS = 1e-08
ADAM_WD = 0.01
ADAM_STEP = 10

N_CHIPS = 4
N_DEV = 8
VMEM_LIMIT = 48 * 1024 * 1024
MESH = pl.DeviceIdType.MESH

SHARDED = (
    ("attn_w_qkv", 2), ("attn_w_o", 2), ("dn_w_in", 2), ("dn_conv", 2), ("dn_w_o", 1),
    ("w_up", 2), ("w_down", 1), ("w_ple", 2), ("w_ple_gate", 1))
ATTN_MATRICES = ("attn_w_qkv", "attn_w_o")
CARGO_GROUPS = (("w_up",), ("w_down",), ("dn_w_in", "dn_w_o", "w_ple", "w_ple_gate"))
REPLICATED = ("mix_norm", "attn_q_gain", "attn_k_gain", "dn_a_log", "dn_dt_bias", "dn_o_gain",
              "mlp_norm", "ple_norm")
WEIGHTS = ("mix_norm", "attn_w_qkv", "attn_q_gain", "attn_k_gain", "attn_w_o", "dn_w_in", "dn_conv",
           "dn_a_log", "dn_dt_bias", "dn_o_gain", "dn_w_o", "mlp_norm", "w_up", "w_down", "ple_norm",
           "w_ple", "w_ple_gate")


def _cparams(sem=None):
    return pltpu.CompilerParams(dimension_semantics=sem, vmem_limit_bytes=VMEM_LIMIT)


def _pick(n, cap, quantum=128):
    best = None
    for t in range(quantum, min(n, cap) + 1, quantum):
        if n % t == 0:
            best = t
    return n if best is None else best


_DIMS = {"nn": ((1,), (0,)), "nt": ((1,), (1,)), "tn": ((0,), (0,))}


def _mm(name, a, b, mode, out_dtypes=(F32,), extras=(), epilogue=None, rows=(), n_sums=0, tm_cap=1024):
    if mode == "nn":
        (M, K), (K2, N) = a.shape, b.shape
    elif mode == "nt":
        (M, K), (N, K2) = a.shape, b.shape
    else:
        (K, M), (K2, N) = a.shape, b.shape
    assert K == K2, (name, a.shape, b.shape)
    tn = _pick(N, 1536)
    if mode == "tn":
        tm, tk = _pick(M, tm_cap), _pick(K, 2048)
    elif tn == N and K > 1536:
        tm, tk = _pick(M, min(tm_cap, 512)), K
    elif tn < N:
        tm, tk = _pick(M, 2 * tm_cap), _pick(K, 1536)
    else:
        tm, tk = _pick(M, tm_cap), _pick(K, 1536)
    nk = K // tk
    assert n_sums == 0 or tn == N, name
    if mode == "nn":
        a_spec = pl.BlockSpec((tm, tk), lambda i, j, k: (i, k))
        b_spec = pl.BlockSpec((tk, tn), lambda i, j, k: (k, j))
    elif mode == "nt":
        a_spec = pl.BlockSpec((tm, tk), lambda i, j, k: (i, k))
        b_spec = pl.BlockSpec((tn, tk), lambda i, j, k: (j, k))
    else:
        a_spec = pl.BlockSpec((tk, tm), lambda i, j, k: (k, i))
        b_spec = pl.BlockSpec((tk, tn), lambda i, j, k: (k, j))
    o_spec = pl.BlockSpec((tm, tn), lambda i, j, k: (i, j))
    r_spec = pl.BlockSpec((1, tn), lambda i, j, k: (0, j))
    n_extra, n_out = len(extras) + len(rows), len(out_dtypes)
    dims = (_DIMS[mode], ((), ()))

    def body(a_ref, b_ref, *rest):
        extra_refs, out_refs = rest[:n_extra], rest[n_extra:n_extra + n_out]
        sum_refs = rest[n_extra + n_out:n_extra + n_out + n_sums]
        i, k = pl.program_id(0), pl.program_id(2)
        part = lax.dot_general(a_ref[...].astype(BF16), b_ref[...].astype(BF16), dims, preferred_element_type=F32)

        def finish(total):
            vals = (total,) if epilogue is None else epilogue(total, *[e[...] for e in extra_refs])
            for o, v in zip(out_refs, vals[:n_out]):
                o[...] = v.astype(o.dtype)
            for s, v in zip(sum_refs, vals[n_out:]):
                @pl.when(i == 0)
                def _():
                    s[...] = v

                @pl.when(i > 0)
                def _():
                    s[...] += v

        if nk == 1:
            finish(part)
            return
        acc = rest[-1]

        @pl.when(k == 0)
        def _():
            acc[...] = part

        @pl.when(jnp.logical_and(k > 0, k < nk - 1))
        def _():
            acc[...] += part

        @pl.when(k == nk - 1)
        def _():
            finish(acc[...] + part)

    outs = pl.pallas_call(
        body, name=name, grid=(M // tm, N // tn, nk),
        in_specs=[a_spec, b_spec] + [o_spec] * len(extras) + [r_spec] * len(rows),
        out_specs=[o_spec] * n_out + [r_spec] * n_sums,
        out_shape=[jax.ShapeDtypeStruct((M, N), dt) for dt in out_dtypes]
        + [jax.ShapeDtypeStruct((1, N), F32)] * n_sums,
        scratch_shapes=[pltpu.VMEM((tm, tn), F32)] if nk > 1 else [],
        compiler_params=_cparams(("arbitrary" if n_sums else "parallel", "parallel", "arbitrary")),
    )(a, b, *extras, *rows)
    return outs[0] if n_out + n_sums == 1 else outs


def _rowwise(name, fn, rows, bcast, row_outs, acc_outs=(), tm=512, n_rows=None, cargo=None):
    rows = [r if isinstance(r, tuple) else (r, r.shape[1], 0) for r in rows]
    rows = [r if len(r) == 4 else r + (0,) for r in rows]
    S = rows[0][0].shape[0] if n_rows is None else n_rows
    tm = min(tm, S)
    assert S % tm == 0 and all(r[3] % tm == 0 for r in rows), (name, S, tm)
    n_row, n_bc, n_ro, n_acc = len(rows), len(bcast), len(row_outs), len(acc_outs)
    in_specs = [pl.BlockSpec((tm, w), functools.partial(lambda i, cb, rb: (i + rb, cb), cb=cb, rb=r0 // tm))
                for _, w, cb, r0 in rows]
    in_specs += [pl.BlockSpec(b.shape, lambda i: (0, 0)) for b in bcast]
    out_specs = [pl.BlockSpec((tm, c), lambda i: (i, 0)) for c, _ in row_outs]
    out_specs += [pl.BlockSpec(s, lambda i: (0, 0)) for s in acc_outs]
    out_shape = [jax.ShapeDtypeStruct((S, c), dt) for c, dt in row_outs]
    out_shape += [jax.ShapeDtypeStruct(s, F32) for s in acc_outs]

    def body(*refs):
        ins = [r[...] for r in refs[:n_row + n_bc]]
        outs = refs[n_row + n_bc:]
        vals = fn(*ins)
        if not isinstance(vals, (tuple, list)):
            vals = (vals,)
        for o, v in zip(outs[:n_ro], vals[:n_ro]):
            o[...] = v.astype(o.dtype)
        if n_acc:
            @pl.when(pl.program_id(0) == 0)
            def _():
                for o in outs[n_ro:]:
                    o[...] = jnp.zeros_like(o)
            for o, v in zip(outs[n_ro:], vals[n_ro:]):
                o[...] += v

    n_own = n_ro + n_acc
    body, c_in_specs, c_out_specs, c_out_shape, c_scratch, c_ins = _carry(cargo, n_row + n_bc, n_own, 0, body, S // tm)
    outs = pl.pallas_call(
        body, name=name, grid=(S // tm,), in_specs=in_specs + c_in_specs, out_specs=out_specs + c_out_specs,
        out_shape=out_shape + c_out_shape, scratch_shapes=c_scratch,
        compiler_params=_cparams(("arbitrary",) if n_acc or cargo is not None else ("parallel",)),
    )(*[r[0] for r in rows], *bcast, *c_ins)
    own = outs[0] if n_own == 1 else outs[:n_own]
    return own if cargo is None else (own, outs[n_own:])


def _sigmoid(x):
    return 1.0 / (1.0 + jnp.exp(-x))


def _silu(x):
    return x * _sigmoid(x)


def _softplus(x):
    return jnp.maximum(x, 0.0) + jnp.log(1.0 + jnp.exp(-jnp.abs(x)))


def _rms_fwd_fn(x, g):
    r = lax.rsqrt(jnp.mean(x * x, axis=-1, keepdims=True) + EPS)
    return (x * r) * g


def _rms_bwd_fn(x, dres, *rest):
    dh, g = sum(rest[:-1]), rest[-1]
    r = lax.rsqrt(jnp.mean(x * x, axis=-1, keepdims=True) + EPS)
    xh = x * r
    dxh = dh * g
    dx = dres + r * (dxh - xh * jnp.mean(dxh * xh, axis=-1, keepdims=True))
    return dx, dx, jnp.sum(dh * xh, axis=0, keepdims=True)


def _rms_fwd(name, x, gain, cargo=None):
    return _rowwise(name, _rms_fwd_fn, [x], [gain.reshape(1, -1)], [(x.shape[1], BF16)], cargo=cargo)


def _rms_bwd(name, x, gain, dres, dhs, cargo=None):
    return _rowwise(name, lambda *a: _rms_bwd_fn(*a)[1:], [x, dres] + list(dhs), [gain.reshape(1, -1)],
                    [(x.shape[1], F32)], [(1, x.shape[1])], cargo=cargo)


def _relu2_epilogue(acc):
    r = jnp.maximum(acc, 0.0)
    return (r * r,)


def _relu2_bwd_epilogue(acc, a):
    return (acc * (2.0 * jnp.sqrt(a.astype(F32))),)


def _ple_fwd_fn(x, pp, zg):
    return x + pp * _sigmoid(zg)


def _ple_norm_fwd_fn(x, pp, zg, gain):
    out = _ple_fwd_fn(x, pp, zg)
    return out, _rms_fwd_fn(out, gain)


def _ple_bwd_fn(dx, pp, zg):
    gate = _sigmoid(zg)
    return dx * gate, dx * pp * gate * (1.0 - gate)


def _adamw(w, g, m, v):
    m = ADAM_B1 * m + (1.0 - ADAM_B1) * g
    v = ADAM_B2 * v + (1.0 - ADAM_B2) * jnp.square(g)
    m_hat = m / (1.0 - ADAM_B1 ** ADAM_STEP)
    v_hat = v / (1.0 - ADAM_B2 ** ADAM_STEP)
    delta = -ADAM_LR * (m_hat / (jnp.sqrt(v_hat) + ADAM_EPS) + ADAM_WD * w)
    return delta, m, v


def _lane_take(x, offset):
    n = x.shape[-1]
    return pltpu.roll(x, (-offset) % n, 1)


def _head_lane(shape):
    return lax.broadcasted_iota(jnp.int32, shape, 1) % A_HEAD_DIM


def _rope_partner(x):
    lane = _head_lane(x.shape)
    return jnp.where(lane < ROPE_DIM // 2, _lane_take(x, ROPE_DIM // 2),
                     jnp.where(lane < ROPE_DIM, _lane_take(x, -(ROPE_DIM // 2)), 0.0))


def _head_mean(x, bd):
    hi = x.astype(BF16)
    lo = (x - hi.astype(F32)).astype(BF16)
    b = bd.astype(BF16)
    return jnp.dot(hi, b, preferred_element_type=F32) + jnp.dot(lo, b, preferred_element_type=F32)


def _fold_heads(row):
    out = row[:, :A_HEAD_DIM]
    for h in range(1, A_HEADS):
        out = out + row[:, h * A_HEAD_DIM:(h + 1) * A_HEAD_DIM]
    return out


def _all_heads(t):
    return jnp.concatenate([t] * (A_WIDTH // t.shape[1]), axis=1)


def _qk_prep_fwd_fn(qkv, ct, st, gq, gk, bd):
    ct, st = _all_heads(ct), _all_heads(st)

    def one(t, g):
        n = t * lax.rsqrt(_head_mean(t * t, bd) + EPS) * g
        return n * ct + _rope_partner(n) * st
    q, k, v = qkv[:, :A_WIDTH], qkv[:, A_WIDTH:2 * A_WIDTH], qkv[:, 2 * A_WIDTH:]
    return one(q, gq), one(k, gk), v


def _qk_prep_bwd_fn(qkv, ct, st, dq, dk, dv, gq, gk, bd):
    ct, st = _all_heads(ct), _all_heads(st)

    def one(t, g, dy):
        r = lax.rsqrt(_head_mean(t * t, bd) + EPS)
        nh = t * r
        dn = dy * ct + _rope_partner(dy * st)
        dg = jnp.sum(dn * nh, axis=0, keepdims=True)
        dnh = dn * g
        return r * (dnh - nh * _head_mean(dnh * nh, bd)), _fold_heads(dg)
    q, k = qkv[:, :A_WIDTH], qkv[:, A_WIDTH:2 * A_WIDTH]
    dq_raw, dgq = one(q, gq, dq)
    dk_raw, dgk = one(k, gk, dk)
    return jnp.concatenate([dq_raw, dk_raw, dv], axis=1), dgq, dgk


_BATCH_DIMS = {"nn": ((2,), (1,)), "nt": ((2,), (2,)), "tn": ((1,), (1,))}


def _bdot(a, b, mode, precision=None):
    return lax.dot_general(a, b, (_BATCH_DIMS[mode], ((0,), (0,))), precision=precision,
                           preferred_element_type=F32)


def _attn_cols(h):
    return slice(h * A_HEAD_DIM, (h + 1) * A_HEAD_DIM)


def _attn_heads(ref):
    return jnp.stack([ref[:, _attn_cols(h)] for h in range(A_HEADS)])


def _band_masks():
    qi = lax.broadcasted_iota(jnp.int32, (BAND, BAND), 0)
    kj = lax.broadcasted_iota(jnp.int32, (BAND, BAND), 1)
    return kj <= qi, kj >= qi


def _attn_fwd(name, q, k, v, blocks_per_class, cargo=None):
    S = q.shape[0]
    nblk = S // BAND
    scale = A_HEAD_DIM ** -0.5

    def body(q_ref, kp_ref, kc_ref, vp_ref, vc_ref, o_ref, l_ref):
        i = pl.program_id(0)
        has_prev = (i % blocks_per_class) != 0
        m_cur, m_prev = _band_masks()
        m_prev = jnp.logical_and(m_prev, has_prev)
        q, kc, kp, vc, vp = (_attn_heads(r) for r in (q_ref, kc_ref, kp_ref, vc_ref, vp_ref))
        s_c = jnp.where(m_cur[None], _bdot(q, kc, "nt") * scale, -jnp.inf)
        s_p = jnp.where(m_prev[None], _bdot(q, kp, "nt") * scale, -jnp.inf)
        m = jnp.maximum(jnp.max(s_c, axis=-1, keepdims=True), jnp.max(s_p, axis=-1, keepdims=True))
        e_c, e_p = jnp.exp(s_c - m), jnp.exp(s_p - m)
        l = jnp.sum(e_c, axis=-1, keepdims=True) + jnp.sum(e_p, axis=-1, keepdims=True)
        o = _bdot((e_c / l).astype(BF16), vc, "nn") + _bdot((e_p / l).astype(BF16), vp, "nn")
        lse = m + jnp.log(l)
        for h in range(A_HEADS):
            o_ref[:, _attn_cols(h)] = o[h]
            l_ref[:, _attn_cols(h)] = jnp.broadcast_to(lse[h], (BAND, A_HEAD_DIM))

    cur = pl.BlockSpec((BAND, A_WIDTH), lambda i: (i, 0))
    prev = pl.BlockSpec((BAND, A_WIDTH), lambda i: (jnp.maximum(i - 1, 0), 0))
    body, c_in_specs, c_out_specs, c_out_shape, c_scratch, c_ins = _carry(cargo, 5, 2, 0, body, nblk)
    outs = pl.pallas_call(
        body, name=name, grid=(nblk,), in_specs=[cur, prev, cur, prev, cur] + c_in_specs,
        out_specs=[cur, cur] + c_out_specs,
        out_shape=[jax.ShapeDtypeStruct((S, A_WIDTH), F32)] * 2 + c_out_shape, scratch_shapes=c_scratch,
        compiler_params=_cparams(("arbitrary",)),
    )(q, k, k, v, v, *c_ins)
    return outs[0], outs[1], outs[2:]


def _carry(cargo, n_in, n_out, n_scratch, body, steps):
    if cargo is None:
        return body, [], [], [], [], []
    n_ci, n_co = len(cargo.ins), len(cargo.out_shape)

    def carrying(*refs):
        refs = list(refs)
        ins, refs = refs[:n_in], refs[n_in:]
        c_ins, refs = refs[:n_ci], refs[n_ci:]
        outs, refs = refs[:n_out], refs[n_out:]
        c_outs, refs = refs[:n_co], refs[n_co:]
        scratch, sems = refs[:n_scratch], refs[n_scratch:]

        @pl.when(pl.program_id(0) == 0)
        def _():
            cargo.start(c_ins, c_outs, sems)

        body(*ins, *outs, *scratch)

        @pl.when(pl.program_id(0) == steps - 1)
        def _():
            cargo.finish(c_ins, c_outs, sems)

    any_spec = pl.BlockSpec(memory_space=pl.ANY)
    return carrying, [any_spec] * n_ci, [any_spec] * n_co, list(cargo.out_shape), list(cargo.scratch), list(cargo.ins)


def _attn_bwd(name, q, k, v, o, lse, do, dlse, blocks_per_class, cargo=None):
    S = q.shape[0]
    nblk = S // BAND
    scale = A_HEAD_DIM ** -0.5

    def body(q_ref, kp_ref, kc_ref, vp_ref, vc_ref, o_ref, l_ref, do_ref, dl_ref,
             dq_ref, dk_ref, dv_ref, ck, cv):
        i = pl.program_id(0)

        @pl.when(i == 0)
        def _():
            ck[...] = jnp.zeros_like(ck)
            cv[...] = jnp.zeros_like(cv)

        @pl.when(i == nblk)
        def _():
            dk_ref[...] = ck[...]
            dv_ref[...] = cv[...]

        @pl.when(i < nblk)
        def _():
            has_prev = (i % blocks_per_class) != 0
            m_cur, m_prev = _band_masks()
            m_prev = jnp.logical_and(m_prev, has_prev)
            q, kc, kp, vc, vp = (_attn_heads(r) for r in (q_ref, kc_ref, kp_ref, vc_ref, vp_ref))
            do, o, dl = _attn_heads(do_ref), _attn_heads(o_ref), _attn_heads(dl_ref)
            lse = jnp.max(_attn_heads(l_ref), axis=-1, keepdims=True)
            p_c = jnp.where(m_cur[None], jnp.exp(_bdot(q, kc, "nt") * scale - lse), 0.0)
            p_p = jnp.where(m_prev[None], jnp.exp(_bdot(q, kp, "nt") * scale - lse), 0.0)
            corr = jnp.sum(dl, axis=-1, keepdims=True) - jnp.sum(do * o, axis=-1, keepdims=True)
            dob = do.astype(BF16)
            ds_c = (p_c * (_bdot(dob, vc, "nt") + corr)).astype(BF16)
            ds_p = (p_p * (_bdot(dob, vp, "nt") + corr)).astype(BF16)
            dq = (_bdot(ds_c, kc, "nn") + _bdot(ds_p, kp, "nn")) * scale
            dk_p, dk_c = _bdot(ds_p, q, "tn") * scale, _bdot(ds_c, q, "tn") * scale
            dv_p, dv_c = _bdot(p_p.astype(BF16), dob, "tn"), _bdot(p_c.astype(BF16), dob, "tn")
            for h in range(A_HEADS):
                sl = _attn_cols(h)
                dq_ref[:, sl] = dq[h]
                dk_ref[:, sl] = ck[:, sl] + dk_p[h]
                dv_ref[:, sl] = cv[:, sl] + dv_p[h]
                ck[:, sl] = dk_c[h]
                cv[:, sl] = dv_c[h]

    last = nblk - 1
    cur = pl.BlockSpec((BAND, A_WIDTH), lambda i: (jnp.minimum(i, last), 0))
    prev = pl.BlockSpec((BAND, A_WIDTH), lambda i: (jnp.minimum(jnp.maximum(i - 1, 0), last), 0))
    body, c_in_specs, c_out_specs, c_out_shape, c_scratch, c_ins = _carry(cargo, 9, 3, 2, body, nblk + 1)
    outs = pl.pallas_call(
        body, name=name, grid=(nblk + 1,),
        in_specs=[cur, prev, cur, prev, cur, cur, cur, cur, cur] + c_in_specs,
        out_specs=[cur, prev, prev] + c_out_specs,
        out_shape=[jax.ShapeDtypeStruct((S, A_WIDTH), F32)] * 3 + c_out_shape,
        scratch_shapes=[pltpu.VMEM((BAND, A_WIDTH), F32)] * 2 + c_scratch,
        compiler_params=_cparams(("arbitrary",)),
    )(q, k, k, v, v, o, lse, do, dlse, *c_ins)
    return outs[0], outs[1], outs[2], outs[3:]


def _merge_fwd_fn(o0, o1, o2, l0, l1, l2):
    m = jnp.maximum(jnp.maximum(l0, l1), l2)
    e0, e1, e2 = jnp.exp(l0 - m), jnp.exp(l1 - m), jnp.exp(l2 - m)
    return (e0 * o0 + e1 * o1 + e2 * o2) / (e0 + e1 + e2)


def _merge_bwd_fn(o0, o1, o2, l0, l1, l2, dom):
    m = jnp.maximum(jnp.maximum(l0, l1), l2)
    e0, e1, e2 = jnp.exp(l0 - m), jnp.exp(l1 - m), jnp.exp(l2 - m)
    den = e0 + e1 + e2
    w0, w1, w2 = e0 / den, e1 / den, e2 / den
    dw0, dw1, dw2 = dom * o0, dom * o1, dom * o2
    mean = w0 * dw0 + w1 * dw1 + w2 * dw2
    return w0 * dom, w1 * dom, w2 * dom, w0 * (dw0 - mean), w1 * (dw1 - mean), w2 * (dw2 - mean)


MERGE_ROWS = 512
MERGE_DILATIONS = tuple(d for _, d in SWA_GROUPS) * 2


MERGE_PLANES = A_WIDTH // 128


def _planes(refs):
    return [refs[j:j + MERGE_PLANES] for j in range(0, len(refs), MERGE_PLANES)]


def _class_view(t):
    return [a.reshape(d, a.shape[0] // d, a.shape[1]) for a, d in zip(t, MERGE_DILATIONS)]


def _class_specs():
    return [pl.BlockSpec((d, MERGE_ROWS // d, A_WIDTH), lambda i: (0, i, 0)) for d in MERGE_DILATIONS]


def _token_rows(block, scratch):
    d, n = block.shape[0], block.shape[1]
    if d == 1:
        return block[0]
    for c, plane in enumerate(scratch):
        for r in range(d):
            plane[pl.ds(r, n, stride=d), :] = block[r, :, c * 128:(c + 1) * 128]
    return jnp.concatenate([plane[...] for plane in scratch], axis=1)


def _class_rows(value, block, scratch):
    d, n = block.shape[0], block.shape[1]
    if d == 1:
        block[0] = value
        return
    for c, plane in enumerate(scratch):
        plane[...] = value[:, c * 128:(c + 1) * 128]
        for r in range(d):
            block[r, :, c * 128:(c + 1) * 128] = plane[pl.ds(r, n, stride=d), :]


def _merge_fwd(name, per_group):
    S = per_group[0].shape[0]
    n = len(per_group)

    def body(*refs):
        ins, out, scratch = refs[:n], refs[n], _planes(refs[n + 1:])
        out[...] = _merge_fwd_fn(*[_token_rows(b, s) for b, s in zip(ins, scratch)]).astype(out.dtype)

    return pl.pallas_call(
        body, name=name, grid=(S // MERGE_ROWS,), in_specs=_class_specs(),
        out_specs=pl.BlockSpec((MERGE_ROWS, A_WIDTH), lambda i: (i, 0)),
        out_shape=jax.ShapeDtypeStruct((S, A_WIDTH), BF16),
        scratch_shapes=[pltpu.VMEM((MERGE_ROWS, 128), F32)] * (n * MERGE_PLANES),
        compiler_params=_cparams(("parallel",)),
    )(*_class_view(per_group))


def _merge_bwd(name, per_group, dom, cargo=None):
    S = per_group[0].shape[0]
    n = len(per_group)
    token = pl.BlockSpec((MERGE_ROWS, A_WIDTH), lambda i: (i, 0))

    def body(*refs):
        ins, dom_ref, outs, scratch = refs[:n], refs[n], refs[n + 1:2 * n + 1], _planes(refs[2 * n + 1:])
        res = _merge_bwd_fn(*[_token_rows(b, s) for b, s in zip(ins, scratch)], dom_ref[...])
        for value, block, s in zip(res, outs, scratch):
            _class_rows(value, block, s)

    body, c_in_specs, c_out_specs, c_out_shape, c_scratch, c_ins = _carry(
        cargo, n + 1, n, n * MERGE_PLANES, body, S // MERGE_ROWS)
    outs = pl.pallas_call(
        body, name=name, grid=(S // MERGE_ROWS,), in_specs=_class_specs() + [token] + c_in_specs,
        out_specs=_class_specs() + c_out_specs,
        out_shape=[jax.ShapeDtypeStruct((d, S // d, A_WIDTH), F32) for d in MERGE_DILATIONS] + c_out_shape,
        scratch_shapes=[pltpu.VMEM((MERGE_ROWS, 128), F32)] * (n * MERGE_PLANES) + c_scratch,
        compiler_params=_cparams(("arbitrary",)),
    )(*_class_view(per_group), dom, *c_ins)
    return [o.reshape(S, A_WIDTH) for o in outs[:n]], outs[n:]


def _to_classes(t, d):
    if d == 1:
        return t
    S, C = t.shape
    return t.reshape(S // d, d, C).transpose(1, 0, 2).reshape(S, C)


def _from_classes(t, d):
    if d == 1:
        return t
    S, C = t.shape
    return t.reshape(d, S // d, C).transpose(1, 0, 2).reshape(S, C)


def _rope_lane_tables(positions):
    inv_freq = ROPE_THETA ** (-jnp.arange(0, ROPE_DIM, 2, dtype=F32) / ROPE_DIM)
    ang = positions.astype(F32)[:, None] * inv_freq
    cos, sin = jnp.cos(ang), jnp.sin(ang)
    S = positions.shape[0]
    rest = A_HEAD_DIM - ROPE_DIM
    ct = jnp.concatenate([cos, cos, jnp.ones((S, rest), F32)], axis=1)
    st = jnp.concatenate([-sin, sin, jnp.zeros((S, rest), F32)], axis=1)
    return jnp.tile(ct, (1, 2)), jnp.tile(st, (1, 2))


def _head_mean_matrix():
    r = jnp.arange(A_WIDTH) // A_HEAD_DIM
    return (r[:, None] == r[None, :]).astype(F32) * (1.0 / A_HEAD_DIM)


CONV_LANES = 1024
CONV_BWD_LANES = 512
PAST = CONV_WIDTH - 1


def _strip_starts(tm, strip):
    return range(0, tm, strip)


def _conv_fwd(name, proj, w):
    S = proj.shape[0]
    tm, tc = min(512, S), CONV_LANES
    per8 = tm // 8
    off = DN_QKV0 // tc

    def body(x_ref, halo_ref, w_ref, o_ref, xs):
        i = pl.program_id(0)
        xs[0:8, :] = jnp.where(i > 0, halo_ref[...], 0.0)
        xs[8:, :] = x_ref[...]
        strip = 16
        for r0 in _strip_starts(tm, strip):
            acc = w_ref[PAST:CONV_WIDTH, :] * x_ref[pl.ds(r0, strip), :]
            for j in range(PAST):
                acc = acc + w_ref[j:j + 1, :] * xs[pl.ds(8 - PAST + j + r0, strip), :]
            o_ref[pl.ds(r0, strip), :] = acc

    return pl.pallas_call(
        body, name=name, grid=(S // tm, DN_QKV // tc),
        in_specs=[pl.BlockSpec((tm, tc), lambda i, j: (i, j + off)),
                  pl.BlockSpec((8, tc), lambda i, j: (jnp.maximum(i * per8 - 1, 0), j + off)),
                  pl.BlockSpec((CONV_WIDTH, tc), lambda i, j: (0, j))],
        out_specs=pl.BlockSpec((tm, tc), lambda i, j: (i, j)),
        out_shape=jax.ShapeDtypeStruct((S, DN_QKV), F32),
        scratch_shapes=[pltpu.VMEM((tm + 8, tc), F32)],
        compiler_params=_cparams(("parallel", "parallel")),
    )(proj, proj, w)


def _conv_bwd(name, proj, dpre, w):
    S = proj.shape[0]
    tm, tc = min(512, S), CONV_BWD_LANES
    per8 = tm // 8
    off = DN_QKV0 // tc
    last8 = S // 8 - 1
    nrow = S // tm

    def body(x_ref, xh_ref, d_ref, dh_ref, w_ref, dx_ref, dw_ref, xs, ds):
        i = pl.program_id(1)
        xs[0:8, :] = jnp.where(i > 0, xh_ref[...], 0.0)
        xs[8:, :] = x_ref[...]
        ds[0:tm, :] = d_ref[...]
        ds[tm:, :] = jnp.where(i < nrow - 1, dh_ref[...], 0.0)
        strip = 16
        sums = [jnp.zeros((8, tc), F32)] * CONV_WIDTH
        for r0 in _strip_starts(tm, strip):
            rows = pl.ds(r0, strip)
            d = d_ref[rows, :]
            acc = w_ref[PAST:CONV_WIDTH, :] * d
            for j in range(PAST):
                acc = acc + w_ref[j:j + 1, :] * ds[pl.ds(r0 + PAST - j, strip), :]
            dx_ref[rows, :] = acc.astype(dx_ref.dtype)
            taps = [xs[pl.ds(8 - PAST + j + r0, strip), :] for j in range(PAST)] + [x_ref[rows, :]]
            for j, tap in enumerate(taps):
                prod = d * tap
                sums[j] = sums[j] + (prod[0:8] + prod[8:16])

        @pl.when(i == 0)
        def _():
            dw_ref[...] = jnp.zeros_like(dw_ref)

        for j in range(CONV_WIDTH):
            dw_ref[j:j + 1, :] += jnp.sum(sums[j], axis=0, keepdims=True)

    return pl.pallas_call(
        body, name=name, grid=(DN_QKV // tc, nrow),
        in_specs=[pl.BlockSpec((tm, tc), lambda j, i: (i, j + off)),
                  pl.BlockSpec((8, tc), lambda j, i: (jnp.maximum(i * per8 - 1, 0), j + off)),
                  pl.BlockSpec((tm, tc), lambda j, i: (i, j)),
                  pl.BlockSpec((8, tc), lambda j, i: (jnp.minimum((i + 1) * per8, last8), j)),
                  pl.BlockSpec((CONV_WIDTH, tc), lambda j, i: (0, j))],
        out_specs=[pl.BlockSpec((tm, tc), lambda j, i: (i, j)),
                   pl.BlockSpec((CONV_WIDTH, tc), lambda j, i: (0, j))],
        out_shape=[jax.ShapeDtypeStruct((S, DN_QKV), BF16), jax.ShapeDtypeStruct((CONV_WIDTH, DN_QKV), F32)],
        scratch_shapes=[pltpu.VMEM((tm + 8, tc), F32)] * 2,
        compiler_params=_cparams(("parallel", "arbitrary")),
    )(proj, proj, dpre, dpre, w)


def _gate_lane(shape):
    return lax.broadcasted_iota(jnp.int32, shape, 1)


GATES_ROWS = 256


def _chunk_cumsum_matrix():
    r = jnp.arange(GATES_ROWS)
    return ((r[:, None] >= r[None, :]) & (r[:, None] // CHUNK == r[None, :] // CHUNK)).astype(F32)


def _gates_fwd_fn(ab, alog, dt, cum):
    g = -jnp.exp(alog) * _softplus(ab + dt)
    gc = jnp.dot(cum, g, precision=HIGHEST, preferred_element_type=F32)
    return jnp.where(_gate_lane(ab.shape) < DN_HEADS, gc, _sigmoid(ab))


def _gates_bwd_fn(ab, dgb, alog, dt, cum):
    lane = _gate_lane(ab.shape)
    is_g = lane < DN_HEADS
    neg_a = -jnp.exp(alog)
    sp = _softplus(ab + dt)
    dsp = _sigmoid(ab + dt)
    beta = _sigmoid(ab)
    dgc = jnp.where(is_g, dgb, 0.0)
    dg = lax.dot_general(cum, dgc, (_DIMS["tn"], ((), ())), precision=HIGHEST, preferred_element_type=F32)
    dab = jnp.where(is_g, dg * neg_a * dsp, jnp.where(lane < 2 * DN_HEADS, dgb * beta * (1.0 - beta), 0.0))
    d_alog = jnp.sum(dg * neg_a * sp, axis=0, keepdims=True)
    d_dt = jnp.sum(dg * neg_a * dsp, axis=0, keepdims=True)
    return dab, d_alog, d_dt


def _chunk_math(precision):
    def dg(a, b, mode, prec=precision):
        return _bdot(a, b, mode, prec)

    @jax.custom_vjp
    def nn(a, b):
        return dg(a, b, "nn")

    @jax.custom_vjp
    def nt(a, b):
        return dg(a, b, "nt")

    @jax.custom_vjp
    def tn(a, b):
        return dg(a, b, "tn")

    nn.defvjp(lambda a, b: (nn(a, b), (a, b)), lambda r, g: (nt(g, r[1]), tn(r[0], g)))
    nt.defvjp(lambda a, b: (nt(a, b), (a, b)), lambda r, g: (nn(g, r[1]), tn(g, r[0])))
    tn.defvjp(lambda a, b: (tn(a, b), (a, b)), lambda r, g: (nt(r[1], g), nn(r[0], g)))

    def split(x):
        hi = x.astype(BF16)
        return hi, (x - hi.astype(F32)).astype(BF16)

    def fine(a, b, mode):
        ah, al = split(a)
        bh, bl = split(b)
        return dg(ah, bh, mode, None) + (dg(ah, bl, mode, None) + dg(al, bh, mode, None))

    def unit_lower_inverse(a):
        row = lax.broadcasted_iota(jnp.int32, a.shape, 1)
        col = lax.broadcasted_iota(jnp.int32, a.shape, 2)
        x = -a
        p = jnp.where(row == col, 1.0, 0.0) + x
        for _ in range(int(math.log2(CHUNK)) - 1):
            x = fine(x, x, "nn")
            p = p + fine(p, x, "nn")
        return p

    @jax.custom_vjp
    def solve2(a, ti, r1, r2):
        return dg(ti, r1, "nn"), dg(ti, r2, "nn")

    def solve2_fwd(a, ti, r1, r2):
        s1, s2 = dg(ti, r1, "nn"), dg(ti, r2, "nn")
        return (s1, s2), (ti, s1, s2)

    def solve2_bwd(res, g):
        ti, s1, s2 = res
        d1, d2 = dg(ti, g[0], "tn"), dg(ti, g[1], "tn")
        return -(dg(d1, s1, "nt") + dg(d2, s2, "nt")), jnp.zeros_like(ti), d1, d2

    solve2.defvjp(solve2_fwd, solve2_bwd)

    def chunk_fn(pq, pk, pv, z, g_col, b_col, g_row, ogain, s_in, inverse=None):
        nb = pq.shape[0]
        sq = (nb, CHUNK, CHUNK)
        row = lax.broadcasted_iota(jnp.int32, sq, 1)
        col = lax.broadcasted_iota(jnp.int32, sq, 2)
        lower, strict = row >= col, row > col
        q, k, v = _silu(pq), _silu(pk), _silu(pv)
        q = q * lax.rsqrt(jnp.sum(q * q, axis=-1, keepdims=True) + EPS) * (DN_HEAD_DIM ** -0.5)
        k = k * lax.rsqrt(jnp.sum(k * k, axis=-1, keepdims=True) + EPS)
        gc_wide = jnp.broadcast_to(g_col, pq.shape)
        gc_i = jnp.broadcast_to(g_col, sq)
        gc_j = jnp.broadcast_to(g_row, sq)
        is_last = lax.broadcasted_iota(jnp.int32, pq.shape, 1) == CHUNK - 1
        g_last = jnp.sum(jnp.where(is_last, gc_wide, 0.0), axis=1, keepdims=True)
        decay = jnp.exp(jnp.where(lower, gc_i - gc_j, -jnp.inf))
        kb = k * b_col
        a_mat = jnp.where(strict, nt(kb, k) * decay, 0.0)
        eg = jnp.exp(gc_wide)
        ti = unit_lower_inverse(a_mat) if inverse is None else inverse
        u, w = solve2(a_mat, ti, v * b_col, kb * eg)
        attn = nt(q, k) * decay
        q_dec = q * eg
        k_dec = k * jnp.exp(g_last - gc_wide)
        c_dec = jnp.exp(g_last)
        v_new = u - nn(w, s_in)
        o = nn(q_dec, s_in) + nn(attn, v_new)
        s_out = s_in * c_dec + tn(k_dec, v_new)
        y = o * lax.rsqrt(jnp.mean(o * o, axis=-1, keepdims=True) + EPS) * ogain * _silu(z)
        return (y, s_out, ti) if inverse is None else (y, s_out)

    return chunk_fn


DN_PRECISION = None


def _chunk_specs(n_of):
    groups = DN_HEADS // DN_HB
    wide = DN_HB * DN_HEAD_DIM
    hd = pl.BlockSpec((CHUNK, wide), lambda h, n: (n_of(n), h))
    specs = dict(
        pq=hd,
        pk=pl.BlockSpec((CHUNK, wide), lambda h, n: (n_of(n), groups + h)),
        pv=pl.BlockSpec((CHUNK, wide), lambda h, n: (n_of(n), 2 * groups + h)),
        z=hd,
        gates=pl.BlockSpec((CHUNK, DN_AB_PAD), lambda h, n: (n_of(n), 0)),
        row=pl.BlockSpec((DN_HB, None, 1, CHUNK), lambda h, n: (h, n_of(n), 0, 0)),
        gain=pl.BlockSpec((1, DN_HEAD_DIM), lambda h, n: (0, 0)),
        state=pl.BlockSpec((DN_HB, None, DN_HEAD_DIM, DN_HEAD_DIM), lambda h, n: (h, n_of(n), 0, 0)),
        inverse=pl.BlockSpec((DN_HB, None, CHUNK, CHUNK), lambda h, n: (h, n_of(n), 0, 0)),
        qkv=pl.BlockSpec((CHUNK, DN_QKV), lambda h, n: (n_of(n), 0)),
        head=hd,
    )
    return specs


def _head_cols(j):
    return slice(j * DN_HEAD_DIM, (j + 1) * DN_HEAD_DIM)


def _split_heads(ref):
    return jnp.stack([ref[:, _head_cols(j)] for j in range(DN_HB)])


def _gate_columns(gates, first_lane):
    lane = lax.broadcasted_iota(jnp.int32, gates.shape, 1)
    return jnp.stack([jnp.sum(jnp.where(lane == first_lane + h, gates, 0.0), axis=-1, keepdims=True)
                      for h in range(DN_HEADS)])


def _gate_lanes(columns, first_lane):
    shape = (columns.shape[1], DN_AB_PAD)
    lane = lax.broadcasted_iota(jnp.int32, shape, 1)
    out = jnp.zeros(shape, F32)
    for h in range(DN_HEADS):
        out = out + jnp.where(lane == first_lane + h, columns[h], 0.0)
    return out


def _chunk_fwd(name, pre, proj, gates, g_row, ogain):
    assert DN_HB == DN_HEADS
    S = pre.shape[0]
    N = S // CHUNK
    chunk_fn = _chunk_math(DN_PRECISION)
    sp = _chunk_specs(lambda n: n)

    def body(pq, pk, pv, z, gb, gr, og, y_ref, sin_ref, inv_ref, st):
        @pl.when(pl.program_id(1) == 0)
        def _():
            st[...] = jnp.zeros_like(st)

        s_in = st[...]
        sin_ref[...] = s_in
        y, s_out, inverse = chunk_fn(_split_heads(pq), _split_heads(pk), _split_heads(pv), _split_heads(z),
                                     _gate_columns(gb[...], 0), _gate_columns(gb[...], DN_HEADS), gr[...],
                                     og[...], s_in)
        for j in range(DN_HB):
            y_ref[:, _head_cols(j)] = y[j].astype(y_ref.dtype)
        inv_ref[...] = inverse
        st[...] = s_out

    return pl.pallas_call(
        body, name=name, grid=(DN_HEADS // DN_HB, N),
        in_specs=[sp["pq"], sp["pk"], sp["pv"], sp["z"], sp["gates"], sp["row"], sp["gain"]],
        out_specs=[sp["head"], sp["state"], sp["inverse"]],
        out_shape=[jax.ShapeDtypeStruct((S, DN_WIDTH), BF16),
                   jax.ShapeDtypeStruct((DN_HEADS, N, DN_HEAD_DIM, DN_HEAD_DIM), F32),
                   jax.ShapeDtypeStruct((DN_HEADS, N, CHUNK, CHUNK), F32)],
        scratch_shapes=[pltpu.VMEM((DN_HB, DN_HEAD_DIM, DN_HEAD_DIM), F32)],
        compiler_params=_cparams(("parallel", "arbitrary")),
    )(pre, pre, pre, proj, gates, g_row, ogain)


def _chunk_bwd(name, pre, proj, gates, g_row, ogain, s_in_all, inverse_all, dy):
    assert DN_HB == DN_HEADS
    S = pre.shape[0]
    N = S // CHUNK
    chunk_fn = _chunk_math(DN_PRECISION)
    sp = _chunk_specs(lambda n: N - 1 - n)

    def body(pq, pk, pv, z, gb, gr, og, sin_ref, inv_ref, dy_ref,
             dpre_ref, dz_ref, dgb_ref, dgr_ref, dog_ref, ds):
        @pl.when(pl.program_id(1) == 0)
        def _():
            ds[...] = jnp.zeros_like(ds)
            dog_ref[...] = jnp.zeros_like(dog_ref)

        inverse = inv_ref[...]
        prim = (_split_heads(pq), _split_heads(pk), _split_heads(pv), _split_heads(z),
                _gate_columns(gb[...], 0), _gate_columns(gb[...], DN_HEADS), gr[...], og[...], sin_ref[...])
        _, vjp = jax.vjp(lambda *a: chunk_fn(*a, inverse=inverse), *prim)
        gq, gk, gv, gz, ggc, gbc, ggr, gog, gs = vjp((_split_heads(dy_ref), ds[...]))
        for j in range(DN_HB):
            for part, g in enumerate((gq, gk, gv)):
                dpre_ref[:, pl.ds(part * DN_WIDTH + j * DN_HEAD_DIM, DN_HEAD_DIM)] = g[j]
            dz_ref[:, _head_cols(j)] = gz[j].astype(dz_ref.dtype)
        dgb_ref[...] = _gate_lanes(ggc, 0) + _gate_lanes(gbc, DN_HEADS)
        dgr_ref[...] = ggr
        dog_ref[...] += gog
        ds[...] = gs

    hd = sp["head"]
    return pl.pallas_call(
        body, name=name, grid=(1, N),
        in_specs=[sp["pq"], sp["pk"], sp["pv"], sp["z"], sp["gates"], sp["row"], sp["gain"],
                  sp["state"], sp["inverse"], hd],
        out_specs=[sp["qkv"], hd, sp["gates"], sp["row"], sp["gain"]],
        out_shape=[jax.ShapeDtypeStruct((S, DN_QKV), F32), jax.ShapeDtypeStruct((S, DN_WIDTH), BF16),
                   jax.ShapeDtypeStruct((S, DN_AB_PAD), F32),
                   jax.ShapeDtypeStruct((DN_HEADS, N, 1, CHUNK), F32), jax.ShapeDtypeStruct((1, DN_HEAD_DIM), F32)],
        scratch_shapes=[pltpu.VMEM((DN_HB, DN_HEAD_DIM, DN_HEAD_DIM), F32)],
        compiler_params=_cparams(("arbitrary", "arbitrary")),
    )(pre, pre, pre, proj, gates, g_row, ogain, s_in_all, inverse_all, dy)


def _mm_rms_bwd(name, d_out, w, x, gain, dres, matmul_copy=True):
    first = 0 if matmul_copy else 1
    return _mm(name, d_out, w, "nt", out_dtypes=(F32, BF16)[:2 - first], extras=(x, dres),
               rows=(gain.reshape(1, -1),), n_sums=1, tm_cap=512,
               epilogue=lambda acc, x_, dres_, g: _rms_bwd_fn(x_, dres_, acc, g)[first:])


def _residual_norm_epilogue(acc, res, gain):
    x = acc + res
    return x, _rms_fwd_fn(x, gain)


def _ple_loss_fn(x_mid, pp, zg, t):
    gate = _sigmoid(zg)
    err = x_mid + pp * gate - t
    dy = err * (1.0 / D_MODEL)
    return dy, dy * gate, dy * pp * gate * (1.0 - gate), jnp.broadcast_to(jnp.sum(err * err, keepdims=True), (1, 128))


def _mlp_ple_fwd(tag, x_in, h, p_l, w_up, w_down, norm_ple, w_ple, w_gate, next_gain=None, target=None):
    a = _mm(f"{tag}_up", h, w_up, "nn", out_dtypes=(BF16,), epilogue=_relu2_epilogue)
    x_mid, hg = _mm(f"{tag}_down", a, w_down, "nn", out_dtypes=(F32, BF16), extras=(x_in,),
                    rows=(norm_ple.reshape(1, -1),), epilogue=_residual_norm_epilogue)
    zg = _mm(f"{tag}_gate", hg, w_gate, "nn")
    pp = _mm(f"{tag}_ple", p_l, w_ple, "nn")
    sv = dict(x_in=x_in, h=h, a=a, x_mid=x_mid, hg=hg, zg=zg, pp=pp)
    if target is not None:
        sv["dy"], sv["dpp"], sv["dzg"], sv["sq"] = _rowwise(
            f"{tag}_ple_loss", _ple_loss_fn, [x_mid, pp, zg, target], [],
            [(D_MODEL, F32), (D_MODEL, BF16), (D_MODEL, BF16)], [(1, 128)])
        return None, None, sv
    if next_gain is None:
        return _rowwise(f"{tag}_ple_out", _ple_fwd_fn, [x_mid, pp, zg], [], [(D_MODEL, F32)]), None, sv
    x_out, h_next = _rowwise(f"{tag}_ple_out", _ple_norm_fwd_fn, [x_mid, pp, zg], [next_gain.reshape(1, -1)],
                             [(D_MODEL, F32), (D_MODEL, BF16)])
    return x_out, h_next, sv


def _mlp_ple_bwd(tag, dx, sv, p_l, norm_mlp, w_up, w_down, norm_ple, w_ple, w_gate):
    if "dpp" in sv:
        dpp, dzg = sv["dpp"], sv["dzg"]
    else:
        dpp, dzg = _rowwise(f"{tag}_ple_bwd", _ple_bwd_fn, [dx, sv["pp"], sv["zg"]], [],
                            [(D_MODEL, BF16), (D_MODEL, BF16)])
    d_w_ple = _mm(f"{tag}_d_w_ple", p_l, dpp, "tn", out_dtypes=(BF16,))
    d_w_gate = _mm(f"{tag}_d_w_gate", sv["hg"], dzg, "tn", out_dtypes=(BF16,))
    dx_mid, dx_mid_b, d_norm_ple = _mm_rms_bwd(f"{tag}_d_hg", dzg, w_gate, sv["x_mid"], norm_ple, dx)
    du = _mm(f"{tag}_d_u", dx_mid_b, w_down, "nt", out_dtypes=(BF16,), extras=(sv["a"],),
             epilogue=_relu2_bwd_epilogue)
    d_w_down = _mm(f"{tag}_d_w_down", sv["a"], dx_mid_b, "tn", out_dtypes=(BF16,))
    d_w_up = _mm(f"{tag}_d_w_up", sv["h"], du, "tn", out_dtypes=(BF16,))
    dx_in, dx_in_b, d_norm_mlp = _mm_rms_bwd(f"{tag}_d_h", du, w_up, sv["x_in"], norm_mlp, dx_mid)
    return dx_in, dx_in_b, dict(mlp_norm=d_norm_mlp, w_up=d_w_up, w_down=d_w_down, ple_norm=d_norm_ple,
                                w_ple=d_w_ple, w_ple_gate=d_w_gate)


class _NoHooks:
    first_cargo = None
    fwd_cargo = (None,) * len(SWA_GROUPS)

    def first_weights(self, results):
        return {}

    def weights_from(self, results):
        return {}

    def split_cargo(self, early_grads):
        return None

    def bwd_cargo(self, results):
        return (None,) * len(SWA_GROUPS)

    def last_cargo(self, attn_grads):
        return None


def _with_cargo(result, cargo):
    return (result, ()) if cargo is None else result


def _local_step(x, p, positions, target, small, big, hooks=_NoHooks()):
    S = x.shape[0]
    ct, st = _rope_lane_tables(positions)
    bd = _head_mean_matrix()

    h0, first = _with_cargo(_rms_fwd("l0_mix_norm", x, small["mix_norm"][0], cargo=hooks.first_cargo),
                            hooks.first_cargo)
    big = {**big, **hooks.first_weights(first)}
    attn, brought = [], []
    for g, (window, d) in enumerate(SWA_GROUPS):
        assert window // d == BAND and (S // d) % BAND == 0
        h0g = _to_classes(h0, d)
        ctg, stg = _to_classes(ct, d), _to_classes(st, d)
        w_g = big["attn_w_qkv"][:, g * 3 * A_WIDTH:(g + 1) * 3 * A_WIDTH]
        gq = jnp.tile(small["attn_q_gain"][0, g], A_HEADS).reshape(1, A_WIDTH)
        gk = jnp.tile(small["attn_k_gain"][0, g], A_HEADS).reshape(1, A_WIDTH)
        qkv = _mm(f"l0_qkv{g}", h0g, w_g, "nn")
        q, k, v = _rowwise(f"l0_qk_prep{g}", _qk_prep_fwd_fn, [qkv, ctg, stg], [gq, gk, bd], [(A_WIDTH, BF16)] * 3)
        o, lse, cargo_out = _attn_fwd(f"l0_attn{g}", q, k, v, (S // d) // BAND, cargo=hooks.fwd_cargo[g])
        brought.append(cargo_out)
        attn.append(dict(d=d, h0g=h0g, ct=ctg, st=stg, w=w_g, gq=gq, gk=gk, qkv=qkv, q=q, k=k, v=v, o=o, lse=lse))
    big = {**big, **hooks.weights_from(brought)}
    per_group = [a["o"] for a in attn] + [a["lse"] for a in attn]
    om = _merge_fwd("l0_merge", per_group)
    x1, h1 = _mm("l0_attn_out", om, big["attn_w_o"], "nn", out_dtypes=(F32, BF16), extras=(x,),
                 rows=(small["mlp_norm"][0].reshape(1, -1),), epilogue=_residual_norm_epilogue)
    x3, h3, sv0 = _mlp_ple_fwd("l0", x1, h1, p[0], big["w_up"][0], big["w_down"][0], small["ple_norm"][0],
                               big["w_ple"][0], big["w_ple_gate"][0], next_gain=small["mix_norm"][1])

    N = S // CHUNK
    proj = _mm("l1_in", h3, big["dn_w_in"], "nn")
    pre = _conv_fwd("l1_conv", proj, small["dn_conv"])
    ab = proj[:, DN_AB0:DN_AB0 + DN_AB_PAD]
    lane_pad = DN_AB_PAD - DN_HEADS
    alog_row = jnp.pad(small["dn_a_log"][0], (0, lane_pad)).reshape(1, DN_AB_PAD)
    dt_row = jnp.pad(small["dn_dt_bias"][0], (0, lane_pad)).reshape(1, DN_AB_PAD)
    cum = _chunk_cumsum_matrix()
    gb = _rowwise("l1_gates", _gates_fwd_fn, [ab], [alog_row, dt_row, cum], [(DN_AB_PAD, F32)], tm=GATES_ROWS)
    g_row = gb[:, :DN_HEADS].T.reshape(DN_HEADS, N, 1, CHUNK)
    ogain = small["dn_o_gain"][0].reshape(1, DN_HEAD_DIM)
    y, s_in_all, inverse_all = _chunk_fwd("l1_delta", pre, proj, gb, g_row, ogain)
    x4, h4 = _mm("l1_dn_out", y, big["dn_w_o"], "nn", out_dtypes=(F32, BF16), extras=(x3,),
                 rows=(small["mlp_norm"][1].reshape(1, -1),), epilogue=_residual_norm_epilogue)
    _, _, sv1 = _mlp_ple_fwd("l1", x4, h4, p[1], big["w_up"][1], big["w_down"][1], small["ple_norm"][1],
                             big["w_ple"][1], big["w_ple_gate"][1], target=target)
    dy, sq = sv1["dy"], sv1["sq"]

    dx4, dx4_b, gl1 = _mlp_ple_bwd("l1", dy, sv1, p[1], small["mlp_norm"][1], big["w_up"][1], big["w_down"][1],
                            small["ple_norm"][1], big["w_ple"][1], big["w_ple_gate"][1])
    d_y = _mm("l1_d_y", dx4_b, big["dn_w_o"], "nt")
    d_dn_w_o = _mm("l1_d_w_o", y, dx4_b, "tn", out_dtypes=(BF16,))
    dpre, dz, dgb_cols, dg_row, d_ogain = _chunk_bwd(
        "l1_delta_bwd", pre, proj, gb, g_row, ogain, s_in_all, inverse_all, d_y)
    dconv_in, d_conv_w = _conv_bwd("l1_conv_bwd", proj, dpre, small["dn_conv"])
    dgb = dgb_cols + jnp.pad(dg_row.reshape(DN_HEADS, S).T, ((0, 0), (0, DN_AB_PAD - DN_HEADS)))
    dab, d_alog, d_dt = _rowwise("l1_gates_bwd", _gates_bwd_fn, [ab, dgb], [alog_row, dt_row, cum],
                                 [(DN_AB_PAD, F32)], [(1, DN_AB_PAD), (1, DN_AB_PAD)], tm=GATES_ROWS)
    dproj = jnp.concatenate([dz, dconv_in, dab.astype(BF16)], axis=1)
    d_dn_w_in = _mm("l1_d_w_in", h3, dproj, "tn", out_dtypes=(BF16,))
    dx3, d_mix1 = _mm_rms_bwd("l1_d_h", dproj, big["dn_w_in"], x3, small["mix_norm"][1], dx4, matmul_copy=False)

    dx1, dx1_b, gl0 = _mlp_ple_bwd("l0", dx3, sv0, p[0], small["mlp_norm"][0], big["w_up"][0], big["w_down"][0],
                            small["ple_norm"][0], big["w_ple"][0], big["w_ple_gate"][0])
    early = dict(
        dn_w_in=jnp.concatenate([d_dn_w_in[:, DN_QKV0:DN_AB0 + 2 * DN_HEADS], d_dn_w_in[:, :DN_WIDTH]], axis=1),
        dn_w_o=d_dn_w_o,
        w_up=jnp.stack([gl0["w_up"], gl1["w_up"]]),
        w_down=jnp.stack([gl0["w_down"], gl1["w_down"]]),
        w_ple=jnp.stack([gl0["w_ple"], gl1["w_ple"]]),
        w_ple_gate=jnp.stack([gl0["w_ple_gate"], gl1["w_ple_gate"]]))
    split_cargo = hooks.split_cargo(early)
    dom = _mm("l0_d_om", dx1_b, big["attn_w_o"], "nt")
    d_attn_w_o = _mm("l0_d_w_o", om, dx1_b, "tn", out_dtypes=(BF16,))
    merged, split = _merge_bwd("l0_merge_bwd", per_group, dom, cargo=split_cargo)
    bwd_cargo = hooks.bwd_cargo(split)
    dh0, d_w_qkv, d_gq, d_gk, brought_bwd = [], [], [], [], []
    for g, a in enumerate(attn):
        do_g, dl_g = merged[g], merged[3 + g]
        dqn, dkn, dvn, cargo_out = _attn_bwd(f"l0_attn_bwd{g}", a["q"], a["k"], a["v"], a["o"], a["lse"], do_g, dl_g,
                                             (S // a["d"]) // BAND, cargo=bwd_cargo[g])
        brought_bwd.append(cargo_out)
        dqkv, dgq, dgk = _rowwise(f"l0_qk_prep_bwd{g}", _qk_prep_bwd_fn, [a["qkv"], a["ct"], a["st"], dqn, dkn, dvn],
                                  [a["gq"], a["gk"], bd], [(3 * A_WIDTH, BF16)], [(1, A_HEAD_DIM)] * 2)
        d_w_qkv.append(_mm(f"l0_d_w_qkv{g}", a["h0g"], dqkv, "tn", out_dtypes=(BF16,)))
        dh0.append(_from_classes(_mm(f"l0_d_h{g}", dqkv, a["w"], "nt"), a["d"]))
        d_gq.append(dgq)
        d_gk.append(dgk)
    attn_grads = dict(attn_w_qkv=jnp.concatenate(d_w_qkv, axis=1), attn_w_o=d_attn_w_o)
    last_cargo = hooks.last_cargo(attn_grads)
    (grad_x, d_mix0), last = _with_cargo(
        _rms_bwd("l0_mix_norm_bwd", x, small["mix_norm"][0], dx1, dh0, cargo=last_cargo), last_cargo)
    brought_bwd.append(last)

    grads = dict(
        mix_norm=jnp.concatenate([d_mix0, d_mix1], axis=0),
        attn_q_gain=jnp.concatenate(d_gq, axis=0)[None],
        attn_k_gain=jnp.concatenate(d_gk, axis=0)[None],
        **attn_grads,
        dn_conv=d_conv_w,
        dn_a_log=d_alog[:, :DN_HEADS],
        dn_dt_bias=d_dt[:, :DN_HEADS],
        dn_o_gain=d_ogain,
        mlp_norm=jnp.concatenate([gl0["mlp_norm"], gl1["mlp_norm"]], axis=0),
        ple_norm=jnp.concatenate([gl0["ple_norm"], gl1["ple_norm"]], axis=0),
        **early,
    )
    return sq, grad_x, grads, brought_bwd


def _chip_peer(x, y, c, t):
    return (jnp.bitwise_xor(x, t >> 1), jnp.bitwise_xor(y, t & 1), c)


def _place():
    x, y, c = lax.axis_index("x"), lax.axis_index("y"), lax.axis_index("c")
    return x, y, c, 2 * x + y, (x, y, 1 - c)


def _remote(src, dst, send_sem, recv_sem, to):
    return pltpu.make_async_remote_copy(src_ref=src, dst_ref=dst, send_sem=send_sem, recv_sem=recv_sem,
                                        device_id=to, device_id_type=MESH)


def _hbm_call(name, body, ins, out_shape, scratch_shapes):
    any_spec = pl.BlockSpec(memory_space=pl.ANY)
    return pl.pallas_call(body, name=name, out_shape=out_shape, in_specs=[any_spec] * len(ins),
                          out_specs=[any_spec] * len(out_shape), scratch_shapes=scratch_shapes)(*ins)


def _half(n0, which):
    return pl.ds(which * (n0 // 2), n0 // 2)


class _Exchange:
    def __init__(self, ins, out_shape, scratch, start, finish):
        self.ins, self.out_shape, self.scratch, self.start, self.finish = ins, out_shape, scratch, start, finish


def _run_exchange(name, ex):
    n_in, n_out = len(ex.ins), len(ex.out_shape)

    def body(*refs):
        ins, outs, sems = refs[:n_in], refs[n_in:n_in + n_out], refs[n_in + n_out:]
        ex.start(ins, outs, sems)
        ex.finish(ins, outs, sems)

    return _hbm_call(name, body, ex.ins, ex.out_shape, ex.scratch)


def _gather_exchange(shards):
    T = len(shards)

    pairs = [(i, t) for i in range(T) for t in range(1, N_CHIPS)]

    def copies(ins, outs, sems):
        send, recv = sems
        x, y, c, q, sibling = _place()

        def half(i, which):
            return _half(ins[i].shape[0], which)

        def over_ici(i, t):
            return _remote(ins[i].at[half(i, c)], outs[i].at[q, half(i, c)], send.at[i, t - 1], recv.at[i, t - 1],
                           _chip_peer(x, y, c, t))

        def landing(i, t):
            spot = outs[i].at[jnp.bitwise_xor(q, t), half(i, c)]
            return _remote(spot, spot, send.at[i, t - 1], recv.at[i, t - 1], _chip_peer(x, y, c, t))

        def forward(i, t):
            spot = outs[i].at[jnp.bitwise_xor(q, t), half(i, c)]
            return _remote(spot, spot, send.at[i, 2 + t], recv.at[i, 2 + t], sibling)

        def forwarded(i, t):
            spot = outs[i].at[jnp.bitwise_xor(q, t), half(i, 1 - c)]
            return _remote(spot, spot, send.at[i, 2 + t], recv.at[i, 2 + t], sibling)

        return over_ici, landing, forward, forwarded

    def start(ins, outs, sems):
        over_ici = copies(ins, outs, sems)[0]
        for i, t in pairs:
            over_ici(i, t).start()

    def finish(ins, outs, sems):
        over_ici, landing, forward, forwarded = copies(ins, outs, sems)
        for i, t in pairs:
            landing(i, t).wait_recv()
            forward(i, t).start()
        for i, t in pairs:
            forwarded(i, t).wait_recv()
        for i, t in pairs:
            over_ici(i, t).wait_send()
            forward(i, t).wait_send()

    n_rel = 2 * (N_CHIPS - 1)
    return _Exchange(list(shards), [jax.ShapeDtypeStruct((N_CHIPS,) + s.shape, s.dtype) for s in shards],
                     [pltpu.SemaphoreType.DMA((T, n_rel)), pltpu.SemaphoreType.DMA((T, n_rel))], start, finish)


def _scatter_exchange(stacks):
    T = len(stacks)

    def copies(ins, outs, sems):
        send, recv = sems
        x, y, c, q, sibling = _place()
        return [_remote(ins[i].at[jnp.bitwise_xor(q, t)], outs[i].at[t - 1], send.at[i, t - 1], recv.at[i, t - 1],
                        _chip_peer(x, y, c, t)) for i in range(T) for t in range(1, N_CHIPS)]

    def start(ins, outs, sems):
        for cp in copies(ins, outs, sems):
            cp.start()

    def finish(ins, outs, sems):
        for cp in copies(ins, outs, sems):
            cp.wait()

    return _Exchange(list(stacks), [jax.ShapeDtypeStruct((N_CHIPS - 1,) + s.shape[1:], s.dtype) for s in stacks],
                     [pltpu.SemaphoreType.DMA((T, N_CHIPS - 1)), pltpu.SemaphoreType.DMA((T, N_CHIPS - 1))],
                     start, finish)


def _other_half_exchange(stacks):
    T = len(stacks)

    def copies(ins, outs, sems):
        send, recv = sems
        x, y, c, q, sibling = _place()
        return [_remote(ins[i].at[:, _half(ins[i].shape[1], 1 - c)], outs[i], send.at[i], recv.at[i], sibling)
                for i in range(T)]

    def start(ins, outs, sems):
        for cp in copies(ins, outs, sems):
            cp.start()

    def finish(ins, outs, sems):
        for cp in copies(ins, outs, sems):
            cp.wait()

    return _Exchange(list(stacks),
                     [jax.ShapeDtypeStruct((s.shape[0], s.shape[1] // 2) + s.shape[2:], s.dtype) for s in stacks],
                     [pltpu.SemaphoreType.DMA((T,)), pltpu.SemaphoreType.DMA((T,))], start, finish)


def _swap_with_sibling(name, arrays):
    T = len(arrays)

    def body(*refs):
        ins, outs = refs[:T], refs[T:2 * T]
        send, recv = refs[2 * T:]
        x, y, c, q, sibling = _place()
        copies = []
        for i in range(T):
            rc = _remote(ins[i], outs[i], send.at[i], recv.at[i], sibling)
            rc.start()
            copies.append(rc)
        for cp in copies:
            cp.wait()

    return _hbm_call(name, body, arrays, [jax.ShapeDtypeStruct(a.shape, a.dtype) for a in arrays],
                     [pltpu.SemaphoreType.DMA((T,)), pltpu.SemaphoreType.DMA((T,))])


def _gather_from_all(name, block):
    R, C = block.shape

    def body(src, out, send_sems, recv_sems):
        x, y, c = lax.axis_index("x"), lax.axis_index("y"), lax.axis_index("c")
        me = 4 * x + 2 * y + c
        out[me] = src[...]
        copies = []
        for r in range(1, N_DEV):
            peer = (jnp.bitwise_xor(x, r >> 2), jnp.bitwise_xor(y, (r >> 1) & 1), jnp.bitwise_xor(c, r & 1))
            cp = pltpu.make_async_remote_copy(src_ref=src, dst_ref=out.at[me], send_sem=send_sems.at[r - 1],
                                              recv_sem=recv_sems.at[r - 1], device_id=peer, device_id_type=MESH)
            cp.start()
            copies.append(cp)
        for cp in copies:
            cp.wait()

    return pl.pallas_call(
        body, name=name, out_shape=jax.ShapeDtypeStruct((N_DEV, R, C), block.dtype),
        in_specs=[pl.BlockSpec(memory_space=pltpu.VMEM)], out_specs=pl.BlockSpec(memory_space=pltpu.VMEM),
        scratch_shapes=[pltpu.SemaphoreType.DMA((N_DEV - 1,)), pltpu.SemaphoreType.DMA((N_DEV - 1,))],
    )(block)


def _view(a):
    return a[0] if a.shape[0] == 1 else a


def _view_axis(a, axis):
    return axis - 1 if a.shape[0] == 1 else axis


def _rows(a):
    return a.reshape(-1, a.shape[-1])


def _elementwise(name, fn, ins, out_dtypes, tm):
    specs = []
    for a in ins:
        a, row0 = a if isinstance(a, tuple) else (a, 0)
        specs.append((_rows(a), a.shape[-1], 0, row0))
    shape = ins[0][0].shape if isinstance(ins[0], tuple) else ins[0].shape
    outs = _rowwise(name, fn, specs, [], [(shape[-1], dt) for dt in out_dtypes], tm=tm, n_rows=math.prod(shape[:-1]))
    return outs.reshape(shape) if len(out_dtypes) == 1 else [o.reshape(shape) for o in outs]


SMALL_ROWS = 8
CONV_ROWS = CONV_WIDTH * DN_QKV // D_MODEL
SMALL_GRAD_ROWS = 24


def _pack_small(vals, conv=None):
    tail = jnp.concatenate([vals["attn_q_gain"].reshape(-1), vals["attn_k_gain"].reshape(-1),
                            vals["dn_a_log"].reshape(-1), vals["dn_dt_bias"].reshape(-1),
                            vals["dn_o_gain"].reshape(-1)])
    tail = jnp.pad(tail, (0, D_MODEL - tail.shape[0])).reshape(1, D_MODEL)
    rows = [vals["mix_norm"], vals["mlp_norm"], vals["ple_norm"], tail, jnp.zeros((1, D_MODEL), F32)]
    if conv is not None:
        rows += [conv.reshape(CONV_ROWS, D_MODEL),
                 jnp.zeros((SMALL_GRAD_ROWS - SMALL_ROWS - CONV_ROWS, D_MODEL), F32)]
    return jnp.concatenate(rows, axis=0)


def _unpack_small(block):
    nq = 3 * A_HEAD_DIM
    t = block[6]
    return dict(
        mix_norm=block[0:2], mlp_norm=block[2:4], ple_norm=block[4:6],
        attn_q_gain=t[:nq].reshape(1, 3, A_HEAD_DIM), attn_k_gain=t[nq:2 * nq].reshape(1, 3, A_HEAD_DIM),
        dn_a_log=t[2 * nq:2 * nq + DN_HEADS].reshape(1, DN_HEADS),
        dn_dt_bias=t[2 * nq + DN_HEADS:2 * nq + 2 * DN_HEADS].reshape(1, DN_HEADS),
        dn_o_gain=t[2 * nq + 2 * DN_HEADS:2 * nq + 2 * DN_HEADS + DN_HEAD_DIM].reshape(1, DN_HEAD_DIM))


def kernel(x, p, positions, mix_norm, attn_w_qkv, attn_q_gain, attn_k_gain, attn_w_o, dn_w_in, dn_conv, dn_a_log, dn_dt_bias, dn_o_gain, dn_w_o, mlp_norm, w_up, w_down, ple_norm, w_ple, w_ple_gate, loss_target, m_mix_norm, m_attn_w_qkv, m_attn_q_gain, m_attn_k_gain, m_attn_w_o, m_dn_w_in, m_dn_conv, m_dn_a_log, m_dn_dt_bias, m_dn_o_gain, m_dn_w_o, m_mlp_norm, m_w_up, m_w_down, m_ple_norm, m_w_ple, m_w_ple_gate, v_mix_norm, v_attn_w_qkv, v_attn_q_gain, v_attn_k_gain, v_attn_w_o, v_dn_w_in, v_dn_conv, v_dn_a_log, v_dn_dt_bias, v_dn_o_gain, v_dn_w_o, v_mlp_norm, v_w_up, v_w_down, v_ple_norm, v_w_ple, v_w_ple_gate):
    given = dict(locals())
    w = {n: given[n] for n in WEIGHTS}
    m = {n: given["m_" + n] for n in WEIGHTS}
    v = {n: given["v_" + n] for n in WEIGHTS}
    kinds = ("grad", "delta", "new_m", "new_v")
    axes = {n: _view_axis(w[n], axis) for n, axis in SHARDED if n != "dn_conv"}
    chip = 2 * lax.axis_index("x") + lax.axis_index("y")
    core = lax.axis_index("c")
    shards = {n: _view(w[n]).astype(BF16) for n in axes}

    def whole(n, slots):
        return jnp.concatenate([jnp.where(chip == q, shards[n], slots[q]) for q in range(N_CHIPS)], axis=axes[n])

    def stacks_of(grads_of):
        return [jnp.stack(jnp.split(g, N_CHIPS, axis=axes[n])) for n, g in grads_of.items()]

    def chip_sums_of(names, stacks, theirs):
        mine = [lax.dynamic_slice_in_dim(s, core * (s.shape[1] // 2), s.shape[1] // 2, axis=1) for s in stacks]
        return {n: _elementwise(f"add_core_{n}", lambda a, b: a.astype(F32) + b.astype(F32), [a, b], [BF16], 512)
                for n, a, b in zip(names, mine, theirs)}

    class Hooks:
        first_cargo = _gather_exchange([shards[n] for n in ATTN_MATRICES])
        fwd_cargo = [_gather_exchange([shards[n] for n in group]) for group in CARGO_GROUPS]
        chip_sums = {}
        early = None

        def first_weights(self, results):
            return {n: whole(n, slots) for n, slots in zip(ATTN_MATRICES, results)}

        def weights_from(self, results):
            full = {n: whole(n, slots) for group, res in zip(CARGO_GROUPS, results) for n, slots in zip(group, res)}
            w_in, n_ab = full["dn_w_in"], 2 * DN_HEADS
            full["dn_w_in"] = jnp.concatenate([w_in[:, DN_QKV + n_ab:], w_in[:, :DN_QKV + n_ab],
                                               jnp.zeros((D_MODEL, DN_AB_PAD - n_ab), BF16)], axis=1)
            return full

        def split_cargo(self, early_grads):
            self.early = (list(early_grads), stacks_of(early_grads))
            return _other_half_exchange(self.early[1])

        def bwd_cargo(self, results):
            self.chip_sums.update(chip_sums_of(*self.early, results))
            return [_scatter_exchange([self.chip_sums[n] for n in group]) for group in CARGO_GROUPS]

        def last_cargo(self, attn_grads):
            stacks = stacks_of(attn_grads)
            theirs = _run_exchange("split_core_grads_attn", _other_half_exchange(stacks))
            self.chip_sums.update(chip_sums_of(list(attn_grads), stacks, theirs))
            return _scatter_exchange([self.chip_sums[n] for n in attn_grads])

    hooks = Hooks()
    big = {}
    conv_block = jnp.pad(w["dn_conv"].reshape(-1), (0, SMALL_ROWS * D_MODEL - w["dn_conv"].size))
    conv_all = _gather_from_all("gather_conv", conv_block.reshape(SMALL_ROWS, D_MODEL))
    conv_all = conv_all.reshape(N_CHIPS, 2, -1)[:, 0, :w["dn_conv"].size]
    conv_full = jnp.concatenate([conv_all[q].reshape(CONV_WIDTH, -1) for q in range(N_CHIPS)], axis=1)
    small = {n: w[n] for n in REPLICATED}
    small["dn_conv"] = conv_full

    sq, grad_x, grads, brought = _local_step(x[0], p[:, 0], positions[0], loss_target[0], small, big, hooks)
    loss = lax.psum(0.5 * sq[0, 0] / D_MODEL, ("x", "y", "c"))
    out = {}

    landed = {n: r for group, res in zip(CARGO_GROUPS + (ATTN_MATRICES,), brought) for n, r in zip(group, res)}
    half_sums = []
    for n in axes:
        o = lax.dynamic_index_in_dim(hooks.chip_sums[n], chip, axis=0, keepdims=False)
        per = math.prod(o.shape[:-1])
        r = landed[n]
        half_sums.append(_elementwise(
            f"add_chips_{n}", lambda a, b, c, d: ((a.astype(F32) + b.astype(F32)) + c.astype(F32)) + d.astype(F32),
            [o, (r, 0), (r, per), (r, 2 * per)], [F32], 512))
    other_halves = _swap_with_sibling("join_core_sums", half_sums)
    for n, a, b in zip(axes, half_sums, other_halves):
        g = jnp.where(core == 0, jnp.concatenate([a, b], axis=0), jnp.concatenate([b, a], axis=0))
        shp = w[n].shape
        res = _elementwise(f"adamw_{n}", lambda g, w_, m_, v_: (g,) + _adamw(w_, g, m_, v_),
                           [g.reshape(shp), w[n], m[n], v[n]], [F32] * 4, 512)
        for kind, arr in zip(kinds, res):
            out[kind + "_" + n] = arr.reshape(shp)

    slots = _gather_from_all("gather_small_grads", _pack_small(grads, grads["dn_conv"]))

    def small_body(s_ref, w_ref, m_ref, v_ref, sum_out, g_out, d_out, m_out, v_out):
        total = s_ref[0]
        for d in range(1, N_DEV):
            total = total + s_ref[d]
        sum_out[...] = total
        g = total[:SMALL_ROWS]
        for o, r in zip((g_out, d_out, m_out, v_out), (g,) + _adamw(w_ref[...], g, m_ref[...], v_ref[...])):
            o[...] = r

    res = pl.pallas_call(small_body, name="adamw_replicated",
                         out_shape=[jax.ShapeDtypeStruct((SMALL_GRAD_ROWS, D_MODEL), F32)]
                         + [jax.ShapeDtypeStruct((SMALL_ROWS, D_MODEL), F32)] * 4)(
        slots, _pack_small(w), _pack_small(m), _pack_small(v))
    for kind, block in zip(kinds, res[1:]):
        for n, arr in _unpack_small(block).items():
            out[kind + "_" + n] = arr
    conv_sum = res[0][SMALL_ROWS:SMALL_ROWS + CONV_ROWS].reshape(CONV_WIDTH, DN_QKV)
    cols = DN_QKV // N_CHIPS
    chip = 2 * lax.axis_index("x") + lax.axis_index("y")
    conv_mine = lax.dynamic_slice_in_dim(conv_sum, chip * cols, cols, axis=1)
    res = _elementwise("adamw_dn_conv", lambda g, w_, m_, v_: (g,) + _adamw(w_, g, m_, v_),
                       [conv_mine, w["dn_conv"][0], m["dn_conv"][0], v["dn_conv"][0]], [F32] * 4, CONV_WIDTH)
    for kind, arr in zip(kinds, res):
        out[kind + "_dn_conv"] = arr[None]

    return (loss, grad_x[None],
            *[out["grad_" + n] for n in WEIGHTS], *[out["delta_" + n] for n in WEIGHTS],
            *[out["new_m_" + n] for n in WEIGHTS], *[out["new_v_" + n] for n in WEIGHTS])
```
